```python
import jax
import jax.numpy as jnp
from jax import lax
import numpy as np

D_MODEL = 1024
BATCH = 8
SEQ = 2048
DEPTH = 2

GRID_W = 64
CTX_LEN = 256
N_MOD = 6
RMS_EPS = 1e-6
NEG_INF = -1e30
NA_HEADS = 8
NA_HEAD_DIM = 64
NA_WIDTH = NA_HEADS * NA_HEAD_DIM
NA_WIN_R = 8
NA_WIN_C = 16
SC_WIDTH = 512
CONV_W = 3
GLA_HEADS = 4
GLA_KEY_WIDTH = D_MODEL // 2
GLA_VAL_WIDTH = D_MODEL
GLA_DK = GLA_KEY_WIDTH // GLA_HEADS
GLA_DV = GLA_VAL_WIDTH // GLA_HEADS
GLA_GATE_RANK = 16
GLA_GATE_TAU = 16.0
GLA_CHUNK = 64
N_BRANCH = 3
N_EXPERTS = 64
N_EXPERT_GROUPS = 8
TOPK_GROUPS = 4
TOP_K = 8
EXPERT_FF = 256
ROUTED_SCALE = 2.5
MOE_BLOCK = 256
KV_COLS = (NA_WIDTH, NA_WIDTH, GLA_KEY_WIDTH, GLA_VAL_WIDTH, 2 * GLA_GATE_RANK)
LAT_COLS = (NA_WIDTH, SC_WIDTH, SC_WIDTH, SC_WIDTH, GLA_KEY_WIDTH, GLA_VAL_WIDTH, N_BRANCH * D_MODEL)
IN_COLS = KV_COLS + LAT_COLS
N_KV = sum(KV_COLS)
N_IN = sum(IN_COLS)
COL_NAMES = ('k_na', 'v_na', 'k_gla', 'v_gla', 'gate_gla', 'q_na', 'b_sc', 'c_sc', 'x_sc', 'q_gla', 'r_gla', 'merge')

kernel_name = 'hybrid_natten_conv_gla_moe_prefix_dit'


def _rmsnorm(x, g):
    x32 = x.astype(jnp.float32)
    y = x32 * lax.rsqrt(jnp.mean(x32 * x32, axis=-1, keepdims=True) + RMS_EPS)
    return (y * g.astype(jnp.float32)).astype(x.dtype)


def _modulate(x, g, shift, scale):
    return _rmsnorm(x, g) * (1 + scale) + shift


def _heads(t, n):
    return t.reshape(t.shape[:-1] + (n, t.shape[-1] // n))


def _split_cols(z, sizes):
    return dict(zip(COL_NAMES, jnp.split(z, np.cumsum(sizes)[:-1].tolist(), axis=-1)))


def _attend_dense(q, k, v):
    s = jnp.einsum('bqhd,bkhd->bhqk', q, k).astype(jnp.float32) * (q.shape[-1] ** -0.5)
    p = jax.nn.softmax(s, axis=-1).astype(q.dtype)
    o = jnp.einsum('bhqk,bkhd->bqhd', p, v)
    return o.reshape(o.shape[:2] + (-1,))


def _na_latent(q, k, v, k_ctx, v_ctx, rpb):
    b, s, h, d = q.shape
    rows = s // GRID_W
    kr = min(NA_WIN_R, rows)
    row_start = np.clip(np.arange(rows) - kr // 2, 0, rows - kr)
    row_idx = row_start[:, None] + np.arange(kr)
    col = np.arange(GRID_W)
    col_start = np.clip(col - NA_WIN_C // 2, 0, GRID_W - NA_WIN_C)
    col_ok = (col[None, :] >= col_start[:, None]) & (col[None, :] < col_start[:, None] + NA_WIN_C)
    d_row = row_idx - np.arange(rows)[:, None] + NA_WIN_R - 1
    d_col = np.clip(col[None, :] - col[:, None], -(NA_WIN_C - 1), NA_WIN_C - 1) + NA_WIN_C - 1
    bias = rpb.astype(jnp.float32)[:, d_row[:, None, :, None], d_col[None, :, None, :]]
    bias = jnp.where(col_ok[None, None, :, None, :], bias, NEG_INF)
    scale = d ** -0.5
    qg = q.reshape(b, rows, GRID_W, h, d)
    kg = k.reshape(b, rows, GRID_W, h, d)[:, row_idx]
    vg = v.reshape(b, rows, GRID_W, h, d)[:, row_idx]
    s_win = jnp.einsum('brqhd,brjkhd->bhrqjk', qg, kg).astype(jnp.float32) * scale + bias[None]
    s_ctx = jnp.einsum('brqhd,bchd->bhrqc', qg, k_ctx).astype(jnp.float32) * scale
    n_win = kr * GRID_W
    scores = jnp.concatenate([s_win.reshape(b, h, rows, GRID_W, n_win), s_ctx], axis=-1)
    p = jax.nn.softmax(scores, axis=-1).astype(q.dtype)
    p_win = p[..., :n_win].reshape(b, h, rows, GRID_W, kr, GRID_W)
    p_ctx = p[..., n_win:]
    o = jnp.einsum('bhrqjk,brjkhd->brqhd', p_win, vg) + jnp.einsum('bhrqc,bchd->brqhd', p_ctx, v_ctx)
    return o.reshape(b, s, h * d)


def _short_conv(b_gate, c_gate, x_in, w):
    u = c_gate * x_in
    up = jnp.pad(u, ((0, 0), (1, 1), (0, 0)))
    return b_gate * (up[:, :-2] * w[0] + up[:, 1:-1] * w[1] + up[:, 2:] * w[2])


def _gla_decay(gate_lr, w2, b2):
    outs = []
    for dr, lr in enumerate(jnp.split(gate_lr, 2, axis=-1)):
        logit = (lr @ w2[dr] + b2[dr]).astype(jnp.float32)
        outs.append(_heads(jax.nn.log_sigmoid(logit) / GLA_GATE_TAU, GLA_HEADS))
    return outs[0], outs[1]


def _gla_scan(q, k, v, g, s0, emit):
    b, l, h, _ = k.shape
    dv = v.shape[-1]
    n = l // GLA_CHUNK

    def chunks(t):
        return t.reshape(b, n, GLA_CHUNK, h, t.shape[-1]).transpose(1, 0, 3, 2, 4).astype(jnp.float32)

    lower = np.tril(np.ones((GLA_CHUNK, GLA_CHUNK), dtype=bool))

    def step(state, inp):
        kc, vc, gc = inp[0], inp[1], inp[2]
        cum = jnp.cumsum(gc, axis=2)
        last = cum[:, :, -1:, :]
        new_state = (jnp.exp(last[:, :, 0, :])[..., None] * state
                     + jnp.einsum('bhsd,bhsv->bhdv', kc * jnp.exp(last - cum), vc))
        if not emit:
            return new_state, None
        qc = inp[3]
        inter = jnp.einsum('bhtd,bhdv->bhtv', qc * jnp.exp(cum), state)
        diff = cum[:, :, :, None, :] - cum[:, :, None, :, :]
        decay = jnp.exp(jnp.where(lower[:, :, None], diff, -jnp.inf))
        attn = jnp.einsum('bhtd,bhsd,bhtsd->bhts', qc, kc, decay)
        intra = jnp.einsum('bhts,bhsv->bhtv', attn, vc)
        return new_state, inter + intra

    xs = (chunks(k), chunks(v), chunks(g)) + ((chunks(q),) if emit else ())
    state, o = lax.scan(step, s0, xs)
    if emit:
        o = o.transpose(1, 0, 3, 2, 4).reshape(b, l, h, dv)
    return o, state


def _gla_bidir(q, k, v, g_f, g_b, s_f, s_b, emit):
    o_f, st_f = _gla_scan(q, k, v, g_f, s_f, emit)
    rev = lambda t: None if t is None else jnp.flip(t, axis=1)
    o_b, st_b = _gla_scan(rev(q), rev(k), rev(v), rev(g_b), s_b, emit)
    o = o_f + jnp.flip(o_b, axis=1) if emit else None
    return o, st_f, st_b


def _gla_out(o, r, g):
    on = o * lax.rsqrt(jnp.mean(o * o, axis=-1, keepdims=True) + RMS_EPS) * g.astype(jnp.float32)
    y = on * jax.nn.silu(_heads(r, GLA_HEADS).astype(jnp.float32))
    return y.reshape(r.shape).astype(r.dtype)


def _merge(o_na, o_sc, o_gla, gates, w_na, w_sc, w_gla, w_out):
    g_na, g_sc, g_gla = jnp.split(jax.nn.sigmoid(gates), N_BRANCH, axis=-1)
    y = g_na * (o_na @ w_na) + g_sc * (o_sc @ w_sc) + g_gla * (o_gla @ w_gla)
    return y @ w_out


def _moe(h, w_router, b_router, w_gate, w_up, w_down, ws_gate, ws_up, ws_down):
    t, d = h.shape
    scores = jax.nn.sigmoid((h @ w_router).astype(jnp.float32))
    sel = scores + b_router.astype(jnp.float32)
    group_score = lax.top_k(sel.reshape(t, N_EXPERT_GROUPS, N_EXPERTS // N_EXPERT_GROUPS), 2)[0].sum(-1)
    _, top_groups = lax.top_k(group_score, TOPK_GROUPS)
    group_mask = (top_groups[:, :, None] == jnp.arange(N_EXPERT_GROUPS)[None, None, :]).any(axis=1)
    sel = jnp.where(jnp.repeat(group_mask, N_EXPERTS // N_EXPERT_GROUPS, axis=1), sel, -jnp.inf)
    _, eidx = lax.top_k(sel, TOP_K)
    gate_w = jnp.take_along_axis(scores, eidx, axis=1)
    gate_w = gate_w / gate_w.sum(-1, keepdims=True) * ROUTED_SCALE
    a = t * TOP_K
    e_flat = eidx.reshape(-1).astype(jnp.int32)
    tok_flat = jnp.repeat(jnp.arange(t, dtype=jnp.int32), TOP_K)
    w_flat = gate_w.reshape(-1)
    order = jnp.argsort(e_flat)
    e_sorted = e_flat[order]
    counts = jnp.zeros((N_EXPERTS,), jnp.int32).at[e_flat].add(1)
    padded = (counts + MOE_BLOCK - 1) // MOE_BLOCK * MOE_BLOCK
    pad_end = jnp.cumsum(padded)
    pad_start = pad_end - padded
    grp_start = jnp.cumsum(counts) - counts
    dest = pad_start[e_sorted] + jnp.arange(a, dtype=jnp.int32) - grp_start[e_sorted]
    n_blocks = -(-(a + N_EXPERTS * (MOE_BLOCK - 1)) // MOE_BLOCK)
    slots = n_blocks * MOE_BLOCK
    slot_tok = jnp.full((slots,), t, jnp.int32).at[dest].set(tok_flat[order])
    slot_w = jnp.zeros((slots,), h.dtype).at[dest].set(w_flat[order].astype(h.dtype))
    blk_e = jnp.minimum(jnp.searchsorted(pad_end, jnp.arange(n_blocks, dtype=jnp.int32) * MOE_BLOCK, side='right'), N_EXPERTS - 1)
    h_pad = jnp.concatenate([h, jnp.zeros((1, d), h.dtype)], axis=0)

    def step(acc, blk):
        tok, w, e = blk
        xb = h_pad[tok]
        hid = jax.nn.silu(xb @ w_gate[e]) * (xb @ w_up[e])
        return acc.at[tok].add((hid @ w_down[e]) * w[:, None]), None

    routed, _ = lax.scan(step, jnp.zeros((t + 1, d), h.dtype),
                         (slot_tok.reshape(n_blocks, MOE_BLOCK), slot_w.reshape(n_blocks, MOE_BLOCK), blk_e))
    shared = (jax.nn.silu(h @ ws_gate) * (h @ ws_up)) @ ws_down
    return shared + routed[:t]


def _layer(x, ctx_s, c, c_ctx, w_mod, b_mod, g_norm1, g_norm2, w_in, na_rpb, w_branch_na, conv_w, w_branch_sc,
           gla_gate_w, gla_gate_b, gla_norm_g, w_branch_gla, w_out, w_router, b_router, w_exp_gate, w_exp_up,
           w_exp_down, w_sh_gate, w_sh_up, w_sh_down, ctx_out):
    sh1, sc1, gt1, sh2, sc2, gt2 = jnp.split((jax.nn.silu(c) @ w_mod + b_mod)[:, None, :], N_MOD, axis=-1)
    csh1, csc1, cgt1, csh2, csc2, cgt2 = jnp.split(jax.nn.silu(c_ctx) @ w_mod + b_mod, N_MOD, axis=-1)

    lat = _split_cols(_modulate(x, g_norm1, sh1, sc1) @ w_in, IN_COLS)
    hc = _modulate(ctx_s, g_norm1, csh1, csc1)
    cx = _split_cols(hc @ w_in, IN_COLS) if ctx_out else _split_cols(hc @ w_in[:, :N_KV], KV_COLS)

    k_na_c = _heads(cx['k_na'], NA_HEADS)
    v_na_c = _heads(cx['v_na'], NA_HEADS)
    o_na = _na_latent(_heads(lat['q_na'], NA_HEADS), _heads(lat['k_na'], NA_HEADS), _heads(lat['v_na'], NA_HEADS),
                      k_na_c, v_na_c, na_rpb)
    o_sc = _short_conv(lat['b_sc'], lat['c_sc'], lat['x_sc'], conv_w)
    q_scale = GLA_DK ** -0.5
    s0 = jnp.zeros((x.shape[0], GLA_HEADS, GLA_DK, GLA_DV), jnp.float32)
    gcf, gcb = _gla_decay(cx['gate_gla'], gla_gate_w, gla_gate_b)
    q_gla_c = _heads(cx['q_gla'], GLA_HEADS) * q_scale if ctx_out else None
    o_gla_c, st_f, st_b = _gla_bidir(q_gla_c, _heads(cx['k_gla'], GLA_HEADS), _heads(cx['v_gla'], GLA_HEADS),
                                     gcf, gcb, s0, s0, ctx_out)
    glf, glb = _gla_decay(lat['gate_gla'], gla_gate_w, gla_gate_b)
    o_gla, _, _ = _gla_bidir(_heads(lat['q_gla'], GLA_HEADS) * q_scale, _heads(lat['k_gla'], GLA_HEADS),
                             _heads(lat['v_gla'], GLA_HEADS), glf, glb, st_f, st_b, True)
    o_gla = _gla_out(o_gla, lat['r_gla'], gla_norm_g)
    x = x + gt1 * _merge(o_na, o_sc, o_gla, lat['merge'], w_branch_na, w_branch_sc, w_branch_gla, w_out)
    if ctx_out:
        o_na_c = _attend_dense(_heads(cx['q_na'], NA_HEADS), k_na_c, v_na_c)
        o_sc_c = _short_conv(cx['b_sc'], cx['c_sc'], cx['x_sc'], conv_w)
        o_gla_c = _gla_out(o_gla_c, cx['r_gla'], gla_norm_g)
        ctx_s = ctx_s + cgt1 * _merge(o_na_c, o_sc_c, o_gla_c, cx['merge'], w_branch_na, w_branch_sc, w_branch_gla, w_out)

    d = x.shape[-1]
    tokens = _modulate(x, g_norm2, sh2, sc2).reshape(-1, d)
    if ctx_out:
        tokens = jnp.concatenate([tokens, _modulate(ctx_s, g_norm2, csh2, csc2).reshape(-1, d)], axis=0)
    y = _moe(tokens, w_router, b_router, w_exp_gate, w_exp_up, w_exp_down, w_sh_gate, w_sh_up, w_sh_down)
    n_lat = x.shape[0] * x.shape[1]
    x = x + gt2 * y[:n_lat].reshape(x.shape)
    if ctx_out:
        ctx_s = ctx_s + cgt2 * y[n_lat:].reshape(ctx_s.shape)
    return x, ctx_s


def setup_inputs(seed: int = 0) -> dict:
    key = jax.random.key(seed)
    ks = jax.random.split(key, 28)
    nrm = lambda k, shape, s: jax.random.normal(k, shape, jnp.float32) * s
    L = DEPTH
    return {
        'x': nrm(ks[0], (BATCH, SEQ, D_MODEL), 1.0),
        'c': nrm(ks[1], (BATCH, D_MODEL), 1.0),
        'ctx': nrm(ks[2], (BATCH, CTX_LEN, D_MODEL), 1.0),
        'c_ctx': nrm(ks[3], (D_MODEL,), 1.0),
        'w_mod': nrm(ks[4], (L, D_MODEL, N_MOD * D_MODEL), 0.5 * D_MODEL ** -0.5),
        'b_mod': nrm(ks[5], (L, N_MOD * D_MODEL), 0.02),
        'g_norm1': 1.0 + nrm(ks[6], (L, D_MODEL), 0.02),
        'g_norm2': 1.0 + nrm(ks[7], (L, D_MODEL), 0.02),
        'w_in': nrm(ks[8], (L, D_MODEL, N_IN), D_MODEL ** -0.5),
        'na_rpb': nrm(ks[9], (L, NA_HEADS, 2 * NA_WIN_R - 1, 2 * NA_WIN_C - 1), 0.1),
        'w_branch_na': nrm(ks[10], (L, NA_WIDTH, D_MODEL), NA_WIDTH ** -0.5),
        'conv_w': nrm(ks[11], (L, CONV_W, SC_WIDTH), CONV_W ** -0.5),
        'w_branch_sc': nrm(ks[12], (L, SC_WIDTH, D_MODEL), SC_WIDTH ** -0.5),
        'gla_gate_w': nrm(ks[13], (L, 2, GLA_GATE_RANK, GLA_KEY_WIDTH), GLA_GATE_RANK ** -0.5),
        'gla_gate_b': nrm(ks[14], (L, 2, GLA_KEY_WIDTH), 0.1),
        'gla_norm_g': 1.0 + nrm(ks[15], (L, GLA_DV), 0.02),
        'w_branch_gla': nrm(ks[16], (L, GLA_VAL_WIDTH, D_MODEL), GLA_VAL_WIDTH ** -0.5),
        'w_out': nrm(ks[17], (L, D_MODEL, D_MODEL), D_MODEL ** -0.5),
        'w_router': nrm(ks[18], (L, D_MODEL, N_EXPERTS), D_MODEL ** -0.5),
        'b_router': nrm(ks[19], (L, N_EXPERTS), 0.01),
        'w_exp_gate': nrm(ks[20], (L, N_EXPERTS, D_MODEL, EXPERT_FF), D_MODEL ** -0.5),
        'w_exp_up': nrm(ks[21], (L, N_EXPERTS, D_MODEL, EXPERT_FF), D_MODEL ** -0.5),
        'w_exp_down': nrm(ks[22], (L, N_EXPERTS, EXPERT_FF, D_MODEL), EXPERT_FF ** -0.5),
        'w_sh_gate': nrm(ks[23], (L, D_MODEL, EXPERT_FF), D_MODEL ** -0.5),
        'w_sh_up': nrm(ks[24], (L, D_MODEL, EXPERT_FF), D_MODEL ** -0.5),
        'w_sh_down': nrm(ks[25], (L, EXPERT_FF, D_MODEL), EXPERT_FF ** -0.5),
        'g_final': 1.0 + nrm(ks[26], (D_MODEL,), 0.02),
    }


def reference(x, c, ctx, c_ctx, w_mod, b_mod, g_norm1, g_norm2, w_in, na_rpb, w_branch_na, conv_w, w_branch_sc,
              gla_gate_w, gla_gate_b, gla_norm_g, w_branch_gla, w_out, w_router, b_router, w_exp_gate, w_exp_up,
              w_exp_down, w_sh_gate, w_sh_up, w_sh_down, g_final):
    ctx_s = ctx
    for i in range(DEPTH):
        x, ctx_s = _layer(x, ctx_s, c, c_ctx, w_mod[i], b_mod[i], g_norm1[i], g_norm2[i], w_in[i], na_rpb[i],
                          w_branch_na[i], conv_w[i], w_branch_sc[i], gla_gate_w[i], gla_gate_b[i], gla_norm_g[i],
                          w_branch_gla[i], w_out[i], w_router[i], b_router[i], w_exp_gate[i], w_exp_up[i],
                          w_exp_down[i], w_sh_gate[i], w_sh_up[i], w_sh_down[i], i < DEPTH - 1)
    return _rmsnorm(x, g_final)
```

```python
import functools

import numpy as np
import jax
import jax.numpy as jnp
from jax import lax
from jax.experimental import pallas as pl
from jax.experimental.pallas import tpu as pltpu

F32 = jnp.float32
BF16 = jnp.bfloat16

D_MODEL = 1024
N_MOD = 6
RMS_EPS = 1e-6
NEG_INF = -1e30
GRID_W = 64
NA_HEADS = 8
NA_HEAD_DIM = 64
NA_WIDTH = NA_HEADS * NA_HEAD_DIM
NA_WIN_R = 8
NA_WIN_C = 16
SC_WIDTH = 512
GLA_HEADS = 4
GLA_KEY_WIDTH = 512
GLA_VAL_WIDTH = 1024
GLA_DK = GLA_KEY_WIDTH // GLA_HEADS
GLA_DV = GLA_VAL_WIDTH // GLA_HEADS
GLA_GATE_RANK = 16
GLA_GATE_TAU = 16.0
N_EXPERTS = 64
N_EXPERT_GROUPS = 8
GROUP_SIZE = N_EXPERTS // N_EXPERT_GROUPS
TOPK_GROUPS = 4
TOP_K = 8
EXPERT_FF = 256
ROUTED_SCALE = 2.5
MOE_BLOCK = 256

LANE = 128
GLA_C = 128
GLA_LEVELS = tuple(GLA_C >> (i + 1) for i in range(GLA_C.bit_length() - 1))
VMEM_LIMIT = 48 * 1024 * 1024

OFF_V_GLA = 0
OFF_K_NA = 1024
OFF_V_NA = 1536
OFF_K_GLA = 2048
N_KV_MAIN = 2560
OFF_Q_NA = 2560
OFF_B_SC = 3072
OFF_C_SC = 3584
OFF_X_SC = 4096
OFF_Q_GLA = 4608
OFF_R_GLA = 5120
OFF_MERGE = 6144
N_MAIN = 9216


def _cparams(sem, vmem=VMEM_LIMIT):
    return pltpu.CompilerParams(dimension_semantics=sem, vmem_limit_bytes=vmem)


def _dot(a, b):
    return jnp.dot(a, b, preferred_element_type=F32)


def _dot_nt(a, b):
    return lax.dot_general(a, b, (((1,), (1,)), ((), ())), preferred_element_type=F32)


def _dot_tn(a, b):
    return lax.dot_general(a, b, (((0,), (0,)), ((), ())), preferred_element_type=F32)


def _sigmoid(x):
    return 1.0 / (1.0 + jnp.exp(-x))


def _mod_kernel(a_ref, w_ref, b_ref, o_ref):
    a = a_ref[...]
    a = a * _sigmoid(a)
    o_ref[...] = _dot(a.astype(BF16), w_ref[...].astype(BF16)) + b_ref[...]


def _mod_vectors(c, c_ctx, w_mod, b_mod):
    b = c.shape[0]
    rows = -(-(b + 1) // 8) * 8
    a = jnp.concatenate([c, c_ctx[None], jnp.zeros((rows - b - 1, D_MODEL), F32)], axis=0)
    n = N_MOD * D_MODEL
    tn = 1536
    out = pl.pallas_call(
        _mod_kernel,
        grid=(n // tn,),
        in_specs=[pl.BlockSpec((rows, D_MODEL), lambda j: (0, 0)),
                  pl.BlockSpec((D_MODEL, tn), lambda j: (0, j)),
                  pl.BlockSpec((1, tn), lambda j: (0, j))],
        out_specs=pl.BlockSpec((rows, tn), lambda j: (0, j)),
        out_shape=jax.ShapeDtypeStruct((rows, n), F32),
        compiler_params=_cparams(("parallel",)),
        name="mod_vectors",
    )(a, w_mod, b_mod[None])
    lat = out[:b].reshape(b, N_MOD, 1, D_MODEL)
    ctx = out[b].reshape(N_MOD, 1, 1, D_MODEL)
    return [lat[:, i] for i in range(N_MOD)], [ctx[i] for i in range(N_MOD)]


def _proj_kernel(x_ref, g_ref, sh_ref, sc_ref, w_ref, wg_ref, o_ref, og_ref, h_ref):
    @pl.when(pl.program_id(2) == 0)
    def _():
        x = x_ref[0]
        ms = jnp.mean(x * x, axis=-1, keepdims=True)
        h = x * lax.rsqrt(ms + RMS_EPS) * g_ref[...] * (1.0 + sc_ref[0]) + sh_ref[0]
        hb = h.astype(BF16)
        h_ref[...] = hb
        og_ref[0] = _dot(hb, wg_ref[...])

    o_ref[0] = _dot(h_ref[...], w_ref[...]).astype(o_ref.dtype)


def _proj_in(x, g, shift, scale, w_main, w_gate, tm, tn):
    b, s, d = x.shape
    n = w_main.shape[1]
    per_batch = shift.shape[0] == b
    mod_map = (lambda bi, i, j: (bi, 0, 0)) if per_batch else (lambda bi, i, j: (0, 0, 0))
    return pl.pallas_call(
        _proj_kernel,
        grid=(b, s // tm, n // tn),
        in_specs=[pl.BlockSpec((1, tm, d), lambda bi, i, j: (bi, i, 0)),
                  pl.BlockSpec((1, d), lambda bi, i, j: (0, 0)),
                  pl.BlockSpec((1, 1, d), mod_map),
                  pl.BlockSpec((1, 1, d), mod_map),
                  pl.BlockSpec((d, tn), lambda bi, i, j: (0, j)),
                  pl.BlockSpec((d, LANE), lambda bi, i, j: (0, 0))],
        out_specs=[pl.BlockSpec((1, tm, tn), lambda bi, i, j: (bi, i, j)),
                   pl.BlockSpec((1, tm, LANE), lambda bi, i, j: (bi, i, 0))],
        out_shape=[jax.ShapeDtypeStruct((b, s, n), BF16),
                   jax.ShapeDtypeStruct((b, s, LANE), F32)],
        scratch_shapes=[pltpu.VMEM((tm, d), BF16)],
        compiler_params=_cparams(("parallel", "parallel", "arbitrary")),
        name="proj_in",
    )(x, g[None], shift, scale, w_main, w_gate)


def _softmax_av(q, keys, vals, biases):
    scores = []
    for kk, bb in zip(keys, biases):
        s = _dot_nt(q, kk)
        scores.append(s if bb is None else s + bb)
    m = scores[0].max(axis=-1, keepdims=True)
    for s in scores[1:]:
        m = jnp.maximum(m, s.max(axis=-1, keepdims=True))
    num = None
    den = None
    for s, vv in zip(scores, vals):
        e = jnp.exp(s - m)
        dsum = e.sum(axis=-1, keepdims=True)
        o = _dot(e.astype(BF16), vv)
        num = o if num is None else num + o
        den = dsum if den is None else den + dsum
    return num / den


def _na_kernel(q_ref, k_ref, v_ref, kc_ref, vc_ref, bias_ref, o_ref, *, rows, kr):
    r = pl.program_id(1)
    row_start = jnp.clip(r - kr // 2, 0, rows - kr)
    start = pl.multiple_of(row_start * GRID_W, GRID_W)
    n_win = kr * GRID_W
    q = q_ref[0]
    kw = k_ref[0, pl.ds(start, n_win), :]
    vw = v_ref[0, pl.ds(start, n_win), :]
    kc = kc_ref[0]
    vc = vc_ref[0]
    outs = []
    for h in range(NA_HEADS):
        sl = slice(h * NA_HEAD_DIM, (h + 1) * NA_HEAD_DIM)
        outs.append(_softmax_av(q[:, sl], [kw[:, sl], kc[:, sl]], [vw[:, sl], vc[:, sl]],
                                [bias_ref[0, h], None]))
    o_ref[0] = jnp.concatenate(outs, axis=-1).astype(o_ref.dtype)


def _na_bias_table(rpb, rows, kr):
    o = np.arange(kr)[:, None]
    j = np.arange(kr)[None, :]
    d_row = j - o + NA_WIN_R - 1
    col = np.arange(GRID_W)
    col_start = np.clip(col - NA_WIN_C // 2, 0, GRID_W - NA_WIN_C)
    col_ok = (col[None, :] >= col_start[:, None]) & (col[None, :] < col_start[:, None] + NA_WIN_C)
    d_col = np.clip(col[None, :] - col[:, None], -(NA_WIN_C - 1), NA_WIN_C - 1) + NA_WIN_C - 1
    bias = rpb.astype(F32)[:, d_row[:, None, :, None], d_col[None, :, None, :]]
    bias = jnp.where(col_ok[None, None, :, None, :], bias, NEG_INF)
    return bias.transpose(1, 0, 2, 3, 4).reshape(kr, NA_HEADS, GRID_W, kr * GRID_W)


def _na_latent(main, main_ctx, rpb):
    b, s, _ = main.shape
    sc = main_ctx.shape[1]
    rows = s // GRID_W
    kr = min(NA_WIN_R, rows)
    bias = _na_bias_table(rpb, rows, kr)
    w = NA_WIDTH

    def bias_map(bi, r):
        return (r - jnp.clip(r - kr // 2, 0, rows - kr), 0, 0, 0)

    return pl.pallas_call(
        functools.partial(_na_kernel, rows=rows, kr=kr),
        grid=(b, rows),
        in_specs=[pl.BlockSpec((1, GRID_W, w), lambda bi, r: (bi, r, OFF_Q_NA // w)),
                  pl.BlockSpec((1, s, w), lambda bi, r: (bi, 0, OFF_K_NA // w)),
                  pl.BlockSpec((1, s, w), lambda bi, r: (bi, 0, OFF_V_NA // w)),
                  pl.BlockSpec((1, sc, w), lambda bi, r: (bi, 0, OFF_K_NA // w)),
                  pl.BlockSpec((1, sc, w), lambda bi, r: (bi, 0, OFF_V_NA // w)),
                  pl.BlockSpec((1, NA_HEADS, GRID_W, kr * GRID_W), bias_map)],
        out_specs=pl.BlockSpec((1, GRID_W, w), lambda bi, r: (bi, r, 0)),
        out_shape=jax.ShapeDtypeStruct((b, s, w), BF16),
        compiler_params=_cparams(("parallel", "arbitrary")),
        name="na_latent",
    )(main, main, main, main_ctx, main_ctx, bias)


def _dense_attn_kernel(q_ref, k_ref, v_ref, o_ref):
    q = q_ref[0]
    k = k_ref[0]
    v = v_ref[0]
    outs = []
    for h in range(NA_HEADS):
        sl = slice(h * NA_HEAD_DIM, (h + 1) * NA_HEAD_DIM)
        outs.append(_softmax_av(q[:, sl], [k[:, sl]], [v[:, sl]], [None]))
    o_ref[0] = jnp.concatenate(outs, axis=-1).astype(o_ref.dtype)


def _dense_attn(main_ctx):
    b, sc, _ = main_ctx.shape
    w = NA_WIDTH
    return pl.pallas_call(
        _dense_attn_kernel,
        grid=(b,),
        in_specs=[pl.BlockSpec((1, sc, w), lambda bi: (bi, 0, OFF_Q_NA // w)),
                  pl.BlockSpec((1, sc, w), lambda bi: (bi, 0, OFF_K_NA // w)),
                  pl.BlockSpec((1, sc, w), lambda bi: (bi, 0, OFF_V_NA // w))],
        out_specs=pl.BlockSpec((1, sc, w), lambda bi: (bi, 0, 0)),
        out_shape=jax.ShapeDtypeStruct((b, sc, w), BF16),
        compiler_params=_cparams(("parallel",)),
        name="ctx_attn",
    )(main_ctx, main_ctx, main_ctx)


def _conv_kernel(b_ref, c_ref, x_ref, w_ref, o_ref):
    u = c_ref[0].astype(F32) * x_ref[0].astype(F32)
    s = u.shape[0]
    t = lax.broadcasted_iota(jnp.int32, u.shape, 0)
    prev = jnp.where(t == 0, 0.0, pltpu.roll(u, 1, axis=0))
    nxt = jnp.where(t == s - 1, 0.0, pltpu.roll(u, s - 1, axis=0))
    w = w_ref[...]
    y = b_ref[0].astype(F32) * (prev * w[0:1] + u * w[1:2] + nxt * w[2:3])
    o_ref[0] = y.astype(o_ref.dtype)


def _short_conv(main, conv_w):
    b, s, _ = main.shape
    nt = SC_WIDTH // LANE
    return pl.pallas_call(
        _conv_kernel,
        grid=(b, nt),
        in_specs=[pl.BlockSpec((1, s, LANE), lambda bi, c: (bi, 0, OFF_B_SC // LANE + c)),
                  pl.BlockSpec((1, s, LANE), lambda bi, c: (bi, 0, OFF_C_SC // LANE + c)),
                  pl.BlockSpec((1, s, LANE), lambda bi, c: (bi, 0, OFF_X_SC // LANE + c)),
                  pl.BlockSpec((3, LANE), lambda bi, c: (0, c))],
        out_specs=pl.BlockSpec((1, s, LANE), lambda bi, c: (bi, 0, c)),
        out_shape=jax.ShapeDtypeStruct((b, s, SC_WIDTH), BF16),
        compiler_params=_cparams(("parallel", "parallel")),
        name="short_conv",
    )(main, main, main, conv_w)


def _gla_matrices(reverse):
    c = GLA_C
    t = np.arange(c)[:, None]
    m = np.arange(c)[None, :]
    blocks = [m <= t, m > t]
    for b in GLA_LEVELS:
        first = (t // (2 * b)) * (2 * b) + b
        is_q = (t & b) != 0
        blocks.append(np.where(is_q, (m > first) & (m <= t), (m > t) & (m <= first)))
    mats = np.stack(blocks).astype(np.float32)
    if reverse:
        mats = mats[:, ::-1, ::-1]
    return jnp.asarray(mats.reshape(-1, c), dtype=BF16)


def _gla_kernel(*refs, reverse, emit):
    if emit:
        q_ref, k_ref, v_ref, gt_ref, w2_ref, b2_ref, a_ref, s0_ref, o_ref, sf_ref, st_ref = refs
    else:
        k_ref, v_ref, gt_ref, w2_ref, b2_ref, a_ref, s0_ref, sf_ref, st_ref = refs
    c = GLA_C
    step = pl.program_id(1)

    @pl.when(step == 0)
    def _():
        st_ref[...] = s0_ref[0]

    logit = jnp.dot(gt_ref[0], w2_ref[...], precision=lax.Precision.HIGHEST,
                    preferred_element_type=F32) + b2_ref[...]
    g = (jnp.minimum(logit, 0.0) - jnp.log1p(jnp.exp(-jnp.abs(logit)))) * (1.0 / GLA_GATE_TAU)
    g_hi = g.astype(BF16)
    g_lo = (g - g_hi.astype(F32)).astype(BF16)
    amat = a_ref[...]
    args = _dot(amat, g_hi) + _dot(amat, g_lo)
    cum = args[0:c]
    rem = args[c:2 * c]
    last_row = cum[0:1] if reverse else cum[c - 1:c]

    k = k_ref[0].astype(F32)
    v = v_ref[0]
    if emit:
        q = q_ref[0].astype(F32) * (GLA_DK ** -0.5)
        ti = lax.broadcasted_iota(jnp.int32, (c, c), 0)
        si = lax.broadcasted_iota(jnp.int32, (c, c), 1)
        tq = lax.broadcasted_iota(jnp.int32, (c, GLA_DK), 0)
        if reverse:
            ti, si, tq = c - 1 - ti, c - 1 - si, c - 1 - tq
        diag = ti == si
        pair_masks = [(((ti ^ si) >> (b.bit_length() - 1)) == 1) & ((ti & b) != 0) for b in GLA_LEVELS]
        q_rows = [(tq & b) != 0 for b in GLA_LEVELS]

    outs = []
    for h in range(GLA_HEADS):
        sl = slice(h * GLA_DK, (h + 1) * GLA_DK)
        kh = k[:, sl]
        vh = v[:, h * GLA_DV:(h + 1) * GLA_DV]
        state = st_ref[h]
        if emit:
            qh = q[:, sl]
            qd = (qh * jnp.exp(cum[:, sl])).astype(BF16)
            o = _dot(qd, state.astype(BF16))
            att = jnp.where(diag, _dot_nt(qh.astype(BF16), kh.astype(BF16)), 0.0)
            for l in range(len(GLA_LEVELS)):
                e = jnp.exp(args[(2 + l) * c:(3 + l) * c, sl])
                scaled = e * jnp.where(q_rows[l], qh, kh)
                zq = jnp.where(q_rows[l], scaled, 0.0).astype(BF16)
                zk = jnp.where(q_rows[l], 0.0, scaled).astype(BF16)
                att = jnp.where(pair_masks[l], _dot_nt(zq, zk), att)
            outs.append(o + _dot(att.astype(BF16), vh))
        kd = (kh * jnp.exp(rem[:, sl])).astype(BF16)
        decay = jnp.exp(jnp.broadcast_to(last_row[:, sl], (GLA_DK, GLA_DK))).T
        decay = jnp.concatenate([decay] * (GLA_DV // GLA_DK), axis=1)
        st_ref[h] = decay * state + _dot_tn(kd, vh)

    if emit:
        o_ref[0] = jnp.concatenate(outs, axis=-1)

    @pl.when(step == pl.num_programs(1) - 1)
    def _():
        sf_ref[0] = st_ref[...]


def _gla_scan(main, gate, w2, b2, s0, reverse, emit):
    b, l, _ = main.shape
    n = l // GLA_C
    amat = _gla_matrices(reverse)
    chunk = (lambda bi, s: (bi, n - 1 - s)) if reverse else (lambda bi, s: (bi, s))

    def col(block):
        return lambda bi, s: chunk(bi, s) + (block,)

    in_specs = []
    args = []
    if emit:
        in_specs.append(pl.BlockSpec((1, GLA_C, GLA_KEY_WIDTH), col(OFF_Q_GLA // GLA_KEY_WIDTH)))
        args.append(main)
    in_specs += [pl.BlockSpec((1, GLA_C, GLA_KEY_WIDTH), col(OFF_K_GLA // GLA_KEY_WIDTH)),
                 pl.BlockSpec((1, GLA_C, GLA_VAL_WIDTH), col(OFF_V_GLA // GLA_VAL_WIDTH)),
                 pl.BlockSpec((1, GLA_C, LANE), col(0)),
                 pl.BlockSpec((LANE, GLA_KEY_WIDTH), lambda bi, s: (0, 0)),
                 pl.BlockSpec((1, GLA_KEY_WIDTH), lambda bi, s: (0, 0)),
                 pl.BlockSpec(amat.shape, lambda bi, s: (0, 0)),
                 pl.BlockSpec((1, GLA_HEADS, GLA_DK, GLA_DV), lambda bi, s: (bi, 0, 0, 0))]
    args += [main, main, gate, w2, b2, amat, s0]
    state_spec = pl.BlockSpec((1, GLA_HEADS, GLA_DK, GLA_DV), lambda bi, s: (bi, 0, 0, 0))
    state_shape = jax.ShapeDtypeStruct((b, GLA_HEADS, GLA_DK, GLA_DV), F32)
    if emit:
        out_specs = [pl.BlockSpec((1, GLA_C, GLA_VAL_WIDTH), col(0)), state_spec]
        out_shape = [jax.ShapeDtypeStruct((b, l, GLA_VAL_WIDTH), F32), state_shape]
    else:
        out_specs = [state_spec]
        out_shape = [state_shape]
    res = pl.pallas_call(
        functools.partial(_gla_kernel, reverse=reverse, emit=emit),
        grid=(b, n),
        in_specs=in_specs,
        out_specs=out_specs,
        out_shape=out_shape,
        scratch_shapes=[pltpu.VMEM((GLA_HEADS, GLA_DK, GLA_DV), F32)],
        compiler_params=_cparams(("parallel", "arbitrary")),
        name="gla_scan",
    )(*args)
    return (res[0], res[1]) if emit else (None, res[0])


def _gla_gate_weights(gate_w, gate_b):
    w2, b2 = [], []
    for dr in range(2):
        w = jnp.zeros((LANE, GLA_KEY_WIDTH), F32)
        w = w.at[dr * GLA_GATE_RANK:(dr + 1) * GLA_GATE_RANK].set(gate_w[dr])
        w2.append(w)
        b2.append(gate_b[dr][None])
    return w2, b2


def _merge_kernel(ona_ref, osc_ref, of_ref, ob_ref, r_ref, gna_ref, gsc_ref, ggl_ref, x_ref, gt_ref,
                  gn_ref, wna_ref, wsc_ref, wgl_ref, wo_ref, g2_ref, sh2_ref, sc2_ref, wr_ref,
                  xo_ref, h2_ref, lg_ref):
    o = of_ref[0] + ob_ref[0]
    normed = []
    for h in range(GLA_HEADS):
        oh = o[:, h * GLA_DV:(h + 1) * GLA_DV]
        ms = jnp.mean(oh * oh, axis=-1, keepdims=True)
        normed.append(oh * lax.rsqrt(ms + RMS_EPS))
    r = r_ref[0].astype(F32)
    y_gla = jnp.concatenate(normed, axis=-1) * gn_ref[...] * (r * _sigmoid(r))
    y = (_sigmoid(gna_ref[0].astype(F32)) * _dot(ona_ref[0], wna_ref[...])
         + _sigmoid(gsc_ref[0].astype(F32)) * _dot(osc_ref[0], wsc_ref[...])
         + _sigmoid(ggl_ref[0].astype(F32)) * _dot(y_gla.astype(BF16), wgl_ref[...]))
    xn = x_ref[0] + gt_ref[0] * _dot(y.astype(BF16), wo_ref[...])
    xo_ref[0] = xn
    ms = jnp.mean(xn * xn, axis=-1, keepdims=True)
    h2 = xn * lax.rsqrt(ms + RMS_EPS) * g2_ref[...] * (1.0 + sc2_ref[0]) + sh2_ref[0]
    h2b = h2.astype(BF16)
    h2_ref[0] = h2b
    lg_ref[0] = _dot(h2b, wr_ref[...])


def _merge(o_na, o_sc, o_f, o_b, main, x, gt1, gn, w_na, w_sc, w_gla, w_out, g2, sh2, sc2, w_router, tm):
    b, s, d = x.shape
    per_batch = gt1.shape[0] == b
    mod_map = (lambda bi, i: (bi, 0, 0)) if per_batch else (lambda bi, i: (0, 0, 0))
    tok = lambda width, blk: pl.BlockSpec((1, tm, width), lambda bi, i: (bi, i, blk))
    full = lambda arr: pl.BlockSpec(arr.shape, lambda bi, i: (0,) * arr.ndim)
    mod = pl.BlockSpec((1, 1, d), mod_map)
    gn_t = jnp.tile(gn, GLA_HEADS)[None]
    g2_t = g2[None]
    return pl.pallas_call(
        _merge_kernel,
        grid=(b, s // tm),
        in_specs=[tok(NA_WIDTH, 0), tok(SC_WIDTH, 0), tok(GLA_VAL_WIDTH, 0), tok(GLA_VAL_WIDTH, 0),
                  tok(d, OFF_R_GLA // d), tok(d, OFF_MERGE // d), tok(d, OFF_MERGE // d + 1),
                  tok(d, OFF_MERGE // d + 2), tok(d, 0), mod,
                  full(gn_t), full(w_na), full(w_sc), full(w_gla), full(w_out), full(g2_t), mod, mod,
                  full(w_router)],
        out_specs=[tok(d, 0), tok(d, 0), tok(LANE, 0)],
        out_shape=[jax.ShapeDtypeStruct((b, s, d), F32),
                   jax.ShapeDtypeStruct((b, s, d), BF16),
                   jax.ShapeDtypeStruct((b, s, LANE), F32)],
        compiler_params=_cparams(("parallel", "parallel")),
        name="merge",
    )(o_na, o_sc, o_f, o_b, main, main, main, main, x, gt1, gn_t, w_na, w_sc, w_gla, w_out, g2_t,
      sh2, sc2, w_router)


def _router_kernel(lg_ref, br_ref, tri_ref, eidx_ref, gw_ref, rank_ref, cnt_ref, carry_ref):
    tm = lg_ref.shape[0]

    @pl.when(pl.program_id(0) == 0)
    def _():
        carry_ref[...] = jnp.zeros_like(carry_ref)

    scores = _sigmoid(lg_ref[...].T[:N_EXPERTS])
    sel = scores + br_ref[...]
    neg = -jnp.inf

    sel3 = sel.reshape(N_EXPERT_GROUPS, GROUP_SIZE, tm)
    i3 = lax.broadcasted_iota(jnp.int32, sel3.shape, 1)
    m1 = sel3.max(axis=1, keepdims=True)
    first = jnp.where(sel3 == m1, i3, GROUP_SIZE).min(axis=1, keepdims=True)
    m2 = jnp.where(i3 == first, neg, sel3).max(axis=1, keepdims=True)
    gscore = (m1 + m2)[:, 0, :]

    gi = lax.broadcasted_iota(jnp.int32, gscore.shape, 0)
    gmask = jnp.zeros(gscore.shape, jnp.bool_)
    for _ in range(TOPK_GROUPS):
        m = gscore.max(axis=0, keepdims=True)
        pick = gi == jnp.where(gscore == m, gi, N_EXPERT_GROUPS).min(axis=0, keepdims=True)
        gmask = gmask | pick
        gscore = jnp.where(pick, neg, gscore)
    emask = jnp.broadcast_to(gmask[:, None, :], sel3.shape).reshape(N_EXPERTS, tm)
    sel = jnp.where(emask, sel, neg)

    ei = lax.broadcasted_iota(jnp.int32, sel.shape, 0)
    picks, idxs, ws = [], [], []
    for _ in range(TOP_K):
        m = sel.max(axis=0, keepdims=True)
        idx = jnp.where(sel == m, ei, N_EXPERTS).min(axis=0, keepdims=True)
        pick = ei == idx
        picks.append(pick)
        idxs.append(idx)
        ws.append(jnp.where(pick, scores, 0.0).sum(axis=0, keepdims=True))
        sel = jnp.where(pick, neg, sel)
    w = jnp.concatenate(ws, axis=0)
    gw_ref[...] = w / w.sum(axis=0, keepdims=True) * ROUTED_SCALE
    eidx_ref[...] = jnp.concatenate(idxs, axis=0)

    onehot = picks[0]
    for p in picks[1:]:
        onehot = onehot | p
    onehot = jnp.where(onehot, 1.0, 0.0).astype(BF16)
    before = _dot(onehot, tri_ref[...]) + jnp.tile(carry_ref[...], (1, tm // LANE))
    rank_ref[...] = jnp.concatenate(
        [jnp.where(p, before, 0.0).sum(axis=0, keepdims=True) for p in picks], axis=0).astype(jnp.int32)
    carry_ref[...] += _dot(onehot, jnp.ones((tm, LANE), BF16))
    cnt_ref[...] = carry_ref[...]


def _route(logits, b_router, tm=512):
    t = logits.shape[0]
    br = jnp.broadcast_to(b_router.astype(F32)[:, None], (N_EXPERTS, tm))
    tri = jnp.asarray(np.triu(np.ones((tm, tm), np.float32), 1), dtype=BF16)
    kt = lambda dt: jax.ShapeDtypeStruct((TOP_K, t), dt)
    eidx, gw, rank, cnt = pl.pallas_call(
        _router_kernel,
        grid=(t // tm,),
        in_specs=[pl.BlockSpec((tm, LANE), lambda i: (i, 0)),
                  pl.BlockSpec((N_EXPERTS, tm), lambda i: (0, 0)),
                  pl.BlockSpec((tm, tm), lambda i: (0, 0))],
        out_specs=[pl.BlockSpec((TOP_K, tm), lambda i: (0, i)),
                   pl.BlockSpec((TOP_K, tm), lambda i: (0, i)),
                   pl.BlockSpec((TOP_K, tm), lambda i: (0, i)),
                   pl.BlockSpec((N_EXPERTS, LANE), lambda i: (0, 0))],
        out_shape=[kt(jnp.int32), kt(F32), kt(jnp.int32),
                   jax.ShapeDtypeStruct((N_EXPERTS, LANE), F32)],
        scratch_shapes=[pltpu.VMEM((N_EXPERTS, LANE), F32)],
        compiler_params=_cparams(("arbitrary",)),
        name="router",
    )(logits, br, tri)
    return eidx, gw, rank, cnt[:, 0].astype(jnp.int32)


def _expert_kernel(be_ref, x_ref, wg_ref, wu_ref, wd_ref, o_ref):
    del be_ref
    x = x_ref[...]
    a = _dot(x, wg_ref[0])
    hid = a * _sigmoid(a) * _dot(x, wu_ref[0])
    o_ref[...] = _dot(hid.astype(BF16), wd_ref[0]).astype(o_ref.dtype)


def _experts(xs, blk_e, w_gate, w_up, w_down):
    slots, d = xs.shape
    nb = slots // MOE_BLOCK
    return pl.pallas_call(
        _expert_kernel,
        grid_spec=pltpu.PrefetchScalarGridSpec(
            num_scalar_prefetch=1,
            grid=(nb,),
            in_specs=[pl.BlockSpec((MOE_BLOCK, d), lambda i, be: (i, 0)),
                      pl.BlockSpec((1, d, EXPERT_FF), lambda i, be: (be[i], 0, 0)),
                      pl.BlockSpec((1, d, EXPERT_FF), lambda i, be: (be[i], 0, 0)),
                      pl.BlockSpec((1, EXPERT_FF, d), lambda i, be: (be[i], 0, 0))],
            out_specs=pl.BlockSpec((MOE_BLOCK, d), lambda i, be: (i, 0))),
        out_shape=jax.ShapeDtypeStruct((slots, d), BF16),
        compiler_params=_cparams(("arbitrary",)),
        name="experts",
    )(blk_e, xs, w_gate, w_up, w_down)


def _combine_kernel(yg_ref, gw_ref, h_ref, x_ref, gt_ref, wsg_ref, wsu_ref, wsd_ref, gf_ref, o_ref, *, final):
    h = h_ref[0]
    a = _dot(h, wsg_ref[...])
    hid = a * _sigmoid(a) * _dot(h, wsu_ref[...])
    y = _dot(hid.astype(BF16), wsd_ref[...])
    gw = gw_ref[0]
    for k in range(TOP_K):
        y = y + gw[:, k:k + 1] * yg_ref[k, 0].astype(F32)
    xn = x_ref[0] + gt_ref[0] * y
    if final:
        ms = jnp.mean(xn * xn, axis=-1, keepdims=True)
        xn = xn * lax.rsqrt(ms + RMS_EPS) * gf_ref[...]
    o_ref[0] = xn


def _combine(yg, gw, h2, x, gt2, ws_gate, ws_up, ws_down, g_final, final, tm):
    b, s, d = x.shape
    per_batch = gt2.shape[0] == b
    mod_map = (lambda bi, i: (bi, 0, 0)) if per_batch else (lambda bi, i: (0, 0, 0))
    full = lambda arr: pl.BlockSpec(arr.shape, lambda bi, i: (0,) * arr.ndim)
    gf = g_final[None]
    return pl.pallas_call(
        functools.partial(_combine_kernel, final=final),
        grid=(b, s // tm),
        in_specs=[pl.BlockSpec((TOP_K, 1, tm, d), lambda bi, i: (0, bi, i, 0)),
                  pl.BlockSpec((1, tm, TOP_K), lambda bi, i: (bi, i, 0)),
                  pl.BlockSpec((1, tm, d), lambda bi, i: (bi, i, 0)),
                  pl.BlockSpec((1, tm, d), lambda bi, i: (bi, i, 0)),
                  pl.BlockSpec((1, 1, d), mod_map),
                  full(ws_gate), full(ws_up), full(ws_down), full(gf)],
        out_specs=pl.BlockSpec((1, tm, d), lambda bi, i: (bi, i, 0)),
        out_shape=jax.ShapeDtypeStruct((b, s, d), F32),
        compiler_params=_cparams(("parallel", "parallel")),
        name="combine",
    )(yg, gw, h2, x, gt2, ws_gate, ws_up, ws_down, gf)


def _reorder_w_in(w_in):
    k_na, v_na, k_gla = w_in[:, 0:512], w_in[:, 512:1024], w_in[:, 1024:1536]
    v_gla, gate = w_in[:, 1536:2560], w_in[:, 2560:2592]
    q_na = w_in[:, 2592:3104] * (NA_HEAD_DIM ** -0.5)
    rest = w_in[:, 3104:]
    main = jnp.concatenate([v_gla, k_na, v_na, k_gla, q_na, rest], axis=1).astype(BF16)
    gate = jnp.pad(gate, ((0, 0), (0, LANE - 2 * GLA_GATE_RANK))).astype(BF16)
    return main, gate


def _layer(x, ctx_s, mods, mods_ctx, p, ctx_out, final, g_final):
    b, s, d = x.shape
    sc = ctx_s.shape[1]
    sh1, sc1, gt1, sh2, sc2, gt2 = mods
    csh1, csc1, cgt1, csh2, csc2, cgt2 = mods_ctx

    w_main, w_gate = _reorder_w_in(p['w_in'])
    main, gate = _proj_in(x, p['g_norm1'], sh1, sc1, w_main, w_gate, tm=min(1024, s), tn=1024)
    if ctx_out:
        main_c, gate_c = _proj_in(ctx_s, p['g_norm1'], csh1, csc1, w_main, w_gate, tm=sc, tn=1024)
    else:
        main_c, gate_c = _proj_in(ctx_s, p['g_norm1'], csh1, csc1, w_main[:, :N_KV_MAIN], w_gate,
                                  tm=sc, tn=N_KV_MAIN // 2)

    o_na = _na_latent(main, main_c, p['na_rpb'])
    o_sc = _short_conv(main, p['conv_w'])

    w2, b2 = _gla_gate_weights(p['gla_gate_w'], p['gla_gate_b'])
    s0 = jnp.zeros((b, GLA_HEADS, GLA_DK, GLA_DV), F32)
    o_cf, st_f = _gla_scan(main_c, gate_c, w2[0], b2[0], s0, False, ctx_out)
    o_cb, st_b = _gla_scan(main_c, gate_c, w2[1], b2[1], s0, True, ctx_out)
    o_f, _ = _gla_scan(main, gate, w2[0], b2[0], st_f, False, True)
    o_b, _ = _gla_scan(main, gate, w2[1], b2[1], st_b, True, True)

    w_na = p['w_branch_na'].astype(BF16)
    w_sc = p['w_branch_sc'].astype(BF16)
    w_gla = p['w_branch_gla'].astype(BF16)
    w_out = p['w_out'].astype(BF16)
    w_router = jnp.pad(p['w_router'], ((0, 0), (0, LANE - N_EXPERTS))).astype(BF16)
    x, h2, logits = _merge(o_na, o_sc, o_f, o_b, main, x, gt1, p['gla_norm_g'], w_na, w_sc, w_gla, w_out,
                           p['g_norm2'], sh2, sc2, w_router, tm=min(256, s))
    n_lat = b * s
    h2_all = h2.reshape(n_lat, d)
    lg_all = logits.reshape(n_lat, LANE)
    if ctx_out:
        o_na_c = _dense_attn(main_c)
        o_sc_c = _short_conv(main_c, p['conv_w'])
        ctx_s, h2_c, lg_c = _merge(o_na_c, o_sc_c, o_cf, o_cb, main_c, ctx_s, cgt1, p['gla_norm_g'], w_na,
                                   w_sc, w_gla, w_out, p['g_norm2'], csh2, csc2, w_router, tm=min(256, sc))
        h2_all = jnp.concatenate([h2_all, h2_c.reshape(b * sc, d)], axis=0)
        lg_all = jnp.concatenate([lg_all, lg_c.reshape(b * sc, LANE)], axis=0)

    t = h2_all.shape[0]
    eidx, gw, rank, counts = _route(lg_all, p['b_router'])
    padded = (counts + MOE_BLOCK - 1) // MOE_BLOCK * MOE_BLOCK
    pad_end = jnp.cumsum(padded)
    pad_start = pad_end - padded
    dest = pad_start[eidx] + rank
    n_blocks = -(-(t * TOP_K + N_EXPERTS * (MOE_BLOCK - 1)) // MOE_BLOCK)
    slots = n_blocks * MOE_BLOCK
    tok = jnp.broadcast_to(jnp.arange(t, dtype=jnp.int32)[None], (TOP_K, t))
    slot_tok = jnp.full((slots,), t, jnp.int32).at[dest.reshape(-1)].set(tok.reshape(-1))
    blk_e = jnp.minimum(jnp.searchsorted(pad_end, jnp.arange(n_blocks, dtype=jnp.int32) * MOE_BLOCK,
                                         side='right'), N_EXPERTS - 1).astype(jnp.int32)
    h2_pad = jnp.concatenate([h2_all, jnp.zeros((8, d), BF16)], axis=0)
    xs = h2_pad[slot_tok]
    ys = _experts(xs, blk_e, p['w_exp_gate'].astype(BF16), p['w_exp_up'].astype(BF16),
                  p['w_exp_down'].astype(BF16))
    yg = ys[dest]
    gw_t = gw.T

    ws_gate = p['w_sh_gate'].astype(BF16)
    ws_up = p['w_sh_up'].astype(BF16)
    ws_down = p['w_sh_down'].astype(BF16)
    x = _combine(yg[:, :n_lat].reshape(TOP_K, b, s, d), gw_t[:n_lat].reshape(b, s, TOP_K), h2, x, gt2,
                 ws_gate, ws_up, ws_down, g_final, final, tm=min(256, s))
    if ctx_out:
        ctx_s = _combine(yg[:, n_lat:].reshape(TOP_K, b, sc, d), gw_t[n_lat:].reshape(b, sc, TOP_K), h2_c,
                         ctx_s, cgt2, ws_gate, ws_up, ws_down, g_final, False, tm=min(256, sc))
    return x, ctx_s


def kernel(x, c, ctx, c_ctx, w_mod, b_mod, g_norm1, g_norm2, w_in, na_rpb, w_branch_na, conv_w, w_branch_sc,
           gla_gate_w, gla_gate_b, gla_norm_g, w_branch_gla, w_out, w_router, b_router, w_exp_gate, w_exp_up,
           w_exp_down, w_sh_gate, w_sh_up, w_sh_down, g_final):
    stacked = dict(g_norm1=g_norm1, g_norm2=g_norm2, w_in=w_in, na_rpb=na_rpb, w_branch_na=w_branch_na,
                   conv_w=conv_w, w_branch_sc=w_branch_sc, gla_gate_w=gla_gate_w, gla_gate_b=gla_gate_b,
                   gla_norm_g=gla_norm_g, w_branch_gla=w_branch_gla, w_out=w_out, w_router=w_router,
                   b_router=b_router, w_exp_gate=w_exp_gate, w_exp_up=w_exp_up, w_exp_down=w_exp_down,
                   w_sh_gate=w_sh_gate, w_sh_up=w_sh_up, w_sh_down=w_sh_down)
    depth = w_in.shape[0]
    ctx_s = ctx
    for i in range(depth):
        p = {name: arr[i] for name, arr in stacked.items()}
        mods, mods_ctx = _mod_vectors(c, c_ctx, w_mod[i], b_mod[i])
        last = i == depth - 1
        x, ctx_s = _layer(x, ctx_s, mods, mods_ctx, p, not last, last, g_final)
    return x
```

```python
import functools

import numpy as np
import jax
import jax.numpy as jnp
from jax import lax
from jax.experimental import pallas as pl
from jax.experimental.pallas import tpu as pltpu
from jax.experimental.pallas import tpu_sc as plsc

F32 = jnp.float32
BF16 = jnp.bfloat16
U32 = jnp.uint32

D_MODEL = 1024
N_MOD = 6
RMS_EPS = 1e-6
NEG_INF = -1e30
GRID_W = 64
NA_HEADS = 8
NA_HEAD_DIM = 64
NA_WIDTH = NA_HEADS * NA_HEAD_DIM
NA_WIN_R = 8
NA_WIN_C = 16
SC_WIDTH = 512
GLA_HEADS = 4
GLA_KEY_WIDTH = 512
GLA_VAL_WIDTH = 1024
GLA_DK = GLA_KEY_WIDTH // GLA_HEADS
GLA_DV = GLA_VAL_WIDTH // GLA_HEADS
GLA_GATE_RANK = 16
GLA_GATE_TAU = 16.0
N_EXPERTS = 64
N_EXPERT_GROUPS = 8
GROUP_SIZE = N_EXPERTS // N_EXPERT_GROUPS
TOPK_GROUPS = 4
TOP_K = 8
EXPERT_FF = 256
ROUTED_SCALE = 2.5
MOE_BLOCK = 256

LANE = 128
GLA_C = 128
GLA_LEVELS = tuple(GLA_C >> (i + 1) for i in range(GLA_C.bit_length() - 1))
VMEM_LIMIT = 48 * 1024 * 1024
SC_WINDOW = 128
SC_ROW = 256
SC_PARTS = D_MODEL // 2 // SC_ROW

OFF_V_GLA = 0
OFF_K_NA = 1024
OFF_V_NA = 1536
OFF_K_GLA = 2048
N_KV_MAIN = 2560
OFF_Q_NA = 2560
OFF_B_SC = 3072
OFF_C_SC = 3584
OFF_X_SC = 4096
OFF_Q_GLA = 4608
OFF_R_GLA = 5120
OFF_MERGE = 6144
N_MAIN = 9216


def _cparams(sem, vmem=VMEM_LIMIT):
    return pltpu.CompilerParams(dimension_semantics=sem, vmem_limit_bytes=vmem)


def _dot(a, b):
    return jnp.dot(a, b, preferred_element_type=F32)


def _dot_nt(a, b):
    return lax.dot_general(a, b, (((1,), (1,)), ((), ())), preferred_element_type=F32)


def _dot_tn(a, b):
    return lax.dot_general(a, b, (((0,), (0,)), ((), ())), preferred_element_type=F32)


def _sigmoid(x):
    return 1.0 / (1.0 + jnp.exp(-x))


def _pack_rows(x):
    n = x.shape[1] // 2
    r = x.astype(BF16).astype(F32)
    lo = pltpu.bitcast(r[:, :n], U32) >> 16
    hi = pltpu.bitcast(r[:, n:], U32)
    return hi | lo


def _store_parts(ref, words):
    for part in range(SC_PARTS):
        dst = ref.at[part, 0] if len(ref.shape) == 4 else ref.at[part]
        dst[...] = words[:, part * SC_ROW:(part + 1) * SC_ROW]


def _load_parts(ref, *lead):
    return jnp.concatenate([ref[(part,) + lead] for part in range(SC_PARTS)], axis=-1)


def _unpack_rows(w):
    lo = pltpu.bitcast(w << 16, F32)
    hi = pltpu.bitcast(w & jnp.uint32(0xFFFF0000), F32)
    return lo, hi


def _mod_kernel(a_ref, w_ref, b_ref, o_ref):
    a = a_ref[...]
    a = a * _sigmoid(a)
    o_ref[...] = _dot(a.astype(BF16), w_ref[...].astype(BF16)) + b_ref[...]


def _mod_vectors(c, c_ctx, w_mod, b_mod):
    b = c.shape[0]
    rows = -(-(b + 1) // 8) * 8
    a = jnp.concatenate([c, c_ctx[None], jnp.zeros((rows - b - 1, D_MODEL), F32)], axis=0)
    n = N_MOD * D_MODEL
    tn = 1536
    out = pl.pallas_call(
        _mod_kernel,
        grid=(n // tn,),
        in_specs=[pl.BlockSpec((rows, D_MODEL), lambda j: (0, 0)),
                  pl.BlockSpec((D_MODEL, tn), lambda j: (0, j)),
                  pl.BlockSpec((1, tn), lambda j: (0, j))],
        out_specs=pl.BlockSpec((rows, tn), lambda j: (0, j)),
        out_shape=jax.ShapeDtypeStruct((rows, n), F32),
        compiler_params=_cparams(("parallel",)),
        name="mod_vectors",
    )(a, w_mod, b_mod[None])
    lat = out[:b].reshape(b, N_MOD, 1, D_MODEL)
    ctx = out[b].reshape(N_MOD, 1, 1, D_MODEL)
    return [lat[:, i] for i in range(N_MOD)], [ctx[i] for i in range(N_MOD)]


def _proj_kernel(x_ref, g_ref, sh_ref, sc_ref, w_ref, wg_ref, o_ref, og_ref, h_ref):
    @pl.when(pl.program_id(2) == 0)
    def _():
        x = x_ref[0]
        ms = jnp.mean(x * x, axis=-1, keepdims=True)
        h = x * lax.rsqrt(ms + RMS_EPS) * g_ref[...] * (1.0 + sc_ref[0]) + sh_ref[0]
        hb = h.astype(BF16)
        h_ref[...] = hb
        og_ref[0] = _dot(hb, wg_ref[...])

    o_ref[0] = _dot(h_ref[...], w_ref[...]).astype(o_ref.dtype)


def _proj_in(x, g, shift, scale, w_main, w_gate, tm, tn):
    b, s, d = x.shape
    n = w_main.shape[1]
    per_batch = shift.shape[0] == b
    mod_map = (lambda bi, i, j: (bi, 0, 0)) if per_batch else (lambda bi, i, j: (0, 0, 0))
    return pl.pallas_call(
        _proj_kernel,
        grid=(b, s // tm, n // tn),
        in_specs=[pl.BlockSpec((1, tm, d), lambda bi, i, j: (bi, i, 0)),
                  pl.BlockSpec((1, d), lambda bi, i, j: (0, 0)),
                  pl.BlockSpec((1, 1, d), mod_map),
                  pl.BlockSpec((1, 1, d), mod_map),
                  pl.BlockSpec((d, tn), lambda bi, i, j: (0, j)),
                  pl.BlockSpec((d, LANE), lambda bi, i, j: (0, 0))],
        out_specs=[pl.BlockSpec((1, tm, tn), lambda bi, i, j: (bi, i, j)),
                   pl.BlockSpec((1, tm, LANE), lambda bi, i, j: (bi, i, 0))],
        out_shape=[jax.ShapeDtypeStruct((b, s, n), BF16),
                   jax.ShapeDtypeStruct((b, s, LANE), F32)],
        scratch_shapes=[pltpu.VMEM((tm, d), BF16)],
        compiler_params=_cparams(("parallel", "parallel", "arbitrary")),
        name="proj_in",
    )(x, g[None], shift, scale, w_main, w_gate)


def _softmax_av(q, keys, vals, biases):
    scores = []
    for kk, bb in zip(keys, biases):
        s = _dot_nt(q, kk)
        scores.append(s if bb is None else s + bb)
    m = scores[0].max(axis=-1, keepdims=True)
    for s in scores[1:]:
        m = jnp.maximum(m, s.max(axis=-1, keepdims=True))
    num = None
    den = None
    for s, vv in zip(scores, vals):
        e = jnp.exp(s - m)
        dsum = e.sum(axis=-1, keepdims=True)
        o = _dot(e.astype(BF16), vv)
        num = o if num is None else num + o
        den = dsum if den is None else den + dsum
    return num / den


def _na_kernel(q_ref, k_ref, v_ref, kc_ref, vc_ref, bias_ref, o_ref, *, rows, kr):
    r = pl.program_id(1)
    row_start = jnp.clip(r - kr // 2, 0, rows - kr)
    start = pl.multiple_of(row_start * GRID_W, GRID_W)
    n_win = kr * GRID_W
    q = q_ref[0]
    kw = k_ref[0, pl.ds(start, n_win), :]
    vw = v_ref[0, pl.ds(start, n_win), :]
    kc = kc_ref[0]
    vc = vc_ref[0]
    outs = []
    for h in range(NA_HEADS):
        sl = slice(h * NA_HEAD_DIM, (h + 1) * NA_HEAD_DIM)
        outs.append(_softmax_av(q[:, sl], [kw[:, sl], kc[:, sl]], [vw[:, sl], vc[:, sl]],
                                [bias_ref[0, h], None]))
    o_ref[0] = jnp.concatenate(outs, axis=-1).astype(o_ref.dtype)


def _na_bias_table(rpb, rows, kr):
    o = np.arange(kr)[:, None]
    j = np.arange(kr)[None, :]
    d_row = j - o + NA_WIN_R - 1
    col = np.arange(GRID_W)
    col_start = np.clip(col - NA_WIN_C // 2, 0, GRID_W - NA_WIN_C)
    col_ok = (col[None, :] >= col_start[:, None]) & (col[None, :] < col_start[:, None] + NA_WIN_C)
    d_col = np.clip(col[None, :] - col[:, None], -(NA_WIN_C - 1), NA_WIN_C - 1) + NA_WIN_C - 1
    bias = rpb.astype(F32)[:, d_row[:, None, :, None], d_col[None, :, None, :]]
    bias = jnp.where(col_ok[None, None, :, None, :], bias, NEG_INF)
    return bias.transpose(1, 0, 2, 3, 4).reshape(kr, NA_HEADS, GRID_W, kr * GRID_W)


def _na_latent(main, main_ctx, rpb):
    b, s, _ = main.shape
    sc = main_ctx.shape[1]
    rows = s // GRID_W
    kr = min(NA_WIN_R, rows)
    bias = _na_bias_table(rpb, rows, kr)
    w = NA_WIDTH

    def bias_map(bi, r):
        return (r - jnp.clip(r - kr // 2, 0, rows - kr), 0, 0, 0)

    return pl.pallas_call(
        functools.partial(_na_kernel, rows=rows, kr=kr),
        grid=(b, rows),
        in_specs=[pl.BlockSpec((1, GRID_W, w), lambda bi, r: (bi, r, OFF_Q_NA // w)),
                  pl.BlockSpec((1, s, w), lambda bi, r: (bi, 0, OFF_K_NA // w)),
                  pl.BlockSpec((1, s, w), lambda bi, r: (bi, 0, OFF_V_NA // w)),
                  pl.BlockSpec((1, sc, w), lambda bi, r: (bi, 0, OFF_K_NA // w)),
                  pl.BlockSpec((1, sc, w), lambda bi, r: (bi, 0, OFF_V_NA // w)),
                  pl.BlockSpec((1, NA_HEADS, GRID_W, kr * GRID_W), bias_map)],
        out_specs=pl.BlockSpec((1, GRID_W, w), lambda bi, r: (bi, r, 0)),
        out_shape=jax.ShapeDtypeStruct((b, s, w), BF16),
        compiler_params=_cparams(("parallel", "arbitrary")),
        name="na_latent",
    )(main, main, main, main_ctx, main_ctx, bias)


def _dense_attn_kernel(q_ref, k_ref, v_ref, o_ref):
    q = q_ref[0]
    k = k_ref[0]
    v = v_ref[0]
    outs = []
    for h in range(NA_HEADS):
        sl = slice(h * NA_HEAD_DIM, (h + 1) * NA_HEAD_DIM)
        outs.append(_softmax_av(q[:, sl], [k[:, sl]], [v[:, sl]], [None]))
    o_ref[0] = jnp.concatenate(outs, axis=-1).astype(o_ref.dtype)


def _dense_attn(main_ctx):
    b, sc, _ = main_ctx.shape
    w = NA_WIDTH
    return pl.pallas_call(
        _dense_attn_kernel,
        grid=(b,),
        in_specs=[pl.BlockSpec((1, sc, w), lambda bi: (bi, 0, OFF_Q_NA // w)),
                  pl.BlockSpec((1, sc, w), lambda bi: (bi, 0, OFF_K_NA // w)),
                  pl.BlockSpec((1, sc, w), lambda bi: (bi, 0, OFF_V_NA // w))],
        out_specs=pl.BlockSpec((1, sc, w), lambda bi: (bi, 0, 0)),
        out_shape=jax.ShapeDtypeStruct((b, sc, w), BF16),
        compiler_params=_cparams(("parallel",)),
        name="ctx_attn",
    )(main_ctx, main_ctx, main_ctx)


def _conv_kernel(b_ref, c_ref, x_ref, w_ref, o_ref):
    u = c_ref[0].astype(F32) * x_ref[0].astype(F32)
    s = u.shape[0]
    t = lax.broadcasted_iota(jnp.int32, u.shape, 0)
    prev = jnp.where(t == 0, 0.0, pltpu.roll(u, 1, axis=0))
    nxt = jnp.where(t == s - 1, 0.0, pltpu.roll(u, s - 1, axis=0))
    w = w_ref[...]
    y = b_ref[0].astype(F32) * (prev * w[0:1] + u * w[1:2] + nxt * w[2:3])
    o_ref[0] = y.astype(o_ref.dtype)


def _short_conv(main, conv_w):
    b, s, _ = main.shape
    nt = SC_WIDTH // LANE
    return pl.pallas_call(
        _conv_kernel,
        grid=(b, nt),
        in_specs=[pl.BlockSpec((1, s, LANE), lambda bi, c: (bi, 0, OFF_B_SC // LANE + c)),
                  pl.BlockSpec((1, s, LANE), lambda bi, c: (bi, 0, OFF_C_SC // LANE + c)),
                  pl.BlockSpec((1, s, LANE), lambda bi, c: (bi, 0, OFF_X_SC // LANE + c)),
                  pl.BlockSpec((3, LANE), lambda bi, c: (0, c))],
        out_specs=pl.BlockSpec((1, s, LANE), lambda bi, c: (bi, 0, c)),
        out_shape=jax.ShapeDtypeStruct((b, s, SC_WIDTH), BF16),
        compiler_params=_cparams(("parallel", "parallel")),
        name="short_conv",
    )(main, main, main, conv_w)


def _gla_matrices(reverse):
    c = GLA_C
    t = np.arange(c)[:, None]
    m = np.arange(c)[None, :]
    blocks = [m <= t, m > t]
    for b in GLA_LEVELS:
        first = (t // (2 * b)) * (2 * b) + b
        is_q = (t & b) != 0
        blocks.append(np.where(is_q, (m > first) & (m <= t), (m > t) & (m <= first)))
    mats = np.stack(blocks).astype(np.float32)
    if reverse:
        mats = mats[:, ::-1, ::-1]
    return jnp.asarray(mats.reshape(-1, c), dtype=BF16)


def _gla_kernel(*refs, reverse, emit):
    if emit:
        q_ref, k_ref, v_ref, gt_ref, w2_ref, b2_ref, a_ref, s0_ref, o_ref, sf_ref, st_ref = refs
    else:
        k_ref, v_ref, gt_ref, w2_ref, b2_ref, a_ref, s0_ref, sf_ref, st_ref = refs
    c = GLA_C
    step = pl.program_id(1)

    @pl.when(step == 0)
    def _():
        st_ref[...] = s0_ref[0]

    logit = jnp.dot(gt_ref[0], w2_ref[...], precision=lax.Precision.HIGHEST,
                    preferred_element_type=F32) + b2_ref[...]
    g = (jnp.minimum(logit, 0.0) - jnp.log1p(jnp.exp(-jnp.abs(logit)))) * (1.0 / GLA_GATE_TAU)
    g_hi = g.astype(BF16)
    g_lo = (g - g_hi.astype(F32)).astype(BF16)
    amat = a_ref[...]
    args = _dot(amat, g_hi) + _dot(amat, g_lo)
    cum = args[0:c]
    rem = args[c:2 * c]
    last_row = cum[0:1] if reverse else cum[c - 1:c]

    k = k_ref[0].astype(F32)
    v = v_ref[0]
    if emit:
        q = q_ref[0].astype(F32) * (GLA_DK ** -0.5)
        ti = lax.broadcasted_iota(jnp.int32, (c, c), 0)
        si = lax.broadcasted_iota(jnp.int32, (c, c), 1)
        tq = lax.broadcasted_iota(jnp.int32, (c, GLA_DK), 0)
        if reverse:
            ti, si, tq = c - 1 - ti, c - 1 - si, c - 1 - tq
        diag = ti == si
        pair_masks = [(((ti ^ si) >> (b.bit_length() - 1)) == 1) & ((ti & b) != 0) for b in GLA_LEVELS]
        q_rows = [(tq & b) != 0 for b in GLA_LEVELS]

    outs = []
    for h in range(GLA_HEADS):
        sl = slice(h * GLA_DK, (h + 1) * GLA_DK)
        kh = k[:, sl]
        vh = v[:, h * GLA_DV:(h + 1) * GLA_DV]
        state = st_ref[h]
        if emit:
            qh = q[:, sl]
            qd = (qh * jnp.exp(cum[:, sl])).astype(BF16)
            o = _dot(qd, state.astype(BF16))
            att = jnp.where(diag, _dot_nt(qh.astype(BF16), kh.astype(BF16)), 0.0)
            for l in range(len(GLA_LEVELS)):
                e = jnp.exp(args[(2 + l) * c:(3 + l) * c, sl])
                scaled = e * jnp.where(q_rows[l], qh, kh)
                zq = jnp.where(q_rows[l], scaled, 0.0).astype(BF16)
                zk = jnp.where(q_rows[l], 0.0, scaled).astype(BF16)
                att = jnp.where(pair_masks[l], _dot_nt(zq, zk), att)
            outs.append(o + _dot(att.astype(BF16), vh))
        kd = (kh * jnp.exp(rem[:, sl])).astype(BF16)
        decay = jnp.exp(jnp.broadcast_to(last_row[:, sl], (GLA_DK, GLA_DK))).T
        decay = jnp.concatenate([decay] * (GLA_DV // GLA_DK), axis=1)
        st_ref[h] = decay * state + _dot_tn(kd, vh)

    if emit:
        o_ref[0] = jnp.concatenate(outs, axis=-1)

    @pl.when(step == pl.num_programs(1) - 1)
    def _():
        sf_ref[0] = st_ref[...]


def _gla_scan(main, gate, w2, b2, s0, reverse, emit):
    b, l, _ = main.shape
    n = l // GLA_C
    amat = _gla_matrices(reverse)
    chunk = (lambda bi, s: (bi, n - 1 - s)) if reverse else (lambda bi, s: (bi, s))

    def col(block):
        return lambda bi, s: chunk(bi, s) + (block,)

    in_specs = []
    args = []
    if emit:
        in_specs.append(pl.BlockSpec((1, GLA_C, GLA_KEY_WIDTH), col(OFF_Q_GLA // GLA_KEY_WIDTH)))
        args.append(main)
    in_specs += [pl.BlockSpec((1, GLA_C, GLA_KEY_WIDTH), col(OFF_K_GLA // GLA_KEY_WIDTH)),
                 pl.BlockSpec((1, GLA_C, GLA_VAL_WIDTH), col(OFF_V_GLA // GLA_VAL_WIDTH)),
                 pl.BlockSpec((1, GLA_C, LANE), col(0)),
                 pl.BlockSpec((LANE, GLA_KEY_WIDTH), lambda bi, s: (0, 0)),
                 pl.BlockSpec((1, GLA_KEY_WIDTH), lambda bi, s: (0, 0)),
                 pl.BlockSpec(amat.shape, lambda bi, s: (0, 0)),
                 pl.BlockSpec((1, GLA_HEADS, GLA_DK, GLA_DV), lambda bi, s: (bi, 0, 0, 0))]
    args += [main, main, gate, w2, b2, amat, s0]
    state_spec = pl.BlockSpec((1, GLA_HEADS, GLA_DK, GLA_DV), lambda bi, s: (bi, 0, 0, 0))
    state_shape = jax.ShapeDtypeStruct((b, GLA_HEADS, GLA_DK, GLA_DV), F32)
    if emit:
        out_specs = [pl.BlockSpec((1, GLA_C, GLA_VAL_WIDTH), col(0)), state_spec]
        out_shape = [jax.ShapeDtypeStruct((b, l, GLA_VAL_WIDTH), F32), state_shape]
    else:
        out_specs = [state_spec]
        out_shape = [state_shape]
    res = pl.pallas_call(
        functools.partial(_gla_kernel, reverse=reverse, emit=emit),
        grid=(b, n),
        in_specs=in_specs,
        out_specs=out_specs,
        out_shape=out_shape,
        scratch_shapes=[pltpu.VMEM((GLA_HEADS, GLA_DK, GLA_DV), F32)],
        compiler_params=_cparams(("parallel", "arbitrary")),
        name="gla_scan",
    )(*args)
    return (res[0], res[1]) if emit else (None, res[0])


def _gla_gate_weights(gate_w, gate_b):
    w2, b2 = [], []
    for dr in range(2):
        w = jnp.zeros((LANE, GLA_KEY_WIDTH), F32)
        w = w.at[dr * GLA_GATE_RANK:(dr + 1) * GLA_GATE_RANK].set(gate_w[dr])
        w2.append(w)
        b2.append(gate_b[dr][None])
    return w2, b2


def _merge_kernel(ona_ref, osc_ref, of_ref, ob_ref, r_ref, gna_ref, gsc_ref, ggl_ref, x_ref, gt_ref,
                  gn_ref, wna_ref, wsc_ref, wgl_ref, wo_ref, g2_ref, sh2_ref, sc2_ref, wr_ref,
                  xo_ref, h2_ref, hp_ref, lg_ref):
    o = of_ref[0] + ob_ref[0]
    normed = []
    for h in range(GLA_HEADS):
        oh = o[:, h * GLA_DV:(h + 1) * GLA_DV]
        ms = jnp.mean(oh * oh, axis=-1, keepdims=True)
        normed.append(oh * lax.rsqrt(ms + RMS_EPS))
    r = r_ref[0].astype(F32)
    y_gla = jnp.concatenate(normed, axis=-1) * gn_ref[...] * (r * _sigmoid(r))
    y = (_sigmoid(gna_ref[0].astype(F32)) * _dot(ona_ref[0], wna_ref[...])
         + _sigmoid(gsc_ref[0].astype(F32)) * _dot(osc_ref[0], wsc_ref[...])
         + _sigmoid(ggl_ref[0].astype(F32)) * _dot(y_gla.astype(BF16), wgl_ref[...]))
    xn = x_ref[0] + gt_ref[0] * _dot(y.astype(BF16), wo_ref[...])
    xo_ref[0] = xn
    ms = jnp.mean(xn * xn, axis=-1, keepdims=True)
    h2 = xn * lax.rsqrt(ms + RMS_EPS) * g2_ref[...] * (1.0 + sc2_ref[0]) + sh2_ref[0]
    h2b = h2.astype(BF16)
    h2_ref[0] = h2b
    _store_parts(hp_ref, _pack_rows(h2))
    lg_ref[0] = _dot(h2b, wr_ref[...])


def _merge(o_na, o_sc, o_f, o_b, main, x, gt1, gn, w_na, w_sc, w_gla, w_out, g2, sh2, sc2, w_router, tm):
    b, s, d = x.shape
    per_batch = gt1.shape[0] == b
    mod_map = (lambda bi, i: (bi, 0, 0)) if per_batch else (lambda bi, i: (0, 0, 0))
    tok = lambda width, blk: pl.BlockSpec((1, tm, width), lambda bi, i: (bi, i, blk))
    full = lambda arr: pl.BlockSpec(arr.shape, lambda bi, i: (0,) * arr.ndim)
    mod = pl.BlockSpec((1, 1, d), mod_map)
    gn_t = jnp.tile(gn, GLA_HEADS)[None]
    g2_t = g2[None]
    return pl.pallas_call(
        _merge_kernel,
        grid=(b, s // tm),
        in_specs=[tok(NA_WIDTH, 0), tok(SC_WIDTH, 0), tok(GLA_VAL_WIDTH, 0), tok(GLA_VAL_WIDTH, 0),
                  tok(d, OFF_R_GLA // d), tok(d, OFF_MERGE // d), tok(d, OFF_MERGE // d + 1),
                  tok(d, OFF_MERGE // d + 2), tok(d, 0), mod,
                  full(gn_t), full(w_na), full(w_sc), full(w_gla), full(w_out), full(g2_t), mod, mod,
                  full(w_router)],
        out_specs=[tok(d, 0), tok(d, 0),
                   pl.BlockSpec((SC_PARTS, 1, tm, SC_ROW), lambda bi, i: (0, bi, i, 0)), tok(LANE, 0)],
        out_shape=[jax.ShapeDtypeStruct((b, s, d), F32),
                   jax.ShapeDtypeStruct((b, s, d), BF16),
                   jax.ShapeDtypeStruct((SC_PARTS, b, s, SC_ROW), U32),
                   jax.ShapeDtypeStruct((b, s, LANE), F32)],
        compiler_params=_cparams(("parallel", "parallel")),
        name="merge",
    )(o_na, o_sc, o_f, o_b, main, main, main, main, x, gt1, gn_t, w_na, w_sc, w_gla, w_out, g2_t,
      sh2, sc2, w_router)


def _router_kernel(lg_ref, br_ref, tri_ref, eidx_ref, gw_ref, rank_ref, cnt_ref, carry_ref):
    tm = lg_ref.shape[0]

    @pl.when(pl.program_id(0) == 0)
    def _():
        carry_ref[...] = jnp.zeros_like(carry_ref)

    scores = _sigmoid(lg_ref[...].T[:N_EXPERTS])
    sel = scores + br_ref[...]
    neg = -jnp.inf

    sel3 = sel.reshape(N_EXPERT_GROUPS, GROUP_SIZE, tm)
    i3 = lax.broadcasted_iota(jnp.int32, sel3.shape, 1)
    m1 = sel3.max(axis=1, keepdims=True)
    first = jnp.where(sel3 == m1, i3, GROUP_SIZE).min(axis=1, keepdims=True)
    m2 = jnp.where(i3 == first, neg, sel3).max(axis=1, keepdims=True)
    gscore = (m1 + m2)[:, 0, :]

    gi = lax.broadcasted_iota(jnp.int32, gscore.shape, 0)
    gmask = jnp.zeros(gscore.shape, jnp.bool_)
    for _ in range(TOPK_GROUPS):
        m = gscore.max(axis=0, keepdims=True)
        pick = gi == jnp.where(gscore == m, gi, N_EXPERT_GROUPS).min(axis=0, keepdims=True)
        gmask = gmask | pick
        gscore = jnp.where(pick, neg, gscore)
    emask = jnp.broadcast_to(gmask[:, None, :], sel3.shape).reshape(N_EXPERTS, tm)
    sel = jnp.where(emask, sel, neg)

    ei = lax.broadcasted_iota(jnp.int32, sel.shape, 0)
    picks, idxs, ws = [], [], []
    for _ in range(TOP_K):
        m = sel.max(axis=0, keepdims=True)
        idx = jnp.where(sel == m, ei, N_EXPERTS).min(axis=0, keepdims=True)
        pick = ei == idx
        picks.append(pick)
        idxs.append(idx)
        ws.append(jnp.where(pick, scores, 0.0).sum(axis=0, keepdims=True))
        sel = jnp.where(pick, neg, sel)
    w = jnp.concatenate(ws, axis=0)
    gw_ref[...] = w / w.sum(axis=0, keepdims=True) * ROUTED_SCALE
    eidx_ref[...] = jnp.concatenate(idxs, axis=0)

    onehot = picks[0]
    for p in picks[1:]:
        onehot = onehot | p
    onehot = jnp.where(onehot, 1.0, 0.0).astype(BF16)
    before = _dot(onehot, tri_ref[...]) + jnp.tile(carry_ref[...], (1, tm // LANE))
    rank_ref[...] = jnp.concatenate(
        [jnp.where(p, before, 0.0).sum(axis=0, keepdims=True) for p in picks], axis=0).astype(jnp.int32)
    carry_ref[...] += _dot(onehot, jnp.ones((tm, LANE), BF16))
    cnt_ref[...] = carry_ref[...]


def _route(logits, b_router, tm=512):
    t = logits.shape[0]
    br = jnp.broadcast_to(b_router.astype(F32)[:, None], (N_EXPERTS, tm))
    tri = jnp.asarray(np.triu(np.ones((tm, tm), np.float32), 1), dtype=BF16)
    kt = lambda dt: jax.ShapeDtypeStruct((TOP_K, t), dt)
    eidx, gw, rank, cnt = pl.pallas_call(
        _router_kernel,
        grid=(t // tm,),
        in_specs=[pl.BlockSpec((tm, LANE), lambda i: (i, 0)),
                  pl.BlockSpec((N_EXPERTS, tm), lambda i: (0, 0)),
                  pl.BlockSpec((tm, tm), lambda i: (0, 0))],
        out_specs=[pl.BlockSpec((TOP_K, tm), lambda i: (0, i)),
                   pl.BlockSpec((TOP_K, tm), lambda i: (0, i)),
                   pl.BlockSpec((TOP_K, tm), lambda i: (0, i)),
                   pl.BlockSpec((N_EXPERTS, LANE), lambda i: (0, 0))],
        out_shape=[kt(jnp.int32), kt(F32), kt(jnp.int32),
                   jax.ShapeDtypeStruct((N_EXPERTS, LANE), F32)],
        scratch_shapes=[pltpu.VMEM((N_EXPERTS, LANE), F32)],
        compiler_params=_cparams(("arbitrary",)),
        name="router",
    )(logits, br, tri)
    return eidx, gw, rank, cnt[:, 0].astype(jnp.int32)


def _sc_mesh():
    return plsc.VectorSubcoreMesh(core_axis_name="core", subcore_axis_name="subcore")


def _dispatch_rows(xp, dest, slots):
    parts, t, _ = xp.shape
    nwin = parts * t // SC_WINDOW
    idx = dest.reshape(TOP_K, t // SC_WINDOW, SC_WINDOW).transpose(1, 0, 2)
    idx = jnp.concatenate([idx + part * slots for part in range(parts)], axis=0)

    @pl.kernel(out_type=jax.ShapeDtypeStruct((parts * slots, SC_ROW), xp.dtype), mesh=_sc_mesh(),
               scratch_types=[], name="moe_dispatch")
    def run(x_hbm, i_hbm, o_hbm):
        def body(x_vmem, i_vmem):
            for k in range(TOP_K):
                pltpu.sync_copy(x_vmem, o_hbm.at[i_vmem.at[0, k]])

        pltpu.emit_pipeline(
            body,
            grid=(nwin,),
            in_specs=[pl.BlockSpec((SC_WINDOW, SC_ROW), lambda i: (i, 0)),
                      pl.BlockSpec((1, TOP_K, SC_WINDOW), lambda i: (i, 0, 0))],
            out_specs=[],
            core_axis_name=("core", "subcore"),
            dimension_semantics=(pltpu.PARALLEL,),
        )(x_hbm, i_hbm)

    return run(xp.reshape(parts * t, SC_ROW), idx).reshape(parts, slots, SC_ROW)


def _gather_rows(yp, idx):
    n = idx.shape[0]
    parts, slots, _ = yp.shape
    idx = jnp.concatenate([idx + part * slots for part in range(parts)]).reshape(1, n * parts)

    @pl.kernel(out_type=jax.ShapeDtypeStruct((n * parts, SC_ROW), yp.dtype), mesh=_sc_mesh(),
               scratch_types=[], name="moe_gather")
    def run(y_hbm, i_hbm, o_hbm):
        def body(i_vmem, o_vmem):
            pltpu.sync_copy(y_hbm.at[i_vmem.at[0]], o_vmem)

        pltpu.emit_pipeline(
            body,
            grid=(n * parts // SC_WINDOW,),
            in_specs=[pl.BlockSpec((1, SC_WINDOW), lambda i: (0, i))],
            out_specs=[pl.BlockSpec((SC_WINDOW, SC_ROW), lambda i: (i, 0))],
            core_axis_name=("core", "subcore"),
            dimension_semantics=(pltpu.PARALLEL,),
        )(i_hbm, o_hbm)

    return run(yp.reshape(parts * slots, SC_ROW), idx).reshape(parts, n, SC_ROW)


def _expert_kernel(be_ref, bv_ref, x_ref, wg_ref, wu_ref, wd_ref, o_ref):
    del be_ref
    valid = bv_ref[pl.program_id(0)]

    @pl.when(valid > 0)
    def _():
        w = _load_parts(x_ref)
        row = lax.broadcasted_iota(jnp.int32, w.shape, 0)
        w = jnp.where(row < valid, w, jnp.uint32(0))
        lo, hi = _unpack_rows(w)
        x = jnp.concatenate([lo, hi], axis=1).astype(BF16)
        a = _dot(x, wg_ref[0])
        hid = a * _sigmoid(a) * _dot(x, wu_ref[0])
        _store_parts(o_ref, _pack_rows(_dot(hid.astype(BF16), wd_ref[0])))


def _experts(xs, blk_e, blk_valid, w_gate, w_up, w_down):
    parts, slots, _ = xs.shape
    d = D_MODEL
    nb = slots // MOE_BLOCK
    return pl.pallas_call(
        _expert_kernel,
        grid_spec=pltpu.PrefetchScalarGridSpec(
            num_scalar_prefetch=2,
            grid=(nb,),
            in_specs=[pl.BlockSpec((parts, MOE_BLOCK, SC_ROW), lambda i, be, bv: (0, i, 0)),
                      pl.BlockSpec((1, d, EXPERT_FF), lambda i, be, bv: (be[i], 0, 0)),
                      pl.BlockSpec((1, d, EXPERT_FF), lambda i, be, bv: (be[i], 0, 0)),
                      pl.BlockSpec((1, EXPERT_FF, d), lambda i, be, bv: (be[i], 0, 0))],
            out_specs=pl.BlockSpec((parts, MOE_BLOCK, SC_ROW), lambda i, be, bv: (0, i, 0))),
        out_shape=jax.ShapeDtypeStruct((parts, slots, SC_ROW), U32),
        compiler_params=_cparams(("arbitrary",)),
        name="experts",
    )(blk_e, blk_valid, xs, w_gate, w_up, w_down)


def _combine_kernel(yg_ref, gw_ref, h_ref, x_ref, gt_ref, wsg_ref, wsu_ref, wsd_ref, gf_ref, o_ref, *, final):
    h = h_ref[0]
    a = _dot(h, wsg_ref[...])
    hid = a * _sigmoid(a) * _dot(h, wsu_ref[...])
    y = _dot(hid.astype(BF16), wsd_ref[...])
    gw = gw_ref[...]
    y_lo = y[:, :D_MODEL // 2]
    y_hi = y[:, D_MODEL // 2:]
    for k in range(TOP_K):
        lo, hi = _unpack_rows(_load_parts(yg_ref, k))
        y_lo = y_lo + gw[:, k:k + 1] * lo
        y_hi = y_hi + gw[:, k:k + 1] * hi
    y = jnp.concatenate([y_lo, y_hi], axis=1)
    xn = x_ref[0] + gt_ref[0] * y
    if final:
        ms = jnp.mean(xn * xn, axis=-1, keepdims=True)
        xn = xn * lax.rsqrt(ms + RMS_EPS) * gf_ref[...]
    o_ref[0] = xn


def _combine(yg, gw, tok_off, h2, x, gt2, ws_gate, ws_up, ws_down, g_final, final, tm):
    b, s, d = x.shape
    per_batch = gt2.shape[0] == b
    mod_map = (lambda bi, i: (bi, 0, 0)) if per_batch else (lambda bi, i: (0, 0, 0))
    full = lambda arr: pl.BlockSpec(arr.shape, lambda bi, i: (0,) * arr.ndim)
    gf = g_final[None]
    blk0 = tok_off // tm
    nblk = s // tm
    return pl.pallas_call(
        functools.partial(_combine_kernel, final=final),
        grid=(b, s // tm),
        in_specs=[pl.BlockSpec((SC_PARTS, TOP_K, tm, SC_ROW), lambda bi, i: (0, 0, blk0 + bi * nblk + i, 0)),
                  pl.BlockSpec((tm, TOP_K), lambda bi, i: (blk0 + bi * nblk + i, 0)),
                  pl.BlockSpec((1, tm, d), lambda bi, i: (bi, i, 0)),
                  pl.BlockSpec((1, tm, d), lambda bi, i: (bi, i, 0)),
                  pl.BlockSpec((1, 1, d), mod_map),
                  full(ws_gate), full(ws_up), full(ws_down), full(gf)],
        out_specs=pl.BlockSpec((1, tm, d), lambda bi, i: (bi, i, 0)),
        out_shape=jax.ShapeDtypeStruct((b, s, d), F32),
        compiler_params=_cparams(("parallel", "parallel")),
        name="combine",
    )(yg, gw, h2, x, gt2, ws_gate, ws_up, ws_down, gf)


def _reorder_w_in(w_in):
    k_na, v_na, k_gla = w_in[:, 0:512], w_in[:, 512:1024], w_in[:, 1024:1536]
    v_gla, gate = w_in[:, 1536:2560], w_in[:, 2560:2592]
    q_na = w_in[:, 2592:3104] * (NA_HEAD_DIM ** -0.5)
    rest = w_in[:, 3104:]
    main = jnp.concatenate([v_gla, k_na, v_na, k_gla, q_na, rest], axis=1).astype(BF16)
    gate = jnp.pad(gate, ((0, 0), (0, LANE - 2 * GLA_GATE_RANK))).astype(BF16)
    return main, gate


def _layer(x, ctx_s, mods, mods_ctx, p, ctx_out, final, g_final):
    b, s, d = x.shape
    sc = ctx_s.shape[1]
    sh1, sc1, gt1, sh2, sc2, gt2 = mods
    csh1, csc1, cgt1, csh2, csc2, cgt2 = mods_ctx

    w_main, w_gate = _reorder_w_in(p['w_in'])
    main, gate = _proj_in(x, p['g_norm1'], sh1, sc1, w_main, w_gate, tm=min(1024, s), tn=1024)
    if ctx_out:
        main_c, gate_c = _proj_in(ctx_s, p['g_norm1'], csh1, csc1, w_main, w_gate, tm=sc, tn=1024)
    else:
        main_c, gate_c = _proj_in(ctx_s, p['g_norm1'], csh1, csc1, w_main[:, :N_KV_MAIN], w_gate,
                                  tm=sc, tn=N_KV_MAIN // 2)

    o_na = _na_latent(main, main_c, p['na_rpb'])
    o_sc = _short_conv(main, p['conv_w'])

    w2, b2 = _gla_gate_weights(p['gla_gate_w'], p['gla_gate_b'])
    s0 = jnp.zeros((b, GLA_HEADS, GLA_DK, GLA_DV), F32)
    o_cf, st_f = _gla_scan(main_c, gate_c, w2[0], b2[0], s0, False, ctx_out)
    o_cb, st_b = _gla_scan(main_c, gate_c, w2[1], b2[1], s0, True, ctx_out)
    o_f, _ = _gla_scan(main, gate, w2[0], b2[0], st_f, False, True)
    o_b, _ = _gla_scan(main, gate, w2[1], b2[1], st_b, True, True)

    w_na = p['w_branch_na'].astype(BF16)
    w_sc = p['w_branch_sc'].astype(BF16)
    w_gla = p['w_branch_gla'].astype(BF16)
    w_out = p['w_out'].astype(BF16)
    w_router = jnp.pad(p['w_router'], ((0, 0), (0, LANE - N_EXPERTS))).astype(BF16)
    x, h2, h2p, logits = _merge(o_na, o_sc, o_f, o_b, main, x, gt1, p['gla_norm_g'], w_na, w_sc, w_gla, w_out,
                                p['g_norm2'], sh2, sc2, w_router, tm=min(256, s))
    n_lat = b * s
    hp_all = h2p.reshape(SC_PARTS, n_lat, SC_ROW)
    lg_all = logits.reshape(n_lat, LANE)
    if ctx_out:
        o_na_c = _dense_attn(main_c)
        o_sc_c = _short_conv(main_c, p['conv_w'])
        ctx_s, h2_c, h2p_c, lg_c = _merge(o_na_c, o_sc_c, o_cf, o_cb, main_c, ctx_s, cgt1, p['gla_norm_g'],
                                          w_na, w_sc, w_gla, w_out, p['g_norm2'], csh2, csc2, w_router,
                                          tm=min(256, sc))
        hp_all = jnp.concatenate([hp_all, h2p_c.reshape(SC_PARTS, b * sc, SC_ROW)], axis=1)
        lg_all = jnp.concatenate([lg_all, lg_c.reshape(b * sc, LANE)], axis=0)

    t = hp_all.shape[1]
    eidx, gw, rank, counts = _route(lg_all, p['b_router'])
    padded = (counts + MOE_BLOCK - 1) // MOE_BLOCK * MOE_BLOCK
    pad_end = jnp.cumsum(padded)
    pad_start = pad_end - padded
    onehot = eidx[:, :, None] == jnp.arange(N_EXPERTS, dtype=jnp.int32)
    dest = jnp.sum(jnp.where(onehot, pad_start, 0), axis=-1) + rank
    n_blocks = -(-(t * TOP_K + N_EXPERTS * (MOE_BLOCK - 1)) // MOE_BLOCK)
    slots = n_blocks * MOE_BLOCK
    blk_start = jnp.arange(n_blocks, dtype=jnp.int32) * MOE_BLOCK
    blk_e = jnp.minimum(jnp.sum(pad_end[None, :] <= blk_start[:, None], axis=1), N_EXPERTS - 1).astype(jnp.int32)
    used_end = (pad_start + counts)[blk_e]
    blk_valid = jnp.clip(used_end - blk_start, 0, MOE_BLOCK).astype(jnp.int32)

    xs = _dispatch_rows(hp_all, dest, slots)
    ys = _experts(xs, blk_e, blk_valid, p['w_exp_gate'].astype(BF16), p['w_exp_up'].astype(BF16),
                  p['w_exp_down'].astype(BF16))
    yg = _gather_rows(ys, dest.reshape(-1)).reshape(SC_PARTS, TOP_K, t, SC_ROW)
    gw_t = gw.T

    ws_gate = p['w_sh_gate'].astype(BF16)
    ws_up = p['w_sh_up'].astype(BF16)
    ws_down = p['w_sh_down'].astype(BF16)
    x = _combine(yg, gw_t, 0, h2, x, gt2, ws_gate, ws_up, ws_down, g_final, final, tm=min(256, s))
    if ctx_out:
        ctx_s = _combine(yg, gw_t, n_lat, h2_c, ctx_s, cgt2, ws_gate, ws_up, ws_down, g_final, False,
                         tm=min(256, sc))
    return x, ctx_s


def kernel(x, c, ctx, c_ctx, w_mod, b_mod, g_norm1, g_norm2, w_in, na_rpb, w_branch_na, conv_w, w_branch_sc,
           gla_gate_w, gla_gate_b, gla_norm_g, w_branch_gla, w_out, w_router, b_router, w_exp_gate, w_exp_up,
           w_exp_down, w_sh_gate, w_sh_up, w_sh_down, g_final):
    stacked = dict(g_norm1=g_norm1, g_norm2=g_norm2, w_in=w_in, na_rpb=na_rpb, w_branch_na=w_branch_na,
                   conv_w=conv_w, w_branch_sc=w_branch_sc, gla_gate_w=gla_gate_w, gla_gate_b=gla_gate_b,
                   gla_norm_g=gla_norm_g, w_branch_gla=w_branch_gla, w_out=w_out, w_router=w_router,
                   b_router=b_router, w_exp_gate=w_exp_gate, w_exp_up=w_exp_up, w_exp_down=w_exp_down,
                   w_sh_gate=w_sh_gate, w_sh_up=w_sh_up, w_sh_down=w_sh_down)
    depth = w_in.shape[0]
    ctx_s = ctx
    for i in range(depth):
        p = {name: arr[i] for name, arr in stacked.items()}
        mods, mods_ctx = _mod_vectors(c, c_ctx, w_mod[i], b_mod[i])
        last = i == depth - 1
        x, ctx_s = _layer(x, ctx_s, mods, mods_ctx, p, not last, last, g_final)
    return x
```

```python
import functools

import numpy as np
import jax
import jax.numpy as jnp
from jax import lax
from jax.experimental import pallas as pl
from jax.experimental.pallas import tpu as pltpu
from jax.experimental.pallas import tpu_sc as plsc

F32 = jnp.float32
BF16 = jnp.bfloat16
U32 = jnp.uint32

D_MODEL = 1024
N_MOD = 6
RMS_EPS = 1e-6
NEG_INF = -1e30
GRID_W = 64
NA_HEADS = 8
NA_HEAD_DIM = 64
NA_WIDTH = NA_HEADS * NA_HEAD_DIM
NA_WIN_R = 8
NA_WIN_C = 16
SC_WIDTH = 512
GLA_HEADS = 4
GLA_KEY_WIDTH = 512
GLA_VAL_WIDTH = 1024
GLA_DK = GLA_KEY_WIDTH // GLA_HEADS
GLA_DV = GLA_VAL_WIDTH // GLA_HEADS
GLA_GATE_RANK = 16
GLA_GATE_TAU = 16.0
N_EXPERTS = 64
N_EXPERT_GROUPS = 8
GROUP_SIZE = N_EXPERTS // N_EXPERT_GROUPS
TOPK_GROUPS = 4
TOP_K = 8
EXPERT_FF = 256
ROUTED_SCALE = 2.5
MOE_BLOCK = 256

LANE = 128
GLA_C = 128
GLA_LEVELS = tuple(GLA_C >> (i + 1) for i in range(GLA_C.bit_length() - 1))
VMEM_LIMIT = 48 * 1024 * 1024
SC_WINDOW = 128
SC_ROW = 256
SC_PARTS = D_MODEL // 2 // SC_ROW

OFF_V_GLA = 0
OFF_K_NA = 1024
OFF_V_NA = 1536
OFF_K_GLA = 2048
N_KV_MAIN = 2560
OFF_Q_NA = 2560
OFF_B_SC = 3072
OFF_C_SC = 3584
OFF_X_SC = 4096
OFF_Q_GLA = 4608
OFF_R_GLA = 5120
OFF_MERGE = 6144
N_MAIN = 9216


def _cparams(sem, vmem=VMEM_LIMIT):
    return pltpu.CompilerParams(dimension_semantics=sem, vmem_limit_bytes=vmem)


def _dot(a, b):
    return jnp.dot(a, b, preferred_element_type=F32)


def _dot_nt(a, b):
    return lax.dot_general(a, b, (((1,), (1,)), ((), ())), preferred_element_type=F32)


def _dot_tn(a, b):
    return lax.dot_general(a, b, (((0,), (0,)), ((), ())), preferred_element_type=F32)


def _sigmoid(x):
    return 1.0 / (1.0 + jnp.exp(-x))


def _pack_rows(x):
    n = x.shape[1] // 2
    r = x.astype(BF16).astype(F32)
    lo = pltpu.bitcast(r[:, :n], U32) >> 16
    hi = pltpu.bitcast(r[:, n:], U32)
    return hi | lo


def _store_parts(ref, words):
    for part in range(SC_PARTS):
        dst = ref.at[part, 0] if len(ref.shape) == 4 else ref.at[part]
        dst[...] = words[:, part * SC_ROW:(part + 1) * SC_ROW]


def _load_parts(ref, *lead):
    return jnp.concatenate([ref[(part,) + lead] for part in range(SC_PARTS)], axis=-1)


def _unpack_rows(w):
    lo = pltpu.bitcast(w << 16, F32)
    hi = pltpu.bitcast(w & jnp.uint32(0xFFFF0000), F32)
    return lo, hi


def _mod_kernel(a_ref, w_ref, b_ref, o_ref):
    a = a_ref[...]
    a = a * _sigmoid(a)
    o_ref[...] = _dot(a.astype(BF16), w_ref[...].astype(BF16)) + b_ref[...]


def _mod_vectors(c, c_ctx, w_mod, b_mod):
    b = c.shape[0]
    rows = -(-(b + 1) // 8) * 8
    a = jnp.concatenate([c, c_ctx[None], jnp.zeros((rows - b - 1, D_MODEL), F32)], axis=0)
    n = N_MOD * D_MODEL
    tn = 1536
    out = pl.pallas_call(
        _mod_kernel,
        grid=(n // tn,),
        in_specs=[pl.BlockSpec((rows, D_MODEL), lambda j: (0, 0)),
                  pl.BlockSpec((D_MODEL, tn), lambda j: (0, j)),
                  pl.BlockSpec((1, tn), lambda j: (0, j))],
        out_specs=pl.BlockSpec((rows, tn), lambda j: (0, j)),
        out_shape=jax.ShapeDtypeStruct((rows, n), F32),
        compiler_params=_cparams(("parallel",)),
        name="mod_vectors",
    )(a, w_mod, b_mod[None])
    lat = out[:b].reshape(b, N_MOD, 1, D_MODEL)
    ctx = out[b].reshape(N_MOD, 1, 1, D_MODEL)
    return [lat[:, i] for i in range(N_MOD)], [ctx[i] for i in range(N_MOD)]


def _proj_kernel(x_ref, g_ref, sh_ref, sc_ref, w_ref, wg_ref, o_ref, og_ref, h_ref):
    @pl.when(pl.program_id(2) == 0)
    def _():
        x = x_ref[0]
        ms = jnp.mean(x * x, axis=-1, keepdims=True)
        h = x * lax.rsqrt(ms + RMS_EPS) * g_ref[...] * (1.0 + sc_ref[0]) + sh_ref[0]
        hb = h.astype(BF16)
        h_ref[...] = hb
        og_ref[0] = _dot(hb, wg_ref[...])

    o_ref[0] = _dot(h_ref[...], w_ref[...]).astype(o_ref.dtype)


def _proj_in(x, g, shift, scale, w_main, w_gate, tm, tn):
    b, s, d = x.shape
    n = w_main.shape[1]
    per_batch = shift.shape[0] == b
    mod_map = (lambda bi, i, j: (bi, 0, 0)) if per_batch else (lambda bi, i, j: (0, 0, 0))
    return pl.pallas_call(
        _proj_kernel,
        grid=(b, s // tm, n // tn),
        in_specs=[pl.BlockSpec((1, tm, d), lambda bi, i, j: (bi, i, 0)),
                  pl.BlockSpec((1, d), lambda bi, i, j: (0, 0)),
                  pl.BlockSpec((1, 1, d), mod_map),
                  pl.BlockSpec((1, 1, d), mod_map),
                  pl.BlockSpec((d, tn), lambda bi, i, j: (0, j)),
                  pl.BlockSpec((d, LANE), lambda bi, i, j: (0, 0))],
        out_specs=[pl.BlockSpec((1, tm, tn), lambda bi, i, j: (bi, i, j)),
                   pl.BlockSpec((1, tm, LANE), lambda bi, i, j: (bi, i, 0))],
        out_shape=[jax.ShapeDtypeStruct((b, s, n), BF16),
                   jax.ShapeDtypeStruct((b, s, LANE), F32)],
        scratch_shapes=[pltpu.VMEM((tm, d), BF16)],
        compiler_params=_cparams(("parallel", "parallel", "arbitrary")),
        name="proj_in",
    )(x, g[None], shift, scale, w_main, w_gate)


def _softmax_av(q, keys, vals, biases):
    scores = []
    for kk, bb in zip(keys, biases):
        s = _dot_nt(q, kk)
        scores.append(s if bb is None else s + bb)
    m = scores[0].max(axis=-1, keepdims=True)
    for s in scores[1:]:
        m = jnp.maximum(m, s.max(axis=-1, keepdims=True))
    num = None
    den = None
    for s, vv in zip(scores, vals):
        e = jnp.exp(s - m)
        dsum = e.sum(axis=-1, keepdims=True)
        o = _dot(e.astype(BF16), vv)
        num = o if num is None else num + o
        den = dsum if den is None else den + dsum
    return num / den


def _na_kernel(q_ref, k_ref, v_ref, kc_ref, vc_ref, bias_ref, o_ref, *, rows, kr):
    r = pl.program_id(1)
    row_start = jnp.clip(r - kr // 2, 0, rows - kr)
    start = pl.multiple_of(row_start * GRID_W, GRID_W)
    n_win = kr * GRID_W
    q = q_ref[0]
    kw = k_ref[0, pl.ds(start, n_win), :]
    vw = v_ref[0, pl.ds(start, n_win), :]
    kc = kc_ref[0]
    vc = vc_ref[0]
    outs = []
    for h in range(NA_HEADS):
        sl = slice(h * NA_HEAD_DIM, (h + 1) * NA_HEAD_DIM)
        outs.append(_softmax_av(q[:, sl], [kw[:, sl], kc[:, sl]], [vw[:, sl], vc[:, sl]],
                                [bias_ref[0, h], None]))
    o_ref[0] = jnp.concatenate(outs, axis=-1).astype(o_ref.dtype)


def _na_bias_table(rpb, rows, kr):
    col = np.arange(GRID_W)
    col_start = np.clip(col - NA_WIN_C // 2, 0, GRID_W - NA_WIN_C)
    col_ok = (col[None, :] >= col_start[:, None]) & (col[None, :] < col_start[:, None] + NA_WIN_C)
    d_col = np.clip(col[None, :] - col[:, None], -(NA_WIN_C - 1), NA_WIN_C - 1) + NA_WIN_C - 1
    n_dr, n_dc = rpb.shape[1], rpb.shape[2]
    onehot = jnp.asarray((d_col.reshape(-1)[None, :] == np.arange(n_dc)[:, None]).astype(np.float32))
    by_col = jnp.dot(rpb.astype(F32).reshape(NA_HEADS * n_dr, n_dc), onehot, precision=lax.Precision.HIGHEST)
    by_col = by_col.reshape(NA_HEADS, n_dr, GRID_W, GRID_W)
    by_col = jnp.where(col_ok[None, None], by_col, NEG_INF)
    tables = []
    for o in range(kr):
        lo = NA_WIN_R - 1 - o
        tables.append(by_col[:, lo:lo + kr].transpose(0, 2, 1, 3).reshape(NA_HEADS, GRID_W, kr * GRID_W))
    return jnp.stack(tables)


def _na_latent(main, main_ctx, rpb):
    b, s, _ = main.shape
    sc = main_ctx.shape[1]
    rows = s // GRID_W
    kr = min(NA_WIN_R, rows)
    bias = _na_bias_table(rpb, rows, kr)
    w = NA_WIDTH

    def bias_map(bi, r):
        return (r - jnp.clip(r - kr // 2, 0, rows - kr), 0, 0, 0)

    return pl.pallas_call(
        functools.partial(_na_kernel, rows=rows, kr=kr),
        grid=(b, rows),
        in_specs=[pl.BlockSpec((1, GRID_W, w), lambda bi, r: (bi, r, OFF_Q_NA // w)),
                  pl.BlockSpec((1, s, w), lambda bi, r: (bi, 0, OFF_K_NA // w)),
                  pl.BlockSpec((1, s, w), lambda bi, r: (bi, 0, OFF_V_NA // w)),
                  pl.BlockSpec((1, sc, w), lambda bi, r: (bi, 0, OFF_K_NA // w)),
                  pl.BlockSpec((1, sc, w), lambda bi, r: (bi, 0, OFF_V_NA // w)),
                  pl.BlockSpec((1, NA_HEADS, GRID_W, kr * GRID_W), bias_map)],
        out_specs=pl.BlockSpec((1, GRID_W, w), lambda bi, r: (bi, r, 0)),
        out_shape=jax.ShapeDtypeStruct((b, s, w), BF16),
        compiler_params=_cparams(("parallel", "arbitrary")),
        name="na_latent",
    )(main, main, main, main_ctx, main_ctx, bias)


def _dense_attn_kernel(q_ref, k_ref, v_ref, o_ref):
    q = q_ref[0]
    k = k_ref[0]
    v = v_ref[0]
    outs = []
    for h in range(NA_HEADS):
        sl = slice(h * NA_HEAD_DIM, (h + 1) * NA_HEAD_DIM)
        outs.append(_softmax_av(q[:, sl], [k[:, sl]], [v[:, sl]], [None]))
    o_ref[0] = jnp.concatenate(outs, axis=-1).astype(o_ref.dtype)


def _dense_attn(main_ctx):
    b, sc, _ = main_ctx.shape
    w = NA_WIDTH
    return pl.pallas_call(
        _dense_attn_kernel,
        grid=(b,),
        in_specs=[pl.BlockSpec((1, sc, w), lambda bi: (bi, 0, OFF_Q_NA // w)),
                  pl.BlockSpec((1, sc, w), lambda bi: (bi, 0, OFF_K_NA // w)),
                  pl.BlockSpec((1, sc, w), lambda bi: (bi, 0, OFF_V_NA // w))],
        out_specs=pl.BlockSpec((1, sc, w), lambda bi: (bi, 0, 0)),
        out_shape=jax.ShapeDtypeStruct((b, sc, w), BF16),
        compiler_params=_cparams(("parallel",)),
        name="ctx_attn",
    )(main_ctx, main_ctx, main_ctx)


def _conv_kernel(b_ref, c_ref, x_ref, w_ref, o_ref):
    u = c_ref[0].astype(F32) * x_ref[0].astype(F32)
    s = u.shape[0]
    t = lax.broadcasted_iota(jnp.int32, u.shape, 0)
    prev = jnp.where(t == 0, 0.0, pltpu.roll(u, 1, axis=0))
    nxt = jnp.where(t == s - 1, 0.0, pltpu.roll(u, s - 1, axis=0))
    w = w_ref[...]
    y = b_ref[0].astype(F32) * (prev * w[0:1] + u * w[1:2] + nxt * w[2:3])
    o_ref[0] = y.astype(o_ref.dtype)


def _short_conv(main, conv_w):
    b, s, _ = main.shape
    nt = SC_WIDTH // LANE
    return pl.pallas_call(
        _conv_kernel,
        grid=(b, nt),
        in_specs=[pl.BlockSpec((1, s, LANE), lambda bi, c: (bi, 0, OFF_B_SC // LANE + c)),
                  pl.BlockSpec((1, s, LANE), lambda bi, c: (bi, 0, OFF_C_SC // LANE + c)),
                  pl.BlockSpec((1, s, LANE), lambda bi, c: (bi, 0, OFF_X_SC // LANE + c)),
                  pl.BlockSpec((3, LANE), lambda bi, c: (0, c))],
        out_specs=pl.BlockSpec((1, s, LANE), lambda bi, c: (bi, 0, c)),
        out_shape=jax.ShapeDtypeStruct((b, s, SC_WIDTH), BF16),
        compiler_params=_cparams(("parallel", "parallel")),
        name="short_conv",
    )(main, main, main, conv_w)


def _gla_matrices(reverse):
    c = GLA_C
    t = np.arange(c)[:, None]
    m = np.arange(c)[None, :]
    blocks = [m <= t, m > t]
    for b in GLA_LEVELS:
        first = (t // (2 * b)) * (2 * b) + b
        is_q = (t & b) != 0
        blocks.append(np.where(is_q, (m > first) & (m <= t), (m > t) & (m <= first)))
    mats = np.stack(blocks).astype(np.float32)
    if reverse:
        mats = mats[:, ::-1, ::-1]
    return jnp.asarray(mats.reshape(-1, c), dtype=BF16)


def _gla_kernel(*refs, reverse, emit):
    if emit:
        q_ref, k_ref, v_ref, gt_ref, w2_ref, b2_ref, a_ref, s0_ref, o_ref, sf_ref, st_ref = refs
    else:
        k_ref, v_ref, gt_ref, w2_ref, b2_ref, a_ref, s0_ref, sf_ref, st_ref = refs
    c = GLA_C
    step = pl.program_id(1)

    @pl.when(step == 0)
    def _():
        st_ref[...] = s0_ref[0]

    logit = jnp.dot(gt_ref[0], w2_ref[...], precision=lax.Precision.HIGHEST,
                    preferred_element_type=F32) + b2_ref[...]
    g = (jnp.minimum(logit, 0.0) - jnp.log1p(jnp.exp(-jnp.abs(logit)))) * (1.0 / GLA_GATE_TAU)
    g_hi = g.astype(BF16)
    g_lo = (g - g_hi.astype(F32)).astype(BF16)
    amat = a_ref[...]
    args = _dot(amat, g_hi) + _dot(amat, g_lo)
    cum = args[0:c]
    rem = args[c:2 * c]
    last_row = cum[0:1] if reverse else cum[c - 1:c]

    k = k_ref[0].astype(F32)
    v = v_ref[0]
    if emit:
        q = q_ref[0].astype(F32) * (GLA_DK ** -0.5)
        ti = lax.broadcasted_iota(jnp.int32, (c, c), 0)
        si = lax.broadcasted_iota(jnp.int32, (c, c), 1)
        tq = lax.broadcasted_iota(jnp.int32, (c, GLA_DK), 0)
        if reverse:
            ti, si, tq = c - 1 - ti, c - 1 - si, c - 1 - tq
        diag = ti == si
        pair_masks = [(((ti ^ si) >> (b.bit_length() - 1)) == 1) & ((ti & b) != 0) for b in GLA_LEVELS]
        q_rows = [(tq & b) != 0 for b in GLA_LEVELS]

    outs = []
    for h in range(GLA_HEADS):
        sl = slice(h * GLA_DK, (h + 1) * GLA_DK)
        kh = k[:, sl]
        vh = v[:, h * GLA_DV:(h + 1) * GLA_DV]
        state = st_ref[h]
        if emit:
            qh = q[:, sl]
            qd = (qh * jnp.exp(cum[:, sl])).astype(BF16)
            o = _dot(qd, state.astype(BF16))
            att = jnp.where(diag, _dot_nt(qh.astype(BF16), kh.astype(BF16)), 0.0)
            for l in range(len(GLA_LEVELS)):
                e = jnp.exp(args[(2 + l) * c:(3 + l) * c, sl])
                scaled = e * jnp.where(q_rows[l], qh, kh)
                zq = jnp.where(q_rows[l], scaled, 0.0).astype(BF16)
                zk = jnp.where(q_rows[l], 0.0, scaled).astype(BF16)
                att = jnp.where(pair_masks[l], _dot_nt(zq, zk), att)
            outs.append(o + _dot(att.astype(BF16), vh))
        kd = (kh * jnp.exp(rem[:, sl])).astype(BF16)
        decay = jnp.exp(jnp.broadcast_to(last_row[:, sl], (GLA_DK, GLA_DK))).T
        decay = jnp.concatenate([decay] * (GLA_DV // GLA_DK), axis=1)
        st_ref[h] = decay * state + _dot_tn(kd, vh)

    if emit:
        o_ref[0] = jnp.concatenate(outs, axis=-1)

    @pl.when(step == pl.num_programs(1) - 1)
    def _():
        sf_ref[0] = st_ref[...]


def _gla_scan(main, gate, w2, b2, s0, reverse, emit):
    b, l, _ = main.shape
    n = l // GLA_C
    amat = _gla_matrices(reverse)
    chunk = (lambda bi, s: (bi, n - 1 - s)) if reverse else (lambda bi, s: (bi, s))

    def col(block):
        return lambda bi, s: chunk(bi, s) + (block,)

    in_specs = []
    args = []
    if emit:
        in_specs.append(pl.BlockSpec((1, GLA_C, GLA_KEY_WIDTH), col(OFF_Q_GLA // GLA_KEY_WIDTH)))
        args.append(main)
    in_specs += [pl.BlockSpec((1, GLA_C, GLA_KEY_WIDTH), col(OFF_K_GLA // GLA_KEY_WIDTH)),
                 pl.BlockSpec((1, GLA_C, GLA_VAL_WIDTH), col(OFF_V_GLA // GLA_VAL_WIDTH)),
                 pl.BlockSpec((1, GLA_C, LANE), col(0)),
                 pl.BlockSpec((LANE, GLA_KEY_WIDTH), lambda bi, s: (0, 0)),
                 pl.BlockSpec((1, GLA_KEY_WIDTH), lambda bi, s: (0, 0)),
                 pl.BlockSpec(amat.shape, lambda bi, s: (0, 0)),
                 pl.BlockSpec((1, GLA_HEADS, GLA_DK, GLA_DV), lambda bi, s: (bi, 0, 0, 0))]
    args += [main, main, gate, w2, b2, amat, s0]
    state_spec = pl.BlockSpec((1, GLA_HEADS, GLA_DK, GLA_DV), lambda bi, s: (bi, 0, 0, 0))
    state_shape = jax.ShapeDtypeStruct((b, GLA_HEADS, GLA_DK, GLA_DV), F32)
    if emit:
        out_specs = [pl.BlockSpec((1, GLA_C, GLA_VAL_WIDTH), col(0)), state_spec]
        out_shape = [jax.ShapeDtypeStruct((b, l, GLA_VAL_WIDTH), F32), state_shape]
    else:
        out_specs = [state_spec]
        out_shape = [state_shape]
    res = pl.pallas_call(
        functools.partial(_gla_kernel, reverse=reverse, emit=emit),
        grid=(b, n),
        in_specs=in_specs,
        out_specs=out_specs,
        out_shape=out_shape,
        scratch_shapes=[pltpu.VMEM((GLA_HEADS, GLA_DK, GLA_DV), F32)],
        compiler_params=_cparams(("parallel", "arbitrary")),
        name="gla_scan",
    )(*args)
    return (res[0], res[1]) if emit else (None, res[0])


def _gla_gate_weights(gate_w, gate_b):
    w2, b2 = [], []
    for dr in range(2):
        w = jnp.zeros((LANE, GLA_KEY_WIDTH), F32)
        w = w.at[dr * GLA_GATE_RANK:(dr + 1) * GLA_GATE_RANK].set(gate_w[dr])
        w2.append(w)
        b2.append(gate_b[dr][None])
    return w2, b2


def _merge_kernel(ona_ref, osc_ref, of_ref, ob_ref, r_ref, gna_ref, gsc_ref, ggl_ref, x_ref, gt_ref,
                  gn_ref, wna_ref, wsc_ref, wgl_ref, wo_ref, g2_ref, sh2_ref, sc2_ref, wr_ref,
                  xo_ref, h2_ref, hp_ref, lg_ref):
    o = of_ref[0] + ob_ref[0]
    normed = []
    for h in range(GLA_HEADS):
        oh = o[:, h * GLA_DV:(h + 1) * GLA_DV]
        ms = jnp.mean(oh * oh, axis=-1, keepdims=True)
        normed.append(oh * lax.rsqrt(ms + RMS_EPS))
    r = r_ref[0].astype(F32)
    y_gla = jnp.concatenate(normed, axis=-1) * gn_ref[...] * (r * _sigmoid(r))
    y = (_sigmoid(gna_ref[0].astype(F32)) * _dot(ona_ref[0], wna_ref[...])
         + _sigmoid(gsc_ref[0].astype(F32)) * _dot(osc_ref[0], wsc_ref[...])
         + _sigmoid(ggl_ref[0].astype(F32)) * _dot(y_gla.astype(BF16), wgl_ref[...]))
    xn = x_ref[0] + gt_ref[0] * _dot(y.astype(BF16), wo_ref[...])
    xo_ref[0] = xn
    ms = jnp.mean(xn * xn, axis=-1, keepdims=True)
    h2 = xn * lax.rsqrt(ms + RMS_EPS) * g2_ref[...] * (1.0 + sc2_ref[0]) + sh2_ref[0]
    h2b = h2.astype(BF16)
    h2_ref[0] = h2b
    _store_parts(hp_ref, _pack_rows(h2))
    lg_ref[0] = _dot(h2b, wr_ref[...])


def _merge(o_na, o_sc, o_f, o_b, main, x, gt1, gn, w_na, w_sc, w_gla, w_out, g2, sh2, sc2, w_router, tm):
    b, s, d = x.shape
    per_batch = gt1.shape[0] == b
    mod_map = (lambda bi, i: (bi, 0, 0)) if per_batch else (lambda bi, i: (0, 0, 0))
    tok = lambda width, blk: pl.BlockSpec((1, tm, width), lambda bi, i: (bi, i, blk))
    full = lambda arr: pl.BlockSpec(arr.shape, lambda bi, i: (0,) * arr.ndim)
    mod = pl.BlockSpec((1, 1, d), mod_map)
    gn_t = jnp.tile(gn, GLA_HEADS)[None]
    g2_t = g2[None]
    return pl.pallas_call(
        _merge_kernel,
        grid=(b, s // tm),
        in_specs=[tok(NA_WIDTH, 0), tok(SC_WIDTH, 0), tok(GLA_VAL_WIDTH, 0), tok(GLA_VAL_WIDTH, 0),
                  tok(d, OFF_R_GLA // d), tok(d, OFF_MERGE // d), tok(d, OFF_MERGE // d + 1),
                  tok(d, OFF_MERGE // d + 2), tok(d, 0), mod,
                  full(gn_t), full(w_na), full(w_sc), full(w_gla), full(w_out), full(g2_t), mod, mod,
                  full(w_router)],
        out_specs=[tok(d, 0), tok(d, 0),
                   pl.BlockSpec((SC_PARTS, 1, tm, SC_ROW), lambda bi, i: (0, bi, i, 0)), tok(LANE, 0)],
        out_shape=[jax.ShapeDtypeStruct((b, s, d), F32),
                   jax.ShapeDtypeStruct((b, s, d), BF16),
                   jax.ShapeDtypeStruct((SC_PARTS, b, s, SC_ROW), U32),
                   jax.ShapeDtypeStruct((b, s, LANE), F32)],
        compiler_params=_cparams(("parallel", "parallel")),
        name="merge",
    )(o_na, o_sc, o_f, o_b, main, main, main, main, x, gt1, gn_t, w_na, w_sc, w_gla, w_out, g2_t,
      sh2, sc2, w_router)


def _router_kernel(lg_ref, br_ref, tri_ref, eidx_ref, gw_ref, rank_ref, cnt_ref, carry_ref):
    tm = lg_ref.shape[0]

    @pl.when(pl.program_id(0) == 0)
    def _():
        carry_ref[...] = jnp.zeros_like(carry_ref)

    scores = _sigmoid(lg_ref[...].T[:N_EXPERTS])
    sel = scores + br_ref[...]
    neg = -jnp.inf

    sel3 = sel.reshape(N_EXPERT_GROUPS, GROUP_SIZE, tm)
    i3 = lax.broadcasted_iota(jnp.int32, sel3.shape, 1)
    m1 = sel3.max(axis=1, keepdims=True)
    first = jnp.where(sel3 == m1, i3, GROUP_SIZE).min(axis=1, keepdims=True)
    m2 = jnp.where(i3 == first, neg, sel3).max(axis=1, keepdims=True)
    gscore = (m1 + m2)[:, 0, :]

    gi = lax.broadcasted_iota(jnp.int32, gscore.shape, 0)
    gmask = jnp.zeros(gscore.shape, jnp.bool_)
    for _ in range(TOPK_GROUPS):
        m = gscore.max(axis=0, keepdims=True)
        pick = gi == jnp.where(gscore == m, gi, N_EXPERT_GROUPS).min(axis=0, keepdims=True)
        gmask = gmask | pick
        gscore = jnp.where(pick, neg, gscore)
    emask = jnp.broadcast_to(gmask[:, None, :], sel3.shape).reshape(N_EXPERTS, tm)
    sel = jnp.where(emask, sel, neg)

    ei = lax.broadcasted_iota(jnp.int32, sel.shape, 0)
    picks, idxs, ws = [], [], []
    for _ in range(TOP_K):
        m = sel.max(axis=0, keepdims=True)
        idx = jnp.where(sel == m, ei, N_EXPERTS).min(axis=0, keepdims=True)
        pick = ei == idx
        picks.append(pick)
        idxs.append(idx)
        ws.append(jnp.where(pick, scores, 0.0).sum(axis=0, keepdims=True))
        sel = jnp.where(pick, neg, sel)
    w = jnp.concatenate(ws, axis=0)
    gw_ref[...] = w / w.sum(axis=0, keepdims=True) * ROUTED_SCALE
    eidx_ref[...] = jnp.concatenate(idxs, axis=0)

    onehot = picks[0]
    for p in picks[1:]:
        onehot = onehot | p
    onehot = jnp.where(onehot, 1.0, 0.0).astype(BF16)
    before = _dot(onehot, tri_ref[...]) + jnp.tile(carry_ref[...], (1, tm // LANE))
    rank_ref[...] = jnp.concatenate(
        [jnp.where(p, before, 0.0).sum(axis=0, keepdims=True) for p in picks], axis=0).astype(jnp.int32)
    carry_ref[...] += _dot(onehot, jnp.ones((tm, LANE), BF16))
    cnt_ref[...] = carry_ref[...]


def _route(logits, b_router, tm=512):
    t = logits.shape[0]
    br = jnp.broadcast_to(b_router.astype(F32)[:, None], (N_EXPERTS, tm))
    tri = jnp.asarray(np.triu(np.ones((tm, tm), np.float32), 1), dtype=BF16)
    kt = lambda dt: jax.ShapeDtypeStruct((TOP_K, t), dt)
    eidx, gw, rank, cnt = pl.pallas_call(
        _router_kernel,
        grid=(t // tm,),
        in_specs=[pl.BlockSpec((tm, LANE), lambda i: (i, 0)),
                  pl.BlockSpec((N_EXPERTS, tm), lambda i: (0, 0)),
                  pl.BlockSpec((tm, tm), lambda i: (0, 0))],
        out_specs=[pl.BlockSpec((TOP_K, tm), lambda i: (0, i)),
                   pl.BlockSpec((TOP_K, tm), lambda i: (0, i)),
                   pl.BlockSpec((TOP_K, tm), lambda i: (0, i)),
                   pl.BlockSpec((N_EXPERTS, LANE), lambda i: (0, 0))],
        out_shape=[kt(jnp.int32), kt(F32), kt(jnp.int32),
                   jax.ShapeDtypeStruct((N_EXPERTS, LANE), F32)],
        scratch_shapes=[pltpu.VMEM((N_EXPERTS, LANE), F32)],
        compiler_params=_cparams(("arbitrary",)),
        name="router",
    )(logits, br, tri)
    return eidx, gw, rank, cnt[:, 0].astype(jnp.int32)


def _sc_mesh():
    return plsc.VectorSubcoreMesh(core_axis_name="core", subcore_axis_name="subcore")


def _dispatch_rows(xp, dest, slots):
    parts, t, _ = xp.shape
    nwin = parts * t // SC_WINDOW
    idx = dest.reshape(TOP_K, t // SC_WINDOW, SC_WINDOW).transpose(1, 0, 2)
    idx = jnp.concatenate([idx + part * slots for part in range(parts)], axis=0)

    @pl.kernel(out_type=jax.ShapeDtypeStruct((parts * slots, SC_ROW), xp.dtype), mesh=_sc_mesh(),
               scratch_types=[], name="moe_dispatch")
    def run(x_hbm, i_hbm, o_hbm):
        def body(x_vmem, i_vmem):
            for k in range(TOP_K):
                pltpu.sync_copy(x_vmem, o_hbm.at[i_vmem.at[0, k]])

        pltpu.emit_pipeline(
            body,
            grid=(nwin,),
            in_specs=[pl.BlockSpec((SC_WINDOW, SC_ROW), lambda i: (i, 0)),
                      pl.BlockSpec((1, TOP_K, SC_WINDOW), lambda i: (i, 0, 0))],
            out_specs=[],
            core_axis_name=("core", "subcore"),
            dimension_semantics=(pltpu.PARALLEL,),
        )(x_hbm, i_hbm)

    return run(xp.reshape(parts * t, SC_ROW), idx).reshape(parts, slots, SC_ROW)


def _gather_rows(yp, idx):
    n = idx.shape[0]
    parts, slots, _ = yp.shape
    idx = jnp.concatenate([idx + part * slots for part in range(parts)]).reshape(1, n * parts)

    @pl.kernel(out_type=jax.ShapeDtypeStruct((n * parts, SC_ROW), yp.dtype), mesh=_sc_mesh(),
               scratch_types=[], name="moe_gather")
    def run(y_hbm, i_hbm, o_hbm):
        def body(i_vmem, o_vmem):
            pltpu.sync_copy(y_hbm.at[i_vmem.at[0]], o_vmem)

        pltpu.emit_pipeline(
            body,
            grid=(n * parts // SC_WINDOW,),
            in_specs=[pl.BlockSpec((1, SC_WINDOW), lambda i: (0, i))],
            out_specs=[pl.BlockSpec((SC_WINDOW, SC_ROW), lambda i: (i, 0))],
            core_axis_name=("core", "subcore"),
            dimension_semantics=(pltpu.PARALLEL,),
        )(i_hbm, o_hbm)

    return run(yp.reshape(parts * slots, SC_ROW), idx).reshape(parts, n, SC_ROW)


def _expert_kernel(be_ref, bv_ref, x_ref, wg_ref, wu_ref, wd_ref, o_ref):
    del be_ref
    valid = bv_ref[pl.program_id(0)]

    @pl.when(valid > 0)
    def _():
        w = _load_parts(x_ref)
        row = lax.broadcasted_iota(jnp.int32, w.shape, 0)
        w = jnp.where(row < valid, w, jnp.uint32(0))
        lo, hi = _unpack_rows(w)
        x = jnp.concatenate([lo, hi], axis=1).astype(BF16)
        a = _dot(x, wg_ref[0])
        hid = a * _sigmoid(a) * _dot(x, wu_ref[0])
        _store_parts(o_ref, _pack_rows(_dot(hid.astype(BF16), wd_ref[0])))


def _experts(xs, blk_e, blk_valid, w_gate, w_up, w_down):
    parts, slots, _ = xs.shape
    d = D_MODEL
    nb = slots // MOE_BLOCK
    return pl.pallas_call(
        _expert_kernel,
        grid_spec=pltpu.PrefetchScalarGridSpec(
            num_scalar_prefetch=2,
            grid=(nb,),
            in_specs=[pl.BlockSpec((parts, MOE_BLOCK, SC_ROW), lambda i, be, bv: (0, i, 0)),
                      pl.BlockSpec((1, d, EXPERT_FF), lambda i, be, bv: (be[i], 0, 0)),
                      pl.BlockSpec((1, d, EXPERT_FF), lambda i, be, bv: (be[i], 0, 0)),
                      pl.BlockSpec((1, EXPERT_FF, d), lambda i, be, bv: (be[i], 0, 0))],
            out_specs=pl.BlockSpec((parts, MOE_BLOCK, SC_ROW), lambda i, be, bv: (0, i, 0))),
        out_shape=jax.ShapeDtypeStruct((parts, slots, SC_ROW), U32),
        compiler_params=_cparams(("arbitrary",)),
        name="experts",
    )(blk_e, blk_valid, xs, w_gate, w_up, w_down)


def _combine_kernel(yg_ref, gw_ref, h_ref, x_ref, gt_ref, wsg_ref, wsu_ref, wsd_ref, gf_ref, o_ref, *, final):
    h = h_ref[0]
    a = _dot(h, wsg_ref[...])
    hid = a * _sigmoid(a) * _dot(h, wsu_ref[...])
    y = _dot(hid.astype(BF16), wsd_ref[...])
    gw = gw_ref[...]
    y_lo = y[:, :D_MODEL // 2]
    y_hi = y[:, D_MODEL // 2:]
    for k in range(TOP_K):
        lo, hi = _unpack_rows(_load_parts(yg_ref, k))
        y_lo = y_lo + gw[:, k:k + 1] * lo
        y_hi = y_hi + gw[:, k:k + 1] * hi
    y = jnp.concatenate([y_lo, y_hi], axis=1)
    xn = x_ref[0] + gt_ref[0] * y
    if final:
        ms = jnp.mean(xn * xn, axis=-1, keepdims=True)
        xn = xn * lax.rsqrt(ms + RMS_EPS) * gf_ref[...]
    o_ref[0] = xn


def _combine(yg, gw, tok_off, h2, x, gt2, ws_gate, ws_up, ws_down, g_final, final, tm):
    b, s, d = x.shape
    per_batch = gt2.shape[0] == b
    mod_map = (lambda bi, i: (bi, 0, 0)) if per_batch else (lambda bi, i: (0, 0, 0))
    full = lambda arr: pl.BlockSpec(arr.shape, lambda bi, i: (0,) * arr.ndim)
    gf = g_final[None]
    blk0 = tok_off // tm
    nblk = s // tm
    return pl.pallas_call(
        functools.partial(_combine_kernel, final=final),
        grid=(b, s // tm),
        in_specs=[pl.BlockSpec((SC_PARTS, TOP_K, tm, SC_ROW), lambda bi, i: (0, 0, blk0 + bi * nblk + i, 0)),
                  pl.BlockSpec((tm, TOP_K), lambda bi, i: (blk0 + bi * nblk + i, 0)),
                  pl.BlockSpec((1, tm, d), lambda bi, i: (bi, i, 0)),
                  pl.BlockSpec((1, tm, d), lambda bi, i: (bi, i, 0)),
                  pl.BlockSpec((1, 1, d), mod_map),
                  full(ws_gate), full(ws_up), full(ws_down), full(gf)],
        out_specs=pl.BlockSpec((1, tm, d), lambda bi, i: (bi, i, 0)),
        out_shape=jax.ShapeDtypeStruct((b, s, d), F32),
        compiler_params=_cparams(("parallel", "parallel")),
        name="combine",
    )(yg, gw, h2, x, gt2, ws_gate, ws_up, ws_down, gf)


def _reorder_w_in(w_in):
    k_na, v_na, k_gla = w_in[:, 0:512], w_in[:, 512:1024], w_in[:, 1024:1536]
    v_gla, gate = w_in[:, 1536:2560], w_in[:, 2560:2592]
    q_na = w_in[:, 2592:3104] * (NA_HEAD_DIM ** -0.5)
    rest = w_in[:, 3104:]
    main = jnp.concatenate([v_gla, k_na, v_na, k_gla, q_na, rest], axis=1).astype(BF16)
    gate = jnp.pad(gate, ((0, 0), (0, LANE - 2 * GLA_GATE_RANK))).astype(BF16)
    return main, gate


def _layer(x, ctx_s, mods, mods_ctx, p, ctx_out, final, g_final):
    b, s, d = x.shape
    sc = ctx_s.shape[1]
    sh1, sc1, gt1, sh2, sc2, gt2 = mods
    csh1, csc1, cgt1, csh2, csc2, cgt2 = mods_ctx

    w_main, w_gate = _reorder_w_in(p['w_in'])
    main, gate = _proj_in(x, p['g_norm1'], sh1, sc1, w_main, w_gate, tm=min(1024, s), tn=1024)
    if ctx_out:
        main_c, gate_c = _proj_in(ctx_s, p['g_norm1'], csh1, csc1, w_main, w_gate, tm=sc, tn=1024)
    else:
        main_c, gate_c = _proj_in(ctx_s, p['g_norm1'], csh1, csc1, w_main[:, :N_KV_MAIN], w_gate,
                                  tm=sc, tn=N_KV_MAIN // 2)

    o_na = _na_latent(main, main_c, p['na_rpb'])
    o_sc = _short_conv(main, p['conv_w'])

    w2, b2 = _gla_gate_weights(p['gla_gate_w'], p['gla_gate_b'])
    s0 = jnp.zeros((b, GLA_HEADS, GLA_DK, GLA_DV), F32)
    o_cf, st_f = _gla_scan(main_c, gate_c, w2[0], b2[0], s0, False, ctx_out)
    o_cb, st_b = _gla_scan(main_c, gate_c, w2[1], b2[1], s0, True, ctx_out)
    o_f, _ = _gla_scan(main, gate, w2[0], b2[0], st_f, False, True)
    o_b, _ = _gla_scan(main, gate, w2[1], b2[1], st_b, True, True)

    w_na = p['w_branch_na'].astype(BF16)
    w_sc = p['w_branch_sc'].astype(BF16)
    w_gla = p['w_branch_gla'].astype(BF16)
    w_out = p['w_out'].astype(BF16)
    w_router = jnp.pad(p['w_router'], ((0, 0), (0, LANE - N_EXPERTS))).astype(BF16)
    x, h2, h2p, logits = _merge(o_na, o_sc, o_f, o_b, main, x, gt1, p['gla_norm_g'], w_na, w_sc, w_gla, w_out,
                                p['g_norm2'], sh2, sc2, w_router, tm=min(256, s))
    n_lat = b * s
    hp_all = h2p.reshape(SC_PARTS, n_lat, SC_ROW)
    lg_all = logits.reshape(n_lat, LANE)
    if ctx_out:
        o_na_c = _dense_attn(main_c)
        o_sc_c = _short_conv(main_c, p['conv_w'])
        ctx_s, h2_c, h2p_c, lg_c = _merge(o_na_c, o_sc_c, o_cf, o_cb, main_c, ctx_s, cgt1, p['gla_norm_g'],
                                          w_na, w_sc, w_gla, w_out, p['g_norm2'], csh2, csc2, w_router,
                                          tm=min(256, sc))
        hp_all = jnp.concatenate([hp_all, h2p_c.reshape(SC_PARTS, b * sc, SC_ROW)], axis=1)
        lg_all = jnp.concatenate([lg_all, lg_c.reshape(b * sc, LANE)], axis=0)

    t = hp_all.shape[1]
    eidx, gw, rank, counts = _route(lg_all, p['b_router'])
    padded = (counts + MOE_BLOCK - 1) // MOE_BLOCK * MOE_BLOCK
    pad_end = jnp.cumsum(padded)
    pad_start = pad_end - padded
    onehot = eidx[:, :, None] == jnp.arange(N_EXPERTS, dtype=jnp.int32)
    dest = jnp.sum(jnp.where(onehot, pad_start, 0), axis=-1) + rank
    n_blocks = -(-(t * TOP_K + N_EXPERTS * (MOE_BLOCK - 1)) // MOE_BLOCK)
    slots = n_blocks * MOE_BLOCK
    blk_start = jnp.arange(n_blocks, dtype=jnp.int32) * MOE_BLOCK
    blk_e = jnp.minimum(jnp.sum(pad_end[None, :] <= blk_start[:, None], axis=1), N_EXPERTS - 1).astype(jnp.int32)
    used_end = (pad_start + counts)[blk_e]
    blk_valid = jnp.clip(used_end - blk_start, 0, MOE_BLOCK).astype(jnp.int32)

    xs = _dispatch_rows(hp_all, dest, slots)
    ys = _experts(xs, blk_e, blk_valid, p['w_exp_gate'].astype(BF16), p['w_exp_up'].astype(BF16),
                  p['w_exp_down'].astype(BF16))
    yg = _gather_rows(ys, dest.reshape(-1)).reshape(SC_PARTS, TOP_K, t, SC_ROW)
    gw_t = gw.T

    ws_gate = p['w_sh_gate'].astype(BF16)
    ws_up = p['w_sh_up'].astype(BF16)
    ws_down = p['w_sh_down'].astype(BF16)
    x = _combine(yg, gw_t, 0, h2, x, gt2, ws_gate, ws_up, ws_down, g_final, final, tm=min(256, s))
    if ctx_out:
        ctx_s = _combine(yg, gw_t, n_lat, h2_c, ctx_s, cgt2, ws_gate, ws_up, ws_down, g_final, False,
                         tm=min(256, sc))
    return x, ctx_s


def kernel(x, c, ctx, c_ctx, w_mod, b_mod, g_norm1, g_norm2, w_in, na_rpb, w_branch_na, conv_w, w_branch_sc,
           gla_gate_w, gla_gate_b, gla_norm_g, w_branch_gla, w_out, w_router, b_router, w_exp_gate, w_exp_up,
           w_exp_down, w_sh_gate, w_sh_up, w_sh_down, g_final):
    stacked = dict(g_norm1=g_norm1, g_norm2=g_norm2, w_in=w_in, na_rpb=na_rpb, w_branch_na=w_branch_na,
                   conv_w=conv_w, w_branch_sc=w_branch_sc, gla_gate_w=gla_gate_w, gla_gate_b=gla_gate_b,
                   gla_norm_g=gla_norm_g, w_branch_gla=w_branch_gla, w_out=w_out, w_router=w_router,
                   b_router=b_router, w_exp_gate=w_exp_gate, w_exp_up=w_exp_up, w_exp_down=w_exp_down,
                   w_sh_gate=w_sh_gate, w_sh_up=w_sh_up, w_sh_down=w_sh_down)
    depth = w_in.shape[0]
    ctx_s = ctx
    for i in range(depth):
        p = {name: arr[i] for name, arr in stacked.items()}
        mods, mods_ctx = _mod_vectors(c, c_ctx, w_mod[i], b_mod[i])
        last = i == depth - 1
        x, ctx_s = _layer(x, ctx_s, mods, mods_ctx, p, not last, last, g_final)
    return x
```

```python
import functools

import numpy as np
import jax
import jax.numpy as jnp
from jax import lax
from jax.experimental import pallas as pl
from jax.experimental.pallas import tpu as pltpu
from jax.experimental.pallas import tpu_sc as plsc

F32 = jnp.float32
BF16 = jnp.bfloat16
U32 = jnp.uint32

D_MODEL = 1024
N_MOD = 6
RMS_EPS = 1e-6
NEG_INF = -1e30
GRID_W = 64
NA_HEADS = 8
NA_HEAD_DIM = 64
NA_WIDTH = NA_HEADS * NA_HEAD_DIM
NA_WIN_R = 8
NA_WIN_C = 16
NA_GROUP = 4
SC_WIDTH = 512
GLA_HEADS = 4
GLA_KEY_WIDTH = 512
GLA_VAL_WIDTH = 1024
GLA_DK = GLA_KEY_WIDTH // GLA_HEADS
GLA_DV = GLA_VAL_WIDTH // GLA_HEADS
GLA_GATE_RANK = 16
GLA_GATE_TAU = 16.0
N_EXPERTS = 64
N_EXPERT_GROUPS = 8
GROUP_SIZE = N_EXPERTS // N_EXPERT_GROUPS
TOPK_GROUPS = 4
TOP_K = 8
EXPERT_FF = 256
ROUTED_SCALE = 2.5
MOE_BLOCK = 512

LANE = 128
GLA_C = 128
GLA_LEVELS = tuple(GLA_C >> (i + 1) for i in range(GLA_C.bit_length() - 1))
VMEM_LIMIT = 48 * 1024 * 1024
SC_WINDOW = 128
SC_ROW = 256
SC_PARTS = D_MODEL // 2 // SC_ROW

OFF_V_GLA = 0
OFF_K_NA = 1024
OFF_V_NA = 1536
OFF_K_GLA = 2048
N_KV_MAIN = 2560
OFF_Q_NA = 2560
OFF_B_SC = 3072
OFF_C_SC = 3584
OFF_X_SC = 4096
OFF_Q_GLA = 4608
OFF_R_GLA = 5120
OFF_MERGE = 6144
N_MAIN = 9216


def _cparams(sem, vmem=VMEM_LIMIT):
    return pltpu.CompilerParams(dimension_semantics=sem, vmem_limit_bytes=vmem)


def _dot(a, b):
    return jnp.dot(a, b, preferred_element_type=F32)


def _dot_nt(a, b):
    return lax.dot_general(a, b, (((1,), (1,)), ((), ())), preferred_element_type=F32)


def _dot_tn(a, b):
    return lax.dot_general(a, b, (((0,), (0,)), ((), ())), preferred_element_type=F32)


def _sigmoid(x):
    return 1.0 / (1.0 + jnp.exp(-x))


def _pack_rows(x):
    n = x.shape[1] // 2
    r = x.astype(BF16).astype(F32)
    lo = pltpu.bitcast(r[:, :n], U32) >> 16
    hi = pltpu.bitcast(r[:, n:], U32)
    return hi | lo


def _store_parts(ref, words):
    for part in range(SC_PARTS):
        dst = ref.at[part, 0] if len(ref.shape) == 4 else ref.at[part]
        dst[...] = words[:, part * SC_ROW:(part + 1) * SC_ROW]


def _load_parts(ref, *lead):
    return jnp.concatenate([ref[(part,) + lead] for part in range(SC_PARTS)], axis=-1)


def _unpack_rows(w):
    lo = pltpu.bitcast(w << 16, F32)
    hi = pltpu.bitcast(w & jnp.uint32(0xFFFF0000), F32)
    return lo, hi


def _mod_kernel(a_ref, w_ref, b_ref, o_ref):
    a = a_ref[...]
    a = a * _sigmoid(a)
    o_ref[...] = _dot(a.astype(BF16), w_ref[...].astype(BF16)) + b_ref[...]


def _mod_vectors(c, c_ctx, w_mod, b_mod):
    b = c.shape[0]
    rows = -(-(b + 1) // 8) * 8
    a = jnp.concatenate([c, c_ctx[None], jnp.zeros((rows - b - 1, D_MODEL), F32)], axis=0)
    n = N_MOD * D_MODEL
    tn = 1536
    out = pl.pallas_call(
        _mod_kernel,
        grid=(n // tn,),
        in_specs=[pl.BlockSpec((rows, D_MODEL), lambda j: (0, 0)),
                  pl.BlockSpec((D_MODEL, tn), lambda j: (0, j)),
                  pl.BlockSpec((1, tn), lambda j: (0, j))],
        out_specs=pl.BlockSpec((rows, tn), lambda j: (0, j)),
        out_shape=jax.ShapeDtypeStruct((rows, n), F32),
        compiler_params=_cparams(("parallel",)),
        name="mod_vectors",
    )(a, w_mod, b_mod[None])
    lat = out[:b].reshape(b, N_MOD, 1, D_MODEL)
    ctx = out[b].reshape(N_MOD, 1, 1, D_MODEL)
    return [lat[:, i] for i in range(N_MOD)], [ctx[i] for i in range(N_MOD)]


def _proj_kernel(x_ref, g_ref, sh_ref, sc_ref, w_ref, wg_ref, o_ref, og_ref, h_ref):
    @pl.when(pl.program_id(2) == 0)
    def _():
        x = x_ref[0]
        ms = jnp.mean(x * x, axis=-1, keepdims=True)
        h = x * lax.rsqrt(ms + RMS_EPS) * g_ref[...] * (1.0 + sc_ref[0]) + sh_ref[0]
        hb = h.astype(BF16)
        h_ref[...] = hb
        og_ref[0] = _dot(hb, wg_ref[...])

    o_ref[0] = _dot(h_ref[...], w_ref[...]).astype(o_ref.dtype)


def _proj_in(x, g, shift, scale, w_main, w_gate, tm, tn):
    b, s, d = x.shape
    n = w_main.shape[1]
    per_batch = shift.shape[0] == b
    mod_map = (lambda bi, i, j: (bi, 0, 0)) if per_batch else (lambda bi, i, j: (0, 0, 0))
    return pl.pallas_call(
        _proj_kernel,
        grid=(b, s // tm, n // tn),
        in_specs=[pl.BlockSpec((1, tm, d), lambda bi, i, j: (bi, i, 0)),
                  pl.BlockSpec((1, d), lambda bi, i, j: (0, 0)),
                  pl.BlockSpec((1, 1, d), mod_map),
                  pl.BlockSpec((1, 1, d), mod_map),
                  pl.BlockSpec((d, tn), lambda bi, i, j: (0, j)),
                  pl.BlockSpec((d, LANE), lambda bi, i, j: (0, 0))],
        out_specs=[pl.BlockSpec((1, tm, tn), lambda bi, i, j: (bi, i, j)),
                   pl.BlockSpec((1, tm, LANE), lambda bi, i, j: (bi, i, 0))],
        out_shape=[jax.ShapeDtypeStruct((b, s, n), BF16),
                   jax.ShapeDtypeStruct((b, s, LANE), F32)],
        scratch_shapes=[pltpu.VMEM((tm, d), BF16)],
        compiler_params=_cparams(("parallel", "parallel", "arbitrary")),
        name="proj_in",
    )(x, g[None], shift, scale, w_main, w_gate)


def _softmax_av(q, keys, vals, biases):
    scores = []
    for kk, bb in zip(keys, biases):
        s = _dot_nt(q, kk)
        scores.append(s if bb is None else s + bb)
    m = scores[0].max(axis=-1, keepdims=True)
    for s in scores[1:]:
        m = jnp.maximum(m, s.max(axis=-1, keepdims=True))
    num = None
    den = None
    for s, vv in zip(scores, vals):
        e = jnp.exp(s - m)
        dsum = e.sum(axis=-1, keepdims=True)
        o = _dot(e.astype(BF16), vv)
        num = o if num is None else num + o
        den = dsum if den is None else den + dsum
    return num / den


def _na_kernel(q_ref, k_ref, v_ref, kc_ref, vc_ref, bias_ref, o_ref, *, rows, kr):
    r = pl.program_id(1)
    row_start = jnp.clip(r - kr // 2, 0, rows - kr)
    start = pl.multiple_of(row_start * GRID_W, GRID_W)
    n_win = kr * GRID_W
    q = q_ref[0]
    kw = k_ref[0, pl.ds(start, n_win), :]
    vw = v_ref[0, pl.ds(start, n_win), :]
    kc = kc_ref[0]
    vc = vc_ref[0]
    gw = NA_GROUP * NA_HEAD_DIM
    stacked = (NA_GROUP * GRID_W, gw)
    on_head = (lax.broadcasted_iota(jnp.int32, stacked, 0) // GRID_W
               == lax.broadcasted_iota(jnp.int32, stacked, 1) // NA_HEAD_DIM)
    outs = []
    for g in range(NA_HEADS // NA_GROUP):
        sl = slice(g * gw, (g + 1) * gw)
        q_all = jnp.where(on_head, jnp.concatenate([q[:, sl]] * NA_GROUP, axis=0), jnp.zeros((), q.dtype))
        bias = bias_ref[0, g * NA_GROUP * GRID_W:(g + 1) * NA_GROUP * GRID_W, :]
        o_all = _softmax_av(q_all, [kw[:, sl], kc[:, sl]], [vw[:, sl], vc[:, sl]], [bias, None])
        o_all = jnp.where(on_head, o_all, 0.0).reshape(NA_GROUP, GRID_W, gw)
        outs.append(o_all.sum(axis=0))
    o_ref[0] = jnp.concatenate(outs, axis=-1).astype(o_ref.dtype)


def _na_bias_table(rpb, rows, kr):
    col = np.arange(GRID_W)
    col_start = np.clip(col - NA_WIN_C // 2, 0, GRID_W - NA_WIN_C)
    col_ok = (col[None, :] >= col_start[:, None]) & (col[None, :] < col_start[:, None] + NA_WIN_C)
    d_col = np.clip(col[None, :] - col[:, None], -(NA_WIN_C - 1), NA_WIN_C - 1) + NA_WIN_C - 1
    n_dr, n_dc = rpb.shape[1], rpb.shape[2]
    onehot = jnp.asarray((d_col.reshape(-1)[None, :] == np.arange(n_dc)[:, None]).astype(np.float32))
    by_col = jnp.dot(rpb.astype(F32).reshape(NA_HEADS * n_dr, n_dc), onehot, precision=lax.Precision.HIGHEST)
    by_col = by_col.reshape(NA_HEADS, n_dr, GRID_W, GRID_W)
    by_col = jnp.where(col_ok[None, None], by_col, NEG_INF)
    tables = []
    for o in range(kr):
        lo = NA_WIN_R - 1 - o
        tables.append(by_col[:, lo:lo + kr].transpose(0, 2, 1, 3).reshape(NA_HEADS, GRID_W, kr * GRID_W))
    return jnp.stack(tables).reshape(kr, NA_HEADS * GRID_W, kr * GRID_W)


def _na_latent(main, main_ctx, rpb):
    b, s, _ = main.shape
    sc = main_ctx.shape[1]
    rows = s // GRID_W
    kr = min(NA_WIN_R, rows)
    bias = _na_bias_table(rpb, rows, kr)
    w = NA_WIDTH

    def bias_map(bi, r):
        return (r - jnp.clip(r - kr // 2, 0, rows - kr), 0, 0)

    return pl.pallas_call(
        functools.partial(_na_kernel, rows=rows, kr=kr),
        grid=(b, rows),
        in_specs=[pl.BlockSpec((1, GRID_W, w), lambda bi, r: (bi, r, OFF_Q_NA // w)),
                  pl.BlockSpec((1, s, w), lambda bi, r: (bi, 0, OFF_K_NA // w)),
                  pl.BlockSpec((1, s, w), lambda bi, r: (bi, 0, OFF_V_NA // w)),
                  pl.BlockSpec((1, sc, w), lambda bi, r: (bi, 0, OFF_K_NA // w)),
                  pl.BlockSpec((1, sc, w), lambda bi, r: (bi, 0, OFF_V_NA // w)),
                  pl.BlockSpec((1, NA_HEADS * GRID_W, kr * GRID_W), bias_map)],
        out_specs=pl.BlockSpec((1, GRID_W, w), lambda bi, r: (bi, r, 0)),
        out_shape=jax.ShapeDtypeStruct((b, s, w), BF16),
        compiler_params=_cparams(("parallel", "arbitrary")),
        name="na_latent",
    )(main, main, main, main_ctx, main_ctx, bias)


def _dense_attn_kernel(q_ref, k_ref, v_ref, o_ref):
    q = q_ref[0]
    k = k_ref[0]
    v = v_ref[0]
    outs = []
    for h in range(NA_HEADS):
        sl = slice(h * NA_HEAD_DIM, (h + 1) * NA_HEAD_DIM)
        outs.append(_softmax_av(q[:, sl], [k[:, sl]], [v[:, sl]], [None]))
    o_ref[0] = jnp.concatenate(outs, axis=-1).astype(o_ref.dtype)


def _dense_attn(main_ctx):
    b, sc, _ = main_ctx.shape
    w = NA_WIDTH
    return pl.pallas_call(
        _dense_attn_kernel,
        grid=(b,),
        in_specs=[pl.BlockSpec((1, sc, w), lambda bi: (bi, 0, OFF_Q_NA // w)),
                  pl.BlockSpec((1, sc, w), lambda bi: (bi, 0, OFF_K_NA // w)),
                  pl.BlockSpec((1, sc, w), lambda bi: (bi, 0, OFF_V_NA // w))],
        out_specs=pl.BlockSpec((1, sc, w), lambda bi: (bi, 0, 0)),
        out_shape=jax.ShapeDtypeStruct((b, sc, w), BF16),
        compiler_params=_cparams(("parallel",)),
        name="ctx_attn",
    )(main_ctx, main_ctx, main_ctx)


def _conv_kernel(b_ref, c_ref, x_ref, w_ref, o_ref):
    u = c_ref[0].astype(F32) * x_ref[0].astype(F32)
    s = u.shape[0]
    t = lax.broadcasted_iota(jnp.int32, u.shape, 0)
    prev = jnp.where(t == 0, 0.0, pltpu.roll(u, 1, axis=0))
    nxt = jnp.where(t == s - 1, 0.0, pltpu.roll(u, s - 1, axis=0))
    w = w_ref[...]
    y = b_ref[0].astype(F32) * (prev * w[0:1] + u * w[1:2] + nxt * w[2:3])
    o_ref[0] = y.astype(o_ref.dtype)


def _short_conv(main, conv_w):
    b, s, _ = main.shape
    nt = SC_WIDTH // LANE
    return pl.pallas_call(
        _conv_kernel,
        grid=(b, nt),
        in_specs=[pl.BlockSpec((1, s, LANE), lambda bi, c: (bi, 0, OFF_B_SC // LANE + c)),
                  pl.BlockSpec((1, s, LANE), lambda bi, c: (bi, 0, OFF_C_SC // LANE + c)),
                  pl.BlockSpec((1, s, LANE), lambda bi, c: (bi, 0, OFF_X_SC // LANE + c)),
                  pl.BlockSpec((3, LANE), lambda bi, c: (0, c))],
        out_specs=pl.BlockSpec((1, s, LANE), lambda bi, c: (bi, 0, c)),
        out_shape=jax.ShapeDtypeStruct((b, s, SC_WIDTH), BF16),
        compiler_params=_cparams(("parallel", "parallel")),
        name="short_conv",
    )(main, main, main, conv_w)


def _gla_matrices(reverse):
    c = GLA_C
    t = np.arange(c)[:, None]
    m = np.arange(c)[None, :]
    blocks = [m <= t, m > t]
    for b in GLA_LEVELS:
        first = (t // (2 * b)) * (2 * b) + b
        is_q = (t & b) != 0
        blocks.append(np.where(is_q, (m > first) & (m <= t), (m > t) & (m <= first)))
    mats = np.stack(blocks).astype(np.float32)
    if reverse:
        mats = mats[:, ::-1, ::-1]
    return jnp.asarray(mats.reshape(-1, c), dtype=BF16)


def _gla_kernel(*refs, reverse, emit):
    if emit:
        q_ref, k_ref, v_ref, gt_ref, w2_ref, b2_ref, a_ref, s0_ref, o_ref, sf_ref, st_ref = refs
    else:
        k_ref, v_ref, gt_ref, w2_ref, b2_ref, a_ref, s0_ref, sf_ref, st_ref = refs
    c = GLA_C
    step = pl.program_id(1)

    @pl.when(step == 0)
    def _():
        st_ref[...] = s0_ref[0]

    logit = jnp.dot(gt_ref[0], w2_ref[...], precision=lax.Precision.HIGHEST,
                    preferred_element_type=F32) + b2_ref[...]
    g = (jnp.minimum(logit, 0.0) - jnp.log1p(jnp.exp(-jnp.abs(logit)))) * (1.0 / GLA_GATE_TAU)
    g_hi = g.astype(BF16)
    g_lo = (g - g_hi.astype(F32)).astype(BF16)
    amat = a_ref[...]
    args = _dot(amat, g_hi) + _dot(amat, g_lo)
    cum = args[0:c]
    rem = args[c:2 * c]
    last_row = cum[0:1] if reverse else cum[c - 1:c]

    k = k_ref[0].astype(F32)
    v = v_ref[0]
    if emit:
        q = q_ref[0].astype(F32) * (GLA_DK ** -0.5)
        ti = lax.broadcasted_iota(jnp.int32, (c, c), 0)
        si = lax.broadcasted_iota(jnp.int32, (c, c), 1)
        tq = lax.broadcasted_iota(jnp.int32, (c, GLA_DK), 0)
        if reverse:
            ti, si, tq = c - 1 - ti, c - 1 - si, c - 1 - tq
        diag = ti == si
        pair_masks = [(((ti ^ si) >> (b.bit_length() - 1)) == 1) & ((ti & b) != 0) for b in GLA_LEVELS]
        q_rows = [(tq & b) != 0 for b in GLA_LEVELS]

    outs = []
    for h in range(GLA_HEADS):
        sl = slice(h * GLA_DK, (h + 1) * GLA_DK)
        kh = k[:, sl]
        vh = v[:, h * GLA_DV:(h + 1) * GLA_DV]
        state = st_ref[h]
        if emit:
            qh = q[:, sl]
            qd = (qh * jnp.exp(cum[:, sl])).astype(BF16)
            o = _dot(qd, state.astype(BF16))
            att = jnp.where(diag, _dot_nt(qh.astype(BF16), kh.astype(BF16)), 0.0)
            for l in range(len(GLA_LEVELS)):
                e = jnp.exp(args[(2 + l) * c:(3 + l) * c, sl])
                scaled = e * jnp.where(q_rows[l], qh, kh)
                zq = jnp.where(q_rows[l], scaled, 0.0).astype(BF16)
                zk = jnp.where(q_rows[l], 0.0, scaled).astype(BF16)
                att = jnp.where(pair_masks[l], _dot_nt(zq, zk), att)
            outs.append(o + _dot(att.astype(BF16), vh))
        kd = (kh * jnp.exp(rem[:, sl])).astype(BF16)
        decay = jnp.exp(jnp.broadcast_to(last_row[:, sl], (GLA_DK, GLA_DK))).T
        decay = jnp.concatenate([decay] * (GLA_DV // GLA_DK), axis=1)
        st_ref[h] = decay * state + _dot_tn(kd, vh)

    if emit:
        o_ref[0] = jnp.concatenate(outs, axis=-1)

    @pl.when(step == pl.num_programs(1) - 1)
    def _():
        sf_ref[0] = st_ref[...]


def _gla_scan(main, gate, w2, b2, s0, reverse, emit):
    b, l, _ = main.shape
    n = l // GLA_C
    amat = _gla_matrices(reverse)
    chunk = (lambda bi, s: (bi, n - 1 - s)) if reverse else (lambda bi, s: (bi, s))

    def col(block):
        return lambda bi, s: chunk(bi, s) + (block,)

    in_specs = []
    args = []
    if emit:
        in_specs.append(pl.BlockSpec((1, GLA_C, GLA_KEY_WIDTH), col(OFF_Q_GLA // GLA_KEY_WIDTH)))
        args.append(main)
    in_specs += [pl.BlockSpec((1, GLA_C, GLA_KEY_WIDTH), col(OFF_K_GLA // GLA_KEY_WIDTH)),
                 pl.BlockSpec((1, GLA_C, GLA_VAL_WIDTH), col(OFF_V_GLA // GLA_VAL_WIDTH)),
                 pl.BlockSpec((1, GLA_C, LANE), col(0)),
                 pl.BlockSpec((LANE, GLA_KEY_WIDTH), lambda bi, s: (0, 0)),
                 pl.BlockSpec((1, GLA_KEY_WIDTH), lambda bi, s: (0, 0)),
                 pl.BlockSpec(amat.shape, lambda bi, s: (0, 0)),
                 pl.BlockSpec((1, GLA_HEADS, GLA_DK, GLA_DV), lambda bi, s: (bi, 0, 0, 0))]
    args += [main, main, gate, w2, b2, amat, s0]
    state_spec = pl.BlockSpec((1, GLA_HEADS, GLA_DK, GLA_DV), lambda bi, s: (bi, 0, 0, 0))
    state_shape = jax.ShapeDtypeStruct((b, GLA_HEADS, GLA_DK, GLA_DV), F32)
    if emit:
        out_specs = [pl.BlockSpec((1, GLA_C, GLA_VAL_WIDTH), col(0)), state_spec]
        out_shape = [jax.ShapeDtypeStruct((b, l, GLA_VAL_WIDTH), F32), state_shape]
    else:
        out_specs = [state_spec]
        out_shape = [state_shape]
    res = pl.pallas_call(
        functools.partial(_gla_kernel, reverse=reverse, emit=emit),
        grid=(b, n),
        in_specs=in_specs,
        out_specs=out_specs,
        out_shape=out_shape,
        scratch_shapes=[pltpu.VMEM((GLA_HEADS, GLA_DK, GLA_DV), F32)],
        compiler_params=_cparams(("parallel", "arbitrary")),
        name="gla_scan",
    )(*args)
    return (res[0], res[1]) if emit else (None, res[0])


def _gla_gate_weights(gate_w, gate_b):
    w2, b2 = [], []
    for dr in range(2):
        w = jnp.zeros((LANE, GLA_KEY_WIDTH), F32)
        w = w.at[dr * GLA_GATE_RANK:(dr + 1) * GLA_GATE_RANK].set(gate_w[dr])
        w2.append(w)
        b2.append(gate_b[dr][None])
    return w2, b2


def _merge_kernel(ona_ref, osc_ref, of_ref, ob_ref, r_ref, gna_ref, gsc_ref, ggl_ref, x_ref, gt_ref,
                  gn_ref, wna_ref, wsc_ref, wgl_ref, wo_ref, g2_ref, sh2_ref, sc2_ref, wr_ref,
                  xo_ref, h2_ref, hp_ref, lg_ref):
    o = of_ref[0] + ob_ref[0]
    normed = []
    for h in range(GLA_HEADS):
        oh = o[:, h * GLA_DV:(h + 1) * GLA_DV]
        ms = jnp.mean(oh * oh, axis=-1, keepdims=True)
        normed.append(oh * lax.rsqrt(ms + RMS_EPS))
    r = r_ref[0].astype(F32)
    y_gla = jnp.concatenate(normed, axis=-1) * gn_ref[...] * (r * _sigmoid(r))
    y = (_sigmoid(gna_ref[0].astype(F32)) * _dot(ona_ref[0], wna_ref[...])
         + _sigmoid(gsc_ref[0].astype(F32)) * _dot(osc_ref[0], wsc_ref[...])
         + _sigmoid(ggl_ref[0].astype(F32)) * _dot(y_gla.astype(BF16), wgl_ref[...]))
    xn = x_ref[0] + gt_ref[0] * _dot(y.astype(BF16), wo_ref[...])
    xo_ref[0] = xn
    ms = jnp.mean(xn * xn, axis=-1, keepdims=True)
    h2 = xn * lax.rsqrt(ms + RMS_EPS) * g2_ref[...] * (1.0 + sc2_ref[0]) + sh2_ref[0]
    h2b = h2.astype(BF16)
    h2_ref[0] = h2b
    _store_parts(hp_ref, _pack_rows(h2))
    lg_ref[0] = _dot(h2b, wr_ref[...])


def _merge(o_na, o_sc, o_f, o_b, main, x, gt1, gn, w_na, w_sc, w_gla, w_out, g2, sh2, sc2, w_router, tm):
    b, s, d = x.shape
    per_batch = gt1.shape[0] == b
    mod_map = (lambda bi, i: (bi, 0, 0)) if per_batch else (lambda bi, i: (0, 0, 0))
    tok = lambda width, blk: pl.BlockSpec((1, tm, width), lambda bi, i: (bi, i, blk))
    full = lambda arr: pl.BlockSpec(arr.shape, lambda bi, i: (0,) * arr.ndim)
    mod = pl.BlockSpec((1, 1, d), mod_map)
    gn_t = jnp.tile(gn, GLA_HEADS)[None]
    g2_t = g2[None]
    return pl.pallas_call(
        _merge_kernel,
        grid=(b, s // tm),
        in_specs=[tok(NA_WIDTH, 0), tok(SC_WIDTH, 0), tok(GLA_VAL_WIDTH, 0), tok(GLA_VAL_WIDTH, 0),
                  tok(d, OFF_R_GLA // d), tok(d, OFF_MERGE // d), tok(d, OFF_MERGE // d + 1),
                  tok(d, OFF_MERGE // d + 2), tok(d, 0), mod,
                  full(gn_t), full(w_na), full(w_sc), full(w_gla), full(w_out), full(g2_t), mod, mod,
                  full(w_router)],
        out_specs=[tok(d, 0), tok(d, 0),
                   pl.BlockSpec((SC_PARTS, 1, tm, SC_ROW), lambda bi, i: (0, bi, i, 0)), tok(LANE, 0)],
        out_shape=[jax.ShapeDtypeStruct((b, s, d), F32),
                   jax.ShapeDtypeStruct((b, s, d), BF16),
                   jax.ShapeDtypeStruct((SC_PARTS, b, s, SC_ROW), U32),
                   jax.ShapeDtypeStruct((b, s, LANE), F32)],
        compiler_params=_cparams(("parallel", "parallel")),
        name="merge",
    )(o_na, o_sc, o_f, o_b, main, main, main, main, x, gt1, gn_t, w_na, w_sc, w_gla, w_out, g2_t,
      sh2, sc2, w_router)


def _router_kernel(lg_ref, br_ref, tri_ref, eidx_ref, gw_ref, rank_ref, cnt_ref, carry_ref):
    tm = lg_ref.shape[0]

    @pl.when(pl.program_id(0) == 0)
    def _():
        carry_ref[...] = jnp.zeros_like(carry_ref)

    scores = _sigmoid(lg_ref[...].T[:N_EXPERTS])
    sel = scores + br_ref[...]
    neg = -jnp.inf

    sel3 = sel.reshape(N_EXPERT_GROUPS, GROUP_SIZE, tm)
    i3 = lax.broadcasted_iota(jnp.int32, sel3.shape, 1)
    m1 = sel3.max(axis=1, keepdims=True)
    first = jnp.where(sel3 == m1, i3, GROUP_SIZE).min(axis=1, keepdims=True)
    m2 = jnp.where(i3 == first, neg, sel3).max(axis=1, keepdims=True)
    gscore = (m1 + m2)[:, 0, :]

    gi = lax.broadcasted_iota(jnp.int32, gscore.shape, 0)
    gmask = jnp.zeros(gscore.shape, jnp.bool_)
    for _ in range(TOPK_GROUPS):
        m = gscore.max(axis=0, keepdims=True)
        pick = gi == jnp.where(gscore == m, gi, N_EXPERT_GROUPS).min(axis=0, keepdims=True)
        gmask = gmask | pick
        gscore = jnp.where(pick, neg, gscore)
    emask = jnp.broadcast_to(gmask[:, None, :], sel3.shape).reshape(N_EXPERTS, tm)
    sel = jnp.where(emask, sel, neg)

    ei = lax.broadcasted_iota(jnp.int32, sel.shape, 0)
    picks, idxs, ws = [], [], []
    for _ in range(TOP_K):
        m = sel.max(axis=0, keepdims=True)
        idx = jnp.where(sel == m, ei, N_EXPERTS).min(axis=0, keepdims=True)
        pick = ei == idx
        picks.append(pick)
        idxs.append(idx)
        ws.append(jnp.where(pick, scores, 0.0).sum(axis=0, keepdims=True))
        sel = jnp.where(pick, neg, sel)
    w = jnp.concatenate(ws, axis=0)
    gw_ref[...] = w / w.sum(axis=0, keepdims=True) * ROUTED_SCALE
    eidx_ref[...] = jnp.concatenate(idxs, axis=0)

    onehot = picks[0]
    for p in picks[1:]:
        onehot = onehot | p
    onehot = jnp.where(onehot, 1.0, 0.0).astype(BF16)
    before = _dot(onehot, tri_ref[...]) + jnp.tile(carry_ref[...], (1, tm // LANE))
    rank_ref[...] = jnp.concatenate(
        [jnp.where(p, before, 0.0).sum(axis=0, keepdims=True) for p in picks], axis=0).astype(jnp.int32)
    carry_ref[...] += _dot(onehot, jnp.ones((tm, LANE), BF16))
    cnt_ref[...] = carry_ref[...]


def _route(logits, b_router, tm=512):
    t = logits.shape[0]
    br = jnp.broadcast_to(b_router.astype(F32)[:, None], (N_EXPERTS, tm))
    tri = jnp.asarray(np.triu(np.ones((tm, tm), np.float32), 1), dtype=BF16)
    kt = lambda dt: jax.ShapeDtypeStruct((TOP_K, t), dt)
    eidx, gw, rank, cnt = pl.pallas_call(
        _router_kernel,
        grid=(t // tm,),
        in_specs=[pl.BlockSpec((tm, LANE), lambda i: (i, 0)),
                  pl.BlockSpec((N_EXPERTS, tm), lambda i: (0, 0)),
                  pl.BlockSpec((tm, tm), lambda i: (0, 0))],
        out_specs=[pl.BlockSpec((TOP_K, tm), lambda i: (0, i)),
                   pl.BlockSpec((TOP_K, tm), lambda i: (0, i)),
                   pl.BlockSpec((TOP_K, tm), lambda i: (0, i)),
                   pl.BlockSpec((N_EXPERTS, LANE), lambda i: (0, 0))],
        out_shape=[kt(jnp.int32), kt(F32), kt(jnp.int32),
                   jax.ShapeDtypeStruct((N_EXPERTS, LANE), F32)],
        scratch_shapes=[pltpu.VMEM((N_EXPERTS, LANE), F32)],
        compiler_params=_cparams(("arbitrary",)),
        name="router",
    )(logits, br, tri)
    return eidx, gw, rank, cnt[:, 0].astype(jnp.int32)


def _sc_mesh():
    return plsc.VectorSubcoreMesh(core_axis_name="core", subcore_axis_name="subcore")


def _dispatch_rows(xp, dest, slots):
    parts, t, _ = xp.shape
    nwin = parts * t // SC_WINDOW
    idx = dest.reshape(TOP_K, t // SC_WINDOW, SC_WINDOW).transpose(1, 0, 2)
    idx = jnp.concatenate([idx + part * slots for part in range(parts)], axis=0)

    @pl.kernel(out_type=jax.ShapeDtypeStruct((parts * slots, SC_ROW), xp.dtype), mesh=_sc_mesh(),
               scratch_types=[], name="moe_dispatch")
    def run(x_hbm, i_hbm, o_hbm):
        def body(x_vmem, i_vmem):
            for k in range(TOP_K):
                pltpu.sync_copy(x_vmem, o_hbm.at[i_vmem.at[0, k]])

        pltpu.emit_pipeline(
            body,
            grid=(nwin,),
            in_specs=[pl.BlockSpec((SC_WINDOW, SC_ROW), lambda i: (i, 0)),
                      pl.BlockSpec((1, TOP_K, SC_WINDOW), lambda i: (i, 0, 0))],
            out_specs=[],
            core_axis_name=("core", "subcore"),
            dimension_semantics=(pltpu.PARALLEL,),
        )(x_hbm, i_hbm)

    return run(xp.reshape(parts * t, SC_ROW), idx).reshape(parts, slots, SC_ROW)


def _gather_rows(yp, idx):
    n = idx.shape[0]
    parts, slots, _ = yp.shape
    idx = jnp.concatenate([idx + part * slots for part in range(parts)]).reshape(1, n * parts)

    @pl.kernel(out_type=jax.ShapeDtypeStruct((n * parts, SC_ROW), yp.dtype), mesh=_sc_mesh(),
               scratch_types=[], name="moe_gather")
    def run(y_hbm, i_hbm, o_hbm):
        def body(i_vmem, o_vmem):
            pltpu.sync_copy(y_hbm.at[i_vmem.at[0]], o_vmem)

        pltpu.emit_pipeline(
            body,
            grid=(n * parts // SC_WINDOW,),
            in_specs=[pl.BlockSpec((1, SC_WINDOW), lambda i: (0, i))],
            out_specs=[pl.BlockSpec((SC_WINDOW, SC_ROW), lambda i: (i, 0))],
            core_axis_name=("core", "subcore"),
            dimension_semantics=(pltpu.PARALLEL,),
        )(i_hbm, o_hbm)

    return run(yp.reshape(parts * slots, SC_ROW), idx).reshape(parts, n, SC_ROW)


def _expert_kernel(be_ref, bv_ref, x_ref, wg_ref, wu_ref, wd_ref, o_ref, wg_s, wu_s, wd_s):
    i = pl.program_id(0)
    valid = bv_ref[i]
    new_expert = (i == 0) | (be_ref[i] != be_ref[jnp.maximum(i - 1, 0)])

    @pl.when(new_expert)
    def _():
        wg_s[...] = wg_ref[0, 0].astype(BF16)
        wu_s[...] = wu_ref[0, 0].astype(BF16)
        wd_s[...] = wd_ref[0, 0].astype(BF16)

    @pl.when(valid > 0)
    def _():
        w = _load_parts(x_ref)
        row = lax.broadcasted_iota(jnp.int32, w.shape, 0)
        w = jnp.where(row < valid, w, jnp.uint32(0))
        lo, hi = _unpack_rows(w)
        x = jnp.concatenate([lo, hi], axis=1).astype(BF16)
        a = _dot(x, wg_s[...])
        hid = a * _sigmoid(a) * _dot(x, wu_s[...])
        _store_parts(o_ref, _pack_rows(_dot(hid.astype(BF16), wd_s[...])))


def _experts(xs, blk_e, blk_valid, layer, w_gate, w_up, w_down):
    parts, slots, _ = xs.shape
    d = D_MODEL
    nb = slots // MOE_BLOCK
    return pl.pallas_call(
        _expert_kernel,
        grid_spec=pltpu.PrefetchScalarGridSpec(
            num_scalar_prefetch=2,
            grid=(nb,),
            in_specs=[pl.BlockSpec((parts, MOE_BLOCK, SC_ROW), lambda i, be, bv: (0, i, 0)),
                      pl.BlockSpec((1, 1, d, EXPERT_FF), lambda i, be, bv: (layer, be[i], 0, 0)),
                      pl.BlockSpec((1, 1, d, EXPERT_FF), lambda i, be, bv: (layer, be[i], 0, 0)),
                      pl.BlockSpec((1, 1, EXPERT_FF, d), lambda i, be, bv: (layer, be[i], 0, 0))],
            out_specs=pl.BlockSpec((parts, MOE_BLOCK, SC_ROW), lambda i, be, bv: (0, i, 0)),
            scratch_shapes=[pltpu.VMEM((d, EXPERT_FF), BF16), pltpu.VMEM((d, EXPERT_FF), BF16),
                            pltpu.VMEM((EXPERT_FF, d), BF16)]),
        out_shape=jax.ShapeDtypeStruct((parts, slots, SC_ROW), U32),
        compiler_params=_cparams(("arbitrary",)),
        name="experts",
    )(blk_e, blk_valid, xs, w_gate, w_up, w_down)


def _combine_kernel(yg_ref, gw_ref, h_ref, x_ref, gt_ref, wsg_ref, wsu_ref, wsd_ref, gf_ref, o_ref, *, final):
    h = h_ref[0]
    a = _dot(h, wsg_ref[...])
    hid = a * _sigmoid(a) * _dot(h, wsu_ref[...])
    y = _dot(hid.astype(BF16), wsd_ref[...])
    gw = gw_ref[...]
    y_lo = y[:, :D_MODEL // 2]
    y_hi = y[:, D_MODEL // 2:]
    for k in range(TOP_K):
        lo, hi = _unpack_rows(_load_parts(yg_ref, k))
        y_lo = y_lo + gw[:, k:k + 1] * lo
        y_hi = y_hi + gw[:, k:k + 1] * hi
    y = jnp.concatenate([y_lo, y_hi], axis=1)
    xn = x_ref[0] + gt_ref[0] * y
    if final:
        ms = jnp.mean(xn * xn, axis=-1, keepdims=True)
        xn = xn * lax.rsqrt(ms + RMS_EPS) * gf_ref[...]
    o_ref[0] = xn


def _combine(yg, gw, tok_off, h2, x, gt2, ws_gate, ws_up, ws_down, g_final, final, tm):
    b, s, d = x.shape
    per_batch = gt2.shape[0] == b
    mod_map = (lambda bi, i: (bi, 0, 0)) if per_batch else (lambda bi, i: (0, 0, 0))
    full = lambda arr: pl.BlockSpec(arr.shape, lambda bi, i: (0,) * arr.ndim)
    gf = g_final[None]
    blk0 = tok_off // tm
    nblk = s // tm
    return pl.pallas_call(
        functools.partial(_combine_kernel, final=final),
        grid=(b, s // tm),
        in_specs=[pl.BlockSpec((SC_PARTS, TOP_K, tm, SC_ROW), lambda bi, i: (0, 0, blk0 + bi * nblk + i, 0)),
                  pl.BlockSpec((tm, TOP_K), lambda bi, i: (blk0 + bi * nblk + i, 0)),
                  pl.BlockSpec((1, tm, d), lambda bi, i: (bi, i, 0)),
                  pl.BlockSpec((1, tm, d), lambda bi, i: (bi, i, 0)),
                  pl.BlockSpec((1, 1, d), mod_map),
                  full(ws_gate), full(ws_up), full(ws_down), full(gf)],
        out_specs=pl.BlockSpec((1, tm, d), lambda bi, i: (bi, i, 0)),
        out_shape=jax.ShapeDtypeStruct((b, s, d), F32),
        compiler_params=_cparams(("parallel", "parallel")),
        name="combine",
    )(yg, gw, h2, x, gt2, ws_gate, ws_up, ws_down, gf)


def _reorder_w_in(w_in):
    k_na, v_na, k_gla = w_in[:, 0:512], w_in[:, 512:1024], w_in[:, 1024:1536]
    v_gla, gate = w_in[:, 1536:2560], w_in[:, 2560:2592]
    q_na = w_in[:, 2592:3104] * (NA_HEAD_DIM ** -0.5)
    rest = w_in[:, 3104:]
    main = jnp.concatenate([v_gla, k_na, v_na, k_gla, q_na, rest], axis=1).astype(BF16)
    gate = jnp.pad(gate, ((0, 0), (0, LANE - 2 * GLA_GATE_RANK))).astype(BF16)
    return main, gate


def _layer(x, ctx_s, mods, mods_ctx, p, ctx_out, final, g_final):
    b, s, d = x.shape
    sc = ctx_s.shape[1]
    sh1, sc1, gt1, sh2, sc2, gt2 = mods
    csh1, csc1, cgt1, csh2, csc2, cgt2 = mods_ctx

    w_main, w_gate = _reorder_w_in(p['w_in'])
    main, gate = _proj_in(x, p['g_norm1'], sh1, sc1, w_main, w_gate, tm=min(1024, s), tn=1024)
    if ctx_out:
        main_c, gate_c = _proj_in(ctx_s, p['g_norm1'], csh1, csc1, w_main, w_gate, tm=sc, tn=1024)
    else:
        main_c, gate_c = _proj_in(ctx_s, p['g_norm1'], csh1, csc1, w_main[:, :N_KV_MAIN], w_gate,
                                  tm=sc, tn=N_KV_MAIN // 2)

    o_na = _na_latent(main, main_c, p['na_rpb'])
    o_sc = _short_conv(main, p['conv_w'])

    w2, b2 = _gla_gate_weights(p['gla_gate_w'], p['gla_gate_b'])
    s0 = jnp.zeros((b, GLA_HEADS, GLA_DK, GLA_DV), F32)
    o_cf, st_f = _gla_scan(main_c, gate_c, w2[0], b2[0], s0, False, ctx_out)
    o_cb, st_b = _gla_scan(main_c, gate_c, w2[1], b2[1], s0, True, ctx_out)
    o_f, _ = _gla_scan(main, gate, w2[0], b2[0], st_f, False, True)
    o_b, _ = _gla_scan(main, gate, w2[1], b2[1], st_b, True, True)

    w_na = p['w_branch_na'].astype(BF16)
    w_sc = p['w_branch_sc'].astype(BF16)
    w_gla = p['w_branch_gla'].astype(BF16)
    w_out = p['w_out'].astype(BF16)
    w_router = jnp.pad(p['w_router'], ((0, 0), (0, LANE - N_EXPERTS))).astype(BF16)
    x, h2, h2p, logits = _merge(o_na, o_sc, o_f, o_b, main, x, gt1, p['gla_norm_g'], w_na, w_sc, w_gla, w_out,
                                p['g_norm2'], sh2, sc2, w_router, tm=min(256, s))
    n_lat = b * s
    hp_all = h2p.reshape(SC_PARTS, n_lat, SC_ROW)
    lg_all = logits.reshape(n_lat, LANE)
    if ctx_out:
        o_na_c = _dense_attn(main_c)
        o_sc_c = _short_conv(main_c, p['conv_w'])
        ctx_s, h2_c, h2p_c, lg_c = _merge(o_na_c, o_sc_c, o_cf, o_cb, main_c, ctx_s, cgt1, p['gla_norm_g'],
                                          w_na, w_sc, w_gla, w_out, p['g_norm2'], csh2, csc2, w_router,
                                          tm=min(256, sc))
        hp_all = jnp.concatenate([hp_all, h2p_c.reshape(SC_PARTS, b * sc, SC_ROW)], axis=1)
        lg_all = jnp.concatenate([lg_all, lg_c.reshape(b * sc, LANE)], axis=0)

    t = hp_all.shape[1]
    eidx, gw, rank, counts = _route(lg_all, p['b_router'])
    padded = (counts + MOE_BLOCK - 1) // MOE_BLOCK * MOE_BLOCK
    pad_end = jnp.cumsum(padded)
    pad_start = pad_end - padded
    onehot = eidx[:, :, None] == jnp.arange(N_EXPERTS, dtype=jnp.int32)
    dest = jnp.sum(jnp.where(onehot, pad_start, 0), axis=-1) + rank
    n_blocks = -(-(t * TOP_K + N_EXPERTS * (MOE_BLOCK - 1)) // MOE_BLOCK)
    slots = n_blocks * MOE_BLOCK
    blk_start = jnp.arange(n_blocks, dtype=jnp.int32) * MOE_BLOCK
    blk_e = jnp.minimum(jnp.sum(pad_end[None, :] <= blk_start[:, None], axis=1), N_EXPERTS - 1).astype(jnp.int32)
    used_end = (pad_start + counts)[blk_e]
    blk_valid = jnp.clip(used_end - blk_start, 0, MOE_BLOCK).astype(jnp.int32)

    xs = _dispatch_rows(hp_all, dest, slots)
    ys = _experts(xs, blk_e, blk_valid, p['layer'], p['w_exp_gate'], p['w_exp_up'], p['w_exp_down'])
    yg = _gather_rows(ys, dest.reshape(-1)).reshape(SC_PARTS, TOP_K, t, SC_ROW)
    gw_t = gw.T

    ws_gate = p['w_sh_gate'].astype(BF16)
    ws_up = p['w_sh_up'].astype(BF16)
    ws_down = p['w_sh_down'].astype(BF16)
    x = _combine(yg, gw_t, 0, h2, x, gt2, ws_gate, ws_up, ws_down, g_final, final, tm=min(256, s))
    if ctx_out:
        ctx_s = _combine(yg, gw_t, n_lat, h2_c, ctx_s, cgt2, ws_gate, ws_up, ws_down, g_final, False,
                         tm=min(256, sc))
    return x, ctx_s


def kernel(x, c, ctx, c_ctx, w_mod, b_mod, g_norm1, g_norm2, w_in, na_rpb, w_branch_na, conv_w, w_branch_sc,
           gla_gate_w, gla_gate_b, gla_norm_g, w_branch_gla, w_out, w_router, b_router, w_exp_gate, w_exp_up,
           w_exp_down, w_sh_gate, w_sh_up, w_sh_down, g_final):
    stacked = dict(g_norm1=g_norm1, g_norm2=g_norm2, w_in=w_in, na_rpb=na_rpb, w_branch_na=w_branch_na,
                   conv_w=conv_w, w_branch_sc=w_branch_sc, gla_gate_w=gla_gate_w, gla_gate_b=gla_gate_b,
                   gla_norm_g=gla_norm_g, w_branch_gla=w_branch_gla, w_out=w_out, w_router=w_router,
                   b_router=b_router,
                   w_sh_gate=w_sh_gate, w_sh_up=w_sh_up, w_sh_down=w_sh_down)
    depth = w_in.shape[0]
    ctx_s = ctx
    for i in range(depth):
        p = {name: arr[i] for name, arr in stacked.items()}
        p.update(layer=i, w_exp_gate=w_exp_gate, w_exp_up=w_exp_up, w_exp_down=w_exp_down)
        mods, mods_ctx = _mod_vectors(c, c_ctx, w_mod[i], b_mod[i])
        last = i == depth - 1
        x, ctx_s = _layer(x, ctx_s, mods, mods_ctx, p, not last, last, g_final)
    return x
```

```python
import functools

import numpy as np
import jax
import jax.numpy as jnp
from jax import lax
from jax.experimental import pallas as pl
from jax.experimental.pallas import tpu as pltpu
from jax.experimental.pallas import tpu_sc as plsc

F32 = jnp.float32
BF16 = jnp.bfloat16
U32 = jnp.uint32

D_MODEL = 1024
N_MOD = 6
RMS_EPS = 1e-6
NEG_INF = -1e30
GRID_W = 64
NA_HEADS = 8
NA_HEAD_DIM = 64
NA_WIDTH = NA_HEADS * NA_HEAD_DIM
NA_WIN_R = 8
NA_WIN_C = 16
NA_GROUP = 4
SC_WIDTH = 512
GLA_HEADS = 4
GLA_KEY_WIDTH = 512
GLA_VAL_WIDTH = 1024
GLA_DK = GLA_KEY_WIDTH // GLA_HEADS
GLA_DV = GLA_VAL_WIDTH // GLA_HEADS
GLA_GATE_RANK = 16
GLA_GATE_TAU = 16.0
N_EXPERTS = 64
N_EXPERT_GROUPS = 8
GROUP_SIZE = N_EXPERTS // N_EXPERT_GROUPS
TOPK_GROUPS = 4
TOP_K = 8
EXPERT_FF = 256
ROUTED_SCALE = 2.5
MOE_BLOCK = 512

LANE = 128
GLA_C = 128
GLA_LEVELS = tuple(GLA_C >> (i + 1) for i in range(GLA_C.bit_length() - 1))
VMEM_LIMIT = 48 * 1024 * 1024
SC_WINDOW = 128
SC_ROW = 256
SC_PARTS = D_MODEL // 2 // SC_ROW

OFF_V_GLA = 0
OFF_K_NA = 1024
OFF_V_NA = 1536
OFF_K_GLA = 2048
N_KV_MAIN = 2560
OFF_Q_NA = 2560
OFF_B_SC = 3072
OFF_C_SC = 3584
OFF_X_SC = 4096
OFF_Q_GLA = 4608
OFF_R_GLA = 5120
OFF_MERGE = 6144
N_MAIN = 9216


def _cparams(sem, vmem=VMEM_LIMIT):
    return pltpu.CompilerParams(dimension_semantics=sem, vmem_limit_bytes=vmem)


def _dot(a, b):
    return jnp.dot(a, b, preferred_element_type=F32)


def _dot_nt(a, b):
    return lax.dot_general(a, b, (((1,), (1,)), ((), ())), preferred_element_type=F32)


def _dot_tn(a, b):
    return lax.dot_general(a, b, (((0,), (0,)), ((), ())), preferred_element_type=F32)


def _sigmoid(x):
    return 1.0 / (1.0 + jnp.exp(-x))


def _pack_rows(x):
    n = x.shape[1] // 2
    r = x.astype(BF16).astype(F32)
    lo = pltpu.bitcast(r[:, :n], U32) >> 16
    hi = pltpu.bitcast(r[:, n:], U32)
    return hi | lo


def _store_parts(ref, words):
    for part in range(SC_PARTS):
        dst = ref.at[part, 0] if len(ref.shape) == 4 else ref.at[part]
        dst[...] = words[:, part * SC_ROW:(part + 1) * SC_ROW]


def _load_parts(ref, *lead):
    return jnp.concatenate([ref[(part,) + lead] for part in range(SC_PARTS)], axis=-1)


def _unpack_rows(w):
    lo = pltpu.bitcast(w << 16, F32)
    hi = pltpu.bitcast(w & jnp.uint32(0xFFFF0000), F32)
    return lo, hi


def _mod_kernel(a_ref, w_ref, b_ref, o_ref):
    a = a_ref[...]
    a = a * _sigmoid(a)
    o_ref[...] = _dot(a.astype(BF16), w_ref[...].astype(BF16)) + b_ref[...]


def _mod_vectors(c, c_ctx, w_mod, b_mod):
    b = c.shape[0]
    rows = -(-(b + 1) // 8) * 8
    a = jnp.concatenate([c, c_ctx[None], jnp.zeros((rows - b - 1, D_MODEL), F32)], axis=0)
    n = N_MOD * D_MODEL
    tn = 1536
    out = pl.pallas_call(
        _mod_kernel,
        grid=(n // tn,),
        in_specs=[pl.BlockSpec((rows, D_MODEL), lambda j: (0, 0)),
                  pl.BlockSpec((D_MODEL, tn), lambda j: (0, j)),
                  pl.BlockSpec((1, tn), lambda j: (0, j))],
        out_specs=pl.BlockSpec((rows, tn), lambda j: (0, j)),
        out_shape=jax.ShapeDtypeStruct((rows, n), F32),
        compiler_params=_cparams(("parallel",)),
        name="mod_vectors",
    )(a, w_mod, b_mod[None])
    lat = out[:b].reshape(b, N_MOD, 1, D_MODEL)
    ctx = out[b].reshape(N_MOD, 1, 1, D_MODEL)
    return [lat[:, i] for i in range(N_MOD)], [ctx[i] for i in range(N_MOD)]


def _proj_kernel(x_ref, g_ref, sh_ref, sc_ref, w_ref, wg_ref, o_ref, og_ref, h_ref):
    @pl.when(pl.program_id(2) == 0)
    def _():
        x = x_ref[0]
        ms = jnp.mean(x * x, axis=-1, keepdims=True)
        h = x * lax.rsqrt(ms + RMS_EPS) * g_ref[...] * (1.0 + sc_ref[0]) + sh_ref[0]
        hb = h.astype(BF16)
        h_ref[...] = hb
        og_ref[0] = _dot(hb, wg_ref[...])

    o_ref[0] = _dot(h_ref[...], w_ref[...]).astype(o_ref.dtype)


def _proj_in(x, g, shift, scale, w_main, w_gate, tm, tn):
    b, s, d = x.shape
    n = w_main.shape[1]
    per_batch = shift.shape[0] == b
    mod_map = (lambda bi, i, j: (bi, 0, 0)) if per_batch else (lambda bi, i, j: (0, 0, 0))
    return pl.pallas_call(
        _proj_kernel,
        grid=(b, s // tm, n // tn),
        in_specs=[pl.BlockSpec((1, tm, d), lambda bi, i, j: (bi, i, 0)),
                  pl.BlockSpec((1, d), lambda bi, i, j: (0, 0)),
                  pl.BlockSpec((1, 1, d), mod_map),
                  pl.BlockSpec((1, 1, d), mod_map),
                  pl.BlockSpec((d, tn), lambda bi, i, j: (0, j)),
                  pl.BlockSpec((d, LANE), lambda bi, i, j: (0, 0))],
        out_specs=[pl.BlockSpec((1, tm, tn), lambda bi, i, j: (bi, i, j)),
                   pl.BlockSpec((1, tm, LANE), lambda bi, i, j: (bi, i, 0))],
        out_shape=[jax.ShapeDtypeStruct((b, s, n), BF16),
                   jax.ShapeDtypeStruct((b, s, LANE), F32)],
        scratch_shapes=[pltpu.VMEM((tm, d), BF16)],
        compiler_params=_cparams(("parallel", "parallel", "arbitrary")),
        name="proj_in",
    )(x, g[None], shift, scale, w_main, w_gate)


def _softmax_av(q, keys, vals, biases):
    scores = []
    for kk, bb in zip(keys, biases):
        s = _dot_nt(q, kk)
        scores.append(s if bb is None else s + bb)
    m = scores[0].max(axis=-1, keepdims=True)
    for s in scores[1:]:
        m = jnp.maximum(m, s.max(axis=-1, keepdims=True))
    num = None
    den = None
    for s, vv in zip(scores, vals):
        e = jnp.exp(s - m)
        dsum = e.sum(axis=-1, keepdims=True)
        o = _dot(e.astype(BF16), vv)
        num = o if num is None else num + o
        den = dsum if den is None else den + dsum
    return num / den


def _na_kernel(q_ref, k_ref, v_ref, kc_ref, vc_ref, bias_ref, o_ref, *, rows, kr):
    r = pl.program_id(1)
    row_start = jnp.clip(r - kr // 2, 0, rows - kr)
    start = pl.multiple_of(row_start * GRID_W, GRID_W)
    n_win = kr * GRID_W
    q = q_ref[0]
    kw = k_ref[0, pl.ds(start, n_win), :]
    vw = v_ref[0, pl.ds(start, n_win), :]
    kc = kc_ref[0]
    vc = vc_ref[0]
    gw = NA_GROUP * NA_HEAD_DIM
    stacked = (NA_GROUP * GRID_W, gw)
    on_head = (lax.broadcasted_iota(jnp.int32, stacked, 0) // GRID_W
               == lax.broadcasted_iota(jnp.int32, stacked, 1) // NA_HEAD_DIM)
    outs = []
    for g in range(NA_HEADS // NA_GROUP):
        sl = slice(g * gw, (g + 1) * gw)
        q_all = jnp.where(on_head, jnp.concatenate([q[:, sl]] * NA_GROUP, axis=0), jnp.zeros((), q.dtype))
        bias = bias_ref[0, g * NA_GROUP * GRID_W:(g + 1) * NA_GROUP * GRID_W, :]
        o_all = _softmax_av(q_all, [kw[:, sl], kc[:, sl]], [vw[:, sl], vc[:, sl]], [bias, None])
        o_all = jnp.where(on_head, o_all, 0.0).reshape(NA_GROUP, GRID_W, gw)
        outs.append(o_all.sum(axis=0))
    o_ref[0] = jnp.concatenate(outs, axis=-1).astype(o_ref.dtype)


def _na_bias_table(rpb, rows, kr):
    col = np.arange(GRID_W)
    col_start = np.clip(col - NA_WIN_C // 2, 0, GRID_W - NA_WIN_C)
    col_ok = (col[None, :] >= col_start[:, None]) & (col[None, :] < col_start[:, None] + NA_WIN_C)
    d_col = np.clip(col[None, :] - col[:, None], -(NA_WIN_C - 1), NA_WIN_C - 1) + NA_WIN_C - 1
    n_dr, n_dc = rpb.shape[1], rpb.shape[2]
    onehot = jnp.asarray((d_col.reshape(-1)[None, :] == np.arange(n_dc)[:, None]).astype(np.float32))
    by_col = jnp.dot(rpb.astype(F32).reshape(NA_HEADS * n_dr, n_dc), onehot, precision=lax.Precision.HIGHEST)
    by_col = by_col.reshape(NA_HEADS, n_dr, GRID_W, GRID_W)
    by_col = jnp.where(col_ok[None, None], by_col, NEG_INF)
    tables = []
    for o in range(kr):
        lo = NA_WIN_R - 1 - o
        tables.append(by_col[:, lo:lo + kr].transpose(0, 2, 1, 3).reshape(NA_HEADS, GRID_W, kr * GRID_W))
    return jnp.stack(tables).reshape(kr, NA_HEADS * GRID_W, kr * GRID_W)


def _na_latent(main, main_ctx, rpb):
    b, s, _ = main.shape
    sc = main_ctx.shape[1]
    rows = s // GRID_W
    kr = min(NA_WIN_R, rows)
    bias = _na_bias_table(rpb, rows, kr)
    w = NA_WIDTH

    def bias_map(bi, r):
        return (r - jnp.clip(r - kr // 2, 0, rows - kr), 0, 0)

    return pl.pallas_call(
        functools.partial(_na_kernel, rows=rows, kr=kr),
        grid=(b, rows),
        in_specs=[pl.BlockSpec((1, GRID_W, w), lambda bi, r: (bi, r, OFF_Q_NA // w)),
                  pl.BlockSpec((1, s, w), lambda bi, r: (bi, 0, OFF_K_NA // w)),
                  pl.BlockSpec((1, s, w), lambda bi, r: (bi, 0, OFF_V_NA // w)),
                  pl.BlockSpec((1, sc, w), lambda bi, r: (bi, 0, OFF_K_NA // w)),
                  pl.BlockSpec((1, sc, w), lambda bi, r: (bi, 0, OFF_V_NA // w)),
                  pl.BlockSpec((1, NA_HEADS * GRID_W, kr * GRID_W), bias_map)],
        out_specs=pl.BlockSpec((1, GRID_W, w), lambda bi, r: (bi, r, 0)),
        out_shape=jax.ShapeDtypeStruct((b, s, w), BF16),
        compiler_params=_cparams(("parallel", "arbitrary")),
        name="na_latent",
    )(main, main, main, main_ctx, main_ctx, bias)


def _dense_attn_kernel(q_ref, k_ref, v_ref, o_ref):
    q = q_ref[0]
    k = k_ref[0]
    v = v_ref[0]
    outs = []
    for h in range(NA_HEADS):
        sl = slice(h * NA_HEAD_DIM, (h + 1) * NA_HEAD_DIM)
        outs.append(_softmax_av(q[:, sl], [k[:, sl]], [v[:, sl]], [None]))
    o_ref[0] = jnp.concatenate(outs, axis=-1).astype(o_ref.dtype)


def _dense_attn(main_ctx):
    b, sc, _ = main_ctx.shape
    w = NA_WIDTH
    return pl.pallas_call(
        _dense_attn_kernel,
        grid=(b,),
        in_specs=[pl.BlockSpec((1, sc, w), lambda bi: (bi, 0, OFF_Q_NA // w)),
                  pl.BlockSpec((1, sc, w), lambda bi: (bi, 0, OFF_K_NA // w)),
                  pl.BlockSpec((1, sc, w), lambda bi: (bi, 0, OFF_V_NA // w))],
        out_specs=pl.BlockSpec((1, sc, w), lambda bi: (bi, 0, 0)),
        out_shape=jax.ShapeDtypeStruct((b, sc, w), BF16),
        compiler_params=_cparams(("parallel",)),
        name="ctx_attn",
    )(main_ctx, main_ctx, main_ctx)


def _conv_kernel(b_ref, c_ref, x_ref, w_ref, o_ref):
    u = c_ref[0].astype(F32) * x_ref[0].astype(F32)
    s = u.shape[0]
    t = lax.broadcasted_iota(jnp.int32, u.shape, 0)
    prev = jnp.where(t == 0, 0.0, pltpu.roll(u, 1, axis=0))
    nxt = jnp.where(t == s - 1, 0.0, pltpu.roll(u, s - 1, axis=0))
    w = w_ref[...]
    y = b_ref[0].astype(F32) * (prev * w[0:1] + u * w[1:2] + nxt * w[2:3])
    o_ref[0] = y.astype(o_ref.dtype)


def _short_conv(main, conv_w):
    b, s, _ = main.shape
    nt = SC_WIDTH // LANE
    return pl.pallas_call(
        _conv_kernel,
        grid=(b, nt),
        in_specs=[pl.BlockSpec((1, s, LANE), lambda bi, c: (bi, 0, OFF_B_SC // LANE + c)),
                  pl.BlockSpec((1, s, LANE), lambda bi, c: (bi, 0, OFF_C_SC // LANE + c)),
                  pl.BlockSpec((1, s, LANE), lambda bi, c: (bi, 0, OFF_X_SC // LANE + c)),
                  pl.BlockSpec((3, LANE), lambda bi, c: (0, c))],
        out_specs=pl.BlockSpec((1, s, LANE), lambda bi, c: (bi, 0, c)),
        out_shape=jax.ShapeDtypeStruct((b, s, SC_WIDTH), BF16),
        compiler_params=_cparams(("parallel", "parallel")),
        name="short_conv",
    )(main, main, main, conv_w)


def _gla_matrices(reverse):
    c = GLA_C
    t = np.arange(c)[:, None]
    m = np.arange(c)[None, :]
    blocks = [m <= t, m > t]
    for b in GLA_LEVELS:
        first = (t // (2 * b)) * (2 * b) + b
        is_q = (t & b) != 0
        blocks.append(np.where(is_q, (m > first) & (m <= t), (m > t) & (m <= first)))
    mats = np.stack(blocks).astype(np.float32)
    if reverse:
        mats = mats[:, ::-1, ::-1]
    mats = mats.reshape(-1, c)
    return jnp.asarray(np.concatenate([mats, mats], axis=1), dtype=BF16)


def _split_bf16(x):
    hi = x.astype(BF16)
    return hi, (x - hi.astype(F32)).astype(BF16)


def _pair_block_diag(x):
    lane = lax.broadcasted_iota(jnp.int32, x.shape, 1)
    zero = jnp.zeros((), x.dtype)
    return jnp.concatenate([jnp.where(lane < GLA_DK, x, zero), jnp.where(lane >= GLA_DK, x, zero)], axis=0)


def _gla_kernel(*refs, reverse, emit):
    if emit:
        q_ref, k_ref, v_ref, gt_ref, w2a_ref, w2b_ref, b2_ref, a_ref, s0_ref, o_ref, sf_ref, st_ref = refs
    else:
        k_ref, v_ref, gt_ref, w2a_ref, w2b_ref, b2_ref, a_ref, s0_ref, sf_ref, st_ref = refs
    c = GLA_C
    pw = 2 * GLA_DK
    step = pl.program_id(1)

    @pl.when(step == 0)
    def _():
        st_ref[...] = s0_ref[0]

    lr_hi, lr_lo = _split_bf16(gt_ref[0])
    logit = (_dot(jnp.concatenate([lr_hi, lr_lo], axis=1), w2a_ref[...]) + _dot(lr_hi, w2b_ref[...])
             + b2_ref[...])
    g = (jnp.minimum(logit, 0.0) - jnp.log1p(jnp.exp(-jnp.abs(logit)))) * (1.0 / GLA_GATE_TAU)
    g_hi, g_lo = _split_bf16(g)
    args = _dot(a_ref[...], jnp.concatenate([g_hi, g_lo], axis=0))
    cum = args[0:c]
    rem = args[c:2 * c]
    last_row = cum[0:1] if reverse else cum[c - 1:c]

    k = k_ref[0].astype(F32)
    v = v_ref[0]
    atts = []
    if emit:
        q = q_ref[0].astype(F32) * (GLA_DK ** -0.5)
        ti = lax.broadcasted_iota(jnp.int32, (c, pw), 0)
        si = lax.broadcasted_iota(jnp.int32, (c, pw), 1) & (GLA_DK - 1)
        if reverse:
            ti, si = c - 1 - ti, c - 1 - si
        for hp in range(GLA_HEADS // 2):
            cs = slice(hp * pw, (hp + 1) * pw)
            qp, kp = q[:, cs], k[:, cs]
            att = jnp.where(ti == si, _dot_nt(qp.astype(BF16), _pair_block_diag(kp.astype(BF16))), 0.0)
            for l, b in enumerate(GLA_LEVELS):
                is_q = (ti & b) != 0
                x = (jnp.exp(args[(2 + l) * c:(3 + l) * c, cs]) * jnp.where(is_q, qp, kp)).astype(BF16)
                pair = (((ti ^ si) >> (b.bit_length() - 1)) == 1) & is_q
                att = jnp.where(pair, _dot_nt(x, _pair_block_diag(x)), att)
            atts.append(att.astype(BF16))

    outs = []
    for h in range(GLA_HEADS):
        sl = slice(h * GLA_DK, (h + 1) * GLA_DK)
        kh = k[:, sl]
        vh = v[:, h * GLA_DV:(h + 1) * GLA_DV]
        state = st_ref[h]
        if emit:
            qd = (q[:, sl] * jnp.exp(cum[:, sl])).astype(BF16)
            att = atts[h // 2][:, (h % 2) * GLA_DK:(h % 2 + 1) * GLA_DK]
            outs.append(_dot(qd, state.astype(BF16)) + _dot(att, vh))
        kd = (kh * jnp.exp(rem[:, sl])).astype(BF16)
        decay = jnp.exp(jnp.broadcast_to(last_row[:, sl], (GLA_DK, GLA_DK))).T
        decay = jnp.concatenate([decay] * (GLA_DV // GLA_DK), axis=1)
        st_ref[h] = decay * state + _dot_tn(kd, vh)

    if emit:
        o_ref[0] = jnp.concatenate(outs, axis=-1)

    @pl.when(step == pl.num_programs(1) - 1)
    def _():
        sf_ref[0] = st_ref[...]


def _gla_scan(main, gate, gate_w, s0, reverse, emit):
    b, l, _ = main.shape
    n = l // GLA_C
    amat = _gla_matrices(reverse)
    w2a, w2b, b2 = gate_w
    chunk = (lambda bi, s: (bi, n - 1 - s)) if reverse else (lambda bi, s: (bi, s))

    def col(block):
        return lambda bi, s: chunk(bi, s) + (block,)

    const = lambda arr: pl.BlockSpec(arr.shape, lambda bi, s: (0,) * arr.ndim)
    in_specs = []
    args = []
    if emit:
        in_specs.append(pl.BlockSpec((1, GLA_C, GLA_KEY_WIDTH), col(OFF_Q_GLA // GLA_KEY_WIDTH)))
        args.append(main)
    in_specs += [pl.BlockSpec((1, GLA_C, GLA_KEY_WIDTH), col(OFF_K_GLA // GLA_KEY_WIDTH)),
                 pl.BlockSpec((1, GLA_C, GLA_VAL_WIDTH), col(OFF_V_GLA // GLA_VAL_WIDTH)),
                 pl.BlockSpec((1, GLA_C, LANE), col(0)),
                 const(w2a), const(w2b), const(b2), const(amat),
                 pl.BlockSpec((1, GLA_HEADS, GLA_DK, GLA_DV), lambda bi, s: (bi, 0, 0, 0))]
    args += [main, main, gate, w2a, w2b, b2, amat, s0]
    state_spec = pl.BlockSpec((1, GLA_HEADS, GLA_DK, GLA_DV), lambda bi, s: (bi, 0, 0, 0))
    state_shape = jax.ShapeDtypeStruct((b, GLA_HEADS, GLA_DK, GLA_DV), F32)
    if emit:
        out_specs = [pl.BlockSpec((1, GLA_C, GLA_VAL_WIDTH), col(0)), state_spec]
        out_shape = [jax.ShapeDtypeStruct((b, l, GLA_VAL_WIDTH), F32), state_shape]
    else:
        out_specs = [state_spec]
        out_shape = [state_shape]
    res = pl.pallas_call(
        functools.partial(_gla_kernel, reverse=reverse, emit=emit),
        grid=(b, n),
        in_specs=in_specs,
        out_specs=out_specs,
        out_shape=out_shape,
        scratch_shapes=[pltpu.VMEM((GLA_HEADS, GLA_DK, GLA_DV), F32)],
        compiler_params=_cparams(("parallel", "arbitrary")),
        name="gla_scan",
    )(*args)
    return (res[0], res[1]) if emit else (None, res[0])


def _gla_gate_weights(gate_w, gate_b):
    out = []
    for dr in range(2):
        w = jnp.zeros((LANE, GLA_KEY_WIDTH), F32)
        w = w.at[dr * GLA_GATE_RANK:(dr + 1) * GLA_GATE_RANK].set(gate_w[dr])
        w_hi = w.astype(BF16)
        w_lo = (w - w_hi.astype(F32)).astype(BF16)
        out.append((jnp.concatenate([w_hi, w_hi], axis=0), w_lo, gate_b[dr][None]))
    return out


def _merge_kernel(ona_ref, osc_ref, of_ref, ob_ref, r_ref, gna_ref, gsc_ref, ggl_ref, x_ref, gt_ref,
                  gn_ref, wna_ref, wsc_ref, wgl_ref, wo_ref, g2_ref, sh2_ref, sc2_ref, wr_ref,
                  xo_ref, h2_ref, hp_ref, lg_ref):
    o = of_ref[0] + ob_ref[0]
    normed = []
    for h in range(GLA_HEADS):
        oh = o[:, h * GLA_DV:(h + 1) * GLA_DV]
        ms = jnp.mean(oh * oh, axis=-1, keepdims=True)
        normed.append(oh * lax.rsqrt(ms + RMS_EPS))
    r = r_ref[0].astype(F32)
    y_gla = jnp.concatenate(normed, axis=-1) * gn_ref[...] * (r * _sigmoid(r))
    y = (_sigmoid(gna_ref[0].astype(F32)) * _dot(ona_ref[0], wna_ref[...])
         + _sigmoid(gsc_ref[0].astype(F32)) * _dot(osc_ref[0], wsc_ref[...])
         + _sigmoid(ggl_ref[0].astype(F32)) * _dot(y_gla.astype(BF16), wgl_ref[...]))
    xn = x_ref[0] + gt_ref[0] * _dot(y.astype(BF16), wo_ref[...])
    xo_ref[0] = xn
    ms = jnp.mean(xn * xn, axis=-1, keepdims=True)
    h2 = xn * lax.rsqrt(ms + RMS_EPS) * g2_ref[...] * (1.0 + sc2_ref[0]) + sh2_ref[0]
    h2b = h2.astype(BF16)
    h2_ref[0] = h2b
    _store_parts(hp_ref, _pack_rows(h2))
    lg_ref[0] = _dot(h2b, wr_ref[...])


def _merge(o_na, o_sc, o_f, o_b, main, x, gt1, gn, w_na, w_sc, w_gla, w_out, g2, sh2, sc2, w_router, tm):
    b, s, d = x.shape
    per_batch = gt1.shape[0] == b
    mod_map = (lambda bi, i: (bi, 0, 0)) if per_batch else (lambda bi, i: (0, 0, 0))
    tok = lambda width, blk: pl.BlockSpec((1, tm, width), lambda bi, i: (bi, i, blk))
    full = lambda arr: pl.BlockSpec(arr.shape, lambda bi, i: (0,) * arr.ndim)
    mod = pl.BlockSpec((1, 1, d), mod_map)
    gn_t = jnp.tile(gn, GLA_HEADS)[None]
    g2_t = g2[None]
    return pl.pallas_call(
        _merge_kernel,
        grid=(b, s // tm),
        in_specs=[tok(NA_WIDTH, 0), tok(SC_WIDTH, 0), tok(GLA_VAL_WIDTH, 0), tok(GLA_VAL_WIDTH, 0),
                  tok(d, OFF_R_GLA // d), tok(d, OFF_MERGE // d), tok(d, OFF_MERGE // d + 1),
                  tok(d, OFF_MERGE // d + 2), tok(d, 0), mod,
                  full(gn_t), full(w_na), full(w_sc), full(w_gla), full(w_out), full(g2_t), mod, mod,
                  full(w_router)],
        out_specs=[tok(d, 0), tok(d, 0),
                   pl.BlockSpec((SC_PARTS, 1, tm, SC_ROW), lambda bi, i: (0, bi, i, 0)), tok(LANE, 0)],
        out_shape=[jax.ShapeDtypeStruct((b, s, d), F32),
                   jax.ShapeDtypeStruct((b, s, d), BF16),
                   jax.ShapeDtypeStruct((SC_PARTS, b, s, SC_ROW), U32),
                   jax.ShapeDtypeStruct((b, s, LANE), F32)],
        compiler_params=_cparams(("parallel", "parallel")),
        name="merge",
    )(o_na, o_sc, o_f, o_b, main, main, main, main, x, gt1, gn_t, w_na, w_sc, w_gla, w_out, g2_t,
      sh2, sc2, w_router)


def _router_kernel(lg_ref, br_ref, tri_ref, eidx_ref, gw_ref, rank_ref, cnt_ref, carry_ref):
    tm = lg_ref.shape[0]

    @pl.when(pl.program_id(0) == 0)
    def _():
        carry_ref[...] = jnp.zeros_like(carry_ref)

    scores = _sigmoid(lg_ref[...].T[:N_EXPERTS])
    sel = scores + br_ref[...]
    neg = -jnp.inf

    sel3 = sel.reshape(N_EXPERT_GROUPS, GROUP_SIZE, tm)
    i3 = lax.broadcasted_iota(jnp.int32, sel3.shape, 1)
    m1 = sel3.max(axis=1, keepdims=True)
    first = jnp.where(sel3 == m1, i3, GROUP_SIZE).min(axis=1, keepdims=True)
    m2 = jnp.where(i3 == first, neg, sel3).max(axis=1, keepdims=True)
    gscore = (m1 + m2)[:, 0, :]

    gi = lax.broadcasted_iota(jnp.int32, gscore.shape, 0)
    gmask = jnp.zeros(gscore.shape, jnp.bool_)
    for _ in range(TOPK_GROUPS):
        m = gscore.max(axis=0, keepdims=True)
        pick = gi == jnp.where(gscore == m, gi, N_EXPERT_GROUPS).min(axis=0, keepdims=True)
        gmask = gmask | pick
        gscore = jnp.where(pick, neg, gscore)
    emask = jnp.broadcast_to(gmask[:, None, :], sel3.shape).reshape(N_EXPERTS, tm)
    sel = jnp.where(emask, sel, neg)

    ei = lax.broadcasted_iota(jnp.int32, sel.shape, 0)
    picks, idxs, ws = [], [], []
    for _ in range(TOP_K):
        m = sel.max(axis=0, keepdims=True)
        idx = jnp.where(sel == m, ei, N_EXPERTS).min(axis=0, keepdims=True)
        pick = ei == idx
        picks.append(pick)
        idxs.append(idx)
        ws.append(jnp.where(pick, scores, 0.0).sum(axis=0, keepdims=True))
        sel = jnp.where(pick, neg, sel)
    w = jnp.concatenate(ws, axis=0)
    gw_ref[...] = w / w.sum(axis=0, keepdims=True) * ROUTED_SCALE
    eidx_ref[...] = jnp.concatenate(idxs, axis=0)

    onehot = picks[0]
    for p in picks[1:]:
        onehot = onehot | p
    onehot = jnp.where(onehot, 1.0, 0.0).astype(BF16)
    before = _dot(onehot, tri_ref[...]) + jnp.tile(carry_ref[...], (1, tm // LANE))
    rank_ref[...] = jnp.concatenate(
        [jnp.where(p, before, 0.0).sum(axis=0, keepdims=True) for p in picks], axis=0).astype(jnp.int32)
    carry_ref[...] += _dot(onehot, jnp.ones((tm, LANE), BF16))
    cnt_ref[...] = carry_ref[...]


def _route(logits, b_router, tm=512):
    t = logits.shape[0]
    br = jnp.broadcast_to(b_router.astype(F32)[:, None], (N_EXPERTS, tm))
    tri = jnp.asarray(np.triu(np.ones((tm, tm), np.float32), 1), dtype=BF16)
    kt = lambda dt: jax.ShapeDtypeStruct((TOP_K, t), dt)
    eidx, gw, rank, cnt = pl.pallas_call(
        _router_kernel,
        grid=(t // tm,),
        in_specs=[pl.BlockSpec((tm, LANE), lambda i: (i, 0)),
                  pl.BlockSpec((N_EXPERTS, tm), lambda i: (0, 0)),
                  pl.BlockSpec((tm, tm), lambda i: (0, 0))],
        out_specs=[pl.BlockSpec((TOP_K, tm), lambda i: (0, i)),
                   pl.BlockSpec((TOP_K, tm), lambda i: (0, i)),
                   pl.BlockSpec((TOP_K, tm), lambda i: (0, i)),
                   pl.BlockSpec((N_EXPERTS, LANE), lambda i: (0, 0))],
        out_shape=[kt(jnp.int32), kt(F32), kt(jnp.int32),
                   jax.ShapeDtypeStruct((N_EXPERTS, LANE), F32)],
        scratch_shapes=[pltpu.VMEM((N_EXPERTS, LANE), F32)],
        compiler_params=_cparams(("arbitrary",)),
        name="router",
    )(logits, br, tri)
    return eidx, gw, rank, cnt[:, 0].astype(jnp.int32)


def _sc_mesh():
    return plsc.VectorSubcoreMesh(core_axis_name="core", subcore_axis_name="subcore")


def _dispatch_rows(xp, dest, slots):
    parts, t, _ = xp.shape
    nwin = parts * t // SC_WINDOW
    idx = dest.reshape(TOP_K, t // SC_WINDOW, SC_WINDOW).transpose(1, 0, 2)
    idx = jnp.concatenate([idx + part * slots for part in range(parts)], axis=0)

    @pl.kernel(out_type=jax.ShapeDtypeStruct((parts * slots, SC_ROW), xp.dtype), mesh=_sc_mesh(),
               scratch_types=[], name="moe_dispatch")
    def run(x_hbm, i_hbm, o_hbm):
        def body(x_vmem, i_vmem):
            for k in range(TOP_K):
                pltpu.sync_copy(x_vmem, o_hbm.at[i_vmem.at[0, k]])

        pltpu.emit_pipeline(
            body,
            grid=(nwin,),
            in_specs=[pl.BlockSpec((SC_WINDOW, SC_ROW), lambda i: (i, 0)),
                      pl.BlockSpec((1, TOP_K, SC_WINDOW), lambda i: (i, 0, 0))],
            out_specs=[],
            core_axis_name=("core", "subcore"),
            dimension_semantics=(pltpu.PARALLEL,),
        )(x_hbm, i_hbm)

    return run(xp.reshape(parts * t, SC_ROW), idx).reshape(parts, slots, SC_ROW)


def _gather_rows(yp, idx):
    n = idx.shape[0]
    parts, slots, _ = yp.shape
    idx = jnp.concatenate([idx + part * slots for part in range(parts)]).reshape(1, n * parts)

    @pl.kernel(out_type=jax.ShapeDtypeStruct((n * parts, SC_ROW), yp.dtype), mesh=_sc_mesh(),
               scratch_types=[], name="moe_gather")
    def run(y_hbm, i_hbm, o_hbm):
        def body(i_vmem, o_vmem):
            pltpu.sync_copy(y_hbm.at[i_vmem.at[0]], o_vmem)

        pltpu.emit_pipeline(
            body,
            grid=(n * parts // SC_WINDOW,),
            in_specs=[pl.BlockSpec((1, SC_WINDOW), lambda i: (0, i))],
            out_specs=[pl.BlockSpec((SC_WINDOW, SC_ROW), lambda i: (i, 0))],
            core_axis_name=("core", "subcore"),
            dimension_semantics=(pltpu.PARALLEL,),
        )(i_hbm, o_hbm)

    return run(yp.reshape(parts * slots, SC_ROW), idx).reshape(parts, n, SC_ROW)


def _expert_kernel(be_ref, bv_ref, x_ref, wg_ref, wu_ref, wd_ref, o_ref, wg_s, wu_s, wd_s):
    i = pl.program_id(0)
    valid = bv_ref[i]
    new_expert = (i == 0) | (be_ref[i] != be_ref[jnp.maximum(i - 1, 0)])

    @pl.when(new_expert)
    def _():
        wg_s[...] = wg_ref[0, 0].astype(BF16)
        wu_s[...] = wu_ref[0, 0].astype(BF16)
        wd_s[...] = wd_ref[0, 0].astype(BF16)

    @pl.when(valid > 0)
    def _():
        w = _load_parts(x_ref)
        row = lax.broadcasted_iota(jnp.int32, w.shape, 0)
        w = jnp.where(row < valid, w, jnp.uint32(0))
        lo, hi = _unpack_rows(w)
        x = jnp.concatenate([lo, hi], axis=1).astype(BF16)
        a = _dot(x, wg_s[...])
        hid = a * _sigmoid(a) * _dot(x, wu_s[...])
        _store_parts(o_ref, _pack_rows(_dot(hid.astype(BF16), wd_s[...])))


def _experts(xs, blk_e, blk_valid, layer, w_gate, w_up, w_down):
    parts, slots, _ = xs.shape
    d = D_MODEL
    nb = slots // MOE_BLOCK
    return pl.pallas_call(
        _expert_kernel,
        grid_spec=pltpu.PrefetchScalarGridSpec(
            num_scalar_prefetch=2,
            grid=(nb,),
            in_specs=[pl.BlockSpec((parts, MOE_BLOCK, SC_ROW), lambda i, be, bv: (0, i, 0)),
                      pl.BlockSpec((1, 1, d, EXPERT_FF), lambda i, be, bv: (layer, be[i], 0, 0)),
                      pl.BlockSpec((1, 1, d, EXPERT_FF), lambda i, be, bv: (layer, be[i], 0, 0)),
                      pl.BlockSpec((1, 1, EXPERT_FF, d), lambda i, be, bv: (layer, be[i], 0, 0))],
            out_specs=pl.BlockSpec((parts, MOE_BLOCK, SC_ROW), lambda i, be, bv: (0, i, 0)),
            scratch_shapes=[pltpu.VMEM((d, EXPERT_FF), BF16), pltpu.VMEM((d, EXPERT_FF), BF16),
                            pltpu.VMEM((EXPERT_FF, d), BF16)]),
        out_shape=jax.ShapeDtypeStruct((parts, slots, SC_ROW), U32),
        compiler_params=_cparams(("arbitrary",)),
        name="experts",
    )(blk_e, blk_valid, xs, w_gate, w_up, w_down)


def _combine_kernel(yg_ref, gw_ref, h_ref, x_ref, gt_ref, wsg_ref, wsu_ref, wsd_ref, gf_ref, o_ref, *, final):
    h = h_ref[0]
    a = _dot(h, wsg_ref[...])
    hid = a * _sigmoid(a) * _dot(h, wsu_ref[...])
    y = _dot(hid.astype(BF16), wsd_ref[...])
    gw = gw_ref[...]
    y_lo = y[:, :D_MODEL // 2]
    y_hi = y[:, D_MODEL // 2:]
    for k in range(TOP_K):
        lo, hi = _unpack_rows(_load_parts(yg_ref, k))
        y_lo = y_lo + gw[:, k:k + 1] * lo
        y_hi = y_hi + gw[:, k:k + 1] * hi
    y = jnp.concatenate([y_lo, y_hi], axis=1)
    xn = x_ref[0] + gt_ref[0] * y
    if final:
        ms = jnp.mean(xn * xn, axis=-1, keepdims=True)
        xn = xn * lax.rsqrt(ms + RMS_EPS) * gf_ref[...]
    o_ref[0] = xn


def _combine(yg, gw, tok_off, h2, x, gt2, ws_gate, ws_up, ws_down, g_final, final, tm):
    b, s, d = x.shape
    per_batch = gt2.shape[0] == b
    mod_map = (lambda bi, i: (bi, 0, 0)) if per_batch else (lambda bi, i: (0, 0, 0))
    full = lambda arr: pl.BlockSpec(arr.shape, lambda bi, i: (0,) * arr.ndim)
    gf = g_final[None]
    blk0 = tok_off // tm
    nblk = s // tm
    return pl.pallas_call(
        functools.partial(_combine_kernel, final=final),
        grid=(b, s // tm),
        in_specs=[pl.BlockSpec((SC_PARTS, TOP_K, tm, SC_ROW), lambda bi, i: (0, 0, blk0 + bi * nblk + i, 0)),
                  pl.BlockSpec((tm, TOP_K), lambda bi, i: (blk0 + bi * nblk + i, 0)),
                  pl.BlockSpec((1, tm, d), lambda bi, i: (bi, i, 0)),
                  pl.BlockSpec((1, tm, d), lambda bi, i: (bi, i, 0)),
                  pl.BlockSpec((1, 1, d), mod_map),
                  full(ws_gate), full(ws_up), full(ws_down), full(gf)],
        out_specs=pl.BlockSpec((1, tm, d), lambda bi, i: (bi, i, 0)),
        out_shape=jax.ShapeDtypeStruct((b, s, d), F32),
        compiler_params=_cparams(("parallel", "parallel")),
        name="combine",
    )(yg, gw, h2, x, gt2, ws_gate, ws_up, ws_down, gf)


def _reorder_w_in(w_in):
    k_na, v_na, k_gla = w_in[:, 0:512], w_in[:, 512:1024], w_in[:, 1024:1536]
    v_gla, gate = w_in[:, 1536:2560], w_in[:, 2560:2592]
    q_na = w_in[:, 2592:3104] * (NA_HEAD_DIM ** -0.5)
    rest = w_in[:, 3104:]
    main = jnp.concatenate([v_gla, k_na, v_na, k_gla, q_na, rest], axis=1).astype(BF16)
    gate = jnp.pad(gate, ((0, 0), (0, LANE - 2 * GLA_GATE_RANK))).astype(BF16)
    return main, gate


def _layer(x, ctx_s, mods, mods_ctx, p, ctx_out, final, g_final):
    b, s, d = x.shape
    sc = ctx_s.shape[1]
    sh1, sc1, gt1, sh2, sc2, gt2 = mods
    csh1, csc1, cgt1, csh2, csc2, cgt2 = mods_ctx

    w_main, w_gate = _reorder_w_in(p['w_in'])
    main, gate = _proj_in(x, p['g_norm1'], sh1, sc1, w_main, w_gate, tm=min(1024, s), tn=1024)
    if ctx_out:
        main_c, gate_c = _proj_in(ctx_s, p['g_norm1'], csh1, csc1, w_main, w_gate, tm=sc, tn=1024)
    else:
        main_c, gate_c = _proj_in(ctx_s, p['g_norm1'], csh1, csc1, w_main[:, :N_KV_MAIN], w_gate,
                                  tm=sc, tn=N_KV_MAIN // 2)

    o_na = _na_latent(main, main_c, p['na_rpb'])
    o_sc = _short_conv(main, p['conv_w'])

    gw_f, gw_b = _gla_gate_weights(p['gla_gate_w'], p['gla_gate_b'])
    s0 = jnp.zeros((b, GLA_HEADS, GLA_DK, GLA_DV), F32)
    o_cf, st_f = _gla_scan(main_c, gate_c, gw_f, s0, False, ctx_out)
    o_cb, st_b = _gla_scan(main_c, gate_c, gw_b, s0, True, ctx_out)
    o_f, _ = _gla_scan(main, gate, gw_f, st_f, False, True)
    o_b, _ = _gla_scan(main, gate, gw_b, st_b, True, True)

    w_na = p['w_branch_na'].astype(BF16)
    w_sc = p['w_branch_sc'].astype(BF16)
    w_gla = p['w_branch_gla'].astype(BF16)
    w_out = p['w_out'].astype(BF16)
    w_router = jnp.pad(p['w_router'], ((0, 0), (0, LANE - N_EXPERTS))).astype(BF16)
    x, h2, h2p, logits = _merge(o_na, o_sc, o_f, o_b, main, x, gt1, p['gla_norm_g'], w_na, w_sc, w_gla, w_out,
                                p['g_norm2'], sh2, sc2, w_router, tm=min(256, s))
    n_lat = b * s
    hp_all = h2p.reshape(SC_PARTS, n_lat, SC_ROW)
    lg_all = logits.reshape(n_lat, LANE)
    if ctx_out:
        o_na_c = _dense_attn(main_c)
        o_sc_c = _short_conv(main_c, p['conv_w'])
        ctx_s, h2_c, h2p_c, lg_c = _merge(o_na_c, o_sc_c, o_cf, o_cb, main_c, ctx_s, cgt1, p['gla_norm_g'],
                                          w_na, w_sc, w_gla, w_out, p['g_norm2'], csh2, csc2, w_router,
                                          tm=min(256, sc))
        hp_all = jnp.concatenate([hp_all, h2p_c.reshape(SC_PARTS, b * sc, SC_ROW)], axis=1)
        lg_all = jnp.concatenate([lg_all, lg_c.reshape(b * sc, LANE)], axis=0)

    t = hp_all.shape[1]
    eidx, gw, rank, counts = _route(lg_all, p['b_router'])
    padded = (counts + MOE_BLOCK - 1) // MOE_BLOCK * MOE_BLOCK
    pad_end = jnp.cumsum(padded)
    pad_start = pad_end - padded
    onehot = eidx[:, :, None] == jnp.arange(N_EXPERTS, dtype=jnp.int32)
    dest = jnp.sum(jnp.where(onehot, pad_start, 0), axis=-1) + rank
    n_blocks = -(-(t * TOP_K + N_EXPERTS * (MOE_BLOCK - 1)) // MOE_BLOCK)
    slots = n_blocks * MOE_BLOCK
    blk_start = jnp.arange(n_blocks, dtype=jnp.int32) * MOE_BLOCK
    blk_e = jnp.minimum(jnp.sum(pad_end[None, :] <= blk_start[:, None], axis=1), N_EXPERTS - 1).astype(jnp.int32)
    used_end = (pad_start + counts)[blk_e]
    blk_valid = jnp.clip(used_end - blk_start, 0, MOE_BLOCK).astype(jnp.int32)

    xs = _dispatch_rows(hp_all, dest, slots)
    ys = _experts(xs, blk_e, blk_valid, p['layer'], p['w_exp_gate'], p['w_exp_up'], p['w_exp_down'])
    yg = _gather_rows(ys, dest.reshape(-1)).reshape(SC_PARTS, TOP_K, t, SC_ROW)
    gw_t = gw.T

    ws_gate = p['w_sh_gate'].astype(BF16)
    ws_up = p['w_sh_up'].astype(BF16)
    ws_down = p['w_sh_down'].astype(BF16)
    x = _combine(yg, gw_t, 0, h2, x, gt2, ws_gate, ws_up, ws_down, g_final, final, tm=min(256, s))
    if ctx_out:
        ctx_s = _combine(yg, gw_t, n_lat, h2_c, ctx_s, cgt2, ws_gate, ws_up, ws_down, g_final, False,
                         tm=min(256, sc))
    return x, ctx_s


def kernel(x, c, ctx, c_ctx, w_mod, b_mod, g_norm1, g_norm2, w_in, na_rpb, w_branch_na, conv_w, w_branch_sc,
           gla_gate_w, gla_gate_b, gla_norm_g, w_branch_gla, w_out, w_router, b_router, w_exp_gate, w_exp_up,
           w_exp_down, w_sh_gate, w_sh_up, w_sh_down, g_final):
    stacked = dict(g_norm1=g_norm1, g_norm2=g_norm2, w_in=w_in, na_rpb=na_rpb, w_branch_na=w_branch_na,
                   conv_w=conv_w, w_branch_sc=w_branch_sc, gla_gate_w=gla_gate_w, gla_gate_b=gla_gate_b,
                   gla_norm_g=gla_norm_g, w_branch_gla=w_branch_gla, w_out=w_out, w_router=w_router,
                   b_router=b_router,
                   w_sh_gate=w_sh_gate, w_sh_up=w_sh_up, w_sh_down=w_sh_down)
    depth = w_in.shape[0]
    ctx_s = ctx
    for i in range(depth):
        p = {name: arr[i] for name, arr in stacked.items()}
        p.update(layer=i, w_exp_gate=w_exp_gate, w_exp_up=w_exp_up, w_exp_down=w_exp_down)
        mods, mods_ctx = _mod_vectors(c, c_ctx, w_mod[i], b_mod[i])
        last = i == depth - 1
        x, ctx_s = _layer(x, ctx_s, mods, mods_ctx, p, not last, last, g_final)
    return x
```

```python
import functools

import numpy as np
import jax
import jax.numpy as jnp
from jax import lax
from jax.experimental import pallas as pl
from jax.experimental.pallas import tpu as pltpu
from jax.experimental.pallas import tpu_sc as plsc

F32 = jnp.float32
BF16 = jnp.bfloat16
U32 = jnp.uint32

D_MODEL = 1024
N_MOD = 6
RMS_EPS = 1e-6
NEG_INF = -1e30
GRID_W = 64
NA_HEADS = 8
NA_HEAD_DIM = 64
NA_WIDTH = NA_HEADS * NA_HEAD_DIM
NA_WIN_R = 8
NA_WIN_C = 16
NA_GROUP = 4
SC_WIDTH = 512
GLA_HEADS = 4
GLA_KEY_WIDTH = 512
GLA_VAL_WIDTH = 1024
GLA_DK = GLA_KEY_WIDTH // GLA_HEADS
GLA_DV = GLA_VAL_WIDTH // GLA_HEADS
GLA_GATE_RANK = 16
GLA_GATE_TAU = 16.0
LOG2_E = 1.4426950408889634
N_EXPERTS = 64
N_EXPERT_GROUPS = 8
GROUP_SIZE = N_EXPERTS // N_EXPERT_GROUPS
TOPK_GROUPS = 4
TOP_K = 8
EXPERT_FF = 256
ROUTED_SCALE = 2.5
MOE_BLOCK = 512

LANE = 128
GLA_C = 128
GLA_LEVELS = tuple(GLA_C >> (i + 1) for i in range(GLA_C.bit_length() - 1))
VMEM_LIMIT = 48 * 1024 * 1024
SC_WINDOW = 128
SC_ROW = 256
SC_PARTS = D_MODEL // 2 // SC_ROW

OFF_V_GLA = 0
OFF_K_NA = 1024
OFF_V_NA = 1536
OFF_K_GLA = 2048
N_KV_MAIN = 2560
OFF_Q_NA = 2560
OFF_B_SC = 3072
OFF_C_SC = 3584
OFF_X_SC = 4096
OFF_Q_GLA = 4608
OFF_R_GLA = 5120
OFF_MERGE = 6144
N_MAIN = 9216


def _cparams(sem, vmem=VMEM_LIMIT):
    return pltpu.CompilerParams(dimension_semantics=sem, vmem_limit_bytes=vmem)


def _dot(a, b):
    return jnp.dot(a, b, preferred_element_type=F32)


def _dot_nt(a, b):
    return lax.dot_general(a, b, (((1,), (1,)), ((), ())), preferred_element_type=F32)


def _dot_tn(a, b):
    return lax.dot_general(a, b, (((0,), (0,)), ((), ())), preferred_element_type=F32)


def _sigmoid(x):
    return 1.0 / (1.0 + jnp.exp(-x))


def _pack_rows(x):
    n = x.shape[1] // 2
    r = x.astype(BF16).astype(F32)
    lo = pltpu.bitcast(r[:, :n], U32) >> 16
    hi = pltpu.bitcast(r[:, n:], U32)
    return hi | lo


def _store_parts(ref, words):
    for part in range(SC_PARTS):
        dst = ref.at[part, 0] if len(ref.shape) == 4 else ref.at[part]
        dst[...] = words[:, part * SC_ROW:(part + 1) * SC_ROW]


def _load_parts(ref, *lead):
    return jnp.concatenate([ref[(part,) + lead] for part in range(SC_PARTS)], axis=-1)


def _unpack_rows(w):
    lo = pltpu.bitcast(w << 16, F32)
    hi = pltpu.bitcast(w & jnp.uint32(0xFFFF0000), F32)
    return lo, hi


def _mod_kernel(a_ref, w_ref, b_ref, o_ref):
    a = a_ref[...]
    a = a * _sigmoid(a)
    o_ref[...] = _dot(a.astype(BF16), w_ref[...].astype(BF16)) + b_ref[...]


def _mod_vectors(c, c_ctx, w_mod, b_mod):
    b = c.shape[0]
    rows = -(-(b + 1) // 8) * 8
    a = jnp.concatenate([c, c_ctx[None], jnp.zeros((rows - b - 1, D_MODEL), F32)], axis=0)
    n = N_MOD * D_MODEL
    tn = 1536
    out = pl.pallas_call(
        _mod_kernel,
        grid=(n // tn,),
        in_specs=[pl.BlockSpec((rows, D_MODEL), lambda j: (0, 0)),
                  pl.BlockSpec((D_MODEL, tn), lambda j: (0, j)),
                  pl.BlockSpec((1, tn), lambda j: (0, j))],
        out_specs=pl.BlockSpec((rows, tn), lambda j: (0, j)),
        out_shape=jax.ShapeDtypeStruct((rows, n), F32),
        compiler_params=_cparams(("parallel",)),
        name="mod_vectors",
    )(a, w_mod, b_mod[None])
    lat = out[:b].reshape(b, N_MOD, 1, D_MODEL)
    ctx = out[b].reshape(N_MOD, 1, 1, D_MODEL)
    return [lat[:, i] for i in range(N_MOD)], [ctx[i] for i in range(N_MOD)]


def _proj_kernel(x_ref, g_ref, sh_ref, sc_ref, w_ref, wg_ref, o_ref, og_ref, h_ref):
    @pl.when(pl.program_id(2) == 0)
    def _():
        x = x_ref[0]
        ms = jnp.mean(x * x, axis=-1, keepdims=True)
        h = x * lax.rsqrt(ms + RMS_EPS) * g_ref[...] * (1.0 + sc_ref[0]) + sh_ref[0]
        hb = h.astype(BF16)
        h_ref[...] = hb
        og_ref[0] = _dot(hb, wg_ref[...])

    o_ref[0] = _dot(h_ref[...], w_ref[...]).astype(o_ref.dtype)


def _proj_in(x, g, shift, scale, w_main, w_gate, tm, tn):
    b, s, d = x.shape
    n = w_main.shape[1]
    per_batch = shift.shape[0] == b
    mod_map = (lambda bi, i, j: (bi, 0, 0)) if per_batch else (lambda bi, i, j: (0, 0, 0))
    return pl.pallas_call(
        _proj_kernel,
        grid=(b, s // tm, n // tn),
        in_specs=[pl.BlockSpec((1, tm, d), lambda bi, i, j: (bi, i, 0)),
                  pl.BlockSpec((1, d), lambda bi, i, j: (0, 0)),
                  pl.BlockSpec((1, 1, d), mod_map),
                  pl.BlockSpec((1, 1, d), mod_map),
                  pl.BlockSpec((d, tn), lambda bi, i, j: (0, j)),
                  pl.BlockSpec((d, LANE), lambda bi, i, j: (0, 0))],
        out_specs=[pl.BlockSpec((1, tm, tn), lambda bi, i, j: (bi, i, j)),
                   pl.BlockSpec((1, tm, LANE), lambda bi, i, j: (bi, i, 0))],
        out_shape=[jax.ShapeDtypeStruct((b, s, n), BF16),
                   jax.ShapeDtypeStruct((b, s, LANE), F32)],
        scratch_shapes=[pltpu.VMEM((tm, d), BF16)],
        compiler_params=_cparams(("parallel", "parallel", "arbitrary")),
        name="proj_in",
    )(x, g[None], shift, scale, w_main, w_gate)


def _softmax_av(q, keys, vals, biases):
    scores = []
    for kk, bb in zip(keys, biases):
        s = _dot_nt(q, kk)
        scores.append(s if bb is None else s + bb)
    m = scores[0].max(axis=-1, keepdims=True)
    for s in scores[1:]:
        m = jnp.maximum(m, s.max(axis=-1, keepdims=True))
    num = None
    den = None
    for s, vv in zip(scores, vals):
        e = jnp.exp(s - m)
        dsum = e.sum(axis=-1, keepdims=True)
        o = _dot(e.astype(BF16), vv)
        num = o if num is None else num + o
        den = dsum if den is None else den + dsum
    return num / den


def _na_kernel(q_ref, k_ref, v_ref, kc_ref, vc_ref, *rest, rows, kr):
    *bias_refs, o_ref = rest
    kc = kc_ref[0]
    vc = vc_ref[0]
    for j, bias_ref in enumerate(bias_refs):
        r = pl.program_id(1) * len(bias_refs) + j
        row_start = jnp.clip(r - kr // 2, 0, rows - kr)
        start = pl.multiple_of(row_start * GRID_W, GRID_W)
        n_win = kr * GRID_W
        q = q_ref[0, j * GRID_W:(j + 1) * GRID_W, :]
        kw = k_ref[0, pl.ds(start, n_win), :]
        vw = v_ref[0, pl.ds(start, n_win), :]
        o_ref[0, j * GRID_W:(j + 1) * GRID_W, :] = _na_row(q, kw, vw, kc, vc, bias_ref).astype(o_ref.dtype)


def _na_row(q, kw, vw, kc, vc, bias_ref):
    gw = NA_GROUP * NA_HEAD_DIM
    stacked = (NA_GROUP * GRID_W, gw)
    on_head = (lax.broadcasted_iota(jnp.int32, stacked, 0) // GRID_W
               == lax.broadcasted_iota(jnp.int32, stacked, 1) // NA_HEAD_DIM)
    outs = []
    for g in range(NA_HEADS // NA_GROUP):
        sl = slice(g * gw, (g + 1) * gw)
        q_all = jnp.where(on_head, jnp.concatenate([q[:, sl]] * NA_GROUP, axis=0), jnp.zeros((), q.dtype))
        bias = bias_ref[0, g * NA_GROUP * GRID_W:(g + 1) * NA_GROUP * GRID_W, :]
        o_all = _softmax_av(q_all, [kw[:, sl], kc[:, sl]], [vw[:, sl], vc[:, sl]], [bias, None])
        o_all = jnp.where(on_head, o_all, 0.0).reshape(NA_GROUP, GRID_W, gw)
        outs.append(o_all.sum(axis=0))
    return jnp.concatenate(outs, axis=-1)


def _na_bias_table(rpb, rows, kr):
    col = np.arange(GRID_W)
    col_start = np.clip(col - NA_WIN_C // 2, 0, GRID_W - NA_WIN_C)
    col_ok = (col[None, :] >= col_start[:, None]) & (col[None, :] < col_start[:, None] + NA_WIN_C)
    d_col = np.clip(col[None, :] - col[:, None], -(NA_WIN_C - 1), NA_WIN_C - 1) + NA_WIN_C - 1
    n_dr, n_dc = rpb.shape[1], rpb.shape[2]
    onehot = jnp.asarray((d_col.reshape(-1)[None, :] == np.arange(n_dc)[:, None]).astype(np.float32))
    by_col = jnp.dot(rpb.astype(F32).reshape(NA_HEADS * n_dr, n_dc), onehot, precision=lax.Precision.HIGHEST)
    by_col = by_col.reshape(NA_HEADS, n_dr, GRID_W, GRID_W)
    by_col = jnp.where(col_ok[None, None], by_col, NEG_INF)
    tables = []
    for o in range(kr):
        lo = NA_WIN_R - 1 - o
        tables.append(by_col[:, lo:lo + kr].transpose(0, 2, 1, 3).reshape(NA_HEADS, GRID_W, kr * GRID_W))
    return jnp.stack(tables).reshape(kr, NA_HEADS * GRID_W, kr * GRID_W)


def _na_latent(main, main_ctx, rpb):
    b, s, _ = main.shape
    sc = main_ctx.shape[1]
    rows = s // GRID_W
    kr = min(NA_WIN_R, rows)
    bias = _na_bias_table(rpb, rows, kr)
    w = NA_WIDTH

    per_step = 2 if rows % 2 == 0 else 1

    def bias_spec(j):
        def bias_map(bi, i):
            r = i * per_step + j
            return (r - jnp.clip(r - kr // 2, 0, rows - kr), 0, 0)
        return pl.BlockSpec((1, NA_HEADS * GRID_W, kr * GRID_W), bias_map)

    return pl.pallas_call(
        functools.partial(_na_kernel, rows=rows, kr=kr),
        grid=(b, rows // per_step),
        in_specs=[pl.BlockSpec((1, per_step * GRID_W, w), lambda bi, i: (bi, i, OFF_Q_NA // w)),
                  pl.BlockSpec((1, s, w), lambda bi, i: (bi, 0, OFF_K_NA // w)),
                  pl.BlockSpec((1, s, w), lambda bi, i: (bi, 0, OFF_V_NA // w)),
                  pl.BlockSpec((1, sc, w), lambda bi, i: (bi, 0, OFF_K_NA // w)),
                  pl.BlockSpec((1, sc, w), lambda bi, i: (bi, 0, OFF_V_NA // w))]
                 + [bias_spec(j) for j in range(per_step)],
        out_specs=pl.BlockSpec((1, per_step * GRID_W, w), lambda bi, i: (bi, i, 0)),
        out_shape=jax.ShapeDtypeStruct((b, s, w), BF16),
        compiler_params=_cparams(("parallel", "arbitrary")),
        name="na_latent",
    )(main, main, main, main_ctx, main_ctx, *([bias] * per_step))


def _dense_attn_kernel(q_ref, k_ref, v_ref, o_ref):
    q = q_ref[0]
    k = k_ref[0]
    v = v_ref[0]
    outs = []
    for h in range(NA_HEADS):
        sl = slice(h * NA_HEAD_DIM, (h + 1) * NA_HEAD_DIM)
        outs.append(_softmax_av(q[:, sl], [k[:, sl]], [v[:, sl]], [None]))
    o_ref[0] = jnp.concatenate(outs, axis=-1).astype(o_ref.dtype)


def _dense_attn(main_ctx):
    b, sc, _ = main_ctx.shape
    w = NA_WIDTH
    return pl.pallas_call(
        _dense_attn_kernel,
        grid=(b,),
        in_specs=[pl.BlockSpec((1, sc, w), lambda bi: (bi, 0, OFF_Q_NA // w)),
                  pl.BlockSpec((1, sc, w), lambda bi: (bi, 0, OFF_K_NA // w)),
                  pl.BlockSpec((1, sc, w), lambda bi: (bi, 0, OFF_V_NA // w))],
        out_specs=pl.BlockSpec((1, sc, w), lambda bi: (bi, 0, 0)),
        out_shape=jax.ShapeDtypeStruct((b, sc, w), BF16),
        compiler_params=_cparams(("parallel",)),
        name="ctx_attn",
    )(main_ctx, main_ctx, main_ctx)


def _conv_kernel(b_ref, c_ref, x_ref, w_ref, o_ref):
    u = c_ref[0].astype(F32) * x_ref[0].astype(F32)
    s = u.shape[0]
    t = lax.broadcasted_iota(jnp.int32, u.shape, 0)
    prev = jnp.where(t == 0, 0.0, pltpu.roll(u, 1, axis=0))
    nxt = jnp.where(t == s - 1, 0.0, pltpu.roll(u, s - 1, axis=0))
    w = w_ref[...]
    y = b_ref[0].astype(F32) * (prev * w[0:1] + u * w[1:2] + nxt * w[2:3])
    o_ref[0] = y.astype(o_ref.dtype)


def _short_conv(main, conv_w):
    b, s, _ = main.shape
    nt = SC_WIDTH // LANE
    return pl.pallas_call(
        _conv_kernel,
        grid=(b, nt),
        in_specs=[pl.BlockSpec((1, s, LANE), lambda bi, c: (bi, 0, OFF_B_SC // LANE + c)),
                  pl.BlockSpec((1, s, LANE), lambda bi, c: (bi, 0, OFF_C_SC // LANE + c)),
                  pl.BlockSpec((1, s, LANE), lambda bi, c: (bi, 0, OFF_X_SC // LANE + c)),
                  pl.BlockSpec((3, LANE), lambda bi, c: (0, c))],
        out_specs=pl.BlockSpec((1, s, LANE), lambda bi, c: (bi, 0, c)),
        out_shape=jax.ShapeDtypeStruct((b, s, SC_WIDTH), BF16),
        compiler_params=_cparams(("parallel", "parallel")),
        name="short_conv",
    )(main, main, main, conv_w)


def _gla_matrices(reverse):
    c = GLA_C
    t = np.arange(c)[:, None]
    m = np.arange(c)[None, :]
    blocks = [m <= t, m > t]
    for b in GLA_LEVELS:
        first = (t // (2 * b)) * (2 * b) + b
        is_q = (t & b) != 0
        blocks.append(np.where(is_q, (m > first) & (m <= t), (m > t) & (m <= first)))
    mats = np.stack(blocks).astype(np.float32)
    if reverse:
        mats = mats[:, ::-1, ::-1]
    mats = mats.reshape(-1, c)
    return jnp.asarray(np.concatenate([mats, mats], axis=1), dtype=BF16)


def _split_bf16(x):
    hi = x.astype(BF16)
    return hi, (x - hi.astype(F32)).astype(BF16)


def _pair_block_diag(x):
    lane = lax.broadcasted_iota(jnp.int32, x.shape, 1)
    zero = jnp.zeros((), x.dtype)
    return jnp.concatenate([jnp.where(lane < GLA_DK, x, zero), jnp.where(lane >= GLA_DK, x, zero)], axis=0)


def _gla_kernel(*refs, reverse, emit):
    if emit:
        q_ref, k_ref, v_ref, gt_ref, w2a_ref, w2b_ref, b2_ref, a_ref, s0_ref, o_ref, sf_ref, st_ref = refs
    else:
        k_ref, v_ref, gt_ref, w2a_ref, w2b_ref, b2_ref, a_ref, s0_ref, sf_ref, st_ref = refs
    c = GLA_C
    pw = 2 * GLA_DK
    step = pl.program_id(1)

    @pl.when(step == 0)
    def _():
        st_ref[...] = s0_ref[0]

    lr_hi, lr_lo = _split_bf16(gt_ref[0])
    logit = (_dot(jnp.concatenate([lr_hi, lr_lo], axis=1), w2a_ref[...]) + _dot(lr_hi, w2b_ref[...])
             + b2_ref[...])
    g = (jnp.minimum(logit, 0.0) - jnp.log1p(jnp.exp(-jnp.abs(logit)))) * (LOG2_E / GLA_GATE_TAU)
    g_hi, g_lo = _split_bf16(g)
    args = _dot(a_ref[...], jnp.concatenate([g_hi, g_lo], axis=0))
    cum = args[0:c]
    rem = args[c:2 * c]
    last_row = cum[0:1] if reverse else cum[c - 1:c]

    k = k_ref[0].astype(F32)
    v = v_ref[0]
    atts = []
    if emit:
        q = q_ref[0].astype(F32) * (GLA_DK ** -0.5)
        ti = lax.broadcasted_iota(jnp.int32, (c, pw), 0)
        si = lax.broadcasted_iota(jnp.int32, (c, pw), 1) & (GLA_DK - 1)
        if reverse:
            ti, si = c - 1 - ti, c - 1 - si
        for hp in range(GLA_HEADS // 2):
            cs = slice(hp * pw, (hp + 1) * pw)
            qp, kp = q[:, cs], k[:, cs]
            att = jnp.where(ti == si, _dot_nt(qp.astype(BF16), _pair_block_diag(kp.astype(BF16))), 0.0)
            for l, b in enumerate(GLA_LEVELS):
                is_q = (ti & b) != 0
                x = (jnp.exp2(args[(2 + l) * c:(3 + l) * c, cs]) * jnp.where(is_q, qp, kp)).astype(BF16)
                pair = (((ti ^ si) >> (b.bit_length() - 1)) == 1) & is_q
                att = jnp.where(pair, _dot_nt(x, _pair_block_diag(x)), att)
            atts.append(att.astype(BF16))

    outs = []
    for h in range(GLA_HEADS):
        sl = slice(h * GLA_DK, (h + 1) * GLA_DK)
        kh = k[:, sl]
        vh = v[:, h * GLA_DV:(h + 1) * GLA_DV]
        state = st_ref[h]
        if emit:
            qd = (q[:, sl] * jnp.exp2(cum[:, sl])).astype(BF16)
            att = atts[h // 2][:, (h % 2) * GLA_DK:(h % 2 + 1) * GLA_DK]
            outs.append(_dot(qd, state.astype(BF16)) + _dot(att, vh))
        kd = (kh * jnp.exp2(rem[:, sl])).astype(BF16)
        decay = jnp.exp2(jnp.broadcast_to(last_row[:, sl], (GLA_DK, GLA_DK))).T
        decay = jnp.concatenate([decay] * (GLA_DV // GLA_DK), axis=1)
        st_ref[h] = decay * state + _dot_tn(kd, vh)

    if emit:
        o_ref[0] = jnp.concatenate(outs, axis=-1).astype(o_ref.dtype)

    @pl.when(step == pl.num_programs(1) - 1)
    def _():
        sf_ref[0] = st_ref[...]


def _gla_scan(main, gate, gate_w, s0, reverse, emit):
    b, l, _ = main.shape
    n = l // GLA_C
    amat = _gla_matrices(reverse)
    w2a, w2b, b2 = gate_w
    chunk = (lambda bi, s: (bi, n - 1 - s)) if reverse else (lambda bi, s: (bi, s))

    def col(block):
        return lambda bi, s: chunk(bi, s) + (block,)

    const = lambda arr: pl.BlockSpec(arr.shape, lambda bi, s: (0,) * arr.ndim)
    in_specs = []
    args = []
    if emit:
        in_specs.append(pl.BlockSpec((1, GLA_C, GLA_KEY_WIDTH), col(OFF_Q_GLA // GLA_KEY_WIDTH)))
        args.append(main)
    in_specs += [pl.BlockSpec((1, GLA_C, GLA_KEY_WIDTH), col(OFF_K_GLA // GLA_KEY_WIDTH)),
                 pl.BlockSpec((1, GLA_C, GLA_VAL_WIDTH), col(OFF_V_GLA // GLA_VAL_WIDTH)),
                 pl.BlockSpec((1, GLA_C, LANE), col(0)),
                 const(w2a), const(w2b), const(b2), const(amat),
                 pl.BlockSpec((1, GLA_HEADS, GLA_DK, GLA_DV), lambda bi, s: (bi, 0, 0, 0))]
    args += [main, main, gate, w2a, w2b, b2, amat, s0]
    state_spec = pl.BlockSpec((1, GLA_HEADS, GLA_DK, GLA_DV), lambda bi, s: (bi, 0, 0, 0))
    state_shape = jax.ShapeDtypeStruct((b, GLA_HEADS, GLA_DK, GLA_DV), F32)
    if emit:
        out_specs = [pl.BlockSpec((1, GLA_C, GLA_VAL_WIDTH), col(0)), state_spec]
        out_shape = [jax.ShapeDtypeStruct((b, l, GLA_VAL_WIDTH), BF16), state_shape]
    else:
        out_specs = [state_spec]
        out_shape = [state_shape]
    res = pl.pallas_call(
        functools.partial(_gla_kernel, reverse=reverse, emit=emit),
        grid=(b, n),
        in_specs=in_specs,
        out_specs=out_specs,
        out_shape=out_shape,
        scratch_shapes=[pltpu.VMEM((GLA_HEADS, GLA_DK, GLA_DV), F32)],
        compiler_params=_cparams(("parallel", "arbitrary")),
        name="gla_scan",
    )(*args)
    return (res[0], res[1]) if emit else (None, res[0])


def _gla_gate_weights(gate_w, gate_b):
    out = []
    for dr in range(2):
        w = jnp.zeros((LANE, GLA_KEY_WIDTH), F32)
        w = w.at[dr * GLA_GATE_RANK:(dr + 1) * GLA_GATE_RANK].set(gate_w[dr])
        w_hi = w.astype(BF16)
        w_lo = (w - w_hi.astype(F32)).astype(BF16)
        out.append((jnp.concatenate([w_hi, w_hi], axis=0), w_lo, gate_b[dr][None]))
    return out


def _merge_kernel(ona_ref, osc_ref, of_ref, ob_ref, r_ref, gna_ref, gsc_ref, ggl_ref, x_ref, gt_ref,
                  gn_ref, wna_ref, wsc_ref, wgl_ref, wo_ref, g2_ref, sh2_ref, sc2_ref, wr_ref,
                  xo_ref, h2_ref, hp_ref, lg_ref):
    o = of_ref[0].astype(F32) + ob_ref[0].astype(F32)
    normed = []
    for h in range(GLA_HEADS):
        oh = o[:, h * GLA_DV:(h + 1) * GLA_DV]
        ms = jnp.mean(oh * oh, axis=-1, keepdims=True)
        normed.append(oh * lax.rsqrt(ms + RMS_EPS))
    r = r_ref[0].astype(F32)
    y_gla = jnp.concatenate(normed, axis=-1) * gn_ref[...] * (r * _sigmoid(r))
    y = (_sigmoid(gna_ref[0].astype(F32)) * _dot(ona_ref[0], wna_ref[...])
         + _sigmoid(gsc_ref[0].astype(F32)) * _dot(osc_ref[0], wsc_ref[...])
         + _sigmoid(ggl_ref[0].astype(F32)) * _dot(y_gla.astype(BF16), wgl_ref[...]))
    xn = x_ref[0] + gt_ref[0] * _dot(y.astype(BF16), wo_ref[...])
    xo_ref[0] = xn
    ms = jnp.mean(xn * xn, axis=-1, keepdims=True)
    h2 = xn * lax.rsqrt(ms + RMS_EPS) * g2_ref[...] * (1.0 + sc2_ref[0]) + sh2_ref[0]
    h2b = h2.astype(BF16)
    h2_ref[0] = h2b
    _store_parts(hp_ref, _pack_rows(h2))
    lg_ref[0] = _dot(h2b, wr_ref[...])


def _merge(o_na, o_sc, o_f, o_b, main, x, gt1, gn, w_na, w_sc, w_gla, w_out, g2, sh2, sc2, w_router, tm):
    b, s, d = x.shape
    per_batch = gt1.shape[0] == b
    mod_map = (lambda bi, i: (bi, 0, 0)) if per_batch else (lambda bi, i: (0, 0, 0))
    tok = lambda width, blk: pl.BlockSpec((1, tm, width), lambda bi, i: (bi, i, blk))
    full = lambda arr: pl.BlockSpec(arr.shape, lambda bi, i: (0,) * arr.ndim)
    mod = pl.BlockSpec((1, 1, d), mod_map)
    gn_t = jnp.tile(gn, GLA_HEADS)[None]
    g2_t = g2[None]
    return pl.pallas_call(
        _merge_kernel,
        grid=(b, s // tm),
        in_specs=[tok(NA_WIDTH, 0), tok(SC_WIDTH, 0), tok(GLA_VAL_WIDTH, 0), tok(GLA_VAL_WIDTH, 0),
                  tok(d, OFF_R_GLA // d), tok(d, OFF_MERGE // d), tok(d, OFF_MERGE // d + 1),
                  tok(d, OFF_MERGE // d + 2), tok(d, 0), mod,
                  full(gn_t), full(w_na), full(w_sc), full(w_gla), full(w_out), full(g2_t), mod, mod,
                  full(w_router)],
        out_specs=[tok(d, 0), tok(d, 0),
                   pl.BlockSpec((SC_PARTS, 1, tm, SC_ROW), lambda bi, i: (0, bi, i, 0)), tok(LANE, 0)],
        out_shape=[jax.ShapeDtypeStruct((b, s, d), F32),
                   jax.ShapeDtypeStruct((b, s, d), BF16),
                   jax.ShapeDtypeStruct((SC_PARTS, b, s, SC_ROW), U32),
                   jax.ShapeDtypeStruct((b, s, LANE), F32)],
        compiler_params=_cparams(("parallel", "parallel")),
        name="merge",
    )(o_na, o_sc, o_f, o_b, main, main, main, main, x, gt1, gn_t, w_na, w_sc, w_gla, w_out, g2_t,
      sh2, sc2, w_router)


def _router_kernel(lg_ref, br_ref, tri_ref, eidx_ref, gw_ref, rank_ref, cnt_ref, carry_ref):
    tm = lg_ref.shape[0]

    @pl.when(pl.program_id(0) == 0)
    def _():
        carry_ref[...] = jnp.zeros_like(carry_ref)

    scores = _sigmoid(lg_ref[...].T[:N_EXPERTS])
    sel = scores + br_ref[...]
    neg = -jnp.inf

    sel3 = sel.reshape(N_EXPERT_GROUPS, GROUP_SIZE, tm)
    i3 = lax.broadcasted_iota(jnp.int32, sel3.shape, 1)
    m1 = sel3.max(axis=1, keepdims=True)
    first = jnp.where(sel3 == m1, i3, GROUP_SIZE).min(axis=1, keepdims=True)
    m2 = jnp.where(i3 == first, neg, sel3).max(axis=1, keepdims=True)
    gscore = (m1 + m2)[:, 0, :]

    gi = lax.broadcasted_iota(jnp.int32, gscore.shape, 0)
    gmask = jnp.zeros(gscore.shape, jnp.bool_)
    for _ in range(TOPK_GROUPS):
        m = gscore.max(axis=0, keepdims=True)
        pick = gi == jnp.where(gscore == m, gi, N_EXPERT_GROUPS).min(axis=0, keepdims=True)
        gmask = gmask | pick
        gscore = jnp.where(pick, neg, gscore)
    emask = jnp.broadcast_to(gmask[:, None, :], sel3.shape).reshape(N_EXPERTS, tm)
    sel = jnp.where(emask, sel, neg)

    ei = lax.broadcasted_iota(jnp.int32, sel.shape, 0)
    picks, idxs, ws = [], [], []
    for _ in range(TOP_K):
        m = sel.max(axis=0, keepdims=True)
        idx = jnp.where(sel == m, ei, N_EXPERTS).min(axis=0, keepdims=True)
        pick = ei == idx
        picks.append(pick)
        idxs.append(idx)
        ws.append(jnp.where(pick, scores, 0.0).sum(axis=0, keepdims=True))
        sel = jnp.where(pick, neg, sel)
    w = jnp.concatenate(ws, axis=0)
    gw_ref[...] = w / w.sum(axis=0, keepdims=True) * ROUTED_SCALE
    eidx_ref[...] = jnp.concatenate(idxs, axis=0)

    onehot = picks[0]
    for p in picks[1:]:
        onehot = onehot | p
    onehot = jnp.where(onehot, 1.0, 0.0).astype(BF16)
    before = _dot(onehot, tri_ref[...]) + jnp.tile(carry_ref[...], (1, tm // LANE))
    rank_ref[...] = jnp.concatenate(
        [jnp.where(p, before, 0.0).sum(axis=0, keepdims=True) for p in picks], axis=0).astype(jnp.int32)
    carry_ref[...] += _dot(onehot, jnp.ones((tm, LANE), BF16))
    cnt_ref[...] = carry_ref[...]


def _route(logits, b_router, tm=512):
    t = logits.shape[0]
    br = jnp.broadcast_to(b_router.astype(F32)[:, None], (N_EXPERTS, tm))
    tri = jnp.asarray(np.triu(np.ones((tm, tm), np.float32), 1), dtype=BF16)
    kt = lambda dt: jax.ShapeDtypeStruct((TOP_K, t), dt)
    eidx, gw, rank, cnt = pl.pallas_call(
        _router_kernel,
        grid=(t // tm,),
        in_specs=[pl.BlockSpec((tm, LANE), lambda i: (i, 0)),
                  pl.BlockSpec((N_EXPERTS, tm), lambda i: (0, 0)),
                  pl.BlockSpec((tm, tm), lambda i: (0, 0))],
        out_specs=[pl.BlockSpec((TOP_K, tm), lambda i: (0, i)),
                   pl.BlockSpec((TOP_K, tm), lambda i: (0, i)),
                   pl.BlockSpec((TOP_K, tm), lambda i: (0, i)),
                   pl.BlockSpec((N_EXPERTS, LANE), lambda i: (0, 0))],
        out_shape=[kt(jnp.int32), kt(F32), kt(jnp.int32),
                   jax.ShapeDtypeStruct((N_EXPERTS, LANE), F32)],
        scratch_shapes=[pltpu.VMEM((N_EXPERTS, LANE), F32)],
        compiler_params=_cparams(("arbitrary",)),
        name="router",
    )(logits, br, tri)
    return eidx, gw, rank, cnt[:, 0].astype(jnp.int32)


def _sc_mesh():
    return plsc.VectorSubcoreMesh(core_axis_name="core", subcore_axis_name="subcore")


def _dispatch_rows(xp, dest, slots):
    parts, t, _ = xp.shape
    nwin = parts * t // SC_WINDOW
    idx = dest.reshape(TOP_K, t // SC_WINDOW, SC_WINDOW).transpose(1, 0, 2)
    idx = jnp.concatenate([idx + part * slots for part in range(parts)], axis=0)

    @pl.kernel(out_type=jax.ShapeDtypeStruct((parts * slots, SC_ROW), xp.dtype), mesh=_sc_mesh(),
               scratch_types=[], name="moe_dispatch")
    def run(x_hbm, i_hbm, o_hbm):
        def body(x_vmem, i_vmem):
            for k in range(TOP_K):
                pltpu.sync_copy(x_vmem, o_hbm.at[i_vmem.at[0, k]])

        pltpu.emit_pipeline(
            body,
            grid=(nwin,),
            in_specs=[pl.BlockSpec((SC_WINDOW, SC_ROW), lambda i: (i, 0)),
                      pl.BlockSpec((1, TOP_K, SC_WINDOW), lambda i: (i, 0, 0))],
            out_specs=[],
            core_axis_name=("core", "subcore"),
            dimension_semantics=(pltpu.PARALLEL,),
        )(x_hbm, i_hbm)

    return run(xp.reshape(parts * t, SC_ROW), idx).reshape(parts, slots, SC_ROW)


def _gather_rows(yp, idx):
    n = idx.shape[0]
    parts, slots, _ = yp.shape
    idx = jnp.concatenate([idx + part * slots for part in range(parts)]).reshape(1, n * parts)

    @pl.kernel(out_type=jax.ShapeDtypeStruct((n * parts, SC_ROW), yp.dtype), mesh=_sc_mesh(),
               scratch_types=[], name="moe_gather")
    def run(y_hbm, i_hbm, o_hbm):
        def body(i_vmem, o_vmem):
            pltpu.sync_copy(y_hbm.at[i_vmem.at[0]], o_vmem)

        pltpu.emit_pipeline(
            body,
            grid=(n * parts // SC_WINDOW,),
            in_specs=[pl.BlockSpec((1, SC_WINDOW), lambda i: (0, i))],
            out_specs=[pl.BlockSpec((SC_WINDOW, SC_ROW), lambda i: (i, 0))],
            core_axis_name=("core", "subcore"),
            dimension_semantics=(pltpu.PARALLEL,),
        )(i_hbm, o_hbm)

    return run(yp.reshape(parts * slots, SC_ROW), idx).reshape(parts, n, SC_ROW)


def _expert_kernel(be_ref, bv_ref, x_ref, wg_ref, wu_ref, wd_ref, o_ref, wg_s, wu_s, wd_s):
    i = pl.program_id(0)
    valid = bv_ref[i]
    new_expert = (i == 0) | (be_ref[i] != be_ref[jnp.maximum(i - 1, 0)])

    @pl.when(new_expert)
    def _():
        wg_s[...] = wg_ref[0, 0].astype(BF16)
        wu_s[...] = wu_ref[0, 0].astype(BF16)
        wd_s[...] = wd_ref[0, 0].astype(BF16)

    @pl.when(valid > 0)
    def _():
        w = _load_parts(x_ref)
        row = lax.broadcasted_iota(jnp.int32, w.shape, 0)
        w = jnp.where(row < valid, w, jnp.uint32(0))
        lo, hi = _unpack_rows(w)
        x = jnp.concatenate([lo, hi], axis=1).astype(BF16)
        a = _dot(x, wg_s[...])
        hid = a * _sigmoid(a) * _dot(x, wu_s[...])
        _store_parts(o_ref, _pack_rows(_dot(hid.astype(BF16), wd_s[...])))


def _experts(xs, blk_e, blk_valid, layer, w_gate, w_up, w_down):
    parts, slots, _ = xs.shape
    d = D_MODEL
    nb = slots // MOE_BLOCK
    return pl.pallas_call(
        _expert_kernel,
        grid_spec=pltpu.PrefetchScalarGridSpec(
            num_scalar_prefetch=2,
            grid=(nb,),
            in_specs=[pl.BlockSpec((parts, MOE_BLOCK, SC_ROW), lambda i, be, bv: (0, i, 0)),
                      pl.BlockSpec((1, 1, d, EXPERT_FF), lambda i, be, bv: (layer, be[i], 0, 0)),
                      pl.BlockSpec((1, 1, d, EXPERT_FF), lambda i, be, bv: (layer, be[i], 0, 0)),
                      pl.BlockSpec((1, 1, EXPERT_FF, d), lambda i, be, bv: (layer, be[i], 0, 0))],
            out_specs=pl.BlockSpec((parts, MOE_BLOCK, SC_ROW), lambda i, be, bv: (0, i, 0)),
            scratch_shapes=[pltpu.VMEM((d, EXPERT_FF), BF16), pltpu.VMEM((d, EXPERT_FF), BF16),
                            pltpu.VMEM((EXPERT_FF, d), BF16)]),
        out_shape=jax.ShapeDtypeStruct((parts, slots, SC_ROW), U32),
        compiler_params=_cparams(("arbitrary",)),
        name="experts",
    )(blk_e, blk_valid, xs, w_gate, w_up, w_down)


def _combine_kernel(yg_ref, gw_ref, h_ref, x_ref, gt_ref, wsg_ref, wsu_ref, wsd_ref, gf_ref, o_ref, *, final):
    h = h_ref[0]
    a = _dot(h, wsg_ref[...])
    hid = a * _sigmoid(a) * _dot(h, wsu_ref[...])
    y = _dot(hid.astype(BF16), wsd_ref[...])
    gw = gw_ref[...]
    y_lo = y[:, :D_MODEL // 2]
    y_hi = y[:, D_MODEL // 2:]
    for k in range(TOP_K):
        lo, hi = _unpack_rows(_load_parts(yg_ref, k))
        y_lo = y_lo + gw[:, k:k + 1] * lo
        y_hi = y_hi + gw[:, k:k + 1] * hi
    y = jnp.concatenate([y_lo, y_hi], axis=1)
    xn = x_ref[0] + gt_ref[0] * y
    if final:
        ms = jnp.mean(xn * xn, axis=-1, keepdims=True)
        xn = xn * lax.rsqrt(ms + RMS_EPS) * gf_ref[...]
    o_ref[0] = xn


def _combine(yg, gw, tok_off, h2, x, gt2, ws_gate, ws_up, ws_down, g_final, final, tm):
    b, s, d = x.shape
    per_batch = gt2.shape[0] == b
    mod_map = (lambda bi, i: (bi, 0, 0)) if per_batch else (lambda bi, i: (0, 0, 0))
    full = lambda arr: pl.BlockSpec(arr.shape, lambda bi, i: (0,) * arr.ndim)
    gf = g_final[None]
    blk0 = tok_off // tm
    nblk = s // tm
    return pl.pallas_call(
        functools.partial(_combine_kernel, final=final),
        grid=(b, s // tm),
        in_specs=[pl.BlockSpec((SC_PARTS, TOP_K, tm, SC_ROW), lambda bi, i: (0, 0, blk0 + bi * nblk + i, 0)),
                  pl.BlockSpec((tm, TOP_K), lambda bi, i: (blk0 + bi * nblk + i, 0)),
                  pl.BlockSpec((1, tm, d), lambda bi, i: (bi, i, 0)),
                  pl.BlockSpec((1, tm, d), lambda bi, i: (bi, i, 0)),
                  pl.BlockSpec((1, 1, d), mod_map),
                  full(ws_gate), full(ws_up), full(ws_down), full(gf)],
        out_specs=pl.BlockSpec((1, tm, d), lambda bi, i: (bi, i, 0)),
        out_shape=jax.ShapeDtypeStruct((b, s, d), F32),
        compiler_params=_cparams(("parallel", "parallel")),
        name="combine",
    )(yg, gw, h2, x, gt2, ws_gate, ws_up, ws_down, gf)


def _reorder_w_in(w_in):
    k_na, v_na, k_gla = w_in[:, 0:512], w_in[:, 512:1024], w_in[:, 1024:1536]
    v_gla, gate = w_in[:, 1536:2560], w_in[:, 2560:2592]
    q_na = w_in[:, 2592:3104] * (NA_HEAD_DIM ** -0.5)
    rest = w_in[:, 3104:]
    main = jnp.concatenate([v_gla, k_na, v_na, k_gla, q_na, rest], axis=1).astype(BF16)
    gate = jnp.pad(gate, ((0, 0), (0, LANE - 2 * GLA_GATE_RANK))).astype(BF16)
    return main, gate


def _layer(x, ctx_s, mods, mods_ctx, p, ctx_out, final, g_final):
    b, s, d = x.shape
    sc = ctx_s.shape[1]
    sh1, sc1, gt1, sh2, sc2, gt2 = mods
    csh1, csc1, cgt1, csh2, csc2, cgt2 = mods_ctx

    w_main, w_gate = _reorder_w_in(p['w_in'])
    main, gate = _proj_in(x, p['g_norm1'], sh1, sc1, w_main, w_gate, tm=min(1024, s), tn=1024)
    if ctx_out:
        main_c, gate_c = _proj_in(ctx_s, p['g_norm1'], csh1, csc1, w_main, w_gate, tm=sc, tn=1024)
    else:
        main_c, gate_c = _proj_in(ctx_s, p['g_norm1'], csh1, csc1, w_main[:, :N_KV_MAIN], w_gate,
                                  tm=sc, tn=N_KV_MAIN // 2)

    o_na = _na_latent(main, main_c, p['na_rpb'])
    o_sc = _short_conv(main, p['conv_w'])

    gw_f, gw_b = _gla_gate_weights(p['gla_gate_w'], p['gla_gate_b'])
    s0 = jnp.zeros((b, GLA_HEADS, GLA_DK, GLA_DV), F32)
    o_cf, st_f = _gla_scan(main_c, gate_c, gw_f, s0, False, ctx_out)
    o_cb, st_b = _gla_scan(main_c, gate_c, gw_b, s0, True, ctx_out)
    o_f, _ = _gla_scan(main, gate, gw_f, st_f, False, True)
    o_b, _ = _gla_scan(main, gate, gw_b, st_b, True, True)

    w_na = p['w_branch_na'].astype(BF16)
    w_sc = p['w_branch_sc'].astype(BF16)
    w_gla = p['w_branch_gla'].astype(BF16)
    w_out = p['w_out'].astype(BF16)
    w_router = jnp.pad(p['w_router'], ((0, 0), (0, LANE - N_EXPERTS))).astype(BF16)
    x, h2, h2p, logits = _merge(o_na, o_sc, o_f, o_b, main, x, gt1, p['gla_norm_g'], w_na, w_sc, w_gla, w_out,
                                p['g_norm2'], sh2, sc2, w_router, tm=min(512, s))
    n_lat = b * s
    hp_all = h2p.reshape(SC_PARTS, n_lat, SC_ROW)
    lg_all = logits.reshape(n_lat, LANE)
    if ctx_out:
        o_na_c = _dense_attn(main_c)
        o_sc_c = _short_conv(main_c, p['conv_w'])
        ctx_s, h2_c, h2p_c, lg_c = _merge(o_na_c, o_sc_c, o_cf, o_cb, main_c, ctx_s, cgt1, p['gla_norm_g'],
                                          w_na, w_sc, w_gla, w_out, p['g_norm2'], csh2, csc2, w_router,
                                          tm=min(256, sc))
        hp_all = jnp.concatenate([hp_all, h2p_c.reshape(SC_PARTS, b * sc, SC_ROW)], axis=1)
        lg_all = jnp.concatenate([lg_all, lg_c.reshape(b * sc, LANE)], axis=0)

    t = hp_all.shape[1]
    eidx, gw, rank, counts = _route(lg_all, p['b_router'])
    padded = (counts + MOE_BLOCK - 1) // MOE_BLOCK * MOE_BLOCK
    pad_end = jnp.cumsum(padded)
    pad_start = pad_end - padded
    onehot = eidx[:, :, None] == jnp.arange(N_EXPERTS, dtype=jnp.int32)
    dest = jnp.sum(jnp.where(onehot, pad_start, 0), axis=-1) + rank
    n_blocks = -(-(t * TOP_K + N_EXPERTS * (MOE_BLOCK - 1)) // MOE_BLOCK)
    slots = n_blocks * MOE_BLOCK
    blk_start = jnp.arange(n_blocks, dtype=jnp.int32) * MOE_BLOCK
    blk_e = jnp.minimum(jnp.sum(pad_end[None, :] <= blk_start[:, None], axis=1), N_EXPERTS - 1).astype(jnp.int32)
    used_end = (pad_start + counts)[blk_e]
    blk_valid = jnp.clip(used_end - blk_start, 0, MOE_BLOCK).astype(jnp.int32)

    xs = _dispatch_rows(hp_all, dest, slots)
    ys = _experts(xs, blk_e, blk_valid, p['layer'], p['w_exp_gate'], p['w_exp_up'], p['w_exp_down'])
    yg = _gather_rows(ys, dest.reshape(-1)).reshape(SC_PARTS, TOP_K, t, SC_ROW)
    gw_t = gw.T

    ws_gate = p['w_sh_gate'].astype(BF16)
    ws_up = p['w_sh_up'].astype(BF16)
    ws_down = p['w_sh_down'].astype(BF16)
    x = _combine(yg, gw_t, 0, h2, x, gt2, ws_gate, ws_up, ws_down, g_final, final, tm=min(256, s))
    if ctx_out:
        ctx_s = _combine(yg, gw_t, n_lat, h2_c, ctx_s, cgt2, ws_gate, ws_up, ws_down, g_final, False,
                         tm=min(256, sc))
    return x, ctx_s


def kernel(x, c, ctx, c_ctx, w_mod, b_mod, g_norm1, g_norm2, w_in, na_rpb, w_branch_na, conv_w, w_branch_sc,
           gla_gate_w, gla_gate_b, gla_norm_g, w_branch_gla, w_out, w_router, b_router, w_exp_gate, w_exp_up,
           w_exp_down, w_sh_gate, w_sh_up, w_sh_down, g_final):
    stacked = dict(g_norm1=g_norm1, g_norm2=g_norm2, w_in=w_in, na_rpb=na_rpb, w_branch_na=w_branch_na,
                   conv_w=conv_w, w_branch_sc=w_branch_sc, gla_gate_w=gla_gate_w, gla_gate_b=gla_gate_b,
                   gla_norm_g=gla_norm_g, w_branch_gla=w_branch_gla, w_out=w_out, w_router=w_router,
                   b_router=b_router,
                   w_sh_gate=w_sh_gate, w_sh_up=w_sh_up, w_sh_down=w_sh_down)
    depth = w_in.shape[0]
    ctx_s = ctx
    for i in range(depth):
        p = {name: arr[i] for name, arr in stacked.items()}
        p.update(layer=i, w_exp_gate=w_exp_gate, w_exp_up=w_exp_up, w_exp_down=w_exp_down)
        mods, mods_ctx = _mod_vectors(c, c_ctx, w_mod[i], b_mod[i])
        last = i == depth - 1
        x, ctx_s = _layer(x, ctx_s, mods, mods_ctx, p, not last, last, g_final)
    return x
```

```python
import functools

import numpy as np
import jax
import jax.numpy as jnp
from jax import lax
from jax.experimental import pallas as pl
from jax.experimental.pallas import tpu as pltpu
from jax.experimental.pallas import tpu_sc as plsc

F32 = jnp.float32
BF16 = jnp.bfloat16
U32 = jnp.uint32

D_MODEL = 1024
N_MOD = 6
RMS_EPS = 1e-6
NEG_INF = -1e30
GRID_W = 64
NA_HEADS = 8
NA_HEAD_DIM = 64
NA_WIDTH = NA_HEADS * NA_HEAD_DIM
NA_WIN_R = 8
NA_WIN_C = 16
NA_GROUP = 4
SC_WIDTH = 512
GLA_HEADS = 4
GLA_KEY_WIDTH = 512
GLA_VAL_WIDTH = 1024
GLA_DK = GLA_KEY_WIDTH // GLA_HEADS
GLA_DV = GLA_VAL_WIDTH // GLA_HEADS
GLA_GATE_RANK = 16
GLA_GATE_TAU = 16.0
LOG2_E = 1.4426950408889634
N_EXPERTS = 64
N_EXPERT_GROUPS = 8
GROUP_SIZE = N_EXPERTS // N_EXPERT_GROUPS
TOPK_GROUPS = 4
TOP_K = 8
EXPERT_FF = 256
ROUTED_SCALE = 2.5
MOE_BLOCK = 512

LANE = 128
GLA_C = 128
GLA_LEVELS = tuple(GLA_C >> (i + 1) for i in range(GLA_C.bit_length() - 1))
VMEM_LIMIT = 48 * 1024 * 1024
SC_WINDOW = 128
SC_ROW = 256
SC_PARTS = D_MODEL // 2 // SC_ROW

OFF_V_GLA = 0
OFF_K_NA = 1024
OFF_V_NA = 1536
OFF_K_GLA = 2048
N_KV_MAIN = 2560
OFF_Q_NA = 2560
OFF_B_SC = 3072
OFF_C_SC = 3584
OFF_X_SC = 4096
OFF_Q_GLA = 4608
OFF_R_GLA = 5120
OFF_MERGE = 6144
N_MAIN = 9216


def _cparams(sem, vmem=VMEM_LIMIT):
    return pltpu.CompilerParams(dimension_semantics=sem, vmem_limit_bytes=vmem)


def _dot(a, b):
    return jnp.dot(a, b, preferred_element_type=F32)


def _dot_nt(a, b):
    return lax.dot_general(a, b, (((1,), (1,)), ((), ())), preferred_element_type=F32)


def _dot_tn(a, b):
    return lax.dot_general(a, b, (((0,), (0,)), ((), ())), preferred_element_type=F32)


def _sigmoid(x):
    return 1.0 / (1.0 + jnp.exp(-x))


def _pack_rows(x):
    n = x.shape[1] // 2
    r = x.astype(BF16).astype(F32)
    lo = pltpu.bitcast(r[:, :n], U32) >> 16
    hi = pltpu.bitcast(r[:, n:], U32)
    return hi | lo


def _store_parts(ref, words):
    for part in range(SC_PARTS):
        dst = ref.at[part, 0] if len(ref.shape) == 4 else ref.at[part]
        dst[...] = words[:, part * SC_ROW:(part + 1) * SC_ROW]


def _load_parts(ref, *lead):
    return jnp.concatenate([ref[(part,) + lead] for part in range(SC_PARTS)], axis=-1)


def _unpack_rows(w):
    lo = pltpu.bitcast(w << 16, F32)
    hi = pltpu.bitcast(w & jnp.uint32(0xFFFF0000), F32)
    return lo, hi


def _mod_kernel(a_ref, w_ref, b_ref, o_ref):
    a = a_ref[...]
    a = a * _sigmoid(a)
    o_ref[...] = _dot(a.astype(BF16), w_ref[...].astype(BF16)) + b_ref[...]


def _mod_vectors(c, c_ctx, w_mod, b_mod):
    b = c.shape[0]
    rows = -(-(b + 1) // 8) * 8
    a = jnp.concatenate([c, c_ctx[None], jnp.zeros((rows - b - 1, D_MODEL), F32)], axis=0)
    n = N_MOD * D_MODEL
    tn = 1536
    out = pl.pallas_call(
        _mod_kernel,
        grid=(n // tn,),
        in_specs=[pl.BlockSpec((rows, D_MODEL), lambda j: (0, 0)),
                  pl.BlockSpec((D_MODEL, tn), lambda j: (0, j)),
                  pl.BlockSpec((1, tn), lambda j: (0, j))],
        out_specs=pl.BlockSpec((rows, tn), lambda j: (0, j)),
        out_shape=jax.ShapeDtypeStruct((rows, n), F32),
        compiler_params=_cparams(("parallel",)),
        name="mod_vectors",
    )(a, w_mod, b_mod[None])
    lat = out[:b].reshape(b, N_MOD, 1, D_MODEL)
    ctx = out[b].reshape(N_MOD, 1, 1, D_MODEL)
    return [lat[:, i] for i in range(N_MOD)], [ctx[i] for i in range(N_MOD)]


def _proj_kernel(x_ref, g_ref, sh_ref, sc_ref, w_ref, wg_ref, o_ref, og_ref, h_ref):
    @pl.when(pl.program_id(2) == 0)
    def _():
        x = x_ref[0]
        ms = jnp.mean(x * x, axis=-1, keepdims=True)
        h = x * lax.rsqrt(ms + RMS_EPS) * g_ref[...] * (1.0 + sc_ref[0]) + sh_ref[0]
        hb = h.astype(BF16)
        h_ref[...] = hb
        og_ref[0] = _dot(hb, wg_ref[...])

    o_ref[0] = _dot(h_ref[...], w_ref[...]).astype(o_ref.dtype)


W_IN_TILE = 512
W_IN_GATE_SHIFT = 2 * GLA_GATE_RANK


def _prep_w_in_kernel(a_ref, b_ref, o_ref, g_ref):
    t = pl.program_id(0)
    first_lat = N_KV_MAIN // W_IN_TILE
    a = a_ref[0]

    @pl.when(t < first_lat)
    def _():
        o_ref[...] = a.astype(BF16)

    @pl.when(t >= first_lat)
    def _():
        cat = jnp.concatenate([a, b_ref[0]], axis=1)
        moved = pltpu.roll(cat, cat.shape[1] - W_IN_GATE_SHIFT, axis=1)[:, :W_IN_TILE]
        scale = jnp.where(t == first_lat, NA_HEAD_DIM ** -0.5, 1.0)
        o_ref[...] = (moved * scale).astype(BF16)

    @pl.when(t == first_lat)
    def _():
        head = a[:, :LANE]
        lane = lax.broadcasted_iota(jnp.int32, head.shape, 1)
        g_ref[...] = jnp.where(lane < W_IN_GATE_SHIFT, head, 0.0).astype(BF16)


def _prep_w_in(w_in, layer):
    d = w_in.shape[1]
    first_lat = N_KV_MAIN // W_IN_TILE
    kv_perm = OFF_K_NA // W_IN_TILE

    def a_map(t):
        return (layer, 0, jnp.where(t < first_lat, (t + first_lat - kv_perm) % first_lat, t))

    def b_map(t):
        return (layer, 0, jnp.where(t < first_lat, 0, (t + 1) * (W_IN_TILE // LANE)))

    return pl.pallas_call(
        _prep_w_in_kernel,
        grid=(N_MAIN // W_IN_TILE,),
        in_specs=[pl.BlockSpec((1, d, W_IN_TILE), a_map),
                  pl.BlockSpec((1, d, LANE), b_map)],
        out_specs=[pl.BlockSpec((d, W_IN_TILE), lambda t: (0, t)),
                   pl.BlockSpec((d, LANE), lambda t: (0, 0))],
        out_shape=[jax.ShapeDtypeStruct((d, N_MAIN), BF16), jax.ShapeDtypeStruct((d, LANE), BF16)],
        compiler_params=_cparams(("arbitrary",)),
        name="prep_w_in",
    )(w_in, w_in)


def _proj_in(x, g, shift, scale, w_main, w_gate, tm, tn, n=None):
    b, s, d = x.shape
    n = w_main.shape[1] if n is None else n
    per_batch = shift.shape[0] == b
    mod_map = (lambda bi, i, j: (bi, 0, 0)) if per_batch else (lambda bi, i, j: (0, 0, 0))
    return pl.pallas_call(
        _proj_kernel,
        grid=(b, s // tm, n // tn),
        in_specs=[pl.BlockSpec((1, tm, d), lambda bi, i, j: (bi, i, 0)),
                  pl.BlockSpec((1, d), lambda bi, i, j: (0, 0)),
                  pl.BlockSpec((1, 1, d), mod_map),
                  pl.BlockSpec((1, 1, d), mod_map),
                  pl.BlockSpec((d, tn), lambda bi, i, j: (0, j)),
                  pl.BlockSpec((d, LANE), lambda bi, i, j: (0, 0))],
        out_specs=[pl.BlockSpec((1, tm, tn), lambda bi, i, j: (bi, i, j)),
                   pl.BlockSpec((1, tm, LANE), lambda bi, i, j: (bi, i, 0))],
        out_shape=[jax.ShapeDtypeStruct((b, s, n), BF16),
                   jax.ShapeDtypeStruct((b, s, LANE), F32)],
        scratch_shapes=[pltpu.VMEM((tm, d), BF16)],
        compiler_params=_cparams(("parallel", "parallel", "arbitrary")),
        name="proj_in",
    )(x, g[None], shift, scale, w_main, w_gate)


def _softmax_av(q, keys, vals, biases):
    scores = []
    for kk, bb in zip(keys, biases):
        s = _dot_nt(q, kk)
        scores.append(s if bb is None else s + bb)
    m = scores[0].max(axis=-1, keepdims=True)
    for s in scores[1:]:
        m = jnp.maximum(m, s.max(axis=-1, keepdims=True))
    num = None
    den = None
    for s, vv in zip(scores, vals):
        e = jnp.exp(s - m)
        dsum = e.sum(axis=-1, keepdims=True)
        o = _dot(e.astype(BF16), vv)
        num = o if num is None else num + o
        den = dsum if den is None else den + dsum
    return num / den


def _na_kernel(q_ref, k_ref, v_ref, kc_ref, vc_ref, *rest, rows, kr):
    *bias_refs, o_ref = rest
    kc = kc_ref[0]
    vc = vc_ref[0]
    for j, bias_ref in enumerate(bias_refs):
        r = pl.program_id(1) * len(bias_refs) + j
        row_start = jnp.clip(r - kr // 2, 0, rows - kr)
        start = pl.multiple_of(row_start * GRID_W, GRID_W)
        n_win = kr * GRID_W
        q = q_ref[0, j * GRID_W:(j + 1) * GRID_W, :]
        kw = k_ref[0, pl.ds(start, n_win), :]
        vw = v_ref[0, pl.ds(start, n_win), :]
        o_ref[0, j * GRID_W:(j + 1) * GRID_W, :] = _na_row(q, kw, vw, kc, vc, bias_ref).astype(o_ref.dtype)


def _na_row(q, kw, vw, kc, vc, bias_ref):
    gw = NA_GROUP * NA_HEAD_DIM
    stacked = (NA_GROUP * GRID_W, gw)
    on_head = (lax.broadcasted_iota(jnp.int32, stacked, 0) // GRID_W
               == lax.broadcasted_iota(jnp.int32, stacked, 1) // NA_HEAD_DIM)
    outs = []
    for g in range(NA_HEADS // NA_GROUP):
        sl = slice(g * gw, (g + 1) * gw)
        q_all = jnp.where(on_head, jnp.concatenate([q[:, sl]] * NA_GROUP, axis=0), jnp.zeros((), q.dtype))
        bias = bias_ref[0, g * NA_GROUP * GRID_W:(g + 1) * NA_GROUP * GRID_W, :]
        o_all = _softmax_av(q_all, [kw[:, sl], kc[:, sl]], [vw[:, sl], vc[:, sl]], [bias, None])
        o_all = jnp.where(on_head, o_all, 0.0).reshape(NA_GROUP, GRID_W, gw)
        outs.append(o_all.sum(axis=0))
    return jnp.concatenate(outs, axis=-1)


def _na_bias_table(rpb, rows, kr):
    col = np.arange(GRID_W)
    col_start = np.clip(col - NA_WIN_C // 2, 0, GRID_W - NA_WIN_C)
    col_ok = (col[None, :] >= col_start[:, None]) & (col[None, :] < col_start[:, None] + NA_WIN_C)
    d_col = np.clip(col[None, :] - col[:, None], -(NA_WIN_C - 1), NA_WIN_C - 1) + NA_WIN_C - 1
    n_dr, n_dc = rpb.shape[1], rpb.shape[2]
    onehot = jnp.asarray((d_col.reshape(-1)[None, :] == np.arange(n_dc)[:, None]).astype(np.float32))
    by_col = jnp.dot(rpb.astype(F32).reshape(NA_HEADS * n_dr, n_dc), onehot, precision=lax.Precision.HIGHEST)
    by_col = by_col.reshape(NA_HEADS, n_dr, GRID_W, GRID_W)
    by_col = jnp.where(col_ok[None, None], by_col, NEG_INF)
    tables = []
    for o in range(kr):
        lo = NA_WIN_R - 1 - o
        tables.append(by_col[:, lo:lo + kr].transpose(0, 2, 1, 3).reshape(NA_HEADS, GRID_W, kr * GRID_W))
    return jnp.stack(tables).reshape(kr, NA_HEADS * GRID_W, kr * GRID_W)


def _na_latent(main, main_ctx, rpb):
    b, s, _ = main.shape
    sc = main_ctx.shape[1]
    rows = s // GRID_W
    kr = min(NA_WIN_R, rows)
    bias = _na_bias_table(rpb, rows, kr)
    w = NA_WIDTH

    per_step = 2 if rows % 2 == 0 else 1

    def bias_spec(j):
        def bias_map(bi, i):
            r = i * per_step + j
            return (r - jnp.clip(r - kr // 2, 0, rows - kr), 0, 0)
        return pl.BlockSpec((1, NA_HEADS * GRID_W, kr * GRID_W), bias_map)

    return pl.pallas_call(
        functools.partial(_na_kernel, rows=rows, kr=kr),
        grid=(b, rows // per_step),
        in_specs=[pl.BlockSpec((1, per_step * GRID_W, w), lambda bi, i: (bi, i, OFF_Q_NA // w)),
                  pl.BlockSpec((1, s, w), lambda bi, i: (bi, 0, OFF_K_NA // w)),
                  pl.BlockSpec((1, s, w), lambda bi, i: (bi, 0, OFF_V_NA // w)),
                  pl.BlockSpec((1, sc, w), lambda bi, i: (bi, 0, OFF_K_NA // w)),
                  pl.BlockSpec((1, sc, w), lambda bi, i: (bi, 0, OFF_V_NA // w))]
                 + [bias_spec(j) for j in range(per_step)],
        out_specs=pl.BlockSpec((1, per_step * GRID_W, w), lambda bi, i: (bi, i, 0)),
        out_shape=jax.ShapeDtypeStruct((b, s, w), BF16),
        compiler_params=_cparams(("parallel", "arbitrary")),
        name="na_latent",
    )(main, main, main, main_ctx, main_ctx, *([bias] * per_step))


def _dense_attn_kernel(q_ref, k_ref, v_ref, o_ref):
    q = q_ref[0]
    k = k_ref[0]
    v = v_ref[0]
    outs = []
    for h in range(NA_HEADS):
        sl = slice(h * NA_HEAD_DIM, (h + 1) * NA_HEAD_DIM)
        outs.append(_softmax_av(q[:, sl], [k[:, sl]], [v[:, sl]], [None]))
    o_ref[0] = jnp.concatenate(outs, axis=-1).astype(o_ref.dtype)


def _dense_attn(main_ctx):
    b, sc, _ = main_ctx.shape
    w = NA_WIDTH
    return pl.pallas_call(
        _dense_attn_kernel,
        grid=(b,),
        in_specs=[pl.BlockSpec((1, sc, w), lambda bi: (bi, 0, OFF_Q_NA // w)),
                  pl.BlockSpec((1, sc, w), lambda bi: (bi, 0, OFF_K_NA // w)),
                  pl.BlockSpec((1, sc, w), lambda bi: (bi, 0, OFF_V_NA // w))],
        out_specs=pl.BlockSpec((1, sc, w), lambda bi: (bi, 0, 0)),
        out_shape=jax.ShapeDtypeStruct((b, sc, w), BF16),
        compiler_params=_cparams(("parallel",)),
        name="ctx_attn",
    )(main_ctx, main_ctx, main_ctx)


def _conv_kernel(b_ref, c_ref, x_ref, w_ref, o_ref):
    u = c_ref[0].astype(F32) * x_ref[0].astype(F32)
    s = u.shape[0]
    t = lax.broadcasted_iota(jnp.int32, u.shape, 0)
    prev = jnp.where(t == 0, 0.0, pltpu.roll(u, 1, axis=0))
    nxt = jnp.where(t == s - 1, 0.0, pltpu.roll(u, s - 1, axis=0))
    w = w_ref[...]
    y = b_ref[0].astype(F32) * (prev * w[0:1] + u * w[1:2] + nxt * w[2:3])
    o_ref[0] = y.astype(o_ref.dtype)


def _short_conv(main, conv_w):
    b, s, _ = main.shape
    nt = SC_WIDTH // LANE
    return pl.pallas_call(
        _conv_kernel,
        grid=(b, nt),
        in_specs=[pl.BlockSpec((1, s, LANE), lambda bi, c: (bi, 0, OFF_B_SC // LANE + c)),
                  pl.BlockSpec((1, s, LANE), lambda bi, c: (bi, 0, OFF_C_SC // LANE + c)),
                  pl.BlockSpec((1, s, LANE), lambda bi, c: (bi, 0, OFF_X_SC // LANE + c)),
                  pl.BlockSpec((3, LANE), lambda bi, c: (0, c))],
        out_specs=pl.BlockSpec((1, s, LANE), lambda bi, c: (bi, 0, c)),
        out_shape=jax.ShapeDtypeStruct((b, s, SC_WIDTH), BF16),
        compiler_params=_cparams(("parallel", "parallel")),
        name="short_conv",
    )(main, main, main, conv_w)


def _gla_matrices(reverse):
    c = GLA_C
    t = np.arange(c)[:, None]
    m = np.arange(c)[None, :]
    blocks = [m <= t, m > t]
    for b in GLA_LEVELS:
        first = (t // (2 * b)) * (2 * b) + b
        is_q = (t & b) != 0
        blocks.append(np.where(is_q, (m > first) & (m <= t), (m > t) & (m <= first)))
    mats = np.stack(blocks).astype(np.float32)
    if reverse:
        mats = mats[:, ::-1, ::-1]
    mats = mats.reshape(-1, c)
    return jnp.asarray(np.concatenate([mats, mats], axis=1), dtype=BF16)


def _split_bf16(x):
    hi = x.astype(BF16)
    return hi, (x - hi.astype(F32)).astype(BF16)


def _pair_block_diag(x):
    lane = lax.broadcasted_iota(jnp.int32, x.shape, 1)
    zero = jnp.zeros((), x.dtype)
    return jnp.concatenate([jnp.where(lane < GLA_DK, x, zero), jnp.where(lane >= GLA_DK, x, zero)], axis=0)


def _gla_kernel(*refs, reverse, emit):
    if emit:
        q_ref, k_ref, v_ref, gt_ref, w2a_ref, w2b_ref, b2_ref, a_ref, s0_ref, o_ref, sf_ref, st_ref = refs
    else:
        k_ref, v_ref, gt_ref, w2a_ref, w2b_ref, b2_ref, a_ref, s0_ref, sf_ref, st_ref = refs
    c = GLA_C
    pw = 2 * GLA_DK
    step = pl.program_id(1)

    @pl.when(step == 0)
    def _():
        st_ref[...] = s0_ref[0]

    lr_hi, lr_lo = _split_bf16(gt_ref[0])
    logit = (_dot(jnp.concatenate([lr_hi, lr_lo], axis=1), w2a_ref[...]) + _dot(lr_hi, w2b_ref[...])
             + b2_ref[...])
    g = (jnp.minimum(logit, 0.0) - jnp.log1p(jnp.exp(-jnp.abs(logit)))) * (LOG2_E / GLA_GATE_TAU)
    g_hi, g_lo = _split_bf16(g)
    args = _dot(a_ref[...], jnp.concatenate([g_hi, g_lo], axis=0))
    cum = args[0:c]
    rem = args[c:2 * c]
    last_row = cum[0:1] if reverse else cum[c - 1:c]

    k = k_ref[0].astype(F32)
    v = v_ref[0]
    atts = []
    if emit:
        q = q_ref[0].astype(F32) * (GLA_DK ** -0.5)
        ti = lax.broadcasted_iota(jnp.int32, (c, pw), 0)
        si = lax.broadcasted_iota(jnp.int32, (c, pw), 1) & (GLA_DK - 1)
        if reverse:
            ti, si = c - 1 - ti, c - 1 - si
        for hp in range(GLA_HEADS // 2):
            cs = slice(hp * pw, (hp + 1) * pw)
            qp, kp = q[:, cs], k[:, cs]
            att = jnp.where(ti == si, _dot_nt(qp.astype(BF16), _pair_block_diag(kp.astype(BF16))), 0.0)
            for l, b in enumerate(GLA_LEVELS):
                is_q = (ti & b) != 0
                x = (jnp.exp2(args[(2 + l) * c:(3 + l) * c, cs]) * jnp.where(is_q, qp, kp)).astype(BF16)
                pair = (((ti ^ si) >> (b.bit_length() - 1)) == 1) & is_q
                att = jnp.where(pair, _dot_nt(x, _pair_block_diag(x)), att)
            atts.append(att.astype(BF16))

    outs = []
    for h in range(GLA_HEADS):
        sl = slice(h * GLA_DK, (h + 1) * GLA_DK)
        kh = k[:, sl]
        vh = v[:, h * GLA_DV:(h + 1) * GLA_DV]
        state = st_ref[h]
        if emit:
            qd = (q[:, sl] * jnp.exp2(cum[:, sl])).astype(BF16)
            att = atts[h // 2][:, (h % 2) * GLA_DK:(h % 2 + 1) * GLA_DK]
            outs.append(_dot(qd, state.astype(BF16)) + _dot(att, vh))
        kd = (kh * jnp.exp2(rem[:, sl])).astype(BF16)
        decay = jnp.exp2(jnp.broadcast_to(last_row[:, sl], (GLA_DK, GLA_DK))).T
        decay = jnp.concatenate([decay] * (GLA_DV // GLA_DK), axis=1)
        st_ref[h] = decay * state + _dot_tn(kd, vh)

    if emit:
        o_ref[0] = jnp.concatenate(outs, axis=-1).astype(o_ref.dtype)

    @pl.when(step == pl.num_programs(1) - 1)
    def _():
        sf_ref[0] = st_ref[...]


def _gla_scan(main, gate, gate_w, s0, reverse, emit):
    b, l, _ = main.shape
    n = l // GLA_C
    amat = _gla_matrices(reverse)
    w2a, w2b, b2 = gate_w
    chunk = (lambda bi, s: (bi, n - 1 - s)) if reverse else (lambda bi, s: (bi, s))

    def col(block):
        return lambda bi, s: chunk(bi, s) + (block,)

    const = lambda arr: pl.BlockSpec(arr.shape, lambda bi, s: (0,) * arr.ndim)
    in_specs = []
    args = []
    if emit:
        in_specs.append(pl.BlockSpec((1, GLA_C, GLA_KEY_WIDTH), col(OFF_Q_GLA // GLA_KEY_WIDTH)))
        args.append(main)
    in_specs += [pl.BlockSpec((1, GLA_C, GLA_KEY_WIDTH), col(OFF_K_GLA // GLA_KEY_WIDTH)),
                 pl.BlockSpec((1, GLA_C, GLA_VAL_WIDTH), col(OFF_V_GLA // GLA_VAL_WIDTH)),
                 pl.BlockSpec((1, GLA_C, LANE), col(0)),
                 const(w2a), const(w2b), const(b2), const(amat),
                 pl.BlockSpec((1, GLA_HEADS, GLA_DK, GLA_DV), lambda bi, s: (bi, 0, 0, 0))]
    args += [main, main, gate, w2a, w2b, b2, amat, s0]
    state_spec = pl.BlockSpec((1, GLA_HEADS, GLA_DK, GLA_DV), lambda bi, s: (bi, 0, 0, 0))
    state_shape = jax.ShapeDtypeStruct((b, GLA_HEADS, GLA_DK, GLA_DV), F32)
    if emit:
        out_specs = [pl.BlockSpec((1, GLA_C, GLA_VAL_WIDTH), col(0)), state_spec]
        out_shape = [jax.ShapeDtypeStruct((b, l, GLA_VAL_WIDTH), BF16), state_shape]
    else:
        out_specs = [state_spec]
        out_shape = [state_shape]
    res = pl.pallas_call(
        functools.partial(_gla_kernel, reverse=reverse, emit=emit),
        grid=(b, n),
        in_specs=in_specs,
        out_specs=out_specs,
        out_shape=out_shape,
        scratch_shapes=[pltpu.VMEM((GLA_HEADS, GLA_DK, GLA_DV), F32)],
        compiler_params=_cparams(("parallel", "arbitrary")),
        name="gla_scan",
    )(*args)
    return (res[0], res[1]) if emit else (None, res[0])


def _gla_gate_weights(gate_w, gate_b):
    out = []
    for dr in range(2):
        w = jnp.zeros((LANE, GLA_KEY_WIDTH), F32)
        w = w.at[dr * GLA_GATE_RANK:(dr + 1) * GLA_GATE_RANK].set(gate_w[dr])
        w_hi = w.astype(BF16)
        w_lo = (w - w_hi.astype(F32)).astype(BF16)
        out.append((jnp.concatenate([w_hi, w_hi], axis=0), w_lo, gate_b[dr][None]))
    return out


def _merge_kernel(ona_ref, osc_ref, of_ref, ob_ref, r_ref, gna_ref, gsc_ref, ggl_ref, x_ref, gt_ref,
                  gn_ref, wna_ref, wsc_ref, wgl_ref, wo_ref, g2_ref, sh2_ref, sc2_ref, wr_ref,
                  xo_ref, h2_ref, hp_ref, lg_ref):
    o = of_ref[0].astype(F32) + ob_ref[0].astype(F32)
    normed = []
    for h in range(GLA_HEADS):
        oh = o[:, h * GLA_DV:(h + 1) * GLA_DV]
        ms = jnp.mean(oh * oh, axis=-1, keepdims=True)
        normed.append(oh * lax.rsqrt(ms + RMS_EPS))
    r = r_ref[0].astype(F32)
    y_gla = jnp.concatenate(normed, axis=-1) * gn_ref[...] * (r * _sigmoid(r))
    y = (_sigmoid(gna_ref[0].astype(F32)) * _dot(ona_ref[0], wna_ref[...])
         + _sigmoid(gsc_ref[0].astype(F32)) * _dot(osc_ref[0], wsc_ref[...])
         + _sigmoid(ggl_ref[0].astype(F32)) * _dot(y_gla.astype(BF16), wgl_ref[...]))
    xn = x_ref[0] + gt_ref[0] * _dot(y.astype(BF16), wo_ref[...])
    xo_ref[0] = xn
    ms = jnp.mean(xn * xn, axis=-1, keepdims=True)
    h2 = xn * lax.rsqrt(ms + RMS_EPS) * g2_ref[...] * (1.0 + sc2_ref[0]) + sh2_ref[0]
    h2b = h2.astype(BF16)
    h2_ref[0] = h2b
    _store_parts(hp_ref, _pack_rows(h2))
    lg_ref[0] = _dot(h2b, wr_ref[...])


def _merge(o_na, o_sc, o_f, o_b, main, x, gt1, gn, w_na, w_sc, w_gla, w_out, g2, sh2, sc2, w_router, tm):
    b, s, d = x.shape
    per_batch = gt1.shape[0] == b
    mod_map = (lambda bi, i: (bi, 0, 0)) if per_batch else (lambda bi, i: (0, 0, 0))
    tok = lambda width, blk: pl.BlockSpec((1, tm, width), lambda bi, i: (bi, i, blk))
    full = lambda arr: pl.BlockSpec(arr.shape, lambda bi, i: (0,) * arr.ndim)
    mod = pl.BlockSpec((1, 1, d), mod_map)
    gn_t = jnp.tile(gn, GLA_HEADS)[None]
    g2_t = g2[None]
    return pl.pallas_call(
        _merge_kernel,
        grid=(b, s // tm),
        in_specs=[tok(NA_WIDTH, 0), tok(SC_WIDTH, 0), tok(GLA_VAL_WIDTH, 0), tok(GLA_VAL_WIDTH, 0),
                  tok(d, OFF_R_GLA // d), tok(d, OFF_MERGE // d), tok(d, OFF_MERGE // d + 1),
                  tok(d, OFF_MERGE // d + 2), tok(d, 0), mod,
                  full(gn_t), full(w_na), full(w_sc), full(w_gla), full(w_out), full(g2_t), mod, mod,
                  full(w_router)],
        out_specs=[tok(d, 0), tok(d, 0),
                   pl.BlockSpec((SC_PARTS, 1, tm, SC_ROW), lambda bi, i: (0, bi, i, 0)), tok(LANE, 0)],
        out_shape=[jax.ShapeDtypeStruct((b, s, d), F32),
                   jax.ShapeDtypeStruct((b, s, d), BF16),
                   jax.ShapeDtypeStruct((SC_PARTS, b, s, SC_ROW), U32),
                   jax.ShapeDtypeStruct((b, s, LANE), F32)],
        compiler_params=_cparams(("parallel", "parallel")),
        name="merge",
    )(o_na, o_sc, o_f, o_b, main, main, main, main, x, gt1, gn_t, w_na, w_sc, w_gla, w_out, g2_t,
      sh2, sc2, w_router)


def _router_kernel(lg_ref, br_ref, tri_ref, eidx_ref, gw_ref, rank_ref, cnt_ref, carry_ref):
    tm = lg_ref.shape[0]

    @pl.when(pl.program_id(0) == 0)
    def _():
        carry_ref[...] = jnp.zeros_like(carry_ref)

    scores = _sigmoid(lg_ref[...].T[:N_EXPERTS])
    sel = scores + br_ref[...]
    neg = -jnp.inf

    sel3 = sel.reshape(N_EXPERT_GROUPS, GROUP_SIZE, tm)
    i3 = lax.broadcasted_iota(jnp.int32, sel3.shape, 1)
    m1 = sel3.max(axis=1, keepdims=True)
    first = jnp.where(sel3 == m1, i3, GROUP_SIZE).min(axis=1, keepdims=True)
    m2 = jnp.where(i3 == first, neg, sel3).max(axis=1, keepdims=True)
    gscore = (m1 + m2)[:, 0, :]

    gi = lax.broadcasted_iota(jnp.int32, gscore.shape, 0)
    gmask = jnp.zeros(gscore.shape, jnp.bool_)
    for _ in range(TOPK_GROUPS):
        m = gscore.max(axis=0, keepdims=True)
        pick = gi == jnp.where(gscore == m, gi, N_EXPERT_GROUPS).min(axis=0, keepdims=True)
        gmask = gmask | pick
        gscore = jnp.where(pick, neg, gscore)
    emask = jnp.broadcast_to(gmask[:, None, :], sel3.shape).reshape(N_EXPERTS, tm)
    sel = jnp.where(emask, sel, neg)

    ei = lax.broadcasted_iota(jnp.int32, sel.shape, 0)
    picks, idxs, ws = [], [], []
    for _ in range(TOP_K):
        m = sel.max(axis=0, keepdims=True)
        idx = jnp.where(sel == m, ei, N_EXPERTS).min(axis=0, keepdims=True)
        pick = ei == idx
        picks.append(pick)
        idxs.append(idx)
        ws.append(jnp.where(pick, scores, 0.0).sum(axis=0, keepdims=True))
        sel = jnp.where(pick, neg, sel)
    w = jnp.concatenate(ws, axis=0)
    gw_ref[...] = w / w.sum(axis=0, keepdims=True) * ROUTED_SCALE
    eidx_ref[...] = jnp.concatenate(idxs, axis=0)

    onehot = picks[0]
    for p in picks[1:]:
        onehot = onehot | p
    onehot = jnp.where(onehot, 1.0, 0.0).astype(BF16)
    before = _dot(onehot, tri_ref[...]) + jnp.tile(carry_ref[...], (1, tm // LANE))
    rank_ref[...] = jnp.concatenate(
        [jnp.where(p, before, 0.0).sum(axis=0, keepdims=True) for p in picks], axis=0).astype(jnp.int32)
    carry_ref[...] += _dot(onehot, jnp.ones((tm, LANE), BF16))
    cnt_ref[...] = carry_ref[...]


def _route(logits, b_router, tm=512):
    t = logits.shape[0]
    br = jnp.broadcast_to(b_router.astype(F32)[:, None], (N_EXPERTS, tm))
    tri = jnp.asarray(np.triu(np.ones((tm, tm), np.float32), 1), dtype=BF16)
    kt = lambda dt: jax.ShapeDtypeStruct((TOP_K, t), dt)
    eidx, gw, rank, cnt = pl.pallas_call(
        _router_kernel,
        grid=(t // tm,),
        in_specs=[pl.BlockSpec((tm, LANE), lambda i: (i, 0)),
                  pl.BlockSpec((N_EXPERTS, tm), lambda i: (0, 0)),
                  pl.BlockSpec((tm, tm), lambda i: (0, 0))],
        out_specs=[pl.BlockSpec((TOP_K, tm), lambda i: (0, i)),
                   pl.BlockSpec((TOP_K, tm), lambda i: (0, i)),
                   pl.BlockSpec((TOP_K, tm), lambda i: (0, i)),
                   pl.BlockSpec((N_EXPERTS, LANE), lambda i: (0, 0))],
        out_shape=[kt(jnp.int32), kt(F32), kt(jnp.int32),
                   jax.ShapeDtypeStruct((N_EXPERTS, LANE), F32)],
        scratch_shapes=[pltpu.VMEM((N_EXPERTS, LANE), F32)],
        compiler_params=_cparams(("arbitrary",)),
        name="router",
    )(logits, br, tri)
    return eidx, gw, rank, cnt[:, 0].astype(jnp.int32)


def _sc_mesh():
    return plsc.VectorSubcoreMesh(core_axis_name="core", subcore_axis_name="subcore")


def _dispatch_rows(xp, dest, slots):
    parts, t, _ = xp.shape
    nwin = parts * t // SC_WINDOW
    idx = dest.reshape(TOP_K, t // SC_WINDOW, SC_WINDOW).transpose(1, 0, 2)
    idx = jnp.concatenate([idx + part * slots for part in range(parts)], axis=0)

    @pl.kernel(out_type=jax.ShapeDtypeStruct((parts * slots, SC_ROW), xp.dtype), mesh=_sc_mesh(),
               scratch_types=[], name="moe_dispatch")
    def run(x_hbm, i_hbm, o_hbm):
        def body(x_vmem, i_vmem):
            for k in range(TOP_K):
                pltpu.sync_copy(x_vmem, o_hbm.at[i_vmem.at[0, k]])

        pltpu.emit_pipeline(
            body,
            grid=(nwin,),
            in_specs=[pl.BlockSpec((SC_WINDOW, SC_ROW), lambda i: (i, 0)),
                      pl.BlockSpec((1, TOP_K, SC_WINDOW), lambda i: (i, 0, 0))],
            out_specs=[],
            core_axis_name=("core", "subcore"),
            dimension_semantics=(pltpu.PARALLEL,),
        )(x_hbm, i_hbm)

    return run(xp.reshape(parts * t, SC_ROW), idx).reshape(parts, slots, SC_ROW)


def _gather_rows(yp, idx):
    n = idx.shape[0]
    parts, slots, _ = yp.shape
    idx = jnp.concatenate([idx + part * slots for part in range(parts)]).reshape(1, n * parts)

    @pl.kernel(out_type=jax.ShapeDtypeStruct((n * parts, SC_ROW), yp.dtype), mesh=_sc_mesh(),
               scratch_types=[], name="moe_gather")
    def run(y_hbm, i_hbm, o_hbm):
        def body(i_vmem, o_vmem):
            pltpu.sync_copy(y_hbm.at[i_vmem.at[0]], o_vmem)

        pltpu.emit_pipeline(
            body,
            grid=(n * parts // SC_WINDOW,),
            in_specs=[pl.BlockSpec((1, SC_WINDOW), lambda i: (0, i))],
            out_specs=[pl.BlockSpec((SC_WINDOW, SC_ROW), lambda i: (i, 0))],
            core_axis_name=("core", "subcore"),
            dimension_semantics=(pltpu.PARALLEL,),
        )(i_hbm, o_hbm)

    return run(yp.reshape(parts * slots, SC_ROW), idx).reshape(parts, n, SC_ROW)


def _expert_kernel(be_ref, bv_ref, x_ref, wg_ref, wu_ref, wd_ref, o_ref, wg_s, wu_s, wd_s):
    i = pl.program_id(0)
    valid = bv_ref[i]
    new_expert = (i == 0) | (be_ref[i] != be_ref[jnp.maximum(i - 1, 0)])

    @pl.when(new_expert)
    def _():
        wg_s[...] = wg_ref[0, 0].astype(BF16)
        wu_s[...] = wu_ref[0, 0].astype(BF16)
        wd_s[...] = wd_ref[0, 0].astype(BF16)

    @pl.when(valid > 0)
    def _():
        w = _load_parts(x_ref)
        row = lax.broadcasted_iota(jnp.int32, w.shape, 0)
        w = jnp.where(row < valid, w, jnp.uint32(0))
        lo, hi = _unpack_rows(w)
        x = jnp.concatenate([lo, hi], axis=1).astype(BF16)
        a = _dot(x, wg_s[...])
        hid = a * _sigmoid(a) * _dot(x, wu_s[...])
        _store_parts(o_ref, _pack_rows(_dot(hid.astype(BF16), wd_s[...])))


def _experts(xs, blk_e, blk_valid, layer, w_gate, w_up, w_down):
    parts, slots, _ = xs.shape
    d = D_MODEL
    nb = slots // MOE_BLOCK
    return pl.pallas_call(
        _expert_kernel,
        grid_spec=pltpu.PrefetchScalarGridSpec(
            num_scalar_prefetch=2,
            grid=(nb,),
            in_specs=[pl.BlockSpec((parts, MOE_BLOCK, SC_ROW), lambda i, be, bv: (0, i, 0)),
                      pl.BlockSpec((1, 1, d, EXPERT_FF), lambda i, be, bv: (layer, be[i], 0, 0)),
                      pl.BlockSpec((1, 1, d, EXPERT_FF), lambda i, be, bv: (layer, be[i], 0, 0)),
                      pl.BlockSpec((1, 1, EXPERT_FF, d), lambda i, be, bv: (layer, be[i], 0, 0))],
            out_specs=pl.BlockSpec((parts, MOE_BLOCK, SC_ROW), lambda i, be, bv: (0, i, 0)),
            scratch_shapes=[pltpu.VMEM((d, EXPERT_FF), BF16), pltpu.VMEM((d, EXPERT_FF), BF16),
                            pltpu.VMEM((EXPERT_FF, d), BF16)]),
        out_shape=jax.ShapeDtypeStruct((parts, slots, SC_ROW), U32),
        compiler_params=_cparams(("arbitrary",)),
        name="experts",
    )(blk_e, blk_valid, xs, w_gate, w_up, w_down)


def _combine_kernel(yg_ref, gw_ref, h_ref, x_ref, gt_ref, wsg_ref, wsu_ref, wsd_ref, gf_ref, o_ref, *, final):
    h = h_ref[0]
    a = _dot(h, wsg_ref[...])
    hid = a * _sigmoid(a) * _dot(h, wsu_ref[...])
    y = _dot(hid.astype(BF16), wsd_ref[...])
    gw = gw_ref[...]
    y_lo = y[:, :D_MODEL // 2]
    y_hi = y[:, D_MODEL // 2:]
    for k in range(TOP_K):
        lo, hi = _unpack_rows(_load_parts(yg_ref, k))
        y_lo = y_lo + gw[:, k:k + 1] * lo
        y_hi = y_hi + gw[:, k:k + 1] * hi
    y = jnp.concatenate([y_lo, y_hi], axis=1)
    xn = x_ref[0] + gt_ref[0] * y
    if final:
        ms = jnp.mean(xn * xn, axis=-1, keepdims=True)
        xn = xn * lax.rsqrt(ms + RMS_EPS) * gf_ref[...]
    o_ref[0] = xn


def _combine(yg, gw, tok_off, h2, x, gt2, b0, nb, ws_gate, ws_up, ws_down, g_final, final, tm):
    b, s, d = x.shape
    per_batch = gt2.shape[0] == b
    mod_map = (lambda bi, i: (b0 + bi, 0, 0)) if per_batch else (lambda bi, i: (0, 0, 0))
    full = lambda arr: pl.BlockSpec(arr.shape, lambda bi, i: (0,) * arr.ndim)
    tok = lambda width: pl.BlockSpec((1, tm, width), lambda bi, i: (b0 + bi, i, 0))
    gf = g_final[None]
    nblk = s // tm
    blk0 = (tok_off + b0 * s) // tm
    return pl.pallas_call(
        functools.partial(_combine_kernel, final=final),
        grid=(nb, nblk),
        in_specs=[pl.BlockSpec((SC_PARTS, TOP_K, tm, SC_ROW), lambda bi, i: (0, 0, bi * nblk + i, 0)),
                  pl.BlockSpec((tm, TOP_K), lambda bi, i: (blk0 + bi * nblk + i, 0)),
                  tok(d), tok(d),
                  pl.BlockSpec((1, 1, d), mod_map),
                  full(ws_gate), full(ws_up), full(ws_down), full(gf)],
        out_specs=tok(d),
        out_shape=jax.ShapeDtypeStruct((b, s, d), F32),
        input_output_aliases={3: 0},
        compiler_params=_cparams(("parallel", "parallel")),
        name="combine",
    )(yg, gw, h2, x, gt2, ws_gate, ws_up, ws_down, gf)


def _layer(x, ctx_s, mods, mods_ctx, p, ctx_out, final, g_final):
    b, s, d = x.shape
    sc = ctx_s.shape[1]
    sh1, sc1, gt1, sh2, sc2, gt2 = mods
    csh1, csc1, cgt1, csh2, csc2, cgt2 = mods_ctx

    w_main, w_gate = _prep_w_in(p['w_in'], p['layer'])
    main, gate = _proj_in(x, p['g_norm1'], sh1, sc1, w_main, w_gate, tm=min(1024, s), tn=1024)
    ctx_flat = ctx_s.reshape(1, b * sc, d)
    n_ctx, tn_ctx = (N_MAIN, 1024) if ctx_out else (N_KV_MAIN, N_KV_MAIN // 2)
    main_c, gate_c = _proj_in(ctx_flat, p['g_norm1'], csh1, csc1, w_main, w_gate, tm=min(1024, b * sc),
                              tn=tn_ctx, n=n_ctx)
    main_c = main_c.reshape(b, sc, n_ctx)
    gate_c = gate_c.reshape(b, sc, LANE)

    o_na = _na_latent(main, main_c, p['na_rpb'])
    o_sc = _short_conv(main, p['conv_w'])

    gw_f, gw_b = _gla_gate_weights(p['gla_gate_w'], p['gla_gate_b'])
    s0 = jnp.zeros((b, GLA_HEADS, GLA_DK, GLA_DV), F32)
    o_cf, st_f = _gla_scan(main_c, gate_c, gw_f, s0, False, ctx_out)
    o_cb, st_b = _gla_scan(main_c, gate_c, gw_b, s0, True, ctx_out)
    o_f, _ = _gla_scan(main, gate, gw_f, st_f, False, True)
    o_b, _ = _gla_scan(main, gate, gw_b, st_b, True, True)

    w_na = p['w_branch_na'].astype(BF16)
    w_sc = p['w_branch_sc'].astype(BF16)
    w_gla = p['w_branch_gla'].astype(BF16)
    w_out = p['w_out'].astype(BF16)
    w_router = jnp.pad(p['w_router'], ((0, 0), (0, LANE - N_EXPERTS))).astype(BF16)
    x, h2, h2p, logits = _merge(o_na, o_sc, o_f, o_b, main, x, gt1, p['gla_norm_g'], w_na, w_sc, w_gla, w_out,
                                p['g_norm2'], sh2, sc2, w_router, tm=min(512, s))
    n_lat = b * s
    hp_all = h2p.reshape(SC_PARTS, n_lat, SC_ROW)
    lg_all = logits.reshape(n_lat, LANE)
    if ctx_out:
        o_na_c = _dense_attn(main_c)
        o_sc_c = _short_conv(main_c, p['conv_w'])
        ctx_s, h2_c, h2p_c, lg_c = _merge(o_na_c, o_sc_c, o_cf, o_cb, main_c, ctx_s, cgt1, p['gla_norm_g'],
                                          w_na, w_sc, w_gla, w_out, p['g_norm2'], csh2, csc2, w_router,
                                          tm=min(256, sc))
        hp_all = jnp.concatenate([hp_all, h2p_c.reshape(SC_PARTS, b * sc, SC_ROW)], axis=1)
        lg_all = jnp.concatenate([lg_all, lg_c.reshape(b * sc, LANE)], axis=0)

    t = hp_all.shape[1]
    eidx, gw, rank, counts = _route(lg_all, p['b_router'])
    padded = (counts + MOE_BLOCK - 1) // MOE_BLOCK * MOE_BLOCK
    pad_end = jnp.cumsum(padded)
    pad_start = pad_end - padded
    onehot = eidx[:, :, None] == jnp.arange(N_EXPERTS, dtype=jnp.int32)
    dest = jnp.sum(jnp.where(onehot, pad_start, 0), axis=-1) + rank
    n_blocks = -(-(t * TOP_K + N_EXPERTS * (MOE_BLOCK - 1)) // MOE_BLOCK)
    slots = n_blocks * MOE_BLOCK
    blk_start = jnp.arange(n_blocks, dtype=jnp.int32) * MOE_BLOCK
    blk_e = jnp.minimum(jnp.sum(pad_end[None, :] <= blk_start[:, None], axis=1), N_EXPERTS - 1).astype(jnp.int32)
    used_end = (pad_start + counts)[blk_e]
    blk_valid = jnp.clip(used_end - blk_start, 0, MOE_BLOCK).astype(jnp.int32)

    xs = _dispatch_rows(hp_all, dest, slots)
    ys = _experts(xs, blk_e, blk_valid, p['layer'], p['w_exp_gate'], p['w_exp_up'], p['w_exp_down'])
    gw_t = gw.T
    ws_gate = p['w_sh_gate'].astype(BF16)
    ws_up = p['w_sh_up'].astype(BF16)
    ws_down = p['w_sh_down'].astype(BF16)

    def gathered(t0, n):
        return _gather_rows(ys, dest[:, t0:t0 + n].reshape(-1)).reshape(SC_PARTS, TOP_K, n, SC_ROW)

    pieces = next(n for n in (4, 2, 1) if b % n == 0)
    nb = b // pieces
    for q in range(pieces):
        x = _combine(gathered(q * nb * s, nb * s), gw_t, 0, h2, x, gt2, q * nb, nb, ws_gate, ws_up, ws_down,
                     g_final, final, tm=min(256, s))
    if ctx_out:
        ctx_s = _combine(gathered(n_lat, b * sc), gw_t, n_lat, h2_c, ctx_s, cgt2, 0, b, ws_gate, ws_up, ws_down,
                         g_final, False, tm=min(256, sc))
    return x, ctx_s


def kernel(x, c, ctx, c_ctx, w_mod, b_mod, g_norm1, g_norm2, w_in, na_rpb, w_branch_na, conv_w, w_branch_sc,
           gla_gate_w, gla_gate_b, gla_norm_g, w_branch_gla, w_out, w_router, b_router, w_exp_gate, w_exp_up,
           w_exp_down, w_sh_gate, w_sh_up, w_sh_down, g_final):
    stacked = dict(g_norm1=g_norm1, g_norm2=g_norm2, na_rpb=na_rpb, w_branch_na=w_branch_na,
                   conv_w=conv_w, w_branch_sc=w_branch_sc, gla_gate_w=gla_gate_w, gla_gate_b=gla_gate_b,
                   gla_norm_g=gla_norm_g, w_branch_gla=w_branch_gla, w_out=w_out, w_router=w_router,
                   b_router=b_router,
                   w_sh_gate=w_sh_gate, w_sh_up=w_sh_up, w_sh_down=w_sh_down)
    depth = w_in.shape[0]
    ctx_s = ctx
    for i in range(depth):
        p = {name: arr[i] for name, arr in stacked.items()}
        p.update(layer=i, w_in=w_in, w_exp_gate=w_exp_gate, w_exp_up=w_exp_up, w_exp_down=w_exp_down)
        mods, mods_ctx = _mod_vectors(c, c_ctx, w_mod[i], b_mod[i])
        last = i == depth - 1
        x, ctx_s = _layer(x, ctx_s, mods, mods_ctx, p, not last, last, g_final)
    return x
```

```python
import functools

import numpy as np
import jax
import jax.numpy as jnp
from jax import lax
from jax.experimental import pallas as pl
from jax.experimental.pallas import tpu as pltpu
from jax.experimental.pallas import tpu_sc as plsc

F32 = jnp.float32
BF16 = jnp.bfloat16
U32 = jnp.uint32

D_MODEL = 1024
N_MOD = 6
RMS_EPS = 1e-6
NEG_INF = -1e30
GRID_W = 64
NA_HEADS = 8
NA_HEAD_DIM = 64
NA_WIDTH = NA_HEADS * NA_HEAD_DIM
NA_WIN_R = 8
NA_WIN_C = 16
NA_GROUP = 4
SC_WIDTH = 512
GLA_HEADS = 4
GLA_KEY_WIDTH = 512
GLA_VAL_WIDTH = 1024
GLA_DK = GLA_KEY_WIDTH // GLA_HEADS
GLA_DV = GLA_VAL_WIDTH // GLA_HEADS
GLA_GATE_RANK = 16
GLA_GATE_TAU = 16.0
LOG2_E = 1.4426950408889634
N_EXPERTS = 64
N_EXPERT_GROUPS = 8
GROUP_SIZE = N_EXPERTS // N_EXPERT_GROUPS
TOPK_GROUPS = 4
TOP_K = 8
EXPERT_FF = 256
ROUTED_SCALE = 2.5
MOE_BLOCK = 512

LANE = 128
GLA_C = 128
GLA_LEVELS = tuple(GLA_C >> (i + 1) for i in range(GLA_C.bit_length() - 1))
VMEM_LIMIT = 48 * 1024 * 1024
SC_WINDOW = 128
SC_ROW = 256
SC_PARTS = D_MODEL // 2 // SC_ROW

OFF_V_GLA = 0
OFF_K_NA = 1024
OFF_V_NA = 1536
OFF_K_GLA = 2048
N_KV_MAIN = 2560
OFF_Q_NA = 2560
OFF_B_SC = 3072
OFF_C_SC = 3584
OFF_X_SC = 4096
OFF_Q_GLA = 4608
OFF_R_GLA = 5120
OFF_MERGE = 6144
N_MAIN = 9216


def _cparams(sem, vmem=VMEM_LIMIT):
    return pltpu.CompilerParams(dimension_semantics=sem, vmem_limit_bytes=vmem)


def _dot(a, b):
    return jnp.dot(a, b, preferred_element_type=F32)


def _dot_nt(a, b):
    return lax.dot_general(a, b, (((1,), (1,)), ((), ())), preferred_element_type=F32)


def _dot_tn(a, b):
    return lax.dot_general(a, b, (((0,), (0,)), ((), ())), preferred_element_type=F32)


def _sigmoid(x):
    return 1.0 / (1.0 + jnp.exp(-x))


def _pack_rows(x):
    n = x.shape[1] // 2
    r = x.astype(BF16).astype(F32)
    lo = pltpu.bitcast(r[:, :n], U32) >> 16
    hi = pltpu.bitcast(r[:, n:], U32)
    return hi | lo


def _store_parts(ref, words):
    for part in range(SC_PARTS):
        dst = ref.at[part, 0] if len(ref.shape) == 4 else ref.at[part]
        dst[...] = words[:, part * SC_ROW:(part + 1) * SC_ROW]


def _load_parts(ref, *lead):
    return jnp.concatenate([ref[(part,) + lead] for part in range(SC_PARTS)], axis=-1)


def _unpack_rows(w):
    lo = pltpu.bitcast(w << 16, F32)
    hi = pltpu.bitcast(w & jnp.uint32(0xFFFF0000), F32)
    return lo, hi


def _mod_kernel(a_ref, w_ref, b_ref, o_ref):
    a = a_ref[...]
    a = a * _sigmoid(a)
    o_ref[...] = _dot(a.astype(BF16), w_ref[...].astype(BF16)) + b_ref[...]


def _mod_vectors(c, c_ctx, w_mod, b_mod):
    b = c.shape[0]
    rows = -(-(b + 1) // 8) * 8
    a = jnp.concatenate([c, c_ctx[None], jnp.zeros((rows - b - 1, D_MODEL), F32)], axis=0)
    n = N_MOD * D_MODEL
    tn = 1536
    out = pl.pallas_call(
        _mod_kernel,
        grid=(n // tn,),
        in_specs=[pl.BlockSpec((rows, D_MODEL), lambda j: (0, 0)),
                  pl.BlockSpec((D_MODEL, tn), lambda j: (0, j)),
                  pl.BlockSpec((1, tn), lambda j: (0, j))],
        out_specs=pl.BlockSpec((rows, tn), lambda j: (0, j)),
        out_shape=jax.ShapeDtypeStruct((rows, n), F32),
        compiler_params=_cparams(("parallel",)),
        name="mod_vectors",
    )(a, w_mod, b_mod[None])
    lat = out[:b].reshape(b, N_MOD, 1, D_MODEL)
    ctx = out[b].reshape(N_MOD, 1, 1, D_MODEL)
    return [lat[:, i] for i in range(N_MOD)], [ctx[i] for i in range(N_MOD)]


def _proj_kernel(x_ref, g_ref, sh_ref, sc_ref, w_ref, wg_ref, o_ref, og_ref, h_ref):
    @pl.when(pl.program_id(2) == 0)
    def _():
        x = x_ref[0]
        ms = jnp.mean(x * x, axis=-1, keepdims=True)
        h = x * lax.rsqrt(ms + RMS_EPS) * g_ref[...] * (1.0 + sc_ref[0]) + sh_ref[0]
        hb = h.astype(BF16)
        h_ref[...] = hb
        og_ref[0] = _dot(hb, wg_ref[...])

    o_ref[0] = _dot(h_ref[...], w_ref[...]).astype(o_ref.dtype)


W_IN_TILE = 512
W_IN_GATE_SHIFT = 2 * GLA_GATE_RANK


def _prep_w_in_kernel(a_ref, b_ref, o_ref, g_ref):
    t = pl.program_id(0)
    first_lat = N_KV_MAIN // W_IN_TILE
    a = a_ref[0]

    @pl.when(t < first_lat)
    def _():
        o_ref[...] = a.astype(BF16)

    @pl.when(t >= first_lat)
    def _():
        cat = jnp.concatenate([a, b_ref[0]], axis=1)
        moved = pltpu.roll(cat, cat.shape[1] - W_IN_GATE_SHIFT, axis=1)[:, :W_IN_TILE]
        scale = jnp.where(t == first_lat, NA_HEAD_DIM ** -0.5, 1.0)
        o_ref[...] = (moved * scale).astype(BF16)

    @pl.when(t == first_lat)
    def _():
        head = a[:, :LANE]
        lane = lax.broadcasted_iota(jnp.int32, head.shape, 1)
        g_ref[...] = jnp.where(lane < W_IN_GATE_SHIFT, head, 0.0).astype(BF16)


def _prep_w_in(w_in, layer):
    d = w_in.shape[1]
    first_lat = N_KV_MAIN // W_IN_TILE
    kv_perm = OFF_K_NA // W_IN_TILE

    def a_map(t):
        return (layer, 0, jnp.where(t < first_lat, (t + first_lat - kv_perm) % first_lat, t))

    def b_map(t):
        return (layer, 0, jnp.where(t < first_lat, 0, (t + 1) * (W_IN_TILE // LANE)))

    return pl.pallas_call(
        _prep_w_in_kernel,
        grid=(N_MAIN // W_IN_TILE,),
        in_specs=[pl.BlockSpec((1, d, W_IN_TILE), a_map),
                  pl.BlockSpec((1, d, LANE), b_map)],
        out_specs=[pl.BlockSpec((d, W_IN_TILE), lambda t: (0, t)),
                   pl.BlockSpec((d, LANE), lambda t: (0, 0))],
        out_shape=[jax.ShapeDtypeStruct((d, N_MAIN), BF16), jax.ShapeDtypeStruct((d, LANE), BF16)],
        compiler_params=_cparams(("arbitrary",)),
        name="prep_w_in",
    )(w_in, w_in)


def _proj_in(x, g, shift, scale, w_main, w_gate, tm, tn, n=None):
    b, s, d = x.shape
    n = w_main.shape[1] if n is None else n
    per_batch = shift.shape[0] == b
    mod_map = (lambda bi, i, j: (bi, 0, 0)) if per_batch else (lambda bi, i, j: (0, 0, 0))
    return pl.pallas_call(
        _proj_kernel,
        grid=(b, s // tm, n // tn),
        in_specs=[pl.BlockSpec((1, tm, d), lambda bi, i, j: (bi, i, 0)),
                  pl.BlockSpec((1, d), lambda bi, i, j: (0, 0)),
                  pl.BlockSpec((1, 1, d), mod_map),
                  pl.BlockSpec((1, 1, d), mod_map),
                  pl.BlockSpec((d, tn), lambda bi, i, j: (0, j)),
                  pl.BlockSpec((d, LANE), lambda bi, i, j: (0, 0))],
        out_specs=[pl.BlockSpec((1, tm, tn), lambda bi, i, j: (bi, i, j)),
                   pl.BlockSpec((1, tm, LANE), lambda bi, i, j: (bi, i, 0))],
        out_shape=[jax.ShapeDtypeStruct((b, s, n), BF16),
                   jax.ShapeDtypeStruct((b, s, LANE), F32)],
        scratch_shapes=[pltpu.VMEM((tm, d), BF16)],
        compiler_params=_cparams(("parallel", "parallel", "arbitrary")),
        name="proj_in",
    )(x, g[None], shift, scale, w_main, w_gate)


def _softmax_av(q, keys, vals, biases):
    scores = []
    for kk, bb in zip(keys, biases):
        s = _dot_nt(q, kk)
        scores.append(s if bb is None else s + bb)
    m = scores[0].max(axis=-1, keepdims=True)
    for s in scores[1:]:
        m = jnp.maximum(m, s.max(axis=-1, keepdims=True))
    num = None
    den = None
    for s, vv in zip(scores, vals):
        e = jnp.exp(s - m)
        dsum = e.sum(axis=-1, keepdims=True)
        o = _dot(e.astype(BF16), vv)
        num = o if num is None else num + o
        den = dsum if den is None else den + dsum
    return num / den


def _na_kernel(q_ref, k_ref, v_ref, kc_ref, vc_ref, *rest, rows, kr):
    *bias_refs, o_ref = rest
    kc = kc_ref[0]
    vc = vc_ref[0]
    for j, bias_ref in enumerate(bias_refs):
        r = pl.program_id(1) * len(bias_refs) + j
        row_start = jnp.clip(r - kr // 2, 0, rows - kr)
        start = pl.multiple_of(row_start * GRID_W, GRID_W)
        n_win = kr * GRID_W
        q = q_ref[0, j * GRID_W:(j + 1) * GRID_W, :]
        kw = k_ref[0, pl.ds(start, n_win), :]
        vw = v_ref[0, pl.ds(start, n_win), :]
        o_ref[0, j * GRID_W:(j + 1) * GRID_W, :] = _na_row(q, kw, vw, kc, vc, bias_ref).astype(o_ref.dtype)


def _na_row(q, kw, vw, kc, vc, bias_ref):
    gw = NA_GROUP * NA_HEAD_DIM
    stacked = (NA_GROUP * GRID_W, gw)
    on_head = (lax.broadcasted_iota(jnp.int32, stacked, 0) // GRID_W
               == lax.broadcasted_iota(jnp.int32, stacked, 1) // NA_HEAD_DIM)
    outs = []
    for g in range(NA_HEADS // NA_GROUP):
        sl = slice(g * gw, (g + 1) * gw)
        q_all = jnp.where(on_head, jnp.concatenate([q[:, sl]] * NA_GROUP, axis=0), jnp.zeros((), q.dtype))
        bias = bias_ref[0, g * NA_GROUP * GRID_W:(g + 1) * NA_GROUP * GRID_W, :]
        o_all = _softmax_av(q_all, [kw[:, sl], kc[:, sl]], [vw[:, sl], vc[:, sl]], [bias, None])
        o_all = jnp.where(on_head, o_all, 0.0).reshape(NA_GROUP, GRID_W, gw)
        outs.append(o_all.sum(axis=0))
    return jnp.concatenate(outs, axis=-1)


def _na_bias_table(rpb, rows, kr):
    col = np.arange(GRID_W)
    col_start = np.clip(col - NA_WIN_C // 2, 0, GRID_W - NA_WIN_C)
    col_ok = (col[None, :] >= col_start[:, None]) & (col[None, :] < col_start[:, None] + NA_WIN_C)
    d_col = np.clip(col[None, :] - col[:, None], -(NA_WIN_C - 1), NA_WIN_C - 1) + NA_WIN_C - 1
    n_dr, n_dc = rpb.shape[1], rpb.shape[2]
    onehot = jnp.asarray((d_col.reshape(-1)[None, :] == np.arange(n_dc)[:, None]).astype(np.float32))
    by_col = jnp.dot(rpb.astype(F32).reshape(NA_HEADS * n_dr, n_dc), onehot, precision=lax.Precision.HIGHEST)
    by_col = by_col.reshape(NA_HEADS, n_dr, GRID_W, GRID_W)
    by_col = jnp.where(col_ok[None, None], by_col, NEG_INF)
    tables = []
    for o in range(kr):
        lo = NA_WIN_R - 1 - o
        tables.append(by_col[:, lo:lo + kr].transpose(0, 2, 1, 3).reshape(NA_HEADS, GRID_W, kr * GRID_W))
    return jnp.stack(tables).reshape(kr, NA_HEADS * GRID_W, kr * GRID_W)


def _na_latent(main, main_ctx, rpb):
    b, s, _ = main.shape
    sc = main_ctx.shape[1]
    rows = s // GRID_W
    kr = min(NA_WIN_R, rows)
    bias = _na_bias_table(rpb, rows, kr)
    w = NA_WIDTH

    per_step = next(n for n in (4, 2, 1) if rows % n == 0)

    def bias_spec(j):
        def bias_map(bi, i):
            r = i * per_step + j
            return (r - jnp.clip(r - kr // 2, 0, rows - kr), 0, 0)
        return pl.BlockSpec((1, NA_HEADS * GRID_W, kr * GRID_W), bias_map)

    return pl.pallas_call(
        functools.partial(_na_kernel, rows=rows, kr=kr),
        grid=(b, rows // per_step),
        in_specs=[pl.BlockSpec((1, per_step * GRID_W, w), lambda bi, i: (bi, i, OFF_Q_NA // w)),
                  pl.BlockSpec((1, s, w), lambda bi, i: (bi, 0, OFF_K_NA // w)),
                  pl.BlockSpec((1, s, w), lambda bi, i: (bi, 0, OFF_V_NA // w)),
                  pl.BlockSpec((1, sc, w), lambda bi, i: (bi, 0, OFF_K_NA // w)),
                  pl.BlockSpec((1, sc, w), lambda bi, i: (bi, 0, OFF_V_NA // w))]
                 + [bias_spec(j) for j in range(per_step)],
        out_specs=pl.BlockSpec((1, per_step * GRID_W, w), lambda bi, i: (bi, i, 0)),
        out_shape=jax.ShapeDtypeStruct((b, s, w), BF16),
        compiler_params=_cparams(("parallel", "arbitrary")),
        name="na_latent",
    )(main, main, main, main_ctx, main_ctx, *([bias] * per_step))


def _dense_attn_kernel(q_ref, k_ref, v_ref, o_ref):
    q = q_ref[0]
    k = k_ref[0]
    v = v_ref[0]
    outs = []
    for h in range(NA_HEADS):
        sl = slice(h * NA_HEAD_DIM, (h + 1) * NA_HEAD_DIM)
        outs.append(_softmax_av(q[:, sl], [k[:, sl]], [v[:, sl]], [None]))
    o_ref[0] = jnp.concatenate(outs, axis=-1).astype(o_ref.dtype)


def _dense_attn(main_ctx):
    b, sc, _ = main_ctx.shape
    w = NA_WIDTH
    return pl.pallas_call(
        _dense_attn_kernel,
        grid=(b,),
        in_specs=[pl.BlockSpec((1, sc, w), lambda bi: (bi, 0, OFF_Q_NA // w)),
                  pl.BlockSpec((1, sc, w), lambda bi: (bi, 0, OFF_K_NA // w)),
                  pl.BlockSpec((1, sc, w), lambda bi: (bi, 0, OFF_V_NA // w))],
        out_specs=pl.BlockSpec((1, sc, w), lambda bi: (bi, 0, 0)),
        out_shape=jax.ShapeDtypeStruct((b, sc, w), BF16),
        compiler_params=_cparams(("parallel",)),
        name="ctx_attn",
    )(main_ctx, main_ctx, main_ctx)


def _conv_kernel(b_ref, c_ref, x_ref, w_ref, o_ref):
    u = c_ref[0].astype(F32) * x_ref[0].astype(F32)
    s = u.shape[0]
    t = lax.broadcasted_iota(jnp.int32, u.shape, 0)
    prev = jnp.where(t == 0, 0.0, pltpu.roll(u, 1, axis=0))
    nxt = jnp.where(t == s - 1, 0.0, pltpu.roll(u, s - 1, axis=0))
    w = w_ref[...]
    y = b_ref[0].astype(F32) * (prev * w[0:1] + u * w[1:2] + nxt * w[2:3])
    o_ref[0] = y.astype(o_ref.dtype)


def _short_conv(main, conv_w):
    b, s, _ = main.shape
    nt = SC_WIDTH // LANE
    return pl.pallas_call(
        _conv_kernel,
        grid=(b, nt),
        in_specs=[pl.BlockSpec((1, s, LANE), lambda bi, c: (bi, 0, OFF_B_SC // LANE + c)),
                  pl.BlockSpec((1, s, LANE), lambda bi, c: (bi, 0, OFF_C_SC // LANE + c)),
                  pl.BlockSpec((1, s, LANE), lambda bi, c: (bi, 0, OFF_X_SC // LANE + c)),
                  pl.BlockSpec((3, LANE), lambda bi, c: (0, c))],
        out_specs=pl.BlockSpec((1, s, LANE), lambda bi, c: (bi, 0, c)),
        out_shape=jax.ShapeDtypeStruct((b, s, SC_WIDTH), BF16),
        compiler_params=_cparams(("parallel", "parallel")),
        name="short_conv",
    )(main, main, main, conv_w)


def _gla_matrices(reverse):
    c = GLA_C
    t = np.arange(c)[:, None]
    m = np.arange(c)[None, :]
    blocks = [m <= t, m > t]
    for b in GLA_LEVELS:
        first = (t // (2 * b)) * (2 * b) + b
        is_q = (t & b) != 0
        blocks.append(np.where(is_q, (m > first) & (m <= t), (m > t) & (m <= first)))
    mats = np.stack(blocks).astype(np.float32)
    if reverse:
        mats = mats[:, ::-1, ::-1]
    mats = mats.reshape(-1, c)
    return jnp.asarray(np.concatenate([mats, mats], axis=1), dtype=BF16)


def _split_bf16(x):
    hi = x.astype(BF16)
    return hi, (x - hi.astype(F32)).astype(BF16)


def _pair_block_diag(x):
    lane = lax.broadcasted_iota(jnp.int32, x.shape, 1)
    zero = jnp.zeros((), x.dtype)
    return jnp.concatenate([jnp.where(lane < GLA_DK, x, zero), jnp.where(lane >= GLA_DK, x, zero)], axis=0)


def _gla_kernel(*refs, emit):
    n_in = 9 if emit else 8
    n_out = 2 if emit else 1
    ins, outs, scratch = refs[:2 * n_in], refs[2 * n_in:2 * (n_in + n_out)], refs[2 * (n_in + n_out):]
    for d, reverse in enumerate((False, True)):
        _gla_direction(ins[d * n_in:(d + 1) * n_in], outs[d * n_out:(d + 1) * n_out], scratch[d], reverse, emit)


def _gla_direction(ins, outs, st_ref, reverse, emit):
    if emit:
        q_ref, k_ref, v_ref, gt_ref, w2a_ref, w2b_ref, b2_ref, a_ref, s0_ref = ins
        o_ref, sf_ref = outs
    else:
        k_ref, v_ref, gt_ref, w2a_ref, w2b_ref, b2_ref, a_ref, s0_ref = ins
        (sf_ref,) = outs
    c = GLA_C
    pw = 2 * GLA_DK
    step = pl.program_id(1)

    @pl.when(step == 0)
    def _():
        st_ref[...] = s0_ref[0]

    lr_hi, lr_lo = _split_bf16(gt_ref[0])
    logit = (_dot(jnp.concatenate([lr_hi, lr_lo], axis=1), w2a_ref[...]) + _dot(lr_hi, w2b_ref[...])
             + b2_ref[...])
    g = (jnp.minimum(logit, 0.0) - jnp.log1p(jnp.exp(-jnp.abs(logit)))) * (LOG2_E / GLA_GATE_TAU)
    g_hi, g_lo = _split_bf16(g)
    args = _dot(a_ref[...], jnp.concatenate([g_hi, g_lo], axis=0))
    cum = args[0:c]
    rem = args[c:2 * c]
    last_row = cum[0:1] if reverse else cum[c - 1:c]

    k = k_ref[0].astype(F32)
    v = v_ref[0]
    atts = []
    if emit:
        q = q_ref[0].astype(F32) * (GLA_DK ** -0.5)
        ti = lax.broadcasted_iota(jnp.int32, (c, pw), 0)
        si = lax.broadcasted_iota(jnp.int32, (c, pw), 1) & (GLA_DK - 1)
        if reverse:
            ti, si = c - 1 - ti, c - 1 - si
        for hp in range(GLA_HEADS // 2):
            cs = slice(hp * pw, (hp + 1) * pw)
            qp, kp = q[:, cs], k[:, cs]
            att = jnp.where(ti == si, _dot_nt(qp.astype(BF16), _pair_block_diag(kp.astype(BF16))), 0.0)
            for l, b in enumerate(GLA_LEVELS):
                is_q = (ti & b) != 0
                x = (jnp.exp2(args[(2 + l) * c:(3 + l) * c, cs]) * jnp.where(is_q, qp, kp)).astype(BF16)
                pair = (((ti ^ si) >> (b.bit_length() - 1)) == 1) & is_q
                att = jnp.where(pair, _dot_nt(x, _pair_block_diag(x)), att)
            atts.append(att.astype(BF16))

    outs = []
    for h in range(GLA_HEADS):
        sl = slice(h * GLA_DK, (h + 1) * GLA_DK)
        kh = k[:, sl]
        vh = v[:, h * GLA_DV:(h + 1) * GLA_DV]
        state = st_ref[h]
        if emit:
            qd = (q[:, sl] * jnp.exp2(cum[:, sl])).astype(BF16)
            att = atts[h // 2][:, (h % 2) * GLA_DK:(h % 2 + 1) * GLA_DK]
            outs.append(_dot(qd, state.astype(BF16)) + _dot(att, vh))
        kd = (kh * jnp.exp2(rem[:, sl])).astype(BF16)
        decay = jnp.exp2(jnp.broadcast_to(last_row[:, sl], (GLA_DK, GLA_DK))).T
        decay = jnp.concatenate([decay] * (GLA_DV // GLA_DK), axis=1)
        st_ref[h] = decay * state + _dot_tn(kd, vh)

    if emit:
        o_ref[0] = jnp.concatenate(outs, axis=-1).astype(o_ref.dtype)

    @pl.when(step == pl.num_programs(1) - 1)
    def _():
        sf_ref[0] = st_ref[...]


def _gla_scan(main, gate, gate_ws, s0s, emit):
    b, l, _ = main.shape
    n = l // GLA_C
    const = lambda arr: pl.BlockSpec(arr.shape, lambda bi, s: (0,) * arr.ndim)
    state_spec = pl.BlockSpec((1, GLA_HEADS, GLA_DK, GLA_DV), lambda bi, s: (bi, 0, 0, 0))
    state_shape = jax.ShapeDtypeStruct((b, GLA_HEADS, GLA_DK, GLA_DV), F32)
    in_specs, args, out_specs, out_shape = [], [], [], []
    for reverse in (False, True):
        amat = _gla_matrices(reverse)
        w2a, w2b, b2 = gate_ws[reverse]

        def col(block, reverse=reverse):
            return lambda bi, s: (bi, n - 1 - s if reverse else s, block)

        if emit:
            in_specs.append(pl.BlockSpec((1, GLA_C, GLA_KEY_WIDTH), col(OFF_Q_GLA // GLA_KEY_WIDTH)))
            args.append(main)
            out_specs.append(pl.BlockSpec((1, GLA_C, GLA_VAL_WIDTH), col(0)))
            out_shape.append(jax.ShapeDtypeStruct((b, l, GLA_VAL_WIDTH), BF16))
        in_specs += [pl.BlockSpec((1, GLA_C, GLA_KEY_WIDTH), col(OFF_K_GLA // GLA_KEY_WIDTH)),
                     pl.BlockSpec((1, GLA_C, GLA_VAL_WIDTH), col(OFF_V_GLA // GLA_VAL_WIDTH)),
                     pl.BlockSpec((1, GLA_C, LANE), col(0)),
                     const(w2a), const(w2b), const(b2), const(amat), state_spec]
        args += [main, main, gate, w2a, w2b, b2, amat, s0s[reverse]]
        out_specs.append(state_spec)
        out_shape.append(state_shape)
    res = pl.pallas_call(
        functools.partial(_gla_kernel, emit=emit),
        grid=(b, n),
        in_specs=in_specs,
        out_specs=out_specs,
        out_shape=out_shape,
        scratch_shapes=[pltpu.VMEM((GLA_HEADS, GLA_DK, GLA_DV), F32)] * 2,
        compiler_params=_cparams(("parallel", "arbitrary")),
        name="gla_scan",
    )(*args)
    return (res[0], res[2], res[1], res[3]) if emit else (None, None, res[0], res[1])


def _gla_gate_weights(gate_w, gate_b):
    out = []
    for dr in range(2):
        w = jnp.zeros((LANE, GLA_KEY_WIDTH), F32)
        w = w.at[dr * GLA_GATE_RANK:(dr + 1) * GLA_GATE_RANK].set(gate_w[dr])
        w_hi = w.astype(BF16)
        w_lo = (w - w_hi.astype(F32)).astype(BF16)
        out.append((jnp.concatenate([w_hi, w_hi], axis=0), w_lo, gate_b[dr][None]))
    return out


def _merge_kernel(ona_ref, osc_ref, of_ref, ob_ref, r_ref, gna_ref, gsc_ref, ggl_ref, x_ref, gt_ref,
                  gn_ref, wna_ref, wsc_ref, wgl_ref, wo_ref, g2_ref, sh2_ref, sc2_ref, wr_ref,
                  xo_ref, h2_ref, hp_ref, lg_ref):
    o = of_ref[0].astype(F32) + ob_ref[0].astype(F32)
    normed = []
    for h in range(GLA_HEADS):
        oh = o[:, h * GLA_DV:(h + 1) * GLA_DV]
        ms = jnp.mean(oh * oh, axis=-1, keepdims=True)
        normed.append(oh * lax.rsqrt(ms + RMS_EPS))
    r = r_ref[0].astype(F32)
    y_gla = jnp.concatenate(normed, axis=-1) * gn_ref[...] * (r * _sigmoid(r))
    y = (_sigmoid(gna_ref[0].astype(F32)) * _dot(ona_ref[0], wna_ref[...])
         + _sigmoid(gsc_ref[0].astype(F32)) * _dot(osc_ref[0], wsc_ref[...])
         + _sigmoid(ggl_ref[0].astype(F32)) * _dot(y_gla.astype(BF16), wgl_ref[...]))
    xn = x_ref[0] + gt_ref[0] * _dot(y.astype(BF16), wo_ref[...])
    xo_ref[0] = xn
    ms = jnp.mean(xn * xn, axis=-1, keepdims=True)
    h2 = xn * lax.rsqrt(ms + RMS_EPS) * g2_ref[...] * (1.0 + sc2_ref[0]) + sh2_ref[0]
    h2b = h2.astype(BF16)
    h2_ref[0] = h2b
    _store_parts(hp_ref, _pack_rows(h2))
    lg_ref[0] = _dot(h2b, wr_ref[...])


def _merge(o_na, o_sc, o_f, o_b, main, x, gt1, gn, w_na, w_sc, w_gla, w_out, g2, sh2, sc2, w_router, tm):
    b, s, d = x.shape
    per_batch = gt1.shape[0] == b
    mod_map = (lambda bi, i: (bi, 0, 0)) if per_batch else (lambda bi, i: (0, 0, 0))
    tok = lambda width, blk: pl.BlockSpec((1, tm, width), lambda bi, i: (bi, i, blk))
    full = lambda arr: pl.BlockSpec(arr.shape, lambda bi, i: (0,) * arr.ndim)
    mod = pl.BlockSpec((1, 1, d), mod_map)
    gn_t = jnp.tile(gn, GLA_HEADS)[None]
    g2_t = g2[None]
    return pl.pallas_call(
        _merge_kernel,
        grid=(b, s // tm),
        in_specs=[tok(NA_WIDTH, 0), tok(SC_WIDTH, 0), tok(GLA_VAL_WIDTH, 0), tok(GLA_VAL_WIDTH, 0),
                  tok(d, OFF_R_GLA // d), tok(d, OFF_MERGE // d), tok(d, OFF_MERGE // d + 1),
                  tok(d, OFF_MERGE // d + 2), tok(d, 0), mod,
                  full(gn_t), full(w_na), full(w_sc), full(w_gla), full(w_out), full(g2_t), mod, mod,
                  full(w_router)],
        out_specs=[tok(d, 0), tok(d, 0),
                   pl.BlockSpec((SC_PARTS, 1, tm, SC_ROW), lambda bi, i: (0, bi, i, 0)), tok(LANE, 0)],
        out_shape=[jax.ShapeDtypeStruct((b, s, d), F32),
                   jax.ShapeDtypeStruct((b, s, d), BF16),
                   jax.ShapeDtypeStruct((SC_PARTS, b, s, SC_ROW), U32),
                   jax.ShapeDtypeStruct((b, s, LANE), F32)],
        compiler_params=_cparams(("parallel", "parallel")),
        name="merge",
    )(o_na, o_sc, o_f, o_b, main, main, main, main, x, gt1, gn_t, w_na, w_sc, w_gla, w_out, g2_t,
      sh2, sc2, w_router)


def _router_kernel(lg_ref, br_ref, tri_ref, eidx_ref, gw_ref, rank_ref, cnt_ref, carry_ref):
    tm = lg_ref.shape[0]

    @pl.when(pl.program_id(0) == 0)
    def _():
        carry_ref[...] = jnp.zeros_like(carry_ref)

    scores = _sigmoid(lg_ref[...].T[:N_EXPERTS])
    sel = scores + br_ref[...]
    neg = -jnp.inf

    sel3 = sel.reshape(N_EXPERT_GROUPS, GROUP_SIZE, tm)
    i3 = lax.broadcasted_iota(jnp.int32, sel3.shape, 1)
    m1 = sel3.max(axis=1, keepdims=True)
    first = jnp.where(sel3 == m1, i3, GROUP_SIZE).min(axis=1, keepdims=True)
    m2 = jnp.where(i3 == first, neg, sel3).max(axis=1, keepdims=True)
    gscore = (m1 + m2)[:, 0, :]

    gi = lax.broadcasted_iota(jnp.int32, gscore.shape, 0)
    gmask = jnp.zeros(gscore.shape, jnp.bool_)
    for _ in range(TOPK_GROUPS):
        m = gscore.max(axis=0, keepdims=True)
        pick = gi == jnp.where(gscore == m, gi, N_EXPERT_GROUPS).min(axis=0, keepdims=True)
        gmask = gmask | pick
        gscore = jnp.where(pick, neg, gscore)
    emask = jnp.broadcast_to(gmask[:, None, :], sel3.shape).reshape(N_EXPERTS, tm)
    sel = jnp.where(emask, sel, neg)

    ei = lax.broadcasted_iota(jnp.int32, sel.shape, 0)
    picks, idxs, ws = [], [], []
    for _ in range(TOP_K):
        m = sel.max(axis=0, keepdims=True)
        idx = jnp.where(sel == m, ei, N_EXPERTS).min(axis=0, keepdims=True)
        pick = ei == idx
        picks.append(pick)
        idxs.append(idx)
        ws.append(jnp.where(pick, scores, 0.0).sum(axis=0, keepdims=True))
        sel = jnp.where(pick, neg, sel)
    w = jnp.concatenate(ws, axis=0)
    gw_ref[...] = w / w.sum(axis=0, keepdims=True) * ROUTED_SCALE
    eidx_ref[...] = jnp.concatenate(idxs, axis=0)

    onehot = picks[0]
    for p in picks[1:]:
        onehot = onehot | p
    onehot = jnp.where(onehot, 1.0, 0.0).astype(BF16)
    before = _dot(onehot, tri_ref[...]) + jnp.tile(carry_ref[...], (1, tm // LANE))
    rank_ref[...] = jnp.concatenate(
        [jnp.where(p, before, 0.0).sum(axis=0, keepdims=True) for p in picks], axis=0).astype(jnp.int32)
    carry_ref[...] += _dot(onehot, jnp.ones((tm, LANE), BF16))
    cnt_ref[...] = carry_ref[...]


def _route(logits, b_router, tm=512):
    t = logits.shape[0]
    br = jnp.broadcast_to(b_router.astype(F32)[:, None], (N_EXPERTS, tm))
    tri = jnp.asarray(np.triu(np.ones((tm, tm), np.float32), 1), dtype=BF16)
    kt = lambda dt: jax.ShapeDtypeStruct((TOP_K, t), dt)
    eidx, gw, rank, cnt = pl.pallas_call(
        _router_kernel,
        grid=(t // tm,),
        in_specs=[pl.BlockSpec((tm, LANE), lambda i: (i, 0)),
                  pl.BlockSpec((N_EXPERTS, tm), lambda i: (0, 0)),
                  pl.BlockSpec((tm, tm), lambda i: (0, 0))],
        out_specs=[pl.BlockSpec((TOP_K, tm), lambda i: (0, i)),
                   pl.BlockSpec((TOP_K, tm), lambda i: (0, i)),
                   pl.BlockSpec((TOP_K, tm), lambda i: (0, i)),
                   pl.BlockSpec((N_EXPERTS, LANE), lambda i: (0, 0))],
        out_shape=[kt(jnp.int32), kt(F32), kt(jnp.int32),
                   jax.ShapeDtypeStruct((N_EXPERTS, LANE), F32)],
        scratch_shapes=[pltpu.VMEM((N_EXPERTS, LANE), F32)],
        compiler_params=_cparams(("arbitrary",)),
        name="router",
    )(logits, br, tri)
    return eidx, gw, rank, cnt[:, 0].astype(jnp.int32)


def _sc_mesh():
    return plsc.VectorSubcoreMesh(core_axis_name="core", subcore_axis_name="subcore")


def _dispatch_rows(xp, dest, slots):
    parts, t, _ = xp.shape
    nwin = parts * t // SC_WINDOW
    idx = dest.reshape(TOP_K, t // SC_WINDOW, SC_WINDOW).transpose(1, 0, 2)
    idx = jnp.concatenate([idx + part * slots for part in range(parts)], axis=0)

    @pl.kernel(out_type=jax.ShapeDtypeStruct((parts * slots, SC_ROW), xp.dtype), mesh=_sc_mesh(),
               scratch_types=[], name="moe_dispatch")
    def run(x_hbm, i_hbm, o_hbm):
        def body(x_vmem, i_vmem):
            for k in range(TOP_K):
                pltpu.sync_copy(x_vmem, o_hbm.at[i_vmem.at[0, k]])

        pltpu.emit_pipeline(
            body,
            grid=(nwin,),
            in_specs=[pl.BlockSpec((SC_WINDOW, SC_ROW), lambda i: (i, 0)),
                      pl.BlockSpec((1, TOP_K, SC_WINDOW), lambda i: (i, 0, 0))],
            out_specs=[],
            core_axis_name=("core", "subcore"),
            dimension_semantics=(pltpu.PARALLEL,),
        )(x_hbm, i_hbm)

    return run(xp.reshape(parts * t, SC_ROW), idx).reshape(parts, slots, SC_ROW)


def _gather_rows(yp, idx):
    n = idx.shape[0]
    parts, slots, _ = yp.shape
    idx = jnp.concatenate([idx + part * slots for part in range(parts)]).reshape(1, n * parts)

    @pl.kernel(out_type=jax.ShapeDtypeStruct((n * parts, SC_ROW), yp.dtype), mesh=_sc_mesh(),
               scratch_types=[], name="moe_gather")
    def run(y_hbm, i_hbm, o_hbm):
        def body(i_vmem, o_vmem):
            pltpu.sync_copy(y_hbm.at[i_vmem.at[0]], o_vmem)

        pltpu.emit_pipeline(
            body,
            grid=(n * parts // SC_WINDOW,),
            in_specs=[pl.BlockSpec((1, SC_WINDOW), lambda i: (0, i))],
            out_specs=[pl.BlockSpec((SC_WINDOW, SC_ROW), lambda i: (i, 0))],
            core_axis_name=("core", "subcore"),
            dimension_semantics=(pltpu.PARALLEL,),
        )(i_hbm, o_hbm)

    return run(yp.reshape(parts * slots, SC_ROW), idx).reshape(parts, n, SC_ROW)


def _expert_kernel(be_ref, bv_ref, x_ref, wg_ref, wu_ref, wd_ref, o_ref, wg_s, wu_s, wd_s):
    i = pl.program_id(0)
    valid = bv_ref[i]
    new_expert = (i == 0) | (be_ref[i] != be_ref[jnp.maximum(i - 1, 0)])

    @pl.when(new_expert)
    def _():
        wg_s[...] = wg_ref[0, 0].astype(BF16)
        wu_s[...] = wu_ref[0, 0].astype(BF16)
        wd_s[...] = wd_ref[0, 0].astype(BF16)

    @pl.when(valid > 0)
    def _():
        w = _load_parts(x_ref)
        row = lax.broadcasted_iota(jnp.int32, w.shape, 0)
        w = jnp.where(row < valid, w, jnp.uint32(0))
        lo, hi = _unpack_rows(w)
        x = jnp.concatenate([lo, hi], axis=1).astype(BF16)
        a = _dot(x, wg_s[...])
        hid = a * _sigmoid(a) * _dot(x, wu_s[...])
        _store_parts(o_ref, _pack_rows(_dot(hid.astype(BF16), wd_s[...])))


def _experts(xs, blk_e, blk_valid, layer, w_gate, w_up, w_down):
    parts, slots, _ = xs.shape
    d = D_MODEL
    nb = slots // MOE_BLOCK
    return pl.pallas_call(
        _expert_kernel,
        grid_spec=pltpu.PrefetchScalarGridSpec(
            num_scalar_prefetch=2,
            grid=(nb,),
            in_specs=[pl.BlockSpec((parts, MOE_BLOCK, SC_ROW), lambda i, be, bv: (0, i, 0)),
                      pl.BlockSpec((1, 1, d, EXPERT_FF), lambda i, be, bv: (layer, be[i], 0, 0)),
                      pl.BlockSpec((1, 1, d, EXPERT_FF), lambda i, be, bv: (layer, be[i], 0, 0)),
                      pl.BlockSpec((1, 1, EXPERT_FF, d), lambda i, be, bv: (layer, be[i], 0, 0))],
            out_specs=pl.BlockSpec((parts, MOE_BLOCK, SC_ROW), lambda i, be, bv: (0, i, 0)),
            scratch_shapes=[pltpu.VMEM((d, EXPERT_FF), BF16), pltpu.VMEM((d, EXPERT_FF), BF16),
                            pltpu.VMEM((EXPERT_FF, d), BF16)]),
        out_shape=jax.ShapeDtypeStruct((parts, slots, SC_ROW), U32),
        compiler_params=_cparams(("arbitrary",)),
        name="experts",
    )(blk_e, blk_valid, xs, w_gate, w_up, w_down)


def _combine_kernel(yg_ref, gw_ref, h_ref, x_ref, gt_ref, wsg_ref, wsu_ref, wsd_ref, gf_ref, o_ref, *, final):
    h = h_ref[0]
    a = _dot(h, wsg_ref[...])
    hid = a * _sigmoid(a) * _dot(h, wsu_ref[...])
    y = _dot(hid.astype(BF16), wsd_ref[...])
    gw = gw_ref[...]
    y_lo = y[:, :D_MODEL // 2]
    y_hi = y[:, D_MODEL // 2:]
    for k in range(TOP_K):
        lo, hi = _unpack_rows(_load_parts(yg_ref, k))
        y_lo = y_lo + gw[:, k:k + 1] * lo
        y_hi = y_hi + gw[:, k:k + 1] * hi
    y = jnp.concatenate([y_lo, y_hi], axis=1)
    xn = x_ref[0] + gt_ref[0] * y
    if final:
        ms = jnp.mean(xn * xn, axis=-1, keepdims=True)
        xn = xn * lax.rsqrt(ms + RMS_EPS) * gf_ref[...]
    o_ref[0] = xn


def _combine(yg, gw, tok_off, h2, x, gt2, b0, nb, ws_gate, ws_up, ws_down, g_final, final, tm):
    b, s, d = x.shape
    per_batch = gt2.shape[0] == b
    mod_map = (lambda bi, i: (b0 + bi, 0, 0)) if per_batch else (lambda bi, i: (0, 0, 0))
    full = lambda arr: pl.BlockSpec(arr.shape, lambda bi, i: (0,) * arr.ndim)
    tok = lambda width: pl.BlockSpec((1, tm, width), lambda bi, i: (b0 + bi, i, 0))
    gf = g_final[None]
    nblk = s // tm
    blk0 = (tok_off + b0 * s) // tm
    return pl.pallas_call(
        functools.partial(_combine_kernel, final=final),
        grid=(nb, nblk),
        in_specs=[pl.BlockSpec((SC_PARTS, TOP_K, tm, SC_ROW), lambda bi, i: (0, 0, bi * nblk + i, 0)),
                  pl.BlockSpec((tm, TOP_K), lambda bi, i: (blk0 + bi * nblk + i, 0)),
                  tok(d), tok(d),
                  pl.BlockSpec((1, 1, d), mod_map),
                  full(ws_gate), full(ws_up), full(ws_down), full(gf)],
        out_specs=tok(d),
        out_shape=jax.ShapeDtypeStruct((b, s, d), F32),
        input_output_aliases={3: 0},
        compiler_params=_cparams(("parallel", "parallel")),
        name="combine",
    )(yg, gw, h2, x, gt2, ws_gate, ws_up, ws_down, gf)


def _layer(x, ctx_s, mods, mods_ctx, p, ctx_out, final, g_final):
    b, s, d = x.shape
    sc = ctx_s.shape[1]
    sh1, sc1, gt1, sh2, sc2, gt2 = mods
    csh1, csc1, cgt1, csh2, csc2, cgt2 = mods_ctx

    w_main, w_gate = _prep_w_in(p['w_in'], p['layer'])
    main, gate = _proj_in(x, p['g_norm1'], sh1, sc1, w_main, w_gate, tm=min(2048, s), tn=1024)
    ctx_flat = ctx_s.reshape(1, b * sc, d)
    n_ctx, tn_ctx = (N_MAIN, 1024) if ctx_out else (N_KV_MAIN, N_KV_MAIN // 2)
    main_c, gate_c = _proj_in(ctx_flat, p['g_norm1'], csh1, csc1, w_main, w_gate, tm=min(1024, b * sc),
                              tn=tn_ctx, n=n_ctx)
    main_c = main_c.reshape(b, sc, n_ctx)
    gate_c = gate_c.reshape(b, sc, LANE)

    o_na = _na_latent(main, main_c, p['na_rpb'])
    o_sc = _short_conv(main, p['conv_w'])

    gate_ws = _gla_gate_weights(p['gla_gate_w'], p['gla_gate_b'])
    s0 = jnp.zeros((b, GLA_HEADS, GLA_DK, GLA_DV), F32)
    o_cf, o_cb, st_f, st_b = _gla_scan(main_c, gate_c, gate_ws, (s0, s0), ctx_out)
    o_f, o_b, _, _ = _gla_scan(main, gate, gate_ws, (st_f, st_b), True)

    w_na = p['w_branch_na'].astype(BF16)
    w_sc = p['w_branch_sc'].astype(BF16)
    w_gla = p['w_branch_gla'].astype(BF16)
    w_out = p['w_out'].astype(BF16)
    w_router = jnp.pad(p['w_router'], ((0, 0), (0, LANE - N_EXPERTS))).astype(BF16)
    x, h2, h2p, logits = _merge(o_na, o_sc, o_f, o_b, main, x, gt1, p['gla_norm_g'], w_na, w_sc, w_gla, w_out,
                                p['g_norm2'], sh2, sc2, w_router, tm=min(512, s))
    n_lat = b * s
    hp_all = h2p.reshape(SC_PARTS, n_lat, SC_ROW)
    lg_all = logits.reshape(n_lat, LANE)
    if ctx_out:
        o_na_c = _dense_attn(main_c)
        o_sc_c = _short_conv(main_c, p['conv_w'])
        ctx_s, h2_c, h2p_c, lg_c = _merge(o_na_c, o_sc_c, o_cf, o_cb, main_c, ctx_s, cgt1, p['gla_norm_g'],
                                          w_na, w_sc, w_gla, w_out, p['g_norm2'], csh2, csc2, w_router,
                                          tm=min(256, sc))
        hp_all = jnp.concatenate([hp_all, h2p_c.reshape(SC_PARTS, b * sc, SC_ROW)], axis=1)
        lg_all = jnp.concatenate([lg_all, lg_c.reshape(b * sc, LANE)], axis=0)

    t = hp_all.shape[1]
    eidx, gw, rank, counts = _route(lg_all, p['b_router'])
    padded = (counts + MOE_BLOCK - 1) // MOE_BLOCK * MOE_BLOCK
    pad_end = jnp.cumsum(padded)
    pad_start = pad_end - padded
    onehot = eidx[:, :, None] == jnp.arange(N_EXPERTS, dtype=jnp.int32)
    dest = jnp.sum(jnp.where(onehot, pad_start, 0), axis=-1) + rank
    n_blocks = -(-(t * TOP_K + N_EXPERTS * (MOE_BLOCK - 1)) // MOE_BLOCK)
    slots = n_blocks * MOE_BLOCK
    blk_start = jnp.arange(n_blocks, dtype=jnp.int32) * MOE_BLOCK
    blk_e = jnp.minimum(jnp.sum(pad_end[None, :] <= blk_start[:, None], axis=1), N_EXPERTS - 1).astype(jnp.int32)
    used_end = (pad_start + counts)[blk_e]
    blk_valid = jnp.clip(used_end - blk_start, 0, MOE_BLOCK).astype(jnp.int32)

    xs = _dispatch_rows(hp_all, dest, slots)
    ys = _experts(xs, blk_e, blk_valid, p['layer'], p['w_exp_gate'], p['w_exp_up'], p['w_exp_down'])
    gw_t = gw.T
    ws_gate = p['w_sh_gate'].astype(BF16)
    ws_up = p['w_sh_up'].astype(BF16)
    ws_down = p['w_sh_down'].astype(BF16)

    def gathered(t0, n):
        return _gather_rows(ys, dest[:, t0:t0 + n].reshape(-1)).reshape(SC_PARTS, TOP_K, n, SC_ROW)

    pieces = next(n for n in (4, 2, 1) if b % n == 0)
    nb = b // pieces
    for q in range(pieces):
        x = _combine(gathered(q * nb * s, nb * s), gw_t, 0, h2, x, gt2, q * nb, nb, ws_gate, ws_up, ws_down,
                     g_final, final, tm=min(256, s))
    if ctx_out:
        ctx_s = _combine(gathered(n_lat, b * sc), gw_t, n_lat, h2_c, ctx_s, cgt2, 0, b, ws_gate, ws_up, ws_down,
                         g_final, False, tm=min(256, sc))
    return x, ctx_s


def kernel(x, c, ctx, c_ctx, w_mod, b_mod, g_norm1, g_norm2, w_in, na_rpb, w_branch_na, conv_w, w_branch_sc,
           gla_gate_w, gla_gate_b, gla_norm_g, w_branch_gla, w_out, w_router, b_router, w_exp_gate, w_exp_up,
           w_exp_down, w_sh_gate, w_sh_up, w_sh_down, g_final):
    stacked = dict(g_norm1=g_norm1, g_norm2=g_norm2, na_rpb=na_rpb, w_branch_na=w_branch_na,
                   conv_w=conv_w, w_branch_sc=w_branch_sc, gla_gate_w=gla_gate_w, gla_gate_b=gla_gate_b,
                   gla_norm_g=gla_norm_g, w_branch_gla=w_branch_gla, w_out=w_out, w_router=w_router,
                   b_router=b_router,
                   w_sh_gate=w_sh_gate, w_sh_up=w_sh_up, w_sh_down=w_sh_down)
    depth = w_in.shape[0]
    ctx_s = ctx
    for i in range(depth):
        p = {name: arr[i] for name, arr in stacked.items()}
        p.update(layer=i, w_in=w_in, w_exp_gate=w_exp_gate, w_exp_up=w_exp_up, w_exp_down=w_exp_down)
        mods, mods_ctx = _mod_vectors(c, c_ctx, w_mod[i], b_mod[i])
        last = i == depth - 1
        x, ctx_s = _layer(x, ctx_s, mods, mods_ctx, p, not last, last, g_final)
    return x
```

```python
import functools

import numpy as np
import jax
import jax.numpy as jnp
from jax import lax
from jax.experimental import pallas as pl
from jax.experimental.pallas import tpu as pltpu
from jax.experimental.pallas import tpu_sc as plsc

F32 = jnp.float32
BF16 = jnp.bfloat16
U32 = jnp.uint32

D_MODEL = 1024
N_MOD = 6
RMS_EPS = 1e-6
NEG_INF = -1e30
GRID_W = 64
NA_HEADS = 8
NA_HEAD_DIM = 64
NA_WIDTH = NA_HEADS * NA_HEAD_DIM
NA_WIN_R = 8
NA_WIN_C = 16
NA_GROUP = 4
SC_WIDTH = 512
GLA_HEADS = 4
GLA_KEY_WIDTH = 512
GLA_VAL_WIDTH = 1024
GLA_DK = GLA_KEY_WIDTH // GLA_HEADS
GLA_DV = GLA_VAL_WIDTH // GLA_HEADS
GLA_GATE_RANK = 16
GLA_GATE_TAU = 16.0
LOG2_E = 1.4426950408889634
N_EXPERTS = 64
N_EXPERT_GROUPS = 8
GROUP_SIZE = N_EXPERTS // N_EXPERT_GROUPS
TOPK_GROUPS = 4
TOP_K = 8
EXPERT_FF = 256
ROUTED_SCALE = 2.5
MOE_BLOCK = 1024

LANE = 128
GLA_C = 128
GLA_LEVELS = tuple(GLA_C >> (i + 1) for i in range(GLA_C.bit_length() - 1))
VMEM_LIMIT = 48 * 1024 * 1024
SC_WINDOW = 128
SC_ROW = 256
SC_PARTS = D_MODEL // 2 // SC_ROW

OFF_V_GLA = 0
OFF_K_NA = 1024
OFF_V_NA = 1536
OFF_K_GLA = 2048
N_KV_MAIN = 2560
OFF_Q_NA = 2560
OFF_B_SC = 3072
OFF_C_SC = 3584
OFF_X_SC = 4096
OFF_Q_GLA = 4608
OFF_R_GLA = 5120
OFF_MERGE = 6144
N_MAIN = 9216


def _cparams(sem, vmem=VMEM_LIMIT):
    return pltpu.CompilerParams(dimension_semantics=sem, vmem_limit_bytes=vmem)


def _dot(a, b):
    return jnp.dot(a, b, preferred_element_type=F32)


def _dot_nt(a, b):
    return lax.dot_general(a, b, (((1,), (1,)), ((), ())), preferred_element_type=F32)


def _dot_tn(a, b):
    return lax.dot_general(a, b, (((0,), (0,)), ((), ())), preferred_element_type=F32)


def _sigmoid(x):
    return 1.0 / (1.0 + jnp.exp(-x))


def _pack_rows(x):
    n = x.shape[1] // 2
    r = x.astype(BF16).astype(F32)
    lo = pltpu.bitcast(r[:, :n], U32) >> 16
    hi = pltpu.bitcast(r[:, n:], U32)
    return hi | lo


def _store_parts(ref, words):
    for part in range(SC_PARTS):
        dst = ref.at[part, 0] if len(ref.shape) == 4 else ref.at[part]
        dst[...] = words[:, part * SC_ROW:(part + 1) * SC_ROW]


def _load_parts(ref, *lead):
    return jnp.concatenate([ref[(part,) + lead] for part in range(SC_PARTS)], axis=-1)


def _unpack_rows(w):
    lo = pltpu.bitcast(w << 16, F32)
    hi = pltpu.bitcast(w & jnp.uint32(0xFFFF0000), F32)
    return lo, hi


def _mod_kernel(a_ref, w_ref, b_ref, o_ref):
    a = a_ref[...]
    a = a * _sigmoid(a)
    o_ref[...] = _dot(a.astype(BF16), w_ref[...].astype(BF16)) + b_ref[...]


def _mod_vectors(c, c_ctx, w_mod, b_mod):
    b = c.shape[0]
    rows = -(-(b + 1) // 8) * 8
    a = jnp.concatenate([c, c_ctx[None], jnp.zeros((rows - b - 1, D_MODEL), F32)], axis=0)
    n = N_MOD * D_MODEL
    tn = 1536
    out = pl.pallas_call(
        _mod_kernel,
        grid=(n // tn,),
        in_specs=[pl.BlockSpec((rows, D_MODEL), lambda j: (0, 0)),
                  pl.BlockSpec((D_MODEL, tn), lambda j: (0, j)),
                  pl.BlockSpec((1, tn), lambda j: (0, j))],
        out_specs=pl.BlockSpec((rows, tn), lambda j: (0, j)),
        out_shape=jax.ShapeDtypeStruct((rows, n), F32),
        compiler_params=_cparams(("parallel",)),
        name="mod_vectors",
    )(a, w_mod, b_mod[None])
    lat = out[:b].reshape(b, N_MOD, 1, D_MODEL)
    ctx = out[b].reshape(N_MOD, 1, 1, D_MODEL)
    return [lat[:, i] for i in range(N_MOD)], [ctx[i] for i in range(N_MOD)]


def _proj_kernel(x_ref, g_ref, sh_ref, sc_ref, w_ref, wg_ref, o_ref, og_ref, h_ref):
    @pl.when(pl.program_id(2) == 0)
    def _():
        x = x_ref[0]
        ms = jnp.mean(x * x, axis=-1, keepdims=True)
        h = x * lax.rsqrt(ms + RMS_EPS) * g_ref[...] * (1.0 + sc_ref[0]) + sh_ref[0]
        hb = h.astype(BF16)
        h_ref[...] = hb
        og_ref[0] = _dot(hb, wg_ref[...])

    o_ref[0] = _dot(h_ref[...], w_ref[...]).astype(o_ref.dtype)


W_IN_TILE = 512
W_IN_GATE_SHIFT = 2 * GLA_GATE_RANK


def _prep_w_in_kernel(a_ref, b_ref, o_ref, g_ref):
    t = pl.program_id(0)
    first_lat = N_KV_MAIN // W_IN_TILE
    a = a_ref[0]

    @pl.when(t < first_lat)
    def _():
        o_ref[...] = a.astype(BF16)

    @pl.when(t >= first_lat)
    def _():
        cat = jnp.concatenate([a, b_ref[0]], axis=1)
        moved = pltpu.roll(cat, cat.shape[1] - W_IN_GATE_SHIFT, axis=1)[:, :W_IN_TILE]
        scale = jnp.where(t == first_lat, NA_HEAD_DIM ** -0.5, 1.0)
        o_ref[...] = (moved * scale).astype(BF16)

    @pl.when(t == first_lat)
    def _():
        head = a[:, :LANE]
        lane = lax.broadcasted_iota(jnp.int32, head.shape, 1)
        g_ref[...] = jnp.where(lane < W_IN_GATE_SHIFT, head, 0.0).astype(BF16)


def _prep_w_in(w_in, layer):
    d = w_in.shape[1]
    first_lat = N_KV_MAIN // W_IN_TILE
    kv_perm = OFF_K_NA // W_IN_TILE

    def a_map(t):
        return (layer, 0, jnp.where(t < first_lat, (t + first_lat - kv_perm) % first_lat, t))

    def b_map(t):
        return (layer, 0, jnp.where(t < first_lat, 0, (t + 1) * (W_IN_TILE // LANE)))

    return pl.pallas_call(
        _prep_w_in_kernel,
        grid=(N_MAIN // W_IN_TILE,),
        in_specs=[pl.BlockSpec((1, d, W_IN_TILE), a_map),
                  pl.BlockSpec((1, d, LANE), b_map)],
        out_specs=[pl.BlockSpec((d, W_IN_TILE), lambda t: (0, t)),
                   pl.BlockSpec((d, LANE), lambda t: (0, 0))],
        out_shape=[jax.ShapeDtypeStruct((d, N_MAIN), BF16), jax.ShapeDtypeStruct((d, LANE), BF16)],
        compiler_params=_cparams(("arbitrary",)),
        name="prep_w_in",
    )(w_in, w_in)


def _proj_in(x, g, shift, scale, w_main, w_gate, tm, tn, n=None):
    b, s, d = x.shape
    n = w_main.shape[1] if n is None else n
    per_batch = shift.shape[0] == b
    mod_map = (lambda bi, i, j: (bi, 0, 0)) if per_batch else (lambda bi, i, j: (0, 0, 0))
    return pl.pallas_call(
        _proj_kernel,
        grid=(b, s // tm, n // tn),
        in_specs=[pl.BlockSpec((1, tm, d), lambda bi, i, j: (bi, i, 0)),
                  pl.BlockSpec((1, d), lambda bi, i, j: (0, 0)),
                  pl.BlockSpec((1, 1, d), mod_map),
                  pl.BlockSpec((1, 1, d), mod_map),
                  pl.BlockSpec((d, tn), lambda bi, i, j: (0, j)),
                  pl.BlockSpec((d, LANE), lambda bi, i, j: (0, 0))],
        out_specs=[pl.BlockSpec((1, tm, tn), lambda bi, i, j: (bi, i, j)),
                   pl.BlockSpec((1, tm, LANE), lambda bi, i, j: (bi, i, 0))],
        out_shape=[jax.ShapeDtypeStruct((b, s, n), BF16),
                   jax.ShapeDtypeStruct((b, s, LANE), F32)],
        scratch_shapes=[pltpu.VMEM((tm, d), BF16)],
        compiler_params=_cparams(("parallel", "parallel", "arbitrary")),
        name="proj_in",
    )(x, g[None], shift, scale, w_main, w_gate)


def _softmax_av(q, keys, vals, biases):
    scores = []
    for kk, bb in zip(keys, biases):
        s = _dot_nt(q, kk)
        scores.append(s if bb is None else s + bb)
    m = scores[0].max(axis=-1, keepdims=True)
    for s in scores[1:]:
        m = jnp.maximum(m, s.max(axis=-1, keepdims=True))
    num = None
    den = None
    for s, vv in zip(scores, vals):
        e = jnp.exp(s - m)
        dsum = e.sum(axis=-1, keepdims=True)
        o = _dot(e.astype(BF16), vv)
        num = o if num is None else num + o
        den = dsum if den is None else den + dsum
    return num / den


def _na_kernel(q_ref, k_ref, v_ref, kc_ref, vc_ref, *rest, rows, kr):
    *bias_refs, o_ref = rest
    kc = kc_ref[0]
    vc = vc_ref[0]
    for j, bias_ref in enumerate(bias_refs):
        r = pl.program_id(1) * len(bias_refs) + j
        row_start = jnp.clip(r - kr // 2, 0, rows - kr)
        start = pl.multiple_of(row_start * GRID_W, GRID_W)
        n_win = kr * GRID_W
        q = q_ref[0, j * GRID_W:(j + 1) * GRID_W, :]
        kw = k_ref[0, pl.ds(start, n_win), :]
        vw = v_ref[0, pl.ds(start, n_win), :]
        o_ref[0, j * GRID_W:(j + 1) * GRID_W, :] = _na_row(q, kw, vw, kc, vc, bias_ref).astype(o_ref.dtype)


def _na_row(q, kw, vw, kc, vc, bias_ref):
    gw = NA_GROUP * NA_HEAD_DIM
    stacked = (NA_GROUP * GRID_W, gw)
    on_head = (lax.broadcasted_iota(jnp.int32, stacked, 0) // GRID_W
               == lax.broadcasted_iota(jnp.int32, stacked, 1) // NA_HEAD_DIM)
    outs = []
    for g in range(NA_HEADS // NA_GROUP):
        sl = slice(g * gw, (g + 1) * gw)
        q_all = jnp.where(on_head, jnp.concatenate([q[:, sl]] * NA_GROUP, axis=0), jnp.zeros((), q.dtype))
        bias = bias_ref[0, g * NA_GROUP * GRID_W:(g + 1) * NA_GROUP * GRID_W, :]
        o_all = _softmax_av(q_all, [kw[:, sl], kc[:, sl]], [vw[:, sl], vc[:, sl]], [bias, None])
        o_all = jnp.where(on_head, o_all, 0.0).reshape(NA_GROUP, GRID_W, gw)
        outs.append(o_all.sum(axis=0))
    return jnp.concatenate(outs, axis=-1)


def _na_bias_table(rpb, rows, kr):
    col = np.arange(GRID_W)
    col_start = np.clip(col - NA_WIN_C // 2, 0, GRID_W - NA_WIN_C)
    col_ok = (col[None, :] >= col_start[:, None]) & (col[None, :] < col_start[:, None] + NA_WIN_C)
    d_col = np.clip(col[None, :] - col[:, None], -(NA_WIN_C - 1), NA_WIN_C - 1) + NA_WIN_C - 1
    n_dr, n_dc = rpb.shape[1], rpb.shape[2]
    onehot = jnp.asarray((d_col.reshape(-1)[None, :] == np.arange(n_dc)[:, None]).astype(np.float32))
    by_col = jnp.dot(rpb.astype(F32).reshape(NA_HEADS * n_dr, n_dc), onehot, precision=lax.Precision.HIGHEST)
    by_col = by_col.reshape(NA_HEADS, n_dr, GRID_W, GRID_W)
    by_col = jnp.where(col_ok[None, None], by_col, NEG_INF)
    tables = []
    for o in range(kr):
        lo = NA_WIN_R - 1 - o
        tables.append(by_col[:, lo:lo + kr].transpose(0, 2, 1, 3).reshape(NA_HEADS, GRID_W, kr * GRID_W))
    return jnp.stack(tables).reshape(kr, NA_HEADS * GRID_W, kr * GRID_W)


def _na_latent(main, main_ctx, rpb):
    b, s, _ = main.shape
    sc = main_ctx.shape[1]
    rows = s // GRID_W
    kr = min(NA_WIN_R, rows)
    bias = _na_bias_table(rpb, rows, kr)
    w = NA_WIDTH

    per_step = next(n for n in (4, 2, 1) if rows % n == 0)

    def bias_spec(j):
        def bias_map(bi, i):
            r = i * per_step + j
            return (r - jnp.clip(r - kr // 2, 0, rows - kr), 0, 0)
        return pl.BlockSpec((1, NA_HEADS * GRID_W, kr * GRID_W), bias_map)

    return pl.pallas_call(
        functools.partial(_na_kernel, rows=rows, kr=kr),
        grid=(b, rows // per_step),
        in_specs=[pl.BlockSpec((1, per_step * GRID_W, w), lambda bi, i: (bi, i, OFF_Q_NA // w)),
                  pl.BlockSpec((1, s, w), lambda bi, i: (bi, 0, OFF_K_NA // w)),
                  pl.BlockSpec((1, s, w), lambda bi, i: (bi, 0, OFF_V_NA // w)),
                  pl.BlockSpec((1, sc, w), lambda bi, i: (bi, 0, OFF_K_NA // w)),
                  pl.BlockSpec((1, sc, w), lambda bi, i: (bi, 0, OFF_V_NA // w))]
                 + [bias_spec(j) for j in range(per_step)],
        out_specs=pl.BlockSpec((1, per_step * GRID_W, w), lambda bi, i: (bi, i, 0)),
        out_shape=jax.ShapeDtypeStruct((b, s, w), BF16),
        compiler_params=_cparams(("parallel", "arbitrary")),
        name="na_latent",
    )(main, main, main, main_ctx, main_ctx, *([bias] * per_step))


def _dense_attn_kernel(q_ref, k_ref, v_ref, o_ref):
    q = q_ref[0]
    k = k_ref[0]
    v = v_ref[0]
    outs = []
    for h in range(NA_HEADS):
        sl = slice(h * NA_HEAD_DIM, (h + 1) * NA_HEAD_DIM)
        outs.append(_softmax_av(q[:, sl], [k[:, sl]], [v[:, sl]], [None]))
    o_ref[0] = jnp.concatenate(outs, axis=-1).astype(o_ref.dtype)


def _dense_attn(main_ctx):
    b, sc, _ = main_ctx.shape
    w = NA_WIDTH
    return pl.pallas_call(
        _dense_attn_kernel,
        grid=(b,),
        in_specs=[pl.BlockSpec((1, sc, w), lambda bi: (bi, 0, OFF_Q_NA // w)),
                  pl.BlockSpec((1, sc, w), lambda bi: (bi, 0, OFF_K_NA // w)),
                  pl.BlockSpec((1, sc, w), lambda bi: (bi, 0, OFF_V_NA // w))],
        out_specs=pl.BlockSpec((1, sc, w), lambda bi: (bi, 0, 0)),
        out_shape=jax.ShapeDtypeStruct((b, sc, w), BF16),
        compiler_params=_cparams(("parallel",)),
        name="ctx_attn",
    )(main_ctx, main_ctx, main_ctx)


def _conv_kernel(b_ref, c_ref, x_ref, w_ref, o_ref):
    u = c_ref[0].astype(F32) * x_ref[0].astype(F32)
    s = u.shape[0]
    t = lax.broadcasted_iota(jnp.int32, u.shape, 0)
    prev = jnp.where(t == 0, 0.0, pltpu.roll(u, 1, axis=0))
    nxt = jnp.where(t == s - 1, 0.0, pltpu.roll(u, s - 1, axis=0))
    w = w_ref[...]
    y = b_ref[0].astype(F32) * (prev * w[0:1] + u * w[1:2] + nxt * w[2:3])
    o_ref[0] = y.astype(o_ref.dtype)


def _short_conv(main, conv_w):
    b, s, _ = main.shape
    nt = SC_WIDTH // LANE
    return pl.pallas_call(
        _conv_kernel,
        grid=(b, nt),
        in_specs=[pl.BlockSpec((1, s, LANE), lambda bi, c: (bi, 0, OFF_B_SC // LANE + c)),
                  pl.BlockSpec((1, s, LANE), lambda bi, c: (bi, 0, OFF_C_SC // LANE + c)),
                  pl.BlockSpec((1, s, LANE), lambda bi, c: (bi, 0, OFF_X_SC // LANE + c)),
                  pl.BlockSpec((3, LANE), lambda bi, c: (0, c))],
        out_specs=pl.BlockSpec((1, s, LANE), lambda bi, c: (bi, 0, c)),
        out_shape=jax.ShapeDtypeStruct((b, s, SC_WIDTH), BF16),
        compiler_params=_cparams(("parallel", "parallel")),
        name="short_conv",
    )(main, main, main, conv_w)


def _gla_matrices(reverse):
    c = GLA_C
    t = np.arange(c)[:, None]
    m = np.arange(c)[None, :]
    blocks = [m <= t, m > t]
    for b in GLA_LEVELS:
        first = (t // (2 * b)) * (2 * b) + b
        is_q = (t & b) != 0
        blocks.append(np.where(is_q, (m > first) & (m <= t), (m > t) & (m <= first)))
    mats = np.stack(blocks).astype(np.float32)
    if reverse:
        mats = mats[:, ::-1, ::-1]
    mats = mats.reshape(-1, c)
    return jnp.asarray(np.concatenate([mats, mats], axis=1), dtype=BF16)


def _split_bf16(x):
    hi = x.astype(BF16)
    return hi, (x - hi.astype(F32)).astype(BF16)


def _pair_block_diag(x):
    lane = lax.broadcasted_iota(jnp.int32, x.shape, 1)
    zero = jnp.zeros((), x.dtype)
    return jnp.concatenate([jnp.where(lane < GLA_DK, x, zero), jnp.where(lane >= GLA_DK, x, zero)], axis=0)


def _gla_kernel(*refs, emit):
    n_in = 9 if emit else 8
    n_out = 2 if emit else 1
    ins, outs, scratch = refs[:2 * n_in], refs[2 * n_in:2 * (n_in + n_out)], refs[2 * (n_in + n_out):]
    for d, reverse in enumerate((False, True)):
        _gla_direction(ins[d * n_in:(d + 1) * n_in], outs[d * n_out:(d + 1) * n_out], scratch[d], reverse, emit)


def _gla_direction(ins, outs, st_ref, reverse, emit):
    if emit:
        q_ref, k_ref, v_ref, gt_ref, w2a_ref, w2b_ref, b2_ref, a_ref, s0_ref = ins
        o_ref, sf_ref = outs
    else:
        k_ref, v_ref, gt_ref, w2a_ref, w2b_ref, b2_ref, a_ref, s0_ref = ins
        (sf_ref,) = outs
    c = GLA_C
    pw = 2 * GLA_DK
    step = pl.program_id(1)

    @pl.when(step == 0)
    def _():
        st_ref[...] = s0_ref[0]

    lr_hi, lr_lo = _split_bf16(gt_ref[0])
    logit = (_dot(jnp.concatenate([lr_hi, lr_lo], axis=1), w2a_ref[...]) + _dot(lr_hi, w2b_ref[...])
             + b2_ref[...])
    g = (jnp.minimum(logit, 0.0) - jnp.log1p(jnp.exp(-jnp.abs(logit)))) * (LOG2_E / GLA_GATE_TAU)
    g_hi, g_lo = _split_bf16(g)
    args = _dot(a_ref[...], jnp.concatenate([g_hi, g_lo], axis=0))
    cum = args[0:c]
    rem = args[c:2 * c]
    last_row = cum[0:1] if reverse else cum[c - 1:c]

    k = k_ref[0].astype(F32)
    v = v_ref[0]
    atts = []
    if emit:
        q = q_ref[0].astype(F32) * (GLA_DK ** -0.5)
        ti = lax.broadcasted_iota(jnp.int32, (c, pw), 0)
        si = lax.broadcasted_iota(jnp.int32, (c, pw), 1) & (GLA_DK - 1)
        if reverse:
            ti, si = c - 1 - ti, c - 1 - si
        for hp in range(GLA_HEADS // 2):
            cs = slice(hp * pw, (hp + 1) * pw)
            qp, kp = q[:, cs], k[:, cs]
            att = jnp.where(ti == si, _dot_nt(qp.astype(BF16), _pair_block_diag(kp.astype(BF16))), 0.0)
            for l, b in enumerate(GLA_LEVELS):
                is_q = (ti & b) != 0
                x = (jnp.exp2(args[(2 + l) * c:(3 + l) * c, cs]) * jnp.where(is_q, qp, kp)).astype(BF16)
                pair = (((ti ^ si) >> (b.bit_length() - 1)) == 1) & is_q
                att = jnp.where(pair, _dot_nt(x, _pair_block_diag(x)), att)
            atts.append(att.astype(BF16))

    outs = []
    for h in range(GLA_HEADS):
        sl = slice(h * GLA_DK, (h + 1) * GLA_DK)
        kh = k[:, sl]
        vh = v[:, h * GLA_DV:(h + 1) * GLA_DV]
        state = st_ref[h]
        if emit:
            qd = (q[:, sl] * jnp.exp2(cum[:, sl])).astype(BF16)
            att = atts[h // 2][:, (h % 2) * GLA_DK:(h % 2 + 1) * GLA_DK]
            outs.append(_dot(qd, state.astype(BF16)) + _dot(att, vh))
        kd = (kh * jnp.exp2(rem[:, sl])).astype(BF16)
        decay = jnp.exp2(jnp.broadcast_to(last_row[:, sl], (GLA_DK, GLA_DK))).T
        decay = jnp.concatenate([decay] * (GLA_DV // GLA_DK), axis=1)
        st_ref[h] = decay * state + _dot_tn(kd, vh)

    if emit:
        o_ref[0] = jnp.concatenate(outs, axis=-1).astype(o_ref.dtype)

    @pl.when(step == pl.num_programs(1) - 1)
    def _():
        sf_ref[0] = st_ref[...]


def _gla_scan(main, gate, gate_ws, s0s, emit):
    b, l, _ = main.shape
    n = l // GLA_C
    const = lambda arr: pl.BlockSpec(arr.shape, lambda bi, s: (0,) * arr.ndim)
    state_spec = pl.BlockSpec((1, GLA_HEADS, GLA_DK, GLA_DV), lambda bi, s: (bi, 0, 0, 0))
    state_shape = jax.ShapeDtypeStruct((b, GLA_HEADS, GLA_DK, GLA_DV), F32)
    in_specs, args, out_specs, out_shape = [], [], [], []
    for reverse in (False, True):
        amat = _gla_matrices(reverse)
        w2a, w2b, b2 = gate_ws[reverse]

        def col(block, reverse=reverse):
            return lambda bi, s: (bi, n - 1 - s if reverse else s, block)

        if emit:
            in_specs.append(pl.BlockSpec((1, GLA_C, GLA_KEY_WIDTH), col(OFF_Q_GLA // GLA_KEY_WIDTH)))
            args.append(main)
            out_specs.append(pl.BlockSpec((1, GLA_C, GLA_VAL_WIDTH), col(0)))
            out_shape.append(jax.ShapeDtypeStruct((b, l, GLA_VAL_WIDTH), BF16))
        in_specs += [pl.BlockSpec((1, GLA_C, GLA_KEY_WIDTH), col(OFF_K_GLA // GLA_KEY_WIDTH)),
                     pl.BlockSpec((1, GLA_C, GLA_VAL_WIDTH), col(OFF_V_GLA // GLA_VAL_WIDTH)),
                     pl.BlockSpec((1, GLA_C, LANE), col(0)),
                     const(w2a), const(w2b), const(b2), const(amat), state_spec]
        args += [main, main, gate, w2a, w2b, b2, amat, s0s[reverse]]
        out_specs.append(state_spec)
        out_shape.append(state_shape)
    res = pl.pallas_call(
        functools.partial(_gla_kernel, emit=emit),
        grid=(b, n),
        in_specs=in_specs,
        out_specs=out_specs,
        out_shape=out_shape,
        scratch_shapes=[pltpu.VMEM((GLA_HEADS, GLA_DK, GLA_DV), F32)] * 2,
        compiler_params=_cparams(("parallel", "arbitrary")),
        name="gla_scan",
    )(*args)
    return (res[0], res[2], res[1], res[3]) if emit else (None, None, res[0], res[1])


def _gla_gate_weights(gate_w, gate_b):
    out = []
    for dr in range(2):
        w = jnp.zeros((LANE, GLA_KEY_WIDTH), F32)
        w = w.at[dr * GLA_GATE_RANK:(dr + 1) * GLA_GATE_RANK].set(gate_w[dr])
        w_hi = w.astype(BF16)
        w_lo = (w - w_hi.astype(F32)).astype(BF16)
        out.append((jnp.concatenate([w_hi, w_hi], axis=0), w_lo, gate_b[dr][None]))
    return out


def _merge_kernel(ona_ref, osc_ref, of_ref, ob_ref, r_ref, gna_ref, gsc_ref, ggl_ref, x_ref, gt_ref,
                  gn_ref, wna_ref, wsc_ref, wgl_ref, wo_ref, g2_ref, sh2_ref, sc2_ref, wr_ref,
                  xo_ref, h2_ref, hp_ref, lg_ref):
    o = of_ref[0].astype(F32) + ob_ref[0].astype(F32)
    normed = []
    for h in range(GLA_HEADS):
        oh = o[:, h * GLA_DV:(h + 1) * GLA_DV]
        ms = jnp.mean(oh * oh, axis=-1, keepdims=True)
        normed.append(oh * lax.rsqrt(ms + RMS_EPS))
    r = r_ref[0].astype(F32)
    y_gla = jnp.concatenate(normed, axis=-1) * gn_ref[...] * (r * _sigmoid(r))
    y = (_sigmoid(gna_ref[0].astype(F32)) * _dot(ona_ref[0], wna_ref[...])
         + _sigmoid(gsc_ref[0].astype(F32)) * _dot(osc_ref[0], wsc_ref[...])
         + _sigmoid(ggl_ref[0].astype(F32)) * _dot(y_gla.astype(BF16), wgl_ref[...]))
    xn = x_ref[0] + gt_ref[0] * _dot(y.astype(BF16), wo_ref[...])
    xo_ref[0] = xn
    ms = jnp.mean(xn * xn, axis=-1, keepdims=True)
    h2 = xn * lax.rsqrt(ms + RMS_EPS) * g2_ref[...] * (1.0 + sc2_ref[0]) + sh2_ref[0]
    h2b = h2.astype(BF16)
    h2_ref[0] = h2b
    _store_parts(hp_ref, _pack_rows(h2))
    lg_ref[0] = _dot(h2b, wr_ref[...])


def _merge(o_na, o_sc, o_f, o_b, main, x, gt1, gn, w_na, w_sc, w_gla, w_out, g2, sh2, sc2, w_router, tm):
    b, s, d = x.shape
    per_batch = gt1.shape[0] == b
    mod_map = (lambda bi, i: (bi, 0, 0)) if per_batch else (lambda bi, i: (0, 0, 0))
    tok = lambda width, blk: pl.BlockSpec((1, tm, width), lambda bi, i: (bi, i, blk))
    full = lambda arr: pl.BlockSpec(arr.shape, lambda bi, i: (0,) * arr.ndim)
    mod = pl.BlockSpec((1, 1, d), mod_map)
    gn_t = jnp.tile(gn, GLA_HEADS)[None]
    g2_t = g2[None]
    return pl.pallas_call(
        _merge_kernel,
        grid=(b, s // tm),
        in_specs=[tok(NA_WIDTH, 0), tok(SC_WIDTH, 0), tok(GLA_VAL_WIDTH, 0), tok(GLA_VAL_WIDTH, 0),
                  tok(d, OFF_R_GLA // d), tok(d, OFF_MERGE // d), tok(d, OFF_MERGE // d + 1),
                  tok(d, OFF_MERGE // d + 2), tok(d, 0), mod,
                  full(gn_t), full(w_na), full(w_sc), full(w_gla), full(w_out), full(g2_t), mod, mod,
                  full(w_router)],
        out_specs=[tok(d, 0), tok(d, 0),
                   pl.BlockSpec((SC_PARTS, 1, tm, SC_ROW), lambda bi, i: (0, bi, i, 0)), tok(LANE, 0)],
        out_shape=[jax.ShapeDtypeStruct((b, s, d), F32),
                   jax.ShapeDtypeStruct((b, s, d), BF16),
                   jax.ShapeDtypeStruct((SC_PARTS, b, s, SC_ROW), U32),
                   jax.ShapeDtypeStruct((b, s, LANE), F32)],
        compiler_params=_cparams(("parallel", "parallel")),
        name="merge",
    )(o_na, o_sc, o_f, o_b, main, main, main, main, x, gt1, gn_t, w_na, w_sc, w_gla, w_out, g2_t,
      sh2, sc2, w_router)


def _router_kernel(lg_ref, br_ref, tri_ref, eidx_ref, gw_ref, rank_ref, cnt_ref, carry_ref):
    tm = lg_ref.shape[0]

    @pl.when(pl.program_id(0) == 0)
    def _():
        carry_ref[...] = jnp.zeros_like(carry_ref)

    scores = _sigmoid(lg_ref[...].T[:N_EXPERTS])
    sel = scores + br_ref[...]
    neg = -jnp.inf

    sel3 = sel.reshape(N_EXPERT_GROUPS, GROUP_SIZE, tm)
    i3 = lax.broadcasted_iota(jnp.int32, sel3.shape, 1)
    m1 = sel3.max(axis=1, keepdims=True)
    first = jnp.where(sel3 == m1, i3, GROUP_SIZE).min(axis=1, keepdims=True)
    m2 = jnp.where(i3 == first, neg, sel3).max(axis=1, keepdims=True)
    gscore = (m1 + m2)[:, 0, :]

    gi = lax.broadcasted_iota(jnp.int32, gscore.shape, 0)
    gmask = jnp.zeros(gscore.shape, jnp.bool_)
    for _ in range(TOPK_GROUPS):
        m = gscore.max(axis=0, keepdims=True)
        pick = gi == jnp.where(gscore == m, gi, N_EXPERT_GROUPS).min(axis=0, keepdims=True)
        gmask = gmask | pick
        gscore = jnp.where(pick, neg, gscore)
    emask = jnp.broadcast_to(gmask[:, None, :], sel3.shape).reshape(N_EXPERTS, tm)
    sel = jnp.where(emask, sel, neg)

    ei = lax.broadcasted_iota(jnp.int32, sel.shape, 0)
    picks, idxs, ws = [], [], []
    for _ in range(TOP_K):
        m = sel.max(axis=0, keepdims=True)
        idx = jnp.where(sel == m, ei, N_EXPERTS).min(axis=0, keepdims=True)
        pick = ei == idx
        picks.append(pick)
        idxs.append(idx)
        ws.append(jnp.where(pick, scores, 0.0).sum(axis=0, keepdims=True))
        sel = jnp.where(pick, neg, sel)
    w = jnp.concatenate(ws, axis=0)
    gw_ref[...] = w / w.sum(axis=0, keepdims=True) * ROUTED_SCALE
    eidx_ref[...] = jnp.concatenate(idxs, axis=0)

    onehot = picks[0]
    for p in picks[1:]:
        onehot = onehot | p
    onehot = jnp.where(onehot, 1.0, 0.0).astype(BF16)
    before = _dot(onehot, tri_ref[...]) + jnp.tile(carry_ref[...], (1, tm // LANE))
    rank_ref[...] = jnp.concatenate(
        [jnp.where(p, before, 0.0).sum(axis=0, keepdims=True) for p in picks], axis=0).astype(jnp.int32)
    carry_ref[...] += _dot(onehot, jnp.ones((tm, LANE), BF16))
    cnt_ref[...] = carry_ref[...]


def _route(logits, b_router, tm=512):
    t = logits.shape[0]
    br = jnp.broadcast_to(b_router.astype(F32)[:, None], (N_EXPERTS, tm))
    tri = jnp.asarray(np.triu(np.ones((tm, tm), np.float32), 1), dtype=BF16)
    kt = lambda dt: jax.ShapeDtypeStruct((TOP_K, t), dt)
    eidx, gw, rank, cnt = pl.pallas_call(
        _router_kernel,
        grid=(t // tm,),
        in_specs=[pl.BlockSpec((tm, LANE), lambda i: (i, 0)),
                  pl.BlockSpec((N_EXPERTS, tm), lambda i: (0, 0)),
                  pl.BlockSpec((tm, tm), lambda i: (0, 0))],
        out_specs=[pl.BlockSpec((TOP_K, tm), lambda i: (0, i)),
                   pl.BlockSpec((TOP_K, tm), lambda i: (0, i)),
                   pl.BlockSpec((TOP_K, tm), lambda i: (0, i)),
                   pl.BlockSpec((N_EXPERTS, LANE), lambda i: (0, 0))],
        out_shape=[kt(jnp.int32), kt(F32), kt(jnp.int32),
                   jax.ShapeDtypeStruct((N_EXPERTS, LANE), F32)],
        scratch_shapes=[pltpu.VMEM((N_EXPERTS, LANE), F32)],
        compiler_params=_cparams(("arbitrary",)),
        name="router",
    )(logits, br, tri)
    return eidx, gw, rank, cnt[:, 0].astype(jnp.int32)


def _sc_mesh():
    return plsc.VectorSubcoreMesh(core_axis_name="core", subcore_axis_name="subcore")


def _dispatch_rows(xp, dest, slots):
    parts, t, _ = xp.shape
    nwin = parts * t // SC_WINDOW
    idx = dest.reshape(TOP_K, t // SC_WINDOW, SC_WINDOW).transpose(1, 0, 2)
    idx = jnp.concatenate([idx + part * slots for part in range(parts)], axis=0)

    @pl.kernel(out_type=jax.ShapeDtypeStruct((parts * slots, SC_ROW), xp.dtype), mesh=_sc_mesh(),
               scratch_types=[], name="moe_dispatch")
    def run(x_hbm, i_hbm, o_hbm):
        def body(x_vmem, i_vmem):
            for k in range(TOP_K):
                pltpu.sync_copy(x_vmem, o_hbm.at[i_vmem.at[0, k]])

        pltpu.emit_pipeline(
            body,
            grid=(nwin,),
            in_specs=[pl.BlockSpec((SC_WINDOW, SC_ROW), lambda i: (i, 0)),
                      pl.BlockSpec((1, TOP_K, SC_WINDOW), lambda i: (i, 0, 0))],
            out_specs=[],
            core_axis_name=("core", "subcore"),
            dimension_semantics=(pltpu.PARALLEL,),
        )(x_hbm, i_hbm)

    return run(xp.reshape(parts * t, SC_ROW), idx).reshape(parts, slots, SC_ROW)


def _gather_rows(yp, idx):
    n = idx.shape[0]
    parts, slots, _ = yp.shape
    idx = jnp.concatenate([idx + part * slots for part in range(parts)]).reshape(1, n * parts)

    @pl.kernel(out_type=jax.ShapeDtypeStruct((n * parts, SC_ROW), yp.dtype), mesh=_sc_mesh(),
               scratch_types=[], name="moe_gather")
    def run(y_hbm, i_hbm, o_hbm):
        def body(i_vmem, o_vmem):
            pltpu.sync_copy(y_hbm.at[i_vmem.at[0]], o_vmem)

        pltpu.emit_pipeline(
            body,
            grid=(n * parts // SC_WINDOW,),
            in_specs=[pl.BlockSpec((1, SC_WINDOW), lambda i: (0, i))],
            out_specs=[pl.BlockSpec((SC_WINDOW, SC_ROW), lambda i: (i, 0))],
            core_axis_name=("core", "subcore"),
            dimension_semantics=(pltpu.PARALLEL,),
        )(i_hbm, o_hbm)

    return run(yp.reshape(parts * slots, SC_ROW), idx).reshape(parts, n, SC_ROW)


def _expert_kernel(be_ref, bv_ref, bs_ref, x_ref, wg_ref, wu_ref, wd_ref, o_ref, wg_s, wu_s, wd_s):
    i = pl.program_id(0)
    valid = bv_ref[i]
    new_expert = (i == 0) | (be_ref[i] != be_ref[jnp.maximum(i - 1, 0)])

    @pl.when(new_expert)
    def _():
        wg_s[...] = wg_ref[0, 0].astype(BF16)
        wu_s[...] = wu_ref[0, 0].astype(BF16)
        wd_s[...] = wd_ref[0, 0].astype(BF16)

    @pl.when(valid > 0)
    def _():
        w = _load_parts(x_ref)
        row = lax.broadcasted_iota(jnp.int32, w.shape, 0)
        w = jnp.where(row < valid, w, jnp.uint32(0))
        lo, hi = _unpack_rows(w)
        x = jnp.concatenate([lo, hi], axis=1).astype(BF16)
        a = _dot(x, wg_s[...])
        hid = a * _sigmoid(a) * _dot(x, wu_s[...])
        _store_parts(o_ref, _pack_rows(_dot(hid.astype(BF16), wd_s[...])))


def _experts(xs, blk_e, blk_valid, blk_src, layer, w_gate, w_up, w_down):
    parts, slots, _ = xs.shape
    d = D_MODEL
    nb = slots // MOE_BLOCK
    return pl.pallas_call(
        _expert_kernel,
        grid_spec=pltpu.PrefetchScalarGridSpec(
            num_scalar_prefetch=3,
            grid=(nb,),
            in_specs=[pl.BlockSpec((parts, MOE_BLOCK, SC_ROW), lambda i, be, bv, bs: (0, bs[i], 0)),
                      pl.BlockSpec((1, 1, d, EXPERT_FF), lambda i, be, bv, bs: (layer, be[i], 0, 0)),
                      pl.BlockSpec((1, 1, d, EXPERT_FF), lambda i, be, bv, bs: (layer, be[i], 0, 0)),
                      pl.BlockSpec((1, 1, EXPERT_FF, d), lambda i, be, bv, bs: (layer, be[i], 0, 0))],
            out_specs=pl.BlockSpec((parts, MOE_BLOCK, SC_ROW), lambda i, be, bv, bs: (0, bs[i], 0)),
            scratch_shapes=[pltpu.VMEM((d, EXPERT_FF), BF16), pltpu.VMEM((d, EXPERT_FF), BF16),
                            pltpu.VMEM((EXPERT_FF, d), BF16)]),
        out_shape=jax.ShapeDtypeStruct((parts, slots, SC_ROW), U32),
        compiler_params=_cparams(("arbitrary",)),
        name="experts",
    )(blk_e, blk_valid, blk_src, xs, w_gate, w_up, w_down)


def _combine_kernel(yg_ref, gw_ref, h_ref, x_ref, gt_ref, wsg_ref, wsu_ref, wsd_ref, gf_ref, o_ref, *, final):
    h = h_ref[0]
    a = _dot(h, wsg_ref[...])
    hid = a * _sigmoid(a) * _dot(h, wsu_ref[...])
    y = _dot(hid.astype(BF16), wsd_ref[...])
    gw = gw_ref[...]
    y_lo = y[:, :D_MODEL // 2]
    y_hi = y[:, D_MODEL // 2:]
    for k in range(TOP_K):
        lo, hi = _unpack_rows(_load_parts(yg_ref, k))
        y_lo = y_lo + gw[:, k:k + 1] * lo
        y_hi = y_hi + gw[:, k:k + 1] * hi
    y = jnp.concatenate([y_lo, y_hi], axis=1)
    xn = x_ref[0] + gt_ref[0] * y
    if final:
        ms = jnp.mean(xn * xn, axis=-1, keepdims=True)
        xn = xn * lax.rsqrt(ms + RMS_EPS) * gf_ref[...]
    o_ref[0] = xn


def _combine(yg, gw, tok_off, h2, x, gt2, b0, nb, ws_gate, ws_up, ws_down, g_final, final, tm):
    b, s, d = x.shape
    per_batch = gt2.shape[0] == b
    mod_map = (lambda bi, i: (b0 + bi, 0, 0)) if per_batch else (lambda bi, i: (0, 0, 0))
    full = lambda arr: pl.BlockSpec(arr.shape, lambda bi, i: (0,) * arr.ndim)
    tok = lambda width: pl.BlockSpec((1, tm, width), lambda bi, i: (b0 + bi, i, 0))
    gf = g_final[None]
    nblk = s // tm
    blk0 = (tok_off + b0 * s) // tm
    return pl.pallas_call(
        functools.partial(_combine_kernel, final=final),
        grid=(nb, nblk),
        in_specs=[pl.BlockSpec((SC_PARTS, TOP_K, tm, SC_ROW), lambda bi, i: (0, 0, bi * nblk + i, 0)),
                  pl.BlockSpec((tm, TOP_K), lambda bi, i: (blk0 + bi * nblk + i, 0)),
                  tok(d), tok(d),
                  pl.BlockSpec((1, 1, d), mod_map),
                  full(ws_gate), full(ws_up), full(ws_down), full(gf)],
        out_specs=tok(d),
        out_shape=jax.ShapeDtypeStruct((b, s, d), F32),
        input_output_aliases={3: 0},
        compiler_params=_cparams(("parallel", "parallel")),
        name="combine",
    )(yg, gw, h2, x, gt2, ws_gate, ws_up, ws_down, gf)


def _layer(x, ctx_s, mods, mods_ctx, p, ctx_out, final, g_final):
    b, s, d = x.shape
    sc = ctx_s.shape[1]
    sh1, sc1, gt1, sh2, sc2, gt2 = mods
    csh1, csc1, cgt1, csh2, csc2, cgt2 = mods_ctx

    w_main, w_gate = _prep_w_in(p['w_in'], p['layer'])
    main, gate = _proj_in(x, p['g_norm1'], sh1, sc1, w_main, w_gate, tm=min(2048, s), tn=1024)
    ctx_flat = ctx_s.reshape(1, b * sc, d)
    n_ctx, tn_ctx = (N_MAIN, 1024) if ctx_out else (N_KV_MAIN, N_KV_MAIN // 2)
    main_c, gate_c = _proj_in(ctx_flat, p['g_norm1'], csh1, csc1, w_main, w_gate, tm=min(1024, b * sc),
                              tn=tn_ctx, n=n_ctx)
    main_c = main_c.reshape(b, sc, n_ctx)
    gate_c = gate_c.reshape(b, sc, LANE)

    o_na = _na_latent(main, main_c, p['na_rpb'])
    o_sc = _short_conv(main, p['conv_w'])

    gate_ws = _gla_gate_weights(p['gla_gate_w'], p['gla_gate_b'])
    s0 = jnp.zeros((b, GLA_HEADS, GLA_DK, GLA_DV), F32)
    o_cf, o_cb, st_f, st_b = _gla_scan(main_c, gate_c, gate_ws, (s0, s0), ctx_out)
    o_f, o_b, _, _ = _gla_scan(main, gate, gate_ws, (st_f, st_b), True)

    w_na = p['w_branch_na'].astype(BF16)
    w_sc = p['w_branch_sc'].astype(BF16)
    w_gla = p['w_branch_gla'].astype(BF16)
    w_out = p['w_out'].astype(BF16)
    w_router = jnp.pad(p['w_router'], ((0, 0), (0, LANE - N_EXPERTS))).astype(BF16)
    x, h2, h2p, logits = _merge(o_na, o_sc, o_f, o_b, main, x, gt1, p['gla_norm_g'], w_na, w_sc, w_gla, w_out,
                                p['g_norm2'], sh2, sc2, w_router, tm=min(512, s))
    n_lat = b * s
    hp_all = h2p.reshape(SC_PARTS, n_lat, SC_ROW)
    lg_all = logits.reshape(n_lat, LANE)
    if ctx_out:
        o_na_c = _dense_attn(main_c)
        o_sc_c = _short_conv(main_c, p['conv_w'])
        ctx_s, h2_c, h2p_c, lg_c = _merge(o_na_c, o_sc_c, o_cf, o_cb, main_c, ctx_s, cgt1, p['gla_norm_g'],
                                          w_na, w_sc, w_gla, w_out, p['g_norm2'], csh2, csc2, w_router,
                                          tm=min(256, sc))
        hp_all = jnp.concatenate([hp_all, h2p_c.reshape(SC_PARTS, b * sc, SC_ROW)], axis=1)
        lg_all = jnp.concatenate([lg_all, lg_c.reshape(b * sc, LANE)], axis=0)

    t = hp_all.shape[1]
    eidx, gw, rank, counts = _route(lg_all, p['b_router'])
    padded = (counts + MOE_BLOCK - 1) // MOE_BLOCK * MOE_BLOCK
    pad_end = jnp.cumsum(padded)
    pad_start = pad_end - padded
    onehot = eidx[:, :, None] == jnp.arange(N_EXPERTS, dtype=jnp.int32)
    dest = jnp.sum(jnp.where(onehot, pad_start, 0), axis=-1) + rank
    n_blocks = -(-(t * TOP_K + N_EXPERTS * (MOE_BLOCK - 1)) // MOE_BLOCK)
    slots = n_blocks * MOE_BLOCK
    blk_start = jnp.arange(n_blocks, dtype=jnp.int32) * MOE_BLOCK
    blk_e = jnp.minimum(jnp.sum(pad_end[None, :] <= blk_start[:, None], axis=1), N_EXPERTS - 1).astype(jnp.int32)
    used_end = (pad_start + counts)[blk_e]
    blk_valid = jnp.clip(used_end - blk_start, 0, MOE_BLOCK).astype(jnp.int32)
    n_used = pad_end[-1] // MOE_BLOCK
    blk_src = jnp.minimum(jnp.arange(n_blocks, dtype=jnp.int32), n_used - 1)
    blk_e = blk_e[blk_src]

    xs = _dispatch_rows(hp_all, dest, slots)
    ys = _experts(xs, blk_e, blk_valid, blk_src, p['layer'], p['w_exp_gate'], p['w_exp_up'], p['w_exp_down'])
    gw_t = gw.T
    ws_gate = p['w_sh_gate'].astype(BF16)
    ws_up = p['w_sh_up'].astype(BF16)
    ws_down = p['w_sh_down'].astype(BF16)

    def gathered(t0, n):
        return _gather_rows(ys, dest[:, t0:t0 + n].reshape(-1)).reshape(SC_PARTS, TOP_K, n, SC_ROW)

    pieces = next(n for n in (4, 2, 1) if b % n == 0)
    nb = b // pieces
    for q in range(pieces):
        x = _combine(gathered(q * nb * s, nb * s), gw_t, 0, h2, x, gt2, q * nb, nb, ws_gate, ws_up, ws_down,
                     g_final, final, tm=min(256, s))
    if ctx_out:
        ctx_s = _combine(gathered(n_lat, b * sc), gw_t, n_lat, h2_c, ctx_s, cgt2, 0, b, ws_gate, ws_up, ws_down,
                         g_final, False, tm=min(256, sc))
    return x, ctx_s


def kernel(x, c, ctx, c_ctx, w_mod, b_mod, g_norm1, g_norm2, w_in, na_rpb, w_branch_na, conv_w, w_branch_sc,
           gla_gate_w, gla_gate_b, gla_norm_g, w_branch_gla, w_out, w_router, b_router, w_exp_gate, w_exp_up,
           w_exp_down, w_sh_gate, w_sh_up, w_sh_down, g_final):
    stacked = dict(g_norm1=g_norm1, g_norm2=g_norm2, na_rpb=na_rpb, w_branch_na=w_branch_na,
                   conv_w=conv_w, w_branch_sc=w_branch_sc, gla_gate_w=gla_gate_w, gla_gate_b=gla_gate_b,
                   gla_norm_g=gla_norm_g, w_branch_gla=w_branch_gla, w_out=w_out, w_router=w_router,
                   b_router=b_router,
                   w_sh_gate=w_sh_gate, w_sh_up=w_sh_up, w_sh_down=w_sh_down)
    depth = w_in.shape[0]
    ctx_s = ctx
    for i in range(depth):
        p = {name: arr[i] for name, arr in stacked.items()}
        p.update(layer=i, w_in=w_in, w_exp_gate=w_exp_gate, w_exp_up=w_exp_up, w_exp_down=w_exp_down)
        mods, mods_ctx = _mod_vectors(c, c_ctx, w_mod[i], b_mod[i])
        last = i == depth - 1
        x, ctx_s = _layer(x, ctx_s, mods, mods_ctx, p, not last, last, g_final)
    return x
```

```python
import functools

import numpy as np
import jax
import jax.numpy as jnp
from jax import lax
from jax.experimental import pallas as pl
from jax.experimental.pallas import tpu as pltpu
from jax.experimental.pallas import tpu_sc as plsc

F32 = jnp.float32
BF16 = jnp.bfloat16
U32 = jnp.uint32

D_MODEL = 1024
N_MOD = 6
RMS_EPS = 1e-6
NEG_INF = -1e30
GRID_W = 64
NA_HEADS = 8
NA_HEAD_DIM = 64
NA_WIDTH = NA_HEADS * NA_HEAD_DIM
NA_WIN_R = 8
NA_WIN_C = 16
NA_GROUP = 4
SC_WIDTH = 512
GLA_HEADS = 4
GLA_KEY_WIDTH = 512
GLA_VAL_WIDTH = 1024
GLA_DK = GLA_KEY_WIDTH // GLA_HEADS
GLA_DV = GLA_VAL_WIDTH // GLA_HEADS
GLA_GATE_RANK = 16
GLA_GATE_TAU = 16.0
LOG2_E = 1.4426950408889634
N_EXPERTS = 64
N_EXPERT_GROUPS = 8
GROUP_SIZE = N_EXPERTS // N_EXPERT_GROUPS
TOPK_GROUPS = 4
TOP_K = 8
EXPERT_FF = 256
ROUTED_SCALE = 2.5
MOE_BLOCK = 1024

LANE = 128
GLA_C = 128
GLA_LEVELS = tuple(GLA_C >> (i + 1) for i in range(GLA_C.bit_length() - 1))
VMEM_LIMIT = 48 * 1024 * 1024
SC_WINDOW = 128
SC_ROW = 256
SC_PARTS = D_MODEL // 2 // SC_ROW

OFF_V_GLA = 0
OFF_K_NA = 1024
OFF_V_NA = 1536
OFF_K_GLA = 2048
N_KV_MAIN = 2560
OFF_Q_NA = 2560
OFF_B_SC = 3072
OFF_C_SC = 3584
OFF_X_SC = 4096
OFF_Q_GLA = 4608
OFF_R_GLA = 5120
OFF_MERGE = 6144
N_MAIN = 9216


def _cparams(sem, vmem=VMEM_LIMIT):
    return pltpu.CompilerParams(dimension_semantics=sem, vmem_limit_bytes=vmem)


def _dot(a, b):
    return jnp.dot(a, b, preferred_element_type=F32)


def _dot_nt(a, b):
    return lax.dot_general(a, b, (((1,), (1,)), ((), ())), preferred_element_type=F32)


def _dot_tn(a, b):
    return lax.dot_general(a, b, (((0,), (0,)), ((), ())), preferred_element_type=F32)


def _sigmoid(x):
    return 0.5 * jnp.tanh(0.5 * x) + 0.5


def _pack_rows(x):
    n = x.shape[1] // 2
    r = x.astype(BF16).astype(F32)
    lo = pltpu.bitcast(r[:, :n], U32) >> 16
    hi = pltpu.bitcast(r[:, n:], U32)
    return hi | lo


def _store_parts(ref, words):
    for part in range(SC_PARTS):
        dst = ref.at[part, 0] if len(ref.shape) == 4 else ref.at[part]
        dst[...] = words[:, part * SC_ROW:(part + 1) * SC_ROW]


def _load_parts(ref, *lead):
    return jnp.concatenate([ref[(part,) + lead] for part in range(SC_PARTS)], axis=-1)


def _unpack_rows(w):
    lo = pltpu.bitcast(w << 16, F32)
    hi = pltpu.bitcast(w & jnp.uint32(0xFFFF0000), F32)
    return lo, hi


def _mod_kernel(a_ref, w_ref, b_ref, o_ref):
    a = a_ref[...]
    a = a * _sigmoid(a)
    o_ref[...] = _dot(a.astype(BF16), w_ref[0].astype(BF16)) + b_ref[0]


def _mod_vectors(c, c_ctx, w_mod, b_mod, layer):
    b = c.shape[0]
    rows = -(-(b + 1) // 8) * 8
    a = jnp.concatenate([c, c_ctx[None], jnp.zeros((rows - b - 1, D_MODEL), F32)], axis=0)
    n = N_MOD * D_MODEL
    tn = 1536
    out = pl.pallas_call(
        _mod_kernel,
        grid=(n // tn,),
        in_specs=[pl.BlockSpec((rows, D_MODEL), lambda j: (0, 0)),
                  pl.BlockSpec((1, D_MODEL, tn), lambda j: (layer, 0, j)),
                  pl.BlockSpec((1, 1, tn), lambda j: (layer, 0, j))],
        out_specs=pl.BlockSpec((rows, tn), lambda j: (0, j)),
        out_shape=jax.ShapeDtypeStruct((rows, n), F32),
        compiler_params=_cparams(("parallel",)),
        name="mod_vectors",
    )(a, w_mod, b_mod[:, None])
    lat = out[:b].reshape(b, N_MOD, 1, D_MODEL)
    ctx = out[b].reshape(N_MOD, 1, 1, D_MODEL)
    return [lat[:, i] for i in range(N_MOD)], [ctx[i] for i in range(N_MOD)]


def _proj_kernel(x_ref, g_ref, sh_ref, sc_ref, w_ref, wg_ref, o_ref, og_ref, h_ref):
    @pl.when(pl.program_id(2) == 0)
    def _():
        x = x_ref[0]
        ms = jnp.mean(x * x, axis=-1, keepdims=True)
        h = x * lax.rsqrt(ms + RMS_EPS) * g_ref[...] * (1.0 + sc_ref[0]) + sh_ref[0]
        hb = h.astype(BF16)
        h_ref[...] = hb
        og_ref[0] = _dot(hb, wg_ref[...])

    o_ref[0] = _dot(h_ref[...], w_ref[...]).astype(o_ref.dtype)


W_IN_TILE = 512
W_IN_GATE_SHIFT = 2 * GLA_GATE_RANK


def _prep_w_in_kernel(a_ref, b_ref, o_ref, g_ref):
    t = pl.program_id(0)
    first_lat = N_KV_MAIN // W_IN_TILE
    a = a_ref[0]

    @pl.when(t < first_lat)
    def _():
        o_ref[...] = a.T.astype(BF16)

    @pl.when(t >= first_lat)
    def _():
        moved = jnp.concatenate([a[W_IN_GATE_SHIFT:], b_ref[0]], axis=0)
        scale = jnp.where(t == first_lat, NA_HEAD_DIM ** -0.5, 1.0)
        o_ref[...] = (moved * scale).T.astype(BF16)

    @pl.when(t == first_lat)
    def _():
        head = a[:LANE]
        row = lax.broadcasted_iota(jnp.int32, head.shape, 0)
        g_ref[...] = jnp.where(row < W_IN_GATE_SHIFT, head, 0.0).T.astype(BF16)


def _prep_w_in(w_in, layer):
    d = w_in.shape[1]
    w_t = jnp.swapaxes(w_in, 1, 2)
    first_lat = N_KV_MAIN // W_IN_TILE
    kv_perm = OFF_K_NA // W_IN_TILE

    def a_map(t):
        return (layer, jnp.where(t < first_lat, (t + first_lat - kv_perm) % first_lat, t), 0)

    def b_map(t):
        return (layer, jnp.where(t < first_lat, 0, (t + 1) * (W_IN_TILE // W_IN_GATE_SHIFT)), 0)

    return pl.pallas_call(
        _prep_w_in_kernel,
        grid=(N_MAIN // W_IN_TILE,),
        in_specs=[pl.BlockSpec((1, W_IN_TILE, d), a_map),
                  pl.BlockSpec((1, W_IN_GATE_SHIFT, d), b_map)],
        out_specs=[pl.BlockSpec((d, W_IN_TILE), lambda t: (0, t)),
                   pl.BlockSpec((d, LANE), lambda t: (0, 0))],
        out_shape=[jax.ShapeDtypeStruct((d, N_MAIN), BF16), jax.ShapeDtypeStruct((d, LANE), BF16)],
        compiler_params=_cparams(("arbitrary",)),
        name="prep_w_in",
    )(w_t, w_t)


def _proj_in(x, g, shift, scale, w_main, w_gate, tm, tn, n=None):
    b, s, d = x.shape
    n = w_main.shape[1] if n is None else n
    per_batch = shift.shape[0] == b
    mod_map = (lambda bi, i, j: (bi, 0, 0)) if per_batch else (lambda bi, i, j: (0, 0, 0))
    return pl.pallas_call(
        _proj_kernel,
        grid=(b, s // tm, n // tn),
        in_specs=[pl.BlockSpec((1, tm, d), lambda bi, i, j: (bi, i, 0)),
                  pl.BlockSpec((1, d), lambda bi, i, j: (0, 0)),
                  pl.BlockSpec((1, 1, d), mod_map),
                  pl.BlockSpec((1, 1, d), mod_map),
                  pl.BlockSpec((d, tn), lambda bi, i, j: (0, j)),
                  pl.BlockSpec((d, LANE), lambda bi, i, j: (0, 0))],
        out_specs=[pl.BlockSpec((1, tm, tn), lambda bi, i, j: (bi, i, j)),
                   pl.BlockSpec((1, tm, LANE), lambda bi, i, j: (bi, i, 0))],
        out_shape=[jax.ShapeDtypeStruct((b, s, n), BF16),
                   jax.ShapeDtypeStruct((b, s, LANE), F32)],
        scratch_shapes=[pltpu.VMEM((tm, d), BF16)],
        compiler_params=_cparams(("parallel", "parallel", "arbitrary")),
        name="proj_in",
    )(x, g[None], shift, scale, w_main, w_gate)


def _softmax_av(q, keys, vals, biases):
    scores = []
    for kk, bb in zip(keys, biases):
        s = _dot_nt(q, kk)
        scores.append(s if bb is None else s + bb)
    m = scores[0].max(axis=-1, keepdims=True)
    for s in scores[1:]:
        m = jnp.maximum(m, s.max(axis=-1, keepdims=True))
    num = None
    den = None
    for s, vv in zip(scores, vals):
        e = jnp.exp(s - m)
        dsum = e.sum(axis=-1, keepdims=True)
        o = _dot(e.astype(BF16), vv)
        num = o if num is None else num + o
        den = dsum if den is None else den + dsum
    return num / den


def _na_kernel(q_ref, k_ref, v_ref, kc_ref, vc_ref, *rest, rows, kr):
    *bias_refs, o_ref = rest
    kc = kc_ref[0]
    vc = vc_ref[0]
    for j, bias_ref in enumerate(bias_refs):
        r = pl.program_id(1) * len(bias_refs) + j
        row_start = jnp.clip(r - kr // 2, 0, rows - kr)
        start = pl.multiple_of(row_start * GRID_W, GRID_W)
        n_win = kr * GRID_W
        q = q_ref[0, j * GRID_W:(j + 1) * GRID_W, :]
        kw = k_ref[0, pl.ds(start, n_win), :]
        vw = v_ref[0, pl.ds(start, n_win), :]
        o_ref[0, j * GRID_W:(j + 1) * GRID_W, :] = _na_row(q, kw, vw, kc, vc, bias_ref).astype(o_ref.dtype)


def _na_row(q, kw, vw, kc, vc, bias_ref):
    gw = NA_GROUP * NA_HEAD_DIM
    stacked = (NA_GROUP * GRID_W, gw)
    on_head = (lax.broadcasted_iota(jnp.int32, stacked, 0) // GRID_W
               == lax.broadcasted_iota(jnp.int32, stacked, 1) // NA_HEAD_DIM)
    outs = []
    for g in range(NA_HEADS // NA_GROUP):
        sl = slice(g * gw, (g + 1) * gw)
        q_all = jnp.where(on_head, jnp.concatenate([q[:, sl]] * NA_GROUP, axis=0), jnp.zeros((), q.dtype))
        bias = bias_ref[0, g * NA_GROUP * GRID_W:(g + 1) * NA_GROUP * GRID_W, :]
        o_all = _softmax_av(q_all, [kw[:, sl], kc[:, sl]], [vw[:, sl], vc[:, sl]], [bias, None])
        o_all = jnp.where(on_head, o_all, 0.0).reshape(NA_GROUP, GRID_W, gw)
        outs.append(o_all.sum(axis=0))
    return jnp.concatenate(outs, axis=-1)


def _na_bias_table(rpb, rows, kr):
    col = np.arange(GRID_W)
    col_start = np.clip(col - NA_WIN_C // 2, 0, GRID_W - NA_WIN_C)
    col_ok = (col[None, :] >= col_start[:, None]) & (col[None, :] < col_start[:, None] + NA_WIN_C)
    d_col = np.clip(col[None, :] - col[:, None], -(NA_WIN_C - 1), NA_WIN_C - 1) + NA_WIN_C - 1
    n_dr, n_dc = rpb.shape[1], rpb.shape[2]
    onehot = jnp.asarray((d_col.reshape(-1)[None, :] == np.arange(n_dc)[:, None]).astype(np.float32))
    by_col = jnp.dot(rpb.astype(F32).reshape(NA_HEADS * n_dr, n_dc), onehot, precision=lax.Precision.HIGHEST)
    by_col = by_col.reshape(NA_HEADS, n_dr, GRID_W, GRID_W)
    by_col = jnp.where(col_ok[None, None], by_col, NEG_INF)
    tables = []
    for o in range(kr):
        lo = NA_WIN_R - 1 - o
        tables.append(by_col[:, lo:lo + kr].transpose(0, 2, 1, 3).reshape(NA_HEADS, GRID_W, kr * GRID_W))
    return jnp.stack(tables).reshape(kr, NA_HEADS * GRID_W, kr * GRID_W)


def _na_latent(main, main_ctx, rpb):
    b, s, _ = main.shape
    sc = main_ctx.shape[1]
    rows = s // GRID_W
    kr = min(NA_WIN_R, rows)
    bias = _na_bias_table(rpb, rows, kr)
    w = NA_WIDTH

    per_step = next(n for n in (4, 2, 1) if rows % n == 0)

    def bias_spec(j):
        def bias_map(bi, i):
            r = i * per_step + j
            return (r - jnp.clip(r - kr // 2, 0, rows - kr), 0, 0)
        return pl.BlockSpec((1, NA_HEADS * GRID_W, kr * GRID_W), bias_map)

    return pl.pallas_call(
        functools.partial(_na_kernel, rows=rows, kr=kr),
        grid=(b, rows // per_step),
        in_specs=[pl.BlockSpec((1, per_step * GRID_W, w), lambda bi, i: (bi, i, OFF_Q_NA // w)),
                  pl.BlockSpec((1, s, w), lambda bi, i: (bi, 0, OFF_K_NA // w)),
                  pl.BlockSpec((1, s, w), lambda bi, i: (bi, 0, OFF_V_NA // w)),
                  pl.BlockSpec((1, sc, w), lambda bi, i: (bi, 0, OFF_K_NA // w)),
                  pl.BlockSpec((1, sc, w), lambda bi, i: (bi, 0, OFF_V_NA // w))]
                 + [bias_spec(j) for j in range(per_step)],
        out_specs=pl.BlockSpec((1, per_step * GRID_W, w), lambda bi, i: (bi, i, 0)),
        out_shape=jax.ShapeDtypeStruct((b, s, w), BF16),
        compiler_params=_cparams(("parallel", "arbitrary")),
        name="na_latent",
    )(main, main, main, main_ctx, main_ctx, *([bias] * per_step))


def _dense_attn_kernel(q_ref, k_ref, v_ref, o_ref):
    q = q_ref[0]
    k = k_ref[0]
    v = v_ref[0]
    outs = []
    for h in range(NA_HEADS):
        sl = slice(h * NA_HEAD_DIM, (h + 1) * NA_HEAD_DIM)
        outs.append(_softmax_av(q[:, sl], [k[:, sl]], [v[:, sl]], [None]))
    o_ref[0] = jnp.concatenate(outs, axis=-1).astype(o_ref.dtype)


def _dense_attn(main_ctx):
    b, sc, _ = main_ctx.shape
    w = NA_WIDTH
    return pl.pallas_call(
        _dense_attn_kernel,
        grid=(b,),
        in_specs=[pl.BlockSpec((1, sc, w), lambda bi: (bi, 0, OFF_Q_NA // w)),
                  pl.BlockSpec((1, sc, w), lambda bi: (bi, 0, OFF_K_NA // w)),
                  pl.BlockSpec((1, sc, w), lambda bi: (bi, 0, OFF_V_NA // w))],
        out_specs=pl.BlockSpec((1, sc, w), lambda bi: (bi, 0, 0)),
        out_shape=jax.ShapeDtypeStruct((b, sc, w), BF16),
        compiler_params=_cparams(("parallel",)),
        name="ctx_attn",
    )(main_ctx, main_ctx, main_ctx)


def _conv_kernel(b_ref, c_ref, x_ref, w_ref, o_ref):
    u = c_ref[0].astype(F32) * x_ref[0].astype(F32)
    s = u.shape[0]
    t = lax.broadcasted_iota(jnp.int32, u.shape, 0)
    prev = jnp.where(t == 0, 0.0, pltpu.roll(u, 1, axis=0))
    nxt = jnp.where(t == s - 1, 0.0, pltpu.roll(u, s - 1, axis=0))
    w = w_ref[...]
    y = b_ref[0].astype(F32) * (prev * w[0:1] + u * w[1:2] + nxt * w[2:3])
    o_ref[0] = y.astype(o_ref.dtype)


def _short_conv(main, conv_w):
    b, s, _ = main.shape
    nt = SC_WIDTH // LANE
    return pl.pallas_call(
        _conv_kernel,
        grid=(b, nt),
        in_specs=[pl.BlockSpec((1, s, LANE), lambda bi, c: (bi, 0, OFF_B_SC // LANE + c)),
                  pl.BlockSpec((1, s, LANE), lambda bi, c: (bi, 0, OFF_C_SC // LANE + c)),
                  pl.BlockSpec((1, s, LANE), lambda bi, c: (bi, 0, OFF_X_SC // LANE + c)),
                  pl.BlockSpec((3, LANE), lambda bi, c: (0, c))],
        out_specs=pl.BlockSpec((1, s, LANE), lambda bi, c: (bi, 0, c)),
        out_shape=jax.ShapeDtypeStruct((b, s, SC_WIDTH), BF16),
        compiler_params=_cparams(("parallel", "parallel")),
        name="short_conv",
    )(main, main, main, conv_w)


def _gla_matrices(reverse):
    c = GLA_C
    t = np.arange(c)[:, None]
    m = np.arange(c)[None, :]
    blocks = [m <= t, m > t]
    for b in GLA_LEVELS:
        first = (t // (2 * b)) * (2 * b) + b
        is_q = (t & b) != 0
        blocks.append(np.where(is_q, (m > first) & (m <= t), (m > t) & (m <= first)))
    mats = np.stack(blocks).astype(np.float32)
    if reverse:
        mats = mats[:, ::-1, ::-1]
    mats = mats.reshape(-1, c)
    return jnp.asarray(np.concatenate([mats, mats], axis=1), dtype=BF16)


def _split_bf16(x):
    hi = x.astype(BF16)
    return hi, (x - hi.astype(F32)).astype(BF16)


def _pair_block_diag(x):
    lane = lax.broadcasted_iota(jnp.int32, x.shape, 1)
    zero = jnp.zeros((), x.dtype)
    return jnp.concatenate([jnp.where(lane < GLA_DK, x, zero), jnp.where(lane >= GLA_DK, x, zero)], axis=0)


def _gla_kernel(*refs, emit):
    n_in = 9 if emit else 8
    n_out = 2 if emit else 1
    ins, outs, scratch = refs[:2 * n_in], refs[2 * n_in:2 * (n_in + n_out)], refs[2 * (n_in + n_out):]
    for d, reverse in enumerate((False, True)):
        _gla_direction(ins[d * n_in:(d + 1) * n_in], outs[d * n_out:(d + 1) * n_out], scratch[d], reverse, emit)


def _gla_direction(ins, outs, st_ref, reverse, emit):
    if emit:
        q_ref, k_ref, v_ref, gt_ref, w2a_ref, w2b_ref, b2_ref, a_ref, s0_ref = ins
        o_ref, sf_ref = outs
    else:
        k_ref, v_ref, gt_ref, w2a_ref, w2b_ref, b2_ref, a_ref, s0_ref = ins
        (sf_ref,) = outs
    c = GLA_C
    pw = 2 * GLA_DK
    step = pl.program_id(1)

    @pl.when(step == 0)
    def _():
        st_ref[...] = s0_ref[0]

    lr_hi, lr_lo = _split_bf16(gt_ref[0])
    logit = (_dot(jnp.concatenate([lr_hi, lr_lo], axis=1), w2a_ref[...]) + _dot(lr_hi, w2b_ref[...])
             + b2_ref[...])
    g = (jnp.minimum(logit, 0.0) - jnp.log1p(jnp.exp(-jnp.abs(logit)))) * (LOG2_E / GLA_GATE_TAU)
    g_hi, g_lo = _split_bf16(g)
    args = _dot(a_ref[...], jnp.concatenate([g_hi, g_lo], axis=0))
    cum = args[0:c]
    rem = args[c:2 * c]
    last_row = cum[0:1] if reverse else cum[c - 1:c]

    k = k_ref[0].astype(F32)
    v = v_ref[0]
    atts = []
    if emit:
        q = q_ref[0].astype(F32) * (GLA_DK ** -0.5)
        ti = lax.broadcasted_iota(jnp.int32, (c, pw), 0)
        si = lax.broadcasted_iota(jnp.int32, (c, pw), 1) & (GLA_DK - 1)
        if reverse:
            ti, si = c - 1 - ti, c - 1 - si
        for hp in range(GLA_HEADS // 2):
            cs = slice(hp * pw, (hp + 1) * pw)
            qp, kp = q[:, cs], k[:, cs]
            att = jnp.where(ti == si, _dot_nt(qp.astype(BF16), _pair_block_diag(kp.astype(BF16))), 0.0)
            for l, b in enumerate(GLA_LEVELS):
                is_q = (ti & b) != 0
                x = (jnp.exp2(args[(2 + l) * c:(3 + l) * c, cs]) * jnp.where(is_q, qp, kp)).astype(BF16)
                pair = (((ti ^ si) >> (b.bit_length() - 1)) == 1) & is_q
                att = jnp.where(pair, _dot_nt(x, _pair_block_diag(x)), att)
            atts.append(att.astype(BF16))

    outs = []
    for h in range(GLA_HEADS):
        sl = slice(h * GLA_DK, (h + 1) * GLA_DK)
        kh = k[:, sl]
        vh = v[:, h * GLA_DV:(h + 1) * GLA_DV]
        state = st_ref[h]
        if emit:
            qd = (q[:, sl] * jnp.exp2(cum[:, sl])).astype(BF16)
            att = atts[h // 2][:, (h % 2) * GLA_DK:(h % 2 + 1) * GLA_DK]
            outs.append(_dot(qd, state.astype(BF16)) + _dot(att, vh))
        kd = (kh * jnp.exp2(rem[:, sl])).astype(BF16)
        decay = jnp.exp2(jnp.broadcast_to(last_row[:, sl], (GLA_DK, GLA_DK))).T
        decay = jnp.concatenate([decay] * (GLA_DV // GLA_DK), axis=1)
        st_ref[h] = decay * state + _dot_tn(kd, vh)

    if emit:
        o_ref[0] = jnp.concatenate(outs, axis=-1).astype(o_ref.dtype)

    @pl.when(step == pl.num_programs(1) - 1)
    def _():
        sf_ref[0] = st_ref[...]


def _gla_scan(main, gate, gate_ws, s0s, emit):
    b, l, _ = main.shape
    n = l // GLA_C
    const = lambda arr: pl.BlockSpec(arr.shape, lambda bi, s: (0,) * arr.ndim)
    state_spec = pl.BlockSpec((1, GLA_HEADS, GLA_DK, GLA_DV), lambda bi, s: (bi, 0, 0, 0))
    state_shape = jax.ShapeDtypeStruct((b, GLA_HEADS, GLA_DK, GLA_DV), F32)
    in_specs, args, out_specs, out_shape = [], [], [], []
    for reverse in (False, True):
        amat = _gla_matrices(reverse)
        w2a, w2b, b2 = gate_ws[reverse]

        def col(block, reverse=reverse):
            return lambda bi, s: (bi, n - 1 - s if reverse else s, block)

        if emit:
            in_specs.append(pl.BlockSpec((1, GLA_C, GLA_KEY_WIDTH), col(OFF_Q_GLA // GLA_KEY_WIDTH)))
            args.append(main)
            out_specs.append(pl.BlockSpec((1, GLA_C, GLA_VAL_WIDTH), col(0)))
            out_shape.append(jax.ShapeDtypeStruct((b, l, GLA_VAL_WIDTH), BF16))
        in_specs += [pl.BlockSpec((1, GLA_C, GLA_KEY_WIDTH), col(OFF_K_GLA // GLA_KEY_WIDTH)),
                     pl.BlockSpec((1, GLA_C, GLA_VAL_WIDTH), col(OFF_V_GLA // GLA_VAL_WIDTH)),
                     pl.BlockSpec((1, GLA_C, LANE), col(0)),
                     const(w2a), const(w2b), const(b2), const(amat), state_spec]
        args += [main, main, gate, w2a, w2b, b2, amat, s0s[reverse]]
        out_specs.append(state_spec)
        out_shape.append(state_shape)
    res = pl.pallas_call(
        functools.partial(_gla_kernel, emit=emit),
        grid=(b, n),
        in_specs=in_specs,
        out_specs=out_specs,
        out_shape=out_shape,
        scratch_shapes=[pltpu.VMEM((GLA_HEADS, GLA_DK, GLA_DV), F32)] * 2,
        compiler_params=_cparams(("parallel", "arbitrary")),
        name="gla_scan",
    )(*args)
    return (res[0], res[2], res[1], res[3]) if emit else (None, None, res[0], res[1])


def _gla_gate_weights(gate_w, gate_b):
    out = []
    for dr in range(2):
        w = jnp.zeros((LANE, GLA_KEY_WIDTH), F32)
        w = w.at[dr * GLA_GATE_RANK:(dr + 1) * GLA_GATE_RANK].set(gate_w[dr])
        w_hi = w.astype(BF16)
        w_lo = (w - w_hi.astype(F32)).astype(BF16)
        out.append((jnp.concatenate([w_hi, w_hi], axis=0), w_lo, gate_b[dr][None]))
    return out


def _merge_kernel(ona_ref, osc_ref, of_ref, ob_ref, r_ref, gna_ref, gsc_ref, ggl_ref, x_ref, gt_ref,
                  gn_ref, wna_ref, wsc_ref, wgl_ref, wo_ref, g2_ref, sh2_ref, sc2_ref, wr_ref,
                  xo_ref, h2_ref, hp_ref, lg_ref):
    o = of_ref[0].astype(F32) + ob_ref[0].astype(F32)
    normed = []
    for h in range(GLA_HEADS):
        oh = o[:, h * GLA_DV:(h + 1) * GLA_DV]
        ms = jnp.mean(oh * oh, axis=-1, keepdims=True)
        normed.append(oh * lax.rsqrt(ms + RMS_EPS))
    r = r_ref[0].astype(F32)
    y_gla = jnp.concatenate(normed, axis=-1) * gn_ref[...] * (r * _sigmoid(r))
    y = (_sigmoid(gna_ref[0].astype(F32)) * _dot(ona_ref[0], wna_ref[...])
         + _sigmoid(gsc_ref[0].astype(F32)) * _dot(osc_ref[0], wsc_ref[...])
         + _sigmoid(ggl_ref[0].astype(F32)) * _dot(y_gla.astype(BF16), wgl_ref[...]))
    xn = x_ref[0] + gt_ref[0] * _dot(y.astype(BF16), wo_ref[...])
    xo_ref[0] = xn
    ms = jnp.mean(xn * xn, axis=-1, keepdims=True)
    h2 = xn * lax.rsqrt(ms + RMS_EPS) * g2_ref[...] * (1.0 + sc2_ref[0]) + sh2_ref[0]
    h2b = h2.astype(BF16)
    h2_ref[0] = h2b
    _store_parts(hp_ref, _pack_rows(h2))
    lg_ref[0] = _dot(h2b, wr_ref[...])


def _merge(o_na, o_sc, o_f, o_b, main, x, gt1, gn, w_na, w_sc, w_gla, w_out, g2, sh2, sc2, w_router, tm):
    b, s, d = x.shape
    per_batch = gt1.shape[0] == b
    mod_map = (lambda bi, i: (bi, 0, 0)) if per_batch else (lambda bi, i: (0, 0, 0))
    tok = lambda width, blk: pl.BlockSpec((1, tm, width), lambda bi, i: (bi, i, blk))
    full = lambda arr: pl.BlockSpec(arr.shape, lambda bi, i: (0,) * arr.ndim)
    mod = pl.BlockSpec((1, 1, d), mod_map)
    gn_t = jnp.tile(gn, GLA_HEADS)[None]
    g2_t = g2[None]
    return pl.pallas_call(
        _merge_kernel,
        grid=(b, s // tm),
        in_specs=[tok(NA_WIDTH, 0), tok(SC_WIDTH, 0), tok(GLA_VAL_WIDTH, 0), tok(GLA_VAL_WIDTH, 0),
                  tok(d, OFF_R_GLA // d), tok(d, OFF_MERGE // d), tok(d, OFF_MERGE // d + 1),
                  tok(d, OFF_MERGE // d + 2), tok(d, 0), mod,
                  full(gn_t), full(w_na), full(w_sc), full(w_gla), full(w_out), full(g2_t), mod, mod,
                  full(w_router)],
        out_specs=[tok(d, 0), tok(d, 0),
                   pl.BlockSpec((SC_PARTS, 1, tm, SC_ROW), lambda bi, i: (0, bi, i, 0)), tok(LANE, 0)],
        out_shape=[jax.ShapeDtypeStruct((b, s, d), F32),
                   jax.ShapeDtypeStruct((b, s, d), BF16),
                   jax.ShapeDtypeStruct((SC_PARTS, b, s, SC_ROW), U32),
                   jax.ShapeDtypeStruct((b, s, LANE), F32)],
        compiler_params=_cparams(("parallel", "parallel")),
        name="merge",
    )(o_na, o_sc, o_f, o_b, main, main, main, main, x, gt1, gn_t, w_na, w_sc, w_gla, w_out, g2_t,
      sh2, sc2, w_router)


def _router_kernel(lg_ref, br_ref, tri_ref, eidx_ref, gw_ref, rank_ref, cnt_ref, carry_ref):
    tm = lg_ref.shape[0]

    @pl.when(pl.program_id(0) == 0)
    def _():
        carry_ref[...] = jnp.zeros_like(carry_ref)

    scores = _sigmoid(lg_ref[...].T[:N_EXPERTS])
    sel = scores + br_ref[...]
    neg = -jnp.inf

    sel3 = sel.reshape(N_EXPERT_GROUPS, GROUP_SIZE, tm)
    i3 = lax.broadcasted_iota(jnp.int32, sel3.shape, 1)
    m1 = sel3.max(axis=1, keepdims=True)
    first = jnp.where(sel3 == m1, i3, GROUP_SIZE).min(axis=1, keepdims=True)
    m2 = jnp.where(i3 == first, neg, sel3).max(axis=1, keepdims=True)
    gscore = (m1 + m2)[:, 0, :]

    gi = lax.broadcasted_iota(jnp.int32, gscore.shape, 0)
    gmask = jnp.zeros(gscore.shape, jnp.bool_)
    for _ in range(TOPK_GROUPS):
        m = gscore.max(axis=0, keepdims=True)
        pick = gi == jnp.where(gscore == m, gi, N_EXPERT_GROUPS).min(axis=0, keepdims=True)
        gmask = gmask | pick
        gscore = jnp.where(pick, neg, gscore)
    emask = jnp.broadcast_to(gmask[:, None, :], sel3.shape).reshape(N_EXPERTS, tm)
    sel = jnp.where(emask, sel, neg)

    ei = lax.broadcasted_iota(jnp.int32, sel.shape, 0)
    picks, idxs, ws = [], [], []
    for _ in range(TOP_K):
        m = sel.max(axis=0, keepdims=True)
        idx = jnp.where(sel == m, ei, N_EXPERTS).min(axis=0, keepdims=True)
        pick = ei == idx
        picks.append(pick)
        idxs.append(idx)
        ws.append(jnp.where(pick, scores, 0.0).sum(axis=0, keepdims=True))
        sel = jnp.where(pick, neg, sel)
    w = jnp.concatenate(ws, axis=0)
    gw_ref[...] = w / w.sum(axis=0, keepdims=True) * ROUTED_SCALE
    eidx_ref[...] = jnp.concatenate(idxs, axis=0)

    onehot = picks[0]
    for p in picks[1:]:
        onehot = onehot | p
    onehot = jnp.where(onehot, 1.0, 0.0).astype(BF16)
    before = _dot(onehot, tri_ref[...]) + jnp.tile(carry_ref[...], (1, tm // LANE))
    rank_ref[...] = jnp.concatenate(
        [jnp.where(p, before, 0.0).sum(axis=0, keepdims=True) for p in picks], axis=0).astype(jnp.int32)
    carry_ref[...] += _dot(onehot, jnp.ones((tm, LANE), BF16))
    cnt_ref[...] = carry_ref[...]


def _route(logits, b_router, tm=512):
    t = logits.shape[0]
    br = jnp.broadcast_to(b_router.astype(F32)[:, None], (N_EXPERTS, tm))
    tri = jnp.asarray(np.triu(np.ones((tm, tm), np.float32), 1), dtype=BF16)
    kt = lambda dt: jax.ShapeDtypeStruct((TOP_K, t), dt)
    eidx, gw, rank, cnt = pl.pallas_call(
        _router_kernel,
        grid=(t // tm,),
        in_specs=[pl.BlockSpec((tm, LANE), lambda i: (i, 0)),
                  pl.BlockSpec((N_EXPERTS, tm), lambda i: (0, 0)),
                  pl.BlockSpec((tm, tm), lambda i: (0, 0))],
        out_specs=[pl.BlockSpec((TOP_K, tm), lambda i: (0, i)),
                   pl.BlockSpec((TOP_K, tm), lambda i: (0, i)),
                   pl.BlockSpec((TOP_K, tm), lambda i: (0, i)),
                   pl.BlockSpec((N_EXPERTS, LANE), lambda i: (0, 0))],
        out_shape=[kt(jnp.int32), kt(F32), kt(jnp.int32),
                   jax.ShapeDtypeStruct((N_EXPERTS, LANE), F32)],
        scratch_shapes=[pltpu.VMEM((N_EXPERTS, LANE), F32)],
        compiler_params=_cparams(("arbitrary",)),
        name="router",
    )(logits, br, tri)
    return eidx, gw, rank, cnt[:, 0].astype(jnp.int32)


def _sc_mesh():
    return plsc.VectorSubcoreMesh(core_axis_name="core", subcore_axis_name="subcore")


def _dispatch_rows(xp, dest, slots):
    parts, t, _ = xp.shape
    nwin = parts * t // SC_WINDOW
    idx = dest.reshape(TOP_K, t // SC_WINDOW, SC_WINDOW).transpose(1, 0, 2)
    idx = jnp.concatenate([idx + part * slots for part in range(parts)], axis=0)

    @pl.kernel(out_type=jax.ShapeDtypeStruct((parts * slots, SC_ROW), xp.dtype), mesh=_sc_mesh(),
               scratch_types=[], name="moe_dispatch")
    def run(x_hbm, i_hbm, o_hbm):
        def body(x_vmem, i_vmem):
            for k in range(TOP_K):
                pltpu.sync_copy(x_vmem, o_hbm.at[i_vmem.at[0, k]])

        pltpu.emit_pipeline(
            body,
            grid=(nwin,),
            in_specs=[pl.BlockSpec((SC_WINDOW, SC_ROW), lambda i: (i, 0)),
                      pl.BlockSpec((1, TOP_K, SC_WINDOW), lambda i: (i, 0, 0))],
            out_specs=[],
            core_axis_name=("core", "subcore"),
            dimension_semantics=(pltpu.PARALLEL,),
        )(x_hbm, i_hbm)

    return run(xp.reshape(parts * t, SC_ROW), idx).reshape(parts, slots, SC_ROW)


def _gather_rows(yp, idx):
    n = idx.shape[0]
    parts, slots, _ = yp.shape
    idx = jnp.concatenate([idx + part * slots for part in range(parts)]).reshape(1, n * parts)

    @pl.kernel(out_type=jax.ShapeDtypeStruct((n * parts, SC_ROW), yp.dtype), mesh=_sc_mesh(),
               scratch_types=[], name="moe_gather")
    def run(y_hbm, i_hbm, o_hbm):
        def body(i_vmem, o_vmem):
            pltpu.sync_copy(y_hbm.at[i_vmem.at[0]], o_vmem)

        pltpu.emit_pipeline(
            body,
            grid=(n * parts // SC_WINDOW,),
            in_specs=[pl.BlockSpec((1, SC_WINDOW), lambda i: (0, i))],
            out_specs=[pl.BlockSpec((SC_WINDOW, SC_ROW), lambda i: (i, 0))],
            core_axis_name=("core", "subcore"),
            dimension_semantics=(pltpu.PARALLEL,),
        )(i_hbm, o_hbm)

    return run(yp.reshape(parts * slots, SC_ROW), idx).reshape(parts, n, SC_ROW)


def _expert_kernel(be_ref, bv_ref, bs_ref, x_ref, wg_ref, wu_ref, wd_ref, o_ref, wg_s, wu_s, wd_s):
    i = pl.program_id(0)
    valid = bv_ref[i]
    new_expert = (i == 0) | (be_ref[i] != be_ref[jnp.maximum(i - 1, 0)])

    @pl.when(new_expert)
    def _():
        wg_s[...] = wg_ref[0, 0].astype(BF16)
        wu_s[...] = wu_ref[0, 0].astype(BF16)
        wd_s[...] = wd_ref[0, 0].astype(BF16)

    @pl.when(valid > 0)
    def _():
        w = _load_parts(x_ref)
        row = lax.broadcasted_iota(jnp.int32, w.shape, 0)
        w = jnp.where(row < valid, w, jnp.uint32(0))
        lo, hi = _unpack_rows(w)
        x = jnp.concatenate([lo, hi], axis=1).astype(BF16)
        a = _dot(x, wg_s[...])
        hid = a * _sigmoid(a) * _dot(x, wu_s[...])
        _store_parts(o_ref, _pack_rows(_dot(hid.astype(BF16), wd_s[...])))


def _experts(xs, blk_e, blk_valid, blk_src, layer, w_gate, w_up, w_down):
    parts, slots, _ = xs.shape
    d = D_MODEL
    nb = slots // MOE_BLOCK
    return pl.pallas_call(
        _expert_kernel,
        grid_spec=pltpu.PrefetchScalarGridSpec(
            num_scalar_prefetch=3,
            grid=(nb,),
            in_specs=[pl.BlockSpec((parts, MOE_BLOCK, SC_ROW), lambda i, be, bv, bs: (0, bs[i], 0)),
                      pl.BlockSpec((1, 1, d, EXPERT_FF), lambda i, be, bv, bs: (layer, be[i], 0, 0)),
                      pl.BlockSpec((1, 1, d, EXPERT_FF), lambda i, be, bv, bs: (layer, be[i], 0, 0)),
                      pl.BlockSpec((1, 1, EXPERT_FF, d), lambda i, be, bv, bs: (layer, be[i], 0, 0))],
            out_specs=pl.BlockSpec((parts, MOE_BLOCK, SC_ROW), lambda i, be, bv, bs: (0, bs[i], 0)),
            scratch_shapes=[pltpu.VMEM((d, EXPERT_FF), BF16), pltpu.VMEM((d, EXPERT_FF), BF16),
                            pltpu.VMEM((EXPERT_FF, d), BF16)]),
        out_shape=jax.ShapeDtypeStruct((parts, slots, SC_ROW), U32),
        compiler_params=_cparams(("arbitrary",)),
        name="experts",
    )(blk_e, blk_valid, blk_src, xs, w_gate, w_up, w_down)


def _combine_kernel(yg_ref, gw_ref, h_ref, x_ref, gt_ref, wsg_ref, wsu_ref, wsd_ref, gf_ref, o_ref, *, final):
    h = h_ref[0]
    a = _dot(h, wsg_ref[...])
    hid = a * _sigmoid(a) * _dot(h, wsu_ref[...])
    y = _dot(hid.astype(BF16), wsd_ref[...])
    gw = gw_ref[...]
    y_lo = y[:, :D_MODEL // 2]
    y_hi = y[:, D_MODEL // 2:]
    for k in range(TOP_K):
        lo, hi = _unpack_rows(_load_parts(yg_ref, k))
        y_lo = y_lo + gw[:, k:k + 1] * lo
        y_hi = y_hi + gw[:, k:k + 1] * hi
    y = jnp.concatenate([y_lo, y_hi], axis=1)
    xn = x_ref[0] + gt_ref[0] * y
    if final:
        ms = jnp.mean(xn * xn, axis=-1, keepdims=True)
        xn = xn * lax.rsqrt(ms + RMS_EPS) * gf_ref[...]
    o_ref[0] = xn


def _combine(yg, gw, tok_off, h2, x, gt2, b0, nb, ws_gate, ws_up, ws_down, g_final, final, tm):
    b, s, d = x.shape
    per_batch = gt2.shape[0] == b
    mod_map = (lambda bi, i: (b0 + bi, 0, 0)) if per_batch else (lambda bi, i: (0, 0, 0))
    full = lambda arr: pl.BlockSpec(arr.shape, lambda bi, i: (0,) * arr.ndim)
    tok = lambda width: pl.BlockSpec((1, tm, width), lambda bi, i: (b0 + bi, i, 0))
    gf = g_final[None]
    nblk = s // tm
    blk0 = (tok_off + b0 * s) // tm
    return pl.pallas_call(
        functools.partial(_combine_kernel, final=final),
        grid=(nb, nblk),
        in_specs=[pl.BlockSpec((SC_PARTS, TOP_K, tm, SC_ROW), lambda bi, i: (0, 0, bi * nblk + i, 0)),
                  pl.BlockSpec((tm, TOP_K), lambda bi, i: (blk0 + bi * nblk + i, 0)),
                  tok(d), tok(d),
                  pl.BlockSpec((1, 1, d), mod_map),
                  full(ws_gate), full(ws_up), full(ws_down), full(gf)],
        out_specs=tok(d),
        out_shape=jax.ShapeDtypeStruct((b, s, d), F32),
        input_output_aliases={3: 0},
        compiler_params=_cparams(("parallel", "parallel")),
        name="combine",
    )(yg, gw, h2, x, gt2, ws_gate, ws_up, ws_down, gf)


def _layer(x, ctx_s, mods, mods_ctx, p, ctx_out, final, g_final):
    b, s, d = x.shape
    sc = ctx_s.shape[1]
    sh1, sc1, gt1, sh2, sc2, gt2 = mods
    csh1, csc1, cgt1, csh2, csc2, cgt2 = mods_ctx

    w_main, w_gate = _prep_w_in(p['w_in'], p['layer'])
    main, gate = _proj_in(x, p['g_norm1'], sh1, sc1, w_main, w_gate, tm=min(2048, s), tn=1024)
    ctx_flat = ctx_s.reshape(1, b * sc, d)
    n_ctx, tn_ctx = (N_MAIN, 1024) if ctx_out else (N_KV_MAIN, N_KV_MAIN // 2)
    main_c, gate_c = _proj_in(ctx_flat, p['g_norm1'], csh1, csc1, w_main, w_gate, tm=min(1024, b * sc),
                              tn=tn_ctx, n=n_ctx)
    main_c = main_c.reshape(b, sc, n_ctx)
    gate_c = gate_c.reshape(b, sc, LANE)

    o_na = _na_latent(main, main_c, p['na_rpb'])
    o_sc = _short_conv(main, p['conv_w'])

    gate_ws = _gla_gate_weights(p['gla_gate_w'], p['gla_gate_b'])
    s0 = jnp.zeros((b, GLA_HEADS, GLA_DK, GLA_DV), F32)
    o_cf, o_cb, st_f, st_b = _gla_scan(main_c, gate_c, gate_ws, (s0, s0), ctx_out)
    o_f, o_b, _, _ = _gla_scan(main, gate, gate_ws, (st_f, st_b), True)

    w_na = p['w_branch_na'].astype(BF16)
    w_sc = p['w_branch_sc'].astype(BF16)
    w_gla = p['w_branch_gla'].astype(BF16)
    w_out = p['w_out'].astype(BF16)
    w_router = jnp.pad(p['w_router'], ((0, 0), (0, LANE - N_EXPERTS))).astype(BF16)
    x, h2, h2p, logits = _merge(o_na, o_sc, o_f, o_b, main, x, gt1, p['gla_norm_g'], w_na, w_sc, w_gla, w_out,
                                p['g_norm2'], sh2, sc2, w_router, tm=min(512, s))
    n_lat = b * s
    hp_all = h2p.reshape(SC_PARTS, n_lat, SC_ROW)
    lg_all = logits.reshape(n_lat, LANE)
    if ctx_out:
        o_na_c = _dense_attn(main_c)
        o_sc_c = _short_conv(main_c, p['conv_w'])
        ctx_s, h2_c, h2p_c, lg_c = _merge(o_na_c, o_sc_c, o_cf, o_cb, main_c, ctx_s, cgt1, p['gla_norm_g'],
                                          w_na, w_sc, w_gla, w_out, p['g_norm2'], csh2, csc2, w_router,
                                          tm=min(256, sc))
        hp_all = jnp.concatenate([hp_all, h2p_c.reshape(SC_PARTS, b * sc, SC_ROW)], axis=1)
        lg_all = jnp.concatenate([lg_all, lg_c.reshape(b * sc, LANE)], axis=0)

    t = hp_all.shape[1]
    eidx, gw, rank, counts = _route(lg_all, p['b_router'])
    padded = (counts + MOE_BLOCK - 1) // MOE_BLOCK * MOE_BLOCK
    pad_end = jnp.cumsum(padded)
    pad_start = pad_end - padded
    onehot = eidx[:, :, None] == jnp.arange(N_EXPERTS, dtype=jnp.int32)
    dest = jnp.sum(jnp.where(onehot, pad_start, 0), axis=-1) + rank
    n_blocks = -(-(t * TOP_K + N_EXPERTS * (MOE_BLOCK - 1)) // MOE_BLOCK)
    slots = n_blocks * MOE_BLOCK
    blk_start = jnp.arange(n_blocks, dtype=jnp.int32) * MOE_BLOCK
    blk_e = jnp.minimum(jnp.sum(pad_end[None, :] <= blk_start[:, None], axis=1), N_EXPERTS - 1).astype(jnp.int32)
    used_end = (pad_start + counts)[blk_e]
    blk_valid = jnp.clip(used_end - blk_start, 0, MOE_BLOCK).astype(jnp.int32)
    n_used = pad_end[-1] // MOE_BLOCK
    blk_src = jnp.minimum(jnp.arange(n_blocks, dtype=jnp.int32), n_used - 1)
    blk_e = blk_e[blk_src]

    xs = _dispatch_rows(hp_all, dest, slots)
    ys = _experts(xs, blk_e, blk_valid, blk_src, p['layer'], p['w_exp_gate'], p['w_exp_up'], p['w_exp_down'])
    gw_t = gw.T
    ws_gate = p['w_sh_gate'].astype(BF16)
    ws_up = p['w_sh_up'].astype(BF16)
    ws_down = p['w_sh_down'].astype(BF16)

    def gathered(t0, n):
        return _gather_rows(ys, dest[:, t0:t0 + n].reshape(-1)).reshape(SC_PARTS, TOP_K, n, SC_ROW)

    pieces = next(n for n in (4, 2, 1) if b % n == 0)
    nb = b // pieces
    for q in range(pieces):
        x = _combine(gathered(q * nb * s, nb * s), gw_t, 0, h2, x, gt2, q * nb, nb, ws_gate, ws_up, ws_down,
                     g_final, final, tm=min(256, s))
    if ctx_out:
        ctx_s = _combine(gathered(n_lat, b * sc), gw_t, n_lat, h2_c, ctx_s, cgt2, 0, b, ws_gate, ws_up, ws_down,
                         g_final, False, tm=min(256, sc))
    return x, ctx_s


def kernel(x, c, ctx, c_ctx, w_mod, b_mod, g_norm1, g_norm2, w_in, na_rpb, w_branch_na, conv_w, w_branch_sc,
           gla_gate_w, gla_gate_b, gla_norm_g, w_branch_gla, w_out, w_router, b_router, w_exp_gate, w_exp_up,
           w_exp_down, w_sh_gate, w_sh_up, w_sh_down, g_final):
    stacked = dict(g_norm1=g_norm1, g_norm2=g_norm2, na_rpb=na_rpb, w_branch_na=w_branch_na,
                   conv_w=conv_w, w_branch_sc=w_branch_sc, gla_gate_w=gla_gate_w, gla_gate_b=gla_gate_b,
                   gla_norm_g=gla_norm_g, w_branch_gla=w_branch_gla, w_out=w_out, w_router=w_router,
                   b_router=b_router,
                   w_sh_gate=w_sh_gate, w_sh_up=w_sh_up, w_sh_down=w_sh_down)
    depth = w_in.shape[0]
    ctx_s = ctx
    for i in range(depth):
        p = {name: arr[i] for name, arr in stacked.items()}
        p.update(layer=i, w_in=w_in, w_exp_gate=w_exp_gate, w_exp_up=w_exp_up, w_exp_down=w_exp_down)
        mods, mods_ctx = _mod_vectors(c, c_ctx, w_mod, b_mod, i)
        last = i == depth - 1
        x, ctx_s = _layer(x, ctx_s, mods, mods_ctx, p, not last, last, g_final)
    return x
```

```python
import functools

import numpy as np
import jax
import jax.numpy as jnp
from jax import lax
from jax.experimental import pallas as pl
from jax.experimental.pallas import tpu as pltpu
from jax.experimental.pallas import tpu_sc as plsc

F32 = jnp.float32
BF16 = jnp.bfloat16
U32 = jnp.uint32

D_MODEL = 1024
N_MOD = 6
RMS_EPS = 1e-6
NEG_INF = -1e30
GRID_W = 64
NA_HEADS = 8
NA_HEAD_DIM = 64
NA_WIDTH = NA_HEADS * NA_HEAD_DIM
NA_WIN_R = 8
NA_WIN_C = 16
NA_GROUP = 4
SC_WIDTH = 512
GLA_HEADS = 4
GLA_KEY_WIDTH = 512
GLA_VAL_WIDTH = 1024
GLA_DK = GLA_KEY_WIDTH // GLA_HEADS
GLA_DV = GLA_VAL_WIDTH // GLA_HEADS
GLA_GATE_RANK = 16
GLA_GATE_TAU = 16.0
LOG2_E = 1.4426950408889634
N_EXPERTS = 64
N_EXPERT_GROUPS = 8
GROUP_SIZE = N_EXPERTS // N_EXPERT_GROUPS
TOPK_GROUPS = 4
TOP_K = 8
EXPERT_FF = 256
ROUTED_SCALE = 2.5
MOE_BLOCK = 1024

LANE = 128
GLA_C = 128
GLA_LEVELS = tuple(GLA_C >> (i + 1) for i in range(GLA_C.bit_length() - 1))
VMEM_LIMIT = 48 * 1024 * 1024
SC_WINDOW = 128
SC_ROW = 256
SC_PARTS = D_MODEL // 2 // SC_ROW

OFF_V_GLA = 0
OFF_K_NA = 1024
OFF_V_NA = 1536
OFF_K_GLA = 2048
N_KV_MAIN = 2560
OFF_Q_NA = 2560
OFF_B_SC = 3072
OFF_C_SC = 3584
OFF_X_SC = 4096
OFF_Q_GLA = 4608
OFF_R_GLA = 5120
OFF_MERGE = 6144
N_MAIN = 9216


def _cparams(sem, vmem=VMEM_LIMIT):
    return pltpu.CompilerParams(dimension_semantics=sem, vmem_limit_bytes=vmem)


def _dot(a, b):
    return jnp.dot(a, b, preferred_element_type=F32)


def _dot_nt(a, b):
    return lax.dot_general(a, b, (((1,), (1,)), ((), ())), preferred_element_type=F32)


def _dot_tn(a, b):
    return lax.dot_general(a, b, (((0,), (0,)), ((), ())), preferred_element_type=F32)


def _sigmoid(x):
    return 0.5 * jnp.tanh(0.5 * x) + 0.5


def _pack_rows(x):
    n = x.shape[1] // 2
    r = x.astype(BF16).astype(F32)
    lo = pltpu.bitcast(r[:, :n], U32) >> 16
    hi = pltpu.bitcast(r[:, n:], U32)
    return hi | lo


def _store_parts(ref, words):
    for part in range(SC_PARTS):
        dst = ref.at[part, 0] if len(ref.shape) == 4 else ref.at[part]
        dst[...] = words[:, part * SC_ROW:(part + 1) * SC_ROW]


def _load_parts(ref, *lead):
    return jnp.concatenate([ref[(part,) + lead] for part in range(SC_PARTS)], axis=-1)


def _unpack_rows(w):
    lo = pltpu.bitcast(w << 16, F32)
    hi = pltpu.bitcast(w & jnp.uint32(0xFFFF0000), F32)
    return lo, hi


def _mod_kernel(a_ref, w_ref, b_ref, o_ref):
    a = a_ref[...]
    a = a * _sigmoid(a)
    o_ref[...] = _dot(a.astype(BF16), w_ref[0].astype(BF16)) + b_ref[0]


def _mod_vectors(c, c_ctx, w_mod, b_mod, layer):
    b = c.shape[0]
    rows = -(-(b + 1) // 8) * 8
    a = jnp.concatenate([c, c_ctx[None], jnp.zeros((rows - b - 1, D_MODEL), F32)], axis=0)
    n = N_MOD * D_MODEL
    tn = 1536
    out = pl.pallas_call(
        _mod_kernel,
        grid=(n // tn,),
        in_specs=[pl.BlockSpec((rows, D_MODEL), lambda j: (0, 0)),
                  pl.BlockSpec((1, D_MODEL, tn), lambda j: (layer, 0, j)),
                  pl.BlockSpec((1, 1, tn), lambda j: (layer, 0, j))],
        out_specs=pl.BlockSpec((rows, tn), lambda j: (0, j)),
        out_shape=jax.ShapeDtypeStruct((rows, n), F32),
        compiler_params=_cparams(("parallel",)),
        name="mod_vectors",
    )(a, w_mod, b_mod[:, None])
    lat = out[:b].reshape(b, N_MOD, 1, D_MODEL)
    ctx = out[b].reshape(N_MOD, 1, 1, D_MODEL)
    return [lat[:, i] for i in range(N_MOD)], [ctx[i] for i in range(N_MOD)]


def _proj_kernel(x_ref, g_ref, sh_ref, sc_ref, w_ref, wg_ref, o_ref, og_ref, h_ref):
    @pl.when(pl.program_id(2) == 0)
    def _():
        x = x_ref[0]
        ms = jnp.mean(x * x, axis=-1, keepdims=True)
        h = x * lax.rsqrt(ms + RMS_EPS) * g_ref[...] * (1.0 + sc_ref[0]) + sh_ref[0]
        hb = h.astype(BF16)
        h_ref[...] = hb
        og_ref[0] = _dot(hb, wg_ref[...])

    o_ref[0] = _dot(h_ref[...], w_ref[...]).astype(o_ref.dtype)


W_IN_TILE = 512
W_IN_GATE_SHIFT = 2 * GLA_GATE_RANK


def _prep_w_in_kernel(a_ref, b_ref, o_ref, g_ref):
    t = pl.program_id(0)
    first_lat = N_KV_MAIN // W_IN_TILE
    a = a_ref[0]

    @pl.when(t < first_lat)
    def _():
        o_ref[...] = a.T.astype(BF16)

    @pl.when(t >= first_lat)
    def _():
        moved = jnp.concatenate([a[W_IN_GATE_SHIFT:], b_ref[0]], axis=0)
        scale = jnp.where(t == first_lat, NA_HEAD_DIM ** -0.5, 1.0)
        o_ref[...] = (moved * scale).T.astype(BF16)

    @pl.when(t == first_lat)
    def _():
        head = a[:LANE]
        row = lax.broadcasted_iota(jnp.int32, head.shape, 0)
        g_ref[...] = jnp.where(row < W_IN_GATE_SHIFT, head, 0.0).T.astype(BF16)


def _prep_w_in(w_in, layer):
    d = w_in.shape[1]
    w_t = jnp.swapaxes(w_in, 1, 2)
    first_lat = N_KV_MAIN // W_IN_TILE
    kv_perm = OFF_K_NA // W_IN_TILE

    def a_map(t):
        return (layer, jnp.where(t < first_lat, (t + first_lat - kv_perm) % first_lat, t), 0)

    def b_map(t):
        return (layer, jnp.where(t < first_lat, 0, (t + 1) * (W_IN_TILE // W_IN_GATE_SHIFT)), 0)

    return pl.pallas_call(
        _prep_w_in_kernel,
        grid=(N_MAIN // W_IN_TILE,),
        in_specs=[pl.BlockSpec((1, W_IN_TILE, d), a_map),
                  pl.BlockSpec((1, W_IN_GATE_SHIFT, d), b_map)],
        out_specs=[pl.BlockSpec((d, W_IN_TILE), lambda t: (0, t)),
                   pl.BlockSpec((d, LANE), lambda t: (0, 0))],
        out_shape=[jax.ShapeDtypeStruct((d, N_MAIN), BF16), jax.ShapeDtypeStruct((d, LANE), BF16)],
        compiler_params=_cparams(("arbitrary",)),
        name="prep_w_in",
    )(w_t, w_t)


def _proj_in(x, g, shift, scale, w_main, w_gate, tm, tn, n=None):
    b, s, d = x.shape
    n = w_main.shape[1] if n is None else n
    per_batch = shift.shape[0] == b
    mod_map = (lambda bi, i, j: (bi, 0, 0)) if per_batch else (lambda bi, i, j: (0, 0, 0))
    return pl.pallas_call(
        _proj_kernel,
        grid=(b, s // tm, n // tn),
        in_specs=[pl.BlockSpec((1, tm, d), lambda bi, i, j: (bi, i, 0)),
                  pl.BlockSpec((1, d), lambda bi, i, j: (0, 0)),
                  pl.BlockSpec((1, 1, d), mod_map),
                  pl.BlockSpec((1, 1, d), mod_map),
                  pl.BlockSpec((d, tn), lambda bi, i, j: (0, j)),
                  pl.BlockSpec((d, LANE), lambda bi, i, j: (0, 0))],
        out_specs=[pl.BlockSpec((1, tm, tn), lambda bi, i, j: (bi, i, j)),
                   pl.BlockSpec((1, tm, LANE), lambda bi, i, j: (bi, i, 0))],
        out_shape=[jax.ShapeDtypeStruct((b, s, n), BF16),
                   jax.ShapeDtypeStruct((b, s, LANE), F32)],
        scratch_shapes=[pltpu.VMEM((tm, d), BF16)],
        compiler_params=_cparams(("parallel", "parallel", "arbitrary")),
        name="proj_in",
    )(x, g[None], shift, scale, w_main, w_gate)


def _softmax_av(q, keys, vals, biases):
    scores = []
    for kk, bb in zip(keys, biases):
        s = _dot_nt(q, kk)
        scores.append(s if bb is None else s + bb)
    m = scores[0].max(axis=-1, keepdims=True)
    for s in scores[1:]:
        m = jnp.maximum(m, s.max(axis=-1, keepdims=True))
    num = None
    den = None
    for s, vv in zip(scores, vals):
        e = jnp.exp(s - m)
        dsum = e.sum(axis=-1, keepdims=True)
        o = _dot(e.astype(BF16), vv)
        num = o if num is None else num + o
        den = dsum if den is None else den + dsum
    return num / den


def _na_kernel(q_ref, k_ref, v_ref, kc_ref, vc_ref, *rest, rows, kr):
    *bias_refs, o_ref = rest
    kc = kc_ref[0]
    vc = vc_ref[0]
    for j, bias_ref in enumerate(bias_refs):
        r = pl.program_id(1) * len(bias_refs) + j
        row_start = jnp.clip(r - kr // 2, 0, rows - kr)
        start = pl.multiple_of(row_start * GRID_W, GRID_W)
        n_win = kr * GRID_W
        q = q_ref[0, j * GRID_W:(j + 1) * GRID_W, :]
        kw = k_ref[0, pl.ds(start, n_win), :]
        vw = v_ref[0, pl.ds(start, n_win), :]
        o_ref[0, j * GRID_W:(j + 1) * GRID_W, :] = _na_row(q, kw, vw, kc, vc, bias_ref).astype(o_ref.dtype)


def _na_row(q, kw, vw, kc, vc, bias_ref):
    gw = NA_GROUP * NA_HEAD_DIM
    stacked = (NA_GROUP * GRID_W, gw)
    on_head = (lax.broadcasted_iota(jnp.int32, stacked, 0) // GRID_W
               == lax.broadcasted_iota(jnp.int32, stacked, 1) // NA_HEAD_DIM)
    outs = []
    for g in range(NA_HEADS // NA_GROUP):
        sl = slice(g * gw, (g + 1) * gw)
        q_all = jnp.where(on_head, jnp.concatenate([q[:, sl]] * NA_GROUP, axis=0), jnp.zeros((), q.dtype))
        bias = bias_ref[0, g * NA_GROUP * GRID_W:(g + 1) * NA_GROUP * GRID_W, :]
        o_all = _softmax_av(q_all, [kw[:, sl], kc[:, sl]], [vw[:, sl], vc[:, sl]], [bias, None])
        o_all = jnp.where(on_head, o_all, 0.0).reshape(NA_GROUP, GRID_W, gw)
        outs.append(o_all.sum(axis=0))
    return jnp.concatenate(outs, axis=-1)


def _na_bias_table(rpb, rows, kr):
    col = np.arange(GRID_W)
    col_start = np.clip(col - NA_WIN_C // 2, 0, GRID_W - NA_WIN_C)
    col_ok = (col[None, :] >= col_start[:, None]) & (col[None, :] < col_start[:, None] + NA_WIN_C)
    d_col = np.clip(col[None, :] - col[:, None], -(NA_WIN_C - 1), NA_WIN_C - 1) + NA_WIN_C - 1
    n_dr, n_dc = rpb.shape[1], rpb.shape[2]
    onehot = jnp.asarray((d_col.reshape(-1)[None, :] == np.arange(n_dc)[:, None]).astype(np.float32))
    by_col = jnp.dot(rpb.astype(F32).reshape(NA_HEADS * n_dr, n_dc), onehot, precision=lax.Precision.HIGHEST)
    by_col = by_col.reshape(NA_HEADS, n_dr, GRID_W, GRID_W)
    by_col = jnp.where(col_ok[None, None], by_col, NEG_INF)
    tables = []
    for o in range(kr):
        lo = NA_WIN_R - 1 - o
        tables.append(by_col[:, lo:lo + kr].transpose(0, 2, 1, 3).reshape(NA_HEADS, GRID_W, kr * GRID_W))
    return jnp.stack(tables).reshape(kr, NA_HEADS * GRID_W, kr * GRID_W)


def _na_latent(main, main_ctx, rpb):
    b, s, _ = main.shape
    sc = main_ctx.shape[1]
    rows = s // GRID_W
    kr = min(NA_WIN_R, rows)
    bias = _na_bias_table(rpb, rows, kr)
    w = NA_WIDTH

    per_step = next(n for n in (4, 2, 1) if rows % n == 0)

    def bias_spec(j):
        def bias_map(bi, i):
            r = i * per_step + j
            return (r - jnp.clip(r - kr // 2, 0, rows - kr), 0, 0)
        return pl.BlockSpec((1, NA_HEADS * GRID_W, kr * GRID_W), bias_map)

    return pl.pallas_call(
        functools.partial(_na_kernel, rows=rows, kr=kr),
        grid=(b, rows // per_step),
        in_specs=[pl.BlockSpec((1, per_step * GRID_W, w), lambda bi, i: (bi, i, OFF_Q_NA // w)),
                  pl.BlockSpec((1, s, w), lambda bi, i: (bi, 0, OFF_K_NA // w)),
                  pl.BlockSpec((1, s, w), lambda bi, i: (bi, 0, OFF_V_NA // w)),
                  pl.BlockSpec((1, sc, w), lambda bi, i: (bi, 0, OFF_K_NA // w)),
                  pl.BlockSpec((1, sc, w), lambda bi, i: (bi, 0, OFF_V_NA // w))]
                 + [bias_spec(j) for j in range(per_step)],
        out_specs=pl.BlockSpec((1, per_step * GRID_W, w), lambda bi, i: (bi, i, 0)),
        out_shape=jax.ShapeDtypeStruct((b, s, w), BF16),
        compiler_params=_cparams(("parallel", "arbitrary")),
        name="na_latent",
    )(main, main, main, main_ctx, main_ctx, *([bias] * per_step))


def _dense_attn_kernel(q_ref, k_ref, v_ref, o_ref):
    q = q_ref[0]
    k = k_ref[0]
    v = v_ref[0]
    outs = []
    for h in range(NA_HEADS):
        sl = slice(h * NA_HEAD_DIM, (h + 1) * NA_HEAD_DIM)
        outs.append(_softmax_av(q[:, sl], [k[:, sl]], [v[:, sl]], [None]))
    o_ref[0] = jnp.concatenate(outs, axis=-1).astype(o_ref.dtype)


def _dense_attn(main_ctx):
    b, sc, _ = main_ctx.shape
    w = NA_WIDTH
    return pl.pallas_call(
        _dense_attn_kernel,
        grid=(b,),
        in_specs=[pl.BlockSpec((1, sc, w), lambda bi: (bi, 0, OFF_Q_NA // w)),
                  pl.BlockSpec((1, sc, w), lambda bi: (bi, 0, OFF_K_NA // w)),
                  pl.BlockSpec((1, sc, w), lambda bi: (bi, 0, OFF_V_NA // w))],
        out_specs=pl.BlockSpec((1, sc, w), lambda bi: (bi, 0, 0)),
        out_shape=jax.ShapeDtypeStruct((b, sc, w), BF16),
        compiler_params=_cparams(("parallel",)),
        name="ctx_attn",
    )(main_ctx, main_ctx, main_ctx)


def _conv_kernel(b_ref, c_ref, x_ref, w_ref, o_ref):
    u = c_ref[0].astype(F32) * x_ref[0].astype(F32)
    s = u.shape[0]
    t = lax.broadcasted_iota(jnp.int32, u.shape, 0)
    prev = jnp.where(t == 0, 0.0, pltpu.roll(u, 1, axis=0))
    nxt = jnp.where(t == s - 1, 0.0, pltpu.roll(u, s - 1, axis=0))
    w = w_ref[...]
    y = b_ref[0].astype(F32) * (prev * w[0:1] + u * w[1:2] + nxt * w[2:3])
    o_ref[0] = y.astype(o_ref.dtype)


def _short_conv(main, conv_w):
    b, s, _ = main.shape
    nt = SC_WIDTH // LANE
    return pl.pallas_call(
        _conv_kernel,
        grid=(b, nt),
        in_specs=[pl.BlockSpec((1, s, LANE), lambda bi, c: (bi, 0, OFF_B_SC // LANE + c)),
                  pl.BlockSpec((1, s, LANE), lambda bi, c: (bi, 0, OFF_C_SC // LANE + c)),
                  pl.BlockSpec((1, s, LANE), lambda bi, c: (bi, 0, OFF_X_SC // LANE + c)),
                  pl.BlockSpec((3, LANE), lambda bi, c: (0, c))],
        out_specs=pl.BlockSpec((1, s, LANE), lambda bi, c: (bi, 0, c)),
        out_shape=jax.ShapeDtypeStruct((b, s, SC_WIDTH), BF16),
        compiler_params=_cparams(("parallel", "parallel")),
        name="short_conv",
    )(main, main, main, conv_w)


def _gla_matrices(reverse):
    c = GLA_C
    t = np.arange(c)[:, None]
    m = np.arange(c)[None, :]
    blocks = [m <= t, m > t]
    for b in GLA_LEVELS:
        first = (t // (2 * b)) * (2 * b) + b
        is_q = (t & b) != 0
        blocks.append(np.where(is_q, (m > first) & (m <= t), (m > t) & (m <= first)))
    mats = np.stack(blocks).astype(np.float32)
    if reverse:
        mats = mats[:, ::-1, ::-1]
    mats = mats.reshape(-1, c)
    return jnp.asarray(np.concatenate([mats, mats], axis=1), dtype=BF16)


def _split_bf16(x):
    hi = x.astype(BF16)
    return hi, (x - hi.astype(F32)).astype(BF16)


def _pair_block_diag(x):
    lane = lax.broadcasted_iota(jnp.int32, x.shape, 1)
    zero = jnp.zeros((), x.dtype)
    return jnp.concatenate([jnp.where(lane < GLA_DK, x, zero), jnp.where(lane >= GLA_DK, x, zero)], axis=0)


def _gla_kernel(*refs, emit):
    n_in = 9 if emit else 8
    n_out = 2 if emit else 1
    ins, outs, scratch = refs[:2 * n_in], refs[2 * n_in:2 * (n_in + n_out)], refs[2 * (n_in + n_out):]
    for d, reverse in enumerate((False, True)):
        _gla_direction(ins[d * n_in:(d + 1) * n_in], outs[d * n_out:(d + 1) * n_out], scratch[d], reverse, emit)


def _gla_direction(ins, outs, st_ref, reverse, emit):
    if emit:
        q_ref, k_ref, v_ref, gt_ref, w2a_ref, w2b_ref, b2_ref, a_ref, s0_ref = ins
        o_ref, sf_ref = outs
    else:
        k_ref, v_ref, gt_ref, w2a_ref, w2b_ref, b2_ref, a_ref, s0_ref = ins
        (sf_ref,) = outs
    c = GLA_C
    pw = 2 * GLA_DK
    step = pl.program_id(1)

    @pl.when(step == 0)
    def _():
        st_ref[...] = s0_ref[0]

    lr_hi, lr_lo = _split_bf16(gt_ref[0])
    logit = (_dot(jnp.concatenate([lr_hi, lr_lo], axis=1), w2a_ref[...]) + _dot(lr_hi, w2b_ref[...])
             + b2_ref[...])
    g = (jnp.minimum(logit, 0.0) - jnp.log1p(jnp.exp(-jnp.abs(logit)))) * (LOG2_E / GLA_GATE_TAU)
    g_hi, g_lo = _split_bf16(g)
    args = _dot(a_ref[...], jnp.concatenate([g_hi, g_lo], axis=0))
    cum = args[0:c]
    rem = args[c:2 * c]
    last_row = cum[0:1] if reverse else cum[c - 1:c]

    k = k_ref[0].astype(F32)
    v = v_ref[0]
    atts = []
    if emit:
        q = q_ref[0].astype(F32) * (GLA_DK ** -0.5)
        ti = lax.broadcasted_iota(jnp.int32, (c, pw), 0)
        si = lax.broadcasted_iota(jnp.int32, (c, pw), 1) & (GLA_DK - 1)
        if reverse:
            ti, si = c - 1 - ti, c - 1 - si
        for hp in range(GLA_HEADS // 2):
            cs = slice(hp * pw, (hp + 1) * pw)
            qp, kp = q[:, cs], k[:, cs]
            att = jnp.where(ti == si, _dot_nt(qp.astype(BF16), _pair_block_diag(kp.astype(BF16))), 0.0)
            for l, b in enumerate(GLA_LEVELS):
                is_q = (ti & b) != 0
                x = (jnp.exp2(args[(2 + l) * c:(3 + l) * c, cs]) * jnp.where(is_q, qp, kp)).astype(BF16)
                pair = (((ti ^ si) >> (b.bit_length() - 1)) == 1) & is_q
                att = jnp.where(pair, _dot_nt(x, _pair_block_diag(x)), att)
            atts.append(att.astype(BF16))

    outs = []
    for h in range(GLA_HEADS):
        sl = slice(h * GLA_DK, (h + 1) * GLA_DK)
        kh = k[:, sl]
        vh = v[:, h * GLA_DV:(h + 1) * GLA_DV]
        state = st_ref[h]
        if emit:
            qd = (q[:, sl] * jnp.exp2(cum[:, sl])).astype(BF16)
            att = atts[h // 2][:, (h % 2) * GLA_DK:(h % 2 + 1) * GLA_DK]
            outs.append(_dot(qd, state.astype(BF16)) + _dot(att, vh))
        kd = (kh * jnp.exp2(rem[:, sl])).astype(BF16)
        decay = jnp.exp2(jnp.broadcast_to(last_row[:, sl], (GLA_DK, GLA_DK))).T
        decay = jnp.concatenate([decay] * (GLA_DV // GLA_DK), axis=1)
        st_ref[h] = decay * state + _dot_tn(kd, vh)

    if emit:
        o_ref[0] = jnp.concatenate(outs, axis=-1).astype(o_ref.dtype)

    @pl.when(step == pl.num_programs(1) - 1)
    def _():
        sf_ref[0] = st_ref[...]


def _gla_scan(main, gate, gate_ws, s0s, emit):
    b, l, _ = main.shape
    n = l // GLA_C
    const = lambda arr: pl.BlockSpec(arr.shape, lambda bi, s: (0,) * arr.ndim)
    state_spec = pl.BlockSpec((1, GLA_HEADS, GLA_DK, GLA_DV), lambda bi, s: (bi, 0, 0, 0))
    state_shape = jax.ShapeDtypeStruct((b, GLA_HEADS, GLA_DK, GLA_DV), F32)
    in_specs, args, out_specs, out_shape = [], [], [], []
    for reverse in (False, True):
        amat = _gla_matrices(reverse)
        w2a, w2b, b2 = gate_ws[reverse]

        def col(block, reverse=reverse):
            return lambda bi, s: (bi, n - 1 - s if reverse else s, block)

        if emit:
            in_specs.append(pl.BlockSpec((1, GLA_C, GLA_KEY_WIDTH), col(OFF_Q_GLA // GLA_KEY_WIDTH)))
            args.append(main)
            out_specs.append(pl.BlockSpec((1, GLA_C, GLA_VAL_WIDTH), col(0)))
            out_shape.append(jax.ShapeDtypeStruct((b, l, GLA_VAL_WIDTH), BF16))
        in_specs += [pl.BlockSpec((1, GLA_C, GLA_KEY_WIDTH), col(OFF_K_GLA // GLA_KEY_WIDTH)),
                     pl.BlockSpec((1, GLA_C, GLA_VAL_WIDTH), col(OFF_V_GLA // GLA_VAL_WIDTH)),
                     pl.BlockSpec((1, GLA_C, LANE), col(0)),
                     const(w2a), const(w2b), const(b2), const(amat), state_spec]
        args += [main, main, gate, w2a, w2b, b2, amat, s0s[reverse]]
        out_specs.append(state_spec)
        out_shape.append(state_shape)
    res = pl.pallas_call(
        functools.partial(_gla_kernel, emit=emit),
        grid=(b, n),
        in_specs=in_specs,
        out_specs=out_specs,
        out_shape=out_shape,
        scratch_shapes=[pltpu.VMEM((GLA_HEADS, GLA_DK, GLA_DV), F32)] * 2,
        compiler_params=_cparams(("parallel", "arbitrary")),
        name="gla_scan",
    )(*args)
    return (res[0], res[2], res[1], res[3]) if emit else (None, None, res[0], res[1])


def _gla_gate_weights(gate_w, gate_b):
    out = []
    for dr in range(2):
        w = jnp.zeros((LANE, GLA_KEY_WIDTH), F32)
        w = w.at[dr * GLA_GATE_RANK:(dr + 1) * GLA_GATE_RANK].set(gate_w[dr])
        w_hi = w.astype(BF16)
        w_lo = (w - w_hi.astype(F32)).astype(BF16)
        out.append((jnp.concatenate([w_hi, w_hi], axis=0), w_lo, gate_b[dr][None]))
    return out


def _merge_kernel(ona_ref, osc_ref, of_ref, ob_ref, r_ref, gna_ref, gsc_ref, ggl_ref, x_ref, gt_ref,
                  gn_ref, wna_ref, wsc_ref, wgl_ref, wo_ref, g2_ref, sh2_ref, sc2_ref, wr_ref,
                  xo_ref, h2_ref, hp_ref, lg_ref):
    o = of_ref[0].astype(F32) + ob_ref[0].astype(F32)
    normed = []
    for h in range(GLA_HEADS):
        oh = o[:, h * GLA_DV:(h + 1) * GLA_DV]
        ms = jnp.mean(oh * oh, axis=-1, keepdims=True)
        normed.append(oh * lax.rsqrt(ms + RMS_EPS))
    r = r_ref[0].astype(F32)
    y_gla = jnp.concatenate(normed, axis=-1) * gn_ref[...] * (r * _sigmoid(r))
    y = (_sigmoid(gna_ref[0].astype(F32)) * _dot(ona_ref[0], wna_ref[...])
         + _sigmoid(gsc_ref[0].astype(F32)) * _dot(osc_ref[0], wsc_ref[...])
         + _sigmoid(ggl_ref[0].astype(F32)) * _dot(y_gla.astype(BF16), wgl_ref[...]))
    xn = x_ref[0] + gt_ref[0] * _dot(y.astype(BF16), wo_ref[...])
    xo_ref[0] = xn
    ms = jnp.mean(xn * xn, axis=-1, keepdims=True)
    h2 = xn * lax.rsqrt(ms + RMS_EPS) * g2_ref[...] * (1.0 + sc2_ref[0]) + sh2_ref[0]
    h2b = h2.astype(BF16)
    h2_ref[0] = h2b
    _store_parts(hp_ref, _pack_rows(h2))
    lg_ref[0] = _dot(h2b, wr_ref[...])


def _merge(o_na, o_sc, o_f, o_b, main, x, gt1, gn, w_na, w_sc, w_gla, w_out, g2, sh2, sc2, w_router, tm):
    b, s, d = x.shape
    per_batch = gt1.shape[0] == b
    mod_map = (lambda bi, i: (bi, 0, 0)) if per_batch else (lambda bi, i: (0, 0, 0))
    tok = lambda width, blk: pl.BlockSpec((1, tm, width), lambda bi, i: (bi, i, blk))
    full = lambda arr: pl.BlockSpec(arr.shape, lambda bi, i: (0,) * arr.ndim)
    mod = pl.BlockSpec((1, 1, d), mod_map)
    gn_t = jnp.tile(gn, GLA_HEADS)[None]
    g2_t = g2[None]
    return pl.pallas_call(
        _merge_kernel,
        grid=(b, s // tm),
        in_specs=[tok(NA_WIDTH, 0), tok(SC_WIDTH, 0), tok(GLA_VAL_WIDTH, 0), tok(GLA_VAL_WIDTH, 0),
                  tok(d, OFF_R_GLA // d), tok(d, OFF_MERGE // d), tok(d, OFF_MERGE // d + 1),
                  tok(d, OFF_MERGE // d + 2), tok(d, 0), mod,
                  full(gn_t), full(w_na), full(w_sc), full(w_gla), full(w_out), full(g2_t), mod, mod,
                  full(w_router)],
        out_specs=[tok(d, 0), tok(d, 0),
                   pl.BlockSpec((SC_PARTS, 1, tm, SC_ROW), lambda bi, i: (0, bi, i, 0)), tok(LANE, 0)],
        out_shape=[jax.ShapeDtypeStruct((b, s, d), F32),
                   jax.ShapeDtypeStruct((b, s, d), BF16),
                   jax.ShapeDtypeStruct((SC_PARTS, b, s, SC_ROW), U32),
                   jax.ShapeDtypeStruct((b, s, LANE), F32)],
        compiler_params=_cparams(("parallel", "parallel")),
        name="merge",
    )(o_na, o_sc, o_f, o_b, main, main, main, main, x, gt1, gn_t, w_na, w_sc, w_gla, w_out, g2_t,
      sh2, sc2, w_router)


def _router_kernel(lg_ref, br_ref, tri_ref, eidx_ref, gw_ref, rank_ref, cnt_ref, carry_ref):
    tm = lg_ref.shape[0]

    @pl.when(pl.program_id(0) == 0)
    def _():
        carry_ref[...] = jnp.zeros_like(carry_ref)

    scores = _sigmoid(lg_ref[...].T[:N_EXPERTS])
    sel = scores + br_ref[...]
    neg = -jnp.inf

    sel3 = sel.reshape(N_EXPERT_GROUPS, GROUP_SIZE, tm)
    i3 = lax.broadcasted_iota(jnp.int32, sel3.shape, 1)
    m1 = sel3.max(axis=1, keepdims=True)
    first = jnp.where(sel3 == m1, i3, GROUP_SIZE).min(axis=1, keepdims=True)
    m2 = jnp.where(i3 == first, neg, sel3).max(axis=1, keepdims=True)
    gscore = (m1 + m2)[:, 0, :]

    gi = lax.broadcasted_iota(jnp.int32, gscore.shape, 0)
    gmask = jnp.zeros(gscore.shape, jnp.bool_)
    for _ in range(TOPK_GROUPS):
        m = gscore.max(axis=0, keepdims=True)
        pick = gi == jnp.where(gscore == m, gi, N_EXPERT_GROUPS).min(axis=0, keepdims=True)
        gmask = gmask | pick
        gscore = jnp.where(pick, neg, gscore)
    emask = jnp.broadcast_to(gmask[:, None, :], sel3.shape).reshape(N_EXPERTS, tm)
    sel = jnp.where(emask, sel, neg)

    ei = lax.broadcasted_iota(jnp.int32, sel.shape, 0)
    picks, idxs, ws = [], [], []
    for _ in range(TOP_K):
        m = sel.max(axis=0, keepdims=True)
        idx = jnp.where(sel == m, ei, N_EXPERTS).min(axis=0, keepdims=True)
        pick = ei == idx
        picks.append(pick)
        idxs.append(idx)
        ws.append(jnp.where(pick, scores, 0.0).sum(axis=0, keepdims=True))
        sel = jnp.where(pick, neg, sel)
    w = jnp.concatenate(ws, axis=0)
    gw_ref[...] = w / w.sum(axis=0, keepdims=True) * ROUTED_SCALE
    eidx_ref[...] = jnp.concatenate(idxs, axis=0)

    onehot = picks[0]
    for p in picks[1:]:
        onehot = onehot | p
    onehot = jnp.where(onehot, 1.0, 0.0).astype(BF16)
    before = _dot(onehot, tri_ref[...]) + jnp.tile(carry_ref[...], (1, tm // LANE))
    rank_ref[...] = jnp.concatenate(
        [jnp.where(p, before, 0.0).sum(axis=0, keepdims=True) for p in picks], axis=0).astype(jnp.int32)
    carry_ref[...] += _dot(onehot, jnp.ones((tm, LANE), BF16))
    cnt_ref[...] = carry_ref[...]


def _route(logits, b_router, tm=512):
    t = logits.shape[0]
    br = jnp.broadcast_to(b_router.astype(F32)[:, None], (N_EXPERTS, tm))
    tri = jnp.asarray(np.triu(np.ones((tm, tm), np.float32), 1), dtype=BF16)
    kt = lambda dt: jax.ShapeDtypeStruct((TOP_K, t), dt)
    eidx, gw, rank, cnt = pl.pallas_call(
        _router_kernel,
        grid=(t // tm,),
        in_specs=[pl.BlockSpec((tm, LANE), lambda i: (i, 0)),
                  pl.BlockSpec((N_EXPERTS, tm), lambda i: (0, 0)),
                  pl.BlockSpec((tm, tm), lambda i: (0, 0))],
        out_specs=[pl.BlockSpec((TOP_K, tm), lambda i: (0, i)),
                   pl.BlockSpec((TOP_K, tm), lambda i: (0, i)),
                   pl.BlockSpec((TOP_K, tm), lambda i: (0, i)),
                   pl.BlockSpec((N_EXPERTS, LANE), lambda i: (0, 0))],
        out_shape=[kt(jnp.int32), kt(F32), kt(jnp.int32),
                   jax.ShapeDtypeStruct((N_EXPERTS, LANE), F32)],
        scratch_shapes=[pltpu.VMEM((N_EXPERTS, LANE), F32)],
        compiler_params=_cparams(("arbitrary",)),
        name="router",
    )(logits, br, tri)
    return eidx, gw, rank, cnt[:, 0].astype(jnp.int32)


def _sc_mesh():
    return plsc.VectorSubcoreMesh(core_axis_name="core", subcore_axis_name="subcore")


def _dispatch_rows(xp, dest, slots):
    parts, t, _ = xp.shape
    nwin = parts * t // SC_WINDOW
    idx = dest.reshape(TOP_K, t // SC_WINDOW, SC_WINDOW).transpose(1, 0, 2)
    idx = jnp.concatenate([idx + part * slots for part in range(parts)], axis=0)

    @pl.kernel(out_type=jax.ShapeDtypeStruct((parts * slots, SC_ROW), xp.dtype), mesh=_sc_mesh(),
               scratch_types=[pltpu.SemaphoreType.DMA((TOP_K,))], name="moe_dispatch")
    def run(x_hbm, i_hbm, o_hbm, sems):
        def body(x_vmem, i_vmem):
            copies = [pltpu.make_async_copy(x_vmem, o_hbm.at[i_vmem.at[0, k]], sems.at[k]) for k in range(TOP_K)]
            for cp in copies:
                cp.start()
            for cp in copies:
                cp.wait()

        pltpu.emit_pipeline(
            body,
            grid=(nwin,),
            in_specs=[pl.BlockSpec((SC_WINDOW, SC_ROW), lambda i: (i, 0)),
                      pl.BlockSpec((1, TOP_K, SC_WINDOW), lambda i: (i, 0, 0))],
            out_specs=[],
            core_axis_name=("core", "subcore"),
            dimension_semantics=(pltpu.PARALLEL,),
        )(x_hbm, i_hbm)

    return run(xp.reshape(parts * t, SC_ROW), idx).reshape(parts, slots, SC_ROW)


def _gather_rows(yp, idx):
    n = idx.shape[0]
    parts, slots, _ = yp.shape
    idx = jnp.concatenate([idx + part * slots for part in range(parts)]).reshape(1, n * parts)

    @pl.kernel(out_type=jax.ShapeDtypeStruct((n * parts, SC_ROW), yp.dtype), mesh=_sc_mesh(),
               scratch_types=[], name="moe_gather")
    def run(y_hbm, i_hbm, o_hbm):
        def body(i_vmem, o_vmem):
            pltpu.sync_copy(y_hbm.at[i_vmem.at[0]], o_vmem)

        pltpu.emit_pipeline(
            body,
            grid=(n * parts // SC_WINDOW,),
            in_specs=[pl.BlockSpec((1, SC_WINDOW), lambda i: (0, i))],
            out_specs=[pl.BlockSpec((SC_WINDOW, SC_ROW), lambda i: (i, 0))],
            core_axis_name=("core", "subcore"),
            dimension_semantics=(pltpu.PARALLEL,),
        )(i_hbm, o_hbm)

    return run(yp.reshape(parts * slots, SC_ROW), idx).reshape(parts, n, SC_ROW)


def _expert_kernel(be_ref, bv_ref, bs_ref, x_ref, wg_ref, wu_ref, wd_ref, o_ref, wg_s, wu_s, wd_s):
    i = pl.program_id(0)
    valid = bv_ref[i]
    new_expert = (i == 0) | (be_ref[i] != be_ref[jnp.maximum(i - 1, 0)])

    @pl.when(new_expert)
    def _():
        wg_s[...] = wg_ref[0, 0].astype(BF16)
        wu_s[...] = wu_ref[0, 0].astype(BF16)
        wd_s[...] = wd_ref[0, 0].astype(BF16)

    @pl.when(valid > 0)
    def _():
        w = _load_parts(x_ref)
        row = lax.broadcasted_iota(jnp.int32, w.shape, 0)
        w = jnp.where(row < valid, w, jnp.uint32(0))
        lo, hi = _unpack_rows(w)
        x = jnp.concatenate([lo, hi], axis=1).astype(BF16)
        a = _dot(x, wg_s[...])
        hid = a * _sigmoid(a) * _dot(x, wu_s[...])
        _store_parts(o_ref, _pack_rows(_dot(hid.astype(BF16), wd_s[...])))


def _experts(xs, blk_e, blk_valid, blk_src, layer, w_gate, w_up, w_down):
    parts, slots, _ = xs.shape
    d = D_MODEL
    nb = slots // MOE_BLOCK
    return pl.pallas_call(
        _expert_kernel,
        grid_spec=pltpu.PrefetchScalarGridSpec(
            num_scalar_prefetch=3,
            grid=(nb,),
            in_specs=[pl.BlockSpec((parts, MOE_BLOCK, SC_ROW), lambda i, be, bv, bs: (0, bs[i], 0)),
                      pl.BlockSpec((1, 1, d, EXPERT_FF), lambda i, be, bv, bs: (layer, be[i], 0, 0)),
                      pl.BlockSpec((1, 1, d, EXPERT_FF), lambda i, be, bv, bs: (layer, be[i], 0, 0)),
                      pl.BlockSpec((1, 1, EXPERT_FF, d), lambda i, be, bv, bs: (layer, be[i], 0, 0))],
            out_specs=pl.BlockSpec((parts, MOE_BLOCK, SC_ROW), lambda i, be, bv, bs: (0, bs[i], 0)),
            scratch_shapes=[pltpu.VMEM((d, EXPERT_FF), BF16), pltpu.VMEM((d, EXPERT_FF), BF16),
                            pltpu.VMEM((EXPERT_FF, d), BF16)]),
        out_shape=jax.ShapeDtypeStruct((parts, slots, SC_ROW), U32),
        compiler_params=_cparams(("arbitrary",)),
        name="experts",
    )(blk_e, blk_valid, blk_src, xs, w_gate, w_up, w_down)


def _combine_kernel(yg_ref, gw_ref, h_ref, x_ref, gt_ref, wsg_ref, wsu_ref, wsd_ref, gf_ref, o_ref, *, final):
    h = h_ref[0]
    a = _dot(h, wsg_ref[...])
    hid = a * _sigmoid(a) * _dot(h, wsu_ref[...])
    y = _dot(hid.astype(BF16), wsd_ref[...])
    gw = gw_ref[...]
    y_lo = y[:, :D_MODEL // 2]
    y_hi = y[:, D_MODEL // 2:]
    for k in range(TOP_K):
        lo, hi = _unpack_rows(_load_parts(yg_ref, k))
        y_lo = y_lo + gw[:, k:k + 1] * lo
        y_hi = y_hi + gw[:, k:k + 1] * hi
    y = jnp.concatenate([y_lo, y_hi], axis=1)
    xn = x_ref[0] + gt_ref[0] * y
    if final:
        ms = jnp.mean(xn * xn, axis=-1, keepdims=True)
        xn = xn * lax.rsqrt(ms + RMS_EPS) * gf_ref[...]
    o_ref[0] = xn


def _combine(yg, gw, tok_off, h2, x, gt2, b0, nb, ws_gate, ws_up, ws_down, g_final, final, tm):
    b, s, d = x.shape
    per_batch = gt2.shape[0] == b
    mod_map = (lambda bi, i: (b0 + bi, 0, 0)) if per_batch else (lambda bi, i: (0, 0, 0))
    full = lambda arr: pl.BlockSpec(arr.shape, lambda bi, i: (0,) * arr.ndim)
    tok = lambda width: pl.BlockSpec((1, tm, width), lambda bi, i: (b0 + bi, i, 0))
    gf = g_final[None]
    nblk = s // tm
    blk0 = (tok_off + b0 * s) // tm
    return pl.pallas_call(
        functools.partial(_combine_kernel, final=final),
        grid=(nb, nblk),
        in_specs=[pl.BlockSpec((SC_PARTS, TOP_K, tm, SC_ROW), lambda bi, i: (0, 0, bi * nblk + i, 0)),
                  pl.BlockSpec((tm, TOP_K), lambda bi, i: (blk0 + bi * nblk + i, 0)),
                  tok(d), tok(d),
                  pl.BlockSpec((1, 1, d), mod_map),
                  full(ws_gate), full(ws_up), full(ws_down), full(gf)],
        out_specs=tok(d),
        out_shape=jax.ShapeDtypeStruct((b, s, d), F32),
        input_output_aliases={3: 0},
        compiler_params=_cparams(("parallel", "parallel")),
        name="combine",
    )(yg, gw, h2, x, gt2, ws_gate, ws_up, ws_down, gf)


def _layer(x, ctx_s, mods, mods_ctx, p, ctx_out, final, g_final):
    b, s, d = x.shape
    sc = ctx_s.shape[1]
    sh1, sc1, gt1, sh2, sc2, gt2 = mods
    csh1, csc1, cgt1, csh2, csc2, cgt2 = mods_ctx

    w_main, w_gate = _prep_w_in(p['w_in'], p['layer'])
    main, gate = _proj_in(x, p['g_norm1'], sh1, sc1, w_main, w_gate, tm=min(2048, s), tn=1024)
    ctx_flat = ctx_s.reshape(1, b * sc, d)
    n_ctx, tn_ctx = (N_MAIN, 1024) if ctx_out else (N_KV_MAIN, N_KV_MAIN // 2)
    main_c, gate_c = _proj_in(ctx_flat, p['g_norm1'], csh1, csc1, w_main, w_gate, tm=min(1024, b * sc),
                              tn=tn_ctx, n=n_ctx)
    main_c = main_c.reshape(b, sc, n_ctx)
    gate_c = gate_c.reshape(b, sc, LANE)

    o_na = _na_latent(main, main_c, p['na_rpb'])
    o_sc = _short_conv(main, p['conv_w'])

    gate_ws = _gla_gate_weights(p['gla_gate_w'], p['gla_gate_b'])
    s0 = jnp.zeros((b, GLA_HEADS, GLA_DK, GLA_DV), F32)
    o_cf, o_cb, st_f, st_b = _gla_scan(main_c, gate_c, gate_ws, (s0, s0), ctx_out)
    o_f, o_b, _, _ = _gla_scan(main, gate, gate_ws, (st_f, st_b), True)

    w_na = p['w_branch_na'].astype(BF16)
    w_sc = p['w_branch_sc'].astype(BF16)
    w_gla = p['w_branch_gla'].astype(BF16)
    w_out = p['w_out'].astype(BF16)
    w_router = jnp.pad(p['w_router'], ((0, 0), (0, LANE - N_EXPERTS))).astype(BF16)
    x, h2, h2p, logits = _merge(o_na, o_sc, o_f, o_b, main, x, gt1, p['gla_norm_g'], w_na, w_sc, w_gla, w_out,
                                p['g_norm2'], sh2, sc2, w_router, tm=min(512, s))
    n_lat = b * s
    hp_all = h2p.reshape(SC_PARTS, n_lat, SC_ROW)
    lg_all = logits.reshape(n_lat, LANE)
    if ctx_out:
        o_na_c = _dense_attn(main_c)
        o_sc_c = _short_conv(main_c, p['conv_w'])
        ctx_s, h2_c, h2p_c, lg_c = _merge(o_na_c, o_sc_c, o_cf, o_cb, main_c, ctx_s, cgt1, p['gla_norm_g'],
                                          w_na, w_sc, w_gla, w_out, p['g_norm2'], csh2, csc2, w_router,
                                          tm=min(256, sc))
        hp_all = jnp.concatenate([hp_all, h2p_c.reshape(SC_PARTS, b * sc, SC_ROW)], axis=1)
        lg_all = jnp.concatenate([lg_all, lg_c.reshape(b * sc, LANE)], axis=0)

    t = hp_all.shape[1]
    eidx, gw, rank, counts = _route(lg_all, p['b_router'])
    padded = (counts + MOE_BLOCK - 1) // MOE_BLOCK * MOE_BLOCK
    pad_end = jnp.cumsum(padded)
    pad_start = pad_end - padded
    onehot = eidx[:, :, None] == jnp.arange(N_EXPERTS, dtype=jnp.int32)
    dest = jnp.sum(jnp.where(onehot, pad_start, 0), axis=-1) + rank
    n_blocks = -(-(t * TOP_K + N_EXPERTS * (MOE_BLOCK - 1)) // MOE_BLOCK)
    slots = n_blocks * MOE_BLOCK
    blk_start = jnp.arange(n_blocks, dtype=jnp.int32) * MOE_BLOCK
    blk_e = jnp.minimum(jnp.sum(pad_end[None, :] <= blk_start[:, None], axis=1), N_EXPERTS - 1).astype(jnp.int32)
    used_end = (pad_start + counts)[blk_e]
    blk_valid = jnp.clip(used_end - blk_start, 0, MOE_BLOCK).astype(jnp.int32)
    n_used = pad_end[-1] // MOE_BLOCK
    blk_src = jnp.minimum(jnp.arange(n_blocks, dtype=jnp.int32), n_used - 1)
    blk_e = blk_e[blk_src]

    xs = _dispatch_rows(hp_all, dest, slots)
    ys = _experts(xs, blk_e, blk_valid, blk_src, p['layer'], p['w_exp_gate'], p['w_exp_up'], p['w_exp_down'])
    gw_t = gw.T
    ws_gate = p['w_sh_gate'].astype(BF16)
    ws_up = p['w_sh_up'].astype(BF16)
    ws_down = p['w_sh_down'].astype(BF16)

    def gathered(t0, n):
        return _gather_rows(ys, dest[:, t0:t0 + n].reshape(-1)).reshape(SC_PARTS, TOP_K, n, SC_ROW)

    pieces = next(n for n in (4, 2, 1) if b % n == 0)
    nb = b // pieces
    for q in range(pieces):
        x = _combine(gathered(q * nb * s, nb * s), gw_t, 0, h2, x, gt2, q * nb, nb, ws_gate, ws_up, ws_down,
                     g_final, final, tm=min(256, s))
    if ctx_out:
        ctx_s = _combine(gathered(n_lat, b * sc), gw_t, n_lat, h2_c, ctx_s, cgt2, 0, b, ws_gate, ws_up, ws_down,
                         g_final, False, tm=min(256, sc))
    return x, ctx_s


def kernel(x, c, ctx, c_ctx, w_mod, b_mod, g_norm1, g_norm2, w_in, na_rpb, w_branch_na, conv_w, w_branch_sc,
           gla_gate_w, gla_gate_b, gla_norm_g, w_branch_gla, w_out, w_router, b_router, w_exp_gate, w_exp_up,
           w_exp_down, w_sh_gate, w_sh_up, w_sh_down, g_final):
    stacked = dict(g_norm1=g_norm1, g_norm2=g_norm2, na_rpb=na_rpb, w_branch_na=w_branch_na,
                   conv_w=conv_w, w_branch_sc=w_branch_sc, gla_gate_w=gla_gate_w, gla_gate_b=gla_gate_b,
                   gla_norm_g=gla_norm_g, w_branch_gla=w_branch_gla, w_out=w_out, w_router=w_router,
                   b_router=b_router,
                   w_sh_gate=w_sh_gate, w_sh_up=w_sh_up, w_sh_down=w_sh_down)
    depth = w_in.shape[0]
    ctx_s = ctx
    for i in range(depth):
        p = {name: arr[i] for name, arr in stacked.items()}
        p.update(layer=i, w_in=w_in, w_exp_gate=w_exp_gate, w_exp_up=w_exp_up, w_exp_down=w_exp_down)
        mods, mods_ctx = _mod_vectors(c, c_ctx, w_mod, b_mod, i)
        last = i == depth - 1
        x, ctx_s = _layer(x, ctx_s, mods, mods_ctx, p, not last, last, g_final)
    return x
```

```python
import functools

import numpy as np
import jax
import jax.numpy as jnp
from jax import lax
from jax.experimental import pallas as pl
from jax.experimental.pallas import tpu as pltpu
from jax.experimental.pallas import tpu_sc as plsc

F32 = jnp.float32
BF16 = jnp.bfloat16
U32 = jnp.uint32

D_MODEL = 1024
N_MOD = 6
RMS_EPS = 1e-6
NEG_INF = -1e30
GRID_W = 64
NA_HEADS = 8
NA_HEAD_DIM = 64
NA_WIDTH = NA_HEADS * NA_HEAD_DIM
NA_WIN_R = 8
NA_WIN_C = 16
NA_GROUP = 4
SC_WIDTH = 512
GLA_HEADS = 4
GLA_KEY_WIDTH = 512
GLA_VAL_WIDTH = 1024
GLA_DK = GLA_KEY_WIDTH // GLA_HEADS
GLA_DV = GLA_VAL_WIDTH // GLA_HEADS
GLA_GATE_RANK = 16
GLA_GATE_TAU = 16.0
LOG2_E = 1.4426950408889634
N_EXPERTS = 64
N_EXPERT_GROUPS = 8
GROUP_SIZE = N_EXPERTS // N_EXPERT_GROUPS
TOPK_GROUPS = 4
TOP_K = 8
EXPERT_FF = 256
ROUTED_SCALE = 2.5
MOE_BLOCK = 1024

LANE = 128
GLA_C = 128
GLA_LEVELS = tuple(GLA_C >> (i + 1) for i in range(GLA_C.bit_length() - 1))
VMEM_LIMIT = 48 * 1024 * 1024
SC_WINDOW = 128
SC_ROW = 256
SC_PARTS = D_MODEL // 2 // SC_ROW

OFF_V_GLA = 0
OFF_K_NA = 1024
OFF_V_NA = 1536
OFF_K_GLA = 2048
N_KV_MAIN = 2560
OFF_Q_NA = 2560
OFF_B_SC = 3072
OFF_C_SC = 3584
OFF_X_SC = 4096
OFF_Q_GLA = 4608
OFF_R_GLA = 5120
OFF_MERGE = 6144
N_MAIN = 9216


def _cparams(sem, vmem=VMEM_LIMIT):
    return pltpu.CompilerParams(dimension_semantics=sem, vmem_limit_bytes=vmem)


def _dot(a, b):
    return jnp.dot(a, b, preferred_element_type=F32)


def _dot_nt(a, b):
    return lax.dot_general(a, b, (((1,), (1,)), ((), ())), preferred_element_type=F32)


def _dot_tn(a, b):
    return lax.dot_general(a, b, (((0,), (0,)), ((), ())), preferred_element_type=F32)


def _sigmoid(x):
    return 0.5 * jnp.tanh(0.5 * x) + 0.5


def _pack_rows(x):
    n = x.shape[1] // 2
    r = x.astype(BF16).astype(F32)
    lo = pltpu.bitcast(r[:, :n], U32) >> 16
    hi = pltpu.bitcast(r[:, n:], U32)
    return hi | lo


def _store_parts(ref, words):
    for part in range(SC_PARTS):
        dst = ref.at[part, 0] if len(ref.shape) == 4 else ref.at[part]
        dst[...] = words[:, part * SC_ROW:(part + 1) * SC_ROW]


def _load_parts(ref, *lead):
    return jnp.concatenate([ref[(part,) + lead] for part in range(SC_PARTS)], axis=-1)


def _unpack_rows(w):
    lo = pltpu.bitcast(w << 16, F32)
    hi = pltpu.bitcast(w & jnp.uint32(0xFFFF0000), F32)
    return lo, hi


def _mod_kernel(a_ref, w_ref, b_ref, o_ref):
    a = a_ref[...]
    a = a * _sigmoid(a)
    o_ref[...] = _dot(a.astype(BF16), w_ref[0].astype(BF16)) + b_ref[0]


def _mod_vectors(c, c_ctx, w_mod, b_mod, layer):
    b = c.shape[0]
    rows = -(-(b + 1) // 8) * 8
    a = jnp.concatenate([c, c_ctx[None], jnp.zeros((rows - b - 1, D_MODEL), F32)], axis=0)
    n = N_MOD * D_MODEL
    tn = 1536
    out = pl.pallas_call(
        _mod_kernel,
        grid=(n // tn,),
        in_specs=[pl.BlockSpec((rows, D_MODEL), lambda j: (0, 0)),
                  pl.BlockSpec((1, D_MODEL, tn), lambda j: (layer, 0, j)),
                  pl.BlockSpec((1, 1, tn), lambda j: (layer, 0, j))],
        out_specs=pl.BlockSpec((rows, tn), lambda j: (0, j)),
        out_shape=jax.ShapeDtypeStruct((rows, n), F32),
        compiler_params=_cparams(("parallel",)),
        name="mod_vectors",
    )(a, w_mod, b_mod[:, None])
    lat = out[:b].reshape(b, N_MOD, 1, D_MODEL)
    ctx = out[b].reshape(N_MOD, 1, 1, D_MODEL)
    return [lat[:, i] for i in range(N_MOD)], [ctx[i] for i in range(N_MOD)]


def _proj_kernel(x_ref, g_ref, sh_ref, sc_ref, w_ref, wg_ref, o_ref, og_ref, h_ref):
    @pl.when(pl.program_id(2) == 0)
    def _():
        x = x_ref[0]
        ms = jnp.mean(x * x, axis=-1, keepdims=True)
        h = x * lax.rsqrt(ms + RMS_EPS) * g_ref[...] * (1.0 + sc_ref[0]) + sh_ref[0]
        hb = h.astype(BF16)
        h_ref[...] = hb
        og_ref[0] = _dot(hb, wg_ref[...])

    o_ref[0] = _dot(h_ref[...], w_ref[...]).astype(o_ref.dtype)


W_IN_TILE = 512
W_IN_GATE_SHIFT = 2 * GLA_GATE_RANK


def _prep_w_in_kernel(a_ref, b_ref, o_ref, g_ref):
    t = pl.program_id(0)
    first_lat = N_KV_MAIN // W_IN_TILE
    a = a_ref[0]

    @pl.when(t < first_lat)
    def _():
        o_ref[...] = a.T.astype(BF16)

    @pl.when(t >= first_lat)
    def _():
        moved = jnp.concatenate([a[W_IN_GATE_SHIFT:], b_ref[0]], axis=0)
        scale = jnp.where(t == first_lat, NA_HEAD_DIM ** -0.5, 1.0)
        o_ref[...] = (moved * scale).T.astype(BF16)

    @pl.when(t == first_lat)
    def _():
        head = a[:LANE]
        row = lax.broadcasted_iota(jnp.int32, head.shape, 0)
        g_ref[...] = jnp.where(row < W_IN_GATE_SHIFT, head, 0.0).T.astype(BF16)


def _prep_w_in(w_in, layer):
    d = w_in.shape[1]
    w_t = jnp.swapaxes(w_in, 1, 2)
    first_lat = N_KV_MAIN // W_IN_TILE
    kv_perm = OFF_K_NA // W_IN_TILE

    def a_map(t):
        return (layer, jnp.where(t < first_lat, (t + first_lat - kv_perm) % first_lat, t), 0)

    def b_map(t):
        return (layer, jnp.where(t < first_lat, 0, (t + 1) * (W_IN_TILE // W_IN_GATE_SHIFT)), 0)

    return pl.pallas_call(
        _prep_w_in_kernel,
        grid=(N_MAIN // W_IN_TILE,),
        in_specs=[pl.BlockSpec((1, W_IN_TILE, d), a_map),
                  pl.BlockSpec((1, W_IN_GATE_SHIFT, d), b_map)],
        out_specs=[pl.BlockSpec((d, W_IN_TILE), lambda t: (0, t)),
                   pl.BlockSpec((d, LANE), lambda t: (0, 0))],
        out_shape=[jax.ShapeDtypeStruct((d, N_MAIN), BF16), jax.ShapeDtypeStruct((d, LANE), BF16)],
        compiler_params=_cparams(("arbitrary",)),
        name="prep_w_in",
    )(w_t, w_t)


def _proj_in(x, g, shift, scale, w_main, w_gate, tm, tn, n=None):
    b, s, d = x.shape
    n = w_main.shape[1] if n is None else n
    per_batch = shift.shape[0] == b
    mod_map = (lambda bi, i, j: (bi, 0, 0)) if per_batch else (lambda bi, i, j: (0, 0, 0))
    return pl.pallas_call(
        _proj_kernel,
        grid=(b, s // tm, n // tn),
        in_specs=[pl.BlockSpec((1, tm, d), lambda bi, i, j: (bi, i, 0)),
                  pl.BlockSpec((1, d), lambda bi, i, j: (0, 0)),
                  pl.BlockSpec((1, 1, d), mod_map),
                  pl.BlockSpec((1, 1, d), mod_map),
                  pl.BlockSpec((d, tn), lambda bi, i, j: (0, j)),
                  pl.BlockSpec((d, LANE), lambda bi, i, j: (0, 0))],
        out_specs=[pl.BlockSpec((1, tm, tn), lambda bi, i, j: (bi, i, j)),
                   pl.BlockSpec((1, tm, LANE), lambda bi, i, j: (bi, i, 0))],
        out_shape=[jax.ShapeDtypeStruct((b, s, n), BF16),
                   jax.ShapeDtypeStruct((b, s, LANE), F32)],
        scratch_shapes=[pltpu.VMEM((tm, d), BF16)],
        compiler_params=_cparams(("parallel", "parallel", "arbitrary")),
        name="proj_in",
    )(x, g[None], shift, scale, w_main, w_gate)


def _softmax_av(q, keys, vals, biases):
    scores = []
    for kk, bb in zip(keys, biases):
        s = _dot_nt(q, kk)
        scores.append(s if bb is None else s + bb)
    m = scores[0].max(axis=-1, keepdims=True)
    for s in scores[1:]:
        m = jnp.maximum(m, s.max(axis=-1, keepdims=True))
    num = None
    den = None
    for s, vv in zip(scores, vals):
        e = jnp.exp(s - m)
        dsum = e.sum(axis=-1, keepdims=True)
        o = _dot(e.astype(BF16), vv)
        num = o if num is None else num + o
        den = dsum if den is None else den + dsum
    return num / den


def _na_kernel(q_ref, k_ref, v_ref, kc_ref, vc_ref, *rest, rows, kr):
    *bias_refs, o_ref = rest
    kc = kc_ref[0]
    vc = vc_ref[0]
    for j, bias_ref in enumerate(bias_refs):
        r = pl.program_id(1) * len(bias_refs) + j
        row_start = jnp.clip(r - kr // 2, 0, rows - kr)
        start = pl.multiple_of(row_start * GRID_W, GRID_W)
        n_win = kr * GRID_W
        q = q_ref[0, j * GRID_W:(j + 1) * GRID_W, :]
        kw = k_ref[0, pl.ds(start, n_win), :]
        vw = v_ref[0, pl.ds(start, n_win), :]
        o_ref[0, j * GRID_W:(j + 1) * GRID_W, :] = _na_row(q, kw, vw, kc, vc, bias_ref).astype(o_ref.dtype)


def _na_row(q, kw, vw, kc, vc, bias_ref):
    gw = NA_GROUP * NA_HEAD_DIM
    stacked = (NA_GROUP * GRID_W, gw)
    on_head = (lax.broadcasted_iota(jnp.int32, stacked, 0) // GRID_W
               == lax.broadcasted_iota(jnp.int32, stacked, 1) // NA_HEAD_DIM)
    outs = []
    for g in range(NA_HEADS // NA_GROUP):
        sl = slice(g * gw, (g + 1) * gw)
        q_all = jnp.where(on_head, jnp.concatenate([q[:, sl]] * NA_GROUP, axis=0), jnp.zeros((), q.dtype))
        bias = bias_ref[0, g * NA_GROUP * GRID_W:(g + 1) * NA_GROUP * GRID_W, :]
        o_all = _softmax_av(q_all, [kw[:, sl], kc[:, sl]], [vw[:, sl], vc[:, sl]], [bias, None])
        o_all = jnp.where(on_head, o_all, 0.0).reshape(NA_GROUP, GRID_W, gw)
        outs.append(o_all.sum(axis=0))
    return jnp.concatenate(outs, axis=-1)


def _na_bias_table(rpb, rows, kr):
    col = np.arange(GRID_W)
    col_start = np.clip(col - NA_WIN_C // 2, 0, GRID_W - NA_WIN_C)
    col_ok = (col[None, :] >= col_start[:, None]) & (col[None, :] < col_start[:, None] + NA_WIN_C)
    d_col = np.clip(col[None, :] - col[:, None], -(NA_WIN_C - 1), NA_WIN_C - 1) + NA_WIN_C - 1
    n_dr, n_dc = rpb.shape[1], rpb.shape[2]
    onehot = jnp.asarray((d_col.reshape(-1)[None, :] == np.arange(n_dc)[:, None]).astype(np.float32))
    by_col = jnp.dot(rpb.astype(F32).reshape(NA_HEADS * n_dr, n_dc), onehot, precision=lax.Precision.HIGHEST)
    by_col = by_col.reshape(NA_HEADS, n_dr, GRID_W, GRID_W)
    by_col = jnp.where(col_ok[None, None], by_col, NEG_INF)
    tables = []
    for o in range(kr):
        lo = NA_WIN_R - 1 - o
        tables.append(by_col[:, lo:lo + kr].transpose(0, 2, 1, 3).reshape(NA_HEADS, GRID_W, kr * GRID_W))
    return jnp.stack(tables).reshape(kr, NA_HEADS * GRID_W, kr * GRID_W)


def _na_latent(main, main_ctx, rpb):
    b, s, _ = main.shape
    sc = main_ctx.shape[1]
    rows = s // GRID_W
    kr = min(NA_WIN_R, rows)
    bias = _na_bias_table(rpb, rows, kr)
    w = NA_WIDTH

    per_step = next(n for n in (4, 2, 1) if rows % n == 0)

    def bias_spec(j):
        def bias_map(bi, i):
            r = i * per_step + j
            return (r - jnp.clip(r - kr // 2, 0, rows - kr), 0, 0)
        return pl.BlockSpec((1, NA_HEADS * GRID_W, kr * GRID_W), bias_map)

    return pl.pallas_call(
        functools.partial(_na_kernel, rows=rows, kr=kr),
        grid=(b, rows // per_step),
        in_specs=[pl.BlockSpec((1, per_step * GRID_W, w), lambda bi, i: (bi, i, OFF_Q_NA // w)),
                  pl.BlockSpec((1, s, w), lambda bi, i: (bi, 0, OFF_K_NA // w)),
                  pl.BlockSpec((1, s, w), lambda bi, i: (bi, 0, OFF_V_NA // w)),
                  pl.BlockSpec((1, sc, w), lambda bi, i: (bi, 0, OFF_K_NA // w)),
                  pl.BlockSpec((1, sc, w), lambda bi, i: (bi, 0, OFF_V_NA // w))]
                 + [bias_spec(j) for j in range(per_step)],
        out_specs=pl.BlockSpec((1, per_step * GRID_W, w), lambda bi, i: (bi, i, 0)),
        out_shape=jax.ShapeDtypeStruct((b, s, w), BF16),
        compiler_params=_cparams(("parallel", "arbitrary")),
        name="na_latent",
    )(main, main, main, main_ctx, main_ctx, *([bias] * per_step))


def _dense_attn_kernel(q_ref, k_ref, v_ref, o_ref):
    q = q_ref[0]
    k = k_ref[0]
    v = v_ref[0]
    outs = []
    for h in range(NA_HEADS):
        sl = slice(h * NA_HEAD_DIM, (h + 1) * NA_HEAD_DIM)
        outs.append(_softmax_av(q[:, sl], [k[:, sl]], [v[:, sl]], [None]))
    o_ref[0] = jnp.concatenate(outs, axis=-1).astype(o_ref.dtype)


def _dense_attn(main_ctx):
    b, sc, _ = main_ctx.shape
    w = NA_WIDTH
    return pl.pallas_call(
        _dense_attn_kernel,
        grid=(b,),
        in_specs=[pl.BlockSpec((1, sc, w), lambda bi: (bi, 0, OFF_Q_NA // w)),
                  pl.BlockSpec((1, sc, w), lambda bi: (bi, 0, OFF_K_NA // w)),
                  pl.BlockSpec((1, sc, w), lambda bi: (bi, 0, OFF_V_NA // w))],
        out_specs=pl.BlockSpec((1, sc, w), lambda bi: (bi, 0, 0)),
        out_shape=jax.ShapeDtypeStruct((b, sc, w), BF16),
        compiler_params=_cparams(("parallel",)),
        name="ctx_attn",
    )(main_ctx, main_ctx, main_ctx)


HALO = 16


def _gla_matrices(reverse):
    c = GLA_C
    t = np.arange(c)[:, None]
    m = np.arange(c)[None, :]
    blocks = [m <= t, m > t]
    for b in GLA_LEVELS:
        first = (t // (2 * b)) * (2 * b) + b
        is_q = (t & b) != 0
        blocks.append(np.where(is_q, (m > first) & (m <= t), (m > t) & (m <= first)))
    mats = np.stack(blocks).astype(np.float32)
    if reverse:
        mats = mats[:, ::-1, ::-1]
    mats = mats.reshape(-1, c)
    return jnp.asarray(np.concatenate([mats, mats], axis=1), dtype=BF16)


def _split_bf16(x):
    hi = x.astype(BF16)
    return hi, (x - hi.astype(F32)).astype(BF16)


def _pair_block_diag(x):
    lane = lax.broadcasted_iota(jnp.int32, x.shape, 1)
    zero = jnp.zeros((), x.dtype)
    return jnp.concatenate([jnp.where(lane < GLA_DK, x, zero), jnp.where(lane >= GLA_DK, x, zero)], axis=0)


def _gla_kernel(*refs, emit):
    n_in = 9 if emit else 8
    n_out = 2 if emit else 1
    ins, outs, scratch = refs[:2 * n_in], refs[2 * n_in:2 * (n_in + n_out)], refs[2 * (n_in + n_out):]
    ins = [ins[d * n_in:(d + 1) * n_in] for d in range(2)]
    outs = [outs[d * n_out:(d + 1) * n_out] for d in range(2)]
    step = pl.program_id(1)

    @pl.when(step == 0)
    def _():
        for d in range(2):
            scratch[d][...] = ins[d][-1][0]

    pending = [_gla_direction(ins[d][:-1], outs[d][:-1], scratch[d], reverse, emit)
               for d, reverse in enumerate((False, True))]
    while pending:
        pending = [stages for stages in pending if next(stages, "done") != "done"]

    @pl.when(step == pl.num_programs(1) - 1)
    def _():
        for d in range(2):
            outs[d][-1][0] = scratch[d][...]


def _gla_direction(ins, outs, st_ref, reverse, emit):
    if emit:
        q_ref, k_ref, v_ref, gt_ref, w2a_ref, w2b_ref, b2_ref, a_ref = ins
        (o_ref,) = outs
    else:
        k_ref, v_ref, gt_ref, w2a_ref, w2b_ref, b2_ref, a_ref = ins
    c = GLA_C
    pw = 2 * GLA_DK

    lr_hi, lr_lo = _split_bf16(gt_ref[0])
    logit = (_dot(jnp.concatenate([lr_hi, lr_lo], axis=1), w2a_ref[...]) + _dot(lr_hi, w2b_ref[...])
             + b2_ref[...])
    g = (jnp.minimum(logit, 0.0) - jnp.log1p(jnp.exp(-jnp.abs(logit)))) * (LOG2_E / GLA_GATE_TAU)
    g_hi, g_lo = _split_bf16(g)
    args = _dot(a_ref[...], jnp.concatenate([g_hi, g_lo], axis=0))
    cum = args[0:c]
    rem = args[c:2 * c]
    last_row = cum[0:1] if reverse else cum[c - 1:c]
    yield

    k = k_ref[0].astype(F32)
    v = v_ref[0]
    atts = []
    if emit:
        q = q_ref[0].astype(F32) * (GLA_DK ** -0.5)
        row_t = lax.broadcasted_iota(jnp.int32, (c, pw), 0)
        si = lax.broadcasted_iota(jnp.int32, (2 * c, c), 0) & (c - 1)
        ti = lax.broadcasted_iota(jnp.int32, (2 * c, c), 1)
        if reverse:
            row_t, ti, si = c - 1 - row_t, c - 1 - ti, c - 1 - si
        for hp in range(GLA_HEADS // 2):
            cs = slice(hp * pw, (hp + 1) * pw)
            qp, kp = q[:, cs], k[:, cs]
            att = jnp.where(ti == si, _dot_nt(_pair_block_diag(kp.astype(BF16)), qp.astype(BF16)), 0.0)
            for l, b in enumerate(GLA_LEVELS):
                x = (jnp.exp2(args[(2 + l) * c:(3 + l) * c, cs])
                     * jnp.where((row_t & b) != 0, qp, kp)).astype(BF16)
                pair = (((ti ^ si) >> (b.bit_length() - 1)) == 1) & ((ti & b) != 0)
                att = jnp.where(pair, _dot_nt(_pair_block_diag(x), x), att)
                yield
            atts.append(att.astype(BF16))

    outs = []
    for h in range(GLA_HEADS):
        sl = slice(h * GLA_DK, (h + 1) * GLA_DK)
        kh = k[:, sl]
        vh = v[:, h * GLA_DV:(h + 1) * GLA_DV]
        state = st_ref[h]
        kd = (kh * jnp.exp2(rem[:, sl])).astype(BF16)
        decay = jnp.exp2(jnp.broadcast_to(last_row[:, sl], (GLA_DK, GLA_DK))).T
        decay = jnp.concatenate([decay] * (GLA_DV // GLA_DK), axis=1)
        if emit:
            qd = (q[:, sl] * jnp.exp2(cum[:, sl])).astype(BF16)
            att_t = atts[h // 2][(h % 2) * c:(h % 2 + 1) * c]
            both = _dot_tn(jnp.concatenate([att_t, kd], axis=1), vh)
            outs.append(_dot(qd, state.astype(BF16)) + both[:c])
            st_ref[h] = decay * state + both[c:]
        else:
            st_ref[h] = decay * state + _dot_tn(kd, vh)
        yield

    if emit:
        o_ref[0] = jnp.concatenate(outs, axis=-1).astype(o_ref.dtype)


def _gla_scan(main, gate, gate_ws, s0s, emit):
    b, l, _ = main.shape
    n = l // GLA_C
    const = lambda arr: pl.BlockSpec(arr.shape, lambda bi, s: (0,) * arr.ndim)
    state_spec = pl.BlockSpec((1, GLA_HEADS, GLA_DK, GLA_DV), lambda bi, s: (bi, 0, 0, 0))
    state_shape = jax.ShapeDtypeStruct((b, GLA_HEADS, GLA_DK, GLA_DV), F32)
    in_specs, args, out_specs, out_shape = [], [], [], []
    for reverse in (False, True):
        amat = _gla_matrices(reverse)
        w2a, w2b, b2 = gate_ws[reverse]

        def col(block, reverse=reverse):
            return lambda bi, s: (bi, n - 1 - s if reverse else s, block)

        if emit:
            in_specs.append(pl.BlockSpec((1, GLA_C, GLA_KEY_WIDTH), col(OFF_Q_GLA // GLA_KEY_WIDTH)))
            args.append(main)
            out_specs.append(pl.BlockSpec((1, GLA_C, GLA_VAL_WIDTH), col(0)))
            out_shape.append(jax.ShapeDtypeStruct((b, l, GLA_VAL_WIDTH), BF16))
        in_specs += [pl.BlockSpec((1, GLA_C, GLA_KEY_WIDTH), col(OFF_K_GLA // GLA_KEY_WIDTH)),
                     pl.BlockSpec((1, GLA_C, GLA_VAL_WIDTH), col(OFF_V_GLA // GLA_VAL_WIDTH)),
                     pl.BlockSpec((1, GLA_C, LANE), col(0)),
                     const(w2a), const(w2b), const(b2), const(amat), state_spec]
        args += [main, main, gate, w2a, w2b, b2, amat, s0s[reverse]]
        out_specs.append(state_spec)
        out_shape.append(state_shape)
    res = pl.pallas_call(
        functools.partial(_gla_kernel, emit=emit),
        grid=(b, n),
        in_specs=in_specs,
        out_specs=out_specs,
        out_shape=out_shape,
        scratch_shapes=[pltpu.VMEM((GLA_HEADS, GLA_DK, GLA_DV), F32)] * 2,
        compiler_params=_cparams(("parallel", "arbitrary")),
        name="gla_scan",
    )(*args)
    return (res[0], res[2], res[1], res[3]) if emit else (None, None, res[0], res[1])


def _gla_gate_weights(gate_w, gate_b):
    out = []
    for dr in range(2):
        w = jnp.zeros((LANE, GLA_KEY_WIDTH), F32)
        w = w.at[dr * GLA_GATE_RANK:(dr + 1) * GLA_GATE_RANK].set(gate_w[dr])
        w_hi = w.astype(BF16)
        w_lo = (w - w_hi.astype(F32)).astype(BF16)
        out.append((jnp.concatenate([w_hi, w_hi], axis=0), w_lo, gate_b[dr][None]))
    return out


def _conv_tile(b_ref, c_ref, x_ref, cp_ref, xp_ref, cn_ref, xn_ref, w_ref):
    i, n = pl.program_id(1), pl.num_programs(1)
    u = c_ref[0].astype(F32) * x_ref[0].astype(F32)
    tm = u.shape[0]
    before = jnp.where(i > 0, 1.0, 0.0) * (cp_ref[0, HALO - 1:HALO].astype(F32) * xp_ref[0, HALO - 1:HALO].astype(F32))
    after = jnp.where(i < n - 1, 1.0, 0.0) * (cn_ref[0, 0:1].astype(F32) * xn_ref[0, 0:1].astype(F32))
    t = lax.broadcasted_iota(jnp.int32, u.shape, 0)
    prev = jnp.where(t == 0, before, pltpu.roll(u, 1, axis=0))
    nxt = jnp.where(t == tm - 1, after, pltpu.roll(u, tm - 1, axis=0))
    w = w_ref[...]
    return b_ref[0].astype(F32) * (prev * w[0:1] + u * w[1:2] + nxt * w[2:3])


def _merge_kernel(ona_ref, bsc_ref, csc_ref, xsc_ref, cp_ref, xp_ref, cn_ref, xn_ref, cw_ref, of_ref, ob_ref,
                  r_ref, gna_ref, gsc_ref, ggl_ref, x_ref, gt_ref,
                  gn_ref, wna_ref, wsc_ref, wgl_ref, wo_ref, g2_ref, sh2_ref, sc2_ref, wr_ref,
                  xo_ref, h2_ref, hp_ref, lg_ref):
    o_sc = _conv_tile(bsc_ref, csc_ref, xsc_ref, cp_ref, xp_ref, cn_ref, xn_ref, cw_ref).astype(BF16)
    o = of_ref[0].astype(F32) + ob_ref[0].astype(F32)
    normed = []
    for h in range(GLA_HEADS):
        oh = o[:, h * GLA_DV:(h + 1) * GLA_DV]
        ms = jnp.mean(oh * oh, axis=-1, keepdims=True)
        normed.append(oh * lax.rsqrt(ms + RMS_EPS))
    r = r_ref[0].astype(F32)
    y_gla = jnp.concatenate(normed, axis=-1) * gn_ref[...] * (r * _sigmoid(r))
    y = (_sigmoid(gna_ref[0].astype(F32)) * _dot(ona_ref[0], wna_ref[...])
         + _sigmoid(gsc_ref[0].astype(F32)) * _dot(o_sc, wsc_ref[...])
         + _sigmoid(ggl_ref[0].astype(F32)) * _dot(y_gla.astype(BF16), wgl_ref[...]))
    xn = x_ref[0] + gt_ref[0] * _dot(y.astype(BF16), wo_ref[...])
    xo_ref[0] = xn
    ms = jnp.mean(xn * xn, axis=-1, keepdims=True)
    h2 = xn * lax.rsqrt(ms + RMS_EPS) * g2_ref[...] * (1.0 + sc2_ref[0]) + sh2_ref[0]
    h2b = h2.astype(BF16)
    h2_ref[0] = h2b
    _store_parts(hp_ref, _pack_rows(h2))
    lg_ref[0] = _dot(h2b, wr_ref[...])


def _merge(o_na, conv_w, o_f, o_b, main, x, gt1, gn, w_na, w_sc, w_gla, w_out, g2, sh2, sc2, w_router, tm):
    b, s, d = x.shape
    per_batch = gt1.shape[0] == b
    mod_map = (lambda bi, i: (bi, 0, 0)) if per_batch else (lambda bi, i: (0, 0, 0))
    tok = lambda width, blk: pl.BlockSpec((1, tm, width), lambda bi, i: (bi, i, blk))
    full = lambda arr: pl.BlockSpec(arr.shape, lambda bi, i: (0,) * arr.ndim)
    mod = pl.BlockSpec((1, 1, d), mod_map)
    gn_t = jnp.tile(gn, GLA_HEADS)[None]
    g2_t = g2[None]
    per_tile = tm // HALO
    last_halo = s // HALO - 1
    halo_prev = lambda blk: pl.BlockSpec(
        (1, HALO, SC_WIDTH), lambda bi, i: (bi, jnp.maximum(i * per_tile - 1, 0), blk))
    halo_next = lambda blk: pl.BlockSpec(
        (1, HALO, SC_WIDTH), lambda bi, i: (bi, jnp.minimum((i + 1) * per_tile, last_halo), blk))
    col_b, col_c, col_x = OFF_B_SC // SC_WIDTH, OFF_C_SC // SC_WIDTH, OFF_X_SC // SC_WIDTH
    return pl.pallas_call(
        _merge_kernel,
        grid=(b, s // tm),
        in_specs=[tok(NA_WIDTH, 0), tok(SC_WIDTH, col_b), tok(SC_WIDTH, col_c), tok(SC_WIDTH, col_x),
                  halo_prev(col_c), halo_prev(col_x), halo_next(col_c), halo_next(col_x), full(conv_w),
                  tok(GLA_VAL_WIDTH, 0), tok(GLA_VAL_WIDTH, 0),
                  tok(d, OFF_R_GLA // d), tok(d, OFF_MERGE // d), tok(d, OFF_MERGE // d + 1),
                  tok(d, OFF_MERGE // d + 2), tok(d, 0), mod,
                  full(gn_t), full(w_na), full(w_sc), full(w_gla), full(w_out), full(g2_t), mod, mod,
                  full(w_router)],
        out_specs=[tok(d, 0), tok(d, 0),
                   pl.BlockSpec((SC_PARTS, 1, tm, SC_ROW), lambda bi, i: (0, bi, i, 0)), tok(LANE, 0)],
        out_shape=[jax.ShapeDtypeStruct((b, s, d), F32),
                   jax.ShapeDtypeStruct((b, s, d), BF16),
                   jax.ShapeDtypeStruct((SC_PARTS, b, s, SC_ROW), U32),
                   jax.ShapeDtypeStruct((b, s, LANE), F32)],
        compiler_params=_cparams(("parallel", "parallel")),
        name="merge",
    )(o_na, main, main, main, main, main, main, main, conv_w, o_f, o_b, main, main, main, main, x, gt1,
      gn_t, w_na, w_sc, w_gla, w_out, g2_t, sh2, sc2, w_router)


def _router_kernel(lg_ref, br_ref, tri_ref, eidx_ref, gw_ref, rank_ref, cnt_ref, carry_ref):
    tm = lg_ref.shape[0]

    @pl.when(pl.program_id(0) == 0)
    def _():
        carry_ref[...] = jnp.zeros_like(carry_ref)

    scores = _sigmoid(lg_ref[...].T[:N_EXPERTS])
    sel = scores + br_ref[...]
    neg = -jnp.inf

    sel3 = sel.reshape(N_EXPERT_GROUPS, GROUP_SIZE, tm)
    i3 = lax.broadcasted_iota(jnp.int32, sel3.shape, 1)
    m1 = sel3.max(axis=1, keepdims=True)
    first = jnp.where(sel3 == m1, i3, GROUP_SIZE).min(axis=1, keepdims=True)
    m2 = jnp.where(i3 == first, neg, sel3).max(axis=1, keepdims=True)
    gscore = (m1 + m2)[:, 0, :]

    gi = lax.broadcasted_iota(jnp.int32, gscore.shape, 0)
    gmask = jnp.zeros(gscore.shape, jnp.bool_)
    for _ in range(TOPK_GROUPS):
        m = gscore.max(axis=0, keepdims=True)
        pick = gi == jnp.where(gscore == m, gi, N_EXPERT_GROUPS).min(axis=0, keepdims=True)
        gmask = gmask | pick
        gscore = jnp.where(pick, neg, gscore)
    emask = jnp.broadcast_to(gmask[:, None, :], sel3.shape).reshape(N_EXPERTS, tm)
    sel = jnp.where(emask, sel, neg)

    ei = lax.broadcasted_iota(jnp.int32, sel.shape, 0)
    picks, idxs, ws = [], [], []
    for _ in range(TOP_K):
        m = sel.max(axis=0, keepdims=True)
        idx = jnp.where(sel == m, ei, N_EXPERTS).min(axis=0, keepdims=True)
        pick = ei == idx
        picks.append(pick)
        idxs.append(idx)
        ws.append(jnp.where(pick, scores, 0.0).sum(axis=0, keepdims=True))
        sel = jnp.where(pick, neg, sel)
    w = jnp.concatenate(ws, axis=0)
    gw_ref[...] = w / w.sum(axis=0, keepdims=True) * ROUTED_SCALE
    eidx_ref[...] = jnp.concatenate(idxs, axis=0)

    onehot = picks[0]
    for p in picks[1:]:
        onehot = onehot | p
    onehot = jnp.where(onehot, 1.0, 0.0).astype(BF16)
    before = _dot(onehot, tri_ref[...]) + jnp.tile(carry_ref[...], (1, tm // LANE))
    rank_ref[...] = jnp.concatenate(
        [jnp.where(p, before, 0.0).sum(axis=0, keepdims=True) for p in picks], axis=0).astype(jnp.int32)
    carry_ref[...] += _dot(onehot, jnp.ones((tm, LANE), BF16))
    cnt_ref[...] = carry_ref[...]


def _route(logits, b_router, tm=512):
    t = logits.shape[0]
    br = jnp.broadcast_to(b_router.astype(F32)[:, None], (N_EXPERTS, tm))
    tri = jnp.asarray(np.triu(np.ones((tm, tm), np.float32), 1), dtype=BF16)
    kt = lambda dt: jax.ShapeDtypeStruct((TOP_K, t), dt)
    eidx, gw, rank, cnt = pl.pallas_call(
        _router_kernel,
        grid=(t // tm,),
        in_specs=[pl.BlockSpec((tm, LANE), lambda i: (i, 0)),
                  pl.BlockSpec((N_EXPERTS, tm), lambda i: (0, 0)),
                  pl.BlockSpec((tm, tm), lambda i: (0, 0))],
        out_specs=[pl.BlockSpec((TOP_K, tm), lambda i: (0, i)),
                   pl.BlockSpec((TOP_K, tm), lambda i: (0, i)),
                   pl.BlockSpec((TOP_K, tm), lambda i: (0, i)),
                   pl.BlockSpec((N_EXPERTS, LANE), lambda i: (0, 0))],
        out_shape=[kt(jnp.int32), kt(F32), kt(jnp.int32),
                   jax.ShapeDtypeStruct((N_EXPERTS, LANE), F32)],
        scratch_shapes=[pltpu.VMEM((N_EXPERTS, LANE), F32)],
        compiler_params=_cparams(("arbitrary",)),
        name="router",
    )(logits, br, tri)
    return eidx, gw, rank, cnt[:, 0].astype(jnp.int32)


def _sc_mesh():
    return plsc.VectorSubcoreMesh(core_axis_name="core", subcore_axis_name="subcore")


def _dispatch_rows(xp, dest, slots):
    parts, t, _ = xp.shape
    nwin = parts * t // SC_WINDOW
    idx = dest.reshape(TOP_K, t // SC_WINDOW, SC_WINDOW).transpose(1, 0, 2)
    idx = jnp.concatenate([idx + part * slots for part in range(parts)], axis=0)

    @pl.kernel(out_type=jax.ShapeDtypeStruct((parts * slots, SC_ROW), xp.dtype), mesh=_sc_mesh(),
               scratch_types=[], name="moe_dispatch")
    def run(x_hbm, i_hbm, o_hbm):
        def body(x_vmem, i_vmem):
            for k in range(TOP_K):
                pltpu.sync_copy(x_vmem, o_hbm.at[i_vmem.at[0, k]])

        pltpu.emit_pipeline(
            body,
            grid=(nwin,),
            in_specs=[pl.BlockSpec((SC_WINDOW, SC_ROW), lambda i: (i, 0)),
                      pl.BlockSpec((1, TOP_K, SC_WINDOW), lambda i: (i, 0, 0))],
            out_specs=[],
            core_axis_name=("core", "subcore"),
            dimension_semantics=(pltpu.PARALLEL,),
        )(x_hbm, i_hbm)

    return run(xp.reshape(parts * t, SC_ROW), idx).reshape(parts, slots, SC_ROW)


def _gather_rows(yp, idx):
    n = idx.shape[0]
    parts, slots, _ = yp.shape
    idx = jnp.concatenate([idx + part * slots for part in range(parts)]).reshape(1, n * parts)

    @pl.kernel(out_type=jax.ShapeDtypeStruct((n * parts, SC_ROW), yp.dtype), mesh=_sc_mesh(),
               scratch_types=[], name="moe_gather")
    def run(y_hbm, i_hbm, o_hbm):
        def body(i_vmem, o_vmem):
            pltpu.sync_copy(y_hbm.at[i_vmem.at[0]], o_vmem)

        pltpu.emit_pipeline(
            body,
            grid=(n * parts // SC_WINDOW,),
            in_specs=[pl.BlockSpec((1, SC_WINDOW), lambda i: (0, i))],
            out_specs=[pl.BlockSpec((SC_WINDOW, SC_ROW), lambda i: (i, 0))],
            core_axis_name=("core", "subcore"),
            dimension_semantics=(pltpu.PARALLEL,),
        )(i_hbm, o_hbm)

    return run(yp.reshape(parts * slots, SC_ROW), idx).reshape(parts, n, SC_ROW)


def _expert_kernel(be_ref, bv_ref, bs_ref, x_ref, wg_ref, wu_ref, wd_ref, o_ref, wg_s, wu_s, wd_s):
    i = pl.program_id(0)
    valid = bv_ref[i]
    new_expert = (i == 0) | (be_ref[i] != be_ref[jnp.maximum(i - 1, 0)])

    @pl.when(new_expert)
    def _():
        wg_s[...] = wg_ref[0, 0].astype(BF16)
        wu_s[...] = wu_ref[0, 0].astype(BF16)
        wd_s[...] = wd_ref[0, 0].astype(BF16)

    @pl.when(valid > 0)
    def _():
        w = _load_parts(x_ref)
        row = lax.broadcasted_iota(jnp.int32, w.shape, 0)
        w = jnp.where(row < valid, w, jnp.uint32(0))
        lo, hi = _unpack_rows(w)
        x = jnp.concatenate([lo, hi], axis=1).astype(BF16)
        a = _dot(x, wg_s[...])
        hid = a * _sigmoid(a) * _dot(x, wu_s[...])
        _store_parts(o_ref, _pack_rows(_dot(hid.astype(BF16), wd_s[...])))


def _experts(xs, blk_e, blk_valid, blk_src, layer, w_gate, w_up, w_down):
    parts, slots, _ = xs.shape
    d = D_MODEL
    nb = slots // MOE_BLOCK
    return pl.pallas_call(
        _expert_kernel,
        grid_spec=pltpu.PrefetchScalarGridSpec(
            num_scalar_prefetch=3,
            grid=(nb,),
            in_specs=[pl.BlockSpec((parts, MOE_BLOCK, SC_ROW), lambda i, be, bv, bs: (0, bs[i], 0)),
                      pl.BlockSpec((1, 1, d, EXPERT_FF), lambda i, be, bv, bs: (layer, be[i], 0, 0)),
                      pl.BlockSpec((1, 1, d, EXPERT_FF), lambda i, be, bv, bs: (layer, be[i], 0, 0)),
                      pl.BlockSpec((1, 1, EXPERT_FF, d), lambda i, be, bv, bs: (layer, be[i], 0, 0))],
            out_specs=pl.BlockSpec((parts, MOE_BLOCK, SC_ROW), lambda i, be, bv, bs: (0, bs[i], 0)),
            scratch_shapes=[pltpu.VMEM((d, EXPERT_FF), BF16), pltpu.VMEM((d, EXPERT_FF), BF16),
                            pltpu.VMEM((EXPERT_FF, d), BF16)]),
        out_shape=jax.ShapeDtypeStruct((parts, slots, SC_ROW), U32),
        compiler_params=_cparams(("arbitrary",)),
        name="experts",
    )(blk_e, blk_valid, blk_src, xs, w_gate, w_up, w_down)


def _combine_kernel(yg_ref, gw_ref, h_ref, x_ref, gt_ref, wsg_ref, wsu_ref, wsd_ref, gf_ref, o_ref, *, final):
    h = h_ref[0]
    a = _dot(h, wsg_ref[...])
    hid = a * _sigmoid(a) * _dot(h, wsu_ref[...])
    y = _dot(hid.astype(BF16), wsd_ref[...])
    gw = gw_ref[...]
    y_lo = y[:, :D_MODEL // 2]
    y_hi = y[:, D_MODEL // 2:]
    for k in range(TOP_K):
        lo, hi = _unpack_rows(_load_parts(yg_ref, k))
        y_lo = y_lo + gw[:, k:k + 1] * lo
        y_hi = y_hi + gw[:, k:k + 1] * hi
    y = jnp.concatenate([y_lo, y_hi], axis=1)
    xn = x_ref[0] + gt_ref[0] * y
    if final:
        ms = jnp.mean(xn * xn, axis=-1, keepdims=True)
        xn = xn * lax.rsqrt(ms + RMS_EPS) * gf_ref[...]
    o_ref[0] = xn


def _combine(yg, gw, tok_off, h2, x, gt2, b0, nb, ws_gate, ws_up, ws_down, g_final, final, tm):
    b, s, d = x.shape
    per_batch = gt2.shape[0] == b
    mod_map = (lambda bi, i: (b0 + bi, 0, 0)) if per_batch else (lambda bi, i: (0, 0, 0))
    full = lambda arr: pl.BlockSpec(arr.shape, lambda bi, i: (0,) * arr.ndim)
    tok = lambda width: pl.BlockSpec((1, tm, width), lambda bi, i: (b0 + bi, i, 0))
    gf = g_final[None]
    nblk = s // tm
    blk0 = (tok_off + b0 * s) // tm
    return pl.pallas_call(
        functools.partial(_combine_kernel, final=final),
        grid=(nb, nblk),
        in_specs=[pl.BlockSpec((SC_PARTS, TOP_K, tm, SC_ROW), lambda bi, i: (0, 0, bi * nblk + i, 0)),
                  pl.BlockSpec((tm, TOP_K), lambda bi, i: (blk0 + bi * nblk + i, 0)),
                  tok(d), tok(d),
                  pl.BlockSpec((1, 1, d), mod_map),
                  full(ws_gate), full(ws_up), full(ws_down), full(gf)],
        out_specs=tok(d),
        out_shape=jax.ShapeDtypeStruct((b, s, d), F32),
        input_output_aliases={3: 0},
        compiler_params=_cparams(("parallel", "parallel")),
        name="combine",
    )(yg, gw, h2, x, gt2, ws_gate, ws_up, ws_down, gf)


def _layer(x, ctx_s, mods, mods_ctx, p, ctx_out, final, g_final):
    b, s, d = x.shape
    sc = ctx_s.shape[1]
    sh1, sc1, gt1, sh2, sc2, gt2 = mods
    csh1, csc1, cgt1, csh2, csc2, cgt2 = mods_ctx

    w_main, w_gate = _prep_w_in(p['w_in'], p['layer'])
    main, gate = _proj_in(x, p['g_norm1'], sh1, sc1, w_main, w_gate, tm=min(2048, s), tn=1024)
    ctx_flat = ctx_s.reshape(1, b * sc, d)
    n_ctx, tn_ctx = (N_MAIN, 1024) if ctx_out else (N_KV_MAIN, N_KV_MAIN // 2)
    main_c, gate_c = _proj_in(ctx_flat, p['g_norm1'], csh1, csc1, w_main, w_gate, tm=min(1024, b * sc),
                              tn=tn_ctx, n=n_ctx)
    main_c = main_c.reshape(b, sc, n_ctx)
    gate_c = gate_c.reshape(b, sc, LANE)

    o_na = _na_latent(main, main_c, p['na_rpb'])

    gate_ws = _gla_gate_weights(p['gla_gate_w'], p['gla_gate_b'])
    s0 = jnp.zeros((b, GLA_HEADS, GLA_DK, GLA_DV), F32)
    o_cf, o_cb, st_f, st_b = _gla_scan(main_c, gate_c, gate_ws, (s0, s0), ctx_out)
    o_f, o_b, _, _ = _gla_scan(main, gate, gate_ws, (st_f, st_b), True)

    w_na = p['w_branch_na'].astype(BF16)
    w_sc = p['w_branch_sc'].astype(BF16)
    w_gla = p['w_branch_gla'].astype(BF16)
    w_out = p['w_out'].astype(BF16)
    w_router = jnp.pad(p['w_router'], ((0, 0), (0, LANE - N_EXPERTS))).astype(BF16)
    x, h2, h2p, logits = _merge(o_na, p['conv_w'], o_f, o_b, main, x, gt1, p['gla_norm_g'], w_na, w_sc, w_gla, w_out,
                                p['g_norm2'], sh2, sc2, w_router, tm=min(512, s))
    n_lat = b * s
    hp_all = h2p.reshape(SC_PARTS, n_lat, SC_ROW)
    lg_all = logits.reshape(n_lat, LANE)
    if ctx_out:
        o_na_c = _dense_attn(main_c)
        ctx_s, h2_c, h2p_c, lg_c = _merge(o_na_c, p['conv_w'], o_cf, o_cb, main_c, ctx_s, cgt1, p['gla_norm_g'],
                                          w_na, w_sc, w_gla, w_out, p['g_norm2'], csh2, csc2, w_router,
                                          tm=min(256, sc))
        hp_all = jnp.concatenate([hp_all, h2p_c.reshape(SC_PARTS, b * sc, SC_ROW)], axis=1)
        lg_all = jnp.concatenate([lg_all, lg_c.reshape(b * sc, LANE)], axis=0)

    t = hp_all.shape[1]
    eidx, gw, rank, counts = _route(lg_all, p['b_router'])
    padded = (counts + MOE_BLOCK - 1) // MOE_BLOCK * MOE_BLOCK
    pad_end = jnp.cumsum(padded)
    pad_start = pad_end - padded
    onehot = eidx[:, :, None] == jnp.arange(N_EXPERTS, dtype=jnp.int32)
    dest = jnp.sum(jnp.where(onehot, pad_start, 0), axis=-1) + rank
    n_blocks = -(-(t * TOP_K + N_EXPERTS * (MOE_BLOCK - 1)) // MOE_BLOCK)
    slots = n_blocks * MOE_BLOCK
    blk_start = jnp.arange(n_blocks, dtype=jnp.int32) * MOE_BLOCK
    blk_e = jnp.minimum(jnp.sum(pad_end[None, :] <= blk_start[:, None], axis=1), N_EXPERTS - 1).astype(jnp.int32)
    used_end = (pad_start + counts)[blk_e]
    blk_valid = jnp.clip(used_end - blk_start, 0, MOE_BLOCK).astype(jnp.int32)
    n_used = pad_end[-1] // MOE_BLOCK
    blk_src = jnp.minimum(jnp.arange(n_blocks, dtype=jnp.int32), n_used - 1)
    blk_e = blk_e[blk_src]

    xs = _dispatch_rows(hp_all, dest, slots)
    ys = _experts(xs, blk_e, blk_valid, blk_src, p['layer'], p['w_exp_gate'], p['w_exp_up'], p['w_exp_down'])
    gw_t = gw.T
    ws_gate = p['w_sh_gate'].astype(BF16)
    ws_up = p['w_sh_up'].astype(BF16)
    ws_down = p['w_sh_down'].astype(BF16)

    def gathered(t0, n):
        return _gather_rows(ys, dest[:, t0:t0 + n].reshape(-1)).reshape(SC_PARTS, TOP_K, n, SC_ROW)

    pieces = next(n for n in (4, 2, 1) if b % n == 0)
    nb = b // pieces
    for q in range(pieces):
        x = _combine(gathered(q * nb * s, nb * s), gw_t, 0, h2, x, gt2, q * nb, nb, ws_gate, ws_up, ws_down,
                     g_final, final, tm=min(256, s))
    if ctx_out:
        ctx_s = _combine(gathered(n_lat, b * sc), gw_t, n_lat, h2_c, ctx_s, cgt2, 0, b, ws_gate, ws_up, ws_down,
                         g_final, False, tm=min(256, sc))
    return x, ctx_s


def kernel(x, c, ctx, c_ctx, w_mod, b_mod, g_norm1, g_norm2, w_in, na_rpb, w_branch_na, conv_w, w_branch_sc,
           gla_gate_w, gla_gate_b, gla_norm_g, w_branch_gla, w_out, w_router, b_router, w_exp_gate, w_exp_up,
           w_exp_down, w_sh_gate, w_sh_up, w_sh_down, g_final):
    stacked = dict(g_norm1=g_norm1, g_norm2=g_norm2, na_rpb=na_rpb, w_branch_na=w_branch_na,
                   conv_w=conv_w, w_branch_sc=w_branch_sc, gla_gate_w=gla_gate_w, gla_gate_b=gla_gate_b,
                   gla_norm_g=gla_norm_g, w_branch_gla=w_branch_gla, w_out=w_out, w_router=w_router,
                   b_router=b_router,
                   w_sh_gate=w_sh_gate, w_sh_up=w_sh_up, w_sh_down=w_sh_down)
    depth = w_in.shape[0]
    ctx_s = ctx
    for i in range(depth):
        p = {name: arr[i] for name, arr in stacked.items()}
        p.update(layer=i, w_in=w_in, w_exp_gate=w_exp_gate, w_exp_up=w_exp_up, w_exp_down=w_exp_down)
        mods, mods_ctx = _mod_vectors(c, c_ctx, w_mod, b_mod, i)
        last = i == depth - 1
        x, ctx_s = _layer(x, ctx_s, mods, mods_ctx, p, not last, last, g_final)
    return x
```

```python
import functools

import numpy as np
import jax
import jax.numpy as jnp
from jax import lax
from jax.experimental import pallas as pl
from jax.experimental.pallas import tpu as pltpu
from jax.experimental.pallas import tpu_sc as plsc

F32 = jnp.float32
BF16 = jnp.bfloat16
U32 = jnp.uint32

D_MODEL = 1024
N_MOD = 6
RMS_EPS = 1e-6
NEG_INF = -1e30
GRID_W = 64
NA_HEADS = 8
NA_HEAD_DIM = 64
NA_WIDTH = NA_HEADS * NA_HEAD_DIM
NA_WIN_R = 8
NA_WIN_C = 16
NA_GROUP = 4
SC_WIDTH = 512
GLA_HEADS = 4
GLA_KEY_WIDTH = 512
GLA_VAL_WIDTH = 1024
GLA_DK = GLA_KEY_WIDTH // GLA_HEADS
GLA_DV = GLA_VAL_WIDTH // GLA_HEADS
GLA_GATE_RANK = 16
GLA_GATE_TAU = 16.0
LOG2_E = 1.4426950408889634
N_EXPERTS = 64
N_EXPERT_GROUPS = 8
GROUP_SIZE = N_EXPERTS // N_EXPERT_GROUPS
TOPK_GROUPS = 4
TOP_K = 8
EXPERT_FF = 256
ROUTED_SCALE = 2.5
MOE_SUB = 256
MOE_BLOCK = 1024

LANE = 128
GLA_C = 128
GLA_LEVELS = tuple(GLA_C >> (i + 1) for i in range(GLA_C.bit_length() - 1))
VMEM_LIMIT = 48 * 1024 * 1024
SC_WINDOW = 128
SC_ROW = 256
SC_PARTS = D_MODEL // 2 // SC_ROW

OFF_V_GLA = 0
OFF_K_NA = 1024
OFF_V_NA = 1536
OFF_K_GLA = 2048
N_KV_MAIN = 2560
OFF_Q_NA = 2560
OFF_B_SC = 3072
OFF_C_SC = 3584
OFF_X_SC = 4096
OFF_Q_GLA = 4608
OFF_R_GLA = 5120
OFF_MERGE = 6144
N_MAIN = 9216


def _cparams(sem, vmem=VMEM_LIMIT):
    return pltpu.CompilerParams(dimension_semantics=sem, vmem_limit_bytes=vmem)


def _dot(a, b):
    return jnp.dot(a, b, preferred_element_type=F32)


def _dot_nt(a, b):
    return lax.dot_general(a, b, (((1,), (1,)), ((), ())), preferred_element_type=F32)


def _dot_tn(a, b):
    return lax.dot_general(a, b, (((0,), (0,)), ((), ())), preferred_element_type=F32)


def _sigmoid(x):
    return 0.5 * jnp.tanh(0.5 * x) + 0.5


def _pack_rows(x):
    n = x.shape[1] // 2
    r = x.astype(BF16).astype(F32)
    lo = pltpu.bitcast(r[:, :n], U32) >> 16
    hi = pltpu.bitcast(r[:, n:], U32)
    return hi | lo


def _store_parts(ref, words):
    for part in range(SC_PARTS):
        dst = ref.at[part, 0] if len(ref.shape) == 4 else ref.at[part]
        dst[...] = words[:, part * SC_ROW:(part + 1) * SC_ROW]


def _load_parts(ref, *lead):
    return jnp.concatenate([ref[(part,) + lead] for part in range(SC_PARTS)], axis=-1)


def _unpack_rows(w):
    lo = pltpu.bitcast(w << 16, F32)
    hi = pltpu.bitcast(w & jnp.uint32(0xFFFF0000), F32)
    return lo, hi


def _mod_kernel(a_ref, w_ref, b_ref, o_ref):
    a = a_ref[...]
    a = a * _sigmoid(a)
    o_ref[...] = _dot(a.astype(BF16), w_ref[0].astype(BF16)) + b_ref[0]


def _mod_vectors(c, c_ctx, w_mod, b_mod, layer):
    b = c.shape[0]
    rows = -(-(b + 1) // 8) * 8
    a = jnp.concatenate([c, c_ctx[None], jnp.zeros((rows - b - 1, D_MODEL), F32)], axis=0)
    n = N_MOD * D_MODEL
    tn = 1536
    out = pl.pallas_call(
        _mod_kernel,
        grid=(n // tn,),
        in_specs=[pl.BlockSpec((rows, D_MODEL), lambda j: (0, 0)),
                  pl.BlockSpec((1, D_MODEL, tn), lambda j: (layer, 0, j)),
                  pl.BlockSpec((1, 1, tn), lambda j: (layer, 0, j))],
        out_specs=pl.BlockSpec((rows, tn), lambda j: (0, j)),
        out_shape=jax.ShapeDtypeStruct((rows, n), F32),
        compiler_params=_cparams(("parallel",)),
        name="mod_vectors",
    )(a, w_mod, b_mod[:, None])
    lat = out[:b].reshape(b, N_MOD, 1, D_MODEL)
    ctx = out[b].reshape(N_MOD, 1, 1, D_MODEL)
    return [lat[:, i] for i in range(N_MOD)], [ctx[i] for i in range(N_MOD)]


def _proj_kernel(x_ref, g_ref, sh_ref, sc_ref, w_ref, wg_ref, o_ref, og_ref, h_ref):
    @pl.when(pl.program_id(2) == 0)
    def _():
        x = x_ref[0]
        ms = jnp.mean(x * x, axis=-1, keepdims=True)
        h = x * lax.rsqrt(ms + RMS_EPS) * g_ref[...] * (1.0 + sc_ref[0]) + sh_ref[0]
        hb = h.astype(BF16)
        h_ref[...] = hb
        og_ref[0] = _dot(hb, wg_ref[...])

    o_ref[0] = _dot(h_ref[...], w_ref[...]).astype(o_ref.dtype)


W_IN_TILE = 512
W_IN_GATE_SHIFT = 2 * GLA_GATE_RANK


def _prep_w_in_kernel(a_ref, b_ref, o_ref, g_ref):
    t = pl.program_id(0)
    first_lat = N_KV_MAIN // W_IN_TILE
    a = a_ref[0]

    @pl.when(t < first_lat)
    def _():
        o_ref[...] = a.T.astype(BF16)

    @pl.when(t >= first_lat)
    def _():
        moved = jnp.concatenate([a[W_IN_GATE_SHIFT:], b_ref[0]], axis=0)
        scale = jnp.where(t == first_lat, NA_HEAD_DIM ** -0.5, 1.0)
        o_ref[...] = (moved * scale).T.astype(BF16)

    @pl.when(t == first_lat)
    def _():
        head = a[:LANE]
        row = lax.broadcasted_iota(jnp.int32, head.shape, 0)
        g_ref[...] = jnp.where(row < W_IN_GATE_SHIFT, head, 0.0).T.astype(BF16)


def _prep_w_in(w_in, layer):
    d = w_in.shape[1]
    w_t = jnp.swapaxes(w_in, 1, 2)
    first_lat = N_KV_MAIN // W_IN_TILE
    kv_perm = OFF_K_NA // W_IN_TILE

    def a_map(t):
        return (layer, jnp.where(t < first_lat, (t + first_lat - kv_perm) % first_lat, t), 0)

    def b_map(t):
        return (layer, jnp.where(t < first_lat, 0, (t + 1) * (W_IN_TILE // W_IN_GATE_SHIFT)), 0)

    return pl.pallas_call(
        _prep_w_in_kernel,
        grid=(N_MAIN // W_IN_TILE,),
        in_specs=[pl.BlockSpec((1, W_IN_TILE, d), a_map),
                  pl.BlockSpec((1, W_IN_GATE_SHIFT, d), b_map)],
        out_specs=[pl.BlockSpec((d, W_IN_TILE), lambda t: (0, t)),
                   pl.BlockSpec((d, LANE), lambda t: (0, 0))],
        out_shape=[jax.ShapeDtypeStruct((d, N_MAIN), BF16), jax.ShapeDtypeStruct((d, LANE), BF16)],
        compiler_params=_cparams(("arbitrary",)),
        name="prep_w_in",
    )(w_t, w_t)


def _proj_in(x, g, shift, scale, w_main, w_gate, tm, tn, n=None):
    b, s, d = x.shape
    n = w_main.shape[1] if n is None else n
    per_batch = shift.shape[0] == b
    mod_map = (lambda bi, i, j: (bi, 0, 0)) if per_batch else (lambda bi, i, j: (0, 0, 0))
    return pl.pallas_call(
        _proj_kernel,
        grid=(b, s // tm, n // tn),
        in_specs=[pl.BlockSpec((1, tm, d), lambda bi, i, j: (bi, i, 0)),
                  pl.BlockSpec((1, d), lambda bi, i, j: (0, 0)),
                  pl.BlockSpec((1, 1, d), mod_map),
                  pl.BlockSpec((1, 1, d), mod_map),
                  pl.BlockSpec((d, tn), lambda bi, i, j: (0, j)),
                  pl.BlockSpec((d, LANE), lambda bi, i, j: (0, 0))],
        out_specs=[pl.BlockSpec((1, tm, tn), lambda bi, i, j: (bi, i, j)),
                   pl.BlockSpec((1, tm, LANE), lambda bi, i, j: (bi, i, 0))],
        out_shape=[jax.ShapeDtypeStruct((b, s, n), BF16),
                   jax.ShapeDtypeStruct((b, s, LANE), F32)],
        scratch_shapes=[pltpu.VMEM((tm, d), BF16)],
        compiler_params=_cparams(("parallel", "parallel", "arbitrary")),
        name="proj_in",
    )(x, g[None], shift, scale, w_main, w_gate)


def _softmax_av(q, keys, vals, biases):
    scores = []
    for kk, bb in zip(keys, biases):
        s = _dot_nt(q, kk)
        scores.append(s if bb is None else s + bb)
    m = scores[0].max(axis=-1, keepdims=True)
    for s in scores[1:]:
        m = jnp.maximum(m, s.max(axis=-1, keepdims=True))
    num = None
    den = None
    for s, vv in zip(scores, vals):
        e = jnp.exp(s - m)
        dsum = e.sum(axis=-1, keepdims=True)
        o = _dot(e.astype(BF16), vv)
        num = o if num is None else num + o
        den = dsum if den is None else den + dsum
    return num / den


def _na_kernel(q_ref, k_ref, v_ref, kc_ref, vc_ref, *rest, rows, kr):
    *bias_refs, o_ref = rest
    kc = kc_ref[0]
    vc = vc_ref[0]
    for j, bias_ref in enumerate(bias_refs):
        r = pl.program_id(1) * len(bias_refs) + j
        row_start = jnp.clip(r - kr // 2, 0, rows - kr)
        start = pl.multiple_of(row_start * GRID_W, GRID_W)
        n_win = kr * GRID_W
        q = q_ref[0, j * GRID_W:(j + 1) * GRID_W, :]
        kw = k_ref[0, pl.ds(start, n_win), :]
        vw = v_ref[0, pl.ds(start, n_win), :]
        o_ref[0, j * GRID_W:(j + 1) * GRID_W, :] = _na_row(q, kw, vw, kc, vc, bias_ref).astype(o_ref.dtype)


def _na_row(q, kw, vw, kc, vc, bias_ref):
    gw = NA_GROUP * NA_HEAD_DIM
    stacked = (NA_GROUP * GRID_W, gw)
    on_head = (lax.broadcasted_iota(jnp.int32, stacked, 0) // GRID_W
               == lax.broadcasted_iota(jnp.int32, stacked, 1) // NA_HEAD_DIM)
    outs = []
    for g in range(NA_HEADS // NA_GROUP):
        sl = slice(g * gw, (g + 1) * gw)
        q_all = jnp.where(on_head, jnp.concatenate([q[:, sl]] * NA_GROUP, axis=0), jnp.zeros((), q.dtype))
        bias = bias_ref[0, g * NA_GROUP * GRID_W:(g + 1) * NA_GROUP * GRID_W, :]
        o_all = _softmax_av(q_all, [kw[:, sl], kc[:, sl]], [vw[:, sl], vc[:, sl]], [bias, None])
        o_all = jnp.where(on_head, o_all, 0.0).reshape(NA_GROUP, GRID_W, gw)
        outs.append(o_all.sum(axis=0))
    return jnp.concatenate(outs, axis=-1)


def _na_bias_table(rpb, rows, kr):
    col = np.arange(GRID_W)
    col_start = np.clip(col - NA_WIN_C // 2, 0, GRID_W - NA_WIN_C)
    col_ok = (col[None, :] >= col_start[:, None]) & (col[None, :] < col_start[:, None] + NA_WIN_C)
    d_col = np.clip(col[None, :] - col[:, None], -(NA_WIN_C - 1), NA_WIN_C - 1) + NA_WIN_C - 1
    n_dr, n_dc = rpb.shape[1], rpb.shape[2]
    onehot = jnp.asarray((d_col.reshape(-1)[None, :] == np.arange(n_dc)[:, None]).astype(np.float32))
    by_col = jnp.dot(rpb.astype(F32).reshape(NA_HEADS * n_dr, n_dc), onehot, precision=lax.Precision.HIGHEST)
    by_col = by_col.reshape(NA_HEADS, n_dr, GRID_W, GRID_W)
    by_col = jnp.where(col_ok[None, None], by_col, NEG_INF)
    tables = []
    for o in range(kr):
        lo = NA_WIN_R - 1 - o
        tables.append(by_col[:, lo:lo + kr].transpose(0, 2, 1, 3).reshape(NA_HEADS, GRID_W, kr * GRID_W))
    return jnp.stack(tables).reshape(kr, NA_HEADS * GRID_W, kr * GRID_W)


def _na_latent(main, main_ctx, rpb):
    b, s, _ = main.shape
    sc = main_ctx.shape[1]
    rows = s // GRID_W
    kr = min(NA_WIN_R, rows)
    bias = _na_bias_table(rpb, rows, kr)
    w = NA_WIDTH

    per_step = next(n for n in (4, 2, 1) if rows % n == 0)

    def bias_spec(j):
        def bias_map(bi, i):
            r = i * per_step + j
            return (r - jnp.clip(r - kr // 2, 0, rows - kr), 0, 0)
        return pl.BlockSpec((1, NA_HEADS * GRID_W, kr * GRID_W), bias_map)

    return pl.pallas_call(
        functools.partial(_na_kernel, rows=rows, kr=kr),
        grid=(b, rows // per_step),
        in_specs=[pl.BlockSpec((1, per_step * GRID_W, w), lambda bi, i: (bi, i, OFF_Q_NA // w)),
                  pl.BlockSpec((1, s, w), lambda bi, i: (bi, 0, OFF_K_NA // w)),
                  pl.BlockSpec((1, s, w), lambda bi, i: (bi, 0, OFF_V_NA // w)),
                  pl.BlockSpec((1, sc, w), lambda bi, i: (bi, 0, OFF_K_NA // w)),
                  pl.BlockSpec((1, sc, w), lambda bi, i: (bi, 0, OFF_V_NA // w))]
                 + [bias_spec(j) for j in range(per_step)],
        out_specs=pl.BlockSpec((1, per_step * GRID_W, w), lambda bi, i: (bi, i, 0)),
        out_shape=jax.ShapeDtypeStruct((b, s, w), BF16),
        compiler_params=_cparams(("parallel", "arbitrary")),
        name="na_latent",
    )(main, main, main, main_ctx, main_ctx, *([bias] * per_step))


def _dense_attn_kernel(q_ref, k_ref, v_ref, o_ref):
    q = q_ref[0]
    k = k_ref[0]
    v = v_ref[0]
    outs = []
    for h in range(NA_HEADS):
        sl = slice(h * NA_HEAD_DIM, (h + 1) * NA_HEAD_DIM)
        outs.append(_softmax_av(q[:, sl], [k[:, sl]], [v[:, sl]], [None]))
    o_ref[0] = jnp.concatenate(outs, axis=-1).astype(o_ref.dtype)


def _dense_attn(main_ctx):
    b, sc, _ = main_ctx.shape
    w = NA_WIDTH
    return pl.pallas_call(
        _dense_attn_kernel,
        grid=(b,),
        in_specs=[pl.BlockSpec((1, sc, w), lambda bi: (bi, 0, OFF_Q_NA // w)),
                  pl.BlockSpec((1, sc, w), lambda bi: (bi, 0, OFF_K_NA // w)),
                  pl.BlockSpec((1, sc, w), lambda bi: (bi, 0, OFF_V_NA // w))],
        out_specs=pl.BlockSpec((1, sc, w), lambda bi: (bi, 0, 0)),
        out_shape=jax.ShapeDtypeStruct((b, sc, w), BF16),
        compiler_params=_cparams(("parallel",)),
        name="ctx_attn",
    )(main_ctx, main_ctx, main_ctx)


HALO = 16


def _gla_matrices(reverse):
    c = GLA_C
    t = np.arange(c)[:, None]
    m = np.arange(c)[None, :]
    blocks = [m <= t, m > t]
    for b in GLA_LEVELS:
        first = (t // (2 * b)) * (2 * b) + b
        is_q = (t & b) != 0
        blocks.append(np.where(is_q, (m > first) & (m <= t), (m > t) & (m <= first)))
    mats = np.stack(blocks).astype(np.float32)
    if reverse:
        mats = mats[:, ::-1, ::-1]
    mats = mats.reshape(-1, c)
    return jnp.asarray(np.concatenate([mats, mats], axis=1), dtype=BF16)


def _split_bf16(x):
    hi = x.astype(BF16)
    return hi, (x - hi.astype(F32)).astype(BF16)


def _pair_block_diag(x):
    lane = lax.broadcasted_iota(jnp.int32, x.shape, 1)
    zero = jnp.zeros((), x.dtype)
    return jnp.concatenate([jnp.where(lane < GLA_DK, x, zero), jnp.where(lane >= GLA_DK, x, zero)], axis=0)


def _gla_kernel(*refs, emit):
    n_in = 9 if emit else 8
    n_out = 2 if emit else 1
    ins, outs, scratch = refs[:2 * n_in], refs[2 * n_in:2 * (n_in + n_out)], refs[2 * (n_in + n_out):]
    ins = [ins[d * n_in:(d + 1) * n_in] for d in range(2)]
    outs = [outs[d * n_out:(d + 1) * n_out] for d in range(2)]
    step = pl.program_id(1)

    @pl.when(step == 0)
    def _():
        for d in range(2):
            scratch[d][...] = ins[d][-1][0]

    pending = [_gla_direction(ins[d][:-1], outs[d][:-1], scratch[d], reverse, emit)
               for d, reverse in enumerate((False, True))]
    while pending:
        pending = [stages for stages in pending if next(stages, "done") != "done"]

    @pl.when(step == pl.num_programs(1) - 1)
    def _():
        for d in range(2):
            outs[d][-1][0] = scratch[d][...]


def _gla_direction(ins, outs, st_ref, reverse, emit):
    if emit:
        q_ref, k_ref, v_ref, gt_ref, w2a_ref, w2b_ref, b2_ref, a_ref = ins
        (o_ref,) = outs
    else:
        k_ref, v_ref, gt_ref, w2a_ref, w2b_ref, b2_ref, a_ref = ins
    c = GLA_C
    pw = 2 * GLA_DK

    lr_hi, lr_lo = _split_bf16(gt_ref[0])
    logit = (_dot(jnp.concatenate([lr_hi, lr_lo], axis=1), w2a_ref[...]) + _dot(lr_hi, w2b_ref[...])
             + b2_ref[...])
    g = (jnp.minimum(logit, 0.0) - jnp.log1p(jnp.exp(-jnp.abs(logit)))) * (LOG2_E / GLA_GATE_TAU)
    g_hi, g_lo = _split_bf16(g)
    args = _dot(a_ref[...], jnp.concatenate([g_hi, g_lo], axis=0))
    cum = args[0:c]
    rem = args[c:2 * c]
    last_row = cum[0:1] if reverse else cum[c - 1:c]
    yield

    k = k_ref[0].astype(F32)
    v = v_ref[0]
    atts = []
    if emit:
        q = q_ref[0].astype(F32) * (GLA_DK ** -0.5)
        row_t = lax.broadcasted_iota(jnp.int32, (c, pw), 0)
        si = lax.broadcasted_iota(jnp.int32, (2 * c, c), 0) & (c - 1)
        ti = lax.broadcasted_iota(jnp.int32, (2 * c, c), 1)
        if reverse:
            row_t, ti, si = c - 1 - row_t, c - 1 - ti, c - 1 - si
        for hp in range(GLA_HEADS // 2):
            cs = slice(hp * pw, (hp + 1) * pw)
            qp, kp = q[:, cs], k[:, cs]
            att = jnp.where(ti == si, _dot_nt(_pair_block_diag(kp.astype(BF16)), qp.astype(BF16)), 0.0)
            for l, b in enumerate(GLA_LEVELS):
                x = (jnp.exp2(args[(2 + l) * c:(3 + l) * c, cs])
                     * jnp.where((row_t & b) != 0, qp, kp)).astype(BF16)
                pair = (((ti ^ si) >> (b.bit_length() - 1)) == 1) & ((ti & b) != 0)
                att = jnp.where(pair, _dot_nt(_pair_block_diag(x), x), att)
                yield
            atts.append(att.astype(BF16))

    outs = []
    for h in range(GLA_HEADS):
        sl = slice(h * GLA_DK, (h + 1) * GLA_DK)
        kh = k[:, sl]
        vh = v[:, h * GLA_DV:(h + 1) * GLA_DV]
        state = st_ref[h]
        kd = (kh * jnp.exp2(rem[:, sl])).astype(BF16)
        decay = jnp.exp2(jnp.broadcast_to(last_row[:, sl], (GLA_DK, GLA_DK))).T
        decay = jnp.concatenate([decay] * (GLA_DV // GLA_DK), axis=1)
        if emit:
            qd = (q[:, sl] * jnp.exp2(cum[:, sl])).astype(BF16)
            att_t = atts[h // 2][(h % 2) * c:(h % 2 + 1) * c]
            both = _dot_tn(jnp.concatenate([att_t, kd], axis=1), vh)
            outs.append(_dot(qd, state.astype(BF16)) + both[:c])
            st_ref[h] = decay * state + both[c:]
        else:
            st_ref[h] = decay * state + _dot_tn(kd, vh)
        yield

    if emit:
        o_ref[0] = jnp.concatenate(outs, axis=-1).astype(o_ref.dtype)


def _gla_scan(main, gate, gate_ws, s0s, emit):
    b, l, _ = main.shape
    n = l // GLA_C
    const = lambda arr: pl.BlockSpec(arr.shape, lambda bi, s: (0,) * arr.ndim)
    state_spec = pl.BlockSpec((1, GLA_HEADS, GLA_DK, GLA_DV), lambda bi, s: (bi, 0, 0, 0))
    state_shape = jax.ShapeDtypeStruct((b, GLA_HEADS, GLA_DK, GLA_DV), F32)
    in_specs, args, out_specs, out_shape = [], [], [], []
    for reverse in (False, True):
        amat = _gla_matrices(reverse)
        w2a, w2b, b2 = gate_ws[reverse]

        def col(block, reverse=reverse):
            return lambda bi, s: (bi, n - 1 - s if reverse else s, block)

        if emit:
            in_specs.append(pl.BlockSpec((1, GLA_C, GLA_KEY_WIDTH), col(OFF_Q_GLA // GLA_KEY_WIDTH)))
            args.append(main)
            out_specs.append(pl.BlockSpec((1, GLA_C, GLA_VAL_WIDTH), col(0)))
            out_shape.append(jax.ShapeDtypeStruct((b, l, GLA_VAL_WIDTH), BF16))
        in_specs += [pl.BlockSpec((1, GLA_C, GLA_KEY_WIDTH), col(OFF_K_GLA // GLA_KEY_WIDTH)),
                     pl.BlockSpec((1, GLA_C, GLA_VAL_WIDTH), col(OFF_V_GLA // GLA_VAL_WIDTH)),
                     pl.BlockSpec((1, GLA_C, LANE), col(0)),
                     const(w2a), const(w2b), const(b2), const(amat), state_spec]
        args += [main, main, gate, w2a, w2b, b2, amat, s0s[reverse]]
        out_specs.append(state_spec)
        out_shape.append(state_shape)
    res = pl.pallas_call(
        functools.partial(_gla_kernel, emit=emit),
        grid=(b, n),
        in_specs=in_specs,
        out_specs=out_specs,
        out_shape=out_shape,
        scratch_shapes=[pltpu.VMEM((GLA_HEADS, GLA_DK, GLA_DV), F32)] * 2,
        compiler_params=_cparams(("parallel", "arbitrary")),
        name="gla_scan",
    )(*args)
    return (res[0], res[2], res[1], res[3]) if emit else (None, None, res[0], res[1])


def _gla_gate_weights(gate_w, gate_b):
    out = []
    for dr in range(2):
        w = jnp.zeros((LANE, GLA_KEY_WIDTH), F32)
        w = w.at[dr * GLA_GATE_RANK:(dr + 1) * GLA_GATE_RANK].set(gate_w[dr])
        w_hi = w.astype(BF16)
        w_lo = (w - w_hi.astype(F32)).astype(BF16)
        out.append((jnp.concatenate([w_hi, w_hi], axis=0), w_lo, gate_b[dr][None]))
    return out


def _conv_tile(b_ref, c_ref, x_ref, cp_ref, xp_ref, cn_ref, xn_ref, w_ref):
    i, n = pl.program_id(1), pl.num_programs(1)
    u = c_ref[0].astype(F32) * x_ref[0].astype(F32)
    tm = u.shape[0]
    before = jnp.where(i > 0, 1.0, 0.0) * (cp_ref[0, HALO - 1:HALO].astype(F32) * xp_ref[0, HALO - 1:HALO].astype(F32))
    after = jnp.where(i < n - 1, 1.0, 0.0) * (cn_ref[0, 0:1].astype(F32) * xn_ref[0, 0:1].astype(F32))
    t = lax.broadcasted_iota(jnp.int32, u.shape, 0)
    prev = jnp.where(t == 0, before, pltpu.roll(u, 1, axis=0))
    nxt = jnp.where(t == tm - 1, after, pltpu.roll(u, tm - 1, axis=0))
    w = w_ref[...]
    return b_ref[0].astype(F32) * (prev * w[0:1] + u * w[1:2] + nxt * w[2:3])


def _merge_kernel(ona_ref, bsc_ref, csc_ref, xsc_ref, cp_ref, xp_ref, cn_ref, xn_ref, cw_ref, of_ref, ob_ref,
                  r_ref, gna_ref, gsc_ref, ggl_ref, x_ref, gt_ref,
                  gn_ref, wna_ref, wsc_ref, wgl_ref, wo_ref, g2_ref, sh2_ref, sc2_ref, wr_ref,
                  xo_ref, h2_ref, hp_ref, lg_ref):
    o_sc = _conv_tile(bsc_ref, csc_ref, xsc_ref, cp_ref, xp_ref, cn_ref, xn_ref, cw_ref).astype(BF16)
    o = of_ref[0].astype(F32) + ob_ref[0].astype(F32)
    normed = []
    for h in range(GLA_HEADS):
        oh = o[:, h * GLA_DV:(h + 1) * GLA_DV]
        ms = jnp.mean(oh * oh, axis=-1, keepdims=True)
        normed.append(oh * lax.rsqrt(ms + RMS_EPS))
    r = r_ref[0].astype(F32)
    y_gla = jnp.concatenate(normed, axis=-1) * gn_ref[...] * (r * _sigmoid(r))
    y = (_sigmoid(gna_ref[0].astype(F32)) * _dot(ona_ref[0], wna_ref[...])
         + _sigmoid(gsc_ref[0].astype(F32)) * _dot(o_sc, wsc_ref[...])
         + _sigmoid(ggl_ref[0].astype(F32)) * _dot(y_gla.astype(BF16), wgl_ref[...]))
    xn = x_ref[0] + gt_ref[0] * _dot(y.astype(BF16), wo_ref[...])
    xo_ref[0] = xn
    ms = jnp.mean(xn * xn, axis=-1, keepdims=True)
    h2 = xn * lax.rsqrt(ms + RMS_EPS) * g2_ref[...] * (1.0 + sc2_ref[0]) + sh2_ref[0]
    h2b = h2.astype(BF16)
    h2_ref[0] = h2b
    _store_parts(hp_ref, _pack_rows(h2))
    lg_ref[0] = _dot(h2b, wr_ref[...])


def _merge(o_na, conv_w, o_f, o_b, main, x, gt1, gn, w_na, w_sc, w_gla, w_out, g2, sh2, sc2, w_router, tm):
    b, s, d = x.shape
    per_batch = gt1.shape[0] == b
    mod_map = (lambda bi, i: (bi, 0, 0)) if per_batch else (lambda bi, i: (0, 0, 0))
    tok = lambda width, blk: pl.BlockSpec((1, tm, width), lambda bi, i: (bi, i, blk))
    full = lambda arr: pl.BlockSpec(arr.shape, lambda bi, i: (0,) * arr.ndim)
    mod = pl.BlockSpec((1, 1, d), mod_map)
    gn_t = jnp.tile(gn, GLA_HEADS)[None]
    g2_t = g2[None]
    per_tile = tm // HALO
    last_halo = s // HALO - 1
    halo_prev = lambda blk: pl.BlockSpec(
        (1, HALO, SC_WIDTH), lambda bi, i: (bi, jnp.maximum(i * per_tile - 1, 0), blk))
    halo_next = lambda blk: pl.BlockSpec(
        (1, HALO, SC_WIDTH), lambda bi, i: (bi, jnp.minimum((i + 1) * per_tile, last_halo), blk))
    col_b, col_c, col_x = OFF_B_SC // SC_WIDTH, OFF_C_SC // SC_WIDTH, OFF_X_SC // SC_WIDTH
    return pl.pallas_call(
        _merge_kernel,
        grid=(b, s // tm),
        in_specs=[tok(NA_WIDTH, 0), tok(SC_WIDTH, col_b), tok(SC_WIDTH, col_c), tok(SC_WIDTH, col_x),
                  halo_prev(col_c), halo_prev(col_x), halo_next(col_c), halo_next(col_x), full(conv_w),
                  tok(GLA_VAL_WIDTH, 0), tok(GLA_VAL_WIDTH, 0),
                  tok(d, OFF_R_GLA // d), tok(d, OFF_MERGE // d), tok(d, OFF_MERGE // d + 1),
                  tok(d, OFF_MERGE // d + 2), tok(d, 0), mod,
                  full(gn_t), full(w_na), full(w_sc), full(w_gla), full(w_out), full(g2_t), mod, mod,
                  full(w_router)],
        out_specs=[tok(d, 0), tok(d, 0),
                   pl.BlockSpec((SC_PARTS, 1, tm, SC_ROW), lambda bi, i: (0, bi, i, 0)), tok(LANE, 0)],
        out_shape=[jax.ShapeDtypeStruct((b, s, d), F32),
                   jax.ShapeDtypeStruct((b, s, d), BF16),
                   jax.ShapeDtypeStruct((SC_PARTS, b, s, SC_ROW), U32),
                   jax.ShapeDtypeStruct((b, s, LANE), F32)],
        compiler_params=_cparams(("parallel", "parallel")),
        name="merge",
    )(o_na, main, main, main, main, main, main, main, conv_w, o_f, o_b, main, main, main, main, x, gt1,
      gn_t, w_na, w_sc, w_gla, w_out, g2_t, sh2, sc2, w_router)


def _router_kernel(lg_ref, br_ref, tri_ref, eidx_ref, gw_ref, rank_ref, cnt_ref, carry_ref):
    tm = lg_ref.shape[0]

    @pl.when(pl.program_id(0) == 0)
    def _():
        carry_ref[...] = jnp.zeros_like(carry_ref)

    scores = _sigmoid(lg_ref[...].T[:N_EXPERTS])
    sel = scores + br_ref[...]
    neg = -jnp.inf

    sel3 = sel.reshape(N_EXPERT_GROUPS, GROUP_SIZE, tm)
    i3 = lax.broadcasted_iota(jnp.int32, sel3.shape, 1)
    m1 = sel3.max(axis=1, keepdims=True)
    first = jnp.where(sel3 == m1, i3, GROUP_SIZE).min(axis=1, keepdims=True)
    m2 = jnp.where(i3 == first, neg, sel3).max(axis=1, keepdims=True)
    gscore = (m1 + m2)[:, 0, :]

    gi = lax.broadcasted_iota(jnp.int32, gscore.shape, 0)
    gmask = jnp.zeros(gscore.shape, jnp.bool_)
    for _ in range(TOPK_GROUPS):
        m = gscore.max(axis=0, keepdims=True)
        pick = gi == jnp.where(gscore == m, gi, N_EXPERT_GROUPS).min(axis=0, keepdims=True)
        gmask = gmask | pick
        gscore = jnp.where(pick, neg, gscore)
    emask = jnp.broadcast_to(gmask[:, None, :], sel3.shape).reshape(N_EXPERTS, tm)
    sel = jnp.where(emask, sel, neg)

    ei = lax.broadcasted_iota(jnp.int32, sel.shape, 0)
    picks, idxs, ws = [], [], []
    for _ in range(TOP_K):
        m = sel.max(axis=0, keepdims=True)
        idx = jnp.where(sel == m, ei, N_EXPERTS).min(axis=0, keepdims=True)
        pick = ei == idx
        picks.append(pick)
        idxs.append(idx)
        ws.append(jnp.where(pick, scores, 0.0).sum(axis=0, keepdims=True))
        sel = jnp.where(pick, neg, sel)
    w = jnp.concatenate(ws, axis=0)
    gw_ref[...] = w / w.sum(axis=0, keepdims=True) * ROUTED_SCALE
    eidx_ref[...] = jnp.concatenate(idxs, axis=0)

    onehot = picks[0]
    for p in picks[1:]:
        onehot = onehot | p
    onehot = jnp.where(onehot, 1.0, 0.0).astype(BF16)
    before = _dot(onehot, tri_ref[...]) + jnp.tile(carry_ref[...], (1, tm // LANE))
    rank_ref[...] = jnp.concatenate(
        [jnp.where(p, before, 0.0).sum(axis=0, keepdims=True) for p in picks], axis=0).astype(jnp.int32)
    carry_ref[...] += _dot(onehot, jnp.ones((tm, LANE), BF16))
    cnt_ref[...] = carry_ref[...]


def _route(logits, b_router, tm=512):
    t = logits.shape[0]
    br = jnp.broadcast_to(b_router.astype(F32)[:, None], (N_EXPERTS, tm))
    tri = jnp.asarray(np.triu(np.ones((tm, tm), np.float32), 1), dtype=BF16)
    kt = lambda dt: jax.ShapeDtypeStruct((TOP_K, t), dt)
    eidx, gw, rank, cnt = pl.pallas_call(
        _router_kernel,
        grid=(t // tm,),
        in_specs=[pl.BlockSpec((tm, LANE), lambda i: (i, 0)),
                  pl.BlockSpec((N_EXPERTS, tm), lambda i: (0, 0)),
                  pl.BlockSpec((tm, tm), lambda i: (0, 0))],
        out_specs=[pl.BlockSpec((TOP_K, tm), lambda i: (0, i)),
                   pl.BlockSpec((TOP_K, tm), lambda i: (0, i)),
                   pl.BlockSpec((TOP_K, tm), lambda i: (0, i)),
                   pl.BlockSpec((N_EXPERTS, LANE), lambda i: (0, 0))],
        out_shape=[kt(jnp.int32), kt(F32), kt(jnp.int32),
                   jax.ShapeDtypeStruct((N_EXPERTS, LANE), F32)],
        scratch_shapes=[pltpu.VMEM((N_EXPERTS, LANE), F32)],
        compiler_params=_cparams(("arbitrary",)),
        name="router",
    )(logits, br, tri)
    return eidx, gw, rank, cnt[:, 0].astype(jnp.int32)


def _sc_mesh():
    return plsc.VectorSubcoreMesh(core_axis_name="core", subcore_axis_name="subcore")


def _dispatch_rows(xp, dest, slots):
    parts, t, _ = xp.shape
    nwin = parts * t // SC_WINDOW
    idx = dest.reshape(TOP_K, t // SC_WINDOW, SC_WINDOW).transpose(1, 0, 2)
    idx = jnp.concatenate([idx + part * slots for part in range(parts)], axis=0)

    @pl.kernel(out_type=jax.ShapeDtypeStruct((parts * slots, SC_ROW), xp.dtype), mesh=_sc_mesh(),
               scratch_types=[], name="moe_dispatch")
    def run(x_hbm, i_hbm, o_hbm):
        def body(x_vmem, i_vmem):
            for k in range(TOP_K):
                pltpu.sync_copy(x_vmem, o_hbm.at[i_vmem.at[0, k]])

        pltpu.emit_pipeline(
            body,
            grid=(nwin,),
            in_specs=[pl.BlockSpec((SC_WINDOW, SC_ROW), lambda i: (i, 0)),
                      pl.BlockSpec((1, TOP_K, SC_WINDOW), lambda i: (i, 0, 0))],
            out_specs=[],
            core_axis_name=("core", "subcore"),
            dimension_semantics=(pltpu.PARALLEL,),
        )(x_hbm, i_hbm)

    return run(xp.reshape(parts * t, SC_ROW), idx).reshape(parts, slots, SC_ROW)


def _gather_rows(yp, idx):
    n = idx.shape[0]
    parts, slots, _ = yp.shape
    idx = jnp.concatenate([idx + part * slots for part in range(parts)]).reshape(1, n * parts)

    @pl.kernel(out_type=jax.ShapeDtypeStruct((n * parts, SC_ROW), yp.dtype), mesh=_sc_mesh(),
               scratch_types=[], name="moe_gather")
    def run(y_hbm, i_hbm, o_hbm):
        def body(i_vmem, o_vmem):
            pltpu.sync_copy(y_hbm.at[i_vmem.at[0]], o_vmem)

        pltpu.emit_pipeline(
            body,
            grid=(n * parts // SC_WINDOW,),
            in_specs=[pl.BlockSpec((1, SC_WINDOW), lambda i: (0, i))],
            out_specs=[pl.BlockSpec((SC_WINDOW, SC_ROW), lambda i: (i, 0))],
            core_axis_name=("core", "subcore"),
            dimension_semantics=(pltpu.PARALLEL,),
        )(i_hbm, o_hbm)

    return run(yp.reshape(parts * slots, SC_ROW), idx).reshape(parts, n, SC_ROW)


def _expert_kernel(be_ref, bv_ref, bs_ref, x_ref, wg_ref, wu_ref, wd_ref, o_ref, wg_s, wu_s, wd_s):
    i = pl.program_id(0)
    valid = bv_ref[i]
    new_expert = (i == 0) | (be_ref[i] != be_ref[jnp.maximum(i - 1, 0)])

    @pl.when(new_expert)
    def _():
        wg_s[...] = wg_ref[0, 0].astype(BF16)
        wu_s[...] = wu_ref[0, 0].astype(BF16)
        wd_s[...] = wd_ref[0, 0].astype(BF16)

    def compute(rows):
        w = jnp.concatenate([x_ref[part, 0:rows] for part in range(SC_PARTS)], axis=-1)
        row = lax.broadcasted_iota(jnp.int32, w.shape, 0)
        w = jnp.where(row < valid, w, jnp.uint32(0))
        lo, hi = _unpack_rows(w)
        x = jnp.concatenate([lo, hi], axis=1).astype(BF16)
        a = _dot(x, wg_s[...])
        hid = a * _sigmoid(a) * _dot(x, wu_s[...])
        y = _pack_rows(_dot(hid.astype(BF16), wd_s[...]))
        for part in range(SC_PARTS):
            o_ref[part, 0:rows] = y[:, part * SC_ROW:(part + 1) * SC_ROW]

    for rows in range(MOE_SUB, MOE_BLOCK + 1, MOE_SUB):
        pl.when((valid > rows - MOE_SUB) & (valid <= rows))(functools.partial(compute, rows))


def _experts(xs, blk_e, blk_valid, blk_src, layer, w_gate, w_up, w_down):
    parts, slots, _ = xs.shape
    d = D_MODEL
    nb = slots // MOE_BLOCK
    return pl.pallas_call(
        _expert_kernel,
        grid_spec=pltpu.PrefetchScalarGridSpec(
            num_scalar_prefetch=3,
            grid=(nb,),
            in_specs=[pl.BlockSpec((parts, MOE_BLOCK, SC_ROW), lambda i, be, bv, bs: (0, bs[i], 0)),
                      pl.BlockSpec((1, 1, d, EXPERT_FF), lambda i, be, bv, bs: (layer, be[i], 0, 0)),
                      pl.BlockSpec((1, 1, d, EXPERT_FF), lambda i, be, bv, bs: (layer, be[i], 0, 0)),
                      pl.BlockSpec((1, 1, EXPERT_FF, d), lambda i, be, bv, bs: (layer, be[i], 0, 0))],
            out_specs=pl.BlockSpec((parts, MOE_BLOCK, SC_ROW), lambda i, be, bv, bs: (0, bs[i], 0)),
            scratch_shapes=[pltpu.VMEM((d, EXPERT_FF), BF16), pltpu.VMEM((d, EXPERT_FF), BF16),
                            pltpu.VMEM((EXPERT_FF, d), BF16)]),
        out_shape=jax.ShapeDtypeStruct((parts, slots, SC_ROW), U32),
        compiler_params=_cparams(("arbitrary",)),
        name="experts",
    )(blk_e, blk_valid, blk_src, xs, w_gate, w_up, w_down)


def _combine_kernel(yg_ref, gw_ref, h_ref, x_ref, gt_ref, wsg_ref, wsu_ref, wsd_ref, gf_ref, o_ref, *, final):
    h = h_ref[0]
    a = _dot(h, wsg_ref[...])
    hid = a * _sigmoid(a) * _dot(h, wsu_ref[...])
    y = _dot(hid.astype(BF16), wsd_ref[...])
    gw = gw_ref[...]
    y_lo = y[:, :D_MODEL // 2]
    y_hi = y[:, D_MODEL // 2:]
    for k in range(TOP_K):
        lo, hi = _unpack_rows(_load_parts(yg_ref, k))
        y_lo = y_lo + gw[:, k:k + 1] * lo
        y_hi = y_hi + gw[:, k:k + 1] * hi
    y = jnp.concatenate([y_lo, y_hi], axis=1)
    xn = x_ref[0] + gt_ref[0] * y
    if final:
        ms = jnp.mean(xn * xn, axis=-1, keepdims=True)
        xn = xn * lax.rsqrt(ms + RMS_EPS) * gf_ref[...]
    o_ref[0] = xn


def _combine(yg, gw, tok_off, h2, x, gt2, b0, nb, ws_gate, ws_up, ws_down, g_final, final, tm):
    b, s, d = x.shape
    per_batch = gt2.shape[0] == b
    mod_map = (lambda bi, i: (b0 + bi, 0, 0)) if per_batch else (lambda bi, i: (0, 0, 0))
    full = lambda arr: pl.BlockSpec(arr.shape, lambda bi, i: (0,) * arr.ndim)
    tok = lambda width: pl.BlockSpec((1, tm, width), lambda bi, i: (b0 + bi, i, 0))
    gf = g_final[None]
    nblk = s // tm
    blk0 = (tok_off + b0 * s) // tm
    return pl.pallas_call(
        functools.partial(_combine_kernel, final=final),
        grid=(nb, nblk),
        in_specs=[pl.BlockSpec((SC_PARTS, TOP_K, tm, SC_ROW), lambda bi, i: (0, 0, bi * nblk + i, 0)),
                  pl.BlockSpec((tm, TOP_K), lambda bi, i: (blk0 + bi * nblk + i, 0)),
                  tok(d), tok(d),
                  pl.BlockSpec((1, 1, d), mod_map),
                  full(ws_gate), full(ws_up), full(ws_down), full(gf)],
        out_specs=tok(d),
        out_shape=jax.ShapeDtypeStruct((b, s, d), F32),
        input_output_aliases={3: 0},
        compiler_params=_cparams(("parallel", "parallel")),
        name="combine",
    )(yg, gw, h2, x, gt2, ws_gate, ws_up, ws_down, gf)


def _layer(x, ctx_s, mods, mods_ctx, p, ctx_out, final, g_final):
    b, s, d = x.shape
    sc = ctx_s.shape[1]
    sh1, sc1, gt1, sh2, sc2, gt2 = mods
    csh1, csc1, cgt1, csh2, csc2, cgt2 = mods_ctx

    w_main, w_gate = _prep_w_in(p['w_in'], p['layer'])
    main, gate = _proj_in(x, p['g_norm1'], sh1, sc1, w_main, w_gate, tm=min(2048, s), tn=1024)
    ctx_flat = ctx_s.reshape(1, b * sc, d)
    n_ctx, tn_ctx = (N_MAIN, 1024) if ctx_out else (N_KV_MAIN, N_KV_MAIN // 2)
    main_c, gate_c = _proj_in(ctx_flat, p['g_norm1'], csh1, csc1, w_main, w_gate, tm=min(1024, b * sc),
                              tn=tn_ctx, n=n_ctx)
    main_c = main_c.reshape(b, sc, n_ctx)
    gate_c = gate_c.reshape(b, sc, LANE)

    o_na = _na_latent(main, main_c, p['na_rpb'])

    gate_ws = _gla_gate_weights(p['gla_gate_w'], p['gla_gate_b'])
    s0 = jnp.zeros((b, GLA_HEADS, GLA_DK, GLA_DV), F32)
    o_cf, o_cb, st_f, st_b = _gla_scan(main_c, gate_c, gate_ws, (s0, s0), ctx_out)
    o_f, o_b, _, _ = _gla_scan(main, gate, gate_ws, (st_f, st_b), True)

    w_na = p['w_branch_na'].astype(BF16)
    w_sc = p['w_branch_sc'].astype(BF16)
    w_gla = p['w_branch_gla'].astype(BF16)
    w_out = p['w_out'].astype(BF16)
    w_router = jnp.pad(p['w_router'], ((0, 0), (0, LANE - N_EXPERTS))).astype(BF16)
    x, h2, h2p, logits = _merge(o_na, p['conv_w'], o_f, o_b, main, x, gt1, p['gla_norm_g'], w_na, w_sc, w_gla, w_out,
                                p['g_norm2'], sh2, sc2, w_router, tm=min(512, s))
    n_lat = b * s
    hp_all = h2p.reshape(SC_PARTS, n_lat, SC_ROW)
    lg_all = logits.reshape(n_lat, LANE)
    if ctx_out:
        o_na_c = _dense_attn(main_c)
        ctx_s, h2_c, h2p_c, lg_c = _merge(o_na_c, p['conv_w'], o_cf, o_cb, main_c, ctx_s, cgt1, p['gla_norm_g'],
                                          w_na, w_sc, w_gla, w_out, p['g_norm2'], csh2, csc2, w_router,
                                          tm=min(256, sc))
        hp_all = jnp.concatenate([hp_all, h2p_c.reshape(SC_PARTS, b * sc, SC_ROW)], axis=1)
        lg_all = jnp.concatenate([lg_all, lg_c.reshape(b * sc, LANE)], axis=0)

    t = hp_all.shape[1]
    eidx, gw, rank, counts = _route(lg_all, p['b_router'])
    padded = (counts + MOE_BLOCK - 1) // MOE_BLOCK * MOE_BLOCK
    pad_end = jnp.cumsum(padded)
    pad_start = pad_end - padded
    onehot = eidx[:, :, None] == jnp.arange(N_EXPERTS, dtype=jnp.int32)
    dest = jnp.sum(jnp.where(onehot, pad_start, 0), axis=-1) + rank
    n_blocks = -(-(t * TOP_K + N_EXPERTS * (MOE_BLOCK - 1)) // MOE_BLOCK)
    slots = n_blocks * MOE_BLOCK
    blk_start = jnp.arange(n_blocks, dtype=jnp.int32) * MOE_BLOCK
    blk_e = jnp.minimum(jnp.sum(pad_end[None, :] <= blk_start[:, None], axis=1), N_EXPERTS - 1).astype(jnp.int32)
    used_end = (pad_start + counts)[blk_e]
    blk_valid = jnp.clip(used_end - blk_start, 0, MOE_BLOCK).astype(jnp.int32)
    n_used = pad_end[-1] // MOE_BLOCK
    blk_src = jnp.minimum(jnp.arange(n_blocks, dtype=jnp.int32), n_used - 1)
    blk_e = blk_e[blk_src]

    xs = _dispatch_rows(hp_all, dest, slots)
    ys = _experts(xs, blk_e, blk_valid, blk_src, p['layer'], p['w_exp_gate'], p['w_exp_up'], p['w_exp_down'])
    gw_t = gw.T
    ws_gate = p['w_sh_gate'].astype(BF16)
    ws_up = p['w_sh_up'].astype(BF16)
    ws_down = p['w_sh_down'].astype(BF16)

    def gathered(t0, n):
        return _gather_rows(ys, dest[:, t0:t0 + n].reshape(-1)).reshape(SC_PARTS, TOP_K, n, SC_ROW)

    pieces = next(n for n in (4, 2, 1) if b % n == 0)
    nb = b // pieces
    for q in range(pieces):
        x = _combine(gathered(q * nb * s, nb * s), gw_t, 0, h2, x, gt2, q * nb, nb, ws_gate, ws_up, ws_down,
                     g_final, final, tm=min(256, s))
    if ctx_out:
        ctx_s = _combine(gathered(n_lat, b * sc), gw_t, n_lat, h2_c, ctx_s, cgt2, 0, b, ws_gate, ws_up, ws_down,
                         g_final, False, tm=min(256, sc))
    return x, ctx_s


def kernel(x, c, ctx, c_ctx, w_mod, b_mod, g_norm1, g_norm2, w_in, na_rpb, w_branch_na, conv_w, w_branch_sc,
           gla_gate_w, gla_gate_b, gla_norm_g, w_branch_gla, w_out, w_router, b_router, w_exp_gate, w_exp_up,
           w_exp_down, w_sh_gate, w_sh_up, w_sh_down, g_final):
    stacked = dict(g_norm1=g_norm1, g_norm2=g_norm2, na_rpb=na_rpb, w_branch_na=w_branch_na,
                   conv_w=conv_w, w_branch_sc=w_branch_sc, gla_gate_w=gla_gate_w, gla_gate_b=gla_gate_b,
                   gla_norm_g=gla_norm_g, w_branch_gla=w_branch_gla, w_out=w_out, w_router=w_router,
                   b_router=b_router,
                   w_sh_gate=w_sh_gate, w_sh_up=w_sh_up, w_sh_down=w_sh_down)
    depth = w_in.shape[0]
    ctx_s = ctx
    for i in range(depth):
        p = {name: arr[i] for name, arr in stacked.items()}
        p.update(layer=i, w_in=w_in, w_exp_gate=w_exp_gate, w_exp_up=w_exp_up, w_exp_down=w_exp_down)
        mods, mods_ctx = _mod_vectors(c, c_ctx, w_mod, b_mod, i)
        last = i == depth - 1
        x, ctx_s = _layer(x, ctx_s, mods, mods_ctx, p, not last, last, g_final)
    return x
```

```python
import functools

import numpy as np
import jax
import jax.numpy as jnp
from jax import lax
from jax.experimental import pallas as pl
from jax.experimental.pallas import tpu as pltpu
from jax.experimental.pallas import tpu_sc as plsc

F32 = jnp.float32
BF16 = jnp.bfloat16
U32 = jnp.uint32

D_MODEL = 1024
N_MOD = 6
RMS_EPS = 1e-6
NEG_INF = -1e30
GRID_W = 64
NA_HEADS = 8
NA_HEAD_DIM = 64
NA_WIDTH = NA_HEADS * NA_HEAD_DIM
NA_WIN_R = 8
NA_WIN_C = 16
NA_GROUP = 4
SC_WIDTH = 512
GLA_HEADS = 4
GLA_KEY_WIDTH = 512
GLA_VAL_WIDTH = 1024
GLA_DK = GLA_KEY_WIDTH // GLA_HEADS
GLA_DV = GLA_VAL_WIDTH // GLA_HEADS
GLA_GATE_RANK = 16
GLA_GATE_TAU = 16.0
LOG2_E = 1.4426950408889634
N_EXPERTS = 64
N_EXPERT_GROUPS = 8
GROUP_SIZE = N_EXPERTS // N_EXPERT_GROUPS
TOPK_GROUPS = 4
TOP_K = 8
EXPERT_FF = 256
ROUTED_SCALE = 2.5
MOE_BLOCK = 1024

LANE = 128
GLA_C = 128
GLA_LEVELS = tuple(GLA_C >> (i + 1) for i in range(GLA_C.bit_length() - 1))
VMEM_LIMIT = 48 * 1024 * 1024
SC_WINDOW = 128
SC_ROW = 256
SC_PARTS = D_MODEL // 2 // SC_ROW

OFF_V_GLA = 0
OFF_K_NA = 1024
OFF_V_NA = 1536
OFF_K_GLA = 2048
N_KV_MAIN = 2560
OFF_Q_NA = 2560
OFF_B_SC = 3072
OFF_C_SC = 3584
OFF_X_SC = 4096
OFF_Q_GLA = 4608
OFF_R_GLA = 5120
OFF_MERGE = 6144
N_MAIN = 9216


def _cparams(sem, vmem=VMEM_LIMIT):
    return pltpu.CompilerParams(dimension_semantics=sem, vmem_limit_bytes=vmem)


def _dot(a, b):
    return jnp.dot(a, b, preferred_element_type=F32)


def _dot_nt(a, b):
    return lax.dot_general(a, b, (((1,), (1,)), ((), ())), preferred_element_type=F32)


def _dot_tn(a, b):
    return lax.dot_general(a, b, (((0,), (0,)), ((), ())), preferred_element_type=F32)


def _sigmoid(x):
    return 0.5 * jnp.tanh(0.5 * x) + 0.5


def _pack_rows(x):
    n = x.shape[1] // 2
    r = x.astype(BF16).astype(F32)
    lo = pltpu.bitcast(r[:, :n], U32) >> 16
    hi = pltpu.bitcast(r[:, n:], U32)
    return hi | lo


def _store_parts(ref, words):
    for part in range(SC_PARTS):
        dst = ref.at[part, 0] if len(ref.shape) == 4 else ref.at[part]
        dst[...] = words[:, part * SC_ROW:(part + 1) * SC_ROW]


def _load_parts(ref, *lead):
    return jnp.concatenate([ref[(part,) + lead] for part in range(SC_PARTS)], axis=-1)


def _unpack_rows(w):
    lo = pltpu.bitcast(w << 16, F32)
    hi = pltpu.bitcast(w & jnp.uint32(0xFFFF0000), F32)
    return lo, hi


def _mod_kernel(a_ref, w_ref, b_ref, o_ref):
    a = a_ref[...]
    a = a * _sigmoid(a)
    o_ref[...] = _dot(a.astype(BF16), w_ref[0].astype(BF16)) + b_ref[0]


def _mod_vectors(c, c_ctx, w_mod, b_mod, layer):
    b = c.shape[0]
    rows = -(-(b + 1) // 8) * 8
    a = jnp.concatenate([c, c_ctx[None], jnp.zeros((rows - b - 1, D_MODEL), F32)], axis=0)
    n = N_MOD * D_MODEL
    tn = 1536
    out = pl.pallas_call(
        _mod_kernel,
        grid=(n // tn,),
        in_specs=[pl.BlockSpec((rows, D_MODEL), lambda j: (0, 0)),
                  pl.BlockSpec((1, D_MODEL, tn), lambda j: (layer, 0, j)),
                  pl.BlockSpec((1, 1, tn), lambda j: (layer, 0, j))],
        out_specs=pl.BlockSpec((rows, tn), lambda j: (0, j)),
        out_shape=jax.ShapeDtypeStruct((rows, n), F32),
        compiler_params=_cparams(("parallel",)),
        name="mod_vectors",
    )(a, w_mod, b_mod[:, None])
    lat = out[:b].reshape(b, N_MOD, 1, D_MODEL)
    ctx = out[b].reshape(N_MOD, 1, 1, D_MODEL)
    return [lat[:, i] for i in range(N_MOD)], [ctx[i] for i in range(N_MOD)]


def _proj_kernel(x_ref, g_ref, sh_ref, sc_ref, w_ref, wg_ref, *rest):
    o_ref, og_ref, h_ref = rest[-3:]

    @pl.when(pl.program_id(2) == 0)
    def _():
        x = x_ref[0]
        ms = jnp.mean(x * x, axis=-1, keepdims=True)
        h = x * lax.rsqrt(ms + RMS_EPS) * g_ref[...] * (1.0 + sc_ref[0]) + sh_ref[0]
        hb = h.astype(BF16)
        h_ref[...] = hb
        og_ref[0] = _dot(hb, wg_ref[...])

    o_ref[0] = _dot(h_ref[...], w_ref[...]).astype(o_ref.dtype)


W_IN_TILE = 512
W_IN_GATE_SHIFT = 2 * GLA_GATE_RANK


def _prep_w_in_kernel(a_ref, b_ref, o_ref, g_ref):
    t = pl.program_id(0)
    first_lat = N_KV_MAIN // W_IN_TILE
    a = a_ref[0]

    @pl.when(t < first_lat)
    def _():
        o_ref[...] = a.T.astype(BF16)

    @pl.when(t >= first_lat)
    def _():
        moved = jnp.concatenate([a[W_IN_GATE_SHIFT:], b_ref[0]], axis=0)
        scale = jnp.where(t == first_lat, NA_HEAD_DIM ** -0.5, 1.0)
        o_ref[...] = (moved * scale).T.astype(BF16)

    @pl.when(t == first_lat)
    def _():
        head = a[:LANE]
        row = lax.broadcasted_iota(jnp.int32, head.shape, 0)
        g_ref[...] = jnp.where(row < W_IN_GATE_SHIFT, head, 0.0).T.astype(BF16)


def _prep_w_in(w_in, layer):
    d = w_in.shape[1]
    w_t = jnp.swapaxes(w_in, 1, 2)
    first_lat = N_KV_MAIN // W_IN_TILE
    kv_perm = OFF_K_NA // W_IN_TILE

    def a_map(t):
        return (layer, jnp.where(t < first_lat, (t + first_lat - kv_perm) % first_lat, t), 0)

    def b_map(t):
        return (layer, jnp.where(t < first_lat, 0, (t + 1) * (W_IN_TILE // W_IN_GATE_SHIFT)), 0)

    return pl.pallas_call(
        _prep_w_in_kernel,
        grid=(N_MAIN // W_IN_TILE,),
        in_specs=[pl.BlockSpec((1, W_IN_TILE, d), a_map),
                  pl.BlockSpec((1, W_IN_GATE_SHIFT, d), b_map)],
        out_specs=[pl.BlockSpec((d, W_IN_TILE), lambda t: (0, t)),
                   pl.BlockSpec((d, LANE), lambda t: (0, 0))],
        out_shape=[jax.ShapeDtypeStruct((d, N_MAIN), BF16), jax.ShapeDtypeStruct((d, LANE), BF16)],
        compiler_params=_cparams(("arbitrary",)),
        name="prep_w_in",
    )(w_t, w_t)


def _proj_in(x, g, shift, scale, w_main, w_gate, tm, tn, n=None, b0=0, nb=None, prev=None):
    b, s, d = x.shape
    n = w_main.shape[1] if n is None else n
    nb = b if nb is None else nb
    per_batch = shift.shape[0] == b
    mod_map = (lambda bi, i, j: (b0 + bi, 0, 0)) if per_batch else (lambda bi, i, j: (0, 0, 0))
    operands = (x, g[None], shift, scale, w_main, w_gate)
    extend = () if prev is None else tuple(prev)
    return pl.pallas_call(
        _proj_kernel,
        grid=(nb, s // tm, n // tn),
        in_specs=[pl.BlockSpec((1, tm, d), lambda bi, i, j: (b0 + bi, i, 0)),
                  pl.BlockSpec((1, d), lambda bi, i, j: (0, 0)),
                  pl.BlockSpec((1, 1, d), mod_map),
                  pl.BlockSpec((1, 1, d), mod_map),
                  pl.BlockSpec((d, tn), lambda bi, i, j: (0, j)),
                  pl.BlockSpec((d, LANE), lambda bi, i, j: (0, 0))]
                 + [pl.BlockSpec(memory_space=pl.ANY)] * len(extend),
        out_specs=[pl.BlockSpec((1, tm, tn), lambda bi, i, j: (b0 + bi, i, j)),
                   pl.BlockSpec((1, tm, LANE), lambda bi, i, j: (b0 + bi, i, 0))],
        out_shape=[jax.ShapeDtypeStruct((b, s, n), BF16),
                   jax.ShapeDtypeStruct((b, s, LANE), F32)],
        scratch_shapes=[pltpu.VMEM((tm, d), BF16)],
        input_output_aliases={len(operands): 0, len(operands) + 1: 1} if extend else {},
        compiler_params=_cparams(("parallel", "parallel", "arbitrary")),
        name="proj_in",
    )(*operands, *extend)


def _softmax_av(q, keys, vals, biases):
    scores = []
    for kk, bb in zip(keys, biases):
        s = _dot_nt(q, kk)
        scores.append(s if bb is None else s + bb)
    m = scores[0].max(axis=-1, keepdims=True)
    for s in scores[1:]:
        m = jnp.maximum(m, s.max(axis=-1, keepdims=True))
    num = None
    den = None
    for s, vv in zip(scores, vals):
        e = jnp.exp(s - m)
        dsum = e.sum(axis=-1, keepdims=True)
        o = _dot(e.astype(BF16), vv)
        num = o if num is None else num + o
        den = dsum if den is None else den + dsum
    return num / den


def _na_kernel(q_ref, k_ref, v_ref, kc_ref, vc_ref, *rest, rows, kr):
    *bias_refs, o_ref = rest
    kc = kc_ref[0]
    vc = vc_ref[0]
    for j, bias_ref in enumerate(bias_refs):
        r = pl.program_id(1) * len(bias_refs) + j
        row_start = jnp.clip(r - kr // 2, 0, rows - kr)
        start = pl.multiple_of(row_start * GRID_W, GRID_W)
        n_win = kr * GRID_W
        q = q_ref[0, j * GRID_W:(j + 1) * GRID_W, :]
        kw = k_ref[0, pl.ds(start, n_win), :]
        vw = v_ref[0, pl.ds(start, n_win), :]
        o_ref[0, j * GRID_W:(j + 1) * GRID_W, :] = _na_row(q, kw, vw, kc, vc, bias_ref).astype(o_ref.dtype)


def _na_row(q, kw, vw, kc, vc, bias_ref):
    gw = NA_GROUP * NA_HEAD_DIM
    stacked = (NA_GROUP * GRID_W, gw)
    on_head = (lax.broadcasted_iota(jnp.int32, stacked, 0) // GRID_W
               == lax.broadcasted_iota(jnp.int32, stacked, 1) // NA_HEAD_DIM)
    outs = []
    for g in range(NA_HEADS // NA_GROUP):
        sl = slice(g * gw, (g + 1) * gw)
        q_all = jnp.where(on_head, jnp.concatenate([q[:, sl]] * NA_GROUP, axis=0), jnp.zeros((), q.dtype))
        bias = bias_ref[0, g * NA_GROUP * GRID_W:(g + 1) * NA_GROUP * GRID_W, :]
        o_all = _softmax_av(q_all, [kw[:, sl], kc[:, sl]], [vw[:, sl], vc[:, sl]], [bias, None])
        o_all = jnp.where(on_head, o_all, 0.0).reshape(NA_GROUP, GRID_W, gw)
        outs.append(o_all.sum(axis=0))
    return jnp.concatenate(outs, axis=-1)


def _na_bias_table(rpb, rows, kr):
    col = np.arange(GRID_W)
    col_start = np.clip(col - NA_WIN_C // 2, 0, GRID_W - NA_WIN_C)
    col_ok = (col[None, :] >= col_start[:, None]) & (col[None, :] < col_start[:, None] + NA_WIN_C)
    d_col = np.clip(col[None, :] - col[:, None], -(NA_WIN_C - 1), NA_WIN_C - 1) + NA_WIN_C - 1
    n_dr, n_dc = rpb.shape[1], rpb.shape[2]
    onehot = jnp.asarray((d_col.reshape(-1)[None, :] == np.arange(n_dc)[:, None]).astype(np.float32))
    by_col = jnp.dot(rpb.astype(F32).reshape(NA_HEADS * n_dr, n_dc), onehot, precision=lax.Precision.HIGHEST)
    by_col = by_col.reshape(NA_HEADS, n_dr, GRID_W, GRID_W)
    by_col = jnp.where(col_ok[None, None], by_col, NEG_INF)
    tables = []
    for o in range(kr):
        lo = NA_WIN_R - 1 - o
        tables.append(by_col[:, lo:lo + kr].transpose(0, 2, 1, 3).reshape(NA_HEADS, GRID_W, kr * GRID_W))
    return jnp.stack(tables).reshape(kr, NA_HEADS * GRID_W, kr * GRID_W)


def _na_latent(main, main_ctx, rpb):
    b, s, _ = main.shape
    sc = main_ctx.shape[1]
    rows = s // GRID_W
    kr = min(NA_WIN_R, rows)
    bias = _na_bias_table(rpb, rows, kr)
    w = NA_WIDTH

    per_step = next(n for n in (4, 2, 1) if rows % n == 0)

    def bias_spec(j):
        def bias_map(bi, i):
            r = i * per_step + j
            return (r - jnp.clip(r - kr // 2, 0, rows - kr), 0, 0)
        return pl.BlockSpec((1, NA_HEADS * GRID_W, kr * GRID_W), bias_map)

    return pl.pallas_call(
        functools.partial(_na_kernel, rows=rows, kr=kr),
        grid=(b, rows // per_step),
        in_specs=[pl.BlockSpec((1, per_step * GRID_W, w), lambda bi, i: (bi, i, OFF_Q_NA // w)),
                  pl.BlockSpec((1, s, w), lambda bi, i: (bi, 0, OFF_K_NA // w)),
                  pl.BlockSpec((1, s, w), lambda bi, i: (bi, 0, OFF_V_NA // w)),
                  pl.BlockSpec((1, sc, w), lambda bi, i: (bi, 0, OFF_K_NA // w)),
                  pl.BlockSpec((1, sc, w), lambda bi, i: (bi, 0, OFF_V_NA // w))]
                 + [bias_spec(j) for j in range(per_step)],
        out_specs=pl.BlockSpec((1, per_step * GRID_W, w), lambda bi, i: (bi, i, 0)),
        out_shape=jax.ShapeDtypeStruct((b, s, w), BF16),
        compiler_params=_cparams(("parallel", "arbitrary")),
        name="na_latent",
    )(main, main, main, main_ctx, main_ctx, *([bias] * per_step))


def _dense_attn_kernel(q_ref, k_ref, v_ref, o_ref):
    q = q_ref[0]
    k = k_ref[0]
    v = v_ref[0]
    outs = []
    for h in range(NA_HEADS):
        sl = slice(h * NA_HEAD_DIM, (h + 1) * NA_HEAD_DIM)
        outs.append(_softmax_av(q[:, sl], [k[:, sl]], [v[:, sl]], [None]))
    o_ref[0] = jnp.concatenate(outs, axis=-1).astype(o_ref.dtype)


def _dense_attn(main_ctx):
    b, sc, _ = main_ctx.shape
    w = NA_WIDTH
    return pl.pallas_call(
        _dense_attn_kernel,
        grid=(b,),
        in_specs=[pl.BlockSpec((1, sc, w), lambda bi: (bi, 0, OFF_Q_NA // w)),
                  pl.BlockSpec((1, sc, w), lambda bi: (bi, 0, OFF_K_NA // w)),
                  pl.BlockSpec((1, sc, w), lambda bi: (bi, 0, OFF_V_NA // w))],
        out_specs=pl.BlockSpec((1, sc, w), lambda bi: (bi, 0, 0)),
        out_shape=jax.ShapeDtypeStruct((b, sc, w), BF16),
        compiler_params=_cparams(("parallel",)),
        name="ctx_attn",
    )(main_ctx, main_ctx, main_ctx)


HALO = 16


def _gla_matrices(reverse):
    c = GLA_C
    t = np.arange(c)[:, None]
    m = np.arange(c)[None, :]
    blocks = [m <= t, m > t]
    for b in GLA_LEVELS:
        first = (t // (2 * b)) * (2 * b) + b
        is_q = (t & b) != 0
        blocks.append(np.where(is_q, (m > first) & (m <= t), (m > t) & (m <= first)))
    mats = np.stack(blocks).astype(np.float32)
    if reverse:
        mats = mats[:, ::-1, ::-1]
    mats = mats.reshape(-1, c)
    return jnp.asarray(np.concatenate([mats, mats], axis=1), dtype=BF16)


def _split_bf16(x):
    hi = x.astype(BF16)
    return hi, (x - hi.astype(F32)).astype(BF16)


def _pair_block_diag(x):
    lane = lax.broadcasted_iota(jnp.int32, x.shape, 1)
    zero = jnp.zeros((), x.dtype)
    return jnp.concatenate([jnp.where(lane < GLA_DK, x, zero), jnp.where(lane >= GLA_DK, x, zero)], axis=0)


def _gla_kernel(*refs, emit):
    n_in = 9 if emit else 8
    n_out = 2 if emit else 1
    ins, outs, scratch = refs[:2 * n_in], refs[2 * n_in:2 * (n_in + n_out)], refs[2 * (n_in + n_out):]
    ins = [ins[d * n_in:(d + 1) * n_in] for d in range(2)]
    outs = [outs[d * n_out:(d + 1) * n_out] for d in range(2)]
    step = pl.program_id(1)

    @pl.when(step == 0)
    def _():
        for d in range(2):
            scratch[d][...] = ins[d][-1][0]

    pending = [_gla_direction(ins[d][:-1], outs[d][:-1], scratch[d], reverse, emit)
               for d, reverse in enumerate((False, True))]
    while pending:
        pending = [stages for stages in pending if next(stages, "done") != "done"]

    @pl.when(step == pl.num_programs(1) - 1)
    def _():
        for d in range(2):
            outs[d][-1][0] = scratch[d][...]


def _gla_direction(ins, outs, st_ref, reverse, emit):
    if emit:
        q_ref, k_ref, v_ref, gt_ref, w2a_ref, w2b_ref, b2_ref, a_ref = ins
        (o_ref,) = outs
    else:
        k_ref, v_ref, gt_ref, w2a_ref, w2b_ref, b2_ref, a_ref = ins
    c = GLA_C
    pw = 2 * GLA_DK

    lr_hi, lr_lo = _split_bf16(gt_ref[0])
    logit = (_dot(jnp.concatenate([lr_hi, lr_lo], axis=1), w2a_ref[...]) + _dot(lr_hi, w2b_ref[...])
             + b2_ref[...])
    g = (jnp.minimum(logit, 0.0) - jnp.log1p(jnp.exp(-jnp.abs(logit)))) * (LOG2_E / GLA_GATE_TAU)
    g_hi, g_lo = _split_bf16(g)
    args = _dot(a_ref[...], jnp.concatenate([g_hi, g_lo], axis=0))
    cum = args[0:c]
    rem = args[c:2 * c]
    last_row = cum[0:1] if reverse else cum[c - 1:c]
    yield

    k = k_ref[0].astype(F32)
    v = v_ref[0]
    atts = []
    if emit:
        q = q_ref[0].astype(F32) * (GLA_DK ** -0.5)
        row_t = lax.broadcasted_iota(jnp.int32, (c, pw), 0)
        si = lax.broadcasted_iota(jnp.int32, (2 * c, c), 0) & (c - 1)
        ti = lax.broadcasted_iota(jnp.int32, (2 * c, c), 1)
        if reverse:
            row_t, ti, si = c - 1 - row_t, c - 1 - ti, c - 1 - si
        for hp in range(GLA_HEADS // 2):
            cs = slice(hp * pw, (hp + 1) * pw)
            qp, kp = q[:, cs], k[:, cs]
            att = jnp.where(ti == si, _dot_nt(_pair_block_diag(kp.astype(BF16)), qp.astype(BF16)), 0.0)
            for l, b in enumerate(GLA_LEVELS):
                x = (jnp.exp2(args[(2 + l) * c:(3 + l) * c, cs])
                     * jnp.where((row_t & b) != 0, qp, kp)).astype(BF16)
                pair = (((ti ^ si) >> (b.bit_length() - 1)) == 1) & ((ti & b) != 0)
                att = jnp.where(pair, _dot_nt(_pair_block_diag(x), x), att)
                yield
            atts.append(att.astype(BF16))

    outs = []
    for h in range(GLA_HEADS):
        sl = slice(h * GLA_DK, (h + 1) * GLA_DK)
        kh = k[:, sl]
        vh = v[:, h * GLA_DV:(h + 1) * GLA_DV]
        state = st_ref[h]
        kd = (kh * jnp.exp2(rem[:, sl])).astype(BF16)
        decay = jnp.exp2(jnp.broadcast_to(last_row[:, sl], (GLA_DK, GLA_DK))).T
        decay = jnp.concatenate([decay] * (GLA_DV // GLA_DK), axis=1)
        if emit:
            qd = (q[:, sl] * jnp.exp2(cum[:, sl])).astype(BF16)
            att_t = atts[h // 2][(h % 2) * c:(h % 2 + 1) * c]
            both = _dot_tn(jnp.concatenate([att_t, kd], axis=1), vh)
            outs.append(_dot(qd, state.astype(BF16)) + both[:c])
            st_ref[h] = decay * state + both[c:]
        else:
            st_ref[h] = decay * state + _dot_tn(kd, vh)
        yield

    if emit:
        o_ref[0] = jnp.concatenate(outs, axis=-1).astype(o_ref.dtype)


def _gla_scan(main, gate, gate_ws, s0s, emit):
    b, l, _ = main.shape
    n = l // GLA_C
    const = lambda arr: pl.BlockSpec(arr.shape, lambda bi, s: (0,) * arr.ndim)
    state_spec = pl.BlockSpec((1, GLA_HEADS, GLA_DK, GLA_DV), lambda bi, s: (bi, 0, 0, 0))
    state_shape = jax.ShapeDtypeStruct((b, GLA_HEADS, GLA_DK, GLA_DV), F32)
    in_specs, args, out_specs, out_shape = [], [], [], []
    for reverse in (False, True):
        amat = _gla_matrices(reverse)
        w2a, w2b, b2 = gate_ws[reverse]

        def col(block, reverse=reverse):
            return lambda bi, s: (bi, n - 1 - s if reverse else s, block)

        if emit:
            in_specs.append(pl.BlockSpec((1, GLA_C, GLA_KEY_WIDTH), col(OFF_Q_GLA // GLA_KEY_WIDTH)))
            args.append(main)
            out_specs.append(pl.BlockSpec((1, GLA_C, GLA_VAL_WIDTH), col(0)))
            out_shape.append(jax.ShapeDtypeStruct((b, l, GLA_VAL_WIDTH), BF16))
        in_specs += [pl.BlockSpec((1, GLA_C, GLA_KEY_WIDTH), col(OFF_K_GLA // GLA_KEY_WIDTH)),
                     pl.BlockSpec((1, GLA_C, GLA_VAL_WIDTH), col(OFF_V_GLA // GLA_VAL_WIDTH)),
                     pl.BlockSpec((1, GLA_C, LANE), col(0)),
                     const(w2a), const(w2b), const(b2), const(amat), state_spec]
        args += [main, main, gate, w2a, w2b, b2, amat, s0s[reverse]]
        out_specs.append(state_spec)
        out_shape.append(state_shape)
    res = pl.pallas_call(
        functools.partial(_gla_kernel, emit=emit),
        grid=(b, n),
        in_specs=in_specs,
        out_specs=out_specs,
        out_shape=out_shape,
        scratch_shapes=[pltpu.VMEM((GLA_HEADS, GLA_DK, GLA_DV), F32)] * 2,
        compiler_params=_cparams(("parallel", "arbitrary")),
        name="gla_scan",
    )(*args)
    return (res[0], res[2], res[1], res[3]) if emit else (None, None, res[0], res[1])


def _gla_gate_weights(gate_w, gate_b):
    out = []
    for dr in range(2):
        w = jnp.zeros((LANE, GLA_KEY_WIDTH), F32)
        w = w.at[dr * GLA_GATE_RANK:(dr + 1) * GLA_GATE_RANK].set(gate_w[dr])
        w_hi = w.astype(BF16)
        w_lo = (w - w_hi.astype(F32)).astype(BF16)
        out.append((jnp.concatenate([w_hi, w_hi], axis=0), w_lo, gate_b[dr][None]))
    return out


def _conv_tile(b_ref, c_ref, x_ref, cp_ref, xp_ref, cn_ref, xn_ref, w_ref):
    i, n = pl.program_id(1), pl.num_programs(1)
    u = c_ref[0].astype(F32) * x_ref[0].astype(F32)
    tm = u.shape[0]
    before = jnp.where(i > 0, 1.0, 0.0) * (cp_ref[0, HALO - 1:HALO].astype(F32) * xp_ref[0, HALO - 1:HALO].astype(F32))
    after = jnp.where(i < n - 1, 1.0, 0.0) * (cn_ref[0, 0:1].astype(F32) * xn_ref[0, 0:1].astype(F32))
    t = lax.broadcasted_iota(jnp.int32, u.shape, 0)
    prev = jnp.where(t == 0, before, pltpu.roll(u, 1, axis=0))
    nxt = jnp.where(t == tm - 1, after, pltpu.roll(u, tm - 1, axis=0))
    w = w_ref[...]
    return b_ref[0].astype(F32) * (prev * w[0:1] + u * w[1:2] + nxt * w[2:3])


def _merge_kernel(ona_ref, bsc_ref, csc_ref, xsc_ref, cp_ref, xp_ref, cn_ref, xn_ref, cw_ref, of_ref, ob_ref,
                  r_ref, gna_ref, gsc_ref, ggl_ref, x_ref, gt_ref,
                  gn_ref, wna_ref, wsc_ref, wgl_ref, wo_ref, g2_ref, sh2_ref, sc2_ref, wr_ref, *rest):
    xo_ref, h2_ref, hp_ref, lg_ref = rest[-4:]
    o_sc = _conv_tile(bsc_ref, csc_ref, xsc_ref, cp_ref, xp_ref, cn_ref, xn_ref, cw_ref).astype(BF16)
    o = of_ref[0].astype(F32) + ob_ref[0].astype(F32)
    normed = []
    for h in range(GLA_HEADS):
        oh = o[:, h * GLA_DV:(h + 1) * GLA_DV]
        ms = jnp.mean(oh * oh, axis=-1, keepdims=True)
        normed.append(oh * lax.rsqrt(ms + RMS_EPS))
    r = r_ref[0].astype(F32)
    y_gla = jnp.concatenate(normed, axis=-1) * gn_ref[...] * (r * _sigmoid(r))
    y = (_sigmoid(gna_ref[0].astype(F32)) * _dot(ona_ref[0], wna_ref[...])
         + _sigmoid(gsc_ref[0].astype(F32)) * _dot(o_sc, wsc_ref[...])
         + _sigmoid(ggl_ref[0].astype(F32)) * _dot(y_gla.astype(BF16), wgl_ref[...]))
    xn = x_ref[0] + gt_ref[0] * _dot(y.astype(BF16), wo_ref[...])
    xo_ref[0] = xn
    ms = jnp.mean(xn * xn, axis=-1, keepdims=True)
    h2 = xn * lax.rsqrt(ms + RMS_EPS) * g2_ref[...] * (1.0 + sc2_ref[0]) + sh2_ref[0]
    h2b = h2.astype(BF16)
    h2_ref[0] = h2b
    _store_parts(hp_ref, _pack_rows(h2))
    lg_ref[...] = _dot(h2b, wr_ref[...])


def _merge(o_na, conv_w, o_f, o_b, main, x, gt1, gn, w_na, w_sc, w_gla, w_out, g2, sh2, sc2, w_router, tm,
           n_routed, tok_off, routed=None):
    b, s, d = x.shape
    per_batch = gt1.shape[0] == b
    mod_map = (lambda bi, i: (bi, 0, 0)) if per_batch else (lambda bi, i: (0, 0, 0))
    tok = lambda width, blk: pl.BlockSpec((1, tm, width), lambda bi, i: (bi, i, blk))
    full = lambda arr: pl.BlockSpec(arr.shape, lambda bi, i: (0,) * arr.ndim)
    mod = pl.BlockSpec((1, 1, d), mod_map)
    gn_t = jnp.tile(gn, GLA_HEADS)[None]
    g2_t = g2[None]
    per_tile = tm // HALO
    last_halo = s // HALO - 1
    halo_prev = lambda blk: pl.BlockSpec(
        (1, HALO, SC_WIDTH), lambda bi, i: (bi, jnp.maximum(i * per_tile - 1, 0), blk))
    halo_next = lambda blk: pl.BlockSpec(
        (1, HALO, SC_WIDTH), lambda bi, i: (bi, jnp.minimum((i + 1) * per_tile, last_halo), blk))
    col_b, col_c, col_x = OFF_B_SC // SC_WIDTH, OFF_C_SC // SC_WIDTH, OFF_X_SC // SC_WIDTH
    extend = () if routed is None else tuple(routed)
    operands = (o_na, main, main, main, main, main, main, main, conv_w, o_f, o_b, main, main, main, main, x, gt1,
                gn_t, w_na, w_sc, w_gla, w_out, g2_t, sh2, sc2, w_router)
    n_in = len(operands)
    nblk = s // tm
    blk0 = tok_off // tm
    return pl.pallas_call(
        _merge_kernel,
        grid=(b, s // tm),
        in_specs=[tok(NA_WIDTH, 0), tok(SC_WIDTH, col_b), tok(SC_WIDTH, col_c), tok(SC_WIDTH, col_x),
                  halo_prev(col_c), halo_prev(col_x), halo_next(col_c), halo_next(col_x), full(conv_w),
                  tok(GLA_VAL_WIDTH, 0), tok(GLA_VAL_WIDTH, 0),
                  tok(d, OFF_R_GLA // d), tok(d, OFF_MERGE // d), tok(d, OFF_MERGE // d + 1),
                  tok(d, OFF_MERGE // d + 2), tok(d, 0), mod,
                  full(gn_t), full(w_na), full(w_sc), full(w_gla), full(w_out), full(g2_t), mod, mod,
                  full(w_router)] + [pl.BlockSpec(memory_space=pl.ANY)] * len(extend),
        out_specs=[tok(d, 0), tok(d, 0),
                   pl.BlockSpec((SC_PARTS, tm, SC_ROW), lambda bi, i: (0, blk0 + bi * nblk + i, 0)),
                   pl.BlockSpec((tm, LANE), lambda bi, i: (blk0 + bi * nblk + i, 0))],
        out_shape=[jax.ShapeDtypeStruct((b, s, d), F32),
                   jax.ShapeDtypeStruct((b, s, d), BF16),
                   jax.ShapeDtypeStruct((SC_PARTS, n_routed, SC_ROW), U32),
                   jax.ShapeDtypeStruct((n_routed, LANE), F32)],
        input_output_aliases={n_in: 2, n_in + 1: 3} if extend else {},
        compiler_params=_cparams(("parallel", "parallel")),
        name="merge",
    )(*operands, *extend)


def _router_kernel(lg_ref, br_ref, tri_ref, eidx_ref, gw_ref, rank_ref, cnt_ref, carry_ref):
    tm = lg_ref.shape[0]

    @pl.when(pl.program_id(0) == 0)
    def _():
        carry_ref[...] = jnp.zeros_like(carry_ref)

    scores = _sigmoid(lg_ref[...].T[:N_EXPERTS])
    sel = scores + br_ref[...]
    neg = -jnp.inf

    sel3 = sel.reshape(N_EXPERT_GROUPS, GROUP_SIZE, tm)
    i3 = lax.broadcasted_iota(jnp.int32, sel3.shape, 1)
    m1 = sel3.max(axis=1, keepdims=True)
    first = jnp.where(sel3 == m1, i3, GROUP_SIZE).min(axis=1, keepdims=True)
    m2 = jnp.where(i3 == first, neg, sel3).max(axis=1, keepdims=True)
    gscore = (m1 + m2)[:, 0, :]

    gi = lax.broadcasted_iota(jnp.int32, gscore.shape, 0)
    gmask = jnp.zeros(gscore.shape, jnp.bool_)
    for _ in range(TOPK_GROUPS):
        m = gscore.max(axis=0, keepdims=True)
        pick = gi == jnp.where(gscore == m, gi, N_EXPERT_GROUPS).min(axis=0, keepdims=True)
        gmask = gmask | pick
        gscore = jnp.where(pick, neg, gscore)
    emask = jnp.broadcast_to(gmask[:, None, :], sel3.shape).reshape(N_EXPERTS, tm)
    sel = jnp.where(emask, sel, neg)

    ei = lax.broadcasted_iota(jnp.int32, sel.shape, 0)
    picks, idxs, ws = [], [], []
    for _ in range(TOP_K):
        m = sel.max(axis=0, keepdims=True)
        idx = jnp.where(sel == m, ei, N_EXPERTS).min(axis=0, keepdims=True)
        pick = ei == idx
        picks.append(pick)
        idxs.append(idx)
        ws.append(jnp.where(pick, scores, 0.0).sum(axis=0, keepdims=True))
        sel = jnp.where(pick, neg, sel)
    w = jnp.concatenate(ws, axis=0)
    gw_ref[...] = w / w.sum(axis=0, keepdims=True) * ROUTED_SCALE
    eidx_ref[...] = jnp.concatenate(idxs, axis=0)

    onehot = picks[0]
    for p in picks[1:]:
        onehot = onehot | p
    onehot = jnp.where(onehot, 1.0, 0.0).astype(BF16)
    before = _dot(onehot, tri_ref[...]) + jnp.tile(carry_ref[...], (1, tm // LANE))
    rank_ref[...] = jnp.concatenate(
        [jnp.where(p, before, 0.0).sum(axis=0, keepdims=True) for p in picks], axis=0).astype(jnp.int32)
    carry_ref[...] += _dot(onehot, jnp.ones((tm, LANE), BF16))
    cnt_ref[...] = carry_ref[...]


def _route(logits, b_router, tm=512):
    t = logits.shape[0]
    br = jnp.broadcast_to(b_router.astype(F32)[:, None], (N_EXPERTS, tm))
    tri = jnp.asarray(np.triu(np.ones((tm, tm), np.float32), 1), dtype=BF16)
    kt = lambda dt: jax.ShapeDtypeStruct((TOP_K, t), dt)
    eidx, gw, rank, cnt = pl.pallas_call(
        _router_kernel,
        grid=(t // tm,),
        in_specs=[pl.BlockSpec((tm, LANE), lambda i: (i, 0)),
                  pl.BlockSpec((N_EXPERTS, tm), lambda i: (0, 0)),
                  pl.BlockSpec((tm, tm), lambda i: (0, 0))],
        out_specs=[pl.BlockSpec((TOP_K, tm), lambda i: (0, i)),
                   pl.BlockSpec((TOP_K, tm), lambda i: (0, i)),
                   pl.BlockSpec((TOP_K, tm), lambda i: (0, i)),
                   pl.BlockSpec((N_EXPERTS, LANE), lambda i: (0, 0))],
        out_shape=[kt(jnp.int32), kt(F32), kt(jnp.int32),
                   jax.ShapeDtypeStruct((N_EXPERTS, LANE), F32)],
        scratch_shapes=[pltpu.VMEM((N_EXPERTS, LANE), F32)],
        compiler_params=_cparams(("arbitrary",)),
        name="router",
    )(logits, br, tri)
    return eidx, gw, rank, cnt[:, 0].astype(jnp.int32)


def _sc_mesh():
    return plsc.VectorSubcoreMesh(core_axis_name="core", subcore_axis_name="subcore")


def _dispatch_rows(xp, dest, slots):
    parts, t, _ = xp.shape
    nwin = parts * t // SC_WINDOW
    idx = dest.reshape(TOP_K, t // SC_WINDOW, SC_WINDOW).transpose(1, 0, 2)
    idx = jnp.concatenate([idx + part * slots for part in range(parts)], axis=0)

    @pl.kernel(out_type=jax.ShapeDtypeStruct((parts * slots, SC_ROW), xp.dtype), mesh=_sc_mesh(),
               scratch_types=[], name="moe_dispatch")
    def run(x_hbm, i_hbm, o_hbm):
        def body(x_vmem, i_vmem):
            for k in range(TOP_K):
                pltpu.sync_copy(x_vmem, o_hbm.at[i_vmem.at[0, k]])

        pltpu.emit_pipeline(
            body,
            grid=(nwin,),
            in_specs=[pl.BlockSpec((SC_WINDOW, SC_ROW), lambda i: (i, 0)),
                      pl.BlockSpec((1, TOP_K, SC_WINDOW), lambda i: (i, 0, 0))],
            out_specs=[],
            core_axis_name=("core", "subcore"),
            dimension_semantics=(pltpu.PARALLEL,),
        )(x_hbm, i_hbm)

    return run(xp.reshape(parts * t, SC_ROW), idx).reshape(parts, slots, SC_ROW)


def _gather_rows(yp, idx):
    n = idx.shape[0]
    parts, slots, _ = yp.shape
    idx = jnp.concatenate([idx + part * slots for part in range(parts)]).reshape(1, n * parts)

    @pl.kernel(out_type=jax.ShapeDtypeStruct((n * parts, SC_ROW), yp.dtype), mesh=_sc_mesh(),
               scratch_types=[], name="moe_gather")
    def run(y_hbm, i_hbm, o_hbm):
        def body(i_vmem, o_vmem):
            pltpu.sync_copy(y_hbm.at[i_vmem.at[0]], o_vmem)

        pltpu.emit_pipeline(
            body,
            grid=(n * parts // SC_WINDOW,),
            in_specs=[pl.BlockSpec((1, SC_WINDOW), lambda i: (0, i))],
            out_specs=[pl.BlockSpec((SC_WINDOW, SC_ROW), lambda i: (i, 0))],
            core_axis_name=("core", "subcore"),
            dimension_semantics=(pltpu.PARALLEL,),
        )(i_hbm, o_hbm)

    return run(yp.reshape(parts * slots, SC_ROW), idx).reshape(parts, n, SC_ROW)


def _expert_kernel(be_ref, bv_ref, bs_ref, x_ref, wg_ref, wu_ref, wd_ref, o_ref, wg_s, wu_s, wd_s):
    i = pl.program_id(0)
    valid = bv_ref[i]
    new_expert = (i == 0) | (be_ref[i] != be_ref[jnp.maximum(i - 1, 0)])

    @pl.when(new_expert)
    def _():
        wg_s[...] = wg_ref[0, 0].astype(BF16)
        wu_s[...] = wu_ref[0, 0].astype(BF16)
        wd_s[...] = wd_ref[0, 0].astype(BF16)

    @pl.when(valid > 0)
    def _():
        w = _load_parts(x_ref)
        row = lax.broadcasted_iota(jnp.int32, w.shape, 0)
        w = jnp.where(row < valid, w, jnp.uint32(0))
        lo, hi = _unpack_rows(w)
        x = jnp.concatenate([lo, hi], axis=1).astype(BF16)
        a = _dot(x, wg_s[...])
        hid = a * _sigmoid(a) * _dot(x, wu_s[...])
        _store_parts(o_ref, _pack_rows(_dot(hid.astype(BF16), wd_s[...])))


def _experts(xs, blk_e, blk_valid, blk_src, layer, w_gate, w_up, w_down):
    parts, slots, _ = xs.shape
    d = D_MODEL
    nb = slots // MOE_BLOCK
    return pl.pallas_call(
        _expert_kernel,
        grid_spec=pltpu.PrefetchScalarGridSpec(
            num_scalar_prefetch=3,
            grid=(nb,),
            in_specs=[pl.BlockSpec((parts, MOE_BLOCK, SC_ROW), lambda i, be, bv, bs: (0, bs[i], 0)),
                      pl.BlockSpec((1, 1, d, EXPERT_FF), lambda i, be, bv, bs: (layer, be[i], 0, 0)),
                      pl.BlockSpec((1, 1, d, EXPERT_FF), lambda i, be, bv, bs: (layer, be[i], 0, 0)),
                      pl.BlockSpec((1, 1, EXPERT_FF, d), lambda i, be, bv, bs: (layer, be[i], 0, 0))],
            out_specs=pl.BlockSpec((parts, MOE_BLOCK, SC_ROW), lambda i, be, bv, bs: (0, bs[i], 0)),
            scratch_shapes=[pltpu.VMEM((d, EXPERT_FF), BF16), pltpu.VMEM((d, EXPERT_FF), BF16),
                            pltpu.VMEM((EXPERT_FF, d), BF16)]),
        out_shape=jax.ShapeDtypeStruct((parts, slots, SC_ROW), U32),
        compiler_params=_cparams(("arbitrary",)),
        name="experts",
    )(blk_e, blk_valid, blk_src, xs, w_gate, w_up, w_down)


def _combine_kernel(yg_ref, gw_ref, h_ref, x_ref, gt_ref, wsg_ref, wsu_ref, wsd_ref, gf_ref, o_ref, *, final):
    h = h_ref[0]
    a = _dot(h, wsg_ref[...])
    hid = a * _sigmoid(a) * _dot(h, wsu_ref[...])
    y = _dot(hid.astype(BF16), wsd_ref[...])
    gw = gw_ref[...]
    y_lo = y[:, :D_MODEL // 2]
    y_hi = y[:, D_MODEL // 2:]
    for k in range(TOP_K):
        lo, hi = _unpack_rows(_load_parts(yg_ref, k))
        y_lo = y_lo + gw[:, k:k + 1] * lo
        y_hi = y_hi + gw[:, k:k + 1] * hi
    y = jnp.concatenate([y_lo, y_hi], axis=1)
    xn = x_ref[0] + gt_ref[0] * y
    if final:
        ms = jnp.mean(xn * xn, axis=-1, keepdims=True)
        xn = xn * lax.rsqrt(ms + RMS_EPS) * gf_ref[...]
    o_ref[0] = xn


def _combine(yg, gw, tok_off, h2, x, gt2, b0, nb, ws_gate, ws_up, ws_down, g_final, final, tm):
    b, s, d = x.shape
    per_batch = gt2.shape[0] == b
    mod_map = (lambda bi, i: (b0 + bi, 0, 0)) if per_batch else (lambda bi, i: (0, 0, 0))
    full = lambda arr: pl.BlockSpec(arr.shape, lambda bi, i: (0,) * arr.ndim)
    tok = lambda width: pl.BlockSpec((1, tm, width), lambda bi, i: (b0 + bi, i, 0))
    gf = g_final[None]
    nblk = s // tm
    blk0 = (tok_off + b0 * s) // tm
    return pl.pallas_call(
        functools.partial(_combine_kernel, final=final),
        grid=(nb, nblk),
        in_specs=[pl.BlockSpec((SC_PARTS, TOP_K, tm, SC_ROW), lambda bi, i: (0, 0, bi * nblk + i, 0)),
                  pl.BlockSpec((tm, TOP_K), lambda bi, i: (blk0 + bi * nblk + i, 0)),
                  tok(d), tok(d),
                  pl.BlockSpec((1, 1, d), mod_map),
                  full(ws_gate), full(ws_up), full(ws_down), full(gf)],
        out_specs=tok(d),
        out_shape=jax.ShapeDtypeStruct((b, s, d), F32),
        input_output_aliases={3: 0},
        compiler_params=_cparams(("parallel", "parallel")),
        name="combine",
    )(yg, gw, h2, x, gt2, ws_gate, ws_up, ws_down, gf)


def _project_latent(x, p, mods, b0=0, nb=None, prev=None):
    w_main, w_gate = p['w_in_prepped']
    return _proj_in(x, p['g_norm1'], mods[0], mods[1], w_main, w_gate, tm=min(2048, x.shape[1]), tn=1024,
                    b0=b0, nb=nb, prev=prev)


def _layer(x, ctx_s, mods, mods_ctx, p, ctx_out, final, g_final, projected=None, after_piece=None):
    b, s, d = x.shape
    sc = ctx_s.shape[1]
    sh1, sc1, gt1, sh2, sc2, gt2 = mods
    csh1, csc1, cgt1, csh2, csc2, cgt2 = mods_ctx

    w_main, w_gate = p['w_in_prepped']
    main, gate = _project_latent(x, p, mods) if projected is None else projected
    ctx_flat = ctx_s.reshape(1, b * sc, d)
    n_ctx, tn_ctx = (N_MAIN, 1024) if ctx_out else (N_KV_MAIN, N_KV_MAIN // 2)
    main_c, gate_c = _proj_in(ctx_flat, p['g_norm1'], csh1, csc1, w_main, w_gate, tm=min(1024, b * sc),
                              tn=tn_ctx, n=n_ctx)
    main_c = main_c.reshape(b, sc, n_ctx)
    gate_c = gate_c.reshape(b, sc, LANE)

    o_na = _na_latent(main, main_c, p['na_rpb'])

    gate_ws = _gla_gate_weights(p['gla_gate_w'], p['gla_gate_b'])
    s0 = jnp.zeros((b, GLA_HEADS, GLA_DK, GLA_DV), F32)
    o_cf, o_cb, st_f, st_b = _gla_scan(main_c, gate_c, gate_ws, (s0, s0), ctx_out)
    o_f, o_b, _, _ = _gla_scan(main, gate, gate_ws, (st_f, st_b), True)

    w_na = p['w_branch_na'].astype(BF16)
    w_sc = p['w_branch_sc'].astype(BF16)
    w_gla = p['w_branch_gla'].astype(BF16)
    w_out = p['w_out'].astype(BF16)
    w_router = jnp.pad(p['w_router'], ((0, 0), (0, LANE - N_EXPERTS))).astype(BF16)
    n_lat = b * s
    t = n_lat + (b * sc if ctx_out else 0)
    x, h2, hp_all, lg_all = _merge(o_na, p['conv_w'], o_f, o_b, main, x, gt1, p['gla_norm_g'], w_na, w_sc, w_gla,
                                   w_out, p['g_norm2'], sh2, sc2, w_router, tm=min(512, s), n_routed=t, tok_off=0)
    if ctx_out:
        o_na_c = _dense_attn(main_c)
        ctx_s, h2_c, hp_all, lg_all = _merge(o_na_c, p['conv_w'], o_cf, o_cb, main_c, ctx_s, cgt1, p['gla_norm_g'],
                                             w_na, w_sc, w_gla, w_out, p['g_norm2'], csh2, csc2, w_router,
                                             tm=min(256, sc), n_routed=t, tok_off=n_lat, routed=(hp_all, lg_all))

    eidx, gw, rank, counts = _route(lg_all, p['b_router'])
    padded = (counts + MOE_BLOCK - 1) // MOE_BLOCK * MOE_BLOCK
    pad_end = jnp.cumsum(padded)
    pad_start = pad_end - padded
    onehot = eidx[:, :, None] == jnp.arange(N_EXPERTS, dtype=jnp.int32)
    dest = jnp.sum(jnp.where(onehot, pad_start, 0), axis=-1) + rank
    n_blocks = -(-(t * TOP_K + N_EXPERTS * (MOE_BLOCK - 1)) // MOE_BLOCK)
    slots = n_blocks * MOE_BLOCK
    blk_start = jnp.arange(n_blocks, dtype=jnp.int32) * MOE_BLOCK
    blk_e = jnp.minimum(jnp.sum(pad_end[None, :] <= blk_start[:, None], axis=1), N_EXPERTS - 1).astype(jnp.int32)
    used_end = (pad_start + counts)[blk_e]
    blk_valid = jnp.clip(used_end - blk_start, 0, MOE_BLOCK).astype(jnp.int32)
    n_used = pad_end[-1] // MOE_BLOCK
    blk_src = jnp.minimum(jnp.arange(n_blocks, dtype=jnp.int32), n_used - 1)
    blk_e = blk_e[blk_src]

    xs = _dispatch_rows(hp_all, dest, slots)
    ys = _experts(xs, blk_e, blk_valid, blk_src, p['layer'], p['w_exp_gate'], p['w_exp_up'], p['w_exp_down'])
    gw_t = gw.T
    ws_gate = p['w_sh_gate'].astype(BF16)
    ws_up = p['w_sh_up'].astype(BF16)
    ws_down = p['w_sh_down'].astype(BF16)

    def gathered(t0, n):
        return _gather_rows(ys, dest[:, t0:t0 + n].reshape(-1)).reshape(SC_PARTS, TOP_K, n, SC_ROW)

    pieces = next(n for n in (4, 2, 1) if b % n == 0)
    nb = b // pieces
    for q in range(pieces):
        x = _combine(gathered(q * nb * s, nb * s), gw_t, 0, h2, x, gt2, q * nb, nb, ws_gate, ws_up, ws_down,
                     g_final, final, tm=min(256, s))
        if after_piece is not None:
            after_piece(x, q * nb, nb)
    if ctx_out:
        ctx_s = _combine(gathered(n_lat, b * sc), gw_t, n_lat, h2_c, ctx_s, cgt2, 0, b, ws_gate, ws_up, ws_down,
                         g_final, False, tm=min(256, sc))
    return x, ctx_s


def kernel(x, c, ctx, c_ctx, w_mod, b_mod, g_norm1, g_norm2, w_in, na_rpb, w_branch_na, conv_w, w_branch_sc,
           gla_gate_w, gla_gate_b, gla_norm_g, w_branch_gla, w_out, w_router, b_router, w_exp_gate, w_exp_up,
           w_exp_down, w_sh_gate, w_sh_up, w_sh_down, g_final):
    stacked = dict(g_norm1=g_norm1, g_norm2=g_norm2, na_rpb=na_rpb, w_branch_na=w_branch_na,
                   conv_w=conv_w, w_branch_sc=w_branch_sc, gla_gate_w=gla_gate_w, gla_gate_b=gla_gate_b,
                   gla_norm_g=gla_norm_g, w_branch_gla=w_branch_gla, w_out=w_out, w_router=w_router,
                   b_router=b_router,
                   w_sh_gate=w_sh_gate, w_sh_up=w_sh_up, w_sh_down=w_sh_down)
    depth = w_in.shape[0]
    layers = []
    for i in range(depth):
        p = {name: arr[i] for name, arr in stacked.items()}
        p.update(layer=i, w_in_prepped=_prep_w_in(w_in, i), w_exp_gate=w_exp_gate, w_exp_up=w_exp_up,
                 w_exp_down=w_exp_down)
        layers.append((p,) + _mod_vectors(c, c_ctx, w_mod, b_mod, i))

    ctx_s = ctx
    projected = None
    for i, (p, mods, mods_ctx) in enumerate(layers):
        last = i == depth - 1
        after_piece = None
        next_projected = []
        if not last:
            p_next, mods_next, _ = layers[i + 1]

            def after_piece(xq, b0, nb, p_next=p_next, mods_next=mods_next, acc=next_projected):
                acc.append(_project_latent(xq, p_next, mods_next, b0, nb, acc[-1] if acc else None))

        x, ctx_s = _layer(x, ctx_s, mods, mods_ctx, p, not last, last, g_final, projected, after_piece)
        projected = next_projected[-1] if next_projected else None
    return x
```

```python
import functools

import numpy as np
import jax
import jax.numpy as jnp
from jax import lax
from jax.experimental import pallas as pl
from jax.experimental.pallas import tpu as pltpu
from jax.experimental.pallas import tpu_sc as plsc

F32 = jnp.float32
BF16 = jnp.bfloat16
U32 = jnp.uint32

D_MODEL = 1024
N_MOD = 6
RMS_EPS = 1e-6
NEG_INF = -1e30
GRID_W = 64
NA_HEADS = 8
NA_HEAD_DIM = 64
NA_WIDTH = NA_HEADS * NA_HEAD_DIM
NA_WIN_R = 8
NA_WIN_C = 16
NA_GROUP = 4
SC_WIDTH = 512
GLA_HEADS = 4
GLA_KEY_WIDTH = 512
GLA_VAL_WIDTH = 1024
GLA_DK = GLA_KEY_WIDTH // GLA_HEADS
GLA_DV = GLA_VAL_WIDTH // GLA_HEADS
GLA_GATE_RANK = 16
GLA_GATE_TAU = 16.0
LOG2_E = 1.4426950408889634
N_EXPERTS = 64
N_EXPERT_GROUPS = 8
GROUP_SIZE = N_EXPERTS // N_EXPERT_GROUPS
TOPK_GROUPS = 4
TOP_K = 8
EXPERT_FF = 256
ROUTED_SCALE = 2.5
MOE_BLOCK = 1024

LANE = 128
GLA_C = 128
GLA_LEVELS = tuple(GLA_C >> (i + 1) for i in range(GLA_C.bit_length() - 1))
VMEM_LIMIT = 48 * 1024 * 1024
SC_WINDOW = 128
SC_ROW = 256
SC_PARTS = D_MODEL // 2 // SC_ROW

OFF_V_GLA = 0
OFF_K_NA = 1024
OFF_V_NA = 1536
OFF_K_GLA = 2048
N_KV_MAIN = 2560
OFF_Q_NA = 2560
OFF_B_SC = 3072
OFF_C_SC = 3584
OFF_X_SC = 4096
OFF_Q_GLA = 4608
OFF_R_GLA = 5120
OFF_MERGE = 6144
N_MAIN = 9216


def _cparams(sem, vmem=VMEM_LIMIT):
    return pltpu.CompilerParams(dimension_semantics=sem, vmem_limit_bytes=vmem)


def _dot(a, b):
    return jnp.dot(a, b, preferred_element_type=F32)


def _dot_nt(a, b):
    return lax.dot_general(a, b, (((1,), (1,)), ((), ())), preferred_element_type=F32)


def _dot_tn(a, b):
    return lax.dot_general(a, b, (((0,), (0,)), ((), ())), preferred_element_type=F32)


def _sigmoid(x):
    return 0.5 * jnp.tanh(0.5 * x) + 0.5


def _pack_rows(x):
    n = x.shape[1] // 2
    r = x.astype(BF16).astype(F32)
    lo = pltpu.bitcast(r[:, :n], U32) >> 16
    hi = pltpu.bitcast(r[:, n:], U32)
    return hi | lo


def _store_parts(ref, words):
    for part in range(SC_PARTS):
        dst = ref.at[part, 0] if len(ref.shape) == 4 else ref.at[part]
        dst[...] = words[:, part * SC_ROW:(part + 1) * SC_ROW]


def _load_parts(ref, *lead):
    return jnp.concatenate([ref[(part,) + lead] for part in range(SC_PARTS)], axis=-1)


def _unpack_rows(w):
    lo = pltpu.bitcast(w << 16, F32)
    hi = pltpu.bitcast(w & jnp.uint32(0xFFFF0000), F32)
    return lo, hi


def _mod_kernel(a_ref, w_ref, b_ref, o_ref):
    a = a_ref[...]
    a = a * _sigmoid(a)
    o_ref[...] = _dot(a.astype(BF16), w_ref[0].astype(BF16)) + b_ref[0]


def _mod_vectors(c, c_ctx, w_mod, b_mod, layer):
    b = c.shape[0]
    rows = -(-(b + 1) // 8) * 8
    a = jnp.concatenate([c, c_ctx[None], jnp.zeros((rows - b - 1, D_MODEL), F32)], axis=0)
    n = N_MOD * D_MODEL
    tn = 1536
    out = pl.pallas_call(
        _mod_kernel,
        grid=(n // tn,),
        in_specs=[pl.BlockSpec((rows, D_MODEL), lambda j: (0, 0)),
                  pl.BlockSpec((1, D_MODEL, tn), lambda j: (layer, 0, j)),
                  pl.BlockSpec((1, 1, tn), lambda j: (layer, 0, j))],
        out_specs=pl.BlockSpec((rows, tn), lambda j: (0, j)),
        out_shape=jax.ShapeDtypeStruct((rows, n), F32),
        compiler_params=_cparams(("parallel",)),
        name="mod_vectors",
    )(a, w_mod, b_mod[:, None])
    lat = out[:b].reshape(b, N_MOD, 1, D_MODEL)
    ctx = out[b].reshape(N_MOD, 1, 1, D_MODEL)
    return [lat[:, i] for i in range(N_MOD)], [ctx[i] for i in range(N_MOD)]


def _proj_kernel(x_ref, g_ref, sh_ref, sc_ref, w_ref, wg_ref, *rest):
    o_ref, og_ref, h_ref = rest[-3:]

    @pl.when(pl.program_id(2) == 0)
    def _():
        x = x_ref[0]
        ms = jnp.mean(x * x, axis=-1, keepdims=True)
        h = x * lax.rsqrt(ms + RMS_EPS) * g_ref[...] * (1.0 + sc_ref[0]) + sh_ref[0]
        hb = h.astype(BF16)
        h_ref[...] = hb
        og_ref[0] = _dot(hb, wg_ref[...])

    o_ref[0] = _dot(h_ref[...], w_ref[...]).astype(o_ref.dtype)


W_IN_TILE = 512
W_IN_GATE_SHIFT = 2 * GLA_GATE_RANK


def _prep_w_in_kernel(a_ref, b_ref, o_ref, g_ref):
    t = pl.program_id(0)
    first_lat = N_KV_MAIN // W_IN_TILE
    a = a_ref[0]

    @pl.when(t < first_lat)
    def _():
        o_ref[...] = a.T.astype(BF16)

    @pl.when(t >= first_lat)
    def _():
        moved = jnp.concatenate([a[W_IN_GATE_SHIFT:], b_ref[0]], axis=0)
        scale = jnp.where(t == first_lat, NA_HEAD_DIM ** -0.5, 1.0)
        o_ref[...] = (moved * scale).T.astype(BF16)

    @pl.when(t == first_lat)
    def _():
        head = a[:LANE]
        row = lax.broadcasted_iota(jnp.int32, head.shape, 0)
        g_ref[...] = jnp.where(row < W_IN_GATE_SHIFT, head, 0.0).T.astype(BF16)


def _prep_w_in(w_in, layer):
    d = w_in.shape[1]
    w_t = jnp.swapaxes(w_in, 1, 2)
    first_lat = N_KV_MAIN // W_IN_TILE
    kv_perm = OFF_K_NA // W_IN_TILE

    def a_map(t):
        return (layer, jnp.where(t < first_lat, (t + first_lat - kv_perm) % first_lat, t), 0)

    def b_map(t):
        return (layer, jnp.where(t < first_lat, 0, (t + 1) * (W_IN_TILE // W_IN_GATE_SHIFT)), 0)

    return pl.pallas_call(
        _prep_w_in_kernel,
        grid=(N_MAIN // W_IN_TILE,),
        in_specs=[pl.BlockSpec((1, W_IN_TILE, d), a_map),
                  pl.BlockSpec((1, W_IN_GATE_SHIFT, d), b_map)],
        out_specs=[pl.BlockSpec((d, W_IN_TILE), lambda t: (0, t)),
                   pl.BlockSpec((d, LANE), lambda t: (0, 0))],
        out_shape=[jax.ShapeDtypeStruct((d, N_MAIN), BF16), jax.ShapeDtypeStruct((d, LANE), BF16)],
        compiler_params=_cparams(("arbitrary",)),
        name="prep_w_in",
    )(w_t, w_t)


def _proj_in(x, g, shift, scale, w_main, w_gate, tm, tn, n=None, b0=0, nb=None, prev=None):
    b, s, d = x.shape
    n = w_main.shape[1] if n is None else n
    nb = b if nb is None else nb
    per_batch = shift.shape[0] == b
    mod_map = (lambda bi, i, j: (b0 + bi, 0, 0)) if per_batch else (lambda bi, i, j: (0, 0, 0))
    operands = (x, g[None], shift, scale, w_main, w_gate)
    extend = () if prev is None else tuple(prev)
    return pl.pallas_call(
        _proj_kernel,
        grid=(nb, s // tm, n // tn),
        in_specs=[pl.BlockSpec((1, tm, d), lambda bi, i, j: (b0 + bi, i, 0)),
                  pl.BlockSpec((1, d), lambda bi, i, j: (0, 0)),
                  pl.BlockSpec((1, 1, d), mod_map),
                  pl.BlockSpec((1, 1, d), mod_map),
                  pl.BlockSpec((d, tn), lambda bi, i, j: (0, j)),
                  pl.BlockSpec((d, LANE), lambda bi, i, j: (0, 0))]
                 + [pl.BlockSpec(memory_space=pl.ANY)] * len(extend),
        out_specs=[pl.BlockSpec((1, tm, tn), lambda bi, i, j: (b0 + bi, i, j)),
                   pl.BlockSpec((1, tm, LANE), lambda bi, i, j: (b0 + bi, i, 0))],
        out_shape=[jax.ShapeDtypeStruct((b, s, n), BF16),
                   jax.ShapeDtypeStruct((b, s, LANE), F32)],
        scratch_shapes=[pltpu.VMEM((tm, d), BF16)],
        input_output_aliases={len(operands): 0, len(operands) + 1: 1} if extend else {},
        compiler_params=_cparams(("parallel", "parallel", "arbitrary")),
        name="proj_in",
    )(*operands, *extend)


def _softmax_av(q, keys, vals, biases):
    scores = []
    for kk, bb in zip(keys, biases):
        s = _dot_nt(q, kk)
        scores.append(s if bb is None else s + bb)
    m = scores[0].max(axis=-1, keepdims=True)
    for s in scores[1:]:
        m = jnp.maximum(m, s.max(axis=-1, keepdims=True))
    num = None
    den = None
    for s, vv in zip(scores, vals):
        e = jnp.exp(s - m)
        dsum = e.sum(axis=-1, keepdims=True)
        o = _dot(e.astype(BF16), vv)
        num = o if num is None else num + o
        den = dsum if den is None else den + dsum
    return num / den


def _na_kernel(q_ref, k_ref, v_ref, kc_ref, vc_ref, *rest, rows, kr):
    *bias_refs, o_ref = rest
    kc = kc_ref[0]
    vc = vc_ref[0]
    for j, bias_ref in enumerate(bias_refs):
        r = pl.program_id(1) * len(bias_refs) + j
        row_start = jnp.clip(r - kr // 2, 0, rows - kr)
        start = pl.multiple_of(row_start * GRID_W, GRID_W)
        n_win = kr * GRID_W
        q = q_ref[0, j * GRID_W:(j + 1) * GRID_W, :]
        kw = k_ref[0, pl.ds(start, n_win), :]
        vw = v_ref[0, pl.ds(start, n_win), :]
        o_ref[0, j * GRID_W:(j + 1) * GRID_W, :] = _na_row(q, kw, vw, kc, vc, bias_ref).astype(o_ref.dtype)


def _na_row(q, kw, vw, kc, vc, bias_ref):
    gw = NA_GROUP * NA_HEAD_DIM
    stacked = (NA_GROUP * GRID_W, gw)
    on_head = (lax.broadcasted_iota(jnp.int32, stacked, 0) // GRID_W
               == lax.broadcasted_iota(jnp.int32, stacked, 1) // NA_HEAD_DIM)
    outs = []
    for g in range(NA_HEADS // NA_GROUP):
        sl = slice(g * gw, (g + 1) * gw)
        q_all = jnp.where(on_head, jnp.concatenate([q[:, sl]] * NA_GROUP, axis=0), jnp.zeros((), q.dtype))
        bias = bias_ref[0, g * NA_GROUP * GRID_W:(g + 1) * NA_GROUP * GRID_W, :]
        o_all = _softmax_av(q_all, [kw[:, sl], kc[:, sl]], [vw[:, sl], vc[:, sl]], [bias, None])
        o_all = jnp.where(on_head, o_all, 0.0).reshape(NA_GROUP, GRID_W, gw)
        outs.append(o_all.sum(axis=0))
    return jnp.concatenate(outs, axis=-1)


def _na_bias_table(rpb, rows, kr):
    col = np.arange(GRID_W)
    col_start = np.clip(col - NA_WIN_C // 2, 0, GRID_W - NA_WIN_C)
    col_ok = (col[None, :] >= col_start[:, None]) & (col[None, :] < col_start[:, None] + NA_WIN_C)
    d_col = np.clip(col[None, :] - col[:, None], -(NA_WIN_C - 1), NA_WIN_C - 1) + NA_WIN_C - 1
    n_dr, n_dc = rpb.shape[1], rpb.shape[2]
    onehot = jnp.asarray((d_col.reshape(-1)[None, :] == np.arange(n_dc)[:, None]).astype(np.float32))
    by_col = jnp.dot(rpb.astype(F32).reshape(NA_HEADS * n_dr, n_dc), onehot, precision=lax.Precision.HIGHEST)
    by_col = by_col.reshape(NA_HEADS, n_dr, GRID_W, GRID_W)
    by_col = jnp.where(col_ok[None, None], by_col, NEG_INF)
    tables = []
    for o in range(kr):
        lo = NA_WIN_R - 1 - o
        tables.append(by_col[:, lo:lo + kr].transpose(0, 2, 1, 3).reshape(NA_HEADS, GRID_W, kr * GRID_W))
    return jnp.stack(tables).reshape(kr, NA_HEADS * GRID_W, kr * GRID_W)


def _na_latent(main, main_ctx, rpb):
    b, s, _ = main.shape
    sc = main_ctx.shape[1]
    rows = s // GRID_W
    kr = min(NA_WIN_R, rows)
    bias = _na_bias_table(rpb, rows, kr)
    w = NA_WIDTH

    per_step = next(n for n in (8, 4, 2, 1) if rows % n == 0)

    def bias_spec(j):
        def bias_map(bi, i):
            r = i * per_step + j
            return (r - jnp.clip(r - kr // 2, 0, rows - kr), 0, 0)
        return pl.BlockSpec((1, NA_HEADS * GRID_W, kr * GRID_W), bias_map)

    return pl.pallas_call(
        functools.partial(_na_kernel, rows=rows, kr=kr),
        grid=(b, rows // per_step),
        in_specs=[pl.BlockSpec((1, per_step * GRID_W, w), lambda bi, i: (bi, i, OFF_Q_NA // w)),
                  pl.BlockSpec((1, s, w), lambda bi, i: (bi, 0, OFF_K_NA // w)),
                  pl.BlockSpec((1, s, w), lambda bi, i: (bi, 0, OFF_V_NA // w)),
                  pl.BlockSpec((1, sc, w), lambda bi, i: (bi, 0, OFF_K_NA // w)),
                  pl.BlockSpec((1, sc, w), lambda bi, i: (bi, 0, OFF_V_NA // w))]
                 + [bias_spec(j) for j in range(per_step)],
        out_specs=pl.BlockSpec((1, per_step * GRID_W, w), lambda bi, i: (bi, i, 0)),
        out_shape=jax.ShapeDtypeStruct((b, s, w), BF16),
        compiler_params=_cparams(("parallel", "arbitrary")),
        name="na_latent",
    )(main, main, main, main_ctx, main_ctx, *([bias] * per_step))


def _dense_attn_kernel(q_ref, k_ref, v_ref, o_ref):
    q = q_ref[0]
    k = k_ref[0]
    v = v_ref[0]
    outs = []
    for h in range(NA_HEADS):
        sl = slice(h * NA_HEAD_DIM, (h + 1) * NA_HEAD_DIM)
        outs.append(_softmax_av(q[:, sl], [k[:, sl]], [v[:, sl]], [None]))
    o_ref[0] = jnp.concatenate(outs, axis=-1).astype(o_ref.dtype)


def _dense_attn(main_ctx):
    b, sc, _ = main_ctx.shape
    w = NA_WIDTH
    return pl.pallas_call(
        _dense_attn_kernel,
        grid=(b,),
        in_specs=[pl.BlockSpec((1, sc, w), lambda bi: (bi, 0, OFF_Q_NA // w)),
                  pl.BlockSpec((1, sc, w), lambda bi: (bi, 0, OFF_K_NA // w)),
                  pl.BlockSpec((1, sc, w), lambda bi: (bi, 0, OFF_V_NA // w))],
        out_specs=pl.BlockSpec((1, sc, w), lambda bi: (bi, 0, 0)),
        out_shape=jax.ShapeDtypeStruct((b, sc, w), BF16),
        compiler_params=_cparams(("parallel",)),
        name="ctx_attn",
    )(main_ctx, main_ctx, main_ctx)


HALO = 16


def _gla_matrices(reverse):
    c = GLA_C
    t = np.arange(c)[:, None]
    m = np.arange(c)[None, :]
    blocks = [m <= t, m > t]
    for b in GLA_LEVELS:
        first = (t // (2 * b)) * (2 * b) + b
        is_q = (t & b) != 0
        blocks.append(np.where(is_q, (m > first) & (m <= t), (m > t) & (m <= first)))
    mats = np.stack(blocks).astype(np.float32)
    if reverse:
        mats = mats[:, ::-1, ::-1]
    mats = mats.reshape(-1, c)
    return jnp.asarray(np.concatenate([mats, mats], axis=1), dtype=BF16)


def _split_bf16(x):
    hi = x.astype(BF16)
    return hi, (x - hi.astype(F32)).astype(BF16)


def _pair_block_diag(x):
    lane = lax.broadcasted_iota(jnp.int32, x.shape, 1)
    zero = jnp.zeros((), x.dtype)
    return jnp.concatenate([jnp.where(lane < GLA_DK, x, zero), jnp.where(lane >= GLA_DK, x, zero)], axis=0)


def _gla_kernel(*refs, emit):
    n_in = 9 if emit else 8
    n_out = 2 if emit else 1
    ins, outs, scratch = refs[:2 * n_in], refs[2 * n_in:2 * (n_in + n_out)], refs[2 * (n_in + n_out):]
    ins = [ins[d * n_in:(d + 1) * n_in] for d in range(2)]
    outs = [outs[d * n_out:(d + 1) * n_out] for d in range(2)]
    step = pl.program_id(1)

    @pl.when(step == 0)
    def _():
        for d in range(2):
            scratch[d][...] = ins[d][-1][0]

    pending = [_gla_direction(ins[d][:-1], outs[d][:-1], scratch[d], reverse, emit)
               for d, reverse in enumerate((False, True))]
    while pending:
        pending = [stages for stages in pending if next(stages, "done") != "done"]

    @pl.when(step == pl.num_programs(1) - 1)
    def _():
        for d in range(2):
            outs[d][-1][0] = scratch[d][...]


def _gla_direction(ins, outs, st_ref, reverse, emit):
    if emit:
        q_ref, k_ref, v_ref, gt_ref, w2a_ref, w2b_ref, b2_ref, a_ref = ins
        (o_ref,) = outs
    else:
        k_ref, v_ref, gt_ref, w2a_ref, w2b_ref, b2_ref, a_ref = ins
    c = GLA_C
    pw = 2 * GLA_DK

    lr_hi, lr_lo = _split_bf16(gt_ref[0])
    logit = (_dot(jnp.concatenate([lr_hi, lr_lo], axis=1), w2a_ref[...]) + _dot(lr_hi, w2b_ref[...])
             + b2_ref[...])
    g = (jnp.minimum(logit, 0.0) - jnp.log1p(jnp.exp(-jnp.abs(logit)))) * (LOG2_E / GLA_GATE_TAU)
    g_hi, g_lo = _split_bf16(g)
    args = _dot(a_ref[...], jnp.concatenate([g_hi, g_lo], axis=0))
    cum = args[0:c]
    rem = args[c:2 * c]
    last_row = cum[0:1] if reverse else cum[c - 1:c]
    yield

    k = k_ref[0].astype(F32)
    v = v_ref[0]
    atts = []
    if emit:
        q = q_ref[0].astype(F32) * (GLA_DK ** -0.5)
        row_t = lax.broadcasted_iota(jnp.int32, (c, pw), 0)
        si = lax.broadcasted_iota(jnp.int32, (2 * c, c), 0) & (c - 1)
        ti = lax.broadcasted_iota(jnp.int32, (2 * c, c), 1)
        if reverse:
            row_t, ti, si = c - 1 - row_t, c - 1 - ti, c - 1 - si
        for hp in range(GLA_HEADS // 2):
            cs = slice(hp * pw, (hp + 1) * pw)
            qp, kp = q[:, cs], k[:, cs]
            att = jnp.where(ti == si, _dot_nt(_pair_block_diag(kp.astype(BF16)), qp.astype(BF16)), 0.0)
            for l, b in enumerate(GLA_LEVELS):
                x = (jnp.exp2(args[(2 + l) * c:(3 + l) * c, cs])
                     * jnp.where((row_t & b) != 0, qp, kp)).astype(BF16)
                pair = (((ti ^ si) >> (b.bit_length() - 1)) == 1) & ((ti & b) != 0)
                att = jnp.where(pair, _dot_nt(_pair_block_diag(x), x), att)
                yield
            atts.append(att.astype(BF16))

    outs = []
    for h in range(GLA_HEADS):
        sl = slice(h * GLA_DK, (h + 1) * GLA_DK)
        kh = k[:, sl]
        vh = v[:, h * GLA_DV:(h + 1) * GLA_DV]
        state = st_ref[h]
        kd = (kh * jnp.exp2(rem[:, sl])).astype(BF16)
        decay = jnp.exp2(jnp.broadcast_to(last_row[:, sl], (GLA_DK, GLA_DK))).T
        decay = jnp.concatenate([decay] * (GLA_DV // GLA_DK), axis=1)
        if emit:
            qd = (q[:, sl] * jnp.exp2(cum[:, sl])).astype(BF16)
            att_t = atts[h // 2][(h % 2) * c:(h % 2 + 1) * c]
            both = _dot_tn(jnp.concatenate([att_t, kd], axis=1), vh)
            outs.append(_dot(qd, state.astype(BF16)) + both[:c])
            st_ref[h] = decay * state + both[c:]
        else:
            st_ref[h] = decay * state + _dot_tn(kd, vh)
        yield

    if emit:
        o_ref[0] = jnp.concatenate(outs, axis=-1).astype(o_ref.dtype)


def _gla_scan(main, gate, gate_ws, s0s, emit):
    b, l, _ = main.shape
    n = l // GLA_C
    const = lambda arr: pl.BlockSpec(arr.shape, lambda bi, s: (0,) * arr.ndim)
    state_spec = pl.BlockSpec((1, GLA_HEADS, GLA_DK, GLA_DV), lambda bi, s: (bi, 0, 0, 0))
    state_shape = jax.ShapeDtypeStruct((b, GLA_HEADS, GLA_DK, GLA_DV), F32)
    in_specs, args, out_specs, out_shape = [], [], [], []
    for reverse in (False, True):
        amat = _gla_matrices(reverse)
        w2a, w2b, b2 = gate_ws[reverse]

        def col(block, reverse=reverse):
            return lambda bi, s: (bi, n - 1 - s if reverse else s, block)

        if emit:
            in_specs.append(pl.BlockSpec((1, GLA_C, GLA_KEY_WIDTH), col(OFF_Q_GLA // GLA_KEY_WIDTH)))
            args.append(main)
            out_specs.append(pl.BlockSpec((1, GLA_C, GLA_VAL_WIDTH), col(0)))
            out_shape.append(jax.ShapeDtypeStruct((b, l, GLA_VAL_WIDTH), BF16))
        in_specs += [pl.BlockSpec((1, GLA_C, GLA_KEY_WIDTH), col(OFF_K_GLA // GLA_KEY_WIDTH)),
                     pl.BlockSpec((1, GLA_C, GLA_VAL_WIDTH), col(OFF_V_GLA // GLA_VAL_WIDTH)),
                     pl.BlockSpec((1, GLA_C, LANE), col(0)),
                     const(w2a), const(w2b), const(b2), const(amat), state_spec]
        args += [main, main, gate, w2a, w2b, b2, amat, s0s[reverse]]
        out_specs.append(state_spec)
        out_shape.append(state_shape)
    res = pl.pallas_call(
        functools.partial(_gla_kernel, emit=emit),
        grid=(b, n),
        in_specs=in_specs,
        out_specs=out_specs,
        out_shape=out_shape,
        scratch_shapes=[pltpu.VMEM((GLA_HEADS, GLA_DK, GLA_DV), F32)] * 2,
        compiler_params=_cparams(("parallel", "arbitrary")),
        name="gla_scan",
    )(*args)
    return (res[0], res[2], res[1], res[3]) if emit else (None, None, res[0], res[1])


def _gla_gate_weights(gate_w, gate_b):
    out = []
    for dr in range(2):
        w = jnp.zeros((LANE, GLA_KEY_WIDTH), F32)
        w = w.at[dr * GLA_GATE_RANK:(dr + 1) * GLA_GATE_RANK].set(gate_w[dr])
        w_hi = w.astype(BF16)
        w_lo = (w - w_hi.astype(F32)).astype(BF16)
        out.append((jnp.concatenate([w_hi, w_hi], axis=0), w_lo, gate_b[dr][None]))
    return out


def _conv_tile(b_ref, c_ref, x_ref, cp_ref, xp_ref, cn_ref, xn_ref, w_ref):
    i, n = pl.program_id(1), pl.num_programs(1)
    u = c_ref[0].astype(F32) * x_ref[0].astype(F32)
    tm = u.shape[0]
    before = jnp.where(i > 0, 1.0, 0.0) * (cp_ref[0, HALO - 1:HALO].astype(F32) * xp_ref[0, HALO - 1:HALO].astype(F32))
    after = jnp.where(i < n - 1, 1.0, 0.0) * (cn_ref[0, 0:1].astype(F32) * xn_ref[0, 0:1].astype(F32))
    t = lax.broadcasted_iota(jnp.int32, u.shape, 0)
    prev = jnp.where(t == 0, before, pltpu.roll(u, 1, axis=0))
    nxt = jnp.where(t == tm - 1, after, pltpu.roll(u, tm - 1, axis=0))
    w = w_ref[...]
    return b_ref[0].astype(F32) * (prev * w[0:1] + u * w[1:2] + nxt * w[2:3])


def _merge_kernel(ona_ref, bsc_ref, csc_ref, xsc_ref, cp_ref, xp_ref, cn_ref, xn_ref, cw_ref, of_ref, ob_ref,
                  r_ref, gna_ref, gsc_ref, ggl_ref, x_ref, gt_ref,
                  gn_ref, wna_ref, wsc_ref, wgl_ref, wo_ref, g2_ref, sh2_ref, sc2_ref, wr_ref, *rest):
    xo_ref, h2_ref, hp_ref, lg_ref = rest[-4:]
    o_sc = _conv_tile(bsc_ref, csc_ref, xsc_ref, cp_ref, xp_ref, cn_ref, xn_ref, cw_ref).astype(BF16)
    o = of_ref[0].astype(F32) + ob_ref[0].astype(F32)
    normed = []
    for h in range(GLA_HEADS):
        oh = o[:, h * GLA_DV:(h + 1) * GLA_DV]
        ms = jnp.mean(oh * oh, axis=-1, keepdims=True)
        normed.append(oh * lax.rsqrt(ms + RMS_EPS))
    r = r_ref[0].astype(F32)
    y_gla = jnp.concatenate(normed, axis=-1) * gn_ref[...] * (r * _sigmoid(r))
    y = (_sigmoid(gna_ref[0].astype(F32)) * _dot(ona_ref[0], wna_ref[...])
         + _sigmoid(gsc_ref[0].astype(F32)) * _dot(o_sc, wsc_ref[...])
         + _sigmoid(ggl_ref[0].astype(F32)) * _dot(y_gla.astype(BF16), wgl_ref[...]))
    xn = x_ref[0] + gt_ref[0] * _dot(y.astype(BF16), wo_ref[...])
    xo_ref[0] = xn
    ms = jnp.mean(xn * xn, axis=-1, keepdims=True)
    h2 = xn * lax.rsqrt(ms + RMS_EPS) * g2_ref[...] * (1.0 + sc2_ref[0]) + sh2_ref[0]
    h2b = h2.astype(BF16)
    h2_ref[0] = h2b
    _store_parts(hp_ref, _pack_rows(h2))
    lg_ref[...] = _dot(h2b, wr_ref[...])


def _merge(o_na, conv_w, o_f, o_b, main, x, gt1, gn, w_na, w_sc, w_gla, w_out, g2, sh2, sc2, w_router, tm,
           n_routed, tok_off, routed=None):
    b, s, d = x.shape
    per_batch = gt1.shape[0] == b
    mod_map = (lambda bi, i: (bi, 0, 0)) if per_batch else (lambda bi, i: (0, 0, 0))
    tok = lambda width, blk: pl.BlockSpec((1, tm, width), lambda bi, i: (bi, i, blk))
    full = lambda arr: pl.BlockSpec(arr.shape, lambda bi, i: (0,) * arr.ndim)
    mod = pl.BlockSpec((1, 1, d), mod_map)
    gn_t = jnp.tile(gn, GLA_HEADS)[None]
    g2_t = g2[None]
    per_tile = tm // HALO
    last_halo = s // HALO - 1
    halo_prev = lambda blk: pl.BlockSpec(
        (1, HALO, SC_WIDTH), lambda bi, i: (bi, jnp.maximum(i * per_tile - 1, 0), blk))
    halo_next = lambda blk: pl.BlockSpec(
        (1, HALO, SC_WIDTH), lambda bi, i: (bi, jnp.minimum((i + 1) * per_tile, last_halo), blk))
    col_b, col_c, col_x = OFF_B_SC // SC_WIDTH, OFF_C_SC // SC_WIDTH, OFF_X_SC // SC_WIDTH
    extend = () if routed is None else tuple(routed)
    operands = (o_na, main, main, main, main, main, main, main, conv_w, o_f, o_b, main, main, main, main, x, gt1,
                gn_t, w_na, w_sc, w_gla, w_out, g2_t, sh2, sc2, w_router)
    n_in = len(operands)
    nblk = s // tm
    blk0 = tok_off // tm
    return pl.pallas_call(
        _merge_kernel,
        grid=(b, s // tm),
        in_specs=[tok(NA_WIDTH, 0), tok(SC_WIDTH, col_b), tok(SC_WIDTH, col_c), tok(SC_WIDTH, col_x),
                  halo_prev(col_c), halo_prev(col_x), halo_next(col_c), halo_next(col_x), full(conv_w),
                  tok(GLA_VAL_WIDTH, 0), tok(GLA_VAL_WIDTH, 0),
                  tok(d, OFF_R_GLA // d), tok(d, OFF_MERGE // d), tok(d, OFF_MERGE // d + 1),
                  tok(d, OFF_MERGE // d + 2), tok(d, 0), mod,
                  full(gn_t), full(w_na), full(w_sc), full(w_gla), full(w_out), full(g2_t), mod, mod,
                  full(w_router)] + [pl.BlockSpec(memory_space=pl.ANY)] * len(extend),
        out_specs=[tok(d, 0), tok(d, 0),
                   pl.BlockSpec((SC_PARTS, tm, SC_ROW), lambda bi, i: (0, blk0 + bi * nblk + i, 0)),
                   pl.BlockSpec((tm, LANE), lambda bi, i: (blk0 + bi * nblk + i, 0))],
        out_shape=[jax.ShapeDtypeStruct((b, s, d), F32),
                   jax.ShapeDtypeStruct((b, s, d), BF16),
                   jax.ShapeDtypeStruct((SC_PARTS, n_routed, SC_ROW), U32),
                   jax.ShapeDtypeStruct((n_routed, LANE), F32)],
        input_output_aliases={n_in: 2, n_in + 1: 3} if extend else {},
        compiler_params=_cparams(("parallel", "parallel")),
        name="merge",
    )(*operands, *extend)


def _router_kernel(lg_ref, br_ref, tri_ref, eidx_ref, gw_ref, rank_ref, cnt_ref, carry_ref):
    tm = lg_ref.shape[0]

    @pl.when(pl.program_id(0) == 0)
    def _():
        carry_ref[...] = jnp.zeros_like(carry_ref)

    scores = _sigmoid(lg_ref[...].T[:N_EXPERTS])
    sel = scores + br_ref[...]
    neg = -jnp.inf

    sel3 = sel.reshape(N_EXPERT_GROUPS, GROUP_SIZE, tm)
    i3 = lax.broadcasted_iota(jnp.int32, sel3.shape, 1)
    m1 = sel3.max(axis=1, keepdims=True)
    first = jnp.where(sel3 == m1, i3, GROUP_SIZE).min(axis=1, keepdims=True)
    m2 = jnp.where(i3 == first, neg, sel3).max(axis=1, keepdims=True)
    gscore = (m1 + m2)[:, 0, :]

    gi = lax.broadcasted_iota(jnp.int32, gscore.shape, 0)
    gmask = jnp.zeros(gscore.shape, jnp.bool_)
    for _ in range(TOPK_GROUPS):
        m = gscore.max(axis=0, keepdims=True)
        pick = gi == jnp.where(gscore == m, gi, N_EXPERT_GROUPS).min(axis=0, keepdims=True)
        gmask = gmask | pick
        gscore = jnp.where(pick, neg, gscore)
    emask = jnp.broadcast_to(gmask[:, None, :], sel3.shape).reshape(N_EXPERTS, tm)
    sel = jnp.where(emask, sel, neg)

    ei = lax.broadcasted_iota(jnp.int32, sel.shape, 0)
    picks, idxs, ws = [], [], []
    for _ in range(TOP_K):
        m = sel.max(axis=0, keepdims=True)
        idx = jnp.where(sel == m, ei, N_EXPERTS).min(axis=0, keepdims=True)
        pick = ei == idx
        picks.append(pick)
        idxs.append(idx)
        ws.append(jnp.where(pick, scores, 0.0).sum(axis=0, keepdims=True))
        sel = jnp.where(pick, neg, sel)
    w = jnp.concatenate(ws, axis=0)
    gw_ref[...] = w / w.sum(axis=0, keepdims=True) * ROUTED_SCALE
    eidx_ref[...] = jnp.concatenate(idxs, axis=0)

    onehot = picks[0]
    for p in picks[1:]:
        onehot = onehot | p
    onehot = jnp.where(onehot, 1.0, 0.0).astype(BF16)
    before = _dot(onehot, tri_ref[...]) + jnp.tile(carry_ref[...], (1, tm // LANE))
    rank_ref[...] = jnp.concatenate(
        [jnp.where(p, before, 0.0).sum(axis=0, keepdims=True) for p in picks], axis=0).astype(jnp.int32)
    carry_ref[...] += _dot(onehot, jnp.ones((tm, LANE), BF16))
    cnt_ref[...] = carry_ref[...]


def _route(logits, b_router, tm=512):
    t = logits.shape[0]
    br = jnp.broadcast_to(b_router.astype(F32)[:, None], (N_EXPERTS, tm))
    tri = jnp.asarray(np.triu(np.ones((tm, tm), np.float32), 1), dtype=BF16)
    kt = lambda dt: jax.ShapeDtypeStruct((TOP_K, t), dt)
    eidx, gw, rank, cnt = pl.pallas_call(
        _router_kernel,
        grid=(t // tm,),
        in_specs=[pl.BlockSpec((tm, LANE), lambda i: (i, 0)),
                  pl.BlockSpec((N_EXPERTS, tm), lambda i: (0, 0)),
                  pl.BlockSpec((tm, tm), lambda i: (0, 0))],
        out_specs=[pl.BlockSpec((TOP_K, tm), lambda i: (0, i)),
                   pl.BlockSpec((TOP_K, tm), lambda i: (0, i)),
                   pl.BlockSpec((TOP_K, tm), lambda i: (0, i)),
                   pl.BlockSpec((N_EXPERTS, LANE), lambda i: (0, 0))],
        out_shape=[kt(jnp.int32), kt(F32), kt(jnp.int32),
                   jax.ShapeDtypeStruct((N_EXPERTS, LANE), F32)],
        scratch_shapes=[pltpu.VMEM((N_EXPERTS, LANE), F32)],
        compiler_params=_cparams(("arbitrary",)),
        name="router",
    )(logits, br, tri)
    return eidx, gw, rank, cnt[:, 0].astype(jnp.int32)


def _sc_mesh():
    return plsc.VectorSubcoreMesh(core_axis_name="core", subcore_axis_name="subcore")


def _dispatch_rows(xp, dest, slots):
    parts, t, _ = xp.shape

    @pl.kernel(out_type=jax.ShapeDtypeStruct((parts, slots, SC_ROW), xp.dtype), mesh=_sc_mesh(),
               scratch_types=[], name="moe_dispatch")
    def run(x_hbm, d_hbm, o_hbm):
        for part in range(parts):
            out_part = o_hbm.at[part]

            def body(x_vmem, d_vmem, out_part=out_part):
                for k in range(TOP_K):
                    pltpu.sync_copy(x_vmem, out_part.at[d_vmem.at[k]])

            pltpu.emit_pipeline(
                body,
                grid=(t // SC_WINDOW,),
                in_specs=[pl.BlockSpec((SC_WINDOW, SC_ROW), lambda i: (i, 0)),
                          pl.BlockSpec((TOP_K, SC_WINDOW), lambda i: (0, i))],
                out_specs=[],
                core_axis_name=("core", "subcore"),
                dimension_semantics=(pltpu.PARALLEL,),
            )(x_hbm.at[part], d_hbm)

    return run(xp, dest)


def _gather_rows(yp, dest, t0, n):
    parts = yp.shape[0]
    nwin = n // SC_WINDOW
    win0 = t0 // SC_WINDOW

    @pl.kernel(out_type=jax.ShapeDtypeStruct((parts, TOP_K * n, SC_ROW), yp.dtype), mesh=_sc_mesh(),
               scratch_types=[], name="moe_gather")
    def run(y_hbm, d_hbm, o_hbm):
        for part in range(parts):
            table = y_hbm.at[part]

            def body(d_vmem, o_vmem, table=table):
                pltpu.sync_copy(table.at[d_vmem.at[0]], o_vmem)

            pltpu.emit_pipeline(
                body,
                grid=(TOP_K, nwin),
                in_specs=[pl.BlockSpec((1, SC_WINDOW), lambda k, j: (k, win0 + j))],
                out_specs=[pl.BlockSpec((SC_WINDOW, SC_ROW), lambda k, j: (k * nwin + j, 0))],
                core_axis_name=("core", "subcore"),
                dimension_semantics=(pltpu.PARALLEL, pltpu.PARALLEL),
            )(d_hbm, o_hbm.at[part])

    return run(yp, dest).reshape(parts, TOP_K, n, SC_ROW)


def _expert_kernel(be_ref, bv_ref, bs_ref, x_ref, wg_ref, wu_ref, wd_ref, o_ref, wg_s, wu_s, wd_s):
    i = pl.program_id(0)
    valid = bv_ref[i]
    new_expert = (i == 0) | (be_ref[i] != be_ref[jnp.maximum(i - 1, 0)])

    @pl.when(new_expert)
    def _():
        wg_s[...] = wg_ref[0, 0].astype(BF16)
        wu_s[...] = wu_ref[0, 0].astype(BF16)
        wd_s[...] = wd_ref[0, 0].astype(BF16)

    @pl.when(valid > 0)
    def _():
        w = _load_parts(x_ref)
        row = lax.broadcasted_iota(jnp.int32, w.shape, 0)
        w = jnp.where(row < valid, w, jnp.uint32(0))
        lo, hi = _unpack_rows(w)
        x = jnp.concatenate([lo, hi], axis=1).astype(BF16)
        a = _dot(x, wg_s[...])
        hid = a * _sigmoid(a) * _dot(x, wu_s[...])
        _store_parts(o_ref, _pack_rows(_dot(hid.astype(BF16), wd_s[...])))


def _experts(xs, blk_e, blk_valid, blk_src, layer, w_gate, w_up, w_down):
    parts, slots, _ = xs.shape
    d = D_MODEL
    nb = slots // MOE_BLOCK
    return pl.pallas_call(
        _expert_kernel,
        grid_spec=pltpu.PrefetchScalarGridSpec(
            num_scalar_prefetch=3,
            grid=(nb,),
            in_specs=[pl.BlockSpec((parts, MOE_BLOCK, SC_ROW), lambda i, be, bv, bs: (0, bs[i], 0)),
                      pl.BlockSpec((1, 1, d, EXPERT_FF), lambda i, be, bv, bs: (layer, be[i], 0, 0)),
                      pl.BlockSpec((1, 1, d, EXPERT_FF), lambda i, be, bv, bs: (layer, be[i], 0, 0)),
                      pl.BlockSpec((1, 1, EXPERT_FF, d), lambda i, be, bv, bs: (layer, be[i], 0, 0))],
            out_specs=pl.BlockSpec((parts, MOE_BLOCK, SC_ROW), lambda i, be, bv, bs: (0, bs[i], 0)),
            scratch_shapes=[pltpu.VMEM((d, EXPERT_FF), BF16), pltpu.VMEM((d, EXPERT_FF), BF16),
                            pltpu.VMEM((EXPERT_FF, d), BF16)]),
        out_shape=jax.ShapeDtypeStruct((parts, slots, SC_ROW), U32),
        compiler_params=_cparams(("arbitrary",)),
        name="experts",
    )(blk_e, blk_valid, blk_src, xs, w_gate, w_up, w_down)


def _combine_kernel(yg_ref, gw_ref, h_ref, x_ref, gt_ref, wsg_ref, wsu_ref, wsd_ref, gf_ref, o_ref, *, final):
    h = h_ref[0]
    a = _dot(h, wsg_ref[...])
    hid = a * _sigmoid(a) * _dot(h, wsu_ref[...])
    y = _dot(hid.astype(BF16), wsd_ref[...])
    gw = gw_ref[...]
    y_lo = y[:, :D_MODEL // 2]
    y_hi = y[:, D_MODEL // 2:]
    for k in range(TOP_K):
        lo, hi = _unpack_rows(_load_parts(yg_ref, k))
        y_lo = y_lo + gw[:, k:k + 1] * lo
        y_hi = y_hi + gw[:, k:k + 1] * hi
    y = jnp.concatenate([y_lo, y_hi], axis=1)
    xn = x_ref[0] + gt_ref[0] * y
    if final:
        ms = jnp.mean(xn * xn, axis=-1, keepdims=True)
        xn = xn * lax.rsqrt(ms + RMS_EPS) * gf_ref[...]
    o_ref[0] = xn


def _combine(yg, gw, tok_off, h2, x, gt2, b0, nb, ws_gate, ws_up, ws_down, g_final, final, tm):
    b, s, d = x.shape
    per_batch = gt2.shape[0] == b
    mod_map = (lambda bi, i: (b0 + bi, 0, 0)) if per_batch else (lambda bi, i: (0, 0, 0))
    full = lambda arr: pl.BlockSpec(arr.shape, lambda bi, i: (0,) * arr.ndim)
    tok = lambda width: pl.BlockSpec((1, tm, width), lambda bi, i: (b0 + bi, i, 0))
    gf = g_final[None]
    nblk = s // tm
    blk0 = (tok_off + b0 * s) // tm
    return pl.pallas_call(
        functools.partial(_combine_kernel, final=final),
        grid=(nb, nblk),
        in_specs=[pl.BlockSpec((SC_PARTS, TOP_K, tm, SC_ROW), lambda bi, i: (0, 0, bi * nblk + i, 0)),
                  pl.BlockSpec((tm, TOP_K), lambda bi, i: (blk0 + bi * nblk + i, 0)),
                  tok(d), tok(d),
                  pl.BlockSpec((1, 1, d), mod_map),
                  full(ws_gate), full(ws_up), full(ws_down), full(gf)],
        out_specs=tok(d),
        out_shape=jax.ShapeDtypeStruct((b, s, d), F32),
        input_output_aliases={3: 0},
        compiler_params=_cparams(("parallel", "parallel")),
        name="combine",
    )(yg, gw, h2, x, gt2, ws_gate, ws_up, ws_down, gf)


def _project_latent(x, p, mods, b0=0, nb=None, prev=None):
    w_main, w_gate = p['w_in_prepped']
    return _proj_in(x, p['g_norm1'], mods[0], mods[1], w_main, w_gate, tm=min(2048, x.shape[1]), tn=1024,
                    b0=b0, nb=nb, prev=prev)


def _layer(x, ctx_s, mods, mods_ctx, p, ctx_out, final, g_final, projected=None, after_piece=None):
    b, s, d = x.shape
    sc = ctx_s.shape[1]
    sh1, sc1, gt1, sh2, sc2, gt2 = mods
    csh1, csc1, cgt1, csh2, csc2, cgt2 = mods_ctx

    w_main, w_gate = p['w_in_prepped']
    main, gate = _project_latent(x, p, mods) if projected is None else projected
    ctx_flat = ctx_s.reshape(1, b * sc, d)
    n_ctx, tn_ctx = (N_MAIN, 1024) if ctx_out else (N_KV_MAIN, N_KV_MAIN // 2)
    main_c, gate_c = _proj_in(ctx_flat, p['g_norm1'], csh1, csc1, w_main, w_gate, tm=min(1024, b * sc),
                              tn=tn_ctx, n=n_ctx)
    main_c = main_c.reshape(b, sc, n_ctx)
    gate_c = gate_c.reshape(b, sc, LANE)

    o_na = _na_latent(main, main_c, p['na_rpb'])

    gate_ws = _gla_gate_weights(p['gla_gate_w'], p['gla_gate_b'])
    s0 = jnp.zeros((b, GLA_HEADS, GLA_DK, GLA_DV), F32)
    o_cf, o_cb, st_f, st_b = _gla_scan(main_c, gate_c, gate_ws, (s0, s0), ctx_out)
    o_f, o_b, _, _ = _gla_scan(main, gate, gate_ws, (st_f, st_b), True)

    w_na = p['w_branch_na'].astype(BF16)
    w_sc = p['w_branch_sc'].astype(BF16)
    w_gla = p['w_branch_gla'].astype(BF16)
    w_out = p['w_out'].astype(BF16)
    w_router = jnp.pad(p['w_router'], ((0, 0), (0, LANE - N_EXPERTS))).astype(BF16)
    n_lat = b * s
    t = n_lat + (b * sc if ctx_out else 0)
    x, h2, hp_all, lg_all = _merge(o_na, p['conv_w'], o_f, o_b, main, x, gt1, p['gla_norm_g'], w_na, w_sc, w_gla,
                                   w_out, p['g_norm2'], sh2, sc2, w_router, tm=min(512, s), n_routed=t, tok_off=0)
    if ctx_out:
        o_na_c = _dense_attn(main_c)
        ctx_s, h2_c, hp_all, lg_all = _merge(o_na_c, p['conv_w'], o_cf, o_cb, main_c, ctx_s, cgt1, p['gla_norm_g'],
                                             w_na, w_sc, w_gla, w_out, p['g_norm2'], csh2, csc2, w_router,
                                             tm=min(256, sc), n_routed=t, tok_off=n_lat, routed=(hp_all, lg_all))

    eidx, gw, rank, counts = _route(lg_all, p['b_router'])
    padded = (counts + MOE_BLOCK - 1) // MOE_BLOCK * MOE_BLOCK
    pad_end = jnp.cumsum(padded)
    pad_start = pad_end - padded
    onehot = eidx[:, :, None] == jnp.arange(N_EXPERTS, dtype=jnp.int32)
    dest = jnp.sum(jnp.where(onehot, pad_start, 0), axis=-1) + rank
    n_blocks = -(-(t * TOP_K + N_EXPERTS * (MOE_BLOCK - 1)) // MOE_BLOCK)
    slots = n_blocks * MOE_BLOCK
    blk_start = jnp.arange(n_blocks, dtype=jnp.int32) * MOE_BLOCK
    blk_e = jnp.minimum(jnp.sum(pad_end[None, :] <= blk_start[:, None], axis=1), N_EXPERTS - 1).astype(jnp.int32)
    used_end = (pad_start + counts)[blk_e]
    blk_valid = jnp.clip(used_end - blk_start, 0, MOE_BLOCK).astype(jnp.int32)
    n_used = pad_end[-1] // MOE_BLOCK
    blk_src = jnp.minimum(jnp.arange(n_blocks, dtype=jnp.int32), n_used - 1)
    blk_e = blk_e[blk_src]

    xs = _dispatch_rows(hp_all, dest, slots)
    ys = _experts(xs, blk_e, blk_valid, blk_src, p['layer'], p['w_exp_gate'], p['w_exp_up'], p['w_exp_down'])
    gw_t = gw.T
    ws_gate = p['w_sh_gate'].astype(BF16)
    ws_up = p['w_sh_up'].astype(BF16)
    ws_down = p['w_sh_down'].astype(BF16)

    def gathered(t0, n):
        return _gather_rows(ys, dest, t0, n)

    pieces = next(n for n in (4, 2, 1) if b % n == 0)
    nb = b // pieces
    for q in range(pieces):
        x = _combine(gathered(q * nb * s, nb * s), gw_t, 0, h2, x, gt2, q * nb, nb, ws_gate, ws_up, ws_down,
                     g_final, final, tm=min(256, s))
        if after_piece is not None:
            after_piece(x, q * nb, nb)
    if ctx_out:
        ctx_s = _combine(gathered(n_lat, b * sc), gw_t, n_lat, h2_c, ctx_s, cgt2, 0, b, ws_gate, ws_up, ws_down,
                         g_final, False, tm=min(256, sc))
    return x, ctx_s


def kernel(x, c, ctx, c_ctx, w_mod, b_mod, g_norm1, g_norm2, w_in, na_rpb, w_branch_na, conv_w, w_branch_sc,
           gla_gate_w, gla_gate_b, gla_norm_g, w_branch_gla, w_out, w_router, b_router, w_exp_gate, w_exp_up,
           w_exp_down, w_sh_gate, w_sh_up, w_sh_down, g_final):
    stacked = dict(g_norm1=g_norm1, g_norm2=g_norm2, na_rpb=na_rpb, w_branch_na=w_branch_na,
                   conv_w=conv_w, w_branch_sc=w_branch_sc, gla_gate_w=gla_gate_w, gla_gate_b=gla_gate_b,
                   gla_norm_g=gla_norm_g, w_branch_gla=w_branch_gla, w_out=w_out, w_router=w_router,
                   b_router=b_router,
                   w_sh_gate=w_sh_gate, w_sh_up=w_sh_up, w_sh_down=w_sh_down)
    depth = w_in.shape[0]
    layers = []
    for i in range(depth):
        p = {name: arr[i] for name, arr in stacked.items()}
        p.update(layer=i, w_in_prepped=_prep_w_in(w_in, i), w_exp_gate=w_exp_gate, w_exp_up=w_exp_up,
                 w_exp_down=w_exp_down)
        layers.append((p,) + _mod_vectors(c, c_ctx, w_mod, b_mod, i))

    ctx_s = ctx
    projected = None
    for i, (p, mods, mods_ctx) in enumerate(layers):
        last = i == depth - 1
        after_piece = None
        next_projected = []
        if not last:
            p_next, mods_next, _ = layers[i + 1]

            def after_piece(xq, b0, nb, p_next=p_next, mods_next=mods_next, acc=next_projected):
                acc.append(_project_latent(xq, p_next, mods_next, b0, nb, acc[-1] if acc else None))

        x, ctx_s = _layer(x, ctx_s, mods, mods_ctx, p, not last, last, g_final, projected, after_piece)
        projected = next_projected[-1] if next_projected else None
    return x
```

```python
import functools

import numpy as np
import jax
import jax.numpy as jnp
from jax import lax
from jax.experimental import pallas as pl
from jax.experimental.pallas import tpu as pltpu
from jax.experimental.pallas import tpu_sc as plsc

F32 = jnp.float32
BF16 = jnp.bfloat16
U32 = jnp.uint32

D_MODEL = 1024
N_MOD = 6
RMS_EPS = 1e-6
NEG_INF = -1e30
GRID_W = 64
NA_HEADS = 8
NA_HEAD_DIM = 64
NA_WIDTH = NA_HEADS * NA_HEAD_DIM
NA_WIN_R = 8
NA_WIN_C = 16
NA_GROUP = 4
SC_WIDTH = 512
GLA_HEADS = 4
GLA_KEY_WIDTH = 512
GLA_VAL_WIDTH = 1024
GLA_DK = GLA_KEY_WIDTH // GLA_HEADS
GLA_DV = GLA_VAL_WIDTH // GLA_HEADS
GLA_GATE_RANK = 16
GLA_GATE_TAU = 16.0
LOG2_E = 1.4426950408889634
N_EXPERTS = 64
N_EXPERT_GROUPS = 8
GROUP_SIZE = N_EXPERTS // N_EXPERT_GROUPS
TOPK_GROUPS = 4
TOP_K = 8
EXPERT_FF = 256
ROUTED_SCALE = 2.5
MOE_BLOCK = 1024

LANE = 128
GLA_C = 128
GLA_LEVELS = tuple(GLA_C >> (i + 1) for i in range(GLA_C.bit_length() - 1))
VMEM_LIMIT = 48 * 1024 * 1024
SC_WINDOW = 128
SC_ROW = 256
SC_PARTS = D_MODEL // 2 // SC_ROW

OFF_V_GLA = 0
OFF_K_NA = 1024
OFF_V_NA = 1536
OFF_K_GLA = 2048
N_KV_MAIN = 2560
OFF_Q_NA = 2560
OFF_B_SC = 3072
OFF_C_SC = 3584
OFF_X_SC = 4096
OFF_Q_GLA = 4608
OFF_R_GLA = 5120
OFF_MERGE = 6144
N_MAIN = 9216


def _cparams(sem, vmem=VMEM_LIMIT):
    return pltpu.CompilerParams(dimension_semantics=sem, vmem_limit_bytes=vmem)


def _dot(a, b):
    return jnp.dot(a, b, preferred_element_type=F32)


def _dot_nt(a, b):
    return lax.dot_general(a, b, (((1,), (1,)), ((), ())), preferred_element_type=F32)


def _dot_tn(a, b):
    return lax.dot_general(a, b, (((0,), (0,)), ((), ())), preferred_element_type=F32)


def _sigmoid(x):
    return 0.5 * jnp.tanh(0.5 * x) + 0.5


def _pack_rows(x):
    n = x.shape[1] // 2
    r = x.astype(BF16).astype(F32)
    lo = pltpu.bitcast(r[:, :n], U32) >> 16
    hi = pltpu.bitcast(r[:, n:], U32)
    return hi | lo


def _store_parts(ref, words):
    for part in range(SC_PARTS):
        dst = ref.at[part, 0] if len(ref.shape) == 4 else ref.at[part]
        dst[...] = words[:, part * SC_ROW:(part + 1) * SC_ROW]


def _load_parts(ref, *lead):
    return jnp.concatenate([ref[(part,) + lead] for part in range(SC_PARTS)], axis=-1)


def _unpack_rows(w):
    lo = pltpu.bitcast(w << 16, F32)
    hi = pltpu.bitcast(w & jnp.uint32(0xFFFF0000), F32)
    return lo, hi


def _mod_kernel(a_ref, w_ref, b_ref, o_ref):
    a = a_ref[...]
    a = a * _sigmoid(a)
    o_ref[...] = _dot(a.astype(BF16), w_ref[0].astype(BF16)) + b_ref[0]


def _mod_vectors(c, c_ctx, w_mod, b_mod, layer):
    b = c.shape[0]
    rows = -(-(b + 1) // 8) * 8
    a = jnp.concatenate([c, c_ctx[None], jnp.zeros((rows - b - 1, D_MODEL), F32)], axis=0)
    n = N_MOD * D_MODEL
    tn = 1536
    out = pl.pallas_call(
        _mod_kernel,
        grid=(n // tn,),
        in_specs=[pl.BlockSpec((rows, D_MODEL), lambda j: (0, 0)),
                  pl.BlockSpec((1, D_MODEL, tn), lambda j: (layer, 0, j)),
                  pl.BlockSpec((1, 1, tn), lambda j: (layer, 0, j))],
        out_specs=pl.BlockSpec((rows, tn), lambda j: (0, j)),
        out_shape=jax.ShapeDtypeStruct((rows, n), F32),
        compiler_params=_cparams(("parallel",)),
        name="mod_vectors",
    )(a, w_mod, b_mod[:, None])
    lat = out[:b].reshape(b, N_MOD, 1, D_MODEL)
    ctx = out[b].reshape(N_MOD, 1, 1, D_MODEL)
    return [lat[:, i] for i in range(N_MOD)], [ctx[i] for i in range(N_MOD)]


def _proj_kernel(x_ref, g_ref, sh_ref, sc_ref, w_ref, wg_ref, *rest):
    o_ref, og_ref, h_ref = rest[-3:]

    @pl.when(pl.program_id(2) == 0)
    def _():
        x = x_ref[0]
        ms = jnp.mean(x * x, axis=-1, keepdims=True)
        h = x * lax.rsqrt(ms + RMS_EPS) * g_ref[...] * (1.0 + sc_ref[0]) + sh_ref[0]
        hb = h.astype(BF16)
        h_ref[...] = hb
        og_ref[0] = _dot(hb, wg_ref[...])

    o_ref[0] = _dot(h_ref[...], w_ref[...]).astype(o_ref.dtype)


W_IN_TILE = 512
W_IN_GATE_SHIFT = 2 * GLA_GATE_RANK


def _prep_w_in_kernel(a_ref, b_ref, o_ref, g_ref):
    t = pl.program_id(0)
    first_lat = N_KV_MAIN // W_IN_TILE
    a = a_ref[0]

    @pl.when(t < first_lat)
    def _():
        o_ref[...] = a.T.astype(BF16)

    @pl.when(t >= first_lat)
    def _():
        moved = jnp.concatenate([a[W_IN_GATE_SHIFT:], b_ref[0]], axis=0)
        scale = jnp.where(t == first_lat, NA_HEAD_DIM ** -0.5, 1.0)
        o_ref[...] = (moved * scale).T.astype(BF16)

    @pl.when(t == first_lat)
    def _():
        head = a[:LANE]
        row = lax.broadcasted_iota(jnp.int32, head.shape, 0)
        g_ref[...] = jnp.where(row < W_IN_GATE_SHIFT, head, 0.0).T.astype(BF16)


def _prep_w_in(w_in, layer):
    d = w_in.shape[1]
    w_t = jnp.swapaxes(w_in, 1, 2)
    first_lat = N_KV_MAIN // W_IN_TILE
    kv_perm = OFF_K_NA // W_IN_TILE

    def a_map(t):
        return (layer, jnp.where(t < first_lat, (t + first_lat - kv_perm) % first_lat, t), 0)

    def b_map(t):
        return (layer, jnp.where(t < first_lat, 0, (t + 1) * (W_IN_TILE // W_IN_GATE_SHIFT)), 0)

    return pl.pallas_call(
        _prep_w_in_kernel,
        grid=(N_MAIN // W_IN_TILE,),
        in_specs=[pl.BlockSpec((1, W_IN_TILE, d), a_map),
                  pl.BlockSpec((1, W_IN_GATE_SHIFT, d), b_map)],
        out_specs=[pl.BlockSpec((d, W_IN_TILE), lambda t: (0, t)),
                   pl.BlockSpec((d, LANE), lambda t: (0, 0))],
        out_shape=[jax.ShapeDtypeStruct((d, N_MAIN), BF16), jax.ShapeDtypeStruct((d, LANE), BF16)],
        compiler_params=_cparams(("arbitrary",)),
        name="prep_w_in",
    )(w_t, w_t)


def _proj_in(x, g, shift, scale, w_main, w_gate, tm, tn, n=None, b0=0, nb=None, prev=None):
    b, s, d = x.shape
    n = w_main.shape[1] if n is None else n
    nb = b if nb is None else nb
    per_batch = shift.shape[0] == b
    mod_map = (lambda bi, i, j: (b0 + bi, 0, 0)) if per_batch else (lambda bi, i, j: (0, 0, 0))
    operands = (x, g[None], shift, scale, w_main, w_gate)
    extend = () if prev is None else tuple(prev)
    return pl.pallas_call(
        _proj_kernel,
        grid=(nb, s // tm, n // tn),
        in_specs=[pl.BlockSpec((1, tm, d), lambda bi, i, j: (b0 + bi, i, 0)),
                  pl.BlockSpec((1, d), lambda bi, i, j: (0, 0)),
                  pl.BlockSpec((1, 1, d), mod_map),
                  pl.BlockSpec((1, 1, d), mod_map),
                  pl.BlockSpec((d, tn), lambda bi, i, j: (0, j)),
                  pl.BlockSpec((d, LANE), lambda bi, i, j: (0, 0))]
                 + [pl.BlockSpec(memory_space=pl.ANY)] * len(extend),
        out_specs=[pl.BlockSpec((1, tm, tn), lambda bi, i, j: (b0 + bi, i, j)),
                   pl.BlockSpec((1, tm, LANE), lambda bi, i, j: (b0 + bi, i, 0))],
        out_shape=[jax.ShapeDtypeStruct((b, s, n), BF16),
                   jax.ShapeDtypeStruct((b, s, LANE), F32)],
        scratch_shapes=[pltpu.VMEM((tm, d), BF16)],
        input_output_aliases={len(operands): 0, len(operands) + 1: 1} if extend else {},
        compiler_params=_cparams(("parallel", "parallel", "arbitrary")),
        name="proj_in",
    )(*operands, *extend)


def _softmax_av(q, keys, vals, biases):
    scores = []
    for kk, bb in zip(keys, biases):
        s = _dot_nt(q, kk)
        scores.append(s if bb is None else s + bb)
    m = scores[0].max(axis=-1, keepdims=True)
    for s in scores[1:]:
        m = jnp.maximum(m, s.max(axis=-1, keepdims=True))
    num = None
    den = None
    for s, vv in zip(scores, vals):
        e = jnp.exp(s - m)
        dsum = e.sum(axis=-1, keepdims=True)
        o = _dot(e.astype(BF16), vv)
        num = o if num is None else num + o
        den = dsum if den is None else den + dsum
    return num / den


def _na_kernel(q_ref, k_ref, v_ref, kc_ref, vc_ref, *rest, rows, kr):
    *bias_refs, o_ref = rest
    kc = kc_ref[0]
    vc = vc_ref[0]
    for j, bias_ref in enumerate(bias_refs):
        r = pl.program_id(1) * len(bias_refs) + j
        row_start = jnp.clip(r - kr // 2, 0, rows - kr)
        start = pl.multiple_of(row_start * GRID_W, GRID_W)
        n_win = kr * GRID_W
        q = q_ref[0, j * GRID_W:(j + 1) * GRID_W, :]
        kw = k_ref[0, pl.ds(start, n_win), :]
        vw = v_ref[0, pl.ds(start, n_win), :]
        o_ref[0, j * GRID_W:(j + 1) * GRID_W, :] = _na_row(q, kw, vw, kc, vc, bias_ref).astype(o_ref.dtype)


def _na_row(q, kw, vw, kc, vc, bias_ref):
    gw = NA_GROUP * NA_HEAD_DIM
    stacked = (NA_GROUP * GRID_W, gw)
    on_head = (lax.broadcasted_iota(jnp.int32, stacked, 0) // GRID_W
               == lax.broadcasted_iota(jnp.int32, stacked, 1) // NA_HEAD_DIM)
    outs = []
    for g in range(NA_HEADS // NA_GROUP):
        sl = slice(g * gw, (g + 1) * gw)
        q_all = jnp.where(on_head, jnp.concatenate([q[:, sl]] * NA_GROUP, axis=0), jnp.zeros((), q.dtype))
        bias = bias_ref[0, g * NA_GROUP * GRID_W:(g + 1) * NA_GROUP * GRID_W, :]
        o_all = _softmax_av(q_all, [kw[:, sl], kc[:, sl]], [vw[:, sl], vc[:, sl]], [bias, None])
        o_all = jnp.where(on_head, o_all, 0.0).reshape(NA_GROUP, GRID_W, gw)
        outs.append(o_all.sum(axis=0))
    return jnp.concatenate(outs, axis=-1)


def _na_bias_table(rpb, rows, kr):
    col = np.arange(GRID_W)
    col_start = np.clip(col - NA_WIN_C // 2, 0, GRID_W - NA_WIN_C)
    col_ok = (col[None, :] >= col_start[:, None]) & (col[None, :] < col_start[:, None] + NA_WIN_C)
    d_col = np.clip(col[None, :] - col[:, None], -(NA_WIN_C - 1), NA_WIN_C - 1) + NA_WIN_C - 1
    n_dr, n_dc = rpb.shape[1], rpb.shape[2]
    onehot = jnp.asarray((d_col.reshape(-1)[None, :] == np.arange(n_dc)[:, None]).astype(np.float32))
    by_col = jnp.dot(rpb.astype(F32).reshape(NA_HEADS * n_dr, n_dc), onehot, precision=lax.Precision.HIGHEST)
    by_col = by_col.reshape(NA_HEADS, n_dr, GRID_W, GRID_W)
    by_col = jnp.where(col_ok[None, None], by_col, NEG_INF)
    tables = []
    for o in range(kr):
        lo = NA_WIN_R - 1 - o
        tables.append(by_col[:, lo:lo + kr].transpose(0, 2, 1, 3).reshape(NA_HEADS, GRID_W, kr * GRID_W))
    return jnp.stack(tables).reshape(kr, NA_HEADS * GRID_W, kr * GRID_W)


def _na_latent(main, main_ctx, bias):
    b, s, _ = main.shape
    sc = main_ctx.shape[1]
    rows = s // GRID_W
    kr = min(NA_WIN_R, rows)
    w = NA_WIDTH

    per_step = next(n for n in (8, 4, 2, 1) if rows % n == 0)

    def bias_spec(j):
        def bias_map(bi, i):
            r = i * per_step + j
            return (r - jnp.clip(r - kr // 2, 0, rows - kr), 0, 0)
        return pl.BlockSpec((1, NA_HEADS * GRID_W, kr * GRID_W), bias_map)

    return pl.pallas_call(
        functools.partial(_na_kernel, rows=rows, kr=kr),
        grid=(b, rows // per_step),
        in_specs=[pl.BlockSpec((1, per_step * GRID_W, w), lambda bi, i: (bi, i, OFF_Q_NA // w)),
                  pl.BlockSpec((1, s, w), lambda bi, i: (bi, 0, OFF_K_NA // w)),
                  pl.BlockSpec((1, s, w), lambda bi, i: (bi, 0, OFF_V_NA // w)),
                  pl.BlockSpec((1, sc, w), lambda bi, i: (bi, 0, OFF_K_NA // w)),
                  pl.BlockSpec((1, sc, w), lambda bi, i: (bi, 0, OFF_V_NA // w))]
                 + [bias_spec(j) for j in range(per_step)],
        out_specs=pl.BlockSpec((1, per_step * GRID_W, w), lambda bi, i: (bi, i, 0)),
        out_shape=jax.ShapeDtypeStruct((b, s, w), BF16),
        compiler_params=_cparams(("parallel", "arbitrary")),
        name="na_latent",
    )(main, main, main, main_ctx, main_ctx, *([bias] * per_step))


def _dense_attn_kernel(q_ref, k_ref, v_ref, o_ref):
    q = q_ref[0]
    k = k_ref[0]
    v = v_ref[0]
    outs = []
    for h in range(NA_HEADS):
        sl = slice(h * NA_HEAD_DIM, (h + 1) * NA_HEAD_DIM)
        outs.append(_softmax_av(q[:, sl], [k[:, sl]], [v[:, sl]], [None]))
    o_ref[0] = jnp.concatenate(outs, axis=-1).astype(o_ref.dtype)


def _dense_attn(main_ctx):
    b, sc, _ = main_ctx.shape
    w = NA_WIDTH
    return pl.pallas_call(
        _dense_attn_kernel,
        grid=(b,),
        in_specs=[pl.BlockSpec((1, sc, w), lambda bi: (bi, 0, OFF_Q_NA // w)),
                  pl.BlockSpec((1, sc, w), lambda bi: (bi, 0, OFF_K_NA // w)),
                  pl.BlockSpec((1, sc, w), lambda bi: (bi, 0, OFF_V_NA // w))],
        out_specs=pl.BlockSpec((1, sc, w), lambda bi: (bi, 0, 0)),
        out_shape=jax.ShapeDtypeStruct((b, sc, w), BF16),
        compiler_params=_cparams(("parallel",)),
        name="ctx_attn",
    )(main_ctx, main_ctx, main_ctx)


HALO = 16


def _gla_matrices(reverse):
    c = GLA_C
    t = np.arange(c)[:, None]
    m = np.arange(c)[None, :]
    blocks = [m <= t, m > t]
    for b in GLA_LEVELS:
        first = (t // (2 * b)) * (2 * b) + b
        is_q = (t & b) != 0
        blocks.append(np.where(is_q, (m > first) & (m <= t), (m > t) & (m <= first)))
    mats = np.stack(blocks).astype(np.float32)
    if reverse:
        mats = mats[:, ::-1, ::-1]
    mats = mats.reshape(-1, c)
    return jnp.asarray(np.concatenate([mats, mats], axis=1), dtype=BF16)


def _split_bf16(x):
    hi = x.astype(BF16)
    return hi, (x - hi.astype(F32)).astype(BF16)


def _pair_block_diag(x):
    lane = lax.broadcasted_iota(jnp.int32, x.shape, 1)
    zero = jnp.zeros((), x.dtype)
    return jnp.concatenate([jnp.where(lane < GLA_DK, x, zero), jnp.where(lane >= GLA_DK, x, zero)], axis=0)


def _gla_kernel(*refs, emit):
    n_in = 9 if emit else 8
    n_out = 2 if emit else 1
    ins, outs, scratch = refs[:2 * n_in], refs[2 * n_in:2 * (n_in + n_out)], refs[2 * (n_in + n_out):]
    ins = [ins[d * n_in:(d + 1) * n_in] for d in range(2)]
    outs = [outs[d * n_out:(d + 1) * n_out] for d in range(2)]
    step = pl.program_id(1)

    @pl.when(step == 0)
    def _():
        for d in range(2):
            scratch[d][...] = ins[d][-1][0]

    pending = [_gla_direction(ins[d][:-1], outs[d][:-1], scratch[d], reverse, emit)
               for d, reverse in enumerate((False, True))]
    while pending:
        pending = [stages for stages in pending if next(stages, "done") != "done"]

    @pl.when(step == pl.num_programs(1) - 1)
    def _():
        for d in range(2):
            outs[d][-1][0] = scratch[d][...]


def _gla_direction(ins, outs, st_ref, reverse, emit):
    if emit:
        q_ref, k_ref, v_ref, gt_ref, w2a_ref, w2b_ref, b2_ref, a_ref = ins
        (o_ref,) = outs
    else:
        k_ref, v_ref, gt_ref, w2a_ref, w2b_ref, b2_ref, a_ref = ins
    c = GLA_C
    pw = 2 * GLA_DK

    lr_hi, lr_lo = _split_bf16(gt_ref[0])
    logit = (_dot(jnp.concatenate([lr_hi, lr_lo], axis=1), w2a_ref[...]) + _dot(lr_hi, w2b_ref[...])
             + b2_ref[...])
    g = (jnp.minimum(logit, 0.0) - jnp.log1p(jnp.exp(-jnp.abs(logit)))) * (LOG2_E / GLA_GATE_TAU)
    g_hi, g_lo = _split_bf16(g)
    args = _dot(a_ref[...], jnp.concatenate([g_hi, g_lo], axis=0))
    cum = args[0:c]
    rem = args[c:2 * c]
    last_row = cum[0:1] if reverse else cum[c - 1:c]
    yield

    k = k_ref[0].astype(F32)
    v = v_ref[0]
    atts = []
    if emit:
        q = q_ref[0].astype(F32) * (GLA_DK ** -0.5)
        row_t = lax.broadcasted_iota(jnp.int32, (c, pw), 0)
        si = lax.broadcasted_iota(jnp.int32, (2 * c, c), 0) & (c - 1)
        ti = lax.broadcasted_iota(jnp.int32, (2 * c, c), 1)
        if reverse:
            row_t, ti, si = c - 1 - row_t, c - 1 - ti, c - 1 - si
        for hp in range(GLA_HEADS // 2):
            cs = slice(hp * pw, (hp + 1) * pw)
            qp, kp = q[:, cs], k[:, cs]
            att = jnp.where(ti == si, _dot_nt(_pair_block_diag(kp.astype(BF16)), qp.astype(BF16)), 0.0)
            for l, b in enumerate(GLA_LEVELS):
                x = (jnp.exp2(args[(2 + l) * c:(3 + l) * c, cs])
                     * jnp.where((row_t & b) != 0, qp, kp)).astype(BF16)
                pair = (((ti ^ si) >> (b.bit_length() - 1)) == 1) & ((ti & b) != 0)
                att = jnp.where(pair, _dot_nt(_pair_block_diag(x), x), att)
                yield
            atts.append(att.astype(BF16))

    outs = []
    for h in range(GLA_HEADS):
        sl = slice(h * GLA_DK, (h + 1) * GLA_DK)
        kh = k[:, sl]
        vh = v[:, h * GLA_DV:(h + 1) * GLA_DV]
        state = st_ref[h]
        kd = (kh * jnp.exp2(rem[:, sl])).astype(BF16)
        decay = jnp.exp2(jnp.broadcast_to(last_row[:, sl], (GLA_DK, GLA_DK))).T
        decay = jnp.concatenate([decay] * (GLA_DV // GLA_DK), axis=1)
        if emit:
            qd = (q[:, sl] * jnp.exp2(cum[:, sl])).astype(BF16)
            att_t = atts[h // 2][(h % 2) * c:(h % 2 + 1) * c]
            both = _dot_tn(jnp.concatenate([att_t, kd], axis=1), vh)
            outs.append(_dot(qd, state.astype(BF16)) + both[:c])
            st_ref[h] = decay * state + both[c:]
        else:
            st_ref[h] = decay * state + _dot_tn(kd, vh)
        yield

    if emit:
        o_ref[0] = jnp.concatenate(outs, axis=-1).astype(o_ref.dtype)


def _gla_scan(main, gate, gate_ws, s0s, emit):
    b, l, _ = main.shape
    n = l // GLA_C
    const = lambda arr: pl.BlockSpec(arr.shape, lambda bi, s: (0,) * arr.ndim)
    state_spec = pl.BlockSpec((1, GLA_HEADS, GLA_DK, GLA_DV), lambda bi, s: (bi, 0, 0, 0))
    state_shape = jax.ShapeDtypeStruct((b, GLA_HEADS, GLA_DK, GLA_DV), F32)
    in_specs, args, out_specs, out_shape = [], [], [], []
    for reverse in (False, True):
        amat = _gla_matrices(reverse)
        w2a, w2b, b2 = gate_ws[reverse]

        def col(block, reverse=reverse):
            return lambda bi, s: (bi, n - 1 - s if reverse else s, block)

        if emit:
            in_specs.append(pl.BlockSpec((1, GLA_C, GLA_KEY_WIDTH), col(OFF_Q_GLA // GLA_KEY_WIDTH)))
            args.append(main)
            out_specs.append(pl.BlockSpec((1, GLA_C, GLA_VAL_WIDTH), col(0)))
            out_shape.append(jax.ShapeDtypeStruct((b, l, GLA_VAL_WIDTH), BF16))
        in_specs += [pl.BlockSpec((1, GLA_C, GLA_KEY_WIDTH), col(OFF_K_GLA // GLA_KEY_WIDTH)),
                     pl.BlockSpec((1, GLA_C, GLA_VAL_WIDTH), col(OFF_V_GLA // GLA_VAL_WIDTH)),
                     pl.BlockSpec((1, GLA_C, LANE), col(0)),
                     const(w2a), const(w2b), const(b2), const(amat), state_spec]
        args += [main, main, gate, w2a, w2b, b2, amat, s0s[reverse]]
        out_specs.append(state_spec)
        out_shape.append(state_shape)
    res = pl.pallas_call(
        functools.partial(_gla_kernel, emit=emit),
        grid=(b, n),
        in_specs=in_specs,
        out_specs=out_specs,
        out_shape=out_shape,
        scratch_shapes=[pltpu.VMEM((GLA_HEADS, GLA_DK, GLA_DV), F32)] * 2,
        compiler_params=_cparams(("parallel", "arbitrary")),
        name="gla_scan",
    )(*args)
    return (res[0], res[2], res[1], res[3]) if emit else (None, None, res[0], res[1])


def _gla_gate_weights(gate_w, gate_b):
    out = []
    for dr in range(2):
        w = jnp.zeros((LANE, GLA_KEY_WIDTH), F32)
        w = w.at[dr * GLA_GATE_RANK:(dr + 1) * GLA_GATE_RANK].set(gate_w[dr])
        w_hi = w.astype(BF16)
        w_lo = (w - w_hi.astype(F32)).astype(BF16)
        out.append((jnp.concatenate([w_hi, w_hi], axis=0), w_lo, gate_b[dr][None]))
    return out


def _conv_tile(b_ref, c_ref, x_ref, cp_ref, xp_ref, cn_ref, xn_ref, w_ref):
    i, n = pl.program_id(1), pl.num_programs(1)
    u = c_ref[0].astype(F32) * x_ref[0].astype(F32)
    tm = u.shape[0]
    before = jnp.where(i > 0, 1.0, 0.0) * (cp_ref[0, HALO - 1:HALO].astype(F32) * xp_ref[0, HALO - 1:HALO].astype(F32))
    after = jnp.where(i < n - 1, 1.0, 0.0) * (cn_ref[0, 0:1].astype(F32) * xn_ref[0, 0:1].astype(F32))
    t = lax.broadcasted_iota(jnp.int32, u.shape, 0)
    prev = jnp.where(t == 0, before, pltpu.roll(u, 1, axis=0))
    nxt = jnp.where(t == tm - 1, after, pltpu.roll(u, tm - 1, axis=0))
    w = w_ref[...]
    return b_ref[0].astype(F32) * (prev * w[0:1] + u * w[1:2] + nxt * w[2:3])


def _merge_kernel(ona_ref, bsc_ref, csc_ref, xsc_ref, cp_ref, xp_ref, cn_ref, xn_ref, cw_ref, of_ref, ob_ref,
                  r_ref, gna_ref, gsc_ref, ggl_ref, x_ref, gt_ref,
                  gn_ref, wna_ref, wsc_ref, wgl_ref, wo_ref, g2_ref, sh2_ref, sc2_ref, wr_ref, *rest):
    xo_ref, h2_ref, hp_ref, lg_ref = rest[-4:]
    o_sc = _conv_tile(bsc_ref, csc_ref, xsc_ref, cp_ref, xp_ref, cn_ref, xn_ref, cw_ref).astype(BF16)
    o = of_ref[0].astype(F32) + ob_ref[0].astype(F32)
    normed = []
    for h in range(GLA_HEADS):
        oh = o[:, h * GLA_DV:(h + 1) * GLA_DV]
        ms = jnp.mean(oh * oh, axis=-1, keepdims=True)
        normed.append(oh * lax.rsqrt(ms + RMS_EPS))
    r = r_ref[0].astype(F32)
    y_gla = jnp.concatenate(normed, axis=-1) * gn_ref[...] * (r * _sigmoid(r))
    y = (_sigmoid(gna_ref[0].astype(F32)) * _dot(ona_ref[0], wna_ref[...])
         + _sigmoid(gsc_ref[0].astype(F32)) * _dot(o_sc, wsc_ref[...])
         + _sigmoid(ggl_ref[0].astype(F32)) * _dot(y_gla.astype(BF16), wgl_ref[...]))
    xn = x_ref[0] + gt_ref[0] * _dot(y.astype(BF16), wo_ref[...])
    xo_ref[0] = xn
    ms = jnp.mean(xn * xn, axis=-1, keepdims=True)
    h2 = xn * lax.rsqrt(ms + RMS_EPS) * g2_ref[...] * (1.0 + sc2_ref[0]) + sh2_ref[0]
    h2b = h2.astype(BF16)
    h2_ref[0] = h2b
    _store_parts(hp_ref, _pack_rows(h2))
    lg_ref[...] = _dot(h2b, wr_ref[...])


def _merge(o_na, conv_w, o_f, o_b, main, x, gt1, gn, w_na, w_sc, w_gla, w_out, g2, sh2, sc2, w_router, tm,
           n_routed, tok_off, routed=None):
    b, s, d = x.shape
    per_batch = gt1.shape[0] == b
    mod_map = (lambda bi, i: (bi, 0, 0)) if per_batch else (lambda bi, i: (0, 0, 0))
    tok = lambda width, blk: pl.BlockSpec((1, tm, width), lambda bi, i: (bi, i, blk))
    full = lambda arr: pl.BlockSpec(arr.shape, lambda bi, i: (0,) * arr.ndim)
    mod = pl.BlockSpec((1, 1, d), mod_map)
    gn_t = jnp.tile(gn, GLA_HEADS)[None]
    g2_t = g2[None]
    per_tile = tm // HALO
    last_halo = s // HALO - 1
    halo_prev = lambda blk: pl.BlockSpec(
        (1, HALO, SC_WIDTH), lambda bi, i: (bi, jnp.maximum(i * per_tile - 1, 0), blk))
    halo_next = lambda blk: pl.BlockSpec(
        (1, HALO, SC_WIDTH), lambda bi, i: (bi, jnp.minimum((i + 1) * per_tile, last_halo), blk))
    col_b, col_c, col_x = OFF_B_SC // SC_WIDTH, OFF_C_SC // SC_WIDTH, OFF_X_SC // SC_WIDTH
    extend = () if routed is None else tuple(routed)
    operands = (o_na, main, main, main, main, main, main, main, conv_w, o_f, o_b, main, main, main, main, x, gt1,
                gn_t, w_na, w_sc, w_gla, w_out, g2_t, sh2, sc2, w_router)
    n_in = len(operands)
    nblk = s // tm
    blk0 = tok_off // tm
    return pl.pallas_call(
        _merge_kernel,
        grid=(b, s // tm),
        in_specs=[tok(NA_WIDTH, 0), tok(SC_WIDTH, col_b), tok(SC_WIDTH, col_c), tok(SC_WIDTH, col_x),
                  halo_prev(col_c), halo_prev(col_x), halo_next(col_c), halo_next(col_x), full(conv_w),
                  tok(GLA_VAL_WIDTH, 0), tok(GLA_VAL_WIDTH, 0),
                  tok(d, OFF_R_GLA // d), tok(d, OFF_MERGE // d), tok(d, OFF_MERGE // d + 1),
                  tok(d, OFF_MERGE // d + 2), tok(d, 0), mod,
                  full(gn_t), full(w_na), full(w_sc), full(w_gla), full(w_out), full(g2_t), mod, mod,
                  full(w_router)] + [pl.BlockSpec(memory_space=pl.ANY)] * len(extend),
        out_specs=[tok(d, 0), tok(d, 0),
                   pl.BlockSpec((SC_PARTS, tm, SC_ROW), lambda bi, i: (0, blk0 + bi * nblk + i, 0)),
                   pl.BlockSpec((tm, LANE), lambda bi, i: (blk0 + bi * nblk + i, 0))],
        out_shape=[jax.ShapeDtypeStruct((b, s, d), F32),
                   jax.ShapeDtypeStruct((b, s, d), BF16),
                   jax.ShapeDtypeStruct((SC_PARTS, n_routed, SC_ROW), U32),
                   jax.ShapeDtypeStruct((n_routed, LANE), F32)],
        input_output_aliases={n_in: 2, n_in + 1: 3} if extend else {},
        compiler_params=_cparams(("parallel", "parallel")),
        name="merge",
    )(*operands, *extend)


def _router_kernel(lg_ref, br_ref, tri_ref, eidx_ref, gw_ref, rank_ref, cnt_ref, carry_ref):
    tm = lg_ref.shape[0]

    @pl.when(pl.program_id(0) == 0)
    def _():
        carry_ref[...] = jnp.zeros_like(carry_ref)

    scores = _sigmoid(lg_ref[...].T[:N_EXPERTS])
    sel = scores + br_ref[...]
    neg = -jnp.inf

    sel3 = sel.reshape(N_EXPERT_GROUPS, GROUP_SIZE, tm)
    i3 = lax.broadcasted_iota(jnp.int32, sel3.shape, 1)
    m1 = sel3.max(axis=1, keepdims=True)
    first = jnp.where(sel3 == m1, i3, GROUP_SIZE).min(axis=1, keepdims=True)
    m2 = jnp.where(i3 == first, neg, sel3).max(axis=1, keepdims=True)
    gscore = (m1 + m2)[:, 0, :]

    gi = lax.broadcasted_iota(jnp.int32, gscore.shape, 0)
    gmask = jnp.zeros(gscore.shape, jnp.bool_)
    for _ in range(TOPK_GROUPS):
        m = gscore.max(axis=0, keepdims=True)
        pick = gi == jnp.where(gscore == m, gi, N_EXPERT_GROUPS).min(axis=0, keepdims=True)
        gmask = gmask | pick
        gscore = jnp.where(pick, neg, gscore)
    emask = jnp.broadcast_to(gmask[:, None, :], sel3.shape).reshape(N_EXPERTS, tm)
    sel = jnp.where(emask, sel, neg)

    ei = lax.broadcasted_iota(jnp.int32, sel.shape, 0)
    picks, idxs, ws = [], [], []
    for _ in range(TOP_K):
        m = sel.max(axis=0, keepdims=True)
        idx = jnp.where(sel == m, ei, N_EXPERTS).min(axis=0, keepdims=True)
        pick = ei == idx
        picks.append(pick)
        idxs.append(idx)
        ws.append(jnp.where(pick, scores, 0.0).sum(axis=0, keepdims=True))
        sel = jnp.where(pick, neg, sel)
    w = jnp.concatenate(ws, axis=0)
    gw_ref[...] = w / w.sum(axis=0, keepdims=True) * ROUTED_SCALE
    eidx_ref[...] = jnp.concatenate(idxs, axis=0)

    onehot = picks[0]
    for p in picks[1:]:
        onehot = onehot | p
    onehot = jnp.where(onehot, 1.0, 0.0).astype(BF16)
    before = _dot(onehot, tri_ref[...]) + jnp.tile(carry_ref[...], (1, tm // LANE))
    rank_ref[...] = jnp.concatenate(
        [jnp.where(p, before, 0.0).sum(axis=0, keepdims=True) for p in picks], axis=0).astype(jnp.int32)
    carry_ref[...] += _dot(onehot, jnp.ones((tm, LANE), BF16))
    cnt_ref[...] = carry_ref[...]


def _route(logits, b_router, tm=512):
    t = logits.shape[0]
    br = jnp.broadcast_to(b_router.astype(F32)[:, None], (N_EXPERTS, tm))
    tri = jnp.asarray(np.triu(np.ones((tm, tm), np.float32), 1), dtype=BF16)
    kt = lambda dt: jax.ShapeDtypeStruct((TOP_K, t), dt)
    eidx, gw, rank, cnt = pl.pallas_call(
        _router_kernel,
        grid=(t // tm,),
        in_specs=[pl.BlockSpec((tm, LANE), lambda i: (i, 0)),
                  pl.BlockSpec((N_EXPERTS, tm), lambda i: (0, 0)),
                  pl.BlockSpec((tm, tm), lambda i: (0, 0))],
        out_specs=[pl.BlockSpec((TOP_K, tm), lambda i: (0, i)),
                   pl.BlockSpec((TOP_K, tm), lambda i: (0, i)),
                   pl.BlockSpec((TOP_K, tm), lambda i: (0, i)),
                   pl.BlockSpec((N_EXPERTS, LANE), lambda i: (0, 0))],
        out_shape=[kt(jnp.int32), kt(F32), kt(jnp.int32),
                   jax.ShapeDtypeStruct((N_EXPERTS, LANE), F32)],
        scratch_shapes=[pltpu.VMEM((N_EXPERTS, LANE), F32)],
        compiler_params=_cparams(("arbitrary",)),
        name="router",
    )(logits, br, tri)
    return eidx, gw, rank, cnt[:, 0].astype(jnp.int32)


def _sc_mesh():
    return plsc.VectorSubcoreMesh(core_axis_name="core", subcore_axis_name="subcore")


def _dispatch_rows(xp, dest, slots):
    parts, t, _ = xp.shape

    @pl.kernel(out_type=jax.ShapeDtypeStruct((parts, slots, SC_ROW), xp.dtype), mesh=_sc_mesh(),
               scratch_types=[], name="moe_dispatch")
    def run(x_hbm, d_hbm, o_hbm):
        for part in range(parts):
            out_part = o_hbm.at[part]

            def body(x_vmem, d_vmem, out_part=out_part):
                for k in range(TOP_K):
                    pltpu.sync_copy(x_vmem, out_part.at[d_vmem.at[k]])

            pltpu.emit_pipeline(
                body,
                grid=(t // SC_WINDOW,),
                in_specs=[pl.BlockSpec((SC_WINDOW, SC_ROW), lambda i: (i, 0)),
                          pl.BlockSpec((TOP_K, SC_WINDOW), lambda i: (0, i))],
                out_specs=[],
                core_axis_name=("core", "subcore"),
                dimension_semantics=(pltpu.PARALLEL,),
            )(x_hbm.at[part], d_hbm)

    return run(xp, dest)


def _gather_rows(yp, dest, t0, n):
    parts = yp.shape[0]
    nwin = n // SC_WINDOW
    win0 = t0 // SC_WINDOW

    @pl.kernel(out_type=jax.ShapeDtypeStruct((parts, TOP_K * n, SC_ROW), yp.dtype), mesh=_sc_mesh(),
               scratch_types=[], name="moe_gather")
    def run(y_hbm, d_hbm, o_hbm):
        for part in range(parts):
            table = y_hbm.at[part]

            def body(d_vmem, o_vmem, table=table):
                pltpu.sync_copy(table.at[d_vmem.at[0]], o_vmem)

            pltpu.emit_pipeline(
                body,
                grid=(TOP_K, nwin),
                in_specs=[pl.BlockSpec((1, SC_WINDOW), lambda k, j: (k, win0 + j))],
                out_specs=[pl.BlockSpec((SC_WINDOW, SC_ROW), lambda k, j: (k * nwin + j, 0))],
                core_axis_name=("core", "subcore"),
                dimension_semantics=(pltpu.PARALLEL, pltpu.PARALLEL),
            )(d_hbm, o_hbm.at[part])

    return run(yp, dest).reshape(parts, TOP_K, n, SC_ROW)


def _expert_kernel(be_ref, bv_ref, bs_ref, x_ref, wg_ref, wu_ref, wd_ref, o_ref, wg_s, wu_s, wd_s):
    i = pl.program_id(0)
    valid = bv_ref[i]
    new_expert = (i == 0) | (be_ref[i] != be_ref[jnp.maximum(i - 1, 0)])

    @pl.when(new_expert)
    def _():
        wg_s[...] = wg_ref[0, 0].astype(BF16)
        wu_s[...] = wu_ref[0, 0].astype(BF16)
        wd_s[...] = wd_ref[0, 0].astype(BF16)

    @pl.when(valid > 0)
    def _():
        w = _load_parts(x_ref)
        row = lax.broadcasted_iota(jnp.int32, w.shape, 0)
        w = jnp.where(row < valid, w, jnp.uint32(0))
        lo, hi = _unpack_rows(w)
        x = jnp.concatenate([lo, hi], axis=1).astype(BF16)
        a = _dot(x, wg_s[...])
        hid = a * _sigmoid(a) * _dot(x, wu_s[...])
        _store_parts(o_ref, _pack_rows(_dot(hid.astype(BF16), wd_s[...])))


def _experts(xs, blk_e, blk_valid, blk_src, layer, w_gate, w_up, w_down):
    parts, slots, _ = xs.shape
    d = D_MODEL
    nb = slots // MOE_BLOCK
    return pl.pallas_call(
        _expert_kernel,
        grid_spec=pltpu.PrefetchScalarGridSpec(
            num_scalar_prefetch=3,
            grid=(nb,),
            in_specs=[pl.BlockSpec((parts, MOE_BLOCK, SC_ROW), lambda i, be, bv, bs: (0, bs[i], 0)),
                      pl.BlockSpec((1, 1, d, EXPERT_FF), lambda i, be, bv, bs: (layer, be[i], 0, 0)),
                      pl.BlockSpec((1, 1, d, EXPERT_FF), lambda i, be, bv, bs: (layer, be[i], 0, 0)),
                      pl.BlockSpec((1, 1, EXPERT_FF, d), lambda i, be, bv, bs: (layer, be[i], 0, 0))],
            out_specs=pl.BlockSpec((parts, MOE_BLOCK, SC_ROW), lambda i, be, bv, bs: (0, bs[i], 0)),
            scratch_shapes=[pltpu.VMEM((d, EXPERT_FF), BF16), pltpu.VMEM((d, EXPERT_FF), BF16),
                            pltpu.VMEM((EXPERT_FF, d), BF16)]),
        out_shape=jax.ShapeDtypeStruct((parts, slots, SC_ROW), U32),
        compiler_params=_cparams(("arbitrary",)),
        name="experts",
    )(blk_e, blk_valid, blk_src, xs, w_gate, w_up, w_down)


def _combine_kernel(yg_ref, gw_ref, h_ref, x_ref, gt_ref, wsg_ref, wsu_ref, wsd_ref, gf_ref, o_ref, *, final):
    h = h_ref[0]
    a = _dot(h, wsg_ref[...])
    hid = a * _sigmoid(a) * _dot(h, wsu_ref[...])
    y = _dot(hid.astype(BF16), wsd_ref[...])
    gw = gw_ref[...]
    y_lo = y[:, :D_MODEL // 2]
    y_hi = y[:, D_MODEL // 2:]
    for k in range(TOP_K):
        lo, hi = _unpack_rows(_load_parts(yg_ref, k))
        y_lo = y_lo + gw[:, k:k + 1] * lo
        y_hi = y_hi + gw[:, k:k + 1] * hi
    y = jnp.concatenate([y_lo, y_hi], axis=1)
    xn = x_ref[0] + gt_ref[0] * y
    if final:
        ms = jnp.mean(xn * xn, axis=-1, keepdims=True)
        xn = xn * lax.rsqrt(ms + RMS_EPS) * gf_ref[...]
    o_ref[0] = xn


def _combine(yg, gw, tok_off, h2, x, gt2, b0, nb, ws_gate, ws_up, ws_down, g_final, final, tm):
    b, s, d = x.shape
    per_batch = gt2.shape[0] == b
    mod_map = (lambda bi, i: (b0 + bi, 0, 0)) if per_batch else (lambda bi, i: (0, 0, 0))
    full = lambda arr: pl.BlockSpec(arr.shape, lambda bi, i: (0,) * arr.ndim)
    tok = lambda width: pl.BlockSpec((1, tm, width), lambda bi, i: (b0 + bi, i, 0))
    gf = g_final[None]
    nblk = s // tm
    blk0 = (tok_off + b0 * s) // tm
    return pl.pallas_call(
        functools.partial(_combine_kernel, final=final),
        grid=(nb, nblk),
        in_specs=[pl.BlockSpec((SC_PARTS, TOP_K, tm, SC_ROW), lambda bi, i: (0, 0, bi * nblk + i, 0)),
                  pl.BlockSpec((tm, TOP_K), lambda bi, i: (blk0 + bi * nblk + i, 0)),
                  tok(d), tok(d),
                  pl.BlockSpec((1, 1, d), mod_map),
                  full(ws_gate), full(ws_up), full(ws_down), full(gf)],
        out_specs=tok(d),
        out_shape=jax.ShapeDtypeStruct((b, s, d), F32),
        input_output_aliases={3: 0},
        compiler_params=_cparams(("parallel", "parallel")),
        name="combine",
    )(yg, gw, h2, x, gt2, ws_gate, ws_up, ws_down, gf)


def _project_latent(x, p, b0=0, nb=None, prev=None):
    w_main, w_gate = p['prep']['w_in']
    mods = p['prep']['mods']
    return _proj_in(x, p['g_norm1'], mods[0], mods[1], w_main, w_gate, tm=min(2048, x.shape[1]), tn=1024,
                    b0=b0, nb=nb, prev=prev)


def _layer(x, ctx_s, p, ctx_out, final, g_final, projected=None, after_piece=None, p_next=None):
    b, s, d = x.shape
    sc = ctx_s.shape[1]
    prep = p['prep']
    sh1, sc1, gt1, sh2, sc2, gt2 = prep['mods']
    csh1, csc1, cgt1, csh2, csc2, cgt2 = prep['mods_ctx']

    w_main, w_gate = prep['w_in']
    main, gate = _project_latent(x, p) if projected is None else projected
    ctx_flat = ctx_s.reshape(1, b * sc, d)
    n_ctx, tn_ctx = (N_MAIN, 1024) if ctx_out else (N_KV_MAIN, N_KV_MAIN // 2)
    main_c, gate_c = _proj_in(ctx_flat, p['g_norm1'], csh1, csc1, w_main, w_gate, tm=min(1024, b * sc),
                              tn=tn_ctx, n=n_ctx)
    main_c = main_c.reshape(b, sc, n_ctx)
    gate_c = gate_c.reshape(b, sc, LANE)

    o_na = _na_latent(main, main_c, prep['na_bias'])

    gate_ws = prep['gate_ws']
    s0 = jnp.zeros((b, GLA_HEADS, GLA_DK, GLA_DV), F32)
    o_cf, o_cb, st_f, st_b = _gla_scan(main_c, gate_c, gate_ws, (s0, s0), ctx_out)
    o_f, o_b, _, _ = _gla_scan(main, gate, gate_ws, (st_f, st_b), True)

    w_na, w_sc, w_gla, w_out, w_router = (prep[name] for name in ('w_na', 'w_sc', 'w_gla', 'w_out', 'w_router'))
    n_lat = b * s
    t = n_lat + (b * sc if ctx_out else 0)
    x, h2, hp_all, lg_all = _merge(o_na, p['conv_w'], o_f, o_b, main, x, gt1, p['gla_norm_g'], w_na, w_sc, w_gla,
                                   w_out, p['g_norm2'], sh2, sc2, w_router, tm=min(512, s), n_routed=t, tok_off=0)
    if ctx_out:
        o_na_c = _dense_attn(main_c)
        ctx_s, h2_c, hp_all, lg_all = _merge(o_na_c, p['conv_w'], o_cf, o_cb, main_c, ctx_s, cgt1, p['gla_norm_g'],
                                             w_na, w_sc, w_gla, w_out, p['g_norm2'], csh2, csc2, w_router,
                                             tm=min(256, sc), n_routed=t, tok_off=n_lat, routed=(hp_all, lg_all))

    eidx, gw, rank, counts = _route(lg_all, p['b_router'])
    padded = (counts + MOE_BLOCK - 1) // MOE_BLOCK * MOE_BLOCK
    pad_end = jnp.cumsum(padded)
    pad_start = pad_end - padded
    onehot = eidx[:, :, None] == jnp.arange(N_EXPERTS, dtype=jnp.int32)
    dest = jnp.sum(jnp.where(onehot, pad_start, 0), axis=-1) + rank
    n_blocks = -(-(t * TOP_K + N_EXPERTS * (MOE_BLOCK - 1)) // MOE_BLOCK)
    slots = n_blocks * MOE_BLOCK
    blk_start = jnp.arange(n_blocks, dtype=jnp.int32) * MOE_BLOCK
    blk_e = jnp.minimum(jnp.sum(pad_end[None, :] <= blk_start[:, None], axis=1), N_EXPERTS - 1).astype(jnp.int32)
    used_end = (pad_start + counts)[blk_e]
    blk_valid = jnp.clip(used_end - blk_start, 0, MOE_BLOCK).astype(jnp.int32)
    n_used = pad_end[-1] // MOE_BLOCK
    blk_src = jnp.minimum(jnp.arange(n_blocks, dtype=jnp.int32), n_used - 1)
    blk_e = blk_e[blk_src]

    xs = _dispatch_rows(hp_all, dest, slots)
    if p_next is not None:
        xs, p_next['prep'] = lax.optimization_barrier((xs, p_next['prep']))
    ys = _experts(xs, blk_e, blk_valid, blk_src, p['layer'], p['w_exp_gate'], p['w_exp_up'], p['w_exp_down'])
    gw_t = gw.T
    ws_gate, ws_up, ws_down = prep['ws_gate'], prep['ws_up'], prep['ws_down']

    def gathered(t0, n):
        return _gather_rows(ys, dest, t0, n)

    pieces = next(n for n in (4, 2, 1) if b % n == 0)
    nb = b // pieces
    for q in range(pieces):
        x = _combine(gathered(q * nb * s, nb * s), gw_t, 0, h2, x, gt2, q * nb, nb, ws_gate, ws_up, ws_down,
                     g_final, final, tm=min(256, s))
        if after_piece is not None:
            after_piece(x, q * nb, nb)
    if ctx_out:
        ctx_s = _combine(gathered(n_lat, b * sc), gw_t, n_lat, h2_c, ctx_s, cgt2, 0, b, ws_gate, ws_up, ws_down,
                         g_final, False, tm=min(256, sc))
    return x, ctx_s


def kernel(x, c, ctx, c_ctx, w_mod, b_mod, g_norm1, g_norm2, w_in, na_rpb, w_branch_na, conv_w, w_branch_sc,
           gla_gate_w, gla_gate_b, gla_norm_g, w_branch_gla, w_out, w_router, b_router, w_exp_gate, w_exp_up,
           w_exp_down, w_sh_gate, w_sh_up, w_sh_down, g_final):
    stacked = dict(g_norm1=g_norm1, g_norm2=g_norm2, na_rpb=na_rpb, w_branch_na=w_branch_na,
                   conv_w=conv_w, w_branch_sc=w_branch_sc, gla_gate_w=gla_gate_w, gla_gate_b=gla_gate_b,
                   gla_norm_g=gla_norm_g, w_branch_gla=w_branch_gla, w_out=w_out, w_router=w_router,
                   b_router=b_router,
                   w_sh_gate=w_sh_gate, w_sh_up=w_sh_up, w_sh_down=w_sh_down)
    depth = w_in.shape[0]
    rows = x.shape[1] // GRID_W
    layers = []
    for i in range(depth):
        p = {name: arr[i] for name, arr in stacked.items()}
        p.update(layer=i, w_exp_gate=w_exp_gate, w_exp_up=w_exp_up, w_exp_down=w_exp_down)
        mods, mods_ctx = _mod_vectors(c, c_ctx, w_mod, b_mod, i)
        p['prep'] = dict(
            w_in=_prep_w_in(w_in, i), mods=mods, mods_ctx=mods_ctx,
            na_bias=_na_bias_table(p['na_rpb'], rows, min(NA_WIN_R, rows)),
            gate_ws=_gla_gate_weights(p['gla_gate_w'], p['gla_gate_b']),
            w_na=p['w_branch_na'].astype(BF16), w_sc=p['w_branch_sc'].astype(BF16),
            w_gla=p['w_branch_gla'].astype(BF16), w_out=p['w_out'].astype(BF16),
            w_router=jnp.pad(p['w_router'], ((0, 0), (0, LANE - N_EXPERTS))).astype(BF16),
            ws_gate=p['w_sh_gate'].astype(BF16), ws_up=p['w_sh_up'].astype(BF16),
            ws_down=p['w_sh_down'].astype(BF16))
        layers.append(p)

    ctx_s = ctx
    projected = None
    for i, p in enumerate(layers):
        last = i == depth - 1
        p_next = None if last else layers[i + 1]
        after_piece = None
        next_projected = []
        if not last:
            def after_piece(xq, b0, nb, p_next=p_next, acc=next_projected):
                acc.append(_project_latent(xq, p_next, b0, nb, acc[-1] if acc else None))

        x, ctx_s = _layer(x, ctx_s, p, not last, last, g_final, projected, after_piece, p_next)
        projected = next_projected[-1] if next_projected else None
    return x
```

```python
import functools

import numpy as np
import jax
import jax.numpy as jnp
from jax import lax
from jax.experimental import pallas as pl
from jax.experimental.pallas import tpu as pltpu
from jax.experimental.pallas import tpu_sc as plsc

F32 = jnp.float32
BF16 = jnp.bfloat16
U32 = jnp.uint32

D_MODEL = 1024
N_MOD = 6
RMS_EPS = 1e-6
NEG_INF = -1e30
GRID_W = 64
NA_HEADS = 8
NA_HEAD_DIM = 64
NA_WIDTH = NA_HEADS * NA_HEAD_DIM
NA_WIN_R = 8
NA_WIN_C = 16
NA_GROUP = 4
SC_WIDTH = 512
GLA_HEADS = 4
GLA_KEY_WIDTH = 512
GLA_VAL_WIDTH = 1024
GLA_DK = GLA_KEY_WIDTH // GLA_HEADS
GLA_DV = GLA_VAL_WIDTH // GLA_HEADS
GLA_GATE_RANK = 16
GLA_GATE_TAU = 16.0
LOG2_E = 1.4426950408889634
N_EXPERTS = 64
N_EXPERT_GROUPS = 8
GROUP_SIZE = N_EXPERTS // N_EXPERT_GROUPS
TOPK_GROUPS = 4
TOP_K = 8
EXPERT_FF = 256
ROUTED_SCALE = 2.5
MOE_BLOCK = 1024

LANE = 128
GLA_C = 128
GLA_LEVELS = tuple(GLA_C >> (i + 1) for i in range(GLA_C.bit_length() - 1))
VMEM_LIMIT = 48 * 1024 * 1024
SC_WINDOW = 128
SC_ROW = 256
SC_PARTS = D_MODEL // 2 // SC_ROW

OFF_V_GLA = 0
OFF_K_NA = 1024
OFF_V_NA = 1536
OFF_K_GLA = 2048
N_KV_MAIN = 2560
OFF_Q_NA = 2560
OFF_B_SC = 3072
OFF_C_SC = 3584
OFF_X_SC = 4096
OFF_Q_GLA = 4608
OFF_R_GLA = 5120
OFF_MERGE = 6144
N_MAIN = 9216


def _cparams(sem, vmem=VMEM_LIMIT):
    return pltpu.CompilerParams(dimension_semantics=sem, vmem_limit_bytes=vmem)


def _dot(a, b):
    return jnp.dot(a, b, preferred_element_type=F32)


def _dot_nt(a, b):
    return lax.dot_general(a, b, (((1,), (1,)), ((), ())), preferred_element_type=F32)


def _dot_tn(a, b):
    return lax.dot_general(a, b, (((0,), (0,)), ((), ())), preferred_element_type=F32)


def _sigmoid(x):
    return 0.5 * jnp.tanh(0.5 * x) + 0.5


def _pack_rows(x):
    n = x.shape[1] // 2
    r = x.astype(BF16).astype(F32)
    lo = pltpu.bitcast(r[:, :n], U32) >> 16
    hi = pltpu.bitcast(r[:, n:], U32)
    return hi | lo


def _store_parts(ref, words):
    for part in range(SC_PARTS):
        dst = ref.at[part, 0] if len(ref.shape) == 4 else ref.at[part]
        dst[...] = words[:, part * SC_ROW:(part + 1) * SC_ROW]


def _load_parts(ref, *lead):
    return jnp.concatenate([ref[(part,) + lead] for part in range(SC_PARTS)], axis=-1)


def _unpack_rows(w):
    lo = pltpu.bitcast(w << 16, F32)
    hi = pltpu.bitcast(w & jnp.uint32(0xFFFF0000), F32)
    return lo, hi


def _mod_kernel(a_ref, w_ref, b_ref, o_ref):
    a = a_ref[...]
    a = a * _sigmoid(a)
    o_ref[...] = _dot(a.astype(BF16), w_ref[0].astype(BF16)) + b_ref[0]


def _mod_vectors(c, c_ctx, w_mod, b_mod, layer):
    b = c.shape[0]
    rows = -(-(b + 1) // 8) * 8
    a = jnp.concatenate([c, c_ctx[None], jnp.zeros((rows - b - 1, D_MODEL), F32)], axis=0)
    n = N_MOD * D_MODEL
    tn = 1536
    out = pl.pallas_call(
        _mod_kernel,
        grid=(n // tn,),
        in_specs=[pl.BlockSpec((rows, D_MODEL), lambda j: (0, 0)),
                  pl.BlockSpec((1, D_MODEL, tn), lambda j: (layer, 0, j)),
                  pl.BlockSpec((1, 1, tn), lambda j: (layer, 0, j))],
        out_specs=pl.BlockSpec((rows, tn), lambda j: (0, j)),
        out_shape=jax.ShapeDtypeStruct((rows, n), F32),
        compiler_params=_cparams(("parallel",)),
        name="mod_vectors",
    )(a, w_mod, b_mod[:, None])
    lat = out[:b].reshape(b, N_MOD, 1, D_MODEL)
    ctx = out[b].reshape(N_MOD, 1, 1, D_MODEL)
    return [lat[:, i] for i in range(N_MOD)], [ctx[i] for i in range(N_MOD)]


def _proj_kernel(x_ref, g_ref, sh_ref, sc_ref, w_ref, wg_ref, *rest):
    o_ref, og_ref, h_ref = rest[-3:]

    @pl.when(pl.program_id(2) == 0)
    def _():
        x = x_ref[0]
        ms = jnp.mean(x * x, axis=-1, keepdims=True)
        h = x * lax.rsqrt(ms + RMS_EPS) * g_ref[...] * (1.0 + sc_ref[0]) + sh_ref[0]
        hb = h.astype(BF16)
        h_ref[...] = hb
        og_ref[0] = _dot(hb, wg_ref[...])

    o_ref[0] = _dot(h_ref[...], w_ref[...]).astype(o_ref.dtype)


W_IN_TILE = 512
W_IN_GATE_SHIFT = 2 * GLA_GATE_RANK


def _prep_w_in_kernel(a_ref, b_ref, o_ref, g_ref):
    t = pl.program_id(0)
    first_lat = N_KV_MAIN // W_IN_TILE
    a = a_ref[0]

    @pl.when(t < first_lat)
    def _():
        o_ref[...] = a.T.astype(BF16)

    @pl.when(t >= first_lat)
    def _():
        moved = jnp.concatenate([a[W_IN_GATE_SHIFT:], b_ref[0]], axis=0)
        scale = jnp.where(t == first_lat, NA_HEAD_DIM ** -0.5, 1.0)
        o_ref[...] = (moved * scale).T.astype(BF16)

    @pl.when(t == first_lat)
    def _():
        head = a[:LANE]
        row = lax.broadcasted_iota(jnp.int32, head.shape, 0)
        g_ref[...] = jnp.where(row < W_IN_GATE_SHIFT, head, 0.0).T.astype(BF16)


def _prep_w_in(w_in, layer):
    d = w_in.shape[1]
    w_t = jnp.swapaxes(w_in, 1, 2)
    first_lat = N_KV_MAIN // W_IN_TILE
    kv_perm = OFF_K_NA // W_IN_TILE

    def a_map(t):
        return (layer, jnp.where(t < first_lat, (t + first_lat - kv_perm) % first_lat, t), 0)

    def b_map(t):
        return (layer, jnp.where(t < first_lat, 0, (t + 1) * (W_IN_TILE // W_IN_GATE_SHIFT)), 0)

    return pl.pallas_call(
        _prep_w_in_kernel,
        grid=(N_MAIN // W_IN_TILE,),
        in_specs=[pl.BlockSpec((1, W_IN_TILE, d), a_map),
                  pl.BlockSpec((1, W_IN_GATE_SHIFT, d), b_map)],
        out_specs=[pl.BlockSpec((d, W_IN_TILE), lambda t: (0, t)),
                   pl.BlockSpec((d, LANE), lambda t: (0, 0))],
        out_shape=[jax.ShapeDtypeStruct((d, N_MAIN), BF16), jax.ShapeDtypeStruct((d, LANE), BF16)],
        compiler_params=_cparams(("arbitrary",)),
        name="prep_w_in",
    )(w_t, w_t)


def _proj_in(x, g, shift, scale, w_main, w_gate, tm, tn, n=None, b0=0, nb=None, prev=None):
    b, s, d = x.shape
    n = w_main.shape[1] if n is None else n
    nb = b if nb is None else nb
    per_batch = shift.shape[0] == b
    mod_map = (lambda bi, i, j: (b0 + bi, 0, 0)) if per_batch else (lambda bi, i, j: (0, 0, 0))
    operands = (x, g[None], shift, scale, w_main, w_gate)
    extend = () if prev is None else tuple(prev)
    return pl.pallas_call(
        _proj_kernel,
        grid=(nb, s // tm, n // tn),
        in_specs=[pl.BlockSpec((1, tm, d), lambda bi, i, j: (b0 + bi, i, 0)),
                  pl.BlockSpec((1, d), lambda bi, i, j: (0, 0)),
                  pl.BlockSpec((1, 1, d), mod_map),
                  pl.BlockSpec((1, 1, d), mod_map),
                  pl.BlockSpec((d, tn), lambda bi, i, j: (0, j)),
                  pl.BlockSpec((d, LANE), lambda bi, i, j: (0, 0))]
                 + [pl.BlockSpec(memory_space=pl.ANY)] * len(extend),
        out_specs=[pl.BlockSpec((1, tm, tn), lambda bi, i, j: (b0 + bi, i, j)),
                   pl.BlockSpec((1, tm, LANE), lambda bi, i, j: (b0 + bi, i, 0))],
        out_shape=[jax.ShapeDtypeStruct((b, s, n), BF16),
                   jax.ShapeDtypeStruct((b, s, LANE), F32)],
        scratch_shapes=[pltpu.VMEM((tm, d), BF16)],
        input_output_aliases={len(operands): 0, len(operands) + 1: 1} if extend else {},
        compiler_params=_cparams(("parallel", "parallel", "arbitrary")),
        name="proj_in",
    )(*operands, *extend)


def _softmax_av(q, keys, vals, biases):
    scores = []
    for kk, bb in zip(keys, biases):
        s = _dot_nt(q, kk)
        scores.append(s if bb is None else s + bb)
    m = scores[0].max(axis=-1, keepdims=True)
    for s in scores[1:]:
        m = jnp.maximum(m, s.max(axis=-1, keepdims=True))
    num = None
    den = None
    for s, vv in zip(scores, vals):
        e = jnp.exp(s - m)
        dsum = e.sum(axis=-1, keepdims=True)
        o = _dot(e.astype(BF16), vv)
        num = o if num is None else num + o
        den = dsum if den is None else den + dsum
    return num / den


def _na_kernel(q_ref, k_ref, v_ref, kc_ref, vc_ref, *rest, rows, kr):
    *bias_refs, o_ref = rest
    kc = kc_ref[0]
    vc = vc_ref[0]
    for j, bias_ref in enumerate(bias_refs):
        r = pl.program_id(1) * len(bias_refs) + j
        row_start = jnp.clip(r - kr // 2, 0, rows - kr)
        start = pl.multiple_of(row_start * GRID_W, GRID_W)
        n_win = kr * GRID_W
        q = q_ref[0, j * GRID_W:(j + 1) * GRID_W, :]
        kw = k_ref[0, pl.ds(start, n_win), :]
        vw = v_ref[0, pl.ds(start, n_win), :]
        o_ref[0, j * GRID_W:(j + 1) * GRID_W, :] = _na_row(q, kw, vw, kc, vc, bias_ref).astype(o_ref.dtype)


def _na_row(q, kw, vw, kc, vc, bias_ref):
    gw = NA_GROUP * NA_HEAD_DIM
    stacked = (NA_GROUP * GRID_W, gw)
    on_head = (lax.broadcasted_iota(jnp.int32, stacked, 0) // GRID_W
               == lax.broadcasted_iota(jnp.int32, stacked, 1) // NA_HEAD_DIM)
    outs = []
    for g in range(NA_HEADS // NA_GROUP):
        sl = slice(g * gw, (g + 1) * gw)
        q_all = jnp.where(on_head, jnp.concatenate([q[:, sl]] * NA_GROUP, axis=0), jnp.zeros((), q.dtype))
        bias = bias_ref[0, g * NA_GROUP * GRID_W:(g + 1) * NA_GROUP * GRID_W, :]
        o_all = _softmax_av(q_all, [kw[:, sl], kc[:, sl]], [vw[:, sl], vc[:, sl]], [bias, None])
        o_all = jnp.where(on_head, o_all, 0.0).reshape(NA_GROUP, GRID_W, gw)
        outs.append(o_all.sum(axis=0))
    return jnp.concatenate(outs, axis=-1)


def _na_bias_table(rpb, rows, kr):
    col = np.arange(GRID_W)
    col_start = np.clip(col - NA_WIN_C // 2, 0, GRID_W - NA_WIN_C)
    col_ok = (col[None, :] >= col_start[:, None]) & (col[None, :] < col_start[:, None] + NA_WIN_C)
    d_col = np.clip(col[None, :] - col[:, None], -(NA_WIN_C - 1), NA_WIN_C - 1) + NA_WIN_C - 1
    n_dr, n_dc = rpb.shape[1], rpb.shape[2]
    onehot = jnp.asarray((d_col.reshape(-1)[None, :] == np.arange(n_dc)[:, None]).astype(np.float32))
    by_col = jnp.dot(rpb.astype(F32).reshape(NA_HEADS * n_dr, n_dc), onehot, precision=lax.Precision.HIGHEST)
    by_col = by_col.reshape(NA_HEADS, n_dr, GRID_W, GRID_W)
    by_col = jnp.where(col_ok[None, None], by_col, NEG_INF)
    tables = []
    for o in range(kr):
        lo = NA_WIN_R - 1 - o
        tables.append(by_col[:, lo:lo + kr].transpose(0, 2, 1, 3).reshape(NA_HEADS, GRID_W, kr * GRID_W))
    return jnp.stack(tables).reshape(kr, NA_HEADS * GRID_W, kr * GRID_W)


def _na_latent(main, main_ctx, bias):
    b, s, _ = main.shape
    sc = main_ctx.shape[1]
    rows = s // GRID_W
    kr = min(NA_WIN_R, rows)
    w = NA_WIDTH

    per_step = next(n for n in (8, 4, 2, 1) if rows % n == 0)

    def bias_spec(j):
        def bias_map(bi, i):
            r = i * per_step + j
            return (r - jnp.clip(r - kr // 2, 0, rows - kr), 0, 0)
        return pl.BlockSpec((1, NA_HEADS * GRID_W, kr * GRID_W), bias_map)

    return pl.pallas_call(
        functools.partial(_na_kernel, rows=rows, kr=kr),
        grid=(b, rows // per_step),
        in_specs=[pl.BlockSpec((1, per_step * GRID_W, w), lambda bi, i: (bi, i, OFF_Q_NA // w)),
                  pl.BlockSpec((1, s, w), lambda bi, i: (bi, 0, OFF_K_NA // w)),
                  pl.BlockSpec((1, s, w), lambda bi, i: (bi, 0, OFF_V_NA // w)),
                  pl.BlockSpec((1, sc, w), lambda bi, i: (bi, 0, OFF_K_NA // w)),
                  pl.BlockSpec((1, sc, w), lambda bi, i: (bi, 0, OFF_V_NA // w))]
                 + [bias_spec(j) for j in range(per_step)],
        out_specs=pl.BlockSpec((1, per_step * GRID_W, w), lambda bi, i: (bi, i, 0)),
        out_shape=jax.ShapeDtypeStruct((b, s, w), BF16),
        compiler_params=_cparams(("parallel", "arbitrary")),
        name="na_latent",
    )(main, main, main, main_ctx, main_ctx, *([bias] * per_step))


def _dense_attn_kernel(q_ref, k_ref, v_ref, o_ref):
    q = q_ref[0]
    k = k_ref[0]
    v = v_ref[0]
    outs = []
    for h in range(NA_HEADS):
        sl = slice(h * NA_HEAD_DIM, (h + 1) * NA_HEAD_DIM)
        outs.append(_softmax_av(q[:, sl], [k[:, sl]], [v[:, sl]], [None]))
    o_ref[0] = jnp.concatenate(outs, axis=-1).astype(o_ref.dtype)


def _dense_attn(main_ctx):
    b, sc, _ = main_ctx.shape
    w = NA_WIDTH
    return pl.pallas_call(
        _dense_attn_kernel,
        grid=(b,),
        in_specs=[pl.BlockSpec((1, sc, w), lambda bi: (bi, 0, OFF_Q_NA // w)),
                  pl.BlockSpec((1, sc, w), lambda bi: (bi, 0, OFF_K_NA // w)),
                  pl.BlockSpec((1, sc, w), lambda bi: (bi, 0, OFF_V_NA // w))],
        out_specs=pl.BlockSpec((1, sc, w), lambda bi: (bi, 0, 0)),
        out_shape=jax.ShapeDtypeStruct((b, sc, w), BF16),
        compiler_params=_cparams(("parallel",)),
        name="ctx_attn",
    )(main_ctx, main_ctx, main_ctx)


HALO = 16


def _gla_matrices(reverse):
    c = GLA_C
    t = np.arange(c)[:, None]
    m = np.arange(c)[None, :]
    blocks = [m <= t, m > t]
    for b in GLA_LEVELS:
        first = (t // (2 * b)) * (2 * b) + b
        is_q = (t & b) != 0
        blocks.append(np.where(is_q, (m > first) & (m <= t), (m > t) & (m <= first)))
    mats = np.stack(blocks).astype(np.float32)
    if reverse:
        mats = mats[:, ::-1, ::-1]
    mats = mats.reshape(-1, c)
    return jnp.asarray(np.concatenate([mats, mats], axis=1), dtype=BF16)


def _split_bf16(x):
    hi = x.astype(BF16)
    return hi, (x - hi.astype(F32)).astype(BF16)


def _pair_block_diag(x):
    lane = lax.broadcasted_iota(jnp.int32, x.shape, 1)
    zero = jnp.zeros((), x.dtype)
    return jnp.concatenate([jnp.where(lane < GLA_DK, x, zero), jnp.where(lane >= GLA_DK, x, zero)], axis=0)


def _gla_kernel(*refs, emit):
    n_in = 9 if emit else 8
    n_out = 2 if emit else 1
    ins, outs, scratch = refs[:2 * n_in], refs[2 * n_in:2 * (n_in + n_out)], refs[2 * (n_in + n_out):]
    ins = [ins[d * n_in:(d + 1) * n_in] for d in range(2)]
    outs = [outs[d * n_out:(d + 1) * n_out] for d in range(2)]
    step = pl.program_id(1)

    @pl.when(step == 0)
    def _():
        for d in range(2):
            scratch[d][...] = ins[d][-1][0]

    pending = [_gla_direction(ins[d][:-1], outs[d][:-1], scratch[d], reverse, emit)
               for d, reverse in enumerate((False, True))]
    while pending:
        pending = [stages for stages in pending if next(stages, "done") != "done"]

    @pl.when(step == pl.num_programs(1) - 1)
    def _():
        for d in range(2):
            outs[d][-1][0] = scratch[d][...]


def _gla_direction(ins, outs, st_ref, reverse, emit):
    if emit:
        q_ref, k_ref, v_ref, gt_ref, w2a_ref, w2b_ref, b2_ref, a_ref = ins
        (o_ref,) = outs
    else:
        k_ref, v_ref, gt_ref, w2a_ref, w2b_ref, b2_ref, a_ref = ins
    c = GLA_C
    pw = 2 * GLA_DK

    lr_hi, lr_lo = _split_bf16(gt_ref[0])
    logit = (_dot(jnp.concatenate([lr_hi, lr_lo], axis=1), w2a_ref[...]) + _dot(lr_hi, w2b_ref[...])
             + b2_ref[...])
    g = (jnp.minimum(logit, 0.0) - jnp.log1p(jnp.exp(-jnp.abs(logit)))) * (LOG2_E / GLA_GATE_TAU)
    g_hi, g_lo = _split_bf16(g)
    args = _dot(a_ref[...], jnp.concatenate([g_hi, g_lo], axis=0))
    cum = args[0:c]
    rem = args[c:2 * c]
    last_row = cum[0:1] if reverse else cum[c - 1:c]
    yield

    k = k_ref[0].astype(F32)
    v = v_ref[0]
    atts = []
    if emit:
        q = q_ref[0].astype(F32) * (GLA_DK ** -0.5)
        row_t = lax.broadcasted_iota(jnp.int32, (c, pw), 0)
        si = lax.broadcasted_iota(jnp.int32, (2 * c, c), 0) & (c - 1)
        ti = lax.broadcasted_iota(jnp.int32, (2 * c, c), 1)
        if reverse:
            row_t, ti, si = c - 1 - row_t, c - 1 - ti, c - 1 - si
        for hp in range(GLA_HEADS // 2):
            cs = slice(hp * pw, (hp + 1) * pw)
            qp, kp = q[:, cs], k[:, cs]
            att = jnp.where(ti == si, _dot_nt(_pair_block_diag(kp.astype(BF16)), qp.astype(BF16)), 0.0)
            for l, b in enumerate(GLA_LEVELS):
                x = (jnp.exp2(args[(2 + l) * c:(3 + l) * c, cs])
                     * jnp.where((row_t & b) != 0, qp, kp)).astype(BF16)
                pair = (((ti ^ si) >> (b.bit_length() - 1)) == 1) & ((ti & b) != 0)
                att = jnp.where(pair, _dot_nt(_pair_block_diag(x), x), att)
                yield
            atts.append(att.astype(BF16))

    outs = []
    for h in range(GLA_HEADS):
        sl = slice(h * GLA_DK, (h + 1) * GLA_DK)
        kh = k[:, sl]
        vh = v[:, h * GLA_DV:(h + 1) * GLA_DV]
        state = st_ref[h]
        kd = (kh * jnp.exp2(rem[:, sl])).astype(BF16)
        decay = jnp.exp2(jnp.broadcast_to(last_row[:, sl], (GLA_DK, GLA_DK))).T
        decay = jnp.concatenate([decay] * (GLA_DV // GLA_DK), axis=1)
        if emit:
            qd = (q[:, sl] * jnp.exp2(cum[:, sl])).astype(BF16)
            att_t = atts[h // 2][(h % 2) * c:(h % 2 + 1) * c]
            both = _dot_tn(jnp.concatenate([att_t, kd], axis=1), vh)
            outs.append(_dot(qd, state.astype(BF16)) + both[:c])
            st_ref[h] = decay * state + both[c:]
        else:
            st_ref[h] = decay * state + _dot_tn(kd, vh)
        yield

    if emit:
        o_ref[0] = jnp.concatenate(outs, axis=-1).astype(o_ref.dtype)


def _gla_scan(main, gate, gate_ws, s0s, emit):
    b, l, _ = main.shape
    n = l // GLA_C
    const = lambda arr: pl.BlockSpec(arr.shape, lambda bi, s: (0,) * arr.ndim)
    state_spec = pl.BlockSpec((1, GLA_HEADS, GLA_DK, GLA_DV), lambda bi, s: (bi, 0, 0, 0))
    state_shape = jax.ShapeDtypeStruct((b, GLA_HEADS, GLA_DK, GLA_DV), F32)
    in_specs, args, out_specs, out_shape = [], [], [], []
    for reverse in (False, True):
        amat = _gla_matrices(reverse)
        w2a, w2b, b2 = gate_ws[reverse]

        def col(block, reverse=reverse):
            return lambda bi, s: (bi, n - 1 - s if reverse else s, block)

        if emit:
            in_specs.append(pl.BlockSpec((1, GLA_C, GLA_KEY_WIDTH), col(OFF_Q_GLA // GLA_KEY_WIDTH)))
            args.append(main)
            out_specs.append(pl.BlockSpec((1, GLA_C, GLA_VAL_WIDTH), col(0)))
            out_shape.append(jax.ShapeDtypeStruct((b, l, GLA_VAL_WIDTH), BF16))
        in_specs += [pl.BlockSpec((1, GLA_C, GLA_KEY_WIDTH), col(OFF_K_GLA // GLA_KEY_WIDTH)),
                     pl.BlockSpec((1, GLA_C, GLA_VAL_WIDTH), col(OFF_V_GLA // GLA_VAL_WIDTH)),
                     pl.BlockSpec((1, GLA_C, LANE), col(0)),
                     const(w2a), const(w2b), const(b2), const(amat), state_spec]
        args += [main, main, gate, w2a, w2b, b2, amat, s0s[reverse]]
        out_specs.append(state_spec)
        out_shape.append(state_shape)
    res = pl.pallas_call(
        functools.partial(_gla_kernel, emit=emit),
        grid=(b, n),
        in_specs=in_specs,
        out_specs=out_specs,
        out_shape=out_shape,
        scratch_shapes=[pltpu.VMEM((GLA_HEADS, GLA_DK, GLA_DV), F32)] * 2,
        compiler_params=_cparams(("parallel", "arbitrary")),
        name="gla_scan",
    )(*args)
    return (res[0], res[2], res[1], res[3]) if emit else (None, None, res[0], res[1])


def _gla_gate_weights(gate_w, gate_b):
    out = []
    for dr in range(2):
        w = jnp.zeros((LANE, GLA_KEY_WIDTH), F32)
        w = w.at[dr * GLA_GATE_RANK:(dr + 1) * GLA_GATE_RANK].set(gate_w[dr])
        w_hi = w.astype(BF16)
        w_lo = (w - w_hi.astype(F32)).astype(BF16)
        out.append((jnp.concatenate([w_hi, w_hi], axis=0), w_lo, gate_b[dr][None]))
    return out


def _conv_tile(b_ref, c_ref, x_ref, cp_ref, xp_ref, cn_ref, xn_ref, w_ref):
    i, n = pl.program_id(1), pl.num_programs(1)
    u = c_ref[0].astype(F32) * x_ref[0].astype(F32)
    tm = u.shape[0]
    before = jnp.where(i > 0, 1.0, 0.0) * (cp_ref[0, HALO - 1:HALO].astype(F32) * xp_ref[0, HALO - 1:HALO].astype(F32))
    after = jnp.where(i < n - 1, 1.0, 0.0) * (cn_ref[0, 0:1].astype(F32) * xn_ref[0, 0:1].astype(F32))
    t = lax.broadcasted_iota(jnp.int32, u.shape, 0)
    prev = jnp.where(t == 0, before, pltpu.roll(u, 1, axis=0))
    nxt = jnp.where(t == tm - 1, after, pltpu.roll(u, tm - 1, axis=0))
    w = w_ref[...]
    return b_ref[0].astype(F32) * (prev * w[0:1] + u * w[1:2] + nxt * w[2:3])


def _merge_kernel(ona_ref, bsc_ref, csc_ref, xsc_ref, cp_ref, xp_ref, cn_ref, xn_ref, cw_ref, of_ref, ob_ref,
                  r_ref, gna_ref, gsc_ref, ggl_ref, x_ref, gt_ref,
                  gn_ref, wna_ref, wsc_ref, wgl_ref, wo_ref, g2_ref, sh2_ref, sc2_ref, wr_ref, *rest):
    xo_ref, h2_ref, hp_ref, lg_ref = rest[-4:]
    o_sc = _conv_tile(bsc_ref, csc_ref, xsc_ref, cp_ref, xp_ref, cn_ref, xn_ref, cw_ref).astype(BF16)
    o = of_ref[0].astype(F32) + ob_ref[0].astype(F32)
    normed = []
    for h in range(GLA_HEADS):
        oh = o[:, h * GLA_DV:(h + 1) * GLA_DV]
        ms = jnp.mean(oh * oh, axis=-1, keepdims=True)
        normed.append(oh * lax.rsqrt(ms + RMS_EPS))
    r = r_ref[0].astype(F32)
    y_gla = jnp.concatenate(normed, axis=-1) * gn_ref[...] * (r * _sigmoid(r))
    y = (_sigmoid(gna_ref[0].astype(F32)) * _dot(ona_ref[0], wna_ref[...])
         + _sigmoid(gsc_ref[0].astype(F32)) * _dot(o_sc, wsc_ref[...])
         + _sigmoid(ggl_ref[0].astype(F32)) * _dot(y_gla.astype(BF16), wgl_ref[...]))
    xn = x_ref[0] + gt_ref[0] * _dot(y.astype(BF16), wo_ref[...])
    xo_ref[0] = xn
    ms = jnp.mean(xn * xn, axis=-1, keepdims=True)
    h2 = xn * lax.rsqrt(ms + RMS_EPS) * g2_ref[...] * (1.0 + sc2_ref[0]) + sh2_ref[0]
    h2b = h2.astype(BF16)
    h2_ref[0] = h2b
    _store_parts(hp_ref, _pack_rows(h2))
    lg_ref[...] = _dot(h2b, wr_ref[...])


def _merge(o_na, conv_w, o_f, o_b, main, x, gt1, gn, w_na, w_sc, w_gla, w_out, g2, sh2, sc2, w_router, tm,
           n_routed, tok_off, routed=None):
    b, s, d = x.shape
    per_batch = gt1.shape[0] == b
    mod_map = (lambda bi, i: (bi, 0, 0)) if per_batch else (lambda bi, i: (0, 0, 0))
    tok = lambda width, blk: pl.BlockSpec((1, tm, width), lambda bi, i: (bi, i, blk))
    full = lambda arr: pl.BlockSpec(arr.shape, lambda bi, i: (0,) * arr.ndim)
    mod = pl.BlockSpec((1, 1, d), mod_map)
    gn_t = jnp.tile(gn, GLA_HEADS)[None]
    g2_t = g2[None]
    per_tile = tm // HALO
    last_halo = s // HALO - 1
    halo_prev = lambda blk: pl.BlockSpec(
        (1, HALO, SC_WIDTH), lambda bi, i: (bi, jnp.maximum(i * per_tile - 1, 0), blk))
    halo_next = lambda blk: pl.BlockSpec(
        (1, HALO, SC_WIDTH), lambda bi, i: (bi, jnp.minimum((i + 1) * per_tile, last_halo), blk))
    col_b, col_c, col_x = OFF_B_SC // SC_WIDTH, OFF_C_SC // SC_WIDTH, OFF_X_SC // SC_WIDTH
    extend = () if routed is None else tuple(routed)
    operands = (o_na, main, main, main, main, main, main, main, conv_w, o_f, o_b, main, main, main, main, x, gt1,
                gn_t, w_na, w_sc, w_gla, w_out, g2_t, sh2, sc2, w_router)
    n_in = len(operands)
    nblk = s // tm
    blk0 = tok_off // tm
    return pl.pallas_call(
        _merge_kernel,
        grid=(b, s // tm),
        in_specs=[tok(NA_WIDTH, 0), tok(SC_WIDTH, col_b), tok(SC_WIDTH, col_c), tok(SC_WIDTH, col_x),
                  halo_prev(col_c), halo_prev(col_x), halo_next(col_c), halo_next(col_x), full(conv_w),
                  tok(GLA_VAL_WIDTH, 0), tok(GLA_VAL_WIDTH, 0),
                  tok(d, OFF_R_GLA // d), tok(d, OFF_MERGE // d), tok(d, OFF_MERGE // d + 1),
                  tok(d, OFF_MERGE // d + 2), tok(d, 0), mod,
                  full(gn_t), full(w_na), full(w_sc), full(w_gla), full(w_out), full(g2_t), mod, mod,
                  full(w_router)] + [pl.BlockSpec(memory_space=pl.ANY)] * len(extend),
        out_specs=[tok(d, 0), tok(d, 0),
                   pl.BlockSpec((SC_PARTS, tm, SC_ROW), lambda bi, i: (0, blk0 + bi * nblk + i, 0)),
                   pl.BlockSpec((tm, LANE), lambda bi, i: (blk0 + bi * nblk + i, 0))],
        out_shape=[jax.ShapeDtypeStruct((b, s, d), F32),
                   jax.ShapeDtypeStruct((b, s, d), BF16),
                   jax.ShapeDtypeStruct((SC_PARTS, n_routed, SC_ROW), U32),
                   jax.ShapeDtypeStruct((n_routed, LANE), F32)],
        input_output_aliases={n_in: 2, n_in + 1: 3} if extend else {},
        compiler_params=_cparams(("parallel", "parallel")),
        name="merge",
    )(*operands, *extend)


def _router_kernel(lg_ref, br_ref, tri_ref, eidx_ref, gw_ref, rank_ref, cnt_ref, carry_ref):
    tm = lg_ref.shape[0]

    @pl.when(pl.program_id(0) == 0)
    def _():
        carry_ref[...] = jnp.zeros_like(carry_ref)

    scores = _sigmoid(lg_ref[...].T[:N_EXPERTS])
    sel = scores + br_ref[...]
    neg = -jnp.inf

    sel3 = sel.reshape(N_EXPERT_GROUPS, GROUP_SIZE, tm)
    i3 = lax.broadcasted_iota(jnp.int32, sel3.shape, 1)
    m1 = sel3.max(axis=1, keepdims=True)
    first = jnp.where(sel3 == m1, i3, GROUP_SIZE).min(axis=1, keepdims=True)
    m2 = jnp.where(i3 == first, neg, sel3).max(axis=1, keepdims=True)
    gscore = (m1 + m2)[:, 0, :]

    gi = lax.broadcasted_iota(jnp.int32, gscore.shape, 0)
    gmask = jnp.zeros(gscore.shape, jnp.bool_)
    for _ in range(TOPK_GROUPS):
        m = gscore.max(axis=0, keepdims=True)
        pick = gi == jnp.where(gscore == m, gi, N_EXPERT_GROUPS).min(axis=0, keepdims=True)
        gmask = gmask | pick
        gscore = jnp.where(pick, neg, gscore)
    emask = jnp.broadcast_to(gmask[:, None, :], sel3.shape).reshape(N_EXPERTS, tm)
    sel = jnp.where(emask, sel, neg)

    ei = lax.broadcasted_iota(jnp.int32, sel.shape, 0)
    picks, idxs, ws = [], [], []
    for _ in range(TOP_K):
        m = sel.max(axis=0, keepdims=True)
        idx = jnp.where(sel == m, ei, N_EXPERTS).min(axis=0, keepdims=True)
        pick = ei == idx
        picks.append(pick)
        idxs.append(idx)
        ws.append(jnp.where(pick, scores, 0.0).sum(axis=0, keepdims=True))
        sel = jnp.where(pick, neg, sel)
    w = jnp.concatenate(ws, axis=0)
    gw_ref[...] = w / w.sum(axis=0, keepdims=True) * ROUTED_SCALE
    eidx_ref[...] = jnp.concatenate(idxs, axis=0)

    onehot = picks[0]
    for p in picks[1:]:
        onehot = onehot | p
    onehot = jnp.where(onehot, 1.0, 0.0).astype(BF16)
    before = _dot(onehot, tri_ref[...]) + jnp.tile(carry_ref[...], (1, tm // LANE))
    rank_ref[...] = jnp.concatenate(
        [jnp.where(p, before, 0.0).sum(axis=0, keepdims=True) for p in picks], axis=0).astype(jnp.int32)
    carry_ref[...] += _dot(onehot, jnp.ones((tm, LANE), BF16))
    cnt_ref[...] = carry_ref[...]


def _route(logits, b_router, tm=512):
    t = logits.shape[0]
    br = jnp.broadcast_to(b_router.astype(F32)[:, None], (N_EXPERTS, tm))
    tri = jnp.asarray(np.triu(np.ones((tm, tm), np.float32), 1), dtype=BF16)
    kt = lambda dt: jax.ShapeDtypeStruct((TOP_K, t), dt)
    eidx, gw, rank, cnt = pl.pallas_call(
        _router_kernel,
        grid=(t // tm,),
        in_specs=[pl.BlockSpec((tm, LANE), lambda i: (i, 0)),
                  pl.BlockSpec((N_EXPERTS, tm), lambda i: (0, 0)),
                  pl.BlockSpec((tm, tm), lambda i: (0, 0))],
        out_specs=[pl.BlockSpec((TOP_K, tm), lambda i: (0, i)),
                   pl.BlockSpec((TOP_K, tm), lambda i: (0, i)),
                   pl.BlockSpec((TOP_K, tm), lambda i: (0, i)),
                   pl.BlockSpec((N_EXPERTS, LANE), lambda i: (0, 0))],
        out_shape=[kt(jnp.int32), kt(F32), kt(jnp.int32),
                   jax.ShapeDtypeStruct((N_EXPERTS, LANE), F32)],
        scratch_shapes=[pltpu.VMEM((N_EXPERTS, LANE), F32)],
        compiler_params=_cparams(("arbitrary",)),
        name="router",
    )(logits, br, tri)
    return eidx, gw, rank, cnt[:, 0].astype(jnp.int32)


def _sc_mesh():
    return plsc.VectorSubcoreMesh(core_axis_name="core", subcore_axis_name="subcore")


def _dispatch_rows(xp, dest, slots):
    parts, t, _ = xp.shape

    @pl.kernel(out_type=jax.ShapeDtypeStruct((parts, slots, SC_ROW), xp.dtype), mesh=_sc_mesh(),
               scratch_types=[], name="moe_dispatch")
    def run(x_hbm, d_hbm, o_hbm):
        for part in range(parts):
            out_part = o_hbm.at[part]

            def body(x_vmem, d_vmem, out_part=out_part):
                for k in range(TOP_K):
                    pltpu.sync_copy(x_vmem, out_part.at[d_vmem.at[k]])

            pltpu.emit_pipeline(
                body,
                grid=(t // SC_WINDOW,),
                in_specs=[pl.BlockSpec((SC_WINDOW, SC_ROW), lambda i: (i, 0)),
                          pl.BlockSpec((TOP_K, SC_WINDOW), lambda i: (0, i))],
                out_specs=[],
                core_axis_name=("core", "subcore"),
                dimension_semantics=(pltpu.PARALLEL,),
            )(x_hbm.at[part], d_hbm)

    return run(xp, dest)


def _gather_rows(yp, dest, t0, n):
    parts = yp.shape[0]
    nwin = n // SC_WINDOW
    win0 = t0 // SC_WINDOW

    @pl.kernel(out_type=jax.ShapeDtypeStruct((parts, TOP_K * n, SC_ROW), yp.dtype), mesh=_sc_mesh(),
               scratch_types=[], name="moe_gather")
    def run(y_hbm, d_hbm, o_hbm):
        for part in range(parts):
            table = y_hbm.at[part]

            def body(d_vmem, o_vmem, table=table):
                pltpu.sync_copy(table.at[d_vmem.at[0]], o_vmem)

            pltpu.emit_pipeline(
                body,
                grid=(TOP_K, nwin),
                in_specs=[pl.BlockSpec((1, SC_WINDOW), lambda k, j: (k, win0 + j))],
                out_specs=[pl.BlockSpec((SC_WINDOW, SC_ROW), lambda k, j: (k * nwin + j, 0))],
                core_axis_name=("core", "subcore"),
                dimension_semantics=(pltpu.PARALLEL, pltpu.PARALLEL),
            )(d_hbm, o_hbm.at[part])

    return run(yp, dest).reshape(parts, TOP_K, n, SC_ROW)


def _expert_kernel(be_ref, bv_ref, bs_ref, nx_ref, sl_ref, x_ref, wg_hbm, wu_hbm, wd_hbm, o_ref,
                   wg_f, wu_f, wd_f, wg_s, wu_s, wd_s, sems, *, layer):
    i = pl.program_id(0)
    valid = bv_ref[i]
    expert = be_ref[i]
    new_expert = (i == 0) | (expert != be_ref[jnp.maximum(i - 1, 0)])
    slot = sl_ref[i]

    def fetch(which, into):
        return [pltpu.make_async_copy(src.at[layer, which], dst.at[into], sems.at[into, j])
                for j, (src, dst) in enumerate(((wg_hbm, wg_f), (wu_hbm, wu_f), (wd_hbm, wd_f)))]

    @pl.when(i == 0)
    def _():
        for cp in fetch(expert, slot):
            cp.start()

    @pl.when(new_expert)
    def _():
        for cp in fetch(expert, slot):
            cp.wait()
        upcoming = nx_ref[i]

        @pl.when(upcoming >= 0)
        def _():
            for cp in fetch(upcoming, 1 - slot):
                cp.start()

        wg_s[...] = wg_f[slot].astype(BF16)
        wu_s[...] = wu_f[slot].astype(BF16)
        wd_s[...] = wd_f[slot].astype(BF16)

    @pl.when(valid > 0)
    def _():
        w = _load_parts(x_ref)
        row = lax.broadcasted_iota(jnp.int32, w.shape, 0)
        w = jnp.where(row < valid, w, jnp.uint32(0))
        lo, hi = _unpack_rows(w)
        x = jnp.concatenate([lo, hi], axis=1).astype(BF16)
        a = _dot(x, wg_s[...])
        hid = a * _sigmoid(a) * _dot(x, wu_s[...])
        _store_parts(o_ref, _pack_rows(_dot(hid.astype(BF16), wd_s[...])))


def _experts(xs, blk_e, blk_valid, blk_src, blk_next, blk_slot, layer, w_gate, w_up, w_down):
    parts, slots, _ = xs.shape
    d = D_MODEL
    nb = slots // MOE_BLOCK
    data = pl.BlockSpec((parts, MOE_BLOCK, SC_ROW), lambda i, be, bv, bs, nx, sl: (0, bs[i], 0))
    stage = lambda shape: pltpu.VMEM((2,) + shape, F32)
    return pl.pallas_call(
        functools.partial(_expert_kernel, layer=layer),
        grid_spec=pltpu.PrefetchScalarGridSpec(
            num_scalar_prefetch=5,
            grid=(nb,),
            in_specs=[data] + [pl.BlockSpec(memory_space=pl.ANY)] * 3,
            out_specs=data,
            scratch_shapes=[stage((d, EXPERT_FF)), stage((d, EXPERT_FF)), stage((EXPERT_FF, d)),
                            pltpu.VMEM((d, EXPERT_FF), BF16), pltpu.VMEM((d, EXPERT_FF), BF16),
                            pltpu.VMEM((EXPERT_FF, d), BF16), pltpu.SemaphoreType.DMA((2, 3))]),
        out_shape=jax.ShapeDtypeStruct((parts, slots, SC_ROW), U32),
        compiler_params=_cparams(("arbitrary",)),
        name="experts",
    )(blk_e, blk_valid, blk_src, blk_next, blk_slot, xs, w_gate, w_up, w_down)


def _combine_kernel(yg_ref, gw_ref, h_ref, x_ref, gt_ref, wsg_ref, wsu_ref, wsd_ref, gf_ref, o_ref, *, final):
    h = h_ref[0]
    a = _dot(h, wsg_ref[...])
    hid = a * _sigmoid(a) * _dot(h, wsu_ref[...])
    y = _dot(hid.astype(BF16), wsd_ref[...])
    gw = gw_ref[...]
    y_lo = y[:, :D_MODEL // 2]
    y_hi = y[:, D_MODEL // 2:]
    for k in range(TOP_K):
        lo, hi = _unpack_rows(_load_parts(yg_ref, k))
        y_lo = y_lo + gw[:, k:k + 1] * lo
        y_hi = y_hi + gw[:, k:k + 1] * hi
    y = jnp.concatenate([y_lo, y_hi], axis=1)
    xn = x_ref[0] + gt_ref[0] * y
    if final:
        ms = jnp.mean(xn * xn, axis=-1, keepdims=True)
        xn = xn * lax.rsqrt(ms + RMS_EPS) * gf_ref[...]
    o_ref[0] = xn


def _combine(yg, gw, tok_off, h2, x, gt2, b0, nb, ws_gate, ws_up, ws_down, g_final, final, tm):
    b, s, d = x.shape
    per_batch = gt2.shape[0] == b
    mod_map = (lambda bi, i: (b0 + bi, 0, 0)) if per_batch else (lambda bi, i: (0, 0, 0))
    full = lambda arr: pl.BlockSpec(arr.shape, lambda bi, i: (0,) * arr.ndim)
    tok = lambda width: pl.BlockSpec((1, tm, width), lambda bi, i: (b0 + bi, i, 0))
    gf = g_final[None]
    nblk = s // tm
    blk0 = (tok_off + b0 * s) // tm
    return pl.pallas_call(
        functools.partial(_combine_kernel, final=final),
        grid=(nb, nblk),
        in_specs=[pl.BlockSpec((SC_PARTS, TOP_K, tm, SC_ROW), lambda bi, i: (0, 0, bi * nblk + i, 0)),
                  pl.BlockSpec((tm, TOP_K), lambda bi, i: (blk0 + bi * nblk + i, 0)),
                  tok(d), tok(d),
                  pl.BlockSpec((1, 1, d), mod_map),
                  full(ws_gate), full(ws_up), full(ws_down), full(gf)],
        out_specs=tok(d),
        out_shape=jax.ShapeDtypeStruct((b, s, d), F32),
        input_output_aliases={3: 0},
        compiler_params=_cparams(("parallel", "parallel")),
        name="combine",
    )(yg, gw, h2, x, gt2, ws_gate, ws_up, ws_down, gf)


def _project_latent(x, p, b0=0, nb=None, prev=None):
    w_main, w_gate = p['prep']['w_in']
    mods = p['prep']['mods']
    return _proj_in(x, p['g_norm1'], mods[0], mods[1], w_main, w_gate, tm=min(2048, x.shape[1]), tn=1024,
                    b0=b0, nb=nb, prev=prev)


def _layer(x, ctx_s, p, ctx_out, final, g_final, projected=None, after_piece=None, p_next=None):
    b, s, d = x.shape
    sc = ctx_s.shape[1]
    prep = p['prep']
    sh1, sc1, gt1, sh2, sc2, gt2 = prep['mods']
    csh1, csc1, cgt1, csh2, csc2, cgt2 = prep['mods_ctx']

    w_main, w_gate = prep['w_in']
    main, gate = _project_latent(x, p) if projected is None else projected
    ctx_flat = ctx_s.reshape(1, b * sc, d)
    n_ctx, tn_ctx = (N_MAIN, 1024) if ctx_out else (N_KV_MAIN, N_KV_MAIN // 2)
    main_c, gate_c = _proj_in(ctx_flat, p['g_norm1'], csh1, csc1, w_main, w_gate, tm=min(1024, b * sc),
                              tn=tn_ctx, n=n_ctx)
    main_c = main_c.reshape(b, sc, n_ctx)
    gate_c = gate_c.reshape(b, sc, LANE)

    o_na = _na_latent(main, main_c, prep['na_bias'])

    gate_ws = prep['gate_ws']
    s0 = jnp.zeros((b, GLA_HEADS, GLA_DK, GLA_DV), F32)
    o_cf, o_cb, st_f, st_b = _gla_scan(main_c, gate_c, gate_ws, (s0, s0), ctx_out)
    o_f, o_b, _, _ = _gla_scan(main, gate, gate_ws, (st_f, st_b), True)

    w_na, w_sc, w_gla, w_out, w_router = (prep[name] for name in ('w_na', 'w_sc', 'w_gla', 'w_out', 'w_router'))
    n_lat = b * s
    t = n_lat + (b * sc if ctx_out else 0)
    x, h2, hp_all, lg_all = _merge(o_na, p['conv_w'], o_f, o_b, main, x, gt1, p['gla_norm_g'], w_na, w_sc, w_gla,
                                   w_out, p['g_norm2'], sh2, sc2, w_router, tm=min(512, s), n_routed=t, tok_off=0)
    if ctx_out:
        o_na_c = _dense_attn(main_c)
        ctx_s, h2_c, hp_all, lg_all = _merge(o_na_c, p['conv_w'], o_cf, o_cb, main_c, ctx_s, cgt1, p['gla_norm_g'],
                                             w_na, w_sc, w_gla, w_out, p['g_norm2'], csh2, csc2, w_router,
                                             tm=min(256, sc), n_routed=t, tok_off=n_lat, routed=(hp_all, lg_all))

    eidx, gw, rank, counts = _route(lg_all, p['b_router'])
    padded = (counts + MOE_BLOCK - 1) // MOE_BLOCK * MOE_BLOCK
    pad_end = jnp.cumsum(padded)
    pad_start = pad_end - padded
    onehot = eidx[:, :, None] == jnp.arange(N_EXPERTS, dtype=jnp.int32)
    dest = jnp.sum(jnp.where(onehot, pad_start, 0), axis=-1) + rank
    n_blocks = -(-(t * TOP_K + N_EXPERTS * (MOE_BLOCK - 1)) // MOE_BLOCK)
    slots = n_blocks * MOE_BLOCK
    blk_start = jnp.arange(n_blocks, dtype=jnp.int32) * MOE_BLOCK
    blk_e = jnp.minimum(jnp.sum(pad_end[None, :] <= blk_start[:, None], axis=1), N_EXPERTS - 1).astype(jnp.int32)
    used_end = (pad_start + counts)[blk_e]
    blk_valid = jnp.clip(used_end - blk_start, 0, MOE_BLOCK).astype(jnp.int32)
    n_used = pad_end[-1] // MOE_BLOCK
    blk_src = jnp.minimum(jnp.arange(n_blocks, dtype=jnp.int32), n_used - 1)
    blk_e = blk_e[blk_src]
    ids = jnp.arange(N_EXPERTS, dtype=jnp.int32)
    used = counts > 0
    later_used = jnp.where(used[None, :] & (ids[None, :] > ids[:, None]), ids[None, :], N_EXPERTS).min(axis=1)
    next_used = jnp.where(later_used == N_EXPERTS, -1, later_used).astype(jnp.int32)
    blk_next = next_used[blk_e]
    blk_slot = ((jnp.cumsum(used) - 1) % 2).astype(jnp.int32)[blk_e]

    xs = _dispatch_rows(hp_all, dest, slots)
    if p_next is not None:
        xs, p_next['prep'] = lax.optimization_barrier((xs, p_next['prep']))
    ys = _experts(xs, blk_e, blk_valid, blk_src, blk_next, blk_slot, p['layer'], p['w_exp_gate'], p['w_exp_up'], p['w_exp_down'])
    gw_t = gw.T
    ws_gate, ws_up, ws_down = prep['ws_gate'], prep['ws_up'], prep['ws_down']

    def gathered(t0, n):
        return _gather_rows(ys, dest, t0, n)

    pieces = next(n for n in (4, 2, 1) if b % n == 0)
    nb = b // pieces
    for q in range(pieces):
        x = _combine(gathered(q * nb * s, nb * s), gw_t, 0, h2, x, gt2, q * nb, nb, ws_gate, ws_up, ws_down,
                     g_final, final, tm=min(256, s))
        if after_piece is not None:
            after_piece(x, q * nb, nb)
    if ctx_out:
        ctx_s = _combine(gathered(n_lat, b * sc), gw_t, n_lat, h2_c, ctx_s, cgt2, 0, b, ws_gate, ws_up, ws_down,
                         g_final, False, tm=min(256, sc))
    return x, ctx_s


def kernel(x, c, ctx, c_ctx, w_mod, b_mod, g_norm1, g_norm2, w_in, na_rpb, w_branch_na, conv_w, w_branch_sc,
           gla_gate_w, gla_gate_b, gla_norm_g, w_branch_gla, w_out, w_router, b_router, w_exp_gate, w_exp_up,
           w_exp_down, w_sh_gate, w_sh_up, w_sh_down, g_final):
    stacked = dict(g_norm1=g_norm1, g_norm2=g_norm2, na_rpb=na_rpb, w_branch_na=w_branch_na,
                   conv_w=conv_w, w_branch_sc=w_branch_sc, gla_gate_w=gla_gate_w, gla_gate_b=gla_gate_b,
                   gla_norm_g=gla_norm_g, w_branch_gla=w_branch_gla, w_out=w_out, w_router=w_router,
                   b_router=b_router,
                   w_sh_gate=w_sh_gate, w_sh_up=w_sh_up, w_sh_down=w_sh_down)
    depth = w_in.shape[0]
    rows = x.shape[1] // GRID_W
    layers = []
    for i in range(depth):
        p = {name: arr[i] for name, arr in stacked.items()}
        p.update(layer=i, w_exp_gate=w_exp_gate, w_exp_up=w_exp_up, w_exp_down=w_exp_down)
        mods, mods_ctx = _mod_vectors(c, c_ctx, w_mod, b_mod, i)
        p['prep'] = dict(
            w_in=_prep_w_in(w_in, i), mods=mods, mods_ctx=mods_ctx,
            na_bias=_na_bias_table(p['na_rpb'], rows, min(NA_WIN_R, rows)),
            gate_ws=_gla_gate_weights(p['gla_gate_w'], p['gla_gate_b']),
            w_na=p['w_branch_na'].astype(BF16), w_sc=p['w_branch_sc'].astype(BF16),
            w_gla=p['w_branch_gla'].astype(BF16), w_out=p['w_out'].astype(BF16),
            w_router=jnp.pad(p['w_router'], ((0, 0), (0, LANE - N_EXPERTS))).astype(BF16),
            ws_gate=p['w_sh_gate'].astype(BF16), ws_up=p['w_sh_up'].astype(BF16),
            ws_down=p['w_sh_down'].astype(BF16))
        layers.append(p)

    ctx_s = ctx
    projected = None
    for i, p in enumerate(layers):
        last = i == depth - 1
        p_next = None if last else layers[i + 1]
        after_piece = None
        next_projected = []
        if not last:
            def after_piece(xq, b0, nb, p_next=p_next, acc=next_projected):
                acc.append(_project_latent(xq, p_next, b0, nb, acc[-1] if acc else None))

        x, ctx_s = _layer(x, ctx_s, p, not last, last, g_final, projected, after_piece, p_next)
        projected = next_projected[-1] if next_projected else None
    return x
```

```python
import functools

import numpy as np
import jax
import jax.numpy as jnp
from jax import lax
from jax.experimental import pallas as pl
from jax.experimental.pallas import tpu as pltpu
from jax.experimental.pallas import tpu_sc as plsc

F32 = jnp.float32
BF16 = jnp.bfloat16
U32 = jnp.uint32

D_MODEL = 1024
N_MOD = 6
RMS_EPS = 1e-6
NEG_INF = -1e30
GRID_W = 64
NA_HEADS = 8
NA_HEAD_DIM = 64
NA_WIDTH = NA_HEADS * NA_HEAD_DIM
NA_WIN_R = 8
NA_WIN_C = 16
NA_GROUP = 4
SC_WIDTH = 512
GLA_HEADS = 4
GLA_KEY_WIDTH = 512
GLA_VAL_WIDTH = 1024
GLA_DK = GLA_KEY_WIDTH // GLA_HEADS
GLA_DV = GLA_VAL_WIDTH // GLA_HEADS
GLA_GATE_RANK = 16
GLA_GATE_TAU = 16.0
LOG2_E = 1.4426950408889634
N_EXPERTS = 64
N_EXPERT_GROUPS = 8
GROUP_SIZE = N_EXPERTS // N_EXPERT_GROUPS
TOPK_GROUPS = 4
TOP_K = 8
EXPERT_FF = 256
ROUTED_SCALE = 2.5
MOE_BLOCK = 1024

LANE = 128
GLA_C = 128
GLA_LEVELS = tuple(GLA_C >> (i + 1) for i in range(GLA_C.bit_length() - 1))
VMEM_LIMIT = 48 * 1024 * 1024
SC_WINDOW = 128
SC_ROW = 256
SC_PARTS = D_MODEL // 2 // SC_ROW

OFF_V_GLA = 0
OFF_K_NA = 1024
OFF_V_NA = 1536
OFF_K_GLA = 2048
N_KV_MAIN = 2560
OFF_Q_NA = 2560
OFF_B_SC = 3072
OFF_C_SC = 3584
OFF_X_SC = 4096
OFF_Q_GLA = 4608
OFF_R_GLA = 5120
OFF_MERGE = 6144
N_MAIN = 9216


def _cparams(sem, vmem=VMEM_LIMIT):
    return pltpu.CompilerParams(dimension_semantics=sem, vmem_limit_bytes=vmem)


def _dot(a, b):
    return jnp.dot(a, b, preferred_element_type=F32)


def _dot_nt(a, b):
    return lax.dot_general(a, b, (((1,), (1,)), ((), ())), preferred_element_type=F32)


def _dot_tn(a, b):
    return lax.dot_general(a, b, (((0,), (0,)), ((), ())), preferred_element_type=F32)


def _sigmoid(x):
    return 0.5 * jnp.tanh(0.5 * x) + 0.5


def _pack_rows(x):
    n = x.shape[1] // 2
    r = x.astype(BF16).astype(F32)
    lo = pltpu.bitcast(r[:, :n], U32) >> 16
    hi = pltpu.bitcast(r[:, n:], U32)
    return hi | lo


def _store_parts(ref, words):
    for part in range(SC_PARTS):
        dst = ref.at[part, 0] if len(ref.shape) == 4 else ref.at[part]
        dst[...] = words[:, part * SC_ROW:(part + 1) * SC_ROW]


def _load_parts(ref, *lead):
    return jnp.concatenate([ref[(part,) + lead] for part in range(SC_PARTS)], axis=-1)


def _unpack_rows(w):
    lo = pltpu.bitcast(w << 16, F32)
    hi = pltpu.bitcast(w & jnp.uint32(0xFFFF0000), F32)
    return lo, hi


def _mod_kernel(a_ref, w_ref, b_ref, o_ref):
    a = a_ref[...]
    a = a * _sigmoid(a)
    o_ref[...] = _dot(a.astype(BF16), w_ref[0].astype(BF16)) + b_ref[0]


def _mod_vectors(c, c_ctx, w_mod, b_mod, layer):
    b = c.shape[0]
    rows = -(-(b + 1) // 8) * 8
    a = jnp.concatenate([c, c_ctx[None], jnp.zeros((rows - b - 1, D_MODEL), F32)], axis=0)
    n = N_MOD * D_MODEL
    tn = 1536
    out = pl.pallas_call(
        _mod_kernel,
        grid=(n // tn,),
        in_specs=[pl.BlockSpec((rows, D_MODEL), lambda j: (0, 0)),
                  pl.BlockSpec((1, D_MODEL, tn), lambda j: (layer, 0, j)),
                  pl.BlockSpec((1, 1, tn), lambda j: (layer, 0, j))],
        out_specs=pl.BlockSpec((rows, tn), lambda j: (0, j)),
        out_shape=jax.ShapeDtypeStruct((rows, n), F32),
        compiler_params=_cparams(("parallel",)),
        name="mod_vectors",
    )(a, w_mod, b_mod[:, None])
    lat = out[:b].reshape(b, N_MOD, 1, D_MODEL)
    ctx = out[b].reshape(N_MOD, 1, 1, D_MODEL)
    return [lat[:, i] for i in range(N_MOD)], [ctx[i] for i in range(N_MOD)]


def _proj_kernel(x_ref, g_ref, sh_ref, sc_ref, w_ref, wg_ref, *rest):
    o_ref, og_ref, h_ref = rest[-3:]

    @pl.when(pl.program_id(2) == 0)
    def _():
        x = x_ref[0]
        ms = jnp.mean(x * x, axis=-1, keepdims=True)
        h = x * lax.rsqrt(ms + RMS_EPS) * g_ref[...] * (1.0 + sc_ref[0]) + sh_ref[0]
        hb = h.astype(BF16)
        h_ref[...] = hb
        og_ref[0] = _dot(hb, wg_ref[...])

    o_ref[0] = _dot(h_ref[...], w_ref[...]).astype(o_ref.dtype)


W_IN_TILE = 512
W_IN_GATE_SHIFT = 2 * GLA_GATE_RANK


def _prep_w_in_kernel(a_ref, b_ref, o_ref, g_ref):
    t = pl.program_id(0)
    first_lat = N_KV_MAIN // W_IN_TILE
    a = a_ref[0]

    @pl.when(t < first_lat)
    def _():
        o_ref[...] = a.T.astype(BF16)

    @pl.when(t >= first_lat)
    def _():
        moved = jnp.concatenate([a[W_IN_GATE_SHIFT:], b_ref[0]], axis=0)
        scale = jnp.where(t == first_lat, NA_HEAD_DIM ** -0.5, 1.0)
        o_ref[...] = (moved * scale).T.astype(BF16)

    @pl.when(t == first_lat)
    def _():
        head = a[:LANE]
        row = lax.broadcasted_iota(jnp.int32, head.shape, 0)
        g_ref[...] = jnp.where(row < W_IN_GATE_SHIFT, head, 0.0).T.astype(BF16)


def _prep_w_in(w_in, layer):
    d = w_in.shape[1]
    w_t = jnp.swapaxes(w_in, 1, 2)
    first_lat = N_KV_MAIN // W_IN_TILE
    kv_perm = OFF_K_NA // W_IN_TILE

    def a_map(t):
        return (layer, jnp.where(t < first_lat, (t + first_lat - kv_perm) % first_lat, t), 0)

    def b_map(t):
        return (layer, jnp.where(t < first_lat, 0, (t + 1) * (W_IN_TILE // W_IN_GATE_SHIFT)), 0)

    return pl.pallas_call(
        _prep_w_in_kernel,
        grid=(N_MAIN // W_IN_TILE,),
        in_specs=[pl.BlockSpec((1, W_IN_TILE, d), a_map),
                  pl.BlockSpec((1, W_IN_GATE_SHIFT, d), b_map)],
        out_specs=[pl.BlockSpec((d, W_IN_TILE), lambda t: (0, t)),
                   pl.BlockSpec((d, LANE), lambda t: (0, 0))],
        out_shape=[jax.ShapeDtypeStruct((d, N_MAIN), BF16), jax.ShapeDtypeStruct((d, LANE), BF16)],
        compiler_params=_cparams(("arbitrary",)),
        name="prep_w_in",
    )(w_t, w_t)


def _proj_in(x, g, shift, scale, w_main, w_gate, tm, tn, n=None, b0=0, nb=None, prev=None):
    b, s, d = x.shape
    n = w_main.shape[1] if n is None else n
    nb = b if nb is None else nb
    per_batch = shift.shape[0] == b
    mod_map = (lambda bi, i, j: (b0 + bi, 0, 0)) if per_batch else (lambda bi, i, j: (0, 0, 0))
    operands = (x, g[None], shift, scale, w_main, w_gate)
    extend = () if prev is None else tuple(prev)
    return pl.pallas_call(
        _proj_kernel,
        grid=(nb, s // tm, n // tn),
        in_specs=[pl.BlockSpec((1, tm, d), lambda bi, i, j: (b0 + bi, i, 0)),
                  pl.BlockSpec((1, d), lambda bi, i, j: (0, 0)),
                  pl.BlockSpec((1, 1, d), mod_map),
                  pl.BlockSpec((1, 1, d), mod_map),
                  pl.BlockSpec((d, tn), lambda bi, i, j: (0, j)),
                  pl.BlockSpec((d, LANE), lambda bi, i, j: (0, 0))]
                 + [pl.BlockSpec(memory_space=pl.ANY)] * len(extend),
        out_specs=[pl.BlockSpec((1, tm, tn), lambda bi, i, j: (b0 + bi, i, j)),
                   pl.BlockSpec((1, tm, LANE), lambda bi, i, j: (b0 + bi, i, 0))],
        out_shape=[jax.ShapeDtypeStruct((b, s, n), BF16),
                   jax.ShapeDtypeStruct((b, s, LANE), F32)],
        scratch_shapes=[pltpu.VMEM((tm, d), BF16)],
        input_output_aliases={len(operands): 0, len(operands) + 1: 1} if extend else {},
        compiler_params=_cparams(("parallel", "parallel", "arbitrary")),
        name="proj_in",
    )(*operands, *extend)


def _softmax_av(q, keys, vals, biases):
    scores = []
    for kk, bb in zip(keys, biases):
        s = _dot_nt(q, kk)
        scores.append(s if bb is None else s + bb)
    m = scores[0].max(axis=-1, keepdims=True)
    for s in scores[1:]:
        m = jnp.maximum(m, s.max(axis=-1, keepdims=True))
    num = None
    den = None
    for s, vv in zip(scores, vals):
        e = jnp.exp(s - m)
        dsum = e.sum(axis=-1, keepdims=True)
        o = _dot(e.astype(BF16), vv)
        num = o if num is None else num + o
        den = dsum if den is None else den + dsum
    return num / den


def _na_kernel(q_ref, k_ref, v_ref, kc_ref, vc_ref, *rest, rows, kr):
    *bias_refs, o_ref = rest
    kc = kc_ref[0]
    vc = vc_ref[0]
    for j, bias_ref in enumerate(bias_refs):
        r = pl.program_id(1) * len(bias_refs) + j
        row_start = jnp.clip(r - kr // 2, 0, rows - kr)
        start = pl.multiple_of(row_start * GRID_W, GRID_W)
        n_win = kr * GRID_W
        q = q_ref[0, j * GRID_W:(j + 1) * GRID_W, :]
        kw = k_ref[0, pl.ds(start, n_win), :]
        vw = v_ref[0, pl.ds(start, n_win), :]
        o_ref[0, j * GRID_W:(j + 1) * GRID_W, :] = _na_row(q, kw, vw, kc, vc, bias_ref).astype(o_ref.dtype)


def _na_row(q, kw, vw, kc, vc, bias_ref):
    gw = NA_GROUP * NA_HEAD_DIM
    stacked = (NA_GROUP * GRID_W, gw)
    on_head = (lax.broadcasted_iota(jnp.int32, stacked, 0) // GRID_W
               == lax.broadcasted_iota(jnp.int32, stacked, 1) // NA_HEAD_DIM)
    outs = []
    for g in range(NA_HEADS // NA_GROUP):
        sl = slice(g * gw, (g + 1) * gw)
        q_all = jnp.where(on_head, jnp.concatenate([q[:, sl]] * NA_GROUP, axis=0), jnp.zeros((), q.dtype))
        bias = bias_ref[0, g * NA_GROUP * GRID_W:(g + 1) * NA_GROUP * GRID_W, :]
        o_all = _softmax_av(q_all, [kw[:, sl], kc[:, sl]], [vw[:, sl], vc[:, sl]], [bias, None])
        o_all = jnp.where(on_head, o_all, 0.0).reshape(NA_GROUP, GRID_W, gw)
        outs.append(o_all.sum(axis=0))
    return jnp.concatenate(outs, axis=-1)


def _na_bias_table(rpb, rows, kr):
    col = np.arange(GRID_W)
    col_start = np.clip(col - NA_WIN_C // 2, 0, GRID_W - NA_WIN_C)
    col_ok = (col[None, :] >= col_start[:, None]) & (col[None, :] < col_start[:, None] + NA_WIN_C)
    d_col = np.clip(col[None, :] - col[:, None], -(NA_WIN_C - 1), NA_WIN_C - 1) + NA_WIN_C - 1
    n_dr, n_dc = rpb.shape[1], rpb.shape[2]
    onehot = jnp.asarray((d_col.reshape(-1)[None, :] == np.arange(n_dc)[:, None]).astype(np.float32))
    by_col = jnp.dot(rpb.astype(F32).reshape(NA_HEADS * n_dr, n_dc), onehot, precision=lax.Precision.HIGHEST)
    by_col = by_col.reshape(NA_HEADS, n_dr, GRID_W, GRID_W)
    by_col = jnp.where(col_ok[None, None], by_col, NEG_INF)
    tables = []
    for o in range(kr):
        lo = NA_WIN_R - 1 - o
        tables.append(by_col[:, lo:lo + kr].transpose(0, 2, 1, 3).reshape(NA_HEADS, GRID_W, kr * GRID_W))
    return jnp.stack(tables).reshape(kr, NA_HEADS * GRID_W, kr * GRID_W)


def _na_latent(main, main_ctx, bias):
    b, s, _ = main.shape
    sc = main_ctx.shape[1]
    rows = s // GRID_W
    kr = min(NA_WIN_R, rows)
    w = NA_WIDTH

    per_step = next(n for n in (8, 4, 2, 1) if rows % n == 0)

    def bias_spec(j):
        def bias_map(bi, i):
            r = i * per_step + j
            return (r - jnp.clip(r - kr // 2, 0, rows - kr), 0, 0)
        return pl.BlockSpec((1, NA_HEADS * GRID_W, kr * GRID_W), bias_map)

    return pl.pallas_call(
        functools.partial(_na_kernel, rows=rows, kr=kr),
        grid=(b, rows // per_step),
        in_specs=[pl.BlockSpec((1, per_step * GRID_W, w), lambda bi, i: (bi, i, OFF_Q_NA // w)),
                  pl.BlockSpec((1, s, w), lambda bi, i: (bi, 0, OFF_K_NA // w)),
                  pl.BlockSpec((1, s, w), lambda bi, i: (bi, 0, OFF_V_NA // w)),
                  pl.BlockSpec((1, sc, w), lambda bi, i: (bi, 0, OFF_K_NA // w)),
                  pl.BlockSpec((1, sc, w), lambda bi, i: (bi, 0, OFF_V_NA // w))]
                 + [bias_spec(j) for j in range(per_step)],
        out_specs=pl.BlockSpec((1, per_step * GRID_W, w), lambda bi, i: (bi, i, 0)),
        out_shape=jax.ShapeDtypeStruct((b, s, w), BF16),
        compiler_params=_cparams(("parallel", "arbitrary")),
        name="na_latent",
    )(main, main, main, main_ctx, main_ctx, *([bias] * per_step))


def _dense_attn_kernel(q_ref, k_ref, v_ref, o_ref):
    q = q_ref[0]
    k = k_ref[0]
    v = v_ref[0]
    outs = []
    for h in range(NA_HEADS):
        sl = slice(h * NA_HEAD_DIM, (h + 1) * NA_HEAD_DIM)
        outs.append(_softmax_av(q[:, sl], [k[:, sl]], [v[:, sl]], [None]))
    o_ref[0] = jnp.concatenate(outs, axis=-1).astype(o_ref.dtype)


def _dense_attn(main_ctx):
    b, sc, _ = main_ctx.shape
    w = NA_WIDTH
    return pl.pallas_call(
        _dense_attn_kernel,
        grid=(b,),
        in_specs=[pl.BlockSpec((1, sc, w), lambda bi: (bi, 0, OFF_Q_NA // w)),
                  pl.BlockSpec((1, sc, w), lambda bi: (bi, 0, OFF_K_NA // w)),
                  pl.BlockSpec((1, sc, w), lambda bi: (bi, 0, OFF_V_NA // w))],
        out_specs=pl.BlockSpec((1, sc, w), lambda bi: (bi, 0, 0)),
        out_shape=jax.ShapeDtypeStruct((b, sc, w), BF16),
        compiler_params=_cparams(("parallel",)),
        name="ctx_attn",
    )(main_ctx, main_ctx, main_ctx)


HALO = 16


def _gla_matrices(reverse):
    c = GLA_C
    t = np.arange(c)[:, None]
    m = np.arange(c)[None, :]
    blocks = [m <= t, m > t]
    for b in GLA_LEVELS:
        first = (t // (2 * b)) * (2 * b) + b
        is_q = (t & b) != 0
        blocks.append(np.where(is_q, (m > first) & (m <= t), (m > t) & (m <= first)))
    mats = np.stack(blocks).astype(np.float32)
    if reverse:
        mats = mats[:, ::-1, ::-1]
    mats = mats.reshape(-1, c)
    return jnp.asarray(np.concatenate([mats, mats], axis=1), dtype=BF16)


def _split_bf16(x):
    hi = x.astype(BF16)
    return hi, (x - hi.astype(F32)).astype(BF16)


def _pair_block_diag(x):
    lane = lax.broadcasted_iota(jnp.int32, x.shape, 1)
    zero = jnp.zeros((), x.dtype)
    return jnp.concatenate([jnp.where(lane < GLA_DK, x, zero), jnp.where(lane >= GLA_DK, x, zero)], axis=0)


def _gla_kernel(*refs, emit):
    n_in = 9 if emit else 8
    n_out = 2 if emit else 1
    ins, outs, scratch = refs[:2 * n_in], refs[2 * n_in:2 * (n_in + n_out)], refs[2 * (n_in + n_out):]
    ins = [ins[d * n_in:(d + 1) * n_in] for d in range(2)]
    outs = [outs[d * n_out:(d + 1) * n_out] for d in range(2)]
    step = pl.program_id(1)

    @pl.when(step == 0)
    def _():
        for d in range(2):
            scratch[d][...] = ins[d][-1][0]

    pending = [_gla_direction(ins[d][:-1], outs[d][:-1], scratch[d], reverse, emit)
               for d, reverse in enumerate((False, True))]
    while pending:
        pending = [stages for stages in pending if next(stages, "done") != "done"]

    @pl.when(step == pl.num_programs(1) - 1)
    def _():
        for d in range(2):
            outs[d][-1][0] = scratch[d][...]


def _gla_direction(ins, outs, st_ref, reverse, emit):
    if emit:
        q_ref, k_ref, v_ref, gt_ref, w2a_ref, w2b_ref, b2_ref, a_ref = ins
        (o_ref,) = outs
    else:
        k_ref, v_ref, gt_ref, w2a_ref, w2b_ref, b2_ref, a_ref = ins
    c = GLA_C
    pw = 2 * GLA_DK

    lr_hi, lr_lo = _split_bf16(gt_ref[0])
    logit = (_dot(jnp.concatenate([lr_hi, lr_lo], axis=1), w2a_ref[...]) + _dot(lr_hi, w2b_ref[...])
             + b2_ref[...])
    g = (jnp.minimum(logit, 0.0) - jnp.log1p(jnp.exp(-jnp.abs(logit)))) * (LOG2_E / GLA_GATE_TAU)
    g_hi, g_lo = _split_bf16(g)
    args = _dot(a_ref[...], jnp.concatenate([g_hi, g_lo], axis=0))
    cum = args[0:c]
    rem = args[c:2 * c]
    last_row = cum[0:1] if reverse else cum[c - 1:c]
    yield

    k = k_ref[0].astype(F32)
    v = v_ref[0]
    atts = []
    if emit:
        q = q_ref[0].astype(F32) * (GLA_DK ** -0.5)
        row_t = lax.broadcasted_iota(jnp.int32, (c, pw), 0)
        si = lax.broadcasted_iota(jnp.int32, (2 * c, c), 0) & (c - 1)
        ti = lax.broadcasted_iota(jnp.int32, (2 * c, c), 1)
        if reverse:
            row_t, ti, si = c - 1 - row_t, c - 1 - ti, c - 1 - si
        for hp in range(GLA_HEADS // 2):
            cs = slice(hp * pw, (hp + 1) * pw)
            qp, kp = q[:, cs], k[:, cs]
            att = jnp.where(ti == si, _dot_nt(_pair_block_diag(kp.astype(BF16)), qp.astype(BF16)), 0.0)
            for l, b in enumerate(GLA_LEVELS):
                x = (jnp.exp2(args[(2 + l) * c:(3 + l) * c, cs])
                     * jnp.where((row_t & b) != 0, qp, kp)).astype(BF16)
                pair = (((ti ^ si) >> (b.bit_length() - 1)) == 1) & ((ti & b) != 0)
                att = jnp.where(pair, _dot_nt(_pair_block_diag(x), x), att)
                yield
            atts.append(att.astype(BF16))

    outs = []
    for h in range(GLA_HEADS):
        sl = slice(h * GLA_DK, (h + 1) * GLA_DK)
        kh = k[:, sl]
        vh = v[:, h * GLA_DV:(h + 1) * GLA_DV]
        state = st_ref[h]
        kd = (kh * jnp.exp2(rem[:, sl])).astype(BF16)
        decay = jnp.exp2(jnp.broadcast_to(last_row[:, sl], (GLA_DK, GLA_DK))).T
        decay = jnp.concatenate([decay] * (GLA_DV // GLA_DK), axis=1)
        if emit:
            qd = (q[:, sl] * jnp.exp2(cum[:, sl])).astype(BF16)
            att_t = atts[h // 2][(h % 2) * c:(h % 2 + 1) * c]
            both = _dot_tn(jnp.concatenate([att_t, kd], axis=1), vh)
            outs.append(_dot(qd, state.astype(BF16)) + both[:c])
            st_ref[h] = decay * state + both[c:]
        else:
            st_ref[h] = decay * state + _dot_tn(kd, vh)
        yield

    if emit:
        o_ref[0] = jnp.concatenate(outs, axis=-1).astype(o_ref.dtype)


def _gla_scan(main, gate, gate_ws, s0s, emit):
    b, l, _ = main.shape
    n = l // GLA_C
    const = lambda arr: pl.BlockSpec(arr.shape, lambda bi, s: (0,) * arr.ndim)
    state_spec = pl.BlockSpec((1, GLA_HEADS, GLA_DK, GLA_DV), lambda bi, s: (bi, 0, 0, 0))
    state_shape = jax.ShapeDtypeStruct((b, GLA_HEADS, GLA_DK, GLA_DV), F32)
    in_specs, args, out_specs, out_shape = [], [], [], []
    for reverse in (False, True):
        amat = _gla_matrices(reverse)
        w2a, w2b, b2 = gate_ws[reverse]

        def col(block, reverse=reverse):
            return lambda bi, s: (bi, n - 1 - s if reverse else s, block)

        if emit:
            in_specs.append(pl.BlockSpec((1, GLA_C, GLA_KEY_WIDTH), col(OFF_Q_GLA // GLA_KEY_WIDTH)))
            args.append(main)
            out_specs.append(pl.BlockSpec((1, GLA_C, GLA_VAL_WIDTH), col(0)))
            out_shape.append(jax.ShapeDtypeStruct((b, l, GLA_VAL_WIDTH), BF16))
        in_specs += [pl.BlockSpec((1, GLA_C, GLA_KEY_WIDTH), col(OFF_K_GLA // GLA_KEY_WIDTH)),
                     pl.BlockSpec((1, GLA_C, GLA_VAL_WIDTH), col(OFF_V_GLA // GLA_VAL_WIDTH)),
                     pl.BlockSpec((1, GLA_C, LANE), col(0)),
                     const(w2a), const(w2b), const(b2), const(amat), state_spec]
        args += [main, main, gate, w2a, w2b, b2, amat, s0s[reverse]]
        out_specs.append(state_spec)
        out_shape.append(state_shape)
    res = pl.pallas_call(
        functools.partial(_gla_kernel, emit=emit),
        grid=(b, n),
        in_specs=in_specs,
        out_specs=out_specs,
        out_shape=out_shape,
        scratch_shapes=[pltpu.VMEM((GLA_HEADS, GLA_DK, GLA_DV), F32)] * 2,
        compiler_params=_cparams(("parallel", "arbitrary")),
        name="gla_scan",
    )(*args)
    return (res[0], res[2], res[1], res[3]) if emit else (None, None, res[0], res[1])


def _gla_gate_weights(gate_w, gate_b):
    out = []
    for dr in range(2):
        w = jnp.zeros((LANE, GLA_KEY_WIDTH), F32)
        w = w.at[dr * GLA_GATE_RANK:(dr + 1) * GLA_GATE_RANK].set(gate_w[dr])
        w_hi = w.astype(BF16)
        w_lo = (w - w_hi.astype(F32)).astype(BF16)
        out.append((jnp.concatenate([w_hi, w_hi], axis=0), w_lo, gate_b[dr][None]))
    return out


def _conv_tile(b_ref, c_ref, x_ref, cp_ref, xp_ref, cn_ref, xn_ref, w_ref):
    i, n = pl.program_id(1), pl.num_programs(1)
    u = c_ref[0].astype(F32) * x_ref[0].astype(F32)
    tm = u.shape[0]
    before = jnp.where(i > 0, 1.0, 0.0) * (cp_ref[0, HALO - 1:HALO].astype(F32) * xp_ref[0, HALO - 1:HALO].astype(F32))
    after = jnp.where(i < n - 1, 1.0, 0.0) * (cn_ref[0, 0:1].astype(F32) * xn_ref[0, 0:1].astype(F32))
    t = lax.broadcasted_iota(jnp.int32, u.shape, 0)
    prev = jnp.where(t == 0, before, pltpu.roll(u, 1, axis=0))
    nxt = jnp.where(t == tm - 1, after, pltpu.roll(u, tm - 1, axis=0))
    w = w_ref[...]
    return b_ref[0].astype(F32) * (prev * w[0:1] + u * w[1:2] + nxt * w[2:3])


def _merge_kernel(ona_ref, bsc_ref, csc_ref, xsc_ref, cp_ref, xp_ref, cn_ref, xn_ref, cw_ref, of_ref, ob_ref,
                  r_ref, gna_ref, gsc_ref, ggl_ref, x_ref, gt_ref,
                  gn_ref, wna_ref, wsc_ref, wgl_ref, wo_ref, g2_ref, sh2_ref, sc2_ref, wr_ref, *rest):
    xo_ref, h2_ref, hp_ref, lg_ref = rest[-4:]
    o_sc = _conv_tile(bsc_ref, csc_ref, xsc_ref, cp_ref, xp_ref, cn_ref, xn_ref, cw_ref).astype(BF16)
    o = of_ref[0].astype(F32) + ob_ref[0].astype(F32)
    normed = []
    for h in range(GLA_HEADS):
        oh = o[:, h * GLA_DV:(h + 1) * GLA_DV]
        ms = jnp.mean(oh * oh, axis=-1, keepdims=True)
        normed.append(oh * lax.rsqrt(ms + RMS_EPS))
    r = r_ref[0].astype(F32)
    y_gla = jnp.concatenate(normed, axis=-1) * gn_ref[...] * (r * _sigmoid(r))
    y = (_sigmoid(gna_ref[0].astype(F32)) * _dot(ona_ref[0], wna_ref[...])
         + _sigmoid(gsc_ref[0].astype(F32)) * _dot(o_sc, wsc_ref[...])
         + _sigmoid(ggl_ref[0].astype(F32)) * _dot(y_gla.astype(BF16), wgl_ref[...]))
    xn = x_ref[0] + gt_ref[0] * _dot(y.astype(BF16), wo_ref[...])
    xo_ref[0] = xn
    ms = jnp.mean(xn * xn, axis=-1, keepdims=True)
    h2 = xn * lax.rsqrt(ms + RMS_EPS) * g2_ref[...] * (1.0 + sc2_ref[0]) + sh2_ref[0]
    h2b = h2.astype(BF16)
    h2_ref[0] = h2b
    _store_parts(hp_ref, _pack_rows(h2))
    lg_ref[...] = _dot(h2b, wr_ref[...])


def _merge(o_na, conv_w, o_f, o_b, main, x, gt1, gn, w_na, w_sc, w_gla, w_out, g2, sh2, sc2, w_router, tm,
           n_routed, tok_off, routed=None):
    b, s, d = x.shape
    per_batch = gt1.shape[0] == b
    mod_map = (lambda bi, i: (bi, 0, 0)) if per_batch else (lambda bi, i: (0, 0, 0))
    tok = lambda width, blk: pl.BlockSpec((1, tm, width), lambda bi, i: (bi, i, blk))
    full = lambda arr: pl.BlockSpec(arr.shape, lambda bi, i: (0,) * arr.ndim)
    mod = pl.BlockSpec((1, 1, d), mod_map)
    gn_t = jnp.tile(gn, GLA_HEADS)[None]
    g2_t = g2[None]
    per_tile = tm // HALO
    last_halo = s // HALO - 1
    halo_prev = lambda blk: pl.BlockSpec(
        (1, HALO, SC_WIDTH), lambda bi, i: (bi, jnp.maximum(i * per_tile - 1, 0), blk))
    halo_next = lambda blk: pl.BlockSpec(
        (1, HALO, SC_WIDTH), lambda bi, i: (bi, jnp.minimum((i + 1) * per_tile, last_halo), blk))
    col_b, col_c, col_x = OFF_B_SC // SC_WIDTH, OFF_C_SC // SC_WIDTH, OFF_X_SC // SC_WIDTH
    extend = () if routed is None else tuple(routed)
    operands = (o_na, main, main, main, main, main, main, main, conv_w, o_f, o_b, main, main, main, main, x, gt1,
                gn_t, w_na, w_sc, w_gla, w_out, g2_t, sh2, sc2, w_router)
    n_in = len(operands)
    nblk = s // tm
    blk0 = tok_off // tm
    return pl.pallas_call(
        _merge_kernel,
        grid=(b, s // tm),
        in_specs=[tok(NA_WIDTH, 0), tok(SC_WIDTH, col_b), tok(SC_WIDTH, col_c), tok(SC_WIDTH, col_x),
                  halo_prev(col_c), halo_prev(col_x), halo_next(col_c), halo_next(col_x), full(conv_w),
                  tok(GLA_VAL_WIDTH, 0), tok(GLA_VAL_WIDTH, 0),
                  tok(d, OFF_R_GLA // d), tok(d, OFF_MERGE // d), tok(d, OFF_MERGE // d + 1),
                  tok(d, OFF_MERGE // d + 2), tok(d, 0), mod,
                  full(gn_t), full(w_na), full(w_sc), full(w_gla), full(w_out), full(g2_t), mod, mod,
                  full(w_router)] + [pl.BlockSpec(memory_space=pl.ANY)] * len(extend),
        out_specs=[tok(d, 0), tok(d, 0),
                   pl.BlockSpec((SC_PARTS, tm, SC_ROW), lambda bi, i: (0, blk0 + bi * nblk + i, 0)),
                   pl.BlockSpec((tm, LANE), lambda bi, i: (blk0 + bi * nblk + i, 0))],
        out_shape=[jax.ShapeDtypeStruct((b, s, d), F32),
                   jax.ShapeDtypeStruct((b, s, d), BF16),
                   jax.ShapeDtypeStruct((SC_PARTS, n_routed, SC_ROW), U32),
                   jax.ShapeDtypeStruct((n_routed, LANE), F32)],
        input_output_aliases={n_in: 2, n_in + 1: 3} if extend else {},
        compiler_params=_cparams(("parallel", "parallel")),
        name="merge",
    )(*operands, *extend)


def _router_kernel(lg_ref, br_ref, tri_ref, eidx_ref, gw_ref, rank_ref, cnt_ref, carry_ref):
    tm = lg_ref.shape[0]

    @pl.when(pl.program_id(0) == 0)
    def _():
        carry_ref[...] = jnp.zeros_like(carry_ref)

    scores = _sigmoid(lg_ref[...].T[:N_EXPERTS])
    sel = scores + br_ref[...]
    neg = -jnp.inf

    sel3 = sel.reshape(N_EXPERT_GROUPS, GROUP_SIZE, tm)
    i3 = lax.broadcasted_iota(jnp.int32, sel3.shape, 1)
    m1 = sel3.max(axis=1, keepdims=True)
    first = jnp.where(sel3 == m1, i3, GROUP_SIZE).min(axis=1, keepdims=True)
    m2 = jnp.where(i3 == first, neg, sel3).max(axis=1, keepdims=True)
    gscore = (m1 + m2)[:, 0, :]

    gi = lax.broadcasted_iota(jnp.int32, gscore.shape, 0)
    gmask = jnp.zeros(gscore.shape, jnp.bool_)
    for _ in range(TOPK_GROUPS):
        m = gscore.max(axis=0, keepdims=True)
        pick = gi == jnp.where(gscore == m, gi, N_EXPERT_GROUPS).min(axis=0, keepdims=True)
        gmask = gmask | pick
        gscore = jnp.where(pick, neg, gscore)
    emask = jnp.broadcast_to(gmask[:, None, :], sel3.shape).reshape(N_EXPERTS, tm)
    sel = jnp.where(emask, sel, neg)

    ei = lax.broadcasted_iota(jnp.int32, sel.shape, 0)
    picks, idxs, ws = [], [], []
    for _ in range(TOP_K):
        m = sel.max(axis=0, keepdims=True)
        idx = jnp.where(sel == m, ei, N_EXPERTS).min(axis=0, keepdims=True)
        pick = ei == idx
        picks.append(pick)
        idxs.append(idx)
        ws.append(jnp.where(pick, scores, 0.0).sum(axis=0, keepdims=True))
        sel = jnp.where(pick, neg, sel)
    w = jnp.concatenate(ws, axis=0)
    gw_ref[...] = w / w.sum(axis=0, keepdims=True) * ROUTED_SCALE
    eidx_ref[...] = jnp.concatenate(idxs, axis=0)

    onehot = picks[0]
    for p in picks[1:]:
        onehot = onehot | p
    onehot = jnp.where(onehot, 1.0, 0.0).astype(BF16)
    before = _dot(onehot, tri_ref[...]) + jnp.tile(carry_ref[...], (1, tm // LANE))
    rank_ref[...] = jnp.concatenate(
        [jnp.where(p, before, 0.0).sum(axis=0, keepdims=True) for p in picks], axis=0).astype(jnp.int32)
    carry_ref[...] += _dot(onehot, jnp.ones((tm, LANE), BF16))
    cnt_ref[...] = carry_ref[...]


def _route(logits, b_router, tm=512):
    t = logits.shape[0]
    br = jnp.broadcast_to(b_router.astype(F32)[:, None], (N_EXPERTS, tm))
    tri = jnp.asarray(np.triu(np.ones((tm, tm), np.float32), 1), dtype=BF16)
    kt = lambda dt: jax.ShapeDtypeStruct((TOP_K, t), dt)
    eidx, gw, rank, cnt = pl.pallas_call(
        _router_kernel,
        grid=(t // tm,),
        in_specs=[pl.BlockSpec((tm, LANE), lambda i: (i, 0)),
                  pl.BlockSpec((N_EXPERTS, tm), lambda i: (0, 0)),
                  pl.BlockSpec((tm, tm), lambda i: (0, 0))],
        out_specs=[pl.BlockSpec((TOP_K, tm), lambda i: (0, i)),
                   pl.BlockSpec((TOP_K, tm), lambda i: (0, i)),
                   pl.BlockSpec((TOP_K, tm), lambda i: (0, i)),
                   pl.BlockSpec((N_EXPERTS, LANE), lambda i: (0, 0))],
        out_shape=[kt(jnp.int32), kt(F32), kt(jnp.int32),
                   jax.ShapeDtypeStruct((N_EXPERTS, LANE), F32)],
        scratch_shapes=[pltpu.VMEM((N_EXPERTS, LANE), F32)],
        compiler_params=_cparams(("arbitrary",)),
        name="router",
    )(logits, br, tri)
    return eidx, gw, rank, cnt[:, 0].astype(jnp.int32)


def _sc_mesh():
    return plsc.VectorSubcoreMesh(core_axis_name="core", subcore_axis_name="subcore")


def _dispatch_rows(xp, dest, slots):
    parts, t, _ = xp.shape

    @pl.kernel(out_type=jax.ShapeDtypeStruct((parts, slots, SC_ROW), xp.dtype), mesh=_sc_mesh(),
               scratch_types=[], name="moe_dispatch")
    def run(x_hbm, d_hbm, o_hbm):
        for part in range(parts):
            out_part = o_hbm.at[part]

            def body(x_vmem, d_vmem, out_part=out_part):
                for k in range(TOP_K):
                    pltpu.sync_copy(x_vmem, out_part.at[d_vmem.at[k]])

            pltpu.emit_pipeline(
                body,
                grid=(t // SC_WINDOW,),
                in_specs=[pl.BlockSpec((SC_WINDOW, SC_ROW), lambda i: (i, 0)),
                          pl.BlockSpec((TOP_K, SC_WINDOW), lambda i: (0, i))],
                out_specs=[],
                core_axis_name=("core", "subcore"),
                dimension_semantics=(pltpu.PARALLEL,),
            )(x_hbm.at[part], d_hbm)

    return run(xp, dest)


def _gather_rows(yp, dest, t0, n):
    parts = yp.shape[0]
    nwin = n // SC_WINDOW
    win0 = t0 // SC_WINDOW

    @pl.kernel(out_type=jax.ShapeDtypeStruct((parts, TOP_K * n, SC_ROW), yp.dtype), mesh=_sc_mesh(),
               scratch_types=[], name="moe_gather")
    def run(y_hbm, d_hbm, o_hbm):
        for part in range(parts):
            table = y_hbm.at[part]

            def body(d_vmem, o_vmem, table=table):
                pltpu.sync_copy(table.at[d_vmem.at[0]], o_vmem)

            pltpu.emit_pipeline(
                body,
                grid=(TOP_K, nwin),
                in_specs=[pl.BlockSpec((1, SC_WINDOW), lambda k, j: (k, win0 + j))],
                out_specs=[pl.BlockSpec((SC_WINDOW, SC_ROW), lambda k, j: (k * nwin + j, 0))],
                core_axis_name=("core", "subcore"),
                dimension_semantics=(pltpu.PARALLEL, pltpu.PARALLEL),
            )(d_hbm, o_hbm.at[part])

    return run(yp, dest).reshape(parts, TOP_K, n, SC_ROW)


def _expert_kernel(be_ref, bv_ref, bs_ref, nx_ref, sl_ref, x_ref, wg_hbm, wu_hbm, wd_hbm, o_ref,
                   wg_f, wu_f, wd_f, wg_s, wu_s, wd_s, sems, *, layer):
    i = pl.program_id(0)
    valid = bv_ref[i]
    expert = be_ref[i]
    new_expert = (i == 0) | (expert != be_ref[jnp.maximum(i - 1, 0)])
    slot = sl_ref[i]

    def fetch(which, into):
        return [pltpu.make_async_copy(src.at[layer, which], dst.at[into], sems.at[into, j])
                for j, (src, dst) in enumerate(((wg_hbm, wg_f), (wu_hbm, wu_f), (wd_hbm, wd_f)))]

    @pl.when(i == 0)
    def _():
        for cp in fetch(expert, slot):
            cp.start()

    @pl.when(new_expert)
    def _():
        for cp in fetch(expert, slot):
            cp.wait()
        upcoming = nx_ref[i]

        @pl.when(upcoming >= 0)
        def _():
            for cp in fetch(upcoming, 1 - slot):
                cp.start()

        wg_s[...] = wg_f[slot].astype(BF16)
        wu_s[...] = wu_f[slot].astype(BF16)
        wd_s[...] = wd_f[slot].astype(BF16)

    @pl.when(valid > 0)
    def _():
        w = _load_parts(x_ref)
        row = lax.broadcasted_iota(jnp.int32, w.shape, 0)
        w = jnp.where(row < valid, w, jnp.uint32(0))
        lo, hi = _unpack_rows(w)
        x = jnp.concatenate([lo, hi], axis=1).astype(BF16)
        a = _dot(x, wg_s[...])
        hid = a * _sigmoid(a) * _dot(x, wu_s[...])
        _store_parts(o_ref, _pack_rows(_dot(hid.astype(BF16), wd_s[...])))


def _experts(xs, blk_e, blk_valid, blk_src, blk_next, blk_slot, layer, w_gate, w_up, w_down):
    parts, slots, _ = xs.shape
    d = D_MODEL
    nb = slots // MOE_BLOCK
    data = pl.BlockSpec((parts, MOE_BLOCK, SC_ROW), lambda i, be, bv, bs, nx, sl: (0, bs[i], 0))
    stage = lambda shape: pltpu.VMEM((2,) + shape, F32)
    return pl.pallas_call(
        functools.partial(_expert_kernel, layer=layer),
        grid_spec=pltpu.PrefetchScalarGridSpec(
            num_scalar_prefetch=5,
            grid=(nb,),
            in_specs=[data] + [pl.BlockSpec(memory_space=pl.ANY)] * 3,
            out_specs=data,
            scratch_shapes=[stage((d, EXPERT_FF)), stage((d, EXPERT_FF)), stage((EXPERT_FF, d)),
                            pltpu.VMEM((d, EXPERT_FF), BF16), pltpu.VMEM((d, EXPERT_FF), BF16),
                            pltpu.VMEM((EXPERT_FF, d), BF16), pltpu.SemaphoreType.DMA((2, 3))]),
        out_shape=jax.ShapeDtypeStruct((parts, slots, SC_ROW), U32),
        compiler_params=_cparams(("arbitrary",)),
        name="experts",
    )(blk_e, blk_valid, blk_src, blk_next, blk_slot, xs, w_gate, w_up, w_down)


def _combine_kernel(yg_ref, gw_ref, h_ref, x_ref, gt_ref, wsg_ref, wsu_ref, wsd_ref, gf_ref, o_ref, *, final):
    h = h_ref[0]
    a = _dot(h, wsg_ref[...])
    hid = a * _sigmoid(a) * _dot(h, wsu_ref[...])
    y = _dot(hid.astype(BF16), wsd_ref[...])
    gw = gw_ref[...]
    y_lo = y[:, :D_MODEL // 2]
    y_hi = y[:, D_MODEL // 2:]
    for k in range(TOP_K):
        lo, hi = _unpack_rows(_load_parts(yg_ref, k))
        y_lo = y_lo + gw[:, k:k + 1] * lo
        y_hi = y_hi + gw[:, k:k + 1] * hi
    y = jnp.concatenate([y_lo, y_hi], axis=1)
    xn = x_ref[0] + gt_ref[0] * y
    if final:
        ms = jnp.mean(xn * xn, axis=-1, keepdims=True)
        xn = xn * lax.rsqrt(ms + RMS_EPS) * gf_ref[...]
    o_ref[0] = xn


def _combine(yg, gw, tok_off, h2, x, gt2, b0, nb, ws_gate, ws_up, ws_down, g_final, final, tm):
    b, s, d = x.shape
    per_batch = gt2.shape[0] == b
    mod_map = (lambda bi, i: (b0 + bi, 0, 0)) if per_batch else (lambda bi, i: (0, 0, 0))
    full = lambda arr: pl.BlockSpec(arr.shape, lambda bi, i: (0,) * arr.ndim)
    tok = lambda width: pl.BlockSpec((1, tm, width), lambda bi, i: (b0 + bi, i, 0))
    gf = g_final[None]
    nblk = s // tm
    blk0 = (tok_off + b0 * s) // tm
    return pl.pallas_call(
        functools.partial(_combine_kernel, final=final),
        grid=(nb, nblk),
        in_specs=[pl.BlockSpec((SC_PARTS, TOP_K, tm, SC_ROW), lambda bi, i: (0, 0, bi * nblk + i, 0)),
                  pl.BlockSpec((tm, TOP_K), lambda bi, i: (blk0 + bi * nblk + i, 0)),
                  tok(d), tok(d),
                  pl.BlockSpec((1, 1, d), mod_map),
                  full(ws_gate), full(ws_up), full(ws_down), full(gf)],
        out_specs=tok(d),
        out_shape=jax.ShapeDtypeStruct((b, s, d), F32),
        input_output_aliases={3: 0},
        compiler_params=_cparams(("parallel", "parallel")),
        name="combine",
    )(yg, gw, h2, x, gt2, ws_gate, ws_up, ws_down, gf)


def _project_latent(x, p, b0=0, nb=None, prev=None):
    w_main, w_gate = p['prep']['w_in']
    mods = p['prep']['mods']
    return _proj_in(x, p['g_norm1'], mods[0], mods[1], w_main, w_gate, tm=min(2048, x.shape[1]), tn=1024,
                    b0=b0, nb=nb, prev=prev)


def _layer(x, ctx_s, p, ctx_out, final, g_final, projected=None, after_piece=None, p_next=None):
    b, s, d = x.shape
    sc = ctx_s.shape[1]
    prep = p['prep']
    sh1, sc1, gt1, sh2, sc2, gt2 = prep['mods']
    csh1, csc1, cgt1, csh2, csc2, cgt2 = prep['mods_ctx']

    w_main, w_gate = prep['w_in']
    main, gate = _project_latent(x, p) if projected is None else projected
    ctx_flat = ctx_s.reshape(1, b * sc, d)
    n_ctx, tn_ctx = (N_MAIN, 1024) if ctx_out else (N_KV_MAIN, N_KV_MAIN // 2)
    main_c, gate_c = _proj_in(ctx_flat, p['g_norm1'], csh1, csc1, w_main, w_gate, tm=min(1024, b * sc),
                              tn=tn_ctx, n=n_ctx)
    main_c = main_c.reshape(b, sc, n_ctx)
    gate_c = gate_c.reshape(b, sc, LANE)

    o_na = _na_latent(main, main_c, prep['na_bias'])

    gate_ws = prep['gate_ws']
    s0 = jnp.zeros((b, GLA_HEADS, GLA_DK, GLA_DV), F32)
    o_cf, o_cb, st_f, st_b = _gla_scan(main_c, gate_c, gate_ws, (s0, s0), ctx_out)
    o_f, o_b, _, _ = _gla_scan(main, gate, gate_ws, (st_f, st_b), True)

    w_na, w_sc, w_gla, w_out, w_router = (prep[name] for name in ('w_na', 'w_sc', 'w_gla', 'w_out', 'w_router'))
    n_lat = b * s
    t = n_lat + (b * sc if ctx_out else 0)
    x, h2, hp_all, lg_all = _merge(o_na, p['conv_w'], o_f, o_b, main, x, gt1, p['gla_norm_g'], w_na, w_sc, w_gla,
                                   w_out, p['g_norm2'], sh2, sc2, w_router, tm=min(512, s), n_routed=t, tok_off=0)
    if ctx_out:
        o_na_c = _dense_attn(main_c)
        ctx_s, h2_c, hp_all, lg_all = _merge(o_na_c, p['conv_w'], o_cf, o_cb, main_c, ctx_s, cgt1, p['gla_norm_g'],
                                             w_na, w_sc, w_gla, w_out, p['g_norm2'], csh2, csc2, w_router,
                                             tm=min(256, sc), n_routed=t, tok_off=n_lat, routed=(hp_all, lg_all))

    eidx, gw, rank, counts = _route(lg_all, p['b_router'])
    padded = (counts + MOE_BLOCK - 1) // MOE_BLOCK * MOE_BLOCK
    pad_end = jnp.cumsum(padded)
    pad_start = pad_end - padded
    onehot = eidx[:, :, None] == jnp.arange(N_EXPERTS, dtype=jnp.int32)
    dest = jnp.sum(jnp.where(onehot, pad_start, 0), axis=-1) + rank
    n_blocks = -(-(t * TOP_K + N_EXPERTS * (MOE_BLOCK - 1)) // MOE_BLOCK)
    slots = n_blocks * MOE_BLOCK
    blk_start = jnp.arange(n_blocks, dtype=jnp.int32) * MOE_BLOCK
    blk_e = jnp.minimum(jnp.sum(pad_end[None, :] <= blk_start[:, None], axis=1), N_EXPERTS - 1).astype(jnp.int32)
    used_end = (pad_start + counts)[blk_e]
    blk_valid = jnp.clip(used_end - blk_start, 0, MOE_BLOCK).astype(jnp.int32)
    n_used = pad_end[-1] // MOE_BLOCK
    blk_src = jnp.minimum(jnp.arange(n_blocks, dtype=jnp.int32), n_used - 1)
    blk_e = blk_e[blk_src]
    ids = jnp.arange(N_EXPERTS, dtype=jnp.int32)
    used = counts > 0
    later_used = jnp.where(used[None, :] & (ids[None, :] > ids[:, None]), ids[None, :], N_EXPERTS).min(axis=1)
    next_used = jnp.where(later_used == N_EXPERTS, -1, later_used).astype(jnp.int32)
    blk_next = next_used[blk_e]
    blk_slot = ((jnp.cumsum(used) - 1) % 2).astype(jnp.int32)[blk_e]

    xs = _dispatch_rows(hp_all, dest, slots)
    if p_next is not None:
        xs, p_next['prep'] = lax.optimization_barrier((xs, p_next['prep']))
    ys = _experts(xs, blk_e, blk_valid, blk_src, blk_next, blk_slot, p['layer'], p['w_exp_gate'], p['w_exp_up'], p['w_exp_down'])
    gw_t = gw.T
    ws_gate, ws_up, ws_down = prep['ws_gate'], prep['ws_up'], prep['ws_down']

    def gathered(t0, n):
        return _gather_rows(ys, dest, t0, n)

    pieces = next(n for n in ((4, 2, 1) if after_piece is not None else (8, 4, 2, 1)) if b % n == 0)
    nb = b // pieces
    for q in range(pieces):
        x = _combine(gathered(q * nb * s, nb * s), gw_t, 0, h2, x, gt2, q * nb, nb, ws_gate, ws_up, ws_down,
                     g_final, final, tm=min(256, s))
        if after_piece is not None:
            after_piece(x, q * nb, nb)
    if ctx_out:
        ctx_s = _combine(gathered(n_lat, b * sc), gw_t, n_lat, h2_c, ctx_s, cgt2, 0, b, ws_gate, ws_up, ws_down,
                         g_final, False, tm=min(256, sc))
    return x, ctx_s


def kernel(x, c, ctx, c_ctx, w_mod, b_mod, g_norm1, g_norm2, w_in, na_rpb, w_branch_na, conv_w, w_branch_sc,
           gla_gate_w, gla_gate_b, gla_norm_g, w_branch_gla, w_out, w_router, b_router, w_exp_gate, w_exp_up,
           w_exp_down, w_sh_gate, w_sh_up, w_sh_down, g_final):
    stacked = dict(g_norm1=g_norm1, g_norm2=g_norm2, na_rpb=na_rpb, w_branch_na=w_branch_na,
                   conv_w=conv_w, w_branch_sc=w_branch_sc, gla_gate_w=gla_gate_w, gla_gate_b=gla_gate_b,
                   gla_norm_g=gla_norm_g, w_branch_gla=w_branch_gla, w_out=w_out, w_router=w_router,
                   b_router=b_router,
                   w_sh_gate=w_sh_gate, w_sh_up=w_sh_up, w_sh_down=w_sh_down)
    depth = w_in.shape[0]
    rows = x.shape[1] // GRID_W
    layers = []
    for i in range(depth):
        p = {name: arr[i] for name, arr in stacked.items()}
        p.update(layer=i, w_exp_gate=w_exp_gate, w_exp_up=w_exp_up, w_exp_down=w_exp_down)
        mods, mods_ctx = _mod_vectors(c, c_ctx, w_mod, b_mod, i)
        p['prep'] = dict(
            w_in=_prep_w_in(w_in, i), mods=mods, mods_ctx=mods_ctx,
            na_bias=_na_bias_table(p['na_rpb'], rows, min(NA_WIN_R, rows)),
            gate_ws=_gla_gate_weights(p['gla_gate_w'], p['gla_gate_b']),
            w_na=p['w_branch_na'].astype(BF16), w_sc=p['w_branch_sc'].astype(BF16),
            w_gla=p['w_branch_gla'].astype(BF16), w_out=p['w_out'].astype(BF16),
            w_router=jnp.pad(p['w_router'], ((0, 0), (0, LANE - N_EXPERTS))).astype(BF16),
            ws_gate=p['w_sh_gate'].astype(BF16), ws_up=p['w_sh_up'].astype(BF16),
            ws_down=p['w_sh_down'].astype(BF16))
        layers.append(p)

    ctx_s = ctx
    projected = None
    for i, p in enumerate(layers):
        last = i == depth - 1
        p_next = None if last else layers[i + 1]
        after_piece = None
        next_projected = []
        if not last:
            def after_piece(xq, b0, nb, p_next=p_next, acc=next_projected):
                acc.append(_project_latent(xq, p_next, b0, nb, acc[-1] if acc else None))

        x, ctx_s = _layer(x, ctx_s, p, not last, last, g_final, projected, after_piece, p_next)
        projected = next_projected[-1] if next_projected else None
    return x
```

```python
import functools

import numpy as np
import jax
import jax.numpy as jnp
from jax import lax
from jax.experimental import pallas as pl
from jax.experimental.pallas import tpu as pltpu
from jax.experimental.pallas import tpu_sc as plsc

F32 = jnp.float32
BF16 = jnp.bfloat16
U32 = jnp.uint32

D_MODEL = 1024
N_MOD = 6
RMS_EPS = 1e-6
NEG_INF = -1e30
GRID_W = 64
NA_HEADS = 8
NA_HEAD_DIM = 64
NA_WIDTH = NA_HEADS * NA_HEAD_DIM
NA_WIN_R = 8
NA_WIN_C = 16
NA_GROUP = 4
SC_WIDTH = 512
GLA_HEADS = 4
GLA_KEY_WIDTH = 512
GLA_VAL_WIDTH = 1024
GLA_DK = GLA_KEY_WIDTH // GLA_HEADS
GLA_DV = GLA_VAL_WIDTH // GLA_HEADS
GLA_GATE_RANK = 16
GLA_GATE_TAU = 16.0
LOG2_E = 1.4426950408889634
N_EXPERTS = 64
N_EXPERT_GROUPS = 8
GROUP_SIZE = N_EXPERTS // N_EXPERT_GROUPS
TOPK_GROUPS = 4
TOP_K = 8
EXPERT_FF = 256
ROUTED_SCALE = 2.5
MOE_BLOCK = 512

LANE = 128
GLA_C = 128
GLA_LEVELS = tuple(GLA_C >> (i + 1) for i in range(GLA_C.bit_length() - 1))
VMEM_LIMIT = 48 * 1024 * 1024
SC_WINDOW = 128
SC_ROW = 256
SC_PARTS = D_MODEL // 2 // SC_ROW

OFF_V_GLA = 0
OFF_K_NA = 1024
OFF_V_NA = 1536
OFF_K_GLA = 2048
N_KV_MAIN = 2560
OFF_Q_NA = 2560
OFF_B_SC = 3072
OFF_C_SC = 3584
OFF_X_SC = 4096
OFF_Q_GLA = 4608
OFF_R_GLA = 5120
OFF_MERGE = 6144
N_MAIN = 9216


def _cparams(sem, vmem=VMEM_LIMIT):
    return pltpu.CompilerParams(dimension_semantics=sem, vmem_limit_bytes=vmem)


def _dot(a, b):
    return jnp.dot(a, b, preferred_element_type=F32)


def _dot_nt(a, b):
    return lax.dot_general(a, b, (((1,), (1,)), ((), ())), preferred_element_type=F32)


def _dot_tn(a, b):
    return lax.dot_general(a, b, (((0,), (0,)), ((), ())), preferred_element_type=F32)


def _sigmoid(x):
    return 0.5 * jnp.tanh(0.5 * x) + 0.5


def _pack_rows(x):
    n = x.shape[1] // 2
    r = x.astype(BF16).astype(F32)
    lo = pltpu.bitcast(r[:, :n], U32) >> 16
    hi = pltpu.bitcast(r[:, n:], U32)
    return hi | lo


def _store_parts(ref, words):
    for part in range(SC_PARTS):
        dst = ref.at[part, 0] if len(ref.shape) == 4 else ref.at[part]
        dst[...] = words[:, part * SC_ROW:(part + 1) * SC_ROW]


def _load_parts(ref, *lead):
    return jnp.concatenate([ref[(part,) + lead] for part in range(SC_PARTS)], axis=-1)


def _unpack_rows(w):
    lo = pltpu.bitcast(w << 16, F32)
    hi = pltpu.bitcast(w & jnp.uint32(0xFFFF0000), F32)
    return lo, hi


def _mod_kernel(a_ref, w_ref, b_ref, o_ref):
    a = a_ref[...]
    a = a * _sigmoid(a)
    o_ref[...] = _dot(a.astype(BF16), w_ref[0].astype(BF16)) + b_ref[0]


def _mod_vectors(c, c_ctx, w_mod, b_mod, layer):
    b = c.shape[0]
    rows = -(-(b + 1) // 8) * 8
    a = jnp.concatenate([c, c_ctx[None], jnp.zeros((rows - b - 1, D_MODEL), F32)], axis=0)
    n = N_MOD * D_MODEL
    tn = 1536
    out = pl.pallas_call(
        _mod_kernel,
        grid=(n // tn,),
        in_specs=[pl.BlockSpec((rows, D_MODEL), lambda j: (0, 0)),
                  pl.BlockSpec((1, D_MODEL, tn), lambda j: (layer, 0, j)),
                  pl.BlockSpec((1, 1, tn), lambda j: (layer, 0, j))],
        out_specs=pl.BlockSpec((rows, tn), lambda j: (0, j)),
        out_shape=jax.ShapeDtypeStruct((rows, n), F32),
        compiler_params=_cparams(("parallel",)),
        name="mod_vectors",
    )(a, w_mod, b_mod[:, None])
    lat = out[:b].reshape(b, N_MOD, 1, D_MODEL)
    ctx = out[b].reshape(N_MOD, 1, 1, D_MODEL)
    return [lat[:, i] for i in range(N_MOD)], [ctx[i] for i in range(N_MOD)]


def _proj_kernel(x_ref, g_ref, sh_ref, sc_ref, w_ref, wg_ref, *rest):
    o_ref, og_ref, h_ref = rest[-3:]

    @pl.when(pl.program_id(2) == 0)
    def _():
        x = x_ref[0]
        ms = jnp.mean(x * x, axis=-1, keepdims=True)
        h = x * lax.rsqrt(ms + RMS_EPS) * g_ref[...] * (1.0 + sc_ref[0]) + sh_ref[0]
        hb = h.astype(BF16)
        h_ref[...] = hb
        og_ref[0] = _dot(hb, wg_ref[...])

    o_ref[0] = _dot(h_ref[...], w_ref[...]).astype(o_ref.dtype)


W_IN_TILE = 512
W_IN_GATE_SHIFT = 2 * GLA_GATE_RANK


def _prep_w_in_kernel(a_ref, b_ref, o_ref, g_ref):
    t = pl.program_id(0)
    first_lat = N_KV_MAIN // W_IN_TILE
    a = a_ref[0]

    @pl.when(t < first_lat)
    def _():
        o_ref[...] = a.T.astype(BF16)

    @pl.when(t >= first_lat)
    def _():
        moved = jnp.concatenate([a[W_IN_GATE_SHIFT:], b_ref[0]], axis=0)
        scale = jnp.where(t == first_lat, NA_HEAD_DIM ** -0.5, 1.0)
        o_ref[...] = (moved * scale).T.astype(BF16)

    @pl.when(t == first_lat)
    def _():
        head = a[:LANE]
        row = lax.broadcasted_iota(jnp.int32, head.shape, 0)
        g_ref[...] = jnp.where(row < W_IN_GATE_SHIFT, head, 0.0).T.astype(BF16)


def _prep_w_in(w_in, layer):
    d = w_in.shape[1]
    w_t = jnp.swapaxes(w_in, 1, 2)
    first_lat = N_KV_MAIN // W_IN_TILE
    kv_perm = OFF_K_NA // W_IN_TILE

    def a_map(t):
        return (layer, jnp.where(t < first_lat, (t + first_lat - kv_perm) % first_lat, t), 0)

    def b_map(t):
        return (layer, jnp.where(t < first_lat, 0, (t + 1) * (W_IN_TILE // W_IN_GATE_SHIFT)), 0)

    return pl.pallas_call(
        _prep_w_in_kernel,
        grid=(N_MAIN // W_IN_TILE,),
        in_specs=[pl.BlockSpec((1, W_IN_TILE, d), a_map),
                  pl.BlockSpec((1, W_IN_GATE_SHIFT, d), b_map)],
        out_specs=[pl.BlockSpec((d, W_IN_TILE), lambda t: (0, t)),
                   pl.BlockSpec((d, LANE), lambda t: (0, 0))],
        out_shape=[jax.ShapeDtypeStruct((d, N_MAIN), BF16), jax.ShapeDtypeStruct((d, LANE), BF16)],
        compiler_params=_cparams(("arbitrary",)),
        name="prep_w_in",
    )(w_t, w_t)


def _proj_in(x, g, shift, scale, w_main, w_gate, tm, tn, n=None, b0=0, nb=None, prev=None):
    b, s, d = x.shape
    n = w_main.shape[1] if n is None else n
    nb = b if nb is None else nb
    per_batch = shift.shape[0] == b
    mod_map = (lambda bi, i, j: (b0 + bi, 0, 0)) if per_batch else (lambda bi, i, j: (0, 0, 0))
    operands = (x, g[None], shift, scale, w_main, w_gate)
    extend = () if prev is None else tuple(prev)
    return pl.pallas_call(
        _proj_kernel,
        grid=(nb, s // tm, n // tn),
        in_specs=[pl.BlockSpec((1, tm, d), lambda bi, i, j: (b0 + bi, i, 0)),
                  pl.BlockSpec((1, d), lambda bi, i, j: (0, 0)),
                  pl.BlockSpec((1, 1, d), mod_map),
                  pl.BlockSpec((1, 1, d), mod_map),
                  pl.BlockSpec((d, tn), lambda bi, i, j: (0, j)),
                  pl.BlockSpec((d, LANE), lambda bi, i, j: (0, 0))]
                 + [pl.BlockSpec(memory_space=pl.ANY)] * len(extend),
        out_specs=[pl.BlockSpec((1, tm, tn), lambda bi, i, j: (b0 + bi, i, j)),
                   pl.BlockSpec((1, tm, LANE), lambda bi, i, j: (b0 + bi, i, 0))],
        out_shape=[jax.ShapeDtypeStruct((b, s, n), BF16),
                   jax.ShapeDtypeStruct((b, s, LANE), F32)],
        scratch_shapes=[pltpu.VMEM((tm, d), BF16)],
        input_output_aliases={len(operands): 0, len(operands) + 1: 1} if extend else {},
        compiler_params=_cparams(("parallel", "parallel", "arbitrary")),
        name="proj_in",
    )(*operands, *extend)


def _softmax_av(q, keys, vals, biases):
    scores = []
    for kk, bb in zip(keys, biases):
        s = _dot_nt(q, kk)
        scores.append(s if bb is None else s + bb)
    m = scores[0].max(axis=-1, keepdims=True)
    for s in scores[1:]:
        m = jnp.maximum(m, s.max(axis=-1, keepdims=True))
    num = None
    den = None
    for s, vv in zip(scores, vals):
        e = jnp.exp(s - m)
        dsum = e.sum(axis=-1, keepdims=True)
        o = _dot(e.astype(BF16), vv)
        num = o if num is None else num + o
        den = dsum if den is None else den + dsum
    return num / den


def _na_kernel(q_ref, k_ref, v_ref, kc_ref, vc_ref, *rest, rows, kr):
    *bias_refs, o_ref = rest
    kc = kc_ref[0]
    vc = vc_ref[0]
    for j, bias_ref in enumerate(bias_refs):
        r = pl.program_id(1) * len(bias_refs) + j
        row_start = jnp.clip(r - kr // 2, 0, rows - kr)
        start = pl.multiple_of(row_start * GRID_W, GRID_W)
        n_win = kr * GRID_W
        q = q_ref[0, j * GRID_W:(j + 1) * GRID_W, :]
        kw = k_ref[0, pl.ds(start, n_win), :]
        vw = v_ref[0, pl.ds(start, n_win), :]
        o_ref[0, j * GRID_W:(j + 1) * GRID_W, :] = _na_row(q, kw, vw, kc, vc, bias_ref).astype(o_ref.dtype)


def _na_row(q, kw, vw, kc, vc, bias_ref):
    gw = NA_GROUP * NA_HEAD_DIM
    stacked = (NA_GROUP * GRID_W, gw)
    on_head = (lax.broadcasted_iota(jnp.int32, stacked, 0) // GRID_W
               == lax.broadcasted_iota(jnp.int32, stacked, 1) // NA_HEAD_DIM)
    outs = []
    for g in range(NA_HEADS // NA_GROUP):
        sl = slice(g * gw, (g + 1) * gw)
        q_all = jnp.where(on_head, jnp.concatenate([q[:, sl]] * NA_GROUP, axis=0), jnp.zeros((), q.dtype))
        bias = bias_ref[0, g * NA_GROUP * GRID_W:(g + 1) * NA_GROUP * GRID_W, :]
        o_all = _softmax_av(q_all, [kw[:, sl], kc[:, sl]], [vw[:, sl], vc[:, sl]], [bias, None])
        o_all = jnp.where(on_head, o_all, 0.0).reshape(NA_GROUP, GRID_W, gw)
        outs.append(o_all.sum(axis=0))
    return jnp.concatenate(outs, axis=-1)


def _na_bias_table(rpb, rows, kr):
    col = np.arange(GRID_W)
    col_start = np.clip(col - NA_WIN_C // 2, 0, GRID_W - NA_WIN_C)
    col_ok = (col[None, :] >= col_start[:, None]) & (col[None, :] < col_start[:, None] + NA_WIN_C)
    d_col = np.clip(col[None, :] - col[:, None], -(NA_WIN_C - 1), NA_WIN_C - 1) + NA_WIN_C - 1
    n_dr, n_dc = rpb.shape[1], rpb.shape[2]
    onehot = jnp.asarray((d_col.reshape(-1)[None, :] == np.arange(n_dc)[:, None]).astype(np.float32))
    by_col = jnp.dot(rpb.astype(F32).reshape(NA_HEADS * n_dr, n_dc), onehot, precision=lax.Precision.HIGHEST)
    by_col = by_col.reshape(NA_HEADS, n_dr, GRID_W, GRID_W)
    by_col = jnp.where(col_ok[None, None], by_col, NEG_INF)
    tables = []
    for o in range(kr):
        lo = NA_WIN_R - 1 - o
        tables.append(by_col[:, lo:lo + kr].transpose(0, 2, 1, 3).reshape(NA_HEADS, GRID_W, kr * GRID_W))
    return jnp.stack(tables).reshape(kr, NA_HEADS * GRID_W, kr * GRID_W)


def _na_latent(main, main_ctx, bias):
    b, s, _ = main.shape
    sc = main_ctx.shape[1]
    rows = s // GRID_W
    kr = min(NA_WIN_R, rows)
    w = NA_WIDTH

    per_step = next(n for n in (8, 4, 2, 1) if rows % n == 0)

    def bias_spec(j):
        def bias_map(bi, i):
            r = i * per_step + j
            return (r - jnp.clip(r - kr // 2, 0, rows - kr), 0, 0)
        return pl.BlockSpec((1, NA_HEADS * GRID_W, kr * GRID_W), bias_map)

    return pl.pallas_call(
        functools.partial(_na_kernel, rows=rows, kr=kr),
        grid=(b, rows // per_step),
        in_specs=[pl.BlockSpec((1, per_step * GRID_W, w), lambda bi, i: (bi, i, OFF_Q_NA // w)),
                  pl.BlockSpec((1, s, w), lambda bi, i: (bi, 0, OFF_K_NA // w)),
                  pl.BlockSpec((1, s, w), lambda bi, i: (bi, 0, OFF_V_NA // w)),
                  pl.BlockSpec((1, sc, w), lambda bi, i: (bi, 0, OFF_K_NA // w)),
                  pl.BlockSpec((1, sc, w), lambda bi, i: (bi, 0, OFF_V_NA // w))]
                 + [bias_spec(j) for j in range(per_step)],
        out_specs=pl.BlockSpec((1, per_step * GRID_W, w), lambda bi, i: (bi, i, 0)),
        out_shape=jax.ShapeDtypeStruct((b, s, w), BF16),
        compiler_params=_cparams(("parallel", "arbitrary")),
        name="na_latent",
    )(main, main, main, main_ctx, main_ctx, *([bias] * per_step))


def _dense_attn_kernel(q_ref, k_ref, v_ref, o_ref):
    q = q_ref[0]
    k = k_ref[0]
    v = v_ref[0]
    outs = []
    for h in range(NA_HEADS):
        sl = slice(h * NA_HEAD_DIM, (h + 1) * NA_HEAD_DIM)
        outs.append(_softmax_av(q[:, sl], [k[:, sl]], [v[:, sl]], [None]))
    o_ref[0] = jnp.concatenate(outs, axis=-1).astype(o_ref.dtype)


def _dense_attn(main_ctx):
    b, sc, _ = main_ctx.shape
    w = NA_WIDTH
    return pl.pallas_call(
        _dense_attn_kernel,
        grid=(b,),
        in_specs=[pl.BlockSpec((1, sc, w), lambda bi: (bi, 0, OFF_Q_NA // w)),
                  pl.BlockSpec((1, sc, w), lambda bi: (bi, 0, OFF_K_NA // w)),
                  pl.BlockSpec((1, sc, w), lambda bi: (bi, 0, OFF_V_NA // w))],
        out_specs=pl.BlockSpec((1, sc, w), lambda bi: (bi, 0, 0)),
        out_shape=jax.ShapeDtypeStruct((b, sc, w), BF16),
        compiler_params=_cparams(("parallel",)),
        name="ctx_attn",
    )(main_ctx, main_ctx, main_ctx)


HALO = 16


def _gla_matrices(reverse):
    c = GLA_C
    t = np.arange(c)[:, None]
    m = np.arange(c)[None, :]
    blocks = [m <= t, m > t]
    for b in GLA_LEVELS:
        first = (t // (2 * b)) * (2 * b) + b
        is_q = (t & b) != 0
        blocks.append(np.where(is_q, (m > first) & (m <= t), (m > t) & (m <= first)))
    mats = np.stack(blocks).astype(np.float32)
    if reverse:
        mats = mats[:, ::-1, ::-1]
    mats = mats.reshape(-1, c)
    return jnp.asarray(np.concatenate([mats, mats], axis=1), dtype=BF16)


def _split_bf16(x):
    hi = x.astype(BF16)
    return hi, (x - hi.astype(F32)).astype(BF16)


def _pair_block_diag(x):
    lane = lax.broadcasted_iota(jnp.int32, x.shape, 1)
    zero = jnp.zeros((), x.dtype)
    return jnp.concatenate([jnp.where(lane < GLA_DK, x, zero), jnp.where(lane >= GLA_DK, x, zero)], axis=0)


def _gla_kernel(*refs, emit):
    n_in = 9 if emit else 8
    n_out = 2 if emit else 1
    ins, outs, scratch = refs[:2 * n_in], refs[2 * n_in:2 * (n_in + n_out)], refs[2 * (n_in + n_out):]
    ins = [ins[d * n_in:(d + 1) * n_in] for d in range(2)]
    outs = [outs[d * n_out:(d + 1) * n_out] for d in range(2)]
    step = pl.program_id(1)

    @pl.when(step == 0)
    def _():
        for d in range(2):
            scratch[d][...] = ins[d][-1][0]

    pending = [_gla_direction(ins[d][:-1], outs[d][:-1], scratch[d], reverse, emit)
               for d, reverse in enumerate((False, True))]
    while pending:
        pending = [stages for stages in pending if next(stages, "done") != "done"]

    @pl.when(step == pl.num_programs(1) - 1)
    def _():
        for d in range(2):
            outs[d][-1][0] = scratch[d][...]


def _gla_direction(ins, outs, st_ref, reverse, emit):
    if emit:
        q_ref, k_ref, v_ref, gt_ref, w2a_ref, w2b_ref, b2_ref, a_ref = ins
        (o_ref,) = outs
    else:
        k_ref, v_ref, gt_ref, w2a_ref, w2b_ref, b2_ref, a_ref = ins
    c = GLA_C
    pw = 2 * GLA_DK

    lr_hi, lr_lo = _split_bf16(gt_ref[0])
    logit = (_dot(jnp.concatenate([lr_hi, lr_lo], axis=1), w2a_ref[...]) + _dot(lr_hi, w2b_ref[...])
             + b2_ref[...])
    g = (jnp.minimum(logit, 0.0) - jnp.log1p(jnp.exp(-jnp.abs(logit)))) * (LOG2_E / GLA_GATE_TAU)
    g_hi, g_lo = _split_bf16(g)
    args = _dot(a_ref[...], jnp.concatenate([g_hi, g_lo], axis=0))
    cum = args[0:c]
    rem = args[c:2 * c]
    last_row = cum[0:1] if reverse else cum[c - 1:c]
    yield

    k = k_ref[0].astype(F32)
    v = v_ref[0]
    atts = []
    if emit:
        q = q_ref[0].astype(F32) * (GLA_DK ** -0.5)
        row_t = lax.broadcasted_iota(jnp.int32, (c, pw), 0)
        si = lax.broadcasted_iota(jnp.int32, (2 * c, c), 0) & (c - 1)
        ti = lax.broadcasted_iota(jnp.int32, (2 * c, c), 1)
        if reverse:
            row_t, ti, si = c - 1 - row_t, c - 1 - ti, c - 1 - si
        for hp in range(GLA_HEADS // 2):
            cs = slice(hp * pw, (hp + 1) * pw)
            qp, kp = q[:, cs], k[:, cs]
            att = jnp.where(ti == si, _dot_nt(_pair_block_diag(kp.astype(BF16)), qp.astype(BF16)), 0.0)
            for l, b in enumerate(GLA_LEVELS):
                x = (jnp.exp2(args[(2 + l) * c:(3 + l) * c, cs])
                     * jnp.where((row_t & b) != 0, qp, kp)).astype(BF16)
                pair = (((ti ^ si) >> (b.bit_length() - 1)) == 1) & ((ti & b) != 0)
                att = jnp.where(pair, _dot_nt(_pair_block_diag(x), x), att)
                yield
            atts.append(att.astype(BF16))

    outs = []
    for h in range(GLA_HEADS):
        sl = slice(h * GLA_DK, (h + 1) * GLA_DK)
        kh = k[:, sl]
        vh = v[:, h * GLA_DV:(h + 1) * GLA_DV]
        state = st_ref[h]
        kd = (kh * jnp.exp2(rem[:, sl])).astype(BF16)
        decay = jnp.exp2(jnp.broadcast_to(last_row[:, sl], (GLA_DK, GLA_DK))).T
        decay = jnp.concatenate([decay] * (GLA_DV // GLA_DK), axis=1)
        if emit:
            qd = (q[:, sl] * jnp.exp2(cum[:, sl])).astype(BF16)
            att_t = atts[h // 2][(h % 2) * c:(h % 2 + 1) * c]
            both = _dot_tn(jnp.concatenate([att_t, kd], axis=1), vh)
            outs.append(_dot(qd, state.astype(BF16)) + both[:c])
            st_ref[h] = decay * state + both[c:]
        else:
            st_ref[h] = decay * state + _dot_tn(kd, vh)
        yield

    if emit:
        o_ref[0] = jnp.concatenate(outs, axis=-1).astype(o_ref.dtype)


def _gla_scan(main, gate, gate_ws, s0s, emit):
    b, l, _ = main.shape
    n = l // GLA_C
    const = lambda arr: pl.BlockSpec(arr.shape, lambda bi, s: (0,) * arr.ndim)
    state_spec = pl.BlockSpec((1, GLA_HEADS, GLA_DK, GLA_DV), lambda bi, s: (bi, 0, 0, 0))
    state_shape = jax.ShapeDtypeStruct((b, GLA_HEADS, GLA_DK, GLA_DV), F32)
    in_specs, args, out_specs, out_shape = [], [], [], []
    for reverse in (False, True):
        amat = _gla_matrices(reverse)
        w2a, w2b, b2 = gate_ws[reverse]

        def col(block, reverse=reverse):
            return lambda bi, s: (bi, n - 1 - s if reverse else s, block)

        if emit:
            in_specs.append(pl.BlockSpec((1, GLA_C, GLA_KEY_WIDTH), col(OFF_Q_GLA // GLA_KEY_WIDTH)))
            args.append(main)
            out_specs.append(pl.BlockSpec((1, GLA_C, GLA_VAL_WIDTH), col(0)))
            out_shape.append(jax.ShapeDtypeStruct((b, l, GLA_VAL_WIDTH), BF16))
        in_specs += [pl.BlockSpec((1, GLA_C, GLA_KEY_WIDTH), col(OFF_K_GLA // GLA_KEY_WIDTH)),
                     pl.BlockSpec((1, GLA_C, GLA_VAL_WIDTH), col(OFF_V_GLA // GLA_VAL_WIDTH)),
                     pl.BlockSpec((1, GLA_C, LANE), col(0)),
                     const(w2a), const(w2b), const(b2), const(amat), state_spec]
        args += [main, main, gate, w2a, w2b, b2, amat, s0s[reverse]]
        out_specs.append(state_spec)
        out_shape.append(state_shape)
    res = pl.pallas_call(
        functools.partial(_gla_kernel, emit=emit),
        grid=(b, n),
        in_specs=in_specs,
        out_specs=out_specs,
        out_shape=out_shape,
        scratch_shapes=[pltpu.VMEM((GLA_HEADS, GLA_DK, GLA_DV), F32)] * 2,
        compiler_params=_cparams(("parallel", "arbitrary")),
        name="gla_scan",
    )(*args)
    return (res[0], res[2], res[1], res[3]) if emit else (None, None, res[0], res[1])


def _gla_gate_weights(gate_w, gate_b):
    out = []
    for dr in range(2):
        w = jnp.zeros((LANE, GLA_KEY_WIDTH), F32)
        w = w.at[dr * GLA_GATE_RANK:(dr + 1) * GLA_GATE_RANK].set(gate_w[dr])
        w_hi = w.astype(BF16)
        w_lo = (w - w_hi.astype(F32)).astype(BF16)
        out.append((jnp.concatenate([w_hi, w_hi], axis=0), w_lo, gate_b[dr][None]))
    return out


def _conv_tile(b_ref, c_ref, x_ref, cp_ref, xp_ref, cn_ref, xn_ref, w_ref):
    i, n = pl.program_id(1), pl.num_programs(1)
    u = c_ref[0].astype(F32) * x_ref[0].astype(F32)
    tm = u.shape[0]
    before = jnp.where(i > 0, 1.0, 0.0) * (cp_ref[0, HALO - 1:HALO].astype(F32) * xp_ref[0, HALO - 1:HALO].astype(F32))
    after = jnp.where(i < n - 1, 1.0, 0.0) * (cn_ref[0, 0:1].astype(F32) * xn_ref[0, 0:1].astype(F32))
    t = lax.broadcasted_iota(jnp.int32, u.shape, 0)
    prev = jnp.where(t == 0, before, pltpu.roll(u, 1, axis=0))
    nxt = jnp.where(t == tm - 1, after, pltpu.roll(u, tm - 1, axis=0))
    w = w_ref[...]
    return b_ref[0].astype(F32) * (prev * w[0:1] + u * w[1:2] + nxt * w[2:3])


def _merge_kernel(ona_ref, bsc_ref, csc_ref, xsc_ref, cp_ref, xp_ref, cn_ref, xn_ref, cw_ref, of_ref, ob_ref,
                  r_ref, gna_ref, gsc_ref, ggl_ref, x_ref, gt_ref,
                  gn_ref, wna_ref, wsc_ref, wgl_ref, wo_ref, g2_ref, sh2_ref, sc2_ref, wr_ref, *rest):
    xo_ref, h2_ref, hp_ref, lg_ref = rest[-4:]
    o_sc = _conv_tile(bsc_ref, csc_ref, xsc_ref, cp_ref, xp_ref, cn_ref, xn_ref, cw_ref).astype(BF16)
    o = of_ref[0].astype(F32) + ob_ref[0].astype(F32)
    normed = []
    for h in range(GLA_HEADS):
        oh = o[:, h * GLA_DV:(h + 1) * GLA_DV]
        ms = jnp.mean(oh * oh, axis=-1, keepdims=True)
        normed.append(oh * lax.rsqrt(ms + RMS_EPS))
    r = r_ref[0].astype(F32)
    y_gla = jnp.concatenate(normed, axis=-1) * gn_ref[...] * (r * _sigmoid(r))
    y = (_sigmoid(gna_ref[0].astype(F32)) * _dot(ona_ref[0], wna_ref[...])
         + _sigmoid(gsc_ref[0].astype(F32)) * _dot(o_sc, wsc_ref[...])
         + _sigmoid(ggl_ref[0].astype(F32)) * _dot(y_gla.astype(BF16), wgl_ref[...]))
    xn = x_ref[0] + gt_ref[0] * _dot(y.astype(BF16), wo_ref[...])
    xo_ref[0] = xn
    ms = jnp.mean(xn * xn, axis=-1, keepdims=True)
    h2 = xn * lax.rsqrt(ms + RMS_EPS) * g2_ref[...] * (1.0 + sc2_ref[0]) + sh2_ref[0]
    h2b = h2.astype(BF16)
    h2_ref[0] = h2b
    _store_parts(hp_ref, _pack_rows(h2))
    lg_ref[...] = _dot(h2b, wr_ref[...])


def _merge(o_na, conv_w, o_f, o_b, main, x, gt1, gn, w_na, w_sc, w_gla, w_out, g2, sh2, sc2, w_router, tm,
           n_routed, tok_off, routed=None):
    b, s, d = x.shape
    per_batch = gt1.shape[0] == b
    mod_map = (lambda bi, i: (bi, 0, 0)) if per_batch else (lambda bi, i: (0, 0, 0))
    tok = lambda width, blk: pl.BlockSpec((1, tm, width), lambda bi, i: (bi, i, blk))
    full = lambda arr: pl.BlockSpec(arr.shape, lambda bi, i: (0,) * arr.ndim)
    mod = pl.BlockSpec((1, 1, d), mod_map)
    gn_t = jnp.tile(gn, GLA_HEADS)[None]
    g2_t = g2[None]
    per_tile = tm // HALO
    last_halo = s // HALO - 1
    halo_prev = lambda blk: pl.BlockSpec(
        (1, HALO, SC_WIDTH), lambda bi, i: (bi, jnp.maximum(i * per_tile - 1, 0), blk))
    halo_next = lambda blk: pl.BlockSpec(
        (1, HALO, SC_WIDTH), lambda bi, i: (bi, jnp.minimum((i + 1) * per_tile, last_halo), blk))
    col_b, col_c, col_x = OFF_B_SC // SC_WIDTH, OFF_C_SC // SC_WIDTH, OFF_X_SC // SC_WIDTH
    extend = () if routed is None else tuple(routed)
    operands = (o_na, main, main, main, main, main, main, main, conv_w, o_f, o_b, main, main, main, main, x, gt1,
                gn_t, w_na, w_sc, w_gla, w_out, g2_t, sh2, sc2, w_router)
    n_in = len(operands)
    nblk = s // tm
    blk0 = tok_off // tm
    return pl.pallas_call(
        _merge_kernel,
        grid=(b, s // tm),
        in_specs=[tok(NA_WIDTH, 0), tok(SC_WIDTH, col_b), tok(SC_WIDTH, col_c), tok(SC_WIDTH, col_x),
                  halo_prev(col_c), halo_prev(col_x), halo_next(col_c), halo_next(col_x), full(conv_w),
                  tok(GLA_VAL_WIDTH, 0), tok(GLA_VAL_WIDTH, 0),
                  tok(d, OFF_R_GLA // d), tok(d, OFF_MERGE // d), tok(d, OFF_MERGE // d + 1),
                  tok(d, OFF_MERGE // d + 2), tok(d, 0), mod,
                  full(gn_t), full(w_na), full(w_sc), full(w_gla), full(w_out), full(g2_t), mod, mod,
                  full(w_router)] + [pl.BlockSpec(memory_space=pl.ANY)] * len(extend),
        out_specs=[tok(d, 0), tok(d, 0),
                   pl.BlockSpec((SC_PARTS, tm, SC_ROW), lambda bi, i: (0, blk0 + bi * nblk + i, 0)),
                   pl.BlockSpec((tm, LANE), lambda bi, i: (blk0 + bi * nblk + i, 0))],
        out_shape=[jax.ShapeDtypeStruct((b, s, d), F32),
                   jax.ShapeDtypeStruct((b, s, d), BF16),
                   jax.ShapeDtypeStruct((SC_PARTS, n_routed, SC_ROW), U32),
                   jax.ShapeDtypeStruct((n_routed, LANE), F32)],
        input_output_aliases={n_in: 2, n_in + 1: 3} if extend else {},
        compiler_params=_cparams(("parallel", "parallel")),
        name="merge",
    )(*operands, *extend)


def _router_kernel(lg_ref, br_ref, tri_ref, eidx_ref, gw_ref, rank_ref, cnt_ref, carry_ref):
    tm = lg_ref.shape[0]

    @pl.when(pl.program_id(0) == 0)
    def _():
        carry_ref[...] = jnp.zeros_like(carry_ref)

    scores = _sigmoid(lg_ref[...].T[:N_EXPERTS])
    sel = scores + br_ref[...]
    neg = -jnp.inf

    sel3 = sel.reshape(N_EXPERT_GROUPS, GROUP_SIZE, tm)
    i3 = lax.broadcasted_iota(jnp.int32, sel3.shape, 1)
    m1 = sel3.max(axis=1, keepdims=True)
    first = jnp.where(sel3 == m1, i3, GROUP_SIZE).min(axis=1, keepdims=True)
    m2 = jnp.where(i3 == first, neg, sel3).max(axis=1, keepdims=True)
    gscore = (m1 + m2)[:, 0, :]

    gi = lax.broadcasted_iota(jnp.int32, gscore.shape, 0)
    gmask = jnp.zeros(gscore.shape, jnp.bool_)
    for _ in range(TOPK_GROUPS):
        m = gscore.max(axis=0, keepdims=True)
        pick = gi == jnp.where(gscore == m, gi, N_EXPERT_GROUPS).min(axis=0, keepdims=True)
        gmask = gmask | pick
        gscore = jnp.where(pick, neg, gscore)
    emask = jnp.broadcast_to(gmask[:, None, :], sel3.shape).reshape(N_EXPERTS, tm)
    sel = jnp.where(emask, sel, neg)

    ei = lax.broadcasted_iota(jnp.int32, sel.shape, 0)
    picks, idxs, ws = [], [], []
    for _ in range(TOP_K):
        m = sel.max(axis=0, keepdims=True)
        idx = jnp.where(sel == m, ei, N_EXPERTS).min(axis=0, keepdims=True)
        pick = ei == idx
        picks.append(pick)
        idxs.append(idx)
        ws.append(jnp.where(pick, scores, 0.0).sum(axis=0, keepdims=True))
        sel = jnp.where(pick, neg, sel)
    w = jnp.concatenate(ws, axis=0)
    gw_ref[...] = w / w.sum(axis=0, keepdims=True) * ROUTED_SCALE
    eidx_ref[...] = jnp.concatenate(idxs, axis=0)

    onehot = picks[0]
    for p in picks[1:]:
        onehot = onehot | p
    onehot = jnp.where(onehot, 1.0, 0.0).astype(BF16)
    before = _dot(onehot, tri_ref[...]) + jnp.tile(carry_ref[...], (1, tm // LANE))
    rank_ref[...] = jnp.concatenate(
        [jnp.where(p, before, 0.0).sum(axis=0, keepdims=True) for p in picks], axis=0).astype(jnp.int32)
    carry_ref[...] += _dot(onehot, jnp.ones((tm, LANE), BF16))
    cnt_ref[...] = carry_ref[...]


def _route(logits, b_router, tm=512):
    t = logits.shape[0]
    br = jnp.broadcast_to(b_router.astype(F32)[:, None], (N_EXPERTS, tm))
    tri = jnp.asarray(np.triu(np.ones((tm, tm), np.float32), 1), dtype=BF16)
    kt = lambda dt: jax.ShapeDtypeStruct((TOP_K, t), dt)
    eidx, gw, rank, cnt = pl.pallas_call(
        _router_kernel,
        grid=(t // tm,),
        in_specs=[pl.BlockSpec((tm, LANE), lambda i: (i, 0)),
                  pl.BlockSpec((N_EXPERTS, tm), lambda i: (0, 0)),
                  pl.BlockSpec((tm, tm), lambda i: (0, 0))],
        out_specs=[pl.BlockSpec((TOP_K, tm), lambda i: (0, i)),
                   pl.BlockSpec((TOP_K, tm), lambda i: (0, i)),
                   pl.BlockSpec((TOP_K, tm), lambda i: (0, i)),
                   pl.BlockSpec((N_EXPERTS, LANE), lambda i: (0, 0))],
        out_shape=[kt(jnp.int32), kt(F32), kt(jnp.int32),
                   jax.ShapeDtypeStruct((N_EXPERTS, LANE), F32)],
        scratch_shapes=[pltpu.VMEM((N_EXPERTS, LANE), F32)],
        compiler_params=_cparams(("arbitrary",)),
        name="router",
    )(logits, br, tri)
    return eidx, gw, rank, cnt[:, 0].astype(jnp.int32)


def _sc_mesh():
    return plsc.VectorSubcoreMesh(core_axis_name="core", subcore_axis_name="subcore")


def _dispatch_rows(xp, dest, slots):
    parts, t, _ = xp.shape

    @pl.kernel(out_type=jax.ShapeDtypeStruct((parts, slots, SC_ROW), xp.dtype), mesh=_sc_mesh(),
               scratch_types=[], name="moe_dispatch")
    def run(x_hbm, d_hbm, o_hbm):
        for part in range(parts):
            out_part = o_hbm.at[part]

            def body(x_vmem, d_vmem, out_part=out_part):
                for k in range(TOP_K):
                    pltpu.sync_copy(x_vmem, out_part.at[d_vmem.at[k]])

            pltpu.emit_pipeline(
                body,
                grid=(t // SC_WINDOW,),
                in_specs=[pl.BlockSpec((SC_WINDOW, SC_ROW), lambda i: (i, 0)),
                          pl.BlockSpec((TOP_K, SC_WINDOW), lambda i: (0, i))],
                out_specs=[],
                core_axis_name=("core", "subcore"),
                dimension_semantics=(pltpu.PARALLEL,),
            )(x_hbm.at[part], d_hbm)

    return run(xp, dest)


def _gather_rows(yp, dest, t0, n):
    parts = yp.shape[0]
    nwin = n // SC_WINDOW
    win0 = t0 // SC_WINDOW

    @pl.kernel(out_type=jax.ShapeDtypeStruct((parts, TOP_K * n, SC_ROW), yp.dtype), mesh=_sc_mesh(),
               scratch_types=[], name="moe_gather")
    def run(y_hbm, d_hbm, o_hbm):
        for part in range(parts):
            table = y_hbm.at[part]

            def body(d_vmem, o_vmem, table=table):
                pltpu.sync_copy(table.at[d_vmem.at[0]], o_vmem)

            pltpu.emit_pipeline(
                body,
                grid=(TOP_K, nwin),
                in_specs=[pl.BlockSpec((1, SC_WINDOW), lambda k, j: (k, win0 + j))],
                out_specs=[pl.BlockSpec((SC_WINDOW, SC_ROW), lambda k, j: (k * nwin + j, 0))],
                core_axis_name=("core", "subcore"),
                dimension_semantics=(pltpu.PARALLEL, pltpu.PARALLEL),
            )(d_hbm, o_hbm.at[part])

    return run(yp, dest).reshape(parts, TOP_K, n, SC_ROW)


def _expert_kernel(be_ref, bv_ref, bs_ref, nx_ref, sl_ref, x_ref, wg_hbm, wu_hbm, wd_hbm, o_ref,
                   wg_f, wu_f, wd_f, wg_s, wu_s, wd_s, sems, *, layer):
    i = pl.program_id(0)
    valid = bv_ref[i]
    expert = be_ref[i]
    new_expert = (i == 0) | (expert != be_ref[jnp.maximum(i - 1, 0)])
    slot = sl_ref[i]

    def fetch(which, into):
        return [pltpu.make_async_copy(src.at[layer, which], dst.at[into], sems.at[into, j])
                for j, (src, dst) in enumerate(((wg_hbm, wg_f), (wu_hbm, wu_f), (wd_hbm, wd_f)))]

    @pl.when(i == 0)
    def _():
        for cp in fetch(expert, slot):
            cp.start()

    @pl.when(new_expert)
    def _():
        for cp in fetch(expert, slot):
            cp.wait()
        upcoming = nx_ref[i]

        @pl.when(upcoming >= 0)
        def _():
            for cp in fetch(upcoming, 1 - slot):
                cp.start()

        wg_s[...] = wg_f[slot].astype(BF16)
        wu_s[...] = wu_f[slot].astype(BF16)
        wd_s[...] = wd_f[slot].astype(BF16)

    @pl.when(valid > 0)
    def _():
        w = _load_parts(x_ref)
        row = lax.broadcasted_iota(jnp.int32, w.shape, 0)
        w = jnp.where(row < valid, w, jnp.uint32(0))
        lo, hi = _unpack_rows(w)
        x = jnp.concatenate([lo, hi], axis=1).astype(BF16)
        a = _dot(x, wg_s[...])
        hid = a * _sigmoid(a) * _dot(x, wu_s[...])
        _store_parts(o_ref, _pack_rows(_dot(hid.astype(BF16), wd_s[...])))


def _experts(xs, blk_e, blk_valid, blk_src, blk_next, blk_slot, layer, w_gate, w_up, w_down):
    parts, slots, _ = xs.shape
    d = D_MODEL
    nb = slots // MOE_BLOCK
    data = pl.BlockSpec((parts, MOE_BLOCK, SC_ROW), lambda i, be, bv, bs, nx, sl: (0, bs[i], 0))
    stage = lambda shape: pltpu.VMEM((2,) + shape, F32)
    return pl.pallas_call(
        functools.partial(_expert_kernel, layer=layer),
        grid_spec=pltpu.PrefetchScalarGridSpec(
            num_scalar_prefetch=5,
            grid=(nb,),
            in_specs=[data] + [pl.BlockSpec(memory_space=pl.ANY)] * 3,
            out_specs=data,
            scratch_shapes=[stage((d, EXPERT_FF)), stage((d, EXPERT_FF)), stage((EXPERT_FF, d)),
                            pltpu.VMEM((d, EXPERT_FF), BF16), pltpu.VMEM((d, EXPERT_FF), BF16),
                            pltpu.VMEM((EXPERT_FF, d), BF16), pltpu.SemaphoreType.DMA((2, 3))]),
        out_shape=jax.ShapeDtypeStruct((parts, slots, SC_ROW), U32),
        compiler_params=_cparams(("arbitrary",)),
        name="experts",
    )(blk_e, blk_valid, blk_src, blk_next, blk_slot, xs, w_gate, w_up, w_down)


def _combine_kernel(yg_ref, gw_ref, h_ref, x_ref, gt_ref, wsg_ref, wsu_ref, wsd_ref, gf_ref, o_ref, *, final):
    h = h_ref[0]
    a = _dot(h, wsg_ref[...])
    hid = a * _sigmoid(a) * _dot(h, wsu_ref[...])
    y = _dot(hid.astype(BF16), wsd_ref[...])
    gw = gw_ref[...]
    y_lo = y[:, :D_MODEL // 2]
    y_hi = y[:, D_MODEL // 2:]
    for k in range(TOP_K):
        lo, hi = _unpack_rows(_load_parts(yg_ref, k))
        y_lo = y_lo + gw[:, k:k + 1] * lo
        y_hi = y_hi + gw[:, k:k + 1] * hi
    y = jnp.concatenate([y_lo, y_hi], axis=1)
    xn = x_ref[0] + gt_ref[0] * y
    if final:
        ms = jnp.mean(xn * xn, axis=-1, keepdims=True)
        xn = xn * lax.rsqrt(ms + RMS_EPS) * gf_ref[...]
    o_ref[0] = xn


def _combine(yg, gw, tok_off, h2, x, gt2, b0, nb, ws_gate, ws_up, ws_down, g_final, final, tm):
    b, s, d = x.shape
    per_batch = gt2.shape[0] == b
    mod_map = (lambda bi, i: (b0 + bi, 0, 0)) if per_batch else (lambda bi, i: (0, 0, 0))
    full = lambda arr: pl.BlockSpec(arr.shape, lambda bi, i: (0,) * arr.ndim)
    tok = lambda width: pl.BlockSpec((1, tm, width), lambda bi, i: (b0 + bi, i, 0))
    gf = g_final[None]
    nblk = s // tm
    blk0 = (tok_off + b0 * s) // tm
    return pl.pallas_call(
        functools.partial(_combine_kernel, final=final),
        grid=(nb, nblk),
        in_specs=[pl.BlockSpec((SC_PARTS, TOP_K, tm, SC_ROW), lambda bi, i: (0, 0, bi * nblk + i, 0)),
                  pl.BlockSpec((tm, TOP_K), lambda bi, i: (blk0 + bi * nblk + i, 0)),
                  tok(d), tok(d),
                  pl.BlockSpec((1, 1, d), mod_map),
                  full(ws_gate), full(ws_up), full(ws_down), full(gf)],
        out_specs=tok(d),
        out_shape=jax.ShapeDtypeStruct((b, s, d), F32),
        input_output_aliases={3: 0},
        compiler_params=_cparams(("parallel", "parallel")),
        name="combine",
    )(yg, gw, h2, x, gt2, ws_gate, ws_up, ws_down, gf)


def _project_latent(x, p, b0=0, nb=None, prev=None):
    w_main, w_gate = p['prep']['w_in']
    mods = p['prep']['mods']
    return _proj_in(x, p['g_norm1'], mods[0], mods[1], w_main, w_gate, tm=min(2048, x.shape[1]), tn=1024,
                    b0=b0, nb=nb, prev=prev)


def _layer(x, ctx_s, p, ctx_out, final, g_final, projected=None, after_piece=None, p_next=None):
    b, s, d = x.shape
    sc = ctx_s.shape[1]
    prep = p['prep']
    sh1, sc1, gt1, sh2, sc2, gt2 = prep['mods']
    csh1, csc1, cgt1, csh2, csc2, cgt2 = prep['mods_ctx']

    w_main, w_gate = prep['w_in']
    main, gate = _project_latent(x, p) if projected is None else projected
    ctx_flat = ctx_s.reshape(1, b * sc, d)
    n_ctx, tn_ctx = (N_MAIN, 1024) if ctx_out else (N_KV_MAIN, N_KV_MAIN // 2)
    main_c, gate_c = _proj_in(ctx_flat, p['g_norm1'], csh1, csc1, w_main, w_gate, tm=min(1024, b * sc),
                              tn=tn_ctx, n=n_ctx)
    main_c = main_c.reshape(b, sc, n_ctx)
    gate_c = gate_c.reshape(b, sc, LANE)

    o_na = _na_latent(main, main_c, prep['na_bias'])

    gate_ws = prep['gate_ws']
    s0 = jnp.zeros((b, GLA_HEADS, GLA_DK, GLA_DV), F32)
    o_cf, o_cb, st_f, st_b = _gla_scan(main_c, gate_c, gate_ws, (s0, s0), ctx_out)
    o_f, o_b, _, _ = _gla_scan(main, gate, gate_ws, (st_f, st_b), True)

    w_na, w_sc, w_gla, w_out, w_router = (prep[name] for name in ('w_na', 'w_sc', 'w_gla', 'w_out', 'w_router'))
    n_lat = b * s
    t = n_lat + (b * sc if ctx_out else 0)
    x, h2, hp_all, lg_all = _merge(o_na, p['conv_w'], o_f, o_b, main, x, gt1, p['gla_norm_g'], w_na, w_sc, w_gla,
                                   w_out, p['g_norm2'], sh2, sc2, w_router, tm=min(512, s), n_routed=t, tok_off=0)
    if ctx_out:
        o_na_c = _dense_attn(main_c)
        ctx_s, h2_c, hp_all, lg_all = _merge(o_na_c, p['conv_w'], o_cf, o_cb, main_c, ctx_s, cgt1, p['gla_norm_g'],
                                             w_na, w_sc, w_gla, w_out, p['g_norm2'], csh2, csc2, w_router,
                                             tm=min(256, sc), n_routed=t, tok_off=n_lat, routed=(hp_all, lg_all))

    eidx, gw, rank, counts = _route(lg_all, p['b_router'])
    padded = (counts + MOE_BLOCK - 1) // MOE_BLOCK * MOE_BLOCK
    pad_end = jnp.cumsum(padded)
    pad_start = pad_end - padded
    onehot = eidx[:, :, None] == jnp.arange(N_EXPERTS, dtype=jnp.int32)
    dest = jnp.sum(jnp.where(onehot, pad_start, 0), axis=-1) + rank
    n_blocks = -(-(t * TOP_K + N_EXPERTS * (MOE_BLOCK - 1)) // MOE_BLOCK)
    slots = n_blocks * MOE_BLOCK
    blk_start = jnp.arange(n_blocks, dtype=jnp.int32) * MOE_BLOCK
    blk_e = jnp.minimum(jnp.sum(pad_end[None, :] <= blk_start[:, None], axis=1), N_EXPERTS - 1).astype(jnp.int32)
    used_end = (pad_start + counts)[blk_e]
    blk_valid = jnp.clip(used_end - blk_start, 0, MOE_BLOCK).astype(jnp.int32)
    n_used = pad_end[-1] // MOE_BLOCK
    blk_src = jnp.minimum(jnp.arange(n_blocks, dtype=jnp.int32), n_used - 1)
    blk_e = blk_e[blk_src]
    ids = jnp.arange(N_EXPERTS, dtype=jnp.int32)
    used = counts > 0
    later_used = jnp.where(used[None, :] & (ids[None, :] > ids[:, None]), ids[None, :], N_EXPERTS).min(axis=1)
    next_used = jnp.where(later_used == N_EXPERTS, -1, later_used).astype(jnp.int32)
    blk_next = next_used[blk_e]
    blk_slot = ((jnp.cumsum(used) - 1) % 2).astype(jnp.int32)[blk_e]

    xs = _dispatch_rows(hp_all, dest, slots)
    if p_next is not None:
        xs, p_next['prep'] = lax.optimization_barrier((xs, p_next['prep']))
    ys = _experts(xs, blk_e, blk_valid, blk_src, blk_next, blk_slot, p['layer'], p['w_exp_gate'], p['w_exp_up'], p['w_exp_down'])
    gw_t = gw.T
    ws_gate, ws_up, ws_down = prep['ws_gate'], prep['ws_up'], prep['ws_down']

    def gathered(t0, n):
        return _gather_rows(ys, dest, t0, n)

    pieces = next(n for n in (4, 2, 1) if b % n == 0)
    nb = b // pieces
    for q in range(pieces):
        x = _combine(gathered(q * nb * s, nb * s), gw_t, 0, h2, x, gt2, q * nb, nb, ws_gate, ws_up, ws_down,
                     g_final, final, tm=min(256, s))
        if after_piece is not None:
            after_piece(x, q * nb, nb)
    if ctx_out:
        ctx_s = _combine(gathered(n_lat, b * sc), gw_t, n_lat, h2_c, ctx_s, cgt2, 0, b, ws_gate, ws_up, ws_down,
                         g_final, False, tm=min(256, sc))
    return x, ctx_s


def kernel(x, c, ctx, c_ctx, w_mod, b_mod, g_norm1, g_norm2, w_in, na_rpb, w_branch_na, conv_w, w_branch_sc,
           gla_gate_w, gla_gate_b, gla_norm_g, w_branch_gla, w_out, w_router, b_router, w_exp_gate, w_exp_up,
           w_exp_down, w_sh_gate, w_sh_up, w_sh_down, g_final):
    stacked = dict(g_norm1=g_norm1, g_norm2=g_norm2, na_rpb=na_rpb, w_branch_na=w_branch_na,
                   conv_w=conv_w, w_branch_sc=w_branch_sc, gla_gate_w=gla_gate_w, gla_gate_b=gla_gate_b,
                   gla_norm_g=gla_norm_g, w_branch_gla=w_branch_gla, w_out=w_out, w_router=w_router,
                   b_router=b_router,
                   w_sh_gate=w_sh_gate, w_sh_up=w_sh_up, w_sh_down=w_sh_down)
    depth = w_in.shape[0]
    rows = x.shape[1] // GRID_W
    layers = []
    for i in range(depth):
        p = {name: arr[i] for name, arr in stacked.items()}
        p.update(layer=i, w_exp_gate=w_exp_gate, w_exp_up=w_exp_up, w_exp_down=w_exp_down)
        mods, mods_ctx = _mod_vectors(c, c_ctx, w_mod, b_mod, i)
        p['prep'] = dict(
            w_in=_prep_w_in(w_in, i), mods=mods, mods_ctx=mods_ctx,
            na_bias=_na_bias_table(p['na_rpb'], rows, min(NA_WIN_R, rows)),
            gate_ws=_gla_gate_weights(p['gla_gate_w'], p['gla_gate_b']),
            w_na=p['w_branch_na'].astype(BF16), w_sc=p['w_branch_sc'].astype(BF16),
            w_gla=p['w_branch_gla'].astype(BF16), w_out=p['w_out'].astype(BF16),
            w_router=jnp.pad(p['w_router'], ((0, 0), (0, LANE - N_EXPERTS))).astype(BF16),
            ws_gate=p['w_sh_gate'].astype(BF16), ws_up=p['w_sh_up'].astype(BF16),
            ws_down=p['w_sh_down'].astype(BF16))
        layers.append(p)

    ctx_s = ctx
    projected = None
    for i, p in enumerate(layers):
        last = i == depth - 1
        p_next = None if last else layers[i + 1]
        after_piece = None
        next_projected = []
        if not last:
            def after_piece(xq, b0, nb, p_next=p_next, acc=next_projected):
                acc.append(_project_latent(xq, p_next, b0, nb, acc[-1] if acc else None))

        x, ctx_s = _layer(x, ctx_s, p, not last, last, g_final, projected, after_piece, p_next)
        projected = next_projected[-1] if next_projected else None
    return x
```

```python
import functools

import numpy as np
import jax
import jax.numpy as jnp
from jax import lax
from jax.experimental import pallas as pl
from jax.experimental.pallas import tpu as pltpu
from jax.experimental.pallas import tpu_sc as plsc

F32 = jnp.float32
BF16 = jnp.bfloat16
U32 = jnp.uint32

D_MODEL = 1024
N_MOD = 6
RMS_EPS = 1e-6
NEG_INF = -1e30
GRID_W = 64
NA_HEADS = 8
NA_HEAD_DIM = 64
NA_WIDTH = NA_HEADS * NA_HEAD_DIM
NA_WIN_R = 8
NA_WIN_C = 16
NA_GROUP = 4
SC_WIDTH = 512
GLA_HEADS = 4
GLA_KEY_WIDTH = 512
GLA_VAL_WIDTH = 1024
GLA_DK = GLA_KEY_WIDTH // GLA_HEADS
GLA_DV = GLA_VAL_WIDTH // GLA_HEADS
GLA_GATE_RANK = 16
GLA_GATE_TAU = 16.0
LOG2_E = 1.4426950408889634
N_EXPERTS = 64
N_EXPERT_GROUPS = 8
GROUP_SIZE = N_EXPERTS // N_EXPERT_GROUPS
TOPK_GROUPS = 4
TOP_K = 8
EXPERT_FF = 256
ROUTED_SCALE = 2.5
MOE_BLOCK = 1024

LANE = 128
GLA_C = 128
GLA_STEP_CHUNKS = 2
GLA_LEVELS = tuple(GLA_C >> (i + 1) for i in range(GLA_C.bit_length() - 1))
VMEM_LIMIT = 48 * 1024 * 1024
SC_WINDOW = 128
SC_ROW = 256
SC_PARTS = D_MODEL // 2 // SC_ROW

OFF_V_GLA = 0
OFF_K_NA = 1024
OFF_V_NA = 1536
OFF_K_GLA = 2048
N_KV_MAIN = 2560
OFF_Q_NA = 2560
OFF_B_SC = 3072
OFF_C_SC = 3584
OFF_X_SC = 4096
OFF_Q_GLA = 4608
OFF_R_GLA = 5120
OFF_MERGE = 6144
N_MAIN = 9216


def _cparams(sem, vmem=VMEM_LIMIT):
    return pltpu.CompilerParams(dimension_semantics=sem, vmem_limit_bytes=vmem)


def _dot(a, b):
    return jnp.dot(a, b, preferred_element_type=F32)


def _dot_nt(a, b):
    return lax.dot_general(a, b, (((1,), (1,)), ((), ())), preferred_element_type=F32)


def _dot_tn(a, b):
    return lax.dot_general(a, b, (((0,), (0,)), ((), ())), preferred_element_type=F32)


def _sigmoid(x):
    return 0.5 * jnp.tanh(0.5 * x) + 0.5


def _pack_rows(x):
    n = x.shape[1] // 2
    r = x.astype(BF16).astype(F32)
    lo = pltpu.bitcast(r[:, :n], U32) >> 16
    hi = pltpu.bitcast(r[:, n:], U32)
    return hi | lo


def _store_parts(ref, words):
    for part in range(SC_PARTS):
        dst = ref.at[part, 0] if len(ref.shape) == 4 else ref.at[part]
        dst[...] = words[:, part * SC_ROW:(part + 1) * SC_ROW]


def _load_parts(ref, *lead):
    return jnp.concatenate([ref[(part,) + lead] for part in range(SC_PARTS)], axis=-1)


def _unpack_rows(w):
    lo = pltpu.bitcast(w << 16, F32)
    hi = pltpu.bitcast(w & jnp.uint32(0xFFFF0000), F32)
    return lo, hi


def _mod_kernel(a_ref, w_ref, b_ref, o_ref):
    a = a_ref[...]
    a = a * _sigmoid(a)
    o_ref[...] = _dot(a.astype(BF16), w_ref[0].astype(BF16)) + b_ref[0]


def _mod_vectors(c, c_ctx, w_mod, b_mod, layer):
    b = c.shape[0]
    rows = -(-(b + 1) // 8) * 8
    a = jnp.concatenate([c, c_ctx[None], jnp.zeros((rows - b - 1, D_MODEL), F32)], axis=0)
    n = N_MOD * D_MODEL
    tn = 1536
    out = pl.pallas_call(
        _mod_kernel,
        grid=(n // tn,),
        in_specs=[pl.BlockSpec((rows, D_MODEL), lambda j: (0, 0)),
                  pl.BlockSpec((1, D_MODEL, tn), lambda j: (layer, 0, j)),
                  pl.BlockSpec((1, 1, tn), lambda j: (layer, 0, j))],
        out_specs=pl.BlockSpec((rows, tn), lambda j: (0, j)),
        out_shape=jax.ShapeDtypeStruct((rows, n), F32),
        compiler_params=_cparams(("parallel",)),
        name="mod_vectors",
    )(a, w_mod, b_mod[:, None])
    lat = out[:b].reshape(b, N_MOD, 1, D_MODEL)
    ctx = out[b].reshape(N_MOD, 1, 1, D_MODEL)
    return [lat[:, i] for i in range(N_MOD)], [ctx[i] for i in range(N_MOD)]


def _proj_kernel(x_ref, g_ref, sh_ref, sc_ref, w_ref, wg_ref, *rest):
    o_ref, og_ref, h_ref = rest[-3:]

    @pl.when(pl.program_id(2) == 0)
    def _():
        x = x_ref[0]
        ms = jnp.mean(x * x, axis=-1, keepdims=True)
        h = x * lax.rsqrt(ms + RMS_EPS) * g_ref[...] * (1.0 + sc_ref[0]) + sh_ref[0]
        hb = h.astype(BF16)
        h_ref[...] = hb
        og_ref[0] = _dot(hb, wg_ref[...])

    o_ref[0] = _dot(h_ref[...], w_ref[...]).astype(o_ref.dtype)


W_IN_TILE = 512
W_IN_GATE_SHIFT = 2 * GLA_GATE_RANK


def _prep_w_in_kernel(a_ref, b_ref, o_ref, g_ref):
    t = pl.program_id(0)
    first_lat = N_KV_MAIN // W_IN_TILE
    a = a_ref[0]

    @pl.when(t < first_lat)
    def _():
        o_ref[...] = a.T.astype(BF16)

    @pl.when(t >= first_lat)
    def _():
        moved = jnp.concatenate([a[W_IN_GATE_SHIFT:], b_ref[0]], axis=0)
        scale = jnp.where(t == first_lat, NA_HEAD_DIM ** -0.5, 1.0)
        o_ref[...] = (moved * scale).T.astype(BF16)

    @pl.when(t == first_lat)
    def _():
        head = a[:LANE]
        row = lax.broadcasted_iota(jnp.int32, head.shape, 0)
        g_ref[...] = jnp.where(row < W_IN_GATE_SHIFT, head, 0.0).T.astype(BF16)


def _prep_w_in(w_in, layer):
    d = w_in.shape[1]
    w_t = jnp.swapaxes(w_in, 1, 2)
    first_lat = N_KV_MAIN // W_IN_TILE
    kv_perm = OFF_K_NA // W_IN_TILE

    def a_map(t):
        return (layer, jnp.where(t < first_lat, (t + first_lat - kv_perm) % first_lat, t), 0)

    def b_map(t):
        return (layer, jnp.where(t < first_lat, 0, (t + 1) * (W_IN_TILE // W_IN_GATE_SHIFT)), 0)

    return pl.pallas_call(
        _prep_w_in_kernel,
        grid=(N_MAIN // W_IN_TILE,),
        in_specs=[pl.BlockSpec((1, W_IN_TILE, d), a_map),
                  pl.BlockSpec((1, W_IN_GATE_SHIFT, d), b_map)],
        out_specs=[pl.BlockSpec((d, W_IN_TILE), lambda t: (0, t)),
                   pl.BlockSpec((d, LANE), lambda t: (0, 0))],
        out_shape=[jax.ShapeDtypeStruct((d, N_MAIN), BF16), jax.ShapeDtypeStruct((d, LANE), BF16)],
        compiler_params=_cparams(("arbitrary",)),
        name="prep_w_in",
    )(w_t, w_t)


def _proj_in(x, g, shift, scale, w_main, w_gate, tm, tn, n=None, b0=0, nb=None, prev=None):
    b, s, d = x.shape
    n = w_main.shape[1] if n is None else n
    nb = b if nb is None else nb
    per_batch = shift.shape[0] == b
    mod_map = (lambda bi, i, j: (b0 + bi, 0, 0)) if per_batch else (lambda bi, i, j: (0, 0, 0))
    operands = (x, g[None], shift, scale, w_main, w_gate)
    extend = () if prev is None else tuple(prev)
    return pl.pallas_call(
        _proj_kernel,
        grid=(nb, s // tm, n // tn),
        in_specs=[pl.BlockSpec((1, tm, d), lambda bi, i, j: (b0 + bi, i, 0)),
                  pl.BlockSpec((1, d), lambda bi, i, j: (0, 0)),
                  pl.BlockSpec((1, 1, d), mod_map),
                  pl.BlockSpec((1, 1, d), mod_map),
                  pl.BlockSpec((d, tn), lambda bi, i, j: (0, j)),
                  pl.BlockSpec((d, LANE), lambda bi, i, j: (0, 0))]
                 + [pl.BlockSpec(memory_space=pl.ANY)] * len(extend),
        out_specs=[pl.BlockSpec((1, tm, tn), lambda bi, i, j: (b0 + bi, i, j)),
                   pl.BlockSpec((1, tm, LANE), lambda bi, i, j: (b0 + bi, i, 0))],
        out_shape=[jax.ShapeDtypeStruct((b, s, n), BF16),
                   jax.ShapeDtypeStruct((b, s, LANE), F32)],
        scratch_shapes=[pltpu.VMEM((tm, d), BF16)],
        input_output_aliases={len(operands): 0, len(operands) + 1: 1} if extend else {},
        compiler_params=_cparams(("parallel", "parallel", "arbitrary")),
        name="proj_in",
    )(*operands, *extend)


def _softmax_av(q, keys, vals, biases):
    scores = []
    for kk, bb in zip(keys, biases):
        s = _dot_nt(q, kk)
        scores.append(s if bb is None else s + bb)
    m = scores[0].max(axis=-1, keepdims=True)
    for s in scores[1:]:
        m = jnp.maximum(m, s.max(axis=-1, keepdims=True))
    num = None
    den = None
    for s, vv in zip(scores, vals):
        e = jnp.exp(s - m)
        dsum = e.sum(axis=-1, keepdims=True)
        o = _dot(e.astype(BF16), vv)
        num = o if num is None else num + o
        den = dsum if den is None else den + dsum
    return num / den


def _na_kernel(q_ref, k_ref, v_ref, kc_ref, vc_ref, *rest, rows, kr):
    *bias_refs, o_ref = rest
    kc = kc_ref[0]
    vc = vc_ref[0]
    for j, bias_ref in enumerate(bias_refs):
        r = pl.program_id(1) * len(bias_refs) + j
        row_start = jnp.clip(r - kr // 2, 0, rows - kr)
        start = pl.multiple_of(row_start * GRID_W, GRID_W)
        n_win = kr * GRID_W
        q = q_ref[0, j * GRID_W:(j + 1) * GRID_W, :]
        kw = k_ref[0, pl.ds(start, n_win), :]
        vw = v_ref[0, pl.ds(start, n_win), :]
        o_ref[0, j * GRID_W:(j + 1) * GRID_W, :] = _na_row(q, kw, vw, kc, vc, bias_ref).astype(o_ref.dtype)


def _na_row(q, kw, vw, kc, vc, bias_ref):
    gw = NA_GROUP * NA_HEAD_DIM
    stacked = (NA_GROUP * GRID_W, gw)
    on_head = (lax.broadcasted_iota(jnp.int32, stacked, 0) // GRID_W
               == lax.broadcasted_iota(jnp.int32, stacked, 1) // NA_HEAD_DIM)
    outs = []
    for g in range(NA_HEADS // NA_GROUP):
        sl = slice(g * gw, (g + 1) * gw)
        q_all = jnp.where(on_head, jnp.concatenate([q[:, sl]] * NA_GROUP, axis=0), jnp.zeros((), q.dtype))
        bias = bias_ref[0, g * NA_GROUP * GRID_W:(g + 1) * NA_GROUP * GRID_W, :]
        o_all = _softmax_av(q_all, [kw[:, sl], kc[:, sl]], [vw[:, sl], vc[:, sl]], [bias, None])
        o_all = jnp.where(on_head, o_all, 0.0).reshape(NA_GROUP, GRID_W, gw)
        outs.append(o_all.sum(axis=0))
    return jnp.concatenate(outs, axis=-1)


def _na_bias_table(rpb, rows, kr):
    col = np.arange(GRID_W)
    col_start = np.clip(col - NA_WIN_C // 2, 0, GRID_W - NA_WIN_C)
    col_ok = (col[None, :] >= col_start[:, None]) & (col[None, :] < col_start[:, None] + NA_WIN_C)
    d_col = np.clip(col[None, :] - col[:, None], -(NA_WIN_C - 1), NA_WIN_C - 1) + NA_WIN_C - 1
    n_dr, n_dc = rpb.shape[1], rpb.shape[2]
    onehot = jnp.asarray((d_col.reshape(-1)[None, :] == np.arange(n_dc)[:, None]).astype(np.float32))
    by_col = jnp.dot(rpb.astype(F32).reshape(NA_HEADS * n_dr, n_dc), onehot, precision=lax.Precision.HIGHEST)
    by_col = by_col.reshape(NA_HEADS, n_dr, GRID_W, GRID_W)
    by_col = jnp.where(col_ok[None, None], by_col, NEG_INF)
    tables = []
    for o in range(kr):
        lo = NA_WIN_R - 1 - o
        tables.append(by_col[:, lo:lo + kr].transpose(0, 2, 1, 3).reshape(NA_HEADS, GRID_W, kr * GRID_W))
    return jnp.stack(tables).reshape(kr, NA_HEADS * GRID_W, kr * GRID_W)


def _na_latent(main, main_ctx, bias):
    b, s, _ = main.shape
    sc = main_ctx.shape[1]
    rows = s // GRID_W
    kr = min(NA_WIN_R, rows)
    w = NA_WIDTH

    per_step = next(n for n in (8, 4, 2, 1) if rows % n == 0)

    def bias_spec(j):
        def bias_map(bi, i):
            r = i * per_step + j
            return (r - jnp.clip(r - kr // 2, 0, rows - kr), 0, 0)
        return pl.BlockSpec((1, NA_HEADS * GRID_W, kr * GRID_W), bias_map)

    return pl.pallas_call(
        functools.partial(_na_kernel, rows=rows, kr=kr),
        grid=(b, rows // per_step),
        in_specs=[pl.BlockSpec((1, per_step * GRID_W, w), lambda bi, i: (bi, i, OFF_Q_NA // w)),
                  pl.BlockSpec((1, s, w), lambda bi, i: (bi, 0, OFF_K_NA // w)),
                  pl.BlockSpec((1, s, w), lambda bi, i: (bi, 0, OFF_V_NA // w)),
                  pl.BlockSpec((1, sc, w), lambda bi, i: (bi, 0, OFF_K_NA // w)),
                  pl.BlockSpec((1, sc, w), lambda bi, i: (bi, 0, OFF_V_NA // w))]
                 + [bias_spec(j) for j in range(per_step)],
        out_specs=pl.BlockSpec((1, per_step * GRID_W, w), lambda bi, i: (bi, i, 0)),
        out_shape=jax.ShapeDtypeStruct((b, s, w), BF16),
        compiler_params=_cparams(("parallel", "arbitrary")),
        name="na_latent",
    )(main, main, main, main_ctx, main_ctx, *([bias] * per_step))


def _dense_attn_kernel(q_ref, k_ref, v_ref, o_ref):
    q = q_ref[0]
    k = k_ref[0]
    v = v_ref[0]
    outs = []
    for h in range(NA_HEADS):
        sl = slice(h * NA_HEAD_DIM, (h + 1) * NA_HEAD_DIM)
        outs.append(_softmax_av(q[:, sl], [k[:, sl]], [v[:, sl]], [None]))
    o_ref[0] = jnp.concatenate(outs, axis=-1).astype(o_ref.dtype)


def _dense_attn(main_ctx):
    b, sc, _ = main_ctx.shape
    w = NA_WIDTH
    return pl.pallas_call(
        _dense_attn_kernel,
        grid=(b,),
        in_specs=[pl.BlockSpec((1, sc, w), lambda bi: (bi, 0, OFF_Q_NA // w)),
                  pl.BlockSpec((1, sc, w), lambda bi: (bi, 0, OFF_K_NA // w)),
                  pl.BlockSpec((1, sc, w), lambda bi: (bi, 0, OFF_V_NA // w))],
        out_specs=pl.BlockSpec((1, sc, w), lambda bi: (bi, 0, 0)),
        out_shape=jax.ShapeDtypeStruct((b, sc, w), BF16),
        compiler_params=_cparams(("parallel",)),
        name="ctx_attn",
    )(main_ctx, main_ctx, main_ctx)


HALO = 16


def _gla_matrices(reverse):
    c = GLA_C
    t = np.arange(c)[:, None]
    m = np.arange(c)[None, :]
    blocks = [m <= t, m > t]
    for b in GLA_LEVELS:
        first = (t // (2 * b)) * (2 * b) + b
        is_q = (t & b) != 0
        blocks.append(np.where(is_q, (m > first) & (m <= t), (m > t) & (m <= first)))
    mats = np.stack(blocks).astype(np.float32)
    if reverse:
        mats = mats[:, ::-1, ::-1]
    mats = mats.reshape(-1, c)
    return jnp.asarray(np.concatenate([mats, mats], axis=1), dtype=BF16)


def _split_bf16(x):
    hi = x.astype(BF16)
    return hi, (x - hi.astype(F32)).astype(BF16)


def _pair_block_diag(x):
    lane = lax.broadcasted_iota(jnp.int32, x.shape, 1)
    zero = jnp.zeros((), x.dtype)
    return jnp.concatenate([jnp.where(lane < GLA_DK, x, zero), jnp.where(lane >= GLA_DK, x, zero)], axis=0)


def _gla_kernel(*refs, emit):
    n_in = 9 if emit else 8
    n_out = 2 if emit else 1
    ins, outs, scratch = refs[:2 * n_in], refs[2 * n_in:2 * (n_in + n_out)], refs[2 * (n_in + n_out):]
    ins = [ins[d * n_in:(d + 1) * n_in] for d in range(2)]
    outs = [outs[d * n_out:(d + 1) * n_out] for d in range(2)]
    step = pl.program_id(1)

    @pl.when(step == 0)
    def _():
        for d in range(2):
            scratch[d][...] = ins[d][-1][0]

    pending = [_gla_direction(ins[d][:-1], outs[d][:-1], scratch[d], reverse, emit)
               for d, reverse in enumerate((False, True))]
    while pending:
        pending = [stages for stages in pending if next(stages, "done") != "done"]

    @pl.when(step == pl.num_programs(1) - 1)
    def _():
        for d in range(2):
            outs[d][-1][0] = scratch[d][...]


def _gla_direction(ins, outs, st_ref, reverse, emit):
    order = range(GLA_STEP_CHUNKS)
    for sub in (reversed(order) if reverse else order):
        yield from _gla_chunk(ins, outs, st_ref, reverse, emit, slice(sub * GLA_C, (sub + 1) * GLA_C))


def _gla_chunk(ins, outs, st_ref, reverse, emit, rows):
    if emit:
        q_ref, k_ref, v_ref, gt_ref, w2a_ref, w2b_ref, b2_ref, a_ref = ins
        (o_ref,) = outs
    else:
        k_ref, v_ref, gt_ref, w2a_ref, w2b_ref, b2_ref, a_ref = ins
    c = GLA_C
    pw = 2 * GLA_DK

    lr_hi, lr_lo = _split_bf16(gt_ref[0, rows])
    logit = (_dot(jnp.concatenate([lr_hi, lr_lo], axis=1), w2a_ref[...]) + _dot(lr_hi, w2b_ref[...])
             + b2_ref[...])
    g = (jnp.minimum(logit, 0.0) - jnp.log1p(jnp.exp(-jnp.abs(logit)))) * (LOG2_E / GLA_GATE_TAU)
    g_hi, g_lo = _split_bf16(g)
    args = _dot(a_ref[...], jnp.concatenate([g_hi, g_lo], axis=0))
    cum = args[0:c]
    rem = args[c:2 * c]
    last_row = cum[0:1] if reverse else cum[c - 1:c]
    yield

    k = k_ref[0, rows].astype(F32)
    v = v_ref[0, rows]
    atts = []
    if emit:
        q = q_ref[0, rows].astype(F32) * (GLA_DK ** -0.5)
        row_t = lax.broadcasted_iota(jnp.int32, (c, pw), 0)
        si = lax.broadcasted_iota(jnp.int32, (2 * c, c), 0) & (c - 1)
        ti = lax.broadcasted_iota(jnp.int32, (2 * c, c), 1)
        if reverse:
            row_t, ti, si = c - 1 - row_t, c - 1 - ti, c - 1 - si
        for hp in range(GLA_HEADS // 2):
            cs = slice(hp * pw, (hp + 1) * pw)
            qp, kp = q[:, cs], k[:, cs]
            att = jnp.where(ti == si, _dot_nt(_pair_block_diag(kp.astype(BF16)), qp.astype(BF16)), 0.0)
            for l, b in enumerate(GLA_LEVELS):
                x = (jnp.exp2(args[(2 + l) * c:(3 + l) * c, cs])
                     * jnp.where((row_t & b) != 0, qp, kp)).astype(BF16)
                pair = (((ti ^ si) >> (b.bit_length() - 1)) == 1) & ((ti & b) != 0)
                att = jnp.where(pair, _dot_nt(_pair_block_diag(x), x), att)
                yield
            atts.append(att.astype(BF16))

    outs = []
    for h in range(GLA_HEADS):
        sl = slice(h * GLA_DK, (h + 1) * GLA_DK)
        kh = k[:, sl]
        vh = v[:, h * GLA_DV:(h + 1) * GLA_DV]
        state = st_ref[h]
        kd = (kh * jnp.exp2(rem[:, sl])).astype(BF16)
        decay = jnp.exp2(jnp.broadcast_to(last_row[:, sl], (GLA_DK, GLA_DK))).T
        decay = jnp.concatenate([decay] * (GLA_DV // GLA_DK), axis=1)
        if emit:
            qd = (q[:, sl] * jnp.exp2(cum[:, sl])).astype(BF16)
            att_t = atts[h // 2][(h % 2) * c:(h % 2 + 1) * c]
            both = _dot_tn(jnp.concatenate([att_t, kd], axis=1), vh)
            outs.append(_dot(qd, state.astype(BF16)) + both[:c])
            st_ref[h] = decay * state + both[c:]
        else:
            st_ref[h] = decay * state + _dot_tn(kd, vh)
        yield

    if emit:
        o_ref[0, rows] = jnp.concatenate(outs, axis=-1).astype(o_ref.dtype)


def _gla_scan(main, gate, gate_ws, s0s, emit):
    b, l, _ = main.shape
    step_rows = GLA_C * GLA_STEP_CHUNKS
    n = l // step_rows
    const = lambda arr: pl.BlockSpec(arr.shape, lambda bi, s: (0,) * arr.ndim)
    state_spec = pl.BlockSpec((1, GLA_HEADS, GLA_DK, GLA_DV), lambda bi, s: (bi, 0, 0, 0))
    state_shape = jax.ShapeDtypeStruct((b, GLA_HEADS, GLA_DK, GLA_DV), F32)
    in_specs, args, out_specs, out_shape = [], [], [], []
    for reverse in (False, True):
        amat = _gla_matrices(reverse)
        w2a, w2b, b2 = gate_ws[reverse]

        def col(block, reverse=reverse):
            return lambda bi, s: (bi, n - 1 - s if reverse else s, block)

        if emit:
            in_specs.append(pl.BlockSpec((1, step_rows, GLA_KEY_WIDTH), col(OFF_Q_GLA // GLA_KEY_WIDTH)))
            args.append(main)
            out_specs.append(pl.BlockSpec((1, step_rows, GLA_VAL_WIDTH), col(0)))
            out_shape.append(jax.ShapeDtypeStruct((b, l, GLA_VAL_WIDTH), BF16))
        in_specs += [pl.BlockSpec((1, step_rows, GLA_KEY_WIDTH), col(OFF_K_GLA // GLA_KEY_WIDTH)),
                     pl.BlockSpec((1, step_rows, GLA_VAL_WIDTH), col(OFF_V_GLA // GLA_VAL_WIDTH)),
                     pl.BlockSpec((1, step_rows, LANE), col(0)),
                     const(w2a), const(w2b), const(b2), const(amat), state_spec]
        args += [main, main, gate, w2a, w2b, b2, amat, s0s[reverse]]
        out_specs.append(state_spec)
        out_shape.append(state_shape)
    res = pl.pallas_call(
        functools.partial(_gla_kernel, emit=emit),
        grid=(b, n),
        in_specs=in_specs,
        out_specs=out_specs,
        out_shape=out_shape,
        scratch_shapes=[pltpu.VMEM((GLA_HEADS, GLA_DK, GLA_DV), F32)] * 2,
        compiler_params=_cparams(("parallel", "arbitrary")),
        name="gla_scan",
    )(*args)
    return (res[0], res[2], res[1], res[3]) if emit else (None, None, res[0], res[1])


def _gla_gate_weights(gate_w, gate_b):
    out = []
    for dr in range(2):
        w = jnp.zeros((LANE, GLA_KEY_WIDTH), F32)
        w = w.at[dr * GLA_GATE_RANK:(dr + 1) * GLA_GATE_RANK].set(gate_w[dr])
        w_hi = w.astype(BF16)
        w_lo = (w - w_hi.astype(F32)).astype(BF16)
        out.append((jnp.concatenate([w_hi, w_hi], axis=0), w_lo, gate_b[dr][None]))
    return out


def _conv_tile(b_ref, c_ref, x_ref, cp_ref, xp_ref, cn_ref, xn_ref, w_ref):
    i, n = pl.program_id(1), pl.num_programs(1)
    u = c_ref[0].astype(F32) * x_ref[0].astype(F32)
    tm = u.shape[0]
    before = jnp.where(i > 0, 1.0, 0.0) * (cp_ref[0, HALO - 1:HALO].astype(F32) * xp_ref[0, HALO - 1:HALO].astype(F32))
    after = jnp.where(i < n - 1, 1.0, 0.0) * (cn_ref[0, 0:1].astype(F32) * xn_ref[0, 0:1].astype(F32))
    t = lax.broadcasted_iota(jnp.int32, u.shape, 0)
    prev = jnp.where(t == 0, before, pltpu.roll(u, 1, axis=0))
    nxt = jnp.where(t == tm - 1, after, pltpu.roll(u, tm - 1, axis=0))
    w = w_ref[...]
    return b_ref[0].astype(F32) * (prev * w[0:1] + u * w[1:2] + nxt * w[2:3])


def _merge_kernel(ona_ref, bsc_ref, csc_ref, xsc_ref, cp_ref, xp_ref, cn_ref, xn_ref, cw_ref, of_ref, ob_ref,
                  r_ref, gna_ref, gsc_ref, ggl_ref, x_ref, gt_ref,
                  gn_ref, wna_ref, wsc_ref, wgl_ref, wo_ref, g2_ref, sh2_ref, sc2_ref, wr_ref, *rest):
    xo_ref, h2_ref, hp_ref, lg_ref = rest[-4:]
    o_sc = _conv_tile(bsc_ref, csc_ref, xsc_ref, cp_ref, xp_ref, cn_ref, xn_ref, cw_ref).astype(BF16)
    o = of_ref[0].astype(F32) + ob_ref[0].astype(F32)
    normed = []
    for h in range(GLA_HEADS):
        oh = o[:, h * GLA_DV:(h + 1) * GLA_DV]
        ms = jnp.mean(oh * oh, axis=-1, keepdims=True)
        normed.append(oh * lax.rsqrt(ms + RMS_EPS))
    r = r_ref[0].astype(F32)
    y_gla = jnp.concatenate(normed, axis=-1) * gn_ref[...] * (r * _sigmoid(r))
    y = (_sigmoid(gna_ref[0].astype(F32)) * _dot(ona_ref[0], wna_ref[...])
         + _sigmoid(gsc_ref[0].astype(F32)) * _dot(o_sc, wsc_ref[...])
         + _sigmoid(ggl_ref[0].astype(F32)) * _dot(y_gla.astype(BF16), wgl_ref[...]))
    xn = x_ref[0] + gt_ref[0] * _dot(y.astype(BF16), wo_ref[...])
    xo_ref[0] = xn
    ms = jnp.mean(xn * xn, axis=-1, keepdims=True)
    h2 = xn * lax.rsqrt(ms + RMS_EPS) * g2_ref[...] * (1.0 + sc2_ref[0]) + sh2_ref[0]
    h2b = h2.astype(BF16)
    h2_ref[0] = h2b
    _store_parts(hp_ref, _pack_rows(h2))
    lg_ref[...] = _dot(h2b, wr_ref[...])


def _merge(o_na, conv_w, o_f, o_b, main, x, gt1, gn, w_na, w_sc, w_gla, w_out, g2, sh2, sc2, w_router, tm,
           n_routed, tok_off, routed=None):
    b, s, d = x.shape
    per_batch = gt1.shape[0] == b
    mod_map = (lambda bi, i: (bi, 0, 0)) if per_batch else (lambda bi, i: (0, 0, 0))
    tok = lambda width, blk: pl.BlockSpec((1, tm, width), lambda bi, i: (bi, i, blk))
    full = lambda arr: pl.BlockSpec(arr.shape, lambda bi, i: (0,) * arr.ndim)
    mod = pl.BlockSpec((1, 1, d), mod_map)
    gn_t = jnp.tile(gn, GLA_HEADS)[None]
    g2_t = g2[None]
    per_tile = tm // HALO
    last_halo = s // HALO - 1
    halo_prev = lambda blk: pl.BlockSpec(
        (1, HALO, SC_WIDTH), lambda bi, i: (bi, jnp.maximum(i * per_tile - 1, 0), blk))
    halo_next = lambda blk: pl.BlockSpec(
        (1, HALO, SC_WIDTH), lambda bi, i: (bi, jnp.minimum((i + 1) * per_tile, last_halo), blk))
    col_b, col_c, col_x = OFF_B_SC // SC_WIDTH, OFF_C_SC // SC_WIDTH, OFF_X_SC // SC_WIDTH
    extend = () if routed is None else tuple(routed)
    operands = (o_na, main, main, main, main, main, main, main, conv_w, o_f, o_b, main, main, main, main, x, gt1,
                gn_t, w_na, w_sc, w_gla, w_out, g2_t, sh2, sc2, w_router)
    n_in = len(operands)
    nblk = s // tm
    blk0 = tok_off // tm
    return pl.pallas_call(
        _merge_kernel,
        grid=(b, s // tm),
        in_specs=[tok(NA_WIDTH, 0), tok(SC_WIDTH, col_b), tok(SC_WIDTH, col_c), tok(SC_WIDTH, col_x),
                  halo_prev(col_c), halo_prev(col_x), halo_next(col_c), halo_next(col_x), full(conv_w),
                  tok(GLA_VAL_WIDTH, 0), tok(GLA_VAL_WIDTH, 0),
                  tok(d, OFF_R_GLA // d), tok(d, OFF_MERGE // d), tok(d, OFF_MERGE // d + 1),
                  tok(d, OFF_MERGE // d + 2), tok(d, 0), mod,
                  full(gn_t), full(w_na), full(w_sc), full(w_gla), full(w_out), full(g2_t), mod, mod,
                  full(w_router)] + [pl.BlockSpec(memory_space=pl.ANY)] * len(extend),
        out_specs=[tok(d, 0), tok(d, 0),
                   pl.BlockSpec((SC_PARTS, tm, SC_ROW), lambda bi, i: (0, blk0 + bi * nblk + i, 0)),
                   pl.BlockSpec((tm, LANE), lambda bi, i: (blk0 + bi * nblk + i, 0))],
        out_shape=[jax.ShapeDtypeStruct((b, s, d), F32),
                   jax.ShapeDtypeStruct((b, s, d), BF16),
                   jax.ShapeDtypeStruct((SC_PARTS, n_routed, SC_ROW), U32),
                   jax.ShapeDtypeStruct((n_routed, LANE), F32)],
        input_output_aliases={n_in: 2, n_in + 1: 3} if extend else {},
        compiler_params=_cparams(("parallel", "parallel")),
        name="merge",
    )(*operands, *extend)


def _router_kernel(lg_ref, br_ref, tri_ref, eidx_ref, gw_ref, rank_ref, cnt_ref, carry_ref):
    tm = lg_ref.shape[0]

    @pl.when(pl.program_id(0) == 0)
    def _():
        carry_ref[...] = jnp.zeros_like(carry_ref)

    scores = _sigmoid(lg_ref[...].T[:N_EXPERTS])
    sel = scores + br_ref[...]
    neg = -jnp.inf

    sel3 = sel.reshape(N_EXPERT_GROUPS, GROUP_SIZE, tm)
    i3 = lax.broadcasted_iota(jnp.int32, sel3.shape, 1)
    m1 = sel3.max(axis=1, keepdims=True)
    first = jnp.where(sel3 == m1, i3, GROUP_SIZE).min(axis=1, keepdims=True)
    m2 = jnp.where(i3 == first, neg, sel3).max(axis=1, keepdims=True)
    gscore = (m1 + m2)[:, 0, :]

    gi = lax.broadcasted_iota(jnp.int32, gscore.shape, 0)
    gmask = jnp.zeros(gscore.shape, jnp.bool_)
    for _ in range(TOPK_GROUPS):
        m = gscore.max(axis=0, keepdims=True)
        pick = gi == jnp.where(gscore == m, gi, N_EXPERT_GROUPS).min(axis=0, keepdims=True)
        gmask = gmask | pick
        gscore = jnp.where(pick, neg, gscore)
    emask = jnp.broadcast_to(gmask[:, None, :], sel3.shape).reshape(N_EXPERTS, tm)
    sel = jnp.where(emask, sel, neg)

    ei = lax.broadcasted_iota(jnp.int32, sel.shape, 0)
    picks, idxs, ws = [], [], []
    for _ in range(TOP_K):
        m = sel.max(axis=0, keepdims=True)
        idx = jnp.where(sel == m, ei, N_EXPERTS).min(axis=0, keepdims=True)
        pick = ei == idx
        picks.append(pick)
        idxs.append(idx)
        ws.append(jnp.where(pick, scores, 0.0).sum(axis=0, keepdims=True))
        sel = jnp.where(pick, neg, sel)
    w = jnp.concatenate(ws, axis=0)
    gw_ref[...] = w / w.sum(axis=0, keepdims=True) * ROUTED_SCALE
    eidx_ref[...] = jnp.concatenate(idxs, axis=0)

    onehot = picks[0]
    for p in picks[1:]:
        onehot = onehot | p
    onehot = jnp.where(onehot, 1.0, 0.0).astype(BF16)
    before = _dot(onehot, tri_ref[...]) + jnp.tile(carry_ref[...], (1, tm // LANE))
    rank_ref[...] = jnp.concatenate(
        [jnp.where(p, before, 0.0).sum(axis=0, keepdims=True) for p in picks], axis=0).astype(jnp.int32)
    carry_ref[...] += _dot(onehot, jnp.ones((tm, LANE), BF16))
    cnt_ref[...] = carry_ref[...]


def _route(logits, b_router, tm=512):
    t = logits.shape[0]
    br = jnp.broadcast_to(b_router.astype(F32)[:, None], (N_EXPERTS, tm))
    tri = jnp.asarray(np.triu(np.ones((tm, tm), np.float32), 1), dtype=BF16)
    kt = lambda dt: jax.ShapeDtypeStruct((TOP_K, t), dt)
    eidx, gw, rank, cnt = pl.pallas_call(
        _router_kernel,
        grid=(t // tm,),
        in_specs=[pl.BlockSpec((tm, LANE), lambda i: (i, 0)),
                  pl.BlockSpec((N_EXPERTS, tm), lambda i: (0, 0)),
                  pl.BlockSpec((tm, tm), lambda i: (0, 0))],
        out_specs=[pl.BlockSpec((TOP_K, tm), lambda i: (0, i)),
                   pl.BlockSpec((TOP_K, tm), lambda i: (0, i)),
                   pl.BlockSpec((TOP_K, tm), lambda i: (0, i)),
                   pl.BlockSpec((N_EXPERTS, LANE), lambda i: (0, 0))],
        out_shape=[kt(jnp.int32), kt(F32), kt(jnp.int32),
                   jax.ShapeDtypeStruct((N_EXPERTS, LANE), F32)],
        scratch_shapes=[pltpu.VMEM((N_EXPERTS, LANE), F32)],
        compiler_params=_cparams(("arbitrary",)),
        name="router",
    )(logits, br, tri)
    return eidx, gw, rank, cnt[:, 0].astype(jnp.int32)


def _sc_mesh():
    return plsc.VectorSubcoreMesh(core_axis_name="core", subcore_axis_name="subcore")


def _dispatch_rows(xp, dest, slots):
    parts, t, _ = xp.shape

    @pl.kernel(out_type=jax.ShapeDtypeStruct((parts, slots, SC_ROW), xp.dtype), mesh=_sc_mesh(),
               scratch_types=[], name="moe_dispatch")
    def run(x_hbm, d_hbm, o_hbm):
        for part in range(parts):
            out_part = o_hbm.at[part]

            def body(x_vmem, d_vmem, out_part=out_part):
                for k in range(TOP_K):
                    pltpu.sync_copy(x_vmem, out_part.at[d_vmem.at[k]])

            pltpu.emit_pipeline(
                body,
                grid=(t // SC_WINDOW,),
                in_specs=[pl.BlockSpec((SC_WINDOW, SC_ROW), lambda i: (i, 0)),
                          pl.BlockSpec((TOP_K, SC_WINDOW), lambda i: (0, i))],
                out_specs=[],
                core_axis_name=("core", "subcore"),
                dimension_semantics=(pltpu.PARALLEL,),
            )(x_hbm.at[part], d_hbm)

    return run(xp, dest)


def _gather_rows(yp, dest, t0, n):
    parts = yp.shape[0]
    nwin = n // SC_WINDOW
    win0 = t0 // SC_WINDOW

    @pl.kernel(out_type=jax.ShapeDtypeStruct((parts, TOP_K * n, SC_ROW), yp.dtype), mesh=_sc_mesh(),
               scratch_types=[], name="moe_gather")
    def run(y_hbm, d_hbm, o_hbm):
        for part in range(parts):
            table = y_hbm.at[part]

            def body(d_vmem, o_vmem, table=table):
                pltpu.sync_copy(table.at[d_vmem.at[0]], o_vmem)

            pltpu.emit_pipeline(
                body,
                grid=(TOP_K, nwin),
                in_specs=[pl.BlockSpec((1, SC_WINDOW), lambda k, j: (k, win0 + j))],
                out_specs=[pl.BlockSpec((SC_WINDOW, SC_ROW), lambda k, j: (k * nwin + j, 0))],
                core_axis_name=("core", "subcore"),
                dimension_semantics=(pltpu.PARALLEL, pltpu.PARALLEL),
            )(d_hbm, o_hbm.at[part])

    return run(yp, dest).reshape(parts, TOP_K, n, SC_ROW)


def _expert_kernel(be_ref, bv_ref, bs_ref, nx_ref, sl_ref, x_ref, wg_hbm, wu_hbm, wd_hbm, o_ref,
                   wg_f, wu_f, wd_f, wg_s, wu_s, wd_s, sems, *, layer):
    i = pl.program_id(0)
    valid = bv_ref[i]
    expert = be_ref[i]
    new_expert = (i == 0) | (expert != be_ref[jnp.maximum(i - 1, 0)])
    slot = sl_ref[i]

    def fetch(which, into):
        return [pltpu.make_async_copy(src.at[layer, which], dst.at[into], sems.at[into, j])
                for j, (src, dst) in enumerate(((wg_hbm, wg_f), (wu_hbm, wu_f), (wd_hbm, wd_f)))]

    @pl.when(i == 0)
    def _():
        for cp in fetch(expert, slot):
            cp.start()

    @pl.when(new_expert)
    def _():
        for cp in fetch(expert, slot):
            cp.wait()
        upcoming = nx_ref[i]

        @pl.when(upcoming >= 0)
        def _():
            for cp in fetch(upcoming, 1 - slot):
                cp.start()

        wg_s[...] = wg_f[slot].astype(BF16)
        wu_s[...] = wu_f[slot].astype(BF16)
        wd_s[...] = wd_f[slot].astype(BF16)

    @pl.when(valid > 0)
    def _():
        w = _load_parts(x_ref)
        row = lax.broadcasted_iota(jnp.int32, w.shape, 0)
        w = jnp.where(row < valid, w, jnp.uint32(0))
        lo, hi = _unpack_rows(w)
        x = jnp.concatenate([lo, hi], axis=1).astype(BF16)
        a = _dot(x, wg_s[...])
        hid = a * _sigmoid(a) * _dot(x, wu_s[...])
        _store_parts(o_ref, _pack_rows(_dot(hid.astype(BF16), wd_s[...])))


def _experts(xs, blk_e, blk_valid, blk_src, blk_next, blk_slot, layer, w_gate, w_up, w_down):
    parts, slots, _ = xs.shape
    d = D_MODEL
    nb = slots // MOE_BLOCK
    data = pl.BlockSpec((parts, MOE_BLOCK, SC_ROW), lambda i, be, bv, bs, nx, sl: (0, bs[i], 0))
    stage = lambda shape: pltpu.VMEM((2,) + shape, F32)
    return pl.pallas_call(
        functools.partial(_expert_kernel, layer=layer),
        grid_spec=pltpu.PrefetchScalarGridSpec(
            num_scalar_prefetch=5,
            grid=(nb,),
            in_specs=[data] + [pl.BlockSpec(memory_space=pl.ANY)] * 3,
            out_specs=data,
            scratch_shapes=[stage((d, EXPERT_FF)), stage((d, EXPERT_FF)), stage((EXPERT_FF, d)),
                            pltpu.VMEM((d, EXPERT_FF), BF16), pltpu.VMEM((d, EXPERT_FF), BF16),
                            pltpu.VMEM((EXPERT_FF, d), BF16), pltpu.SemaphoreType.DMA((2, 3))]),
        out_shape=jax.ShapeDtypeStruct((parts, slots, SC_ROW), U32),
        compiler_params=_cparams(("arbitrary",)),
        name="experts",
    )(blk_e, blk_valid, blk_src, blk_next, blk_slot, xs, w_gate, w_up, w_down)


def _combine_kernel(yg_ref, gw_ref, h_ref, x_ref, gt_ref, wsg_ref, wsu_ref, wsd_ref, gf_ref, o_ref, *, final):
    h = h_ref[0]
    a = _dot(h, wsg_ref[...])
    hid = a * _sigmoid(a) * _dot(h, wsu_ref[...])
    y = _dot(hid.astype(BF16), wsd_ref[...])
    gw = gw_ref[...]
    y_lo = y[:, :D_MODEL // 2]
    y_hi = y[:, D_MODEL // 2:]
    for k in range(TOP_K):
        lo, hi = _unpack_rows(_load_parts(yg_ref, k))
        y_lo = y_lo + gw[:, k:k + 1] * lo
        y_hi = y_hi + gw[:, k:k + 1] * hi
    y = jnp.concatenate([y_lo, y_hi], axis=1)
    xn = x_ref[0] + gt_ref[0] * y
    if final:
        ms = jnp.mean(xn * xn, axis=-1, keepdims=True)
        xn = xn * lax.rsqrt(ms + RMS_EPS) * gf_ref[...]
    o_ref[0] = xn


def _combine(yg, gw, tok_off, h2, x, gt2, b0, nb, ws_gate, ws_up, ws_down, g_final, final, tm):
    b, s, d = x.shape
    per_batch = gt2.shape[0] == b
    mod_map = (lambda bi, i: (b0 + bi, 0, 0)) if per_batch else (lambda bi, i: (0, 0, 0))
    full = lambda arr: pl.BlockSpec(arr.shape, lambda bi, i: (0,) * arr.ndim)
    tok = lambda width: pl.BlockSpec((1, tm, width), lambda bi, i: (b0 + bi, i, 0))
    gf = g_final[None]
    nblk = s // tm
    blk0 = (tok_off + b0 * s) // tm
    return pl.pallas_call(
        functools.partial(_combine_kernel, final=final),
        grid=(nb, nblk),
        in_specs=[pl.BlockSpec((SC_PARTS, TOP_K, tm, SC_ROW), lambda bi, i: (0, 0, bi * nblk + i, 0)),
                  pl.BlockSpec((tm, TOP_K), lambda bi, i: (blk0 + bi * nblk + i, 0)),
                  tok(d), tok(d),
                  pl.BlockSpec((1, 1, d), mod_map),
                  full(ws_gate), full(ws_up), full(ws_down), full(gf)],
        out_specs=tok(d),
        out_shape=jax.ShapeDtypeStruct((b, s, d), F32),
        input_output_aliases={3: 0},
        compiler_params=_cparams(("parallel", "parallel")),
        name="combine",
    )(yg, gw, h2, x, gt2, ws_gate, ws_up, ws_down, gf)


def _project_latent(x, p, b0=0, nb=None, prev=None):
    w_main, w_gate = p['prep']['w_in']
    mods = p['prep']['mods']
    return _proj_in(x, p['g_norm1'], mods[0], mods[1], w_main, w_gate, tm=min(2048, x.shape[1]), tn=1024,
                    b0=b0, nb=nb, prev=prev)


def _layer(x, ctx_s, p, ctx_out, final, g_final, projected=None, after_piece=None, p_next=None):
    b, s, d = x.shape
    sc = ctx_s.shape[1]
    prep = p['prep']
    sh1, sc1, gt1, sh2, sc2, gt2 = prep['mods']
    csh1, csc1, cgt1, csh2, csc2, cgt2 = prep['mods_ctx']

    w_main, w_gate = prep['w_in']
    main, gate = _project_latent(x, p) if projected is None else projected
    ctx_flat = ctx_s.reshape(1, b * sc, d)
    n_ctx, tn_ctx = (N_MAIN, 1024) if ctx_out else (N_KV_MAIN, N_KV_MAIN // 2)
    main_c, gate_c = _proj_in(ctx_flat, p['g_norm1'], csh1, csc1, w_main, w_gate, tm=min(1024, b * sc),
                              tn=tn_ctx, n=n_ctx)
    main_c = main_c.reshape(b, sc, n_ctx)
    gate_c = gate_c.reshape(b, sc, LANE)

    o_na = _na_latent(main, main_c, prep['na_bias'])

    gate_ws = prep['gate_ws']
    s0 = jnp.zeros((b, GLA_HEADS, GLA_DK, GLA_DV), F32)
    o_cf, o_cb, st_f, st_b = _gla_scan(main_c, gate_c, gate_ws, (s0, s0), ctx_out)
    o_f, o_b, _, _ = _gla_scan(main, gate, gate_ws, (st_f, st_b), True)

    w_na, w_sc, w_gla, w_out, w_router = (prep[name] for name in ('w_na', 'w_sc', 'w_gla', 'w_out', 'w_router'))
    n_lat = b * s
    t = n_lat + (b * sc if ctx_out else 0)
    x, h2, hp_all, lg_all = _merge(o_na, p['conv_w'], o_f, o_b, main, x, gt1, p['gla_norm_g'], w_na, w_sc, w_gla,
                                   w_out, p['g_norm2'], sh2, sc2, w_router, tm=min(512, s), n_routed=t, tok_off=0)
    if ctx_out:
        o_na_c = _dense_attn(main_c)
        ctx_s, h2_c, hp_all, lg_all = _merge(o_na_c, p['conv_w'], o_cf, o_cb, main_c, ctx_s, cgt1, p['gla_norm_g'],
                                             w_na, w_sc, w_gla, w_out, p['g_norm2'], csh2, csc2, w_router,
                                             tm=min(256, sc), n_routed=t, tok_off=n_lat, routed=(hp_all, lg_all))

    eidx, gw, rank, counts = _route(lg_all, p['b_router'])
    padded = (counts + MOE_BLOCK - 1) // MOE_BLOCK * MOE_BLOCK
    pad_end = jnp.cumsum(padded)
    pad_start = pad_end - padded
    onehot = eidx[:, :, None] == jnp.arange(N_EXPERTS, dtype=jnp.int32)
    dest = jnp.sum(jnp.where(onehot, pad_start, 0), axis=-1) + rank
    n_blocks = -(-(t * TOP_K + N_EXPERTS * (MOE_BLOCK - 1)) // MOE_BLOCK)
    slots = n_blocks * MOE_BLOCK
    blk_start = jnp.arange(n_blocks, dtype=jnp.int32) * MOE_BLOCK
    blk_e = jnp.minimum(jnp.sum(pad_end[None, :] <= blk_start[:, None], axis=1), N_EXPERTS - 1).astype(jnp.int32)
    used_end = (pad_start + counts)[blk_e]
    blk_valid = jnp.clip(used_end - blk_start, 0, MOE_BLOCK).astype(jnp.int32)
    n_used = pad_end[-1] // MOE_BLOCK
    blk_src = jnp.minimum(jnp.arange(n_blocks, dtype=jnp.int32), n_used - 1)
    blk_e = blk_e[blk_src]
    ids = jnp.arange(N_EXPERTS, dtype=jnp.int32)
    used = counts > 0
    later_used = jnp.where(used[None, :] & (ids[None, :] > ids[:, None]), ids[None, :], N_EXPERTS).min(axis=1)
    next_used = jnp.where(later_used == N_EXPERTS, -1, later_used).astype(jnp.int32)
    blk_next = next_used[blk_e]
    blk_slot = ((jnp.cumsum(used) - 1) % 2).astype(jnp.int32)[blk_e]

    xs = _dispatch_rows(hp_all, dest, slots)
    if p_next is not None:
        xs, p_next['prep'] = lax.optimization_barrier((xs, p_next['prep']))
    ys = _experts(xs, blk_e, blk_valid, blk_src, blk_next, blk_slot, p['layer'], p['w_exp_gate'], p['w_exp_up'], p['w_exp_down'])
    gw_t = gw.T
    ws_gate, ws_up, ws_down = prep['ws_gate'], prep['ws_up'], prep['ws_down']

    def gathered(t0, n):
        return _gather_rows(ys, dest, t0, n)

    pieces = next(n for n in (4, 2, 1) if b % n == 0)
    nb = b // pieces
    for q in range(pieces):
        x = _combine(gathered(q * nb * s, nb * s), gw_t, 0, h2, x, gt2, q * nb, nb, ws_gate, ws_up, ws_down,
                     g_final, final, tm=min(256, s))
        if after_piece is not None:
            after_piece(x, q * nb, nb)
    if ctx_out:
        ctx_s = _combine(gathered(n_lat, b * sc), gw_t, n_lat, h2_c, ctx_s, cgt2, 0, b, ws_gate, ws_up, ws_down,
                         g_final, False, tm=min(256, sc))
    return x, ctx_s


def kernel(x, c, ctx, c_ctx, w_mod, b_mod, g_norm1, g_norm2, w_in, na_rpb, w_branch_na, conv_w, w_branch_sc,
           gla_gate_w, gla_gate_b, gla_norm_g, w_branch_gla, w_out, w_router, b_router, w_exp_gate, w_exp_up,
           w_exp_down, w_sh_gate, w_sh_up, w_sh_down, g_final):
    stacked = dict(g_norm1=g_norm1, g_norm2=g_norm2, na_rpb=na_rpb, w_branch_na=w_branch_na,
                   conv_w=conv_w, w_branch_sc=w_branch_sc, gla_gate_w=gla_gate_w, gla_gate_b=gla_gate_b,
                   gla_norm_g=gla_norm_g, w_branch_gla=w_branch_gla, w_out=w_out, w_router=w_router,
                   b_router=b_router,
                   w_sh_gate=w_sh_gate, w_sh_up=w_sh_up, w_sh_down=w_sh_down)
    depth = w_in.shape[0]
    rows = x.shape[1] // GRID_W
    layers = []
    for i in range(depth):
        p = {name: arr[i] for name, arr in stacked.items()}
        p.update(layer=i, w_exp_gate=w_exp_gate, w_exp_up=w_exp_up, w_exp_down=w_exp_down)
        mods, mods_ctx = _mod_vectors(c, c_ctx, w_mod, b_mod, i)
        p['prep'] = dict(
            w_in=_prep_w_in(w_in, i), mods=mods, mods_ctx=mods_ctx,
            na_bias=_na_bias_table(p['na_rpb'], rows, min(NA_WIN_R, rows)),
            gate_ws=_gla_gate_weights(p['gla_gate_w'], p['gla_gate_b']),
            w_na=p['w_branch_na'].astype(BF16), w_sc=p['w_branch_sc'].astype(BF16),
            w_gla=p['w_branch_gla'].astype(BF16), w_out=p['w_out'].astype(BF16),
            w_router=jnp.pad(p['w_router'], ((0, 0), (0, LANE - N_EXPERTS))).astype(BF16),
            ws_gate=p['w_sh_gate'].astype(BF16), ws_up=p['w_sh_up'].astype(BF16),
            ws_down=p['w_sh_down'].astype(BF16))
        layers.append(p)

    ctx_s = ctx
    projected = None
    for i, p in enumerate(layers):
        last = i == depth - 1
        p_next = None if last else layers[i + 1]
        after_piece = None
        next_projected = []
        if not last:
            def after_piece(xq, b0, nb, p_next=p_next, acc=next_projected):
                acc.append(_project_latent(xq, p_next, b0, nb, acc[-1] if acc else None))

        x, ctx_s = _layer(x, ctx_s, p, not last, last, g_final, projected, after_piece, p_next)
        projected = next_projected[-1] if next_projected else None
    return x
```

```python
import functools

import numpy as np
import jax
import jax.numpy as jnp
from jax import lax
from jax.experimental import pallas as pl
from jax.experimental.pallas import tpu as pltpu
from jax.experimental.pallas import tpu_sc as plsc

F32 = jnp.float32
BF16 = jnp.bfloat16
U32 = jnp.uint32

D_MODEL = 1024
N_MOD = 6
RMS_EPS = 1e-6
NEG_INF = -1e30
GRID_W = 64
NA_HEADS = 8
NA_HEAD_DIM = 64
NA_WIDTH = NA_HEADS * NA_HEAD_DIM
NA_WIN_R = 8
NA_WIN_C = 16
NA_GROUP = 4
SC_WIDTH = 512
GLA_HEADS = 4
GLA_KEY_WIDTH = 512
GLA_VAL_WIDTH = 1024
GLA_DK = GLA_KEY_WIDTH // GLA_HEADS
GLA_DV = GLA_VAL_WIDTH // GLA_HEADS
GLA_GATE_RANK = 16
GLA_GATE_TAU = 16.0
LOG2_E = 1.4426950408889634
N_EXPERTS = 64
N_EXPERT_GROUPS = 8
GROUP_SIZE = N_EXPERTS // N_EXPERT_GROUPS
TOPK_GROUPS = 4
TOP_K = 8
EXPERT_FF = 256
ROUTED_SCALE = 2.5
MOE_BLOCK = 1024

LANE = 128
GLA_C = 128
GLA_STEP_CHUNKS = 4
GLA_LEVELS = tuple(GLA_C >> (i + 1) for i in range(GLA_C.bit_length() - 1))
VMEM_LIMIT = 48 * 1024 * 1024
SC_WINDOW = 128
SC_ROW = 256
SC_PARTS = D_MODEL // 2 // SC_ROW

OFF_V_GLA = 0
OFF_K_NA = 1024
OFF_V_NA = 1536
OFF_K_GLA = 2048
N_KV_MAIN = 2560
OFF_Q_NA = 2560
OFF_B_SC = 3072
OFF_C_SC = 3584
OFF_X_SC = 4096
OFF_Q_GLA = 4608
OFF_R_GLA = 5120
OFF_MERGE = 6144
N_MAIN = 9216


def _cparams(sem, vmem=VMEM_LIMIT):
    return pltpu.CompilerParams(dimension_semantics=sem, vmem_limit_bytes=vmem)


def _dot(a, b):
    return jnp.dot(a, b, preferred_element_type=F32)


def _dot_nt(a, b):
    return lax.dot_general(a, b, (((1,), (1,)), ((), ())), preferred_element_type=F32)


def _dot_tn(a, b):
    return lax.dot_general(a, b, (((0,), (0,)), ((), ())), preferred_element_type=F32)


def _sigmoid(x):
    return 0.5 * jnp.tanh(0.5 * x) + 0.5


def _pack_rows(x):
    n = x.shape[1] // 2
    r = x.astype(BF16).astype(F32)
    lo = pltpu.bitcast(r[:, :n], U32) >> 16
    hi = pltpu.bitcast(r[:, n:], U32)
    return hi | lo


def _store_parts(ref, words):
    for part in range(SC_PARTS):
        dst = ref.at[part, 0] if len(ref.shape) == 4 else ref.at[part]
        dst[...] = words[:, part * SC_ROW:(part + 1) * SC_ROW]


def _load_parts(ref, *lead):
    return jnp.concatenate([ref[(part,) + lead] for part in range(SC_PARTS)], axis=-1)


def _unpack_rows(w):
    lo = pltpu.bitcast(w << 16, F32)
    hi = pltpu.bitcast(w & jnp.uint32(0xFFFF0000), F32)
    return lo, hi


def _mod_kernel(a_ref, w_ref, b_ref, o_ref):
    a = a_ref[...]
    a = a * _sigmoid(a)
    o_ref[...] = _dot(a.astype(BF16), w_ref[0].astype(BF16)) + b_ref[0]


def _mod_vectors(c, c_ctx, w_mod, b_mod, layer):
    b = c.shape[0]
    rows = -(-(b + 1) // 8) * 8
    a = jnp.concatenate([c, c_ctx[None], jnp.zeros((rows - b - 1, D_MODEL), F32)], axis=0)
    n = N_MOD * D_MODEL
    tn = 1536
    out = pl.pallas_call(
        _mod_kernel,
        grid=(n // tn,),
        in_specs=[pl.BlockSpec((rows, D_MODEL), lambda j: (0, 0)),
                  pl.BlockSpec((1, D_MODEL, tn), lambda j: (layer, 0, j)),
                  pl.BlockSpec((1, 1, tn), lambda j: (layer, 0, j))],
        out_specs=pl.BlockSpec((rows, tn), lambda j: (0, j)),
        out_shape=jax.ShapeDtypeStruct((rows, n), F32),
        compiler_params=_cparams(("parallel",)),
        name="mod_vectors",
    )(a, w_mod, b_mod[:, None])
    lat = out[:b].reshape(b, N_MOD, 1, D_MODEL)
    ctx = out[b].reshape(N_MOD, 1, 1, D_MODEL)
    return [lat[:, i] for i in range(N_MOD)], [ctx[i] for i in range(N_MOD)]


def _proj_kernel(x_ref, g_ref, sh_ref, sc_ref, w_ref, wg_ref, *rest):
    o_ref, og_ref, h_ref = rest[-3:]

    @pl.when(pl.program_id(2) == 0)
    def _():
        x = x_ref[0]
        ms = jnp.mean(x * x, axis=-1, keepdims=True)
        h = x * lax.rsqrt(ms + RMS_EPS) * g_ref[...] * (1.0 + sc_ref[0]) + sh_ref[0]
        hb = h.astype(BF16)
        h_ref[...] = hb
        og_ref[0] = _dot(hb, wg_ref[...])

    o_ref[0] = _dot(h_ref[...], w_ref[...]).astype(o_ref.dtype)


W_IN_TILE = 512
W_IN_GATE_SHIFT = 2 * GLA_GATE_RANK


def _prep_w_in_kernel(a_ref, b_ref, o_ref, g_ref):
    t = pl.program_id(0)
    first_lat = N_KV_MAIN // W_IN_TILE
    a = a_ref[0]

    @pl.when(t < first_lat)
    def _():
        o_ref[...] = a.T.astype(BF16)

    @pl.when(t >= first_lat)
    def _():
        moved = jnp.concatenate([a[W_IN_GATE_SHIFT:], b_ref[0]], axis=0)
        scale = jnp.where(t == first_lat, NA_HEAD_DIM ** -0.5, 1.0)
        o_ref[...] = (moved * scale).T.astype(BF16)

    @pl.when(t == first_lat)
    def _():
        head = a[:LANE]
        row = lax.broadcasted_iota(jnp.int32, head.shape, 0)
        g_ref[...] = jnp.where(row < W_IN_GATE_SHIFT, head, 0.0).T.astype(BF16)


def _prep_w_in(w_in, layer):
    d = w_in.shape[1]
    w_t = jnp.swapaxes(w_in, 1, 2)
    first_lat = N_KV_MAIN // W_IN_TILE
    kv_perm = OFF_K_NA // W_IN_TILE

    def a_map(t):
        return (layer, jnp.where(t < first_lat, (t + first_lat - kv_perm) % first_lat, t), 0)

    def b_map(t):
        return (layer, jnp.where(t < first_lat, 0, (t + 1) * (W_IN_TILE // W_IN_GATE_SHIFT)), 0)

    return pl.pallas_call(
        _prep_w_in_kernel,
        grid=(N_MAIN // W_IN_TILE,),
        in_specs=[pl.BlockSpec((1, W_IN_TILE, d), a_map),
                  pl.BlockSpec((1, W_IN_GATE_SHIFT, d), b_map)],
        out_specs=[pl.BlockSpec((d, W_IN_TILE), lambda t: (0, t)),
                   pl.BlockSpec((d, LANE), lambda t: (0, 0))],
        out_shape=[jax.ShapeDtypeStruct((d, N_MAIN), BF16), jax.ShapeDtypeStruct((d, LANE), BF16)],
        compiler_params=_cparams(("arbitrary",)),
        name="prep_w_in",
    )(w_t, w_t)


def _proj_in(x, g, shift, scale, w_main, w_gate, tm, tn, n=None, b0=0, nb=None, prev=None):
    b, s, d = x.shape
    n = w_main.shape[1] if n is None else n
    nb = b if nb is None else nb
    per_batch = shift.shape[0] == b
    mod_map = (lambda bi, i, j: (b0 + bi, 0, 0)) if per_batch else (lambda bi, i, j: (0, 0, 0))
    operands = (x, g[None], shift, scale, w_main, w_gate)
    extend = () if prev is None else tuple(prev)
    return pl.pallas_call(
        _proj_kernel,
        grid=(nb, s // tm, n // tn),
        in_specs=[pl.BlockSpec((1, tm, d), lambda bi, i, j: (b0 + bi, i, 0)),
                  pl.BlockSpec((1, d), lambda bi, i, j: (0, 0)),
                  pl.BlockSpec((1, 1, d), mod_map),
                  pl.BlockSpec((1, 1, d), mod_map),
                  pl.BlockSpec((d, tn), lambda bi, i, j: (0, j)),
                  pl.BlockSpec((d, LANE), lambda bi, i, j: (0, 0))]
                 + [pl.BlockSpec(memory_space=pl.ANY)] * len(extend),
        out_specs=[pl.BlockSpec((1, tm, tn), lambda bi, i, j: (b0 + bi, i, j)),
                   pl.BlockSpec((1, tm, LANE), lambda bi, i, j: (b0 + bi, i, 0))],
        out_shape=[jax.ShapeDtypeStruct((b, s, n), BF16),
                   jax.ShapeDtypeStruct((b, s, LANE), F32)],
        scratch_shapes=[pltpu.VMEM((tm, d), BF16)],
        input_output_aliases={len(operands): 0, len(operands) + 1: 1} if extend else {},
        compiler_params=_cparams(("parallel", "parallel", "arbitrary")),
        name="proj_in",
    )(*operands, *extend)


def _softmax_av(q, keys, vals, biases):
    scores = []
    for kk, bb in zip(keys, biases):
        s = _dot_nt(q, kk)
        scores.append(s if bb is None else s + bb)
    m = scores[0].max(axis=-1, keepdims=True)
    for s in scores[1:]:
        m = jnp.maximum(m, s.max(axis=-1, keepdims=True))
    num = None
    den = None
    for s, vv in zip(scores, vals):
        e = jnp.exp(s - m)
        dsum = e.sum(axis=-1, keepdims=True)
        o = _dot(e.astype(BF16), vv)
        num = o if num is None else num + o
        den = dsum if den is None else den + dsum
    return num / den


def _na_kernel(q_ref, k_ref, v_ref, kc_ref, vc_ref, *rest, rows, kr):
    *bias_refs, o_ref = rest
    kc = kc_ref[0]
    vc = vc_ref[0]
    for j, bias_ref in enumerate(bias_refs):
        r = pl.program_id(1) * len(bias_refs) + j
        row_start = jnp.clip(r - kr // 2, 0, rows - kr)
        start = pl.multiple_of(row_start * GRID_W, GRID_W)
        n_win = kr * GRID_W
        q = q_ref[0, j * GRID_W:(j + 1) * GRID_W, :]
        kw = k_ref[0, pl.ds(start, n_win), :]
        vw = v_ref[0, pl.ds(start, n_win), :]
        o_ref[0, j * GRID_W:(j + 1) * GRID_W, :] = _na_row(q, kw, vw, kc, vc, bias_ref).astype(o_ref.dtype)


def _na_row(q, kw, vw, kc, vc, bias_ref):
    gw = NA_GROUP * NA_HEAD_DIM
    stacked = (NA_GROUP * GRID_W, gw)
    on_head = (lax.broadcasted_iota(jnp.int32, stacked, 0) // GRID_W
               == lax.broadcasted_iota(jnp.int32, stacked, 1) // NA_HEAD_DIM)
    outs = []
    for g in range(NA_HEADS // NA_GROUP):
        sl = slice(g * gw, (g + 1) * gw)
        q_all = jnp.where(on_head, jnp.concatenate([q[:, sl]] * NA_GROUP, axis=0), jnp.zeros((), q.dtype))
        bias = bias_ref[0, g * NA_GROUP * GRID_W:(g + 1) * NA_GROUP * GRID_W, :]
        o_all = _softmax_av(q_all, [kw[:, sl], kc[:, sl]], [vw[:, sl], vc[:, sl]], [bias, None])
        o_all = jnp.where(on_head, o_all, 0.0).reshape(NA_GROUP, GRID_W, gw)
        outs.append(o_all.sum(axis=0))
    return jnp.concatenate(outs, axis=-1)


def _na_bias_table(rpb, rows, kr):
    col = np.arange(GRID_W)
    col_start = np.clip(col - NA_WIN_C // 2, 0, GRID_W - NA_WIN_C)
    col_ok = (col[None, :] >= col_start[:, None]) & (col[None, :] < col_start[:, None] + NA_WIN_C)
    d_col = np.clip(col[None, :] - col[:, None], -(NA_WIN_C - 1), NA_WIN_C - 1) + NA_WIN_C - 1
    n_dr, n_dc = rpb.shape[1], rpb.shape[2]
    onehot = jnp.asarray((d_col.reshape(-1)[None, :] == np.arange(n_dc)[:, None]).astype(np.float32))
    by_col = jnp.dot(rpb.astype(F32).reshape(NA_HEADS * n_dr, n_dc), onehot, precision=lax.Precision.HIGHEST)
    by_col = by_col.reshape(NA_HEADS, n_dr, GRID_W, GRID_W)
    by_col = jnp.where(col_ok[None, None], by_col, NEG_INF)
    tables = []
    for o in range(kr):
        lo = NA_WIN_R - 1 - o
        tables.append(by_col[:, lo:lo + kr].transpose(0, 2, 1, 3).reshape(NA_HEADS, GRID_W, kr * GRID_W))
    return jnp.stack(tables).reshape(kr, NA_HEADS * GRID_W, kr * GRID_W)


def _na_latent(main, main_ctx, bias):
    b, s, _ = main.shape
    sc = main_ctx.shape[1]
    rows = s // GRID_W
    kr = min(NA_WIN_R, rows)
    w = NA_WIDTH

    per_step = next(n for n in (8, 4, 2, 1) if rows % n == 0)

    def bias_spec(j):
        def bias_map(bi, i):
            r = i * per_step + j
            return (r - jnp.clip(r - kr // 2, 0, rows - kr), 0, 0)
        return pl.BlockSpec((1, NA_HEADS * GRID_W, kr * GRID_W), bias_map)

    return pl.pallas_call(
        functools.partial(_na_kernel, rows=rows, kr=kr),
        grid=(b, rows // per_step),
        in_specs=[pl.BlockSpec((1, per_step * GRID_W, w), lambda bi, i: (bi, i, OFF_Q_NA // w)),
                  pl.BlockSpec((1, s, w), lambda bi, i: (bi, 0, OFF_K_NA // w)),
                  pl.BlockSpec((1, s, w), lambda bi, i: (bi, 0, OFF_V_NA // w)),
                  pl.BlockSpec((1, sc, w), lambda bi, i: (bi, 0, OFF_K_NA // w)),
                  pl.BlockSpec((1, sc, w), lambda bi, i: (bi, 0, OFF_V_NA // w))]
                 + [bias_spec(j) for j in range(per_step)],
        out_specs=pl.BlockSpec((1, per_step * GRID_W, w), lambda bi, i: (bi, i, 0)),
        out_shape=jax.ShapeDtypeStruct((b, s, w), BF16),
        compiler_params=_cparams(("parallel", "arbitrary")),
        name="na_latent",
    )(main, main, main, main_ctx, main_ctx, *([bias] * per_step))


def _dense_attn_kernel(q_ref, k_ref, v_ref, o_ref):
    q = q_ref[0]
    k = k_ref[0]
    v = v_ref[0]
    outs = []
    for h in range(NA_HEADS):
        sl = slice(h * NA_HEAD_DIM, (h + 1) * NA_HEAD_DIM)
        outs.append(_softmax_av(q[:, sl], [k[:, sl]], [v[:, sl]], [None]))
    o_ref[0] = jnp.concatenate(outs, axis=-1).astype(o_ref.dtype)


def _dense_attn(main_ctx):
    b, sc, _ = main_ctx.shape
    w = NA_WIDTH
    return pl.pallas_call(
        _dense_attn_kernel,
        grid=(b,),
        in_specs=[pl.BlockSpec((1, sc, w), lambda bi: (bi, 0, OFF_Q_NA // w)),
                  pl.BlockSpec((1, sc, w), lambda bi: (bi, 0, OFF_K_NA // w)),
                  pl.BlockSpec((1, sc, w), lambda bi: (bi, 0, OFF_V_NA // w))],
        out_specs=pl.BlockSpec((1, sc, w), lambda bi: (bi, 0, 0)),
        out_shape=jax.ShapeDtypeStruct((b, sc, w), BF16),
        compiler_params=_cparams(("parallel",)),
        name="ctx_attn",
    )(main_ctx, main_ctx, main_ctx)


HALO = 16


def _gla_matrices(reverse):
    c = GLA_C
    t = np.arange(c)[:, None]
    m = np.arange(c)[None, :]
    blocks = [m <= t, m > t]
    for b in GLA_LEVELS:
        first = (t // (2 * b)) * (2 * b) + b
        is_q = (t & b) != 0
        blocks.append(np.where(is_q, (m > first) & (m <= t), (m > t) & (m <= first)))
    mats = np.stack(blocks).astype(np.float32)
    if reverse:
        mats = mats[:, ::-1, ::-1]
    mats = mats.reshape(-1, c)
    return jnp.asarray(np.concatenate([mats, mats], axis=1), dtype=BF16)


def _split_bf16(x):
    hi = x.astype(BF16)
    return hi, (x - hi.astype(F32)).astype(BF16)


def _pair_block_diag(x):
    lane = lax.broadcasted_iota(jnp.int32, x.shape, 1)
    zero = jnp.zeros((), x.dtype)
    return jnp.concatenate([jnp.where(lane < GLA_DK, x, zero), jnp.where(lane >= GLA_DK, x, zero)], axis=0)


def _gla_kernel(*refs, emit):
    n_in = 9 if emit else 8
    n_out = 2 if emit else 1
    ins, outs, scratch = refs[:2 * n_in], refs[2 * n_in:2 * (n_in + n_out)], refs[2 * (n_in + n_out):]
    ins = [ins[d * n_in:(d + 1) * n_in] for d in range(2)]
    outs = [outs[d * n_out:(d + 1) * n_out] for d in range(2)]
    step = pl.program_id(1)

    @pl.when(step == 0)
    def _():
        for d in range(2):
            scratch[d][...] = ins[d][-1][0]

    pending = [_gla_direction(ins[d][:-1], outs[d][:-1], scratch[d], reverse, emit)
               for d, reverse in enumerate((False, True))]
    while pending:
        pending = [stages for stages in pending if next(stages, "done") != "done"]

    @pl.when(step == pl.num_programs(1) - 1)
    def _():
        for d in range(2):
            outs[d][-1][0] = scratch[d][...]


def _gla_direction(ins, outs, st_ref, reverse, emit):
    gate_ref = ins[-5]
    order = range(gate_ref.shape[1] // GLA_C)
    for sub in (reversed(order) if reverse else order):
        yield from _gla_chunk(ins, outs, st_ref, reverse, emit, slice(sub * GLA_C, (sub + 1) * GLA_C))


def _gla_chunk(ins, outs, st_ref, reverse, emit, rows):
    if emit:
        q_ref, k_ref, v_ref, gt_ref, w2a_ref, w2b_ref, b2_ref, a_ref = ins
        (o_ref,) = outs
    else:
        k_ref, v_ref, gt_ref, w2a_ref, w2b_ref, b2_ref, a_ref = ins
    c = GLA_C
    pw = 2 * GLA_DK

    lr_hi, lr_lo = _split_bf16(gt_ref[0, rows])
    logit = (_dot(jnp.concatenate([lr_hi, lr_lo], axis=1), w2a_ref[...]) + _dot(lr_hi, w2b_ref[...])
             + b2_ref[...])
    g = (jnp.minimum(logit, 0.0) - jnp.log1p(jnp.exp(-jnp.abs(logit)))) * (LOG2_E / GLA_GATE_TAU)
    g_hi, g_lo = _split_bf16(g)
    args = _dot(a_ref[...], jnp.concatenate([g_hi, g_lo], axis=0))
    cum = args[0:c]
    rem = args[c:2 * c]
    last_row = cum[0:1] if reverse else cum[c - 1:c]
    yield

    k = k_ref[0, rows].astype(F32)
    v = v_ref[0, rows]
    atts = []
    if emit:
        q = q_ref[0, rows].astype(F32) * (GLA_DK ** -0.5)
        row_t = lax.broadcasted_iota(jnp.int32, (c, pw), 0)
        si = lax.broadcasted_iota(jnp.int32, (2 * c, c), 0) & (c - 1)
        ti = lax.broadcasted_iota(jnp.int32, (2 * c, c), 1)
        if reverse:
            row_t, ti, si = c - 1 - row_t, c - 1 - ti, c - 1 - si
        for hp in range(GLA_HEADS // 2):
            cs = slice(hp * pw, (hp + 1) * pw)
            qp, kp = q[:, cs], k[:, cs]
            att = jnp.where(ti == si, _dot_nt(_pair_block_diag(kp.astype(BF16)), qp.astype(BF16)), 0.0)
            for l, b in enumerate(GLA_LEVELS):
                x = (jnp.exp2(args[(2 + l) * c:(3 + l) * c, cs])
                     * jnp.where((row_t & b) != 0, qp, kp)).astype(BF16)
                pair = (((ti ^ si) >> (b.bit_length() - 1)) == 1) & ((ti & b) != 0)
                att = jnp.where(pair, _dot_nt(_pair_block_diag(x), x), att)
                yield
            atts.append(att.astype(BF16))

    outs = []
    for h in range(GLA_HEADS):
        sl = slice(h * GLA_DK, (h + 1) * GLA_DK)
        kh = k[:, sl]
        vh = v[:, h * GLA_DV:(h + 1) * GLA_DV]
        state = st_ref[h]
        kd = (kh * jnp.exp2(rem[:, sl])).astype(BF16)
        decay = jnp.exp2(jnp.broadcast_to(last_row[:, sl], (GLA_DK, GLA_DK))).T
        decay = jnp.concatenate([decay] * (GLA_DV // GLA_DK), axis=1)
        if emit:
            qd = (q[:, sl] * jnp.exp2(cum[:, sl])).astype(BF16)
            att_t = atts[h // 2][(h % 2) * c:(h % 2 + 1) * c]
            both = _dot_tn(jnp.concatenate([att_t, kd], axis=1), vh)
            outs.append(_dot(qd, state.astype(BF16)) + both[:c])
            st_ref[h] = decay * state + both[c:]
        else:
            st_ref[h] = decay * state + _dot_tn(kd, vh)
        yield

    if emit:
        o_ref[0, rows] = jnp.concatenate(outs, axis=-1).astype(o_ref.dtype)


def _gla_scan(main, gate, gate_ws, s0s, emit):
    b, l, _ = main.shape
    step_rows = GLA_C * min(GLA_STEP_CHUNKS, l // GLA_C)
    n = l // step_rows
    const = lambda arr: pl.BlockSpec(arr.shape, lambda bi, s: (0,) * arr.ndim)
    state_spec = pl.BlockSpec((1, GLA_HEADS, GLA_DK, GLA_DV), lambda bi, s: (bi, 0, 0, 0))
    state_shape = jax.ShapeDtypeStruct((b, GLA_HEADS, GLA_DK, GLA_DV), F32)
    in_specs, args, out_specs, out_shape = [], [], [], []
    for reverse in (False, True):
        amat = _gla_matrices(reverse)
        w2a, w2b, b2 = gate_ws[reverse]

        def col(block, reverse=reverse):
            return lambda bi, s: (bi, n - 1 - s if reverse else s, block)

        if emit:
            in_specs.append(pl.BlockSpec((1, step_rows, GLA_KEY_WIDTH), col(OFF_Q_GLA // GLA_KEY_WIDTH)))
            args.append(main)
            out_specs.append(pl.BlockSpec((1, step_rows, GLA_VAL_WIDTH), col(0)))
            out_shape.append(jax.ShapeDtypeStruct((b, l, GLA_VAL_WIDTH), BF16))
        in_specs += [pl.BlockSpec((1, step_rows, GLA_KEY_WIDTH), col(OFF_K_GLA // GLA_KEY_WIDTH)),
                     pl.BlockSpec((1, step_rows, GLA_VAL_WIDTH), col(OFF_V_GLA // GLA_VAL_WIDTH)),
                     pl.BlockSpec((1, step_rows, LANE), col(0)),
                     const(w2a), const(w2b), const(b2), const(amat), state_spec]
        args += [main, main, gate, w2a, w2b, b2, amat, s0s[reverse]]
        out_specs.append(state_spec)
        out_shape.append(state_shape)
    res = pl.pallas_call(
        functools.partial(_gla_kernel, emit=emit),
        grid=(b, n),
        in_specs=in_specs,
        out_specs=out_specs,
        out_shape=out_shape,
        scratch_shapes=[pltpu.VMEM((GLA_HEADS, GLA_DK, GLA_DV), F32)] * 2,
        compiler_params=_cparams(("parallel", "arbitrary")),
        name="gla_scan",
    )(*args)
    return (res[0], res[2], res[1], res[3]) if emit else (None, None, res[0], res[1])


def _gla_gate_weights(gate_w, gate_b):
    out = []
    for dr in range(2):
        w = jnp.zeros((LANE, GLA_KEY_WIDTH), F32)
        w = w.at[dr * GLA_GATE_RANK:(dr + 1) * GLA_GATE_RANK].set(gate_w[dr])
        w_hi = w.astype(BF16)
        w_lo = (w - w_hi.astype(F32)).astype(BF16)
        out.append((jnp.concatenate([w_hi, w_hi], axis=0), w_lo, gate_b[dr][None]))
    return out


def _conv_tile(b_ref, c_ref, x_ref, cp_ref, xp_ref, cn_ref, xn_ref, w_ref):
    i, n = pl.program_id(1), pl.num_programs(1)
    u = c_ref[0].astype(F32) * x_ref[0].astype(F32)
    tm = u.shape[0]
    before = jnp.where(i > 0, 1.0, 0.0) * (cp_ref[0, HALO - 1:HALO].astype(F32) * xp_ref[0, HALO - 1:HALO].astype(F32))
    after = jnp.where(i < n - 1, 1.0, 0.0) * (cn_ref[0, 0:1].astype(F32) * xn_ref[0, 0:1].astype(F32))
    t = lax.broadcasted_iota(jnp.int32, u.shape, 0)
    prev = jnp.where(t == 0, before, pltpu.roll(u, 1, axis=0))
    nxt = jnp.where(t == tm - 1, after, pltpu.roll(u, tm - 1, axis=0))
    w = w_ref[...]
    return b_ref[0].astype(F32) * (prev * w[0:1] + u * w[1:2] + nxt * w[2:3])


def _merge_kernel(ona_ref, bsc_ref, csc_ref, xsc_ref, cp_ref, xp_ref, cn_ref, xn_ref, cw_ref, of_ref, ob_ref,
                  r_ref, gna_ref, gsc_ref, ggl_ref, x_ref, gt_ref,
                  gn_ref, wna_ref, wsc_ref, wgl_ref, wo_ref, g2_ref, sh2_ref, sc2_ref, wr_ref, *rest):
    xo_ref, h2_ref, hp_ref, lg_ref = rest[-4:]
    o_sc = _conv_tile(bsc_ref, csc_ref, xsc_ref, cp_ref, xp_ref, cn_ref, xn_ref, cw_ref).astype(BF16)
    o = of_ref[0].astype(F32) + ob_ref[0].astype(F32)
    normed = []
    for h in range(GLA_HEADS):
        oh = o[:, h * GLA_DV:(h + 1) * GLA_DV]
        ms = jnp.mean(oh * oh, axis=-1, keepdims=True)
        normed.append(oh * lax.rsqrt(ms + RMS_EPS))
    r = r_ref[0].astype(F32)
    y_gla = jnp.concatenate(normed, axis=-1) * gn_ref[...] * (r * _sigmoid(r))
    y = (_sigmoid(gna_ref[0].astype(F32)) * _dot(ona_ref[0], wna_ref[...])
         + _sigmoid(gsc_ref[0].astype(F32)) * _dot(o_sc, wsc_ref[...])
         + _sigmoid(ggl_ref[0].astype(F32)) * _dot(y_gla.astype(BF16), wgl_ref[...]))
    xn = x_ref[0] + gt_ref[0] * _dot(y.astype(BF16), wo_ref[...])
    xo_ref[0] = xn
    ms = jnp.mean(xn * xn, axis=-1, keepdims=True)
    h2 = xn * lax.rsqrt(ms + RMS_EPS) * g2_ref[...] * (1.0 + sc2_ref[0]) + sh2_ref[0]
    h2b = h2.astype(BF16)
    h2_ref[0] = h2b
    _store_parts(hp_ref, _pack_rows(h2))
    lg_ref[...] = _dot(h2b, wr_ref[...])


def _merge(o_na, conv_w, o_f, o_b, main, x, gt1, gn, w_na, w_sc, w_gla, w_out, g2, sh2, sc2, w_router, tm,
           n_routed, tok_off, routed=None):
    b, s, d = x.shape
    per_batch = gt1.shape[0] == b
    mod_map = (lambda bi, i: (bi, 0, 0)) if per_batch else (lambda bi, i: (0, 0, 0))
    tok = lambda width, blk: pl.BlockSpec((1, tm, width), lambda bi, i: (bi, i, blk))
    full = lambda arr: pl.BlockSpec(arr.shape, lambda bi, i: (0,) * arr.ndim)
    mod = pl.BlockSpec((1, 1, d), mod_map)
    gn_t = jnp.tile(gn, GLA_HEADS)[None]
    g2_t = g2[None]
    per_tile = tm // HALO
    last_halo = s // HALO - 1
    halo_prev = lambda blk: pl.BlockSpec(
        (1, HALO, SC_WIDTH), lambda bi, i: (bi, jnp.maximum(i * per_tile - 1, 0), blk))
    halo_next = lambda blk: pl.BlockSpec(
        (1, HALO, SC_WIDTH), lambda bi, i: (bi, jnp.minimum((i + 1) * per_tile, last_halo), blk))
    col_b, col_c, col_x = OFF_B_SC // SC_WIDTH, OFF_C_SC // SC_WIDTH, OFF_X_SC // SC_WIDTH
    extend = () if routed is None else tuple(routed)
    operands = (o_na, main, main, main, main, main, main, main, conv_w, o_f, o_b, main, main, main, main, x, gt1,
                gn_t, w_na, w_sc, w_gla, w_out, g2_t, sh2, sc2, w_router)
    n_in = len(operands)
    nblk = s // tm
    blk0 = tok_off // tm
    return pl.pallas_call(
        _merge_kernel,
        grid=(b, s // tm),
        in_specs=[tok(NA_WIDTH, 0), tok(SC_WIDTH, col_b), tok(SC_WIDTH, col_c), tok(SC_WIDTH, col_x),
                  halo_prev(col_c), halo_prev(col_x), halo_next(col_c), halo_next(col_x), full(conv_w),
                  tok(GLA_VAL_WIDTH, 0), tok(GLA_VAL_WIDTH, 0),
                  tok(d, OFF_R_GLA // d), tok(d, OFF_MERGE // d), tok(d, OFF_MERGE // d + 1),
                  tok(d, OFF_MERGE // d + 2), tok(d, 0), mod,
                  full(gn_t), full(w_na), full(w_sc), full(w_gla), full(w_out), full(g2_t), mod, mod,
                  full(w_router)] + [pl.BlockSpec(memory_space=pl.ANY)] * len(extend),
        out_specs=[tok(d, 0), tok(d, 0),
                   pl.BlockSpec((SC_PARTS, tm, SC_ROW), lambda bi, i: (0, blk0 + bi * nblk + i, 0)),
                   pl.BlockSpec((tm, LANE), lambda bi, i: (blk0 + bi * nblk + i, 0))],
        out_shape=[jax.ShapeDtypeStruct((b, s, d), F32),
                   jax.ShapeDtypeStruct((b, s, d), BF16),
                   jax.ShapeDtypeStruct((SC_PARTS, n_routed, SC_ROW), U32),
                   jax.ShapeDtypeStruct((n_routed, LANE), F32)],
        input_output_aliases={n_in: 2, n_in + 1: 3} if extend else {},
        compiler_params=_cparams(("parallel", "parallel")),
        name="merge",
    )(*operands, *extend)


def _router_kernel(lg_ref, br_ref, tri_ref, eidx_ref, gw_ref, rank_ref, cnt_ref, carry_ref):
    tm = lg_ref.shape[0]

    @pl.when(pl.program_id(0) == 0)
    def _():
        carry_ref[...] = jnp.zeros_like(carry_ref)

    scores = _sigmoid(lg_ref[...].T[:N_EXPERTS])
    sel = scores + br_ref[...]
    neg = -jnp.inf

    sel3 = sel.reshape(N_EXPERT_GROUPS, GROUP_SIZE, tm)
    i3 = lax.broadcasted_iota(jnp.int32, sel3.shape, 1)
    m1 = sel3.max(axis=1, keepdims=True)
    first = jnp.where(sel3 == m1, i3, GROUP_SIZE).min(axis=1, keepdims=True)
    m2 = jnp.where(i3 == first, neg, sel3).max(axis=1, keepdims=True)
    gscore = (m1 + m2)[:, 0, :]

    gi = lax.broadcasted_iota(jnp.int32, gscore.shape, 0)
    gmask = jnp.zeros(gscore.shape, jnp.bool_)
    for _ in range(TOPK_GROUPS):
        m = gscore.max(axis=0, keepdims=True)
        pick = gi == jnp.where(gscore == m, gi, N_EXPERT_GROUPS).min(axis=0, keepdims=True)
        gmask = gmask | pick
        gscore = jnp.where(pick, neg, gscore)
    emask = jnp.broadcast_to(gmask[:, None, :], sel3.shape).reshape(N_EXPERTS, tm)
    sel = jnp.where(emask, sel, neg)

    ei = lax.broadcasted_iota(jnp.int32, sel.shape, 0)
    picks, idxs, ws = [], [], []
    for _ in range(TOP_K):
        m = sel.max(axis=0, keepdims=True)
        idx = jnp.where(sel == m, ei, N_EXPERTS).min(axis=0, keepdims=True)
        pick = ei == idx
        picks.append(pick)
        idxs.append(idx)
        ws.append(jnp.where(pick, scores, 0.0).sum(axis=0, keepdims=True))
        sel = jnp.where(pick, neg, sel)
    w = jnp.concatenate(ws, axis=0)
    gw_ref[...] = w / w.sum(axis=0, keepdims=True) * ROUTED_SCALE
    eidx_ref[...] = jnp.concatenate(idxs, axis=0)

    onehot = picks[0]
    for p in picks[1:]:
        onehot = onehot | p
    onehot = jnp.where(onehot, 1.0, 0.0).astype(BF16)
    before = _dot(onehot, tri_ref[...]) + jnp.tile(carry_ref[...], (1, tm // LANE))
    rank_ref[...] = jnp.concatenate(
        [jnp.where(p, before, 0.0).sum(axis=0, keepdims=True) for p in picks], axis=0).astype(jnp.int32)
    carry_ref[...] += _dot(onehot, jnp.ones((tm, LANE), BF16))
    cnt_ref[...] = carry_ref[...]


def _route(logits, b_router):
    t = logits.shape[0]
    tm = next(n for n in (1024, 512) if t % n == 0)
    br = jnp.broadcast_to(b_router.astype(F32)[:, None], (N_EXPERTS, tm))
    tri = jnp.asarray(np.triu(np.ones((tm, tm), np.float32), 1), dtype=BF16)
    kt = lambda dt: jax.ShapeDtypeStruct((TOP_K, t), dt)
    eidx, gw, rank, cnt = pl.pallas_call(
        _router_kernel,
        grid=(t // tm,),
        in_specs=[pl.BlockSpec((tm, LANE), lambda i: (i, 0)),
                  pl.BlockSpec((N_EXPERTS, tm), lambda i: (0, 0)),
                  pl.BlockSpec((tm, tm), lambda i: (0, 0))],
        out_specs=[pl.BlockSpec((TOP_K, tm), lambda i: (0, i)),
                   pl.BlockSpec((TOP_K, tm), lambda i: (0, i)),
                   pl.BlockSpec((TOP_K, tm), lambda i: (0, i)),
                   pl.BlockSpec((N_EXPERTS, LANE), lambda i: (0, 0))],
        out_shape=[kt(jnp.int32), kt(F32), kt(jnp.int32),
                   jax.ShapeDtypeStruct((N_EXPERTS, LANE), F32)],
        scratch_shapes=[pltpu.VMEM((N_EXPERTS, LANE), F32)],
        compiler_params=_cparams(("arbitrary",)),
        name="router",
    )(logits, br, tri)
    return eidx, gw, rank, cnt[:, 0].astype(jnp.int32)


def _sc_mesh():
    return plsc.VectorSubcoreMesh(core_axis_name="core", subcore_axis_name="subcore")


def _dispatch_rows(xp, dest, slots):
    parts, t, _ = xp.shape

    @pl.kernel(out_type=jax.ShapeDtypeStruct((parts, slots, SC_ROW), xp.dtype), mesh=_sc_mesh(),
               scratch_types=[], name="moe_dispatch")
    def run(x_hbm, d_hbm, o_hbm):
        for part in range(parts):
            out_part = o_hbm.at[part]

            def body(x_vmem, d_vmem, out_part=out_part):
                for k in range(TOP_K):
                    pltpu.sync_copy(x_vmem, out_part.at[d_vmem.at[k]])

            pltpu.emit_pipeline(
                body,
                grid=(t // SC_WINDOW,),
                in_specs=[pl.BlockSpec((SC_WINDOW, SC_ROW), lambda i: (i, 0)),
                          pl.BlockSpec((TOP_K, SC_WINDOW), lambda i: (0, i))],
                out_specs=[],
                core_axis_name=("core", "subcore"),
                dimension_semantics=(pltpu.PARALLEL,),
            )(x_hbm.at[part], d_hbm)

    return run(xp, dest)


def _gather_rows(yp, dest, t0, n):
    parts = yp.shape[0]
    nwin = n // SC_WINDOW
    win0 = t0 // SC_WINDOW

    @pl.kernel(out_type=jax.ShapeDtypeStruct((parts, TOP_K * n, SC_ROW), yp.dtype), mesh=_sc_mesh(),
               scratch_types=[], name="moe_gather")
    def run(y_hbm, d_hbm, o_hbm):
        for part in range(parts):
            table = y_hbm.at[part]

            def body(d_vmem, o_vmem, table=table):
                pltpu.sync_copy(table.at[d_vmem.at[0]], o_vmem)

            pltpu.emit_pipeline(
                body,
                grid=(TOP_K, nwin),
                in_specs=[pl.BlockSpec((1, SC_WINDOW), lambda k, j: (k, win0 + j))],
                out_specs=[pl.BlockSpec((SC_WINDOW, SC_ROW), lambda k, j: (k * nwin + j, 0))],
                core_axis_name=("core", "subcore"),
                dimension_semantics=(pltpu.PARALLEL, pltpu.PARALLEL),
            )(d_hbm, o_hbm.at[part])

    return run(yp, dest).reshape(parts, TOP_K, n, SC_ROW)


def _expert_kernel(be_ref, bv_ref, bs_ref, nx_ref, sl_ref, x_ref, wg_hbm, wu_hbm, wd_hbm, o_ref,
                   wg_f, wu_f, wd_f, wg_s, wu_s, wd_s, sems, *, layer):
    i = pl.program_id(0)
    valid = bv_ref[i]
    expert = be_ref[i]
    new_expert = (i == 0) | (expert != be_ref[jnp.maximum(i - 1, 0)])
    slot = sl_ref[i]

    def fetch(which, into):
        return [pltpu.make_async_copy(src.at[layer, which], dst.at[into], sems.at[into, j])
                for j, (src, dst) in enumerate(((wg_hbm, wg_f), (wu_hbm, wu_f), (wd_hbm, wd_f)))]

    @pl.when(i == 0)
    def _():
        for cp in fetch(expert, slot):
            cp.start()

    @pl.when(new_expert)
    def _():
        for cp in fetch(expert, slot):
            cp.wait()
        upcoming = nx_ref[i]

        @pl.when(upcoming >= 0)
        def _():
            for cp in fetch(upcoming, 1 - slot):
                cp.start()

        wg_s[...] = wg_f[slot].astype(BF16)
        wu_s[...] = wu_f[slot].astype(BF16)
        wd_s[...] = wd_f[slot].astype(BF16)

    @pl.when(valid > 0)
    def _():
        w = _load_parts(x_ref)
        row = lax.broadcasted_iota(jnp.int32, w.shape, 0)
        w = jnp.where(row < valid, w, jnp.uint32(0))
        lo, hi = _unpack_rows(w)
        x = jnp.concatenate([lo, hi], axis=1).astype(BF16)
        a = _dot(x, wg_s[...])
        hid = a * _sigmoid(a) * _dot(x, wu_s[...])
        _store_parts(o_ref, _pack_rows(_dot(hid.astype(BF16), wd_s[...])))


def _experts(xs, blk_e, blk_valid, blk_src, blk_next, blk_slot, layer, w_gate, w_up, w_down):
    parts, slots, _ = xs.shape
    d = D_MODEL
    nb = slots // MOE_BLOCK
    data = pl.BlockSpec((parts, MOE_BLOCK, SC_ROW), lambda i, be, bv, bs, nx, sl: (0, bs[i], 0))
    stage = lambda shape: pltpu.VMEM((2,) + shape, F32)
    return pl.pallas_call(
        functools.partial(_expert_kernel, layer=layer),
        grid_spec=pltpu.PrefetchScalarGridSpec(
            num_scalar_prefetch=5,
            grid=(nb,),
            in_specs=[data] + [pl.BlockSpec(memory_space=pl.ANY)] * 3,
            out_specs=data,
            scratch_shapes=[stage((d, EXPERT_FF)), stage((d, EXPERT_FF)), stage((EXPERT_FF, d)),
                            pltpu.VMEM((d, EXPERT_FF), BF16), pltpu.VMEM((d, EXPERT_FF), BF16),
                            pltpu.VMEM((EXPERT_FF, d), BF16), pltpu.SemaphoreType.DMA((2, 3))]),
        out_shape=jax.ShapeDtypeStruct((parts, slots, SC_ROW), U32),
        compiler_params=_cparams(("arbitrary",)),
        name="experts",
    )(blk_e, blk_valid, blk_src, blk_next, blk_slot, xs, w_gate, w_up, w_down)


def _combine_kernel(yg_ref, gw_ref, h_ref, x_ref, gt_ref, wsg_ref, wsu_ref, wsd_ref, gf_ref, o_ref, *, final):
    h = h_ref[0]
    a = _dot(h, wsg_ref[...])
    hid = a * _sigmoid(a) * _dot(h, wsu_ref[...])
    y = _dot(hid.astype(BF16), wsd_ref[...])
    gw = gw_ref[...]
    y_lo = y[:, :D_MODEL // 2]
    y_hi = y[:, D_MODEL // 2:]
    for k in range(TOP_K):
        lo, hi = _unpack_rows(_load_parts(yg_ref, k))
        y_lo = y_lo + gw[:, k:k + 1] * lo
        y_hi = y_hi + gw[:, k:k + 1] * hi
    y = jnp.concatenate([y_lo, y_hi], axis=1)
    xn = x_ref[0] + gt_ref[0] * y
    if final:
        ms = jnp.mean(xn * xn, axis=-1, keepdims=True)
        xn = xn * lax.rsqrt(ms + RMS_EPS) * gf_ref[...]
    o_ref[0] = xn


def _combine(yg, gw, tok_off, h2, x, gt2, b0, nb, ws_gate, ws_up, ws_down, g_final, final, tm):
    b, s, d = x.shape
    per_batch = gt2.shape[0] == b
    mod_map = (lambda bi, i: (b0 + bi, 0, 0)) if per_batch else (lambda bi, i: (0, 0, 0))
    full = lambda arr: pl.BlockSpec(arr.shape, lambda bi, i: (0,) * arr.ndim)
    tok = lambda width: pl.BlockSpec((1, tm, width), lambda bi, i: (b0 + bi, i, 0))
    gf = g_final[None]
    nblk = s // tm
    blk0 = (tok_off + b0 * s) // tm
    return pl.pallas_call(
        functools.partial(_combine_kernel, final=final),
        grid=(nb, nblk),
        in_specs=[pl.BlockSpec((SC_PARTS, TOP_K, tm, SC_ROW), lambda bi, i: (0, 0, bi * nblk + i, 0)),
                  pl.BlockSpec((tm, TOP_K), lambda bi, i: (blk0 + bi * nblk + i, 0)),
                  tok(d), tok(d),
                  pl.BlockSpec((1, 1, d), mod_map),
                  full(ws_gate), full(ws_up), full(ws_down), full(gf)],
        out_specs=tok(d),
        out_shape=jax.ShapeDtypeStruct((b, s, d), F32),
        input_output_aliases={3: 0},
        compiler_params=_cparams(("parallel", "parallel")),
        name="combine",
    )(yg, gw, h2, x, gt2, ws_gate, ws_up, ws_down, gf)


def _project_latent(x, p, b0=0, nb=None, prev=None):
    w_main, w_gate = p['prep']['w_in']
    mods = p['prep']['mods']
    return _proj_in(x, p['g_norm1'], mods[0], mods[1], w_main, w_gate, tm=min(2048, x.shape[1]), tn=1024,
                    b0=b0, nb=nb, prev=prev)


def _layer(x, ctx_s, p, ctx_out, final, g_final, projected=None, after_piece=None, p_next=None):
    b, s, d = x.shape
    sc = ctx_s.shape[1]
    prep = p['prep']
    sh1, sc1, gt1, sh2, sc2, gt2 = prep['mods']
    csh1, csc1, cgt1, csh2, csc2, cgt2 = prep['mods_ctx']

    w_main, w_gate = prep['w_in']
    main, gate = _project_latent(x, p) if projected is None else projected
    ctx_flat = ctx_s.reshape(1, b * sc, d)
    n_ctx, tn_ctx = (N_MAIN, 1024) if ctx_out else (N_KV_MAIN, N_KV_MAIN // 2)
    main_c, gate_c = _proj_in(ctx_flat, p['g_norm1'], csh1, csc1, w_main, w_gate, tm=min(1024, b * sc),
                              tn=tn_ctx, n=n_ctx)
    main_c = main_c.reshape(b, sc, n_ctx)
    gate_c = gate_c.reshape(b, sc, LANE)

    o_na = _na_latent(main, main_c, prep['na_bias'])

    gate_ws = prep['gate_ws']
    s0 = jnp.zeros((b, GLA_HEADS, GLA_DK, GLA_DV), F32)
    o_cf, o_cb, st_f, st_b = _gla_scan(main_c, gate_c, gate_ws, (s0, s0), ctx_out)
    o_f, o_b, _, _ = _gla_scan(main, gate, gate_ws, (st_f, st_b), True)

    w_na, w_sc, w_gla, w_out, w_router = (prep[name] for name in ('w_na', 'w_sc', 'w_gla', 'w_out', 'w_router'))
    n_lat = b * s
    t = n_lat + (b * sc if ctx_out else 0)
    x, h2, hp_all, lg_all = _merge(o_na, p['conv_w'], o_f, o_b, main, x, gt1, p['gla_norm_g'], w_na, w_sc, w_gla,
                                   w_out, p['g_norm2'], sh2, sc2, w_router, tm=min(512, s), n_routed=t, tok_off=0)
    if ctx_out:
        o_na_c = _dense_attn(main_c)
        ctx_s, h2_c, hp_all, lg_all = _merge(o_na_c, p['conv_w'], o_cf, o_cb, main_c, ctx_s, cgt1, p['gla_norm_g'],
                                             w_na, w_sc, w_gla, w_out, p['g_norm2'], csh2, csc2, w_router,
                                             tm=min(256, sc), n_routed=t, tok_off=n_lat, routed=(hp_all, lg_all))

    eidx, gw, rank, counts = _route(lg_all, p['b_router'])
    padded = (counts + MOE_BLOCK - 1) // MOE_BLOCK * MOE_BLOCK
    pad_end = jnp.cumsum(padded)
    pad_start = pad_end - padded
    onehot = eidx[:, :, None] == jnp.arange(N_EXPERTS, dtype=jnp.int32)
    dest = jnp.sum(jnp.where(onehot, pad_start, 0), axis=-1) + rank
    n_blocks = -(-(t * TOP_K + N_EXPERTS * (MOE_BLOCK - 1)) // MOE_BLOCK)
    slots = n_blocks * MOE_BLOCK
    blk_start = jnp.arange(n_blocks, dtype=jnp.int32) * MOE_BLOCK
    blk_e = jnp.minimum(jnp.sum(pad_end[None, :] <= blk_start[:, None], axis=1), N_EXPERTS - 1).astype(jnp.int32)
    used_end = (pad_start + counts)[blk_e]
    blk_valid = jnp.clip(used_end - blk_start, 0, MOE_BLOCK).astype(jnp.int32)
    n_used = pad_end[-1] // MOE_BLOCK
    blk_src = jnp.minimum(jnp.arange(n_blocks, dtype=jnp.int32), n_used - 1)
    blk_e = blk_e[blk_src]
    ids = jnp.arange(N_EXPERTS, dtype=jnp.int32)
    used = counts > 0
    later_used = jnp.where(used[None, :] & (ids[None, :] > ids[:, None]), ids[None, :], N_EXPERTS).min(axis=1)
    next_used = jnp.where(later_used == N_EXPERTS, -1, later_used).astype(jnp.int32)
    blk_next = next_used[blk_e]
    blk_slot = ((jnp.cumsum(used) - 1) % 2).astype(jnp.int32)[blk_e]

    xs = _dispatch_rows(hp_all, dest, slots)
    if p_next is not None:
        xs, p_next['prep'] = lax.optimization_barrier((xs, p_next['prep']))
    ys = _experts(xs, blk_e, blk_valid, blk_src, blk_next, blk_slot, p['layer'], p['w_exp_gate'], p['w_exp_up'], p['w_exp_down'])
    gw_t = gw.T
    ws_gate, ws_up, ws_down = prep['ws_gate'], prep['ws_up'], prep['ws_down']

    def gathered(t0, n):
        return _gather_rows(ys, dest, t0, n)

    pieces = next(n for n in (4, 2, 1) if b % n == 0)
    nb = b // pieces
    for q in range(pieces):
        x = _combine(gathered(q * nb * s, nb * s), gw_t, 0, h2, x, gt2, q * nb, nb, ws_gate, ws_up, ws_down,
                     g_final, final, tm=min(512, s))
        if after_piece is not None:
            after_piece(x, q * nb, nb)
    if ctx_out:
        ctx_s = _combine(gathered(n_lat, b * sc), gw_t, n_lat, h2_c, ctx_s, cgt2, 0, b, ws_gate, ws_up, ws_down,
                         g_final, False, tm=min(256, sc))
    return x, ctx_s


def kernel(x, c, ctx, c_ctx, w_mod, b_mod, g_norm1, g_norm2, w_in, na_rpb, w_branch_na, conv_w, w_branch_sc,
           gla_gate_w, gla_gate_b, gla_norm_g, w_branch_gla, w_out, w_router, b_router, w_exp_gate, w_exp_up,
           w_exp_down, w_sh_gate, w_sh_up, w_sh_down, g_final):
    stacked = dict(g_norm1=g_norm1, g_norm2=g_norm2, na_rpb=na_rpb, w_branch_na=w_branch_na,
                   conv_w=conv_w, w_branch_sc=w_branch_sc, gla_gate_w=gla_gate_w, gla_gate_b=gla_gate_b,
                   gla_norm_g=gla_norm_g, w_branch_gla=w_branch_gla, w_out=w_out, w_router=w_router,
                   b_router=b_router,
                   w_sh_gate=w_sh_gate, w_sh_up=w_sh_up, w_sh_down=w_sh_down)
    depth = w_in.shape[0]
    rows = x.shape[1] // GRID_W
    layers = []
    for i in range(depth):
        p = {name: arr[i] for name, arr in stacked.items()}
        p.update(layer=i, w_exp_gate=w_exp_gate, w_exp_up=w_exp_up, w_exp_down=w_exp_down)
        mods, mods_ctx = _mod_vectors(c, c_ctx, w_mod, b_mod, i)
        p['prep'] = dict(
            w_in=_prep_w_in(w_in, i), mods=mods, mods_ctx=mods_ctx,
            na_bias=_na_bias_table(p['na_rpb'], rows, min(NA_WIN_R, rows)),
            gate_ws=_gla_gate_weights(p['gla_gate_w'], p['gla_gate_b']),
            w_na=p['w_branch_na'].astype(BF16), w_sc=p['w_branch_sc'].astype(BF16),
            w_gla=p['w_branch_gla'].astype(BF16), w_out=p['w_out'].astype(BF16),
            w_router=jnp.pad(p['w_router'], ((0, 0), (0, LANE - N_EXPERTS))).astype(BF16),
            ws_gate=p['w_sh_gate'].astype(BF16), ws_up=p['w_sh_up'].astype(BF16),
            ws_down=p['w_sh_down'].astype(BF16))
        layers.append(p)

    ctx_s = ctx
    projected = None
    for i, p in enumerate(layers):
        last = i == depth - 1
        p_next = None if last else layers[i + 1]
        after_piece = None
        next_projected = []
        if not last:
            def after_piece(xq, b0, nb, p_next=p_next, acc=next_projected):
                acc.append(_project_latent(xq, p_next, b0, nb, acc[-1] if acc else None))

        x, ctx_s = _layer(x, ctx_s, p, not last, last, g_final, projected, after_piece, p_next)
        projected = next_projected[-1] if next_projected else None
    return x
```

```python
import functools

import numpy as np
import jax
import jax.numpy as jnp
from jax import lax
from jax.experimental import pallas as pl
from jax.experimental.pallas import tpu as pltpu
from jax.experimental.pallas import tpu_sc as plsc

F32 = jnp.float32
BF16 = jnp.bfloat16
U32 = jnp.uint32

D_MODEL = 1024
N_MOD = 6
RMS_EPS = 1e-6
NEG_INF = -1e30
GRID_W = 64
NA_HEADS = 8
NA_HEAD_DIM = 64
NA_WIDTH = NA_HEADS * NA_HEAD_DIM
NA_WIN_R = 8
NA_WIN_C = 16
NA_GROUP = 4
SC_WIDTH = 512
GLA_HEADS = 4
GLA_KEY_WIDTH = 512
GLA_VAL_WIDTH = 1024
GLA_DK = GLA_KEY_WIDTH // GLA_HEADS
GLA_DV = GLA_VAL_WIDTH // GLA_HEADS
GLA_GATE_RANK = 16
GLA_GATE_TAU = 16.0
LOG2_E = 1.4426950408889634
N_EXPERTS = 64
N_EXPERT_GROUPS = 8
GROUP_SIZE = N_EXPERTS // N_EXPERT_GROUPS
TOPK_GROUPS = 4
TOP_K = 8
EXPERT_FF = 256
ROUTED_SCALE = 2.5
MOE_BLOCK = 1024

LANE = 128
GLA_C = 128
GLA_STEP_CHUNKS = 4
GLA_LEVELS = tuple(GLA_C >> (i + 1) for i in range(GLA_C.bit_length() - 1))
VMEM_LIMIT = 48 * 1024 * 1024
SC_WINDOW = 128
SC_ROW = 256
SC_PARTS = D_MODEL // 2 // SC_ROW

OFF_V_GLA = 0
OFF_K_NA = 1024
OFF_V_NA = 1536
OFF_K_GLA = 2048
N_KV_MAIN = 2560
OFF_Q_NA = 2560
OFF_B_SC = 3072
OFF_C_SC = 3584
OFF_X_SC = 4096
OFF_Q_GLA = 4608
OFF_R_GLA = 5120
OFF_MERGE = 6144
N_MAIN = 9216


def _cparams(sem, vmem=VMEM_LIMIT):
    return pltpu.CompilerParams(dimension_semantics=sem, vmem_limit_bytes=vmem)


def _dot(a, b):
    return jnp.dot(a, b, preferred_element_type=F32)


def _dot_nt(a, b):
    return lax.dot_general(a, b, (((1,), (1,)), ((), ())), preferred_element_type=F32)


def _dot_tn(a, b):
    return lax.dot_general(a, b, (((0,), (0,)), ((), ())), preferred_element_type=F32)


def _sigmoid(x):
    return 0.5 * jnp.tanh(0.5 * x) + 0.5


def _pack_rows(x):
    n = x.shape[1] // 2
    r = x.astype(BF16).astype(F32)
    lo = pltpu.bitcast(r[:, :n], U32) >> 16
    hi = pltpu.bitcast(r[:, n:], U32)
    return hi | lo


def _store_parts(ref, words):
    for part in range(SC_PARTS):
        dst = ref.at[part, 0] if len(ref.shape) == 4 else ref.at[part]
        dst[...] = words[:, part * SC_ROW:(part + 1) * SC_ROW]


def _load_parts(ref, *lead):
    return jnp.concatenate([ref[(part,) + lead] for part in range(SC_PARTS)], axis=-1)


def _unpack_rows(w):
    lo = pltpu.bitcast(w << 16, F32)
    hi = pltpu.bitcast(w & jnp.uint32(0xFFFF0000), F32)
    return lo, hi


def _mod_kernel(a_ref, w_ref, b_ref, o_ref):
    a = a_ref[...]
    a = a * _sigmoid(a)
    o_ref[...] = _dot(a.astype(BF16), w_ref[0].astype(BF16)) + b_ref[0]


def _mod_vectors(c, c_ctx, w_mod, b_mod, layer):
    b = c.shape[0]
    rows = -(-(b + 1) // 8) * 8
    a = jnp.concatenate([c, c_ctx[None], jnp.zeros((rows - b - 1, D_MODEL), F32)], axis=0)
    n = N_MOD * D_MODEL
    tn = 1536
    out = pl.pallas_call(
        _mod_kernel,
        grid=(n // tn,),
        in_specs=[pl.BlockSpec((rows, D_MODEL), lambda j: (0, 0)),
                  pl.BlockSpec((1, D_MODEL, tn), lambda j: (layer, 0, j)),
                  pl.BlockSpec((1, 1, tn), lambda j: (layer, 0, j))],
        out_specs=pl.BlockSpec((rows, tn), lambda j: (0, j)),
        out_shape=jax.ShapeDtypeStruct((rows, n), F32),
        compiler_params=_cparams(("parallel",)),
        name="mod_vectors",
    )(a, w_mod, b_mod[:, None])
    lat = out[:b].reshape(b, N_MOD, 1, D_MODEL)
    ctx = out[b].reshape(N_MOD, 1, 1, D_MODEL)
    return [lat[:, i] for i in range(N_MOD)], [ctx[i] for i in range(N_MOD)]


def _proj_kernel(x_ref, g_ref, sh_ref, sc_ref, w_ref, wg_ref, *rest):
    o_ref, og_ref, h_ref = rest[-3:]

    @pl.when(pl.program_id(2) == 0)
    def _():
        x = x_ref[0]
        ms = jnp.mean(x * x, axis=-1, keepdims=True)
        h = x * lax.rsqrt(ms + RMS_EPS) * g_ref[...] * (1.0 + sc_ref[0]) + sh_ref[0]
        hb = h.astype(BF16)
        h_ref[...] = hb
        og_ref[0] = _dot(hb, wg_ref[...])

    o_ref[0] = _dot(h_ref[...], w_ref[...]).astype(o_ref.dtype)


W_IN_TILE = 512
W_IN_GATE_SHIFT = 2 * GLA_GATE_RANK


def _prep_w_in_kernel(a_ref, b_ref, o_ref, g_ref):
    t = pl.program_id(0)
    first_lat = N_KV_MAIN // W_IN_TILE
    a = a_ref[0]

    @pl.when(t < first_lat)
    def _():
        o_ref[...] = a.T.astype(BF16)

    @pl.when(t >= first_lat)
    def _():
        moved = jnp.concatenate([a[W_IN_GATE_SHIFT:], b_ref[0]], axis=0)
        scale = jnp.where(t == first_lat, NA_HEAD_DIM ** -0.5, 1.0)
        o_ref[...] = (moved * scale).T.astype(BF16)

    @pl.when(t == first_lat)
    def _():
        head = a[:LANE]
        row = lax.broadcasted_iota(jnp.int32, head.shape, 0)
        g_ref[...] = jnp.where(row < W_IN_GATE_SHIFT, head, 0.0).T.astype(BF16)


def _prep_w_in(w_in, layer):
    d = w_in.shape[1]
    w_t = jnp.swapaxes(w_in, 1, 2)
    first_lat = N_KV_MAIN // W_IN_TILE
    kv_perm = OFF_K_NA // W_IN_TILE

    def a_map(t):
        return (layer, jnp.where(t < first_lat, (t + first_lat - kv_perm) % first_lat, t), 0)

    def b_map(t):
        return (layer, jnp.where(t < first_lat, 0, (t + 1) * (W_IN_TILE // W_IN_GATE_SHIFT)), 0)

    return pl.pallas_call(
        _prep_w_in_kernel,
        grid=(N_MAIN // W_IN_TILE,),
        in_specs=[pl.BlockSpec((1, W_IN_TILE, d), a_map),
                  pl.BlockSpec((1, W_IN_GATE_SHIFT, d), b_map)],
        out_specs=[pl.BlockSpec((d, W_IN_TILE), lambda t: (0, t)),
                   pl.BlockSpec((d, LANE), lambda t: (0, 0))],
        out_shape=[jax.ShapeDtypeStruct((d, N_MAIN), BF16), jax.ShapeDtypeStruct((d, LANE), BF16)],
        compiler_params=_cparams(("arbitrary",)),
        name="prep_w_in",
    )(w_t, w_t)


def _proj_in(x, g, shift, scale, w_main, w_gate, tm, tn, n=None, b0=0, nb=None, prev=None):
    b, s, d = x.shape
    n = w_main.shape[1] if n is None else n
    nb = b if nb is None else nb
    per_batch = shift.shape[0] == b
    mod_map = (lambda bi, i, j: (b0 + bi, 0, 0)) if per_batch else (lambda bi, i, j: (0, 0, 0))
    operands = (x, g[None], shift, scale, w_main, w_gate)
    extend = () if prev is None else tuple(prev)
    return pl.pallas_call(
        _proj_kernel,
        grid=(nb, s // tm, n // tn),
        in_specs=[pl.BlockSpec((1, tm, d), lambda bi, i, j: (b0 + bi, i, 0)),
                  pl.BlockSpec((1, d), lambda bi, i, j: (0, 0)),
                  pl.BlockSpec((1, 1, d), mod_map),
                  pl.BlockSpec((1, 1, d), mod_map),
                  pl.BlockSpec((d, tn), lambda bi, i, j: (0, j)),
                  pl.BlockSpec((d, LANE), lambda bi, i, j: (0, 0))]
                 + [pl.BlockSpec(memory_space=pl.ANY)] * len(extend),
        out_specs=[pl.BlockSpec((1, tm, tn), lambda bi, i, j: (b0 + bi, i, j)),
                   pl.BlockSpec((1, tm, LANE), lambda bi, i, j: (b0 + bi, i, 0))],
        out_shape=[jax.ShapeDtypeStruct((b, s, n), BF16),
                   jax.ShapeDtypeStruct((b, s, LANE), F32)],
        scratch_shapes=[pltpu.VMEM((tm, d), BF16)],
        input_output_aliases={len(operands): 0, len(operands) + 1: 1} if extend else {},
        compiler_params=_cparams(("parallel", "parallel", "arbitrary")),
        name="proj_in",
    )(*operands, *extend)


def _softmax_av(q, keys, vals, biases):
    scores = []
    for kk, bb in zip(keys, biases):
        s = _dot_nt(q, kk)
        scores.append(s if bb is None else s + bb)
    m = scores[0].max(axis=-1, keepdims=True)
    for s in scores[1:]:
        m = jnp.maximum(m, s.max(axis=-1, keepdims=True))
    num = None
    den = None
    for s, vv in zip(scores, vals):
        e = jnp.exp(s - m)
        dsum = e.sum(axis=-1, keepdims=True)
        o = _dot(e.astype(BF16), vv)
        num = o if num is None else num + o
        den = dsum if den is None else den + dsum
    return num / den


def _na_kernel(q_ref, k_ref, v_ref, kc_ref, vc_ref, *rest, rows, kr):
    *bias_refs, o_ref = rest
    kc = kc_ref[0]
    vc = vc_ref[0]
    for j, bias_ref in enumerate(bias_refs):
        r = pl.program_id(1) * len(bias_refs) + j
        row_start = jnp.clip(r - kr // 2, 0, rows - kr)
        start = pl.multiple_of(row_start * GRID_W, GRID_W)
        n_win = kr * GRID_W
        q = q_ref[0, j * GRID_W:(j + 1) * GRID_W, :]
        kw = k_ref[0, pl.ds(start, n_win), :]
        vw = v_ref[0, pl.ds(start, n_win), :]
        o_ref[0, j * GRID_W:(j + 1) * GRID_W, :] = _na_row(q, kw, vw, kc, vc, bias_ref).astype(o_ref.dtype)


def _na_row(q, kw, vw, kc, vc, bias_ref):
    gw = NA_GROUP * NA_HEAD_DIM
    stacked = (NA_GROUP * GRID_W, gw)
    on_head = (lax.broadcasted_iota(jnp.int32, stacked, 0) // GRID_W
               == lax.broadcasted_iota(jnp.int32, stacked, 1) // NA_HEAD_DIM)
    outs = []
    for g in range(NA_HEADS // NA_GROUP):
        sl = slice(g * gw, (g + 1) * gw)
        q_all = jnp.where(on_head, jnp.concatenate([q[:, sl]] * NA_GROUP, axis=0), jnp.zeros((), q.dtype))
        bias = bias_ref[0, g * NA_GROUP * GRID_W:(g + 1) * NA_GROUP * GRID_W, :]
        o_all = _softmax_av(q_all, [kw[:, sl], kc[:, sl]], [vw[:, sl], vc[:, sl]], [bias, None])
        o_all = jnp.where(on_head, o_all, 0.0).reshape(NA_GROUP, GRID_W, gw)
        outs.append(o_all.sum(axis=0))
    return jnp.concatenate(outs, axis=-1)


def _na_bias_table(rpb, rows, kr):
    col = np.arange(GRID_W)
    col_start = np.clip(col - NA_WIN_C // 2, 0, GRID_W - NA_WIN_C)
    col_ok = (col[None, :] >= col_start[:, None]) & (col[None, :] < col_start[:, None] + NA_WIN_C)
    d_col = np.clip(col[None, :] - col[:, None], -(NA_WIN_C - 1), NA_WIN_C - 1) + NA_WIN_C - 1
    n_dr, n_dc = rpb.shape[1], rpb.shape[2]
    onehot = jnp.asarray((d_col.reshape(-1)[None, :] == np.arange(n_dc)[:, None]).astype(np.float32))
    by_col = jnp.dot(rpb.astype(F32).reshape(NA_HEADS * n_dr, n_dc), onehot, precision=lax.Precision.HIGHEST)
    by_col = by_col.reshape(NA_HEADS, n_dr, GRID_W, GRID_W)
    by_col = jnp.where(col_ok[None, None], by_col, NEG_INF)
    tables = []
    for o in range(kr):
        lo = NA_WIN_R - 1 - o
        tables.append(by_col[:, lo:lo + kr].transpose(0, 2, 1, 3).reshape(NA_HEADS, GRID_W, kr * GRID_W))
    return jnp.stack(tables).reshape(kr, NA_HEADS * GRID_W, kr * GRID_W)


def _na_latent(main, main_ctx, bias):
    b, s, _ = main.shape
    sc = main_ctx.shape[1]
    rows = s // GRID_W
    kr = min(NA_WIN_R, rows)
    w = NA_WIDTH

    per_step = next(n for n in (8, 4, 2, 1) if rows % n == 0)

    def bias_spec(j):
        def bias_map(bi, i):
            r = i * per_step + j
            return (r - jnp.clip(r - kr // 2, 0, rows - kr), 0, 0)
        return pl.BlockSpec((1, NA_HEADS * GRID_W, kr * GRID_W), bias_map)

    return pl.pallas_call(
        functools.partial(_na_kernel, rows=rows, kr=kr),
        grid=(b, rows // per_step),
        in_specs=[pl.BlockSpec((1, per_step * GRID_W, w), lambda bi, i: (bi, i, OFF_Q_NA // w)),
                  pl.BlockSpec((1, s, w), lambda bi, i: (bi, 0, OFF_K_NA // w)),
                  pl.BlockSpec((1, s, w), lambda bi, i: (bi, 0, OFF_V_NA // w)),
                  pl.BlockSpec((1, sc, w), lambda bi, i: (bi, 0, OFF_K_NA // w)),
                  pl.BlockSpec((1, sc, w), lambda bi, i: (bi, 0, OFF_V_NA // w))]
                 + [bias_spec(j) for j in range(per_step)],
        out_specs=pl.BlockSpec((1, per_step * GRID_W, w), lambda bi, i: (bi, i, 0)),
        out_shape=jax.ShapeDtypeStruct((b, s, w), BF16),
        compiler_params=_cparams(("parallel", "arbitrary")),
        name="na_latent",
    )(main, main, main, main_ctx, main_ctx, *([bias] * per_step))


def _dense_attn_kernel(q_ref, k_ref, v_ref, o_ref):
    q = q_ref[0]
    k = k_ref[0]
    v = v_ref[0]
    outs = []
    for h in range(NA_HEADS):
        sl = slice(h * NA_HEAD_DIM, (h + 1) * NA_HEAD_DIM)
        outs.append(_softmax_av(q[:, sl], [k[:, sl]], [v[:, sl]], [None]))
    o_ref[0] = jnp.concatenate(outs, axis=-1).astype(o_ref.dtype)


def _dense_attn(main_ctx):
    b, sc, _ = main_ctx.shape
    w = NA_WIDTH
    return pl.pallas_call(
        _dense_attn_kernel,
        grid=(b,),
        in_specs=[pl.BlockSpec((1, sc, w), lambda bi: (bi, 0, OFF_Q_NA // w)),
                  pl.BlockSpec((1, sc, w), lambda bi: (bi, 0, OFF_K_NA // w)),
                  pl.BlockSpec((1, sc, w), lambda bi: (bi, 0, OFF_V_NA // w))],
        out_specs=pl.BlockSpec((1, sc, w), lambda bi: (bi, 0, 0)),
        out_shape=jax.ShapeDtypeStruct((b, sc, w), BF16),
        compiler_params=_cparams(("parallel",)),
        name="ctx_attn",
    )(main_ctx, main_ctx, main_ctx)


HALO = 16


def _gla_matrices(reverse):
    c = GLA_C
    t = np.arange(c)[:, None]
    m = np.arange(c)[None, :]
    blocks = [m <= t, m > t]
    for b in GLA_LEVELS:
        first = (t // (2 * b)) * (2 * b) + b
        is_q = (t & b) != 0
        blocks.append(np.where(is_q, (m > first) & (m <= t), (m > t) & (m <= first)))
    mats = np.stack(blocks).astype(np.float32)
    if reverse:
        mats = mats[:, ::-1, ::-1]
    mats = mats.reshape(-1, c)
    return jnp.asarray(np.concatenate([mats, mats], axis=1), dtype=BF16)


def _split_bf16(x):
    hi = x.astype(BF16)
    return hi, (x - hi.astype(F32)).astype(BF16)


def _pair_block_diag(x):
    lane = lax.broadcasted_iota(jnp.int32, x.shape, 1)
    zero = jnp.zeros((), x.dtype)
    return jnp.concatenate([jnp.where(lane < GLA_DK, x, zero), jnp.where(lane >= GLA_DK, x, zero)], axis=0)


def _gla_kernel(*refs, emit):
    n_in = 9 if emit else 8
    n_out = 2 if emit else 1
    ins, outs, scratch = refs[:2 * n_in], refs[2 * n_in:2 * (n_in + n_out)], refs[2 * (n_in + n_out):]
    ins = [ins[d * n_in:(d + 1) * n_in] for d in range(2)]
    outs = [outs[d * n_out:(d + 1) * n_out] for d in range(2)]
    step = pl.program_id(1)

    @pl.when(step == 0)
    def _():
        for d in range(2):
            scratch[d][...] = ins[d][-1][0]

    pending = [_gla_direction(ins[d][:-1], outs[d][:-1], scratch[d], reverse, emit)
               for d, reverse in enumerate((False, True))]
    while pending:
        pending = [stages for stages in pending if next(stages, "done") != "done"]

    @pl.when(step == pl.num_programs(1) - 1)
    def _():
        for d in range(2):
            outs[d][-1][0] = scratch[d][...]


def _gla_direction(ins, outs, st_ref, reverse, emit):
    gate_ref = ins[-5]
    order = range(gate_ref.shape[1] // GLA_C)
    for sub in (reversed(order) if reverse else order):
        yield from _gla_chunk(ins, outs, st_ref, reverse, emit, slice(sub * GLA_C, (sub + 1) * GLA_C))


def _gla_chunk(ins, outs, st_ref, reverse, emit, rows):
    if emit:
        q_ref, k_ref, v_ref, gt_ref, w2a_ref, w2b_ref, b2_ref, a_ref = ins
        (o_ref,) = outs
    else:
        k_ref, v_ref, gt_ref, w2a_ref, w2b_ref, b2_ref, a_ref = ins
    c = GLA_C
    pw = 2 * GLA_DK

    lr_hi, lr_lo = _split_bf16(gt_ref[0, rows])
    logit = (_dot(jnp.concatenate([lr_hi, lr_lo], axis=1), w2a_ref[...]) + _dot(lr_hi, w2b_ref[...])
             + b2_ref[...])
    g = (jnp.minimum(logit, 0.0) - jnp.log1p(jnp.exp(-jnp.abs(logit)))) * (LOG2_E / GLA_GATE_TAU)
    g_hi, g_lo = _split_bf16(g)
    args = _dot(a_ref[...], jnp.concatenate([g_hi, g_lo], axis=0))
    cum = args[0:c]
    rem = args[c:2 * c]
    last_row = cum[0:1] if reverse else cum[c - 1:c]
    yield

    k = k_ref[0, rows].astype(F32)
    v = v_ref[0, rows]
    atts = []
    if emit:
        q = q_ref[0, rows].astype(F32) * (GLA_DK ** -0.5)
        row_t = lax.broadcasted_iota(jnp.int32, (c, pw), 0)
        si = lax.broadcasted_iota(jnp.int32, (2 * c, c), 0) & (c - 1)
        ti = lax.broadcasted_iota(jnp.int32, (2 * c, c), 1)
        if reverse:
            row_t, ti, si = c - 1 - row_t, c - 1 - ti, c - 1 - si
        for hp in range(GLA_HEADS // 2):
            cs = slice(hp * pw, (hp + 1) * pw)
            qp, kp = q[:, cs], k[:, cs]
            att = jnp.where(ti == si, _dot_nt(_pair_block_diag(kp.astype(BF16)), qp.astype(BF16)), 0.0)
            for l, b in enumerate(GLA_LEVELS):
                x = (jnp.exp2(args[(2 + l) * c:(3 + l) * c, cs])
                     * jnp.where((row_t & b) != 0, qp, kp)).astype(BF16)
                pair = (((ti ^ si) >> (b.bit_length() - 1)) == 1) & ((ti & b) != 0)
                att = jnp.where(pair, _dot_nt(_pair_block_diag(x), x), att)
                yield
            atts.append(att.astype(BF16))

    outs = []
    for h in range(GLA_HEADS):
        sl = slice(h * GLA_DK, (h + 1) * GLA_DK)
        kh = k[:, sl]
        vh = v[:, h * GLA_DV:(h + 1) * GLA_DV]
        state = st_ref[h]
        kd = (kh * jnp.exp2(rem[:, sl])).astype(BF16)
        decay = jnp.exp2(jnp.broadcast_to(last_row[:, sl], (GLA_DK, GLA_DK))).T
        decay = jnp.concatenate([decay] * (GLA_DV // GLA_DK), axis=1)
        if emit:
            qd = (q[:, sl] * jnp.exp2(cum[:, sl])).astype(BF16)
            att_t = atts[h // 2][(h % 2) * c:(h % 2 + 1) * c]
            both = _dot_tn(jnp.concatenate([att_t, kd], axis=1), vh)
            outs.append(_dot(qd, state.astype(BF16)) + both[:c])
            st_ref[h] = decay * state + both[c:]
        else:
            st_ref[h] = decay * state + _dot_tn(kd, vh)
        yield

    if emit:
        o_ref[0, rows] = jnp.concatenate(outs, axis=-1).astype(o_ref.dtype)


def _gla_scan(main, gate, gate_ws, s0s, emit):
    b, l, _ = main.shape
    step_rows = GLA_C * min(GLA_STEP_CHUNKS, l // GLA_C)
    n = l // step_rows
    const = lambda arr: pl.BlockSpec(arr.shape, lambda bi, s: (0,) * arr.ndim)
    state_spec = pl.BlockSpec((1, GLA_HEADS, GLA_DK, GLA_DV), lambda bi, s: (bi, 0, 0, 0))
    state_shape = jax.ShapeDtypeStruct((b, GLA_HEADS, GLA_DK, GLA_DV), F32)
    in_specs, args, out_specs, out_shape = [], [], [], []
    for reverse in (False, True):
        amat = _gla_matrices(reverse)
        w2a, w2b, b2 = gate_ws[reverse]

        def col(block, reverse=reverse):
            return lambda bi, s: (bi, n - 1 - s if reverse else s, block)

        if emit:
            in_specs.append(pl.BlockSpec((1, step_rows, GLA_KEY_WIDTH), col(OFF_Q_GLA // GLA_KEY_WIDTH)))
            args.append(main)
            out_specs.append(pl.BlockSpec((1, step_rows, GLA_VAL_WIDTH), col(0)))
            out_shape.append(jax.ShapeDtypeStruct((b, l, GLA_VAL_WIDTH), BF16))
        in_specs += [pl.BlockSpec((1, step_rows, GLA_KEY_WIDTH), col(OFF_K_GLA // GLA_KEY_WIDTH)),
                     pl.BlockSpec((1, step_rows, GLA_VAL_WIDTH), col(OFF_V_GLA // GLA_VAL_WIDTH)),
                     pl.BlockSpec((1, step_rows, LANE), col(0)),
                     const(w2a), const(w2b), const(b2), const(amat), state_spec]
        args += [main, main, gate, w2a, w2b, b2, amat, s0s[reverse]]
        out_specs.append(state_spec)
        out_shape.append(state_shape)
    res = pl.pallas_call(
        functools.partial(_gla_kernel, emit=emit),
        grid=(b, n),
        in_specs=in_specs,
        out_specs=out_specs,
        out_shape=out_shape,
        scratch_shapes=[pltpu.VMEM((GLA_HEADS, GLA_DK, GLA_DV), F32)] * 2,
        compiler_params=_cparams(("parallel", "arbitrary")),
        name="gla_scan",
    )(*args)
    return (res[0], res[2], res[1], res[3]) if emit else (None, None, res[0], res[1])


def _gla_gate_weights(gate_w, gate_b):
    out = []
    for dr in range(2):
        w = jnp.zeros((LANE, GLA_KEY_WIDTH), F32)
        w = w.at[dr * GLA_GATE_RANK:(dr + 1) * GLA_GATE_RANK].set(gate_w[dr])
        w_hi = w.astype(BF16)
        w_lo = (w - w_hi.astype(F32)).astype(BF16)
        out.append((jnp.concatenate([w_hi, w_hi], axis=0), w_lo, gate_b[dr][None]))
    return out


def _conv_tile(b_ref, c_ref, x_ref, cp_ref, xp_ref, cn_ref, xn_ref, w_ref):
    i, n = pl.program_id(1), pl.num_programs(1)
    u = c_ref[0].astype(F32) * x_ref[0].astype(F32)
    tm = u.shape[0]
    before = jnp.where(i > 0, 1.0, 0.0) * (cp_ref[0, HALO - 1:HALO].astype(F32) * xp_ref[0, HALO - 1:HALO].astype(F32))
    after = jnp.where(i < n - 1, 1.0, 0.0) * (cn_ref[0, 0:1].astype(F32) * xn_ref[0, 0:1].astype(F32))
    t = lax.broadcasted_iota(jnp.int32, u.shape, 0)
    prev = jnp.where(t == 0, before, pltpu.roll(u, 1, axis=0))
    nxt = jnp.where(t == tm - 1, after, pltpu.roll(u, tm - 1, axis=0))
    w = w_ref[...]
    return b_ref[0].astype(F32) * (prev * w[0:1] + u * w[1:2] + nxt * w[2:3])


def _merge_kernel(ona_ref, bsc_ref, csc_ref, xsc_ref, cp_ref, xp_ref, cn_ref, xn_ref, cw_ref, of_ref, ob_ref,
                  r_ref, gna_ref, gsc_ref, ggl_ref, x_ref, gt_ref,
                  gn_ref, wna_ref, wsc_ref, wgl_ref, wo_ref, g2_ref, sh2_ref, sc2_ref, wr_ref, *rest):
    xo_ref, hp_ref, lg_ref = rest[-3:]
    o_sc = _conv_tile(bsc_ref, csc_ref, xsc_ref, cp_ref, xp_ref, cn_ref, xn_ref, cw_ref).astype(BF16)
    o = of_ref[0].astype(F32) + ob_ref[0].astype(F32)
    normed = []
    for h in range(GLA_HEADS):
        oh = o[:, h * GLA_DV:(h + 1) * GLA_DV]
        ms = jnp.mean(oh * oh, axis=-1, keepdims=True)
        normed.append(oh * lax.rsqrt(ms + RMS_EPS))
    r = r_ref[0].astype(F32)
    y_gla = jnp.concatenate(normed, axis=-1) * gn_ref[...] * (r * _sigmoid(r))
    y = (_sigmoid(gna_ref[0].astype(F32)) * _dot(ona_ref[0], wna_ref[...])
         + _sigmoid(gsc_ref[0].astype(F32)) * _dot(o_sc, wsc_ref[...])
         + _sigmoid(ggl_ref[0].astype(F32)) * _dot(y_gla.astype(BF16), wgl_ref[...]))
    xn = x_ref[0] + gt_ref[0] * _dot(y.astype(BF16), wo_ref[...])
    xo_ref[0] = xn
    ms = jnp.mean(xn * xn, axis=-1, keepdims=True)
    h2 = xn * lax.rsqrt(ms + RMS_EPS) * g2_ref[...] * (1.0 + sc2_ref[0]) + sh2_ref[0]
    _store_parts(hp_ref, _pack_rows(h2))
    lg_ref[...] = _dot(h2.astype(BF16), wr_ref[...])


def _merge(o_na, conv_w, o_f, o_b, main, x, gt1, gn, w_na, w_sc, w_gla, w_out, g2, sh2, sc2, w_router, tm,
           n_routed, tok_off, routed=None):
    b, s, d = x.shape
    per_batch = gt1.shape[0] == b
    mod_map = (lambda bi, i: (bi, 0, 0)) if per_batch else (lambda bi, i: (0, 0, 0))
    tok = lambda width, blk: pl.BlockSpec((1, tm, width), lambda bi, i: (bi, i, blk))
    full = lambda arr: pl.BlockSpec(arr.shape, lambda bi, i: (0,) * arr.ndim)
    mod = pl.BlockSpec((1, 1, d), mod_map)
    gn_t = jnp.tile(gn, GLA_HEADS)[None]
    g2_t = g2[None]
    per_tile = tm // HALO
    last_halo = s // HALO - 1
    halo_prev = lambda blk: pl.BlockSpec(
        (1, HALO, SC_WIDTH), lambda bi, i: (bi, jnp.maximum(i * per_tile - 1, 0), blk))
    halo_next = lambda blk: pl.BlockSpec(
        (1, HALO, SC_WIDTH), lambda bi, i: (bi, jnp.minimum((i + 1) * per_tile, last_halo), blk))
    col_b, col_c, col_x = OFF_B_SC // SC_WIDTH, OFF_C_SC // SC_WIDTH, OFF_X_SC // SC_WIDTH
    extend = () if routed is None else tuple(routed)
    operands = (o_na, main, main, main, main, main, main, main, conv_w, o_f, o_b, main, main, main, main, x, gt1,
                gn_t, w_na, w_sc, w_gla, w_out, g2_t, sh2, sc2, w_router)
    n_in = len(operands)
    nblk = s // tm
    blk0 = tok_off // tm
    return pl.pallas_call(
        _merge_kernel,
        grid=(b, s // tm),
        in_specs=[tok(NA_WIDTH, 0), tok(SC_WIDTH, col_b), tok(SC_WIDTH, col_c), tok(SC_WIDTH, col_x),
                  halo_prev(col_c), halo_prev(col_x), halo_next(col_c), halo_next(col_x), full(conv_w),
                  tok(GLA_VAL_WIDTH, 0), tok(GLA_VAL_WIDTH, 0),
                  tok(d, OFF_R_GLA // d), tok(d, OFF_MERGE // d), tok(d, OFF_MERGE // d + 1),
                  tok(d, OFF_MERGE // d + 2), tok(d, 0), mod,
                  full(gn_t), full(w_na), full(w_sc), full(w_gla), full(w_out), full(g2_t), mod, mod,
                  full(w_router)] + [pl.BlockSpec(memory_space=pl.ANY)] * len(extend),
        out_specs=[tok(d, 0),
                   pl.BlockSpec((SC_PARTS, tm, SC_ROW), lambda bi, i: (0, blk0 + bi * nblk + i, 0)),
                   pl.BlockSpec((tm, LANE), lambda bi, i: (blk0 + bi * nblk + i, 0))],
        out_shape=[jax.ShapeDtypeStruct((b, s, d), F32),
                   jax.ShapeDtypeStruct((SC_PARTS, n_routed, SC_ROW), U32),
                   jax.ShapeDtypeStruct((n_routed, LANE), F32)],
        input_output_aliases={n_in: 1, n_in + 1: 2} if extend else {},
        compiler_params=_cparams(("parallel", "parallel")),
        name="merge",
    )(*operands, *extend)


def _router_kernel(lg_ref, br_ref, tri_ref, eidx_ref, gw_ref, rank_ref, cnt_ref, carry_ref):
    tm = lg_ref.shape[0]

    @pl.when(pl.program_id(0) == 0)
    def _():
        carry_ref[...] = jnp.zeros_like(carry_ref)

    scores = _sigmoid(lg_ref[...].T[:N_EXPERTS])
    sel = scores + br_ref[...]
    neg = -jnp.inf

    sel3 = sel.reshape(N_EXPERT_GROUPS, GROUP_SIZE, tm)
    i3 = lax.broadcasted_iota(jnp.int32, sel3.shape, 1)
    m1 = sel3.max(axis=1, keepdims=True)
    first = jnp.where(sel3 == m1, i3, GROUP_SIZE).min(axis=1, keepdims=True)
    m2 = jnp.where(i3 == first, neg, sel3).max(axis=1, keepdims=True)
    gscore = (m1 + m2)[:, 0, :]

    gi = lax.broadcasted_iota(jnp.int32, gscore.shape, 0)
    gmask = jnp.zeros(gscore.shape, jnp.bool_)
    for _ in range(TOPK_GROUPS):
        m = gscore.max(axis=0, keepdims=True)
        pick = gi == jnp.where(gscore == m, gi, N_EXPERT_GROUPS).min(axis=0, keepdims=True)
        gmask = gmask | pick
        gscore = jnp.where(pick, neg, gscore)
    emask = jnp.broadcast_to(gmask[:, None, :], sel3.shape).reshape(N_EXPERTS, tm)
    sel = jnp.where(emask, sel, neg)

    ei = lax.broadcasted_iota(jnp.int32, sel.shape, 0)
    picks, idxs, ws = [], [], []
    for _ in range(TOP_K):
        m = sel.max(axis=0, keepdims=True)
        idx = jnp.where(sel == m, ei, N_EXPERTS).min(axis=0, keepdims=True)
        pick = ei == idx
        picks.append(pick)
        idxs.append(idx)
        ws.append(jnp.where(pick, scores, 0.0).sum(axis=0, keepdims=True))
        sel = jnp.where(pick, neg, sel)
    w = jnp.concatenate(ws, axis=0)
    gw_ref[...] = w / w.sum(axis=0, keepdims=True) * ROUTED_SCALE
    eidx_ref[...] = jnp.concatenate(idxs, axis=0)

    onehot = picks[0]
    for p in picks[1:]:
        onehot = onehot | p
    onehot = jnp.where(onehot, 1.0, 0.0).astype(BF16)
    before = _dot(onehot, tri_ref[...]) + jnp.tile(carry_ref[...], (1, tm // LANE))
    rank_ref[...] = jnp.concatenate(
        [jnp.where(p, before, 0.0).sum(axis=0, keepdims=True) for p in picks], axis=0).astype(jnp.int32)
    carry_ref[...] += _dot(onehot, jnp.ones((tm, LANE), BF16))
    cnt_ref[...] = carry_ref[...]


def _route(logits, b_router):
    t = logits.shape[0]
    tm = next(n for n in (1024, 512) if t % n == 0)
    br = jnp.broadcast_to(b_router.astype(F32)[:, None], (N_EXPERTS, tm))
    tri = jnp.asarray(np.triu(np.ones((tm, tm), np.float32), 1), dtype=BF16)
    kt = lambda dt: jax.ShapeDtypeStruct((TOP_K, t), dt)
    eidx, gw, rank, cnt = pl.pallas_call(
        _router_kernel,
        grid=(t // tm,),
        in_specs=[pl.BlockSpec((tm, LANE), lambda i: (i, 0)),
                  pl.BlockSpec((N_EXPERTS, tm), lambda i: (0, 0)),
                  pl.BlockSpec((tm, tm), lambda i: (0, 0))],
        out_specs=[pl.BlockSpec((TOP_K, tm), lambda i: (0, i)),
                   pl.BlockSpec((TOP_K, tm), lambda i: (0, i)),
                   pl.BlockSpec((TOP_K, tm), lambda i: (0, i)),
                   pl.BlockSpec((N_EXPERTS, LANE), lambda i: (0, 0))],
        out_shape=[kt(jnp.int32), kt(F32), kt(jnp.int32),
                   jax.ShapeDtypeStruct((N_EXPERTS, LANE), F32)],
        scratch_shapes=[pltpu.VMEM((N_EXPERTS, LANE), F32)],
        compiler_params=_cparams(("arbitrary",)),
        name="router",
    )(logits, br, tri)
    return eidx, gw, rank, cnt[:, 0].astype(jnp.int32)


def _sc_mesh():
    return plsc.VectorSubcoreMesh(core_axis_name="core", subcore_axis_name="subcore")


def _dispatch_rows(xp, dest, slots):
    parts, t, _ = xp.shape

    @pl.kernel(out_type=jax.ShapeDtypeStruct((parts, slots, SC_ROW), xp.dtype), mesh=_sc_mesh(),
               scratch_types=[], name="moe_dispatch")
    def run(x_hbm, d_hbm, o_hbm):
        for part in range(parts):
            out_part = o_hbm.at[part]

            def body(x_vmem, d_vmem, out_part=out_part):
                for k in range(TOP_K):
                    pltpu.sync_copy(x_vmem, out_part.at[d_vmem.at[k]])

            pltpu.emit_pipeline(
                body,
                grid=(t // SC_WINDOW,),
                in_specs=[pl.BlockSpec((SC_WINDOW, SC_ROW), lambda i: (i, 0)),
                          pl.BlockSpec((TOP_K, SC_WINDOW), lambda i: (0, i))],
                out_specs=[],
                core_axis_name=("core", "subcore"),
                dimension_semantics=(pltpu.PARALLEL,),
            )(x_hbm.at[part], d_hbm)

    return run(xp, dest)


def _gather_rows(yp, dest, t0, n):
    parts = yp.shape[0]
    nwin = n // SC_WINDOW
    win0 = t0 // SC_WINDOW

    @pl.kernel(out_type=jax.ShapeDtypeStruct((parts, TOP_K * n, SC_ROW), yp.dtype), mesh=_sc_mesh(),
               scratch_types=[], name="moe_gather")
    def run(y_hbm, d_hbm, o_hbm):
        for part in range(parts):
            table = y_hbm.at[part]

            def body(d_vmem, o_vmem, table=table):
                pltpu.sync_copy(table.at[d_vmem.at[0]], o_vmem)

            pltpu.emit_pipeline(
                body,
                grid=(TOP_K, nwin),
                in_specs=[pl.BlockSpec((1, SC_WINDOW), lambda k, j: (k, win0 + j))],
                out_specs=[pl.BlockSpec((SC_WINDOW, SC_ROW), lambda k, j: (k * nwin + j, 0))],
                core_axis_name=("core", "subcore"),
                dimension_semantics=(pltpu.PARALLEL, pltpu.PARALLEL),
            )(d_hbm, o_hbm.at[part])

    return run(yp, dest).reshape(parts, TOP_K, n, SC_ROW)


def _expert_kernel(be_ref, bv_ref, bs_ref, nx_ref, sl_ref, x_ref, wg_hbm, wu_hbm, wd_hbm, o_ref,
                   wg_f, wu_f, wd_f, wg_s, wu_s, wd_s, sems, *, layer):
    i = pl.program_id(0)
    valid = bv_ref[i]
    expert = be_ref[i]
    new_expert = (i == 0) | (expert != be_ref[jnp.maximum(i - 1, 0)])
    slot = sl_ref[i]

    def fetch(which, into):
        return [pltpu.make_async_copy(src.at[layer, which], dst.at[into], sems.at[into, j])
                for j, (src, dst) in enumerate(((wg_hbm, wg_f), (wu_hbm, wu_f), (wd_hbm, wd_f)))]

    @pl.when(i == 0)
    def _():
        for cp in fetch(expert, slot):
            cp.start()

    @pl.when(new_expert)
    def _():
        for cp in fetch(expert, slot):
            cp.wait()
        upcoming = nx_ref[i]

        @pl.when(upcoming >= 0)
        def _():
            for cp in fetch(upcoming, 1 - slot):
                cp.start()

        wg_s[...] = wg_f[slot].astype(BF16)
        wu_s[...] = wu_f[slot].astype(BF16)
        wd_s[...] = wd_f[slot].astype(BF16)

    @pl.when(valid > 0)
    def _():
        w = _load_parts(x_ref)
        row = lax.broadcasted_iota(jnp.int32, w.shape, 0)
        w = jnp.where(row < valid, w, jnp.uint32(0))
        lo, hi = _unpack_rows(w)
        x = jnp.concatenate([lo, hi], axis=1).astype(BF16)
        a = _dot(x, wg_s[...])
        hid = a * _sigmoid(a) * _dot(x, wu_s[...])
        _store_parts(o_ref, _pack_rows(_dot(hid.astype(BF16), wd_s[...])))


def _experts(xs, blk_e, blk_valid, blk_src, blk_next, blk_slot, layer, w_gate, w_up, w_down):
    parts, slots, _ = xs.shape
    d = D_MODEL
    nb = slots // MOE_BLOCK
    data = pl.BlockSpec((parts, MOE_BLOCK, SC_ROW), lambda i, be, bv, bs, nx, sl: (0, bs[i], 0))
    stage = lambda shape: pltpu.VMEM((2,) + shape, F32)
    return pl.pallas_call(
        functools.partial(_expert_kernel, layer=layer),
        grid_spec=pltpu.PrefetchScalarGridSpec(
            num_scalar_prefetch=5,
            grid=(nb,),
            in_specs=[data] + [pl.BlockSpec(memory_space=pl.ANY)] * 3,
            out_specs=data,
            scratch_shapes=[stage((d, EXPERT_FF)), stage((d, EXPERT_FF)), stage((EXPERT_FF, d)),
                            pltpu.VMEM((d, EXPERT_FF), BF16), pltpu.VMEM((d, EXPERT_FF), BF16),
                            pltpu.VMEM((EXPERT_FF, d), BF16), pltpu.SemaphoreType.DMA((2, 3))]),
        out_shape=jax.ShapeDtypeStruct((parts, slots, SC_ROW), U32),
        compiler_params=_cparams(("arbitrary",)),
        name="experts",
    )(blk_e, blk_valid, blk_src, blk_next, blk_slot, xs, w_gate, w_up, w_down)


def _combine_kernel(yg_ref, gw_ref, hp_ref, x_ref, gt_ref, wsg_ref, wsu_ref, wsd_ref, gf_ref, o_ref, *, final):
    h = jnp.concatenate(_unpack_rows(_load_parts(hp_ref)), axis=1).astype(BF16)
    a = _dot(h, wsg_ref[...])
    hid = a * _sigmoid(a) * _dot(h, wsu_ref[...])
    y = _dot(hid.astype(BF16), wsd_ref[...])
    gw = gw_ref[...]
    y_lo = y[:, :D_MODEL // 2]
    y_hi = y[:, D_MODEL // 2:]
    for k in range(TOP_K):
        lo, hi = _unpack_rows(_load_parts(yg_ref, k))
        y_lo = y_lo + gw[:, k:k + 1] * lo
        y_hi = y_hi + gw[:, k:k + 1] * hi
    y = jnp.concatenate([y_lo, y_hi], axis=1)
    xn = x_ref[0] + gt_ref[0] * y
    if final:
        ms = jnp.mean(xn * xn, axis=-1, keepdims=True)
        xn = xn * lax.rsqrt(ms + RMS_EPS) * gf_ref[...]
    o_ref[0] = xn


def _combine(yg, gw, tok_off, hp, x, gt2, b0, nb, ws_gate, ws_up, ws_down, g_final, final, tm):
    b, s, d = x.shape
    per_batch = gt2.shape[0] == b
    mod_map = (lambda bi, i: (b0 + bi, 0, 0)) if per_batch else (lambda bi, i: (0, 0, 0))
    full = lambda arr: pl.BlockSpec(arr.shape, lambda bi, i: (0,) * arr.ndim)
    tok = lambda width: pl.BlockSpec((1, tm, width), lambda bi, i: (b0 + bi, i, 0))
    gf = g_final[None]
    nblk = s // tm
    blk0 = (tok_off + b0 * s) // tm
    return pl.pallas_call(
        functools.partial(_combine_kernel, final=final),
        grid=(nb, nblk),
        in_specs=[pl.BlockSpec((SC_PARTS, TOP_K, tm, SC_ROW), lambda bi, i: (0, 0, bi * nblk + i, 0)),
                  pl.BlockSpec((tm, TOP_K), lambda bi, i: (blk0 + bi * nblk + i, 0)),
                  pl.BlockSpec((SC_PARTS, tm, SC_ROW), lambda bi, i: (0, blk0 + bi * nblk + i, 0)),
                  tok(d),
                  pl.BlockSpec((1, 1, d), mod_map),
                  full(ws_gate), full(ws_up), full(ws_down), full(gf)],
        out_specs=tok(d),
        out_shape=jax.ShapeDtypeStruct((b, s, d), F32),
        input_output_aliases={3: 0},
        compiler_params=_cparams(("parallel", "parallel")),
        name="combine",
    )(yg, gw, hp, x, gt2, ws_gate, ws_up, ws_down, gf)


def _project_latent(x, p, b0=0, nb=None, prev=None):
    w_main, w_gate = p['prep']['w_in']
    mods = p['prep']['mods']
    return _proj_in(x, p['g_norm1'], mods[0], mods[1], w_main, w_gate, tm=min(2048, x.shape[1]), tn=1024,
                    b0=b0, nb=nb, prev=prev)


def _layer(x, ctx_s, p, ctx_out, final, g_final, projected=None, after_piece=None, p_next=None):
    b, s, d = x.shape
    sc = ctx_s.shape[1]
    prep = p['prep']
    sh1, sc1, gt1, sh2, sc2, gt2 = prep['mods']
    csh1, csc1, cgt1, csh2, csc2, cgt2 = prep['mods_ctx']

    w_main, w_gate = prep['w_in']
    main, gate = _project_latent(x, p) if projected is None else projected
    ctx_flat = ctx_s.reshape(1, b * sc, d)
    n_ctx, tn_ctx = (N_MAIN, 1024) if ctx_out else (N_KV_MAIN, N_KV_MAIN // 2)
    main_c, gate_c = _proj_in(ctx_flat, p['g_norm1'], csh1, csc1, w_main, w_gate, tm=min(1024, b * sc),
                              tn=tn_ctx, n=n_ctx)
    main_c = main_c.reshape(b, sc, n_ctx)
    gate_c = gate_c.reshape(b, sc, LANE)

    o_na = _na_latent(main, main_c, prep['na_bias'])

    gate_ws = prep['gate_ws']
    s0 = jnp.zeros((b, GLA_HEADS, GLA_DK, GLA_DV), F32)
    o_cf, o_cb, st_f, st_b = _gla_scan(main_c, gate_c, gate_ws, (s0, s0), ctx_out)
    o_f, o_b, _, _ = _gla_scan(main, gate, gate_ws, (st_f, st_b), True)

    w_na, w_sc, w_gla, w_out, w_router = (prep[name] for name in ('w_na', 'w_sc', 'w_gla', 'w_out', 'w_router'))
    n_lat = b * s
    t = n_lat + (b * sc if ctx_out else 0)
    x, hp_all, lg_all = _merge(o_na, p['conv_w'], o_f, o_b, main, x, gt1, p['gla_norm_g'], w_na, w_sc, w_gla,
                               w_out, p['g_norm2'], sh2, sc2, w_router, tm=min(512, s), n_routed=t, tok_off=0)
    if ctx_out:
        o_na_c = _dense_attn(main_c)
        ctx_s, hp_all, lg_all = _merge(o_na_c, p['conv_w'], o_cf, o_cb, main_c, ctx_s, cgt1, p['gla_norm_g'],
                                       w_na, w_sc, w_gla, w_out, p['g_norm2'], csh2, csc2, w_router,
                                       tm=min(256, sc), n_routed=t, tok_off=n_lat, routed=(hp_all, lg_all))

    eidx, gw, rank, counts = _route(lg_all, p['b_router'])
    padded = (counts + MOE_BLOCK - 1) // MOE_BLOCK * MOE_BLOCK
    pad_end = jnp.cumsum(padded)
    pad_start = pad_end - padded
    onehot = eidx[:, :, None] == jnp.arange(N_EXPERTS, dtype=jnp.int32)
    dest = jnp.sum(jnp.where(onehot, pad_start, 0), axis=-1) + rank
    n_blocks = -(-(t * TOP_K + N_EXPERTS * (MOE_BLOCK - 1)) // MOE_BLOCK)
    slots = n_blocks * MOE_BLOCK
    blk_start = jnp.arange(n_blocks, dtype=jnp.int32) * MOE_BLOCK
    blk_e = jnp.minimum(jnp.sum(pad_end[None, :] <= blk_start[:, None], axis=1), N_EXPERTS - 1).astype(jnp.int32)
    used_end = (pad_start + counts)[blk_e]
    blk_valid = jnp.clip(used_end - blk_start, 0, MOE_BLOCK).astype(jnp.int32)
    n_used = pad_end[-1] // MOE_BLOCK
    blk_src = jnp.minimum(jnp.arange(n_blocks, dtype=jnp.int32), n_used - 1)
    blk_e = blk_e[blk_src]
    ids = jnp.arange(N_EXPERTS, dtype=jnp.int32)
    used = counts > 0
    later_used = jnp.where(used[None, :] & (ids[None, :] > ids[:, None]), ids[None, :], N_EXPERTS).min(axis=1)
    next_used = jnp.where(later_used == N_EXPERTS, -1, later_used).astype(jnp.int32)
    blk_next = next_used[blk_e]
    blk_slot = ((jnp.cumsum(used) - 1) % 2).astype(jnp.int32)[blk_e]

    xs = _dispatch_rows(hp_all, dest, slots)
    if p_next is not None:
        xs, p_next['prep'] = lax.optimization_barrier((xs, p_next['prep']))
    ys = _experts(xs, blk_e, blk_valid, blk_src, blk_next, blk_slot, p['layer'], p['w_exp_gate'], p['w_exp_up'], p['w_exp_down'])
    gw_t = gw.T
    ws_gate, ws_up, ws_down = prep['ws_gate'], prep['ws_up'], prep['ws_down']

    def gathered(t0, n):
        return _gather_rows(ys, dest, t0, n)

    pieces = next(n for n in (4, 2, 1) if b % n == 0)
    nb = b // pieces
    for q in range(pieces):
        x = _combine(gathered(q * nb * s, nb * s), gw_t, 0, hp_all, x, gt2, q * nb, nb, ws_gate, ws_up, ws_down,
                     g_final, final, tm=min(512, s))
        if after_piece is not None:
            after_piece(x, q * nb, nb)
    if ctx_out:
        ctx_s = _combine(gathered(n_lat, b * sc), gw_t, n_lat, hp_all, ctx_s, cgt2, 0, b, ws_gate, ws_up, ws_down,
                         g_final, False, tm=min(256, sc))
    return x, ctx_s


def kernel(x, c, ctx, c_ctx, w_mod, b_mod, g_norm1, g_norm2, w_in, na_rpb, w_branch_na, conv_w, w_branch_sc,
           gla_gate_w, gla_gate_b, gla_norm_g, w_branch_gla, w_out, w_router, b_router, w_exp_gate, w_exp_up,
           w_exp_down, w_sh_gate, w_sh_up, w_sh_down, g_final):
    stacked = dict(g_norm1=g_norm1, g_norm2=g_norm2, na_rpb=na_rpb, w_branch_na=w_branch_na,
                   conv_w=conv_w, w_branch_sc=w_branch_sc, gla_gate_w=gla_gate_w, gla_gate_b=gla_gate_b,
                   gla_norm_g=gla_norm_g, w_branch_gla=w_branch_gla, w_out=w_out, w_router=w_router,
                   b_router=b_router,
                   w_sh_gate=w_sh_gate, w_sh_up=w_sh_up, w_sh_down=w_sh_down)
    depth = w_in.shape[0]
    rows = x.shape[1] // GRID_W
    layers = []
    for i in range(depth):
        p = {name: arr[i] for name, arr in stacked.items()}
        p.update(layer=i, w_exp_gate=w_exp_gate, w_exp_up=w_exp_up, w_exp_down=w_exp_down)
        mods, mods_ctx = _mod_vectors(c, c_ctx, w_mod, b_mod, i)
        p['prep'] = dict(
            w_in=_prep_w_in(w_in, i), mods=mods, mods_ctx=mods_ctx,
            na_bias=_na_bias_table(p['na_rpb'], rows, min(NA_WIN_R, rows)),
            gate_ws=_gla_gate_weights(p['gla_gate_w'], p['gla_gate_b']),
            w_na=p['w_branch_na'].astype(BF16), w_sc=p['w_branch_sc'].astype(BF16),
            w_gla=p['w_branch_gla'].astype(BF16), w_out=p['w_out'].astype(BF16),
            w_router=jnp.pad(p['w_router'], ((0, 0), (0, LANE - N_EXPERTS))).astype(BF16),
            ws_gate=p['w_sh_gate'].astype(BF16), ws_up=p['w_sh_up'].astype(BF16),
            ws_down=p['w_sh_down'].astype(BF16))
        layers.append(p)

    ctx_s = ctx
    projected = None
    for i, p in enumerate(layers):
        last = i == depth - 1
        p_next = None if last else layers[i + 1]
        after_piece = None
        next_projected = []
        if not last:
            def after_piece(xq, b0, nb, p_next=p_next, acc=next_projected):
                acc.append(_project_latent(xq, p_next, b0, nb, acc[-1] if acc else None))

        x, ctx_s = _layer(x, ctx_s, p, not last, last, g_final, projected, after_piece, p_next)
        projected = next_projected[-1] if next_projected else None
    return x
```

```python
import functools

import numpy as np
import jax
import jax.numpy as jnp
from jax import lax
from jax.experimental import pallas as pl
from jax.experimental.pallas import tpu as pltpu
from jax.experimental.pallas import tpu_sc as plsc

F32 = jnp.float32
BF16 = jnp.bfloat16
U32 = jnp.uint32

D_MODEL = 1024
N_MOD = 6
RMS_EPS = 1e-6
NEG_INF = -1e30
GRID_W = 64
NA_HEADS = 8
NA_HEAD_DIM = 64
NA_WIDTH = NA_HEADS * NA_HEAD_DIM
NA_WIN_R = 8
NA_WIN_C = 16
NA_GROUP = 4
SC_WIDTH = 512
GLA_HEADS = 4
GLA_KEY_WIDTH = 512
GLA_VAL_WIDTH = 1024
GLA_DK = GLA_KEY_WIDTH // GLA_HEADS
GLA_DV = GLA_VAL_WIDTH // GLA_HEADS
GLA_GATE_RANK = 16
GLA_GATE_TAU = 16.0
LOG2_E = 1.4426950408889634
N_EXPERTS = 64
N_EXPERT_GROUPS = 8
GROUP_SIZE = N_EXPERTS // N_EXPERT_GROUPS
TOPK_GROUPS = 4
TOP_K = 8
EXPERT_FF = 256
ROUTED_SCALE = 2.5
MOE_BLOCK = 1024

LANE = 128
GLA_C = 128
GLA_STEP_CHUNKS = 4
GLA_LEVELS = tuple(GLA_C >> (i + 1) for i in range(GLA_C.bit_length() - 1))
VMEM_LIMIT = 48 * 1024 * 1024
SC_WINDOW = 128
SC_ROW = 256
SC_PARTS = D_MODEL // 2 // SC_ROW

OFF_V_GLA = 0
OFF_K_NA = 1024
OFF_V_NA = 1536
OFF_K_GLA = 2048
N_KV_MAIN = 2560
OFF_Q_NA = 2560
OFF_B_SC = 3072
OFF_C_SC = 3584
OFF_X_SC = 4096
OFF_Q_GLA = 4608
OFF_R_GLA = 5120
OFF_MERGE = 6144
N_MAIN = 9216


def _cparams(sem, vmem=VMEM_LIMIT):
    return pltpu.CompilerParams(dimension_semantics=sem, vmem_limit_bytes=vmem)


def _dot(a, b):
    return jnp.dot(a, b, preferred_element_type=F32)


def _dot_nt(a, b):
    return lax.dot_general(a, b, (((1,), (1,)), ((), ())), preferred_element_type=F32)


def _dot_tn(a, b):
    return lax.dot_general(a, b, (((0,), (0,)), ((), ())), preferred_element_type=F32)


def _sigmoid(x):
    return 0.5 * jnp.tanh(0.5 * x) + 0.5


def _pack_rows(x):
    n = x.shape[1] // 2
    r = x.astype(BF16).astype(F32)
    lo = pltpu.bitcast(r[:, :n], U32) >> 16
    hi = pltpu.bitcast(r[:, n:], U32)
    return hi | lo


def _store_parts(ref, words):
    for part in range(SC_PARTS):
        dst = ref.at[part, 0] if len(ref.shape) == 4 else ref.at[part]
        dst[...] = words[:, part * SC_ROW:(part + 1) * SC_ROW]


def _load_parts(ref, *lead):
    return jnp.concatenate([ref[(part,) + lead] for part in range(SC_PARTS)], axis=-1)


def _unpack_rows(w):
    lo = pltpu.bitcast(w << 16, F32)
    hi = pltpu.bitcast(w & jnp.uint32(0xFFFF0000), F32)
    return lo, hi


def _mod_kernel(a_ref, w_ref, b_ref, o_ref):
    a = a_ref[...]
    a = a * _sigmoid(a)
    o_ref[...] = _dot(a.astype(BF16), w_ref[0].astype(BF16)) + b_ref[0]


def _mod_vectors(c, c_ctx, w_mod, b_mod, layer):
    b = c.shape[0]
    rows = -(-(b + 1) // 8) * 8
    a = jnp.concatenate([c, c_ctx[None], jnp.zeros((rows - b - 1, D_MODEL), F32)], axis=0)
    n = N_MOD * D_MODEL
    tn = 1536
    out = pl.pallas_call(
        _mod_kernel,
        grid=(n // tn,),
        in_specs=[pl.BlockSpec((rows, D_MODEL), lambda j: (0, 0)),
                  pl.BlockSpec((1, D_MODEL, tn), lambda j: (layer, 0, j)),
                  pl.BlockSpec((1, 1, tn), lambda j: (layer, 0, j))],
        out_specs=pl.BlockSpec((rows, tn), lambda j: (0, j)),
        out_shape=jax.ShapeDtypeStruct((rows, n), F32),
        compiler_params=_cparams(("parallel",)),
        name="mod_vectors",
    )(a, w_mod, b_mod[:, None])
    lat = out[:b].reshape(b, N_MOD, 1, D_MODEL)
    ctx = out[b].reshape(N_MOD, 1, 1, D_MODEL)
    return [lat[:, i] for i in range(N_MOD)], [ctx[i] for i in range(N_MOD)]


def _proj_kernel(x_ref, g_ref, sh_ref, sc_ref, w_ref, wg_ref, *rest):
    o_ref, og_ref, h_ref = rest[-3:]

    @pl.when(pl.program_id(2) == 0)
    def _():
        x = x_ref[0]
        ms = jnp.mean(x * x, axis=-1, keepdims=True)
        h = x * lax.rsqrt(ms + RMS_EPS) * g_ref[...] * (1.0 + sc_ref[0]) + sh_ref[0]
        hb = h.astype(BF16)
        h_ref[...] = hb
        og_ref[0] = _dot(hb, wg_ref[...])

    o_ref[0] = _dot(h_ref[...], w_ref[...]).astype(o_ref.dtype)


W_IN_TILE = 512
W_IN_GATE_SHIFT = 2 * GLA_GATE_RANK


def _prep_w_in_kernel(a_ref, b_ref, o_ref, g_ref):
    t = pl.program_id(0)
    first_lat = N_KV_MAIN // W_IN_TILE
    a = a_ref[0]

    @pl.when(t < first_lat)
    def _():
        o_ref[...] = a.T.astype(BF16)

    @pl.when(t >= first_lat)
    def _():
        moved = jnp.concatenate([a[W_IN_GATE_SHIFT:], b_ref[0]], axis=0)
        scale = jnp.where(t == first_lat, NA_HEAD_DIM ** -0.5, 1.0)
        o_ref[...] = (moved * scale).T.astype(BF16)

    @pl.when(t == first_lat)
    def _():
        head = a[:LANE]
        row = lax.broadcasted_iota(jnp.int32, head.shape, 0)
        g_ref[...] = jnp.where(row < W_IN_GATE_SHIFT, head, 0.0).T.astype(BF16)


def _prep_w_in(w_in, layer):
    d = w_in.shape[1]
    w_t = jnp.swapaxes(w_in, 1, 2)
    first_lat = N_KV_MAIN // W_IN_TILE
    kv_perm = OFF_K_NA // W_IN_TILE

    def a_map(t):
        return (layer, jnp.where(t < first_lat, (t + first_lat - kv_perm) % first_lat, t), 0)

    def b_map(t):
        return (layer, jnp.where(t < first_lat, 0, (t + 1) * (W_IN_TILE // W_IN_GATE_SHIFT)), 0)

    return pl.pallas_call(
        _prep_w_in_kernel,
        grid=(N_MAIN // W_IN_TILE,),
        in_specs=[pl.BlockSpec((1, W_IN_TILE, d), a_map),
                  pl.BlockSpec((1, W_IN_GATE_SHIFT, d), b_map)],
        out_specs=[pl.BlockSpec((d, W_IN_TILE), lambda t: (0, t)),
                   pl.BlockSpec((d, LANE), lambda t: (0, 0))],
        out_shape=[jax.ShapeDtypeStruct((d, N_MAIN), BF16), jax.ShapeDtypeStruct((d, LANE), BF16)],
        compiler_params=_cparams(("arbitrary",)),
        name="prep_w_in",
    )(w_t, w_t)


def _proj_in(x, g, shift, scale, w_main, w_gate, tm, tn, n=None, b0=0, nb=None, prev=None):
    b, s, d = x.shape
    n = w_main.shape[1] if n is None else n
    nb = b if nb is None else nb
    per_batch = shift.shape[0] == b
    mod_map = (lambda bi, i, j: (b0 + bi, 0, 0)) if per_batch else (lambda bi, i, j: (0, 0, 0))
    operands = (x, g[None], shift, scale, w_main, w_gate)
    extend = () if prev is None else tuple(prev)
    return pl.pallas_call(
        _proj_kernel,
        grid=(nb, s // tm, n // tn),
        in_specs=[pl.BlockSpec((1, tm, d), lambda bi, i, j: (b0 + bi, i, 0)),
                  pl.BlockSpec((1, d), lambda bi, i, j: (0, 0)),
                  pl.BlockSpec((1, 1, d), mod_map),
                  pl.BlockSpec((1, 1, d), mod_map),
                  pl.BlockSpec((d, tn), lambda bi, i, j: (0, j)),
                  pl.BlockSpec((d, LANE), lambda bi, i, j: (0, 0))]
                 + [pl.BlockSpec(memory_space=pl.ANY)] * len(extend),
        out_specs=[pl.BlockSpec((1, tm, tn), lambda bi, i, j: (b0 + bi, i, j)),
                   pl.BlockSpec((1, tm, LANE), lambda bi, i, j: (b0 + bi, i, 0))],
        out_shape=[jax.ShapeDtypeStruct((b, s, n), BF16),
                   jax.ShapeDtypeStruct((b, s, LANE), F32)],
        scratch_shapes=[pltpu.VMEM((tm, d), BF16)],
        input_output_aliases={len(operands): 0, len(operands) + 1: 1} if extend else {},
        compiler_params=_cparams(("parallel", "parallel", "arbitrary")),
        name="proj_in",
    )(*operands, *extend)


def _softmax_av(q, keys, vals, biases):
    scores = []
    for kk, bb in zip(keys, biases):
        s = _dot_nt(q, kk)
        scores.append(s if bb is None else s + bb)
    m = scores[0].max(axis=-1, keepdims=True)
    for s in scores[1:]:
        m = jnp.maximum(m, s.max(axis=-1, keepdims=True))
    num = None
    den = None
    for s, vv in zip(scores, vals):
        e = jnp.exp(s - m)
        dsum = e.sum(axis=-1, keepdims=True)
        o = _dot(e.astype(BF16), vv)
        num = o if num is None else num + o
        den = dsum if den is None else den + dsum
    return num / den


def _na_kernel(q_ref, k_ref, v_ref, kc_ref, vc_ref, *rest, rows, kr):
    *bias_refs, o_ref = rest
    kc = kc_ref[0]
    vc = vc_ref[0]
    for j, bias_ref in enumerate(bias_refs):
        r = pl.program_id(1) * len(bias_refs) + j
        row_start = jnp.clip(r - kr // 2, 0, rows - kr)
        start = pl.multiple_of(row_start * GRID_W, GRID_W)
        n_win = kr * GRID_W
        q = q_ref[0, j * GRID_W:(j + 1) * GRID_W, :]
        kw = k_ref[0, pl.ds(start, n_win), :]
        vw = v_ref[0, pl.ds(start, n_win), :]
        o_ref[0, j * GRID_W:(j + 1) * GRID_W, :] = _na_row(q, kw, vw, kc, vc, bias_ref).astype(o_ref.dtype)


def _na_row(q, kw, vw, kc, vc, bias_ref):
    gw = NA_GROUP * NA_HEAD_DIM
    stacked = (NA_GROUP * GRID_W, gw)
    on_head = (lax.broadcasted_iota(jnp.int32, stacked, 0) // GRID_W
               == lax.broadcasted_iota(jnp.int32, stacked, 1) // NA_HEAD_DIM)
    outs = []
    for g in range(NA_HEADS // NA_GROUP):
        sl = slice(g * gw, (g + 1) * gw)
        q_all = jnp.where(on_head, jnp.concatenate([q[:, sl]] * NA_GROUP, axis=0), jnp.zeros((), q.dtype))
        bias = bias_ref[0, g * NA_GROUP * GRID_W:(g + 1) * NA_GROUP * GRID_W, :]
        o_all = _softmax_av(q_all, [kw[:, sl], kc[:, sl]], [vw[:, sl], vc[:, sl]], [bias, None])
        o_all = jnp.where(on_head, o_all, 0.0).reshape(NA_GROUP, GRID_W, gw)
        outs.append(o_all.sum(axis=0))
    return jnp.concatenate(outs, axis=-1)


def _na_bias_table(rpb, rows, kr):
    col = np.arange(GRID_W)
    col_start = np.clip(col - NA_WIN_C // 2, 0, GRID_W - NA_WIN_C)
    col_ok = (col[None, :] >= col_start[:, None]) & (col[None, :] < col_start[:, None] + NA_WIN_C)
    d_col = np.clip(col[None, :] - col[:, None], -(NA_WIN_C - 1), NA_WIN_C - 1) + NA_WIN_C - 1
    n_dr, n_dc = rpb.shape[1], rpb.shape[2]
    onehot = jnp.asarray((d_col.reshape(-1)[None, :] == np.arange(n_dc)[:, None]).astype(np.float32))
    by_col = jnp.dot(rpb.astype(F32).reshape(NA_HEADS * n_dr, n_dc), onehot, precision=lax.Precision.HIGHEST)
    by_col = by_col.reshape(NA_HEADS, n_dr, GRID_W, GRID_W)
    by_col = jnp.where(col_ok[None, None], by_col, NEG_INF)
    tables = []
    for o in range(kr):
        lo = NA_WIN_R - 1 - o
        tables.append(by_col[:, lo:lo + kr].transpose(0, 2, 1, 3).reshape(NA_HEADS, GRID_W, kr * GRID_W))
    return jnp.stack(tables).reshape(kr, NA_HEADS * GRID_W, kr * GRID_W)


def _na_latent(main, main_ctx, bias):
    b, s, _ = main.shape
    sc = main_ctx.shape[1]
    rows = s // GRID_W
    kr = min(NA_WIN_R, rows)
    w = NA_WIDTH

    per_step = next(n for n in (8, 4, 2, 1) if rows % n == 0)

    def bias_spec(j):
        def bias_map(bi, i):
            r = i * per_step + j
            return (r - jnp.clip(r - kr // 2, 0, rows - kr), 0, 0)
        return pl.BlockSpec((1, NA_HEADS * GRID_W, kr * GRID_W), bias_map)

    return pl.pallas_call(
        functools.partial(_na_kernel, rows=rows, kr=kr),
        grid=(b, rows // per_step),
        in_specs=[pl.BlockSpec((1, per_step * GRID_W, w), lambda bi, i: (bi, i, OFF_Q_NA // w)),
                  pl.BlockSpec((1, s, w), lambda bi, i: (bi, 0, OFF_K_NA // w)),
                  pl.BlockSpec((1, s, w), lambda bi, i: (bi, 0, OFF_V_NA // w)),
                  pl.BlockSpec((1, sc, w), lambda bi, i: (bi, 0, OFF_K_NA // w)),
                  pl.BlockSpec((1, sc, w), lambda bi, i: (bi, 0, OFF_V_NA // w))]
                 + [bias_spec(j) for j in range(per_step)],
        out_specs=pl.BlockSpec((1, per_step * GRID_W, w), lambda bi, i: (bi, i, 0)),
        out_shape=jax.ShapeDtypeStruct((b, s, w), BF16),
        compiler_params=_cparams(("parallel", "arbitrary")),
        name="na_latent",
    )(main, main, main, main_ctx, main_ctx, *([bias] * per_step))


def _dense_attn_kernel(q_ref, k_ref, v_ref, o_ref):
    q = q_ref[0]
    k = k_ref[0]
    v = v_ref[0]
    outs = []
    for h in range(NA_HEADS):
        sl = slice(h * NA_HEAD_DIM, (h + 1) * NA_HEAD_DIM)
        outs.append(_softmax_av(q[:, sl], [k[:, sl]], [v[:, sl]], [None]))
    o_ref[0] = jnp.concatenate(outs, axis=-1).astype(o_ref.dtype)


def _dense_attn(main_ctx):
    b, sc, _ = main_ctx.shape
    w = NA_WIDTH
    return pl.pallas_call(
        _dense_attn_kernel,
        grid=(b,),
        in_specs=[pl.BlockSpec((1, sc, w), lambda bi: (bi, 0, OFF_Q_NA // w)),
                  pl.BlockSpec((1, sc, w), lambda bi: (bi, 0, OFF_K_NA // w)),
                  pl.BlockSpec((1, sc, w), lambda bi: (bi, 0, OFF_V_NA // w))],
        out_specs=pl.BlockSpec((1, sc, w), lambda bi: (bi, 0, 0)),
        out_shape=jax.ShapeDtypeStruct((b, sc, w), BF16),
        compiler_params=_cparams(("parallel",)),
        name="ctx_attn",
    )(main_ctx, main_ctx, main_ctx)


HALO = 16


def _gla_matrices(reverse):
    c = GLA_C
    t = np.arange(c)[:, None]
    m = np.arange(c)[None, :]
    blocks = [m <= t, m > t]
    for b in GLA_LEVELS:
        first = (t // (2 * b)) * (2 * b) + b
        is_q = (t & b) != 0
        blocks.append(np.where(is_q, (m > first) & (m <= t), (m > t) & (m <= first)))
    mats = np.stack(blocks).astype(np.float32)
    if reverse:
        mats = mats[:, ::-1, ::-1]
    mats = mats.reshape(-1, c)
    return jnp.asarray(np.concatenate([mats, mats], axis=1), dtype=BF16)


def _split_bf16(x):
    hi = x.astype(BF16)
    return hi, (x - hi.astype(F32)).astype(BF16)


def _pair_block_diag(x):
    lane = lax.broadcasted_iota(jnp.int32, x.shape, 1)
    zero = jnp.zeros((), x.dtype)
    return jnp.concatenate([jnp.where(lane < GLA_DK, x, zero), jnp.where(lane >= GLA_DK, x, zero)], axis=0)


def _gla_kernel(*refs, emit):
    n_in = 9 if emit else 8
    n_out = 2 if emit else 1
    ins, outs, scratch = refs[:2 * n_in], refs[2 * n_in:2 * (n_in + n_out)], refs[2 * (n_in + n_out):]
    ins = [ins[d * n_in:(d + 1) * n_in] for d in range(2)]
    outs = [outs[d * n_out:(d + 1) * n_out] for d in range(2)]
    step = pl.program_id(1)

    @pl.when(step == 0)
    def _():
        for d in range(2):
            scratch[d][...] = ins[d][-1][0]

    pending = [_gla_direction(ins[d][:-1], outs[d][:-1], scratch[d], reverse, emit)
               for d, reverse in enumerate((False, True))]
    while pending:
        pending = [stages for stages in pending if next(stages, "done") != "done"]

    @pl.when(step == pl.num_programs(1) - 1)
    def _():
        for d in range(2):
            outs[d][-1][0] = scratch[d][...]


def _gla_direction(ins, outs, st_ref, reverse, emit):
    gate_ref = ins[-5]
    order = range(gate_ref.shape[1] // GLA_C)
    for sub in (reversed(order) if reverse else order):
        yield from _gla_chunk(ins, outs, st_ref, reverse, emit, slice(sub * GLA_C, (sub + 1) * GLA_C))


def _gla_chunk(ins, outs, st_ref, reverse, emit, rows):
    if emit:
        q_ref, k_ref, v_ref, gt_ref, w2a_ref, w2b_ref, b2_ref, a_ref = ins
        (o_ref,) = outs
    else:
        k_ref, v_ref, gt_ref, w2a_ref, w2b_ref, b2_ref, a_ref = ins
    c = GLA_C
    pw = 2 * GLA_DK

    lr_hi, lr_lo = _split_bf16(gt_ref[0, rows])
    logit = (_dot(jnp.concatenate([lr_hi, lr_lo], axis=1), w2a_ref[...]) + _dot(lr_hi, w2b_ref[...])
             + b2_ref[...])
    g = (jnp.minimum(logit, 0.0) - jnp.log1p(jnp.exp(-jnp.abs(logit)))) * (LOG2_E / GLA_GATE_TAU)
    g_hi, g_lo = _split_bf16(g)
    args = _dot(a_ref[...], jnp.concatenate([g_hi, g_lo], axis=0))
    cum = args[0:c]
    rem = args[c:2 * c]
    last_row = cum[0:1] if reverse else cum[c - 1:c]
    yield

    k = k_ref[0, rows].astype(F32)
    v = v_ref[0, rows]
    atts = []
    if emit:
        q = q_ref[0, rows].astype(F32) * (GLA_DK ** -0.5)
        row_t = lax.broadcasted_iota(jnp.int32, (c, pw), 0)
        si = lax.broadcasted_iota(jnp.int32, (2 * c, c), 0) & (c - 1)
        ti = lax.broadcasted_iota(jnp.int32, (2 * c, c), 1)
        if reverse:
            row_t, ti, si = c - 1 - row_t, c - 1 - ti, c - 1 - si
        for hp in range(GLA_HEADS // 2):
            cs = slice(hp * pw, (hp + 1) * pw)
            qp, kp = q[:, cs], k[:, cs]
            att = jnp.where(ti == si, _dot_nt(_pair_block_diag(kp.astype(BF16)), qp.astype(BF16)), 0.0)
            for l, b in enumerate(GLA_LEVELS):
                x = (jnp.exp2(args[(2 + l) * c:(3 + l) * c, cs])
                     * jnp.where((row_t & b) != 0, qp, kp)).astype(BF16)
                pair = (((ti ^ si) >> (b.bit_length() - 1)) == 1) & ((ti & b) != 0)
                att = jnp.where(pair, _dot_nt(_pair_block_diag(x), x), att)
                yield
            atts.append(att.astype(BF16))

    outs = []
    for h in range(GLA_HEADS):
        sl = slice(h * GLA_DK, (h + 1) * GLA_DK)
        kh = k[:, sl]
        vh = v[:, h * GLA_DV:(h + 1) * GLA_DV]
        state = st_ref[h]
        kd = (kh * jnp.exp2(rem[:, sl])).astype(BF16)
        decay = jnp.exp2(jnp.broadcast_to(last_row[:, sl], (GLA_DK, GLA_DK))).T
        decay = jnp.concatenate([decay] * (GLA_DV // GLA_DK), axis=1)
        if emit:
            qd = (q[:, sl] * jnp.exp2(cum[:, sl])).astype(BF16)
            att_t = atts[h // 2][(h % 2) * c:(h % 2 + 1) * c]
            both = _dot_tn(jnp.concatenate([att_t, kd], axis=1), vh)
            outs.append(_dot(qd, state.astype(BF16)) + both[:c])
            st_ref[h] = decay * state + both[c:]
        else:
            st_ref[h] = decay * state + _dot_tn(kd, vh)
        yield

    if emit:
        o_ref[0, rows] = jnp.concatenate(outs, axis=-1).astype(o_ref.dtype)


def _gla_scan(main, gate, gate_ws, s0s, emit):
    b, l, _ = main.shape
    step_rows = GLA_C * min(GLA_STEP_CHUNKS, l // GLA_C)
    n = l // step_rows
    const = lambda arr: pl.BlockSpec(arr.shape, lambda bi, s: (0,) * arr.ndim)
    state_spec = pl.BlockSpec((1, GLA_HEADS, GLA_DK, GLA_DV), lambda bi, s: (bi, 0, 0, 0))
    state_shape = jax.ShapeDtypeStruct((b, GLA_HEADS, GLA_DK, GLA_DV), F32)
    in_specs, args, out_specs, out_shape = [], [], [], []
    for reverse in (False, True):
        amat = _gla_matrices(reverse)
        w2a, w2b, b2 = gate_ws[reverse]

        def col(block, reverse=reverse):
            return lambda bi, s: (bi, n - 1 - s if reverse else s, block)

        if emit:
            in_specs.append(pl.BlockSpec((1, step_rows, GLA_KEY_WIDTH), col(OFF_Q_GLA // GLA_KEY_WIDTH)))
            args.append(main)
            out_specs.append(pl.BlockSpec((1, step_rows, GLA_VAL_WIDTH), col(0)))
            out_shape.append(jax.ShapeDtypeStruct((b, l, GLA_VAL_WIDTH), BF16))
        in_specs += [pl.BlockSpec((1, step_rows, GLA_KEY_WIDTH), col(OFF_K_GLA // GLA_KEY_WIDTH)),
                     pl.BlockSpec((1, step_rows, GLA_VAL_WIDTH), col(OFF_V_GLA // GLA_VAL_WIDTH)),
                     pl.BlockSpec((1, step_rows, LANE), col(0)),
                     const(w2a), const(w2b), const(b2), const(amat), state_spec]
        args += [main, main, gate, w2a, w2b, b2, amat, s0s[reverse]]
        out_specs.append(state_spec)
        out_shape.append(state_shape)
    res = pl.pallas_call(
        functools.partial(_gla_kernel, emit=emit),
        grid=(b, n),
        in_specs=in_specs,
        out_specs=out_specs,
        out_shape=out_shape,
        scratch_shapes=[pltpu.VMEM((GLA_HEADS, GLA_DK, GLA_DV), F32)] * 2,
        compiler_params=_cparams(("parallel", "arbitrary")),
        name="gla_scan",
    )(*args)
    return (res[0], res[2], res[1], res[3]) if emit else (None, None, res[0], res[1])


def _gla_gate_weights(gate_w, gate_b):
    out = []
    for dr in range(2):
        w = jnp.zeros((LANE, GLA_KEY_WIDTH), F32)
        w = w.at[dr * GLA_GATE_RANK:(dr + 1) * GLA_GATE_RANK].set(gate_w[dr])
        w_hi = w.astype(BF16)
        w_lo = (w - w_hi.astype(F32)).astype(BF16)
        out.append((jnp.concatenate([w_hi, w_hi], axis=0), w_lo, gate_b[dr][None]))
    return out


def _conv_tile(b_ref, c_ref, x_ref, cp_ref, xp_ref, cn_ref, xn_ref, w_ref):
    i, n = pl.program_id(1), pl.num_programs(1)
    u = c_ref[0].astype(F32) * x_ref[0].astype(F32)
    tm = u.shape[0]
    before = jnp.where(i > 0, 1.0, 0.0) * (cp_ref[0, HALO - 1:HALO].astype(F32) * xp_ref[0, HALO - 1:HALO].astype(F32))
    after = jnp.where(i < n - 1, 1.0, 0.0) * (cn_ref[0, 0:1].astype(F32) * xn_ref[0, 0:1].astype(F32))
    t = lax.broadcasted_iota(jnp.int32, u.shape, 0)
    prev = jnp.where(t == 0, before, pltpu.roll(u, 1, axis=0))
    nxt = jnp.where(t == tm - 1, after, pltpu.roll(u, tm - 1, axis=0))
    w = w_ref[...]
    return b_ref[0].astype(F32) * (prev * w[0:1] + u * w[1:2] + nxt * w[2:3])


def _merge_kernel(ona_ref, bsc_ref, csc_ref, xsc_ref, cp_ref, xp_ref, cn_ref, xn_ref, cw_ref, of_ref, ob_ref,
                  r_ref, gna_ref, gsc_ref, ggl_ref, x_ref, gt_ref,
                  gn_ref, wna_ref, wsc_ref, wgl_ref, wo_ref, g2_ref, sh2_ref, sc2_ref, wr_ref, *rest):
    xo_ref, hp_ref, lg_ref = rest[-3:]
    o_sc = _conv_tile(bsc_ref, csc_ref, xsc_ref, cp_ref, xp_ref, cn_ref, xn_ref, cw_ref).astype(BF16)
    o = of_ref[0].astype(F32) + ob_ref[0].astype(F32)
    normed = []
    for h in range(GLA_HEADS):
        oh = o[:, h * GLA_DV:(h + 1) * GLA_DV]
        ms = jnp.mean(oh * oh, axis=-1, keepdims=True)
        normed.append(oh * lax.rsqrt(ms + RMS_EPS))
    r = r_ref[0].astype(F32)
    y_gla = jnp.concatenate(normed, axis=-1) * gn_ref[...] * (r * _sigmoid(r))
    y = (_sigmoid(gna_ref[0].astype(F32)) * _dot(ona_ref[0], wna_ref[...])
         + _sigmoid(gsc_ref[0].astype(F32)) * _dot(o_sc, wsc_ref[...])
         + _sigmoid(ggl_ref[0].astype(F32)) * _dot(y_gla.astype(BF16), wgl_ref[...]))
    xn = x_ref[0] + gt_ref[0] * _dot(y.astype(BF16), wo_ref[...])
    xo_ref[0] = xn
    ms = jnp.mean(xn * xn, axis=-1, keepdims=True)
    h2 = xn * lax.rsqrt(ms + RMS_EPS) * g2_ref[...] * (1.0 + sc2_ref[0]) + sh2_ref[0]
    _store_parts(hp_ref, _pack_rows(h2))
    lg_ref[...] = _dot(h2.astype(BF16), wr_ref[...])


def _merge(o_na, conv_w, o_f, o_b, main, x, gt1, gn, w_na, w_sc, w_gla, w_out, g2, sh2, sc2, w_router, tm,
           n_routed, tok_off, routed=None):
    b, s, d = x.shape
    per_batch = gt1.shape[0] == b
    mod_map = (lambda bi, i: (bi, 0, 0)) if per_batch else (lambda bi, i: (0, 0, 0))
    tok = lambda width, blk: pl.BlockSpec((1, tm, width), lambda bi, i: (bi, i, blk))
    full = lambda arr: pl.BlockSpec(arr.shape, lambda bi, i: (0,) * arr.ndim)
    mod = pl.BlockSpec((1, 1, d), mod_map)
    gn_t = jnp.tile(gn, GLA_HEADS)[None]
    g2_t = g2[None]
    per_tile = tm // HALO
    last_halo = s // HALO - 1
    halo_prev = lambda blk: pl.BlockSpec(
        (1, HALO, SC_WIDTH), lambda bi, i: (bi, jnp.maximum(i * per_tile - 1, 0), blk))
    halo_next = lambda blk: pl.BlockSpec(
        (1, HALO, SC_WIDTH), lambda bi, i: (bi, jnp.minimum((i + 1) * per_tile, last_halo), blk))
    col_b, col_c, col_x = OFF_B_SC // SC_WIDTH, OFF_C_SC // SC_WIDTH, OFF_X_SC // SC_WIDTH
    extend = () if routed is None else tuple(routed)
    operands = (o_na, main, main, main, main, main, main, main, conv_w, o_f, o_b, main, main, main, main, x, gt1,
                gn_t, w_na, w_sc, w_gla, w_out, g2_t, sh2, sc2, w_router)
    n_in = len(operands)
    nblk = s // tm
    blk0 = tok_off // tm
    return pl.pallas_call(
        _merge_kernel,
        grid=(b, s // tm),
        in_specs=[tok(NA_WIDTH, 0), tok(SC_WIDTH, col_b), tok(SC_WIDTH, col_c), tok(SC_WIDTH, col_x),
                  halo_prev(col_c), halo_prev(col_x), halo_next(col_c), halo_next(col_x), full(conv_w),
                  tok(GLA_VAL_WIDTH, 0), tok(GLA_VAL_WIDTH, 0),
                  tok(d, OFF_R_GLA // d), tok(d, OFF_MERGE // d), tok(d, OFF_MERGE // d + 1),
                  tok(d, OFF_MERGE // d + 2), tok(d, 0), mod,
                  full(gn_t), full(w_na), full(w_sc), full(w_gla), full(w_out), full(g2_t), mod, mod,
                  full(w_router)] + [pl.BlockSpec(memory_space=pl.ANY)] * len(extend),
        out_specs=[tok(d, 0),
                   pl.BlockSpec((SC_PARTS, tm, SC_ROW), lambda bi, i: (0, blk0 + bi * nblk + i, 0)),
                   pl.BlockSpec((tm, LANE), lambda bi, i: (blk0 + bi * nblk + i, 0))],
        out_shape=[jax.ShapeDtypeStruct((b, s, d), F32),
                   jax.ShapeDtypeStruct((SC_PARTS, n_routed, SC_ROW), U32),
                   jax.ShapeDtypeStruct((n_routed, LANE), F32)],
        input_output_aliases={n_in: 1, n_in + 1: 2} if extend else {},
        compiler_params=_cparams(("parallel", "parallel")),
        name="merge",
    )(*operands, *extend)


def _router_kernel(lg_ref, br_ref, tri_ref, eidx_ref, gw_ref, rank_ref, cnt_ref, carry_ref):
    tm = lg_ref.shape[0]

    @pl.when(pl.program_id(0) == 0)
    def _():
        carry_ref[...] = jnp.zeros_like(carry_ref)

    scores = _sigmoid(lg_ref[...].T[:N_EXPERTS])
    sel = scores + br_ref[...]
    neg = -jnp.inf

    sel3 = sel.reshape(N_EXPERT_GROUPS, GROUP_SIZE, tm)
    i3 = lax.broadcasted_iota(jnp.int32, sel3.shape, 1)
    m1 = sel3.max(axis=1, keepdims=True)
    first = jnp.where(sel3 == m1, i3, GROUP_SIZE).min(axis=1, keepdims=True)
    m2 = jnp.where(i3 == first, neg, sel3).max(axis=1, keepdims=True)
    gscore = (m1 + m2)[:, 0, :]

    gi = lax.broadcasted_iota(jnp.int32, gscore.shape, 0)
    gmask = jnp.zeros(gscore.shape, jnp.bool_)
    for _ in range(TOPK_GROUPS):
        m = gscore.max(axis=0, keepdims=True)
        pick = gi == jnp.where(gscore == m, gi, N_EXPERT_GROUPS).min(axis=0, keepdims=True)
        gmask = gmask | pick
        gscore = jnp.where(pick, neg, gscore)
    emask = jnp.broadcast_to(gmask[:, None, :], sel3.shape).reshape(N_EXPERTS, tm)
    sel = jnp.where(emask, sel, neg)

    ei = lax.broadcasted_iota(jnp.int32, sel.shape, 0)
    picks, idxs, ws = [], [], []
    for _ in range(TOP_K):
        m = sel.max(axis=0, keepdims=True)
        idx = jnp.where(sel == m, ei, N_EXPERTS).min(axis=0, keepdims=True)
        pick = ei == idx
        picks.append(pick)
        idxs.append(idx)
        ws.append(jnp.where(pick, scores, 0.0).sum(axis=0, keepdims=True))
        sel = jnp.where(pick, neg, sel)
    w = jnp.concatenate(ws, axis=0)
    gw_ref[...] = (w / w.sum(axis=0, keepdims=True) * ROUTED_SCALE).T
    eidx_ref[...] = jnp.concatenate(idxs, axis=0)

    onehot = picks[0]
    for p in picks[1:]:
        onehot = onehot | p
    onehot = jnp.where(onehot, 1.0, 0.0).astype(BF16)
    before = _dot(onehot, tri_ref[...]) + jnp.tile(carry_ref[...], (1, tm // LANE))
    rank_ref[...] = jnp.concatenate(
        [jnp.where(p, before, 0.0).sum(axis=0, keepdims=True) for p in picks], axis=0).astype(jnp.int32)
    carry_ref[...] += _dot(onehot, jnp.ones((tm, LANE), BF16))
    cnt_ref[...] = carry_ref[...]


def _route(logits, b_router):
    t = logits.shape[0]
    tm = next(n for n in (1024, 512) if t % n == 0)
    br = jnp.broadcast_to(b_router.astype(F32)[:, None], (N_EXPERTS, tm))
    tri = jnp.asarray(np.triu(np.ones((tm, tm), np.float32), 1), dtype=BF16)
    kt = lambda dt: jax.ShapeDtypeStruct((TOP_K, t), dt)
    eidx, gw, rank, cnt = pl.pallas_call(
        _router_kernel,
        grid=(t // tm,),
        in_specs=[pl.BlockSpec((tm, LANE), lambda i: (i, 0)),
                  pl.BlockSpec((N_EXPERTS, tm), lambda i: (0, 0)),
                  pl.BlockSpec((tm, tm), lambda i: (0, 0))],
        out_specs=[pl.BlockSpec((TOP_K, tm), lambda i: (0, i)),
                   pl.BlockSpec((tm, TOP_K), lambda i: (i, 0)),
                   pl.BlockSpec((TOP_K, tm), lambda i: (0, i)),
                   pl.BlockSpec((N_EXPERTS, LANE), lambda i: (0, 0))],
        out_shape=[kt(jnp.int32), jax.ShapeDtypeStruct((t, TOP_K), F32), kt(jnp.int32),
                   jax.ShapeDtypeStruct((N_EXPERTS, LANE), F32)],
        scratch_shapes=[pltpu.VMEM((N_EXPERTS, LANE), F32)],
        compiler_params=_cparams(("arbitrary",)),
        name="router",
    )(logits, br, tri)
    return eidx, gw, rank, cnt[:, 0].astype(jnp.int32)


def _sc_mesh():
    return plsc.VectorSubcoreMesh(core_axis_name="core", subcore_axis_name="subcore")


def _dispatch_rows(xp, dest, slots):
    parts, t, _ = xp.shape

    @pl.kernel(out_type=jax.ShapeDtypeStruct((parts, slots, SC_ROW), xp.dtype), mesh=_sc_mesh(),
               scratch_types=[], name="moe_dispatch")
    def run(x_hbm, d_hbm, o_hbm):
        for part in range(parts):
            out_part = o_hbm.at[part]

            def body(x_vmem, d_vmem, out_part=out_part):
                for k in range(TOP_K):
                    pltpu.sync_copy(x_vmem, out_part.at[d_vmem.at[k]])

            pltpu.emit_pipeline(
                body,
                grid=(t // SC_WINDOW,),
                in_specs=[pl.BlockSpec((SC_WINDOW, SC_ROW), lambda i: (i, 0)),
                          pl.BlockSpec((TOP_K, SC_WINDOW), lambda i: (0, i))],
                out_specs=[],
                core_axis_name=("core", "subcore"),
                dimension_semantics=(pltpu.PARALLEL,),
            )(x_hbm.at[part], d_hbm)

    return run(xp, dest)


def _gather_rows(yp, dest, t0, n):
    parts = yp.shape[0]
    nwin = n // SC_WINDOW
    win0 = t0 // SC_WINDOW

    @pl.kernel(out_type=jax.ShapeDtypeStruct((parts, TOP_K * n, SC_ROW), yp.dtype), mesh=_sc_mesh(),
               scratch_types=[], name="moe_gather")
    def run(y_hbm, d_hbm, o_hbm):
        for part in range(parts):
            table = y_hbm.at[part]

            def body(d_vmem, o_vmem, table=table):
                pltpu.sync_copy(table.at[d_vmem.at[0]], o_vmem)

            pltpu.emit_pipeline(
                body,
                grid=(TOP_K, nwin),
                in_specs=[pl.BlockSpec((1, SC_WINDOW), lambda k, j: (k, win0 + j))],
                out_specs=[pl.BlockSpec((SC_WINDOW, SC_ROW), lambda k, j: (k * nwin + j, 0))],
                core_axis_name=("core", "subcore"),
                dimension_semantics=(pltpu.PARALLEL, pltpu.PARALLEL),
            )(d_hbm, o_hbm.at[part])

    return run(yp, dest).reshape(parts, TOP_K, n, SC_ROW)


def _expert_kernel(be_ref, bv_ref, bs_ref, nx_ref, sl_ref, x_ref, wg_hbm, wu_hbm, wd_hbm, o_ref,
                   wg_f, wu_f, wd_f, wg_s, wu_s, wd_s, sems, *, layer):
    i = pl.program_id(0)
    valid = bv_ref[i]
    expert = be_ref[i]
    new_expert = (i == 0) | (expert != be_ref[jnp.maximum(i - 1, 0)])
    slot = sl_ref[i]

    def fetch(which, into):
        return [pltpu.make_async_copy(src.at[layer, which], dst.at[into], sems.at[into, j])
                for j, (src, dst) in enumerate(((wg_hbm, wg_f), (wu_hbm, wu_f), (wd_hbm, wd_f)))]

    @pl.when(i == 0)
    def _():
        for cp in fetch(expert, slot):
            cp.start()

    @pl.when(new_expert)
    def _():
        for cp in fetch(expert, slot):
            cp.wait()
        upcoming = nx_ref[i]

        @pl.when(upcoming >= 0)
        def _():
            for cp in fetch(upcoming, 1 - slot):
                cp.start()

        wg_s[...] = wg_f[slot].astype(BF16)
        wu_s[...] = wu_f[slot].astype(BF16)
        wd_s[...] = wd_f[slot].astype(BF16)

    @pl.when(valid > 0)
    def _():
        w = _load_parts(x_ref)
        row = lax.broadcasted_iota(jnp.int32, w.shape, 0)
        w = jnp.where(row < valid, w, jnp.uint32(0))
        lo, hi = _unpack_rows(w)
        x = jnp.concatenate([lo, hi], axis=1).astype(BF16)
        a = _dot(x, wg_s[...])
        hid = a * _sigmoid(a) * _dot(x, wu_s[...])
        _store_parts(o_ref, _pack_rows(_dot(hid.astype(BF16), wd_s[...])))


def _experts(xs, blk_e, blk_valid, blk_src, blk_next, blk_slot, layer, w_gate, w_up, w_down):
    parts, slots, _ = xs.shape
    d = D_MODEL
    nb = slots // MOE_BLOCK
    data = pl.BlockSpec((parts, MOE_BLOCK, SC_ROW), lambda i, be, bv, bs, nx, sl: (0, bs[i], 0))
    stage = lambda shape: pltpu.VMEM((2,) + shape, F32)
    return pl.pallas_call(
        functools.partial(_expert_kernel, layer=layer),
        grid_spec=pltpu.PrefetchScalarGridSpec(
            num_scalar_prefetch=5,
            grid=(nb,),
            in_specs=[data] + [pl.BlockSpec(memory_space=pl.ANY)] * 3,
            out_specs=data,
            scratch_shapes=[stage((d, EXPERT_FF)), stage((d, EXPERT_FF)), stage((EXPERT_FF, d)),
                            pltpu.VMEM((d, EXPERT_FF), BF16), pltpu.VMEM((d, EXPERT_FF), BF16),
                            pltpu.VMEM((EXPERT_FF, d), BF16), pltpu.SemaphoreType.DMA((2, 3))]),
        out_shape=jax.ShapeDtypeStruct((parts, slots, SC_ROW), U32),
        compiler_params=_cparams(("arbitrary",)),
        name="experts",
    )(blk_e, blk_valid, blk_src, blk_next, blk_slot, xs, w_gate, w_up, w_down)


def _combine_kernel(yg_ref, gw_ref, hp_ref, x_ref, gt_ref, wsg_ref, wsu_ref, wsd_ref, gf_ref, o_ref, *, final):
    h = jnp.concatenate(_unpack_rows(_load_parts(hp_ref)), axis=1).astype(BF16)
    a = _dot(h, wsg_ref[...])
    hid = a * _sigmoid(a) * _dot(h, wsu_ref[...])
    y = _dot(hid.astype(BF16), wsd_ref[...])
    gw = gw_ref[...]
    y_lo = y[:, :D_MODEL // 2]
    y_hi = y[:, D_MODEL // 2:]
    for k in range(TOP_K):
        lo, hi = _unpack_rows(_load_parts(yg_ref, k))
        y_lo = y_lo + gw[:, k:k + 1] * lo
        y_hi = y_hi + gw[:, k:k + 1] * hi
    y = jnp.concatenate([y_lo, y_hi], axis=1)
    xn = x_ref[0] + gt_ref[0] * y
    if final:
        ms = jnp.mean(xn * xn, axis=-1, keepdims=True)
        xn = xn * lax.rsqrt(ms + RMS_EPS) * gf_ref[...]
    o_ref[0] = xn


def _combine(yg, gw, tok_off, hp, x, gt2, b0, nb, ws_gate, ws_up, ws_down, g_final, final, tm):
    b, s, d = x.shape
    per_batch = gt2.shape[0] == b
    mod_map = (lambda bi, i: (b0 + bi, 0, 0)) if per_batch else (lambda bi, i: (0, 0, 0))
    full = lambda arr: pl.BlockSpec(arr.shape, lambda bi, i: (0,) * arr.ndim)
    tok = lambda width: pl.BlockSpec((1, tm, width), lambda bi, i: (b0 + bi, i, 0))
    gf = g_final[None]
    nblk = s // tm
    blk0 = (tok_off + b0 * s) // tm
    return pl.pallas_call(
        functools.partial(_combine_kernel, final=final),
        grid=(nb, nblk),
        in_specs=[pl.BlockSpec((SC_PARTS, TOP_K, tm, SC_ROW), lambda bi, i: (0, 0, bi * nblk + i, 0)),
                  pl.BlockSpec((tm, TOP_K), lambda bi, i: (blk0 + bi * nblk + i, 0)),
                  pl.BlockSpec((SC_PARTS, tm, SC_ROW), lambda bi, i: (0, blk0 + bi * nblk + i, 0)),
                  tok(d),
                  pl.BlockSpec((1, 1, d), mod_map),
                  full(ws_gate), full(ws_up), full(ws_down), full(gf)],
        out_specs=tok(d),
        out_shape=jax.ShapeDtypeStruct((b, s, d), F32),
        input_output_aliases={3: 0},
        compiler_params=_cparams(("parallel", "parallel")),
        name="combine",
    )(yg, gw, hp, x, gt2, ws_gate, ws_up, ws_down, gf)


def _project_latent(x, p, b0=0, nb=None, prev=None):
    w_main, w_gate = p['prep']['w_in']
    mods = p['prep']['mods']
    return _proj_in(x, p['g_norm1'], mods[0], mods[1], w_main, w_gate, tm=min(2048, x.shape[1]), tn=1024,
                    b0=b0, nb=nb, prev=prev)


def _layer(x, ctx_s, p, ctx_out, final, g_final, projected=None, after_piece=None, p_next=None):
    b, s, d = x.shape
    sc = ctx_s.shape[1]
    prep = p['prep']
    sh1, sc1, gt1, sh2, sc2, gt2 = prep['mods']
    csh1, csc1, cgt1, csh2, csc2, cgt2 = prep['mods_ctx']

    w_main, w_gate = prep['w_in']
    main, gate = _project_latent(x, p) if projected is None else projected
    ctx_flat = ctx_s.reshape(1, b * sc, d)
    n_ctx, tn_ctx = (N_MAIN, 1024) if ctx_out else (N_KV_MAIN, N_KV_MAIN // 2)
    main_c, gate_c = _proj_in(ctx_flat, p['g_norm1'], csh1, csc1, w_main, w_gate, tm=min(1024, b * sc),
                              tn=tn_ctx, n=n_ctx)
    main_c = main_c.reshape(b, sc, n_ctx)
    gate_c = gate_c.reshape(b, sc, LANE)

    o_na = _na_latent(main, main_c, prep['na_bias'])

    gate_ws = prep['gate_ws']
    s0 = jnp.zeros((b, GLA_HEADS, GLA_DK, GLA_DV), F32)
    o_cf, o_cb, st_f, st_b = _gla_scan(main_c, gate_c, gate_ws, (s0, s0), ctx_out)
    o_f, o_b, _, _ = _gla_scan(main, gate, gate_ws, (st_f, st_b), True)

    w_na, w_sc, w_gla, w_out, w_router = (prep[name] for name in ('w_na', 'w_sc', 'w_gla', 'w_out', 'w_router'))
    n_lat = b * s
    t = n_lat + (b * sc if ctx_out else 0)
    x, hp_all, lg_all = _merge(o_na, p['conv_w'], o_f, o_b, main, x, gt1, p['gla_norm_g'], w_na, w_sc, w_gla,
                               w_out, p['g_norm2'], sh2, sc2, w_router, tm=min(512, s), n_routed=t, tok_off=0)
    if ctx_out:
        o_na_c = _dense_attn(main_c)
        ctx_s, hp_all, lg_all = _merge(o_na_c, p['conv_w'], o_cf, o_cb, main_c, ctx_s, cgt1, p['gla_norm_g'],
                                       w_na, w_sc, w_gla, w_out, p['g_norm2'], csh2, csc2, w_router,
                                       tm=min(256, sc), n_routed=t, tok_off=n_lat, routed=(hp_all, lg_all))

    eidx, gw, rank, counts = _route(lg_all, p['b_router'])
    padded = (counts + MOE_BLOCK - 1) // MOE_BLOCK * MOE_BLOCK
    pad_end = jnp.cumsum(padded)
    pad_start = pad_end - padded
    onehot = eidx[:, :, None] == jnp.arange(N_EXPERTS, dtype=jnp.int32)
    dest = jnp.sum(jnp.where(onehot, pad_start, 0), axis=-1) + rank
    n_blocks = -(-(t * TOP_K + N_EXPERTS * (MOE_BLOCK - 1)) // MOE_BLOCK)
    slots = n_blocks * MOE_BLOCK
    blk_start = jnp.arange(n_blocks, dtype=jnp.int32) * MOE_BLOCK
    blk_e = jnp.minimum(jnp.sum(pad_end[None, :] <= blk_start[:, None], axis=1), N_EXPERTS - 1).astype(jnp.int32)
    used_end = (pad_start + counts)[blk_e]
    blk_valid = jnp.clip(used_end - blk_start, 0, MOE_BLOCK).astype(jnp.int32)
    n_used = pad_end[-1] // MOE_BLOCK
    blk_src = jnp.minimum(jnp.arange(n_blocks, dtype=jnp.int32), n_used - 1)
    blk_e = blk_e[blk_src]
    ids = jnp.arange(N_EXPERTS, dtype=jnp.int32)
    used = counts > 0
    later_used = jnp.where(used[None, :] & (ids[None, :] > ids[:, None]), ids[None, :], N_EXPERTS).min(axis=1)
    next_used = jnp.where(later_used == N_EXPERTS, -1, later_used).astype(jnp.int32)
    blk_next = next_used[blk_e]
    blk_slot = ((jnp.cumsum(used) - 1) % 2).astype(jnp.int32)[blk_e]

    xs = _dispatch_rows(hp_all, dest, slots)
    if p_next is not None:
        xs, p_next['prep'] = lax.optimization_barrier((xs, p_next['prep']))
    ys = _experts(xs, blk_e, blk_valid, blk_src, blk_next, blk_slot, p['layer'], p['w_exp_gate'], p['w_exp_up'], p['w_exp_down'])
    gw_t = gw
    ws_gate, ws_up, ws_down = prep['ws_gate'], prep['ws_up'], prep['ws_down']

    def gathered(t0, n):
        return _gather_rows(ys, dest, t0, n)

    pieces = next(n for n in (4, 2, 1) if b % n == 0)
    nb = b // pieces
    for q in range(pieces):
        x = _combine(gathered(q * nb * s, nb * s), gw_t, 0, hp_all, x, gt2, q * nb, nb, ws_gate, ws_up, ws_down,
                     g_final, final, tm=min(512, s))
        if after_piece is not None:
            after_piece(x, q * nb, nb)
    if ctx_out:
        ctx_s = _combine(gathered(n_lat, b * sc), gw_t, n_lat, hp_all, ctx_s, cgt2, 0, b, ws_gate, ws_up, ws_down,
                         g_final, False, tm=min(256, sc))
    return x, ctx_s


def kernel(x, c, ctx, c_ctx, w_mod, b_mod, g_norm1, g_norm2, w_in, na_rpb, w_branch_na, conv_w, w_branch_sc,
           gla_gate_w, gla_gate_b, gla_norm_g, w_branch_gla, w_out, w_router, b_router, w_exp_gate, w_exp_up,
           w_exp_down, w_sh_gate, w_sh_up, w_sh_down, g_final):
    stacked = dict(g_norm1=g_norm1, g_norm2=g_norm2, na_rpb=na_rpb, w_branch_na=w_branch_na,
                   conv_w=conv_w, w_branch_sc=w_branch_sc, gla_gate_w=gla_gate_w, gla_gate_b=gla_gate_b,
                   gla_norm_g=gla_norm_g, w_branch_gla=w_branch_gla, w_out=w_out, w_router=w_router,
                   b_router=b_router,
                   w_sh_gate=w_sh_gate, w_sh_up=w_sh_up, w_sh_down=w_sh_down)
    depth = w_in.shape[0]
    rows = x.shape[1] // GRID_W
    layers = []
    for i in range(depth):
        p = {name: arr[i] for name, arr in stacked.items()}
        p.update(layer=i, w_exp_gate=w_exp_gate, w_exp_up=w_exp_up, w_exp_down=w_exp_down)
        mods, mods_ctx = _mod_vectors(c, c_ctx, w_mod, b_mod, i)
        p['prep'] = dict(
            w_in=_prep_w_in(w_in, i), mods=mods, mods_ctx=mods_ctx,
            na_bias=_na_bias_table(p['na_rpb'], rows, min(NA_WIN_R, rows)),
            gate_ws=_gla_gate_weights(p['gla_gate_w'], p['gla_gate_b']),
            w_na=p['w_branch_na'].astype(BF16), w_sc=p['w_branch_sc'].astype(BF16),
            w_gla=p['w_branch_gla'].astype(BF16), w_out=p['w_out'].astype(BF16),
            w_router=jnp.pad(p['w_router'], ((0, 0), (0, LANE - N_EXPERTS))).astype(BF16),
            ws_gate=p['w_sh_gate'].astype(BF16), ws_up=p['w_sh_up'].astype(BF16),
            ws_down=p['w_sh_down'].astype(BF16))
        layers.append(p)

    ctx_s = ctx
    projected = None
    for i, p in enumerate(layers):
        last = i == depth - 1
        p_next = None if last else layers[i + 1]
        after_piece = None
        next_projected = []
        if not last:
            def after_piece(xq, b0, nb, p_next=p_next, acc=next_projected):
                acc.append(_project_latent(xq, p_next, b0, nb, acc[-1] if acc else None))

        x, ctx_s = _layer(x, ctx_s, p, not last, last, g_final, projected, after_piece, p_next)
        projected = next_projected[-1] if next_projected else None
    return x
```

```python
import functools

import numpy as np
import jax
import jax.numpy as jnp
from jax import lax
from jax.experimental import pallas as pl
from jax.experimental.pallas import tpu as pltpu
from jax.experimental.pallas import tpu_sc as plsc

F32 = jnp.float32
BF16 = jnp.bfloat16
U32 = jnp.uint32

D_MODEL = 1024
N_MOD = 6
RMS_EPS = 1e-6
NEG_INF = -1e30
GRID_W = 64
NA_HEADS = 8
NA_HEAD_DIM = 64
NA_WIDTH = NA_HEADS * NA_HEAD_DIM
NA_WIN_R = 8
NA_WIN_C = 16
NA_GROUP = 4
SC_WIDTH = 512
GLA_HEADS = 4
GLA_KEY_WIDTH = 512
GLA_VAL_WIDTH = 1024
GLA_DK = GLA_KEY_WIDTH // GLA_HEADS
GLA_DV = GLA_VAL_WIDTH // GLA_HEADS
GLA_GATE_RANK = 16
GLA_GATE_TAU = 16.0
LOG2_E = 1.4426950408889634
N_EXPERTS = 64
N_EXPERT_GROUPS = 8
GROUP_SIZE = N_EXPERTS // N_EXPERT_GROUPS
TOPK_GROUPS = 4
TOP_K = 8
EXPERT_FF = 256
ROUTED_SCALE = 2.5
MOE_BLOCK = 1024

LANE = 128
GLA_C = 128
GLA_STEP_CHUNKS = 4
GLA_LEVELS = tuple(GLA_C >> (i + 1) for i in range(GLA_C.bit_length() - 1))
VMEM_LIMIT = 48 * 1024 * 1024
PROJ_TM, PROJ_TM_CTX, PROJ_TN = 2048, 1024, 1024
MERGE_TM, MERGE_TM_CTX = 512, 256
COMBINE_TM, COMBINE_TM_CTX = 512, 256
ROUTER_TMS = (1024, 512)
MOD_TN = 1536
SC_WINDOW = 128
SC_ROW = 256
SC_PARTS = D_MODEL // 2 // SC_ROW

OFF_V_GLA = 0
OFF_K_NA = 1024
OFF_V_NA = 1536
OFF_K_GLA = 2048
N_KV_MAIN = 2560
OFF_Q_NA = 2560
OFF_B_SC = 3072
OFF_C_SC = 3584
OFF_X_SC = 4096
OFF_Q_GLA = 4608
OFF_R_GLA = 5120
OFF_MERGE = 6144
N_MAIN = 9216


def _cparams(sem, vmem=VMEM_LIMIT):
    return pltpu.CompilerParams(dimension_semantics=sem, vmem_limit_bytes=vmem)


def _dot(a, b):
    return jnp.dot(a, b, preferred_element_type=F32)


def _dot_nt(a, b):
    return lax.dot_general(a, b, (((1,), (1,)), ((), ())), preferred_element_type=F32)


def _dot_tn(a, b):
    return lax.dot_general(a, b, (((0,), (0,)), ((), ())), preferred_element_type=F32)


def _sigmoid(x):
    return 0.5 * jnp.tanh(0.5 * x) + 0.5


def _pack_rows(x):
    n = x.shape[1] // 2
    r = x.astype(BF16).astype(F32)
    lo = pltpu.bitcast(r[:, :n], U32) >> 16
    hi = pltpu.bitcast(r[:, n:], U32)
    return hi | lo


def _store_parts(ref, words):
    for part in range(SC_PARTS):
        ref[part] = words[:, part * SC_ROW:(part + 1) * SC_ROW]


def _load_parts(ref, *lead):
    return jnp.concatenate([ref[(part,) + lead] for part in range(SC_PARTS)], axis=-1)


def _unpack_rows(w):
    lo = pltpu.bitcast(w << 16, F32)
    hi = pltpu.bitcast(w & jnp.uint32(0xFFFF0000), F32)
    return lo, hi


def _mod_kernel(a_ref, w_ref, b_ref, o_ref):
    a = a_ref[...]
    a = a * _sigmoid(a)
    o_ref[...] = _dot(a.astype(BF16), w_ref[0].astype(BF16)) + b_ref[0]


def _mod_vectors(c, c_ctx, w_mod, b_mod, layer):
    b = c.shape[0]
    rows = -(-(b + 1) // 8) * 8
    a = jnp.concatenate([c, c_ctx[None], jnp.zeros((rows - b - 1, D_MODEL), F32)], axis=0)
    n = N_MOD * D_MODEL
    tn = MOD_TN
    out = pl.pallas_call(
        _mod_kernel,
        grid=(n // tn,),
        in_specs=[pl.BlockSpec((rows, D_MODEL), lambda j: (0, 0)),
                  pl.BlockSpec((1, D_MODEL, tn), lambda j: (layer, 0, j)),
                  pl.BlockSpec((1, 1, tn), lambda j: (layer, 0, j))],
        out_specs=pl.BlockSpec((rows, tn), lambda j: (0, j)),
        out_shape=jax.ShapeDtypeStruct((rows, n), F32),
        compiler_params=_cparams(("parallel",)),
        name="mod_vectors",
    )(a, w_mod, b_mod[:, None])
    lat = out[:b].reshape(b, N_MOD, 1, D_MODEL)
    ctx = out[b].reshape(N_MOD, 1, 1, D_MODEL)
    return [lat[:, i] for i in range(N_MOD)], [ctx[i] for i in range(N_MOD)]


def _proj_kernel(x_ref, g_ref, sh_ref, sc_ref, w_ref, wg_ref, *rest):
    o_ref, og_ref, h_ref = rest[-3:]

    @pl.when(pl.program_id(2) == 0)
    def _():
        x = x_ref[0]
        ms = jnp.mean(x * x, axis=-1, keepdims=True)
        h = x * lax.rsqrt(ms + RMS_EPS) * g_ref[...] * (1.0 + sc_ref[0]) + sh_ref[0]
        hb = h.astype(BF16)
        h_ref[...] = hb
        og_ref[0] = _dot(hb, wg_ref[...])

    o_ref[0] = _dot(h_ref[...], w_ref[...]).astype(o_ref.dtype)


W_IN_TILE = 512
W_IN_GATE_SHIFT = 2 * GLA_GATE_RANK


def _prep_w_in_kernel(a_ref, b_ref, o_ref, g_ref):
    t = pl.program_id(0)
    first_lat = N_KV_MAIN // W_IN_TILE
    a = a_ref[0]

    @pl.when(t < first_lat)
    def _():
        o_ref[...] = a.T.astype(BF16)

    @pl.when(t >= first_lat)
    def _():
        moved = jnp.concatenate([a[W_IN_GATE_SHIFT:], b_ref[0]], axis=0)
        scale = jnp.where(t == first_lat, NA_HEAD_DIM ** -0.5, 1.0)
        o_ref[...] = (moved * scale).T.astype(BF16)

    @pl.when(t == first_lat)
    def _():
        head = a[:LANE]
        row = lax.broadcasted_iota(jnp.int32, head.shape, 0)
        g_ref[...] = jnp.where(row < W_IN_GATE_SHIFT, head, 0.0).T.astype(BF16)


def _prep_w_in(w_in, layer):
    d = w_in.shape[1]
    w_t = jnp.swapaxes(w_in, 1, 2)
    first_lat = N_KV_MAIN // W_IN_TILE
    kv_perm = OFF_K_NA // W_IN_TILE

    def a_map(t):
        return (layer, jnp.where(t < first_lat, (t + first_lat - kv_perm) % first_lat, t), 0)

    def b_map(t):
        return (layer, jnp.where(t < first_lat, 0, (t + 1) * (W_IN_TILE // W_IN_GATE_SHIFT)), 0)

    return pl.pallas_call(
        _prep_w_in_kernel,
        grid=(N_MAIN // W_IN_TILE,),
        in_specs=[pl.BlockSpec((1, W_IN_TILE, d), a_map),
                  pl.BlockSpec((1, W_IN_GATE_SHIFT, d), b_map)],
        out_specs=[pl.BlockSpec((d, W_IN_TILE), lambda t: (0, t)),
                   pl.BlockSpec((d, LANE), lambda t: (0, 0))],
        out_shape=[jax.ShapeDtypeStruct((d, N_MAIN), BF16), jax.ShapeDtypeStruct((d, LANE), BF16)],
        compiler_params=_cparams(("arbitrary",)),
        name="prep_w_in",
    )(w_t, w_t)


def _proj_in(x, g, shift, scale, w_main, w_gate, tm, tn, n=None, b0=0, nb=None, prev=None):
    b, s, d = x.shape
    n = w_main.shape[1] if n is None else n
    nb = b if nb is None else nb
    per_batch = shift.shape[0] == b
    mod_map = (lambda bi, i, j: (b0 + bi, 0, 0)) if per_batch else (lambda bi, i, j: (0, 0, 0))
    operands = (x, g[None], shift, scale, w_main, w_gate)
    extend = () if prev is None else tuple(prev)
    return pl.pallas_call(
        _proj_kernel,
        grid=(nb, s // tm, n // tn),
        in_specs=[pl.BlockSpec((1, tm, d), lambda bi, i, j: (b0 + bi, i, 0)),
                  pl.BlockSpec((1, d), lambda bi, i, j: (0, 0)),
                  pl.BlockSpec((1, 1, d), mod_map),
                  pl.BlockSpec((1, 1, d), mod_map),
                  pl.BlockSpec((d, tn), lambda bi, i, j: (0, j)),
                  pl.BlockSpec((d, LANE), lambda bi, i, j: (0, 0))]
                 + [pl.BlockSpec(memory_space=pl.ANY)] * len(extend),
        out_specs=[pl.BlockSpec((1, tm, tn), lambda bi, i, j: (b0 + bi, i, j)),
                   pl.BlockSpec((1, tm, LANE), lambda bi, i, j: (b0 + bi, i, 0))],
        out_shape=[jax.ShapeDtypeStruct((b, s, n), BF16),
                   jax.ShapeDtypeStruct((b, s, LANE), F32)],
        scratch_shapes=[pltpu.VMEM((tm, d), BF16)],
        input_output_aliases={len(operands): 0, len(operands) + 1: 1} if extend else {},
        compiler_params=_cparams(("parallel", "parallel", "arbitrary")),
        name="proj_in",
    )(*operands, *extend)


def _softmax_av(q, keys, vals, biases):
    scores = []
    for kk, bb in zip(keys, biases):
        s = _dot_nt(q, kk)
        scores.append(s if bb is None else s + bb)
    m = scores[0].max(axis=-1, keepdims=True)
    for s in scores[1:]:
        m = jnp.maximum(m, s.max(axis=-1, keepdims=True))
    num = None
    den = None
    for s, vv in zip(scores, vals):
        e = jnp.exp(s - m)
        dsum = e.sum(axis=-1, keepdims=True)
        o = _dot(e.astype(BF16), vv)
        num = o if num is None else num + o
        den = dsum if den is None else den + dsum
    return num / den


def _na_kernel(q_ref, k_ref, v_ref, kc_ref, vc_ref, *rest, rows, kr):
    *bias_refs, o_ref = rest
    kc = kc_ref[0]
    vc = vc_ref[0]
    for j, bias_ref in enumerate(bias_refs):
        r = pl.program_id(1) * len(bias_refs) + j
        row_start = jnp.clip(r - kr // 2, 0, rows - kr)
        start = pl.multiple_of(row_start * GRID_W, GRID_W)
        n_win = kr * GRID_W
        q = q_ref[0, j * GRID_W:(j + 1) * GRID_W, :]
        kw = k_ref[0, pl.ds(start, n_win), :]
        vw = v_ref[0, pl.ds(start, n_win), :]
        o_ref[0, j * GRID_W:(j + 1) * GRID_W, :] = _na_row(q, kw, vw, kc, vc, bias_ref).astype(o_ref.dtype)


def _na_row(q, kw, vw, kc, vc, bias_ref):
    gw = NA_GROUP * NA_HEAD_DIM
    stacked = (NA_GROUP * GRID_W, gw)
    on_head = (lax.broadcasted_iota(jnp.int32, stacked, 0) // GRID_W
               == lax.broadcasted_iota(jnp.int32, stacked, 1) // NA_HEAD_DIM)
    outs = []
    for g in range(NA_HEADS // NA_GROUP):
        sl = slice(g * gw, (g + 1) * gw)
        q_all = jnp.where(on_head, jnp.concatenate([q[:, sl]] * NA_GROUP, axis=0), jnp.zeros((), q.dtype))
        bias = bias_ref[0, g * NA_GROUP * GRID_W:(g + 1) * NA_GROUP * GRID_W, :]
        o_all = _softmax_av(q_all, [kw[:, sl], kc[:, sl]], [vw[:, sl], vc[:, sl]], [bias, None])
        o_all = jnp.where(on_head, o_all, 0.0).reshape(NA_GROUP, GRID_W, gw)
        outs.append(o_all.sum(axis=0))
    return jnp.concatenate(outs, axis=-1)


def _na_bias_table(rpb, rows, kr):
    col = np.arange(GRID_W)
    col_start = np.clip(col - NA_WIN_C // 2, 0, GRID_W - NA_WIN_C)
    col_ok = (col[None, :] >= col_start[:, None]) & (col[None, :] < col_start[:, None] + NA_WIN_C)
    d_col = np.clip(col[None, :] - col[:, None], -(NA_WIN_C - 1), NA_WIN_C - 1) + NA_WIN_C - 1
    n_dr, n_dc = rpb.shape[1], rpb.shape[2]
    onehot = jnp.asarray((d_col.reshape(-1)[None, :] == np.arange(n_dc)[:, None]).astype(np.float32))
    by_col = jnp.dot(rpb.astype(F32).reshape(NA_HEADS * n_dr, n_dc), onehot, precision=lax.Precision.HIGHEST)
    by_col = by_col.reshape(NA_HEADS, n_dr, GRID_W, GRID_W)
    by_col = jnp.where(col_ok[None, None], by_col, NEG_INF)
    tables = []
    for o in range(kr):
        lo = NA_WIN_R - 1 - o
        tables.append(by_col[:, lo:lo + kr].transpose(0, 2, 1, 3).reshape(NA_HEADS, GRID_W, kr * GRID_W))
    return jnp.stack(tables).reshape(kr, NA_HEADS * GRID_W, kr * GRID_W)


def _na_latent(main, main_ctx, bias):
    b, s, _ = main.shape
    sc = main_ctx.shape[1]
    rows = s // GRID_W
    kr = min(NA_WIN_R, rows)
    w = NA_WIDTH

    per_step = next(n for n in (8, 4, 2, 1) if rows % n == 0)

    def bias_spec(j):
        def bias_map(bi, i):
            r = i * per_step + j
            return (r - jnp.clip(r - kr // 2, 0, rows - kr), 0, 0)
        return pl.BlockSpec((1, NA_HEADS * GRID_W, kr * GRID_W), bias_map)

    return pl.pallas_call(
        functools.partial(_na_kernel, rows=rows, kr=kr),
        grid=(b, rows // per_step),
        in_specs=[pl.BlockSpec((1, per_step * GRID_W, w), lambda bi, i: (bi, i, OFF_Q_NA // w)),
                  pl.BlockSpec((1, s, w), lambda bi, i: (bi, 0, OFF_K_NA // w)),
                  pl.BlockSpec((1, s, w), lambda bi, i: (bi, 0, OFF_V_NA // w)),
                  pl.BlockSpec((1, sc, w), lambda bi, i: (bi, 0, OFF_K_NA // w)),
                  pl.BlockSpec((1, sc, w), lambda bi, i: (bi, 0, OFF_V_NA // w))]
                 + [bias_spec(j) for j in range(per_step)],
        out_specs=pl.BlockSpec((1, per_step * GRID_W, w), lambda bi, i: (bi, i, 0)),
        out_shape=jax.ShapeDtypeStruct((b, s, w), BF16),
        compiler_params=_cparams(("parallel", "arbitrary")),
        name="na_latent",
    )(main, main, main, main_ctx, main_ctx, *([bias] * per_step))


def _dense_attn_kernel(q_ref, k_ref, v_ref, o_ref):
    q = q_ref[0]
    k = k_ref[0]
    v = v_ref[0]
    outs = []
    for h in range(NA_HEADS):
        sl = slice(h * NA_HEAD_DIM, (h + 1) * NA_HEAD_DIM)
        outs.append(_softmax_av(q[:, sl], [k[:, sl]], [v[:, sl]], [None]))
    o_ref[0] = jnp.concatenate(outs, axis=-1).astype(o_ref.dtype)


def _dense_attn(main_ctx):
    b, sc, _ = main_ctx.shape
    w = NA_WIDTH
    return pl.pallas_call(
        _dense_attn_kernel,
        grid=(b,),
        in_specs=[pl.BlockSpec((1, sc, w), lambda bi: (bi, 0, OFF_Q_NA // w)),
                  pl.BlockSpec((1, sc, w), lambda bi: (bi, 0, OFF_K_NA // w)),
                  pl.BlockSpec((1, sc, w), lambda bi: (bi, 0, OFF_V_NA // w))],
        out_specs=pl.BlockSpec((1, sc, w), lambda bi: (bi, 0, 0)),
        out_shape=jax.ShapeDtypeStruct((b, sc, w), BF16),
        compiler_params=_cparams(("parallel",)),
        name="ctx_attn",
    )(main_ctx, main_ctx, main_ctx)


HALO = 16


def _gla_matrices(reverse):
    c = GLA_C
    t = np.arange(c)[:, None]
    m = np.arange(c)[None, :]
    blocks = [m <= t, m > t]
    for b in GLA_LEVELS:
        first = (t // (2 * b)) * (2 * b) + b
        is_q = (t & b) != 0
        blocks.append(np.where(is_q, (m > first) & (m <= t), (m > t) & (m <= first)))
    mats = np.stack(blocks).astype(np.float32)
    if reverse:
        mats = mats[:, ::-1, ::-1]
    mats = mats.reshape(-1, c)
    return jnp.asarray(np.concatenate([mats, mats], axis=1), dtype=BF16)


def _split_bf16(x):
    hi = x.astype(BF16)
    return hi, (x - hi.astype(F32)).astype(BF16)


def _pair_block_diag(x):
    lane = lax.broadcasted_iota(jnp.int32, x.shape, 1)
    zero = jnp.zeros((), x.dtype)
    return jnp.concatenate([jnp.where(lane < GLA_DK, x, zero), jnp.where(lane >= GLA_DK, x, zero)], axis=0)


def _gla_kernel(*refs, emit):
    n_in = 9 if emit else 8
    n_out = 2 if emit else 1
    ins, outs, scratch = refs[:2 * n_in], refs[2 * n_in:2 * (n_in + n_out)], refs[2 * (n_in + n_out):]
    ins = [ins[d * n_in:(d + 1) * n_in] for d in range(2)]
    outs = [outs[d * n_out:(d + 1) * n_out] for d in range(2)]
    step = pl.program_id(1)

    @pl.when(step == 0)
    def _():
        for d in range(2):
            scratch[d][...] = ins[d][-1][0]

    pending = [_gla_direction(ins[d][:-1], outs[d][:-1], scratch[d], reverse, emit)
               for d, reverse in enumerate((False, True))]
    while pending:
        pending = [stages for stages in pending if next(stages, "done") != "done"]

    @pl.when(step == pl.num_programs(1) - 1)
    def _():
        for d in range(2):
            outs[d][-1][0] = scratch[d][...]


def _gla_direction(ins, outs, st_ref, reverse, emit):
    gate_ref = ins[-5]
    order = range(gate_ref.shape[1] // GLA_C)
    for sub in (reversed(order) if reverse else order):
        yield from _gla_chunk(ins, outs, st_ref, reverse, emit, slice(sub * GLA_C, (sub + 1) * GLA_C))


def _gla_chunk(ins, outs, st_ref, reverse, emit, rows):
    if emit:
        q_ref, k_ref, v_ref, gt_ref, w2a_ref, w2b_ref, b2_ref, a_ref = ins
        (o_ref,) = outs
    else:
        k_ref, v_ref, gt_ref, w2a_ref, w2b_ref, b2_ref, a_ref = ins
    c = GLA_C
    pw = 2 * GLA_DK

    lr_hi, lr_lo = _split_bf16(gt_ref[0, rows])
    logit = (_dot(jnp.concatenate([lr_hi, lr_lo], axis=1), w2a_ref[...]) + _dot(lr_hi, w2b_ref[...])
             + b2_ref[...])
    g = (jnp.minimum(logit, 0.0) - jnp.log1p(jnp.exp(-jnp.abs(logit)))) * (LOG2_E / GLA_GATE_TAU)
    g_hi, g_lo = _split_bf16(g)
    args = _dot(a_ref[...], jnp.concatenate([g_hi, g_lo], axis=0))
    cum = args[0:c]
    rem = args[c:2 * c]
    last_row = cum[0:1] if reverse else cum[c - 1:c]
    yield

    k = k_ref[0, rows].astype(F32)
    v = v_ref[0, rows]
    atts = []
    if emit:
        q = q_ref[0, rows].astype(F32) * (GLA_DK ** -0.5)
        row_t = lax.broadcasted_iota(jnp.int32, (c, pw), 0)
        si = lax.broadcasted_iota(jnp.int32, (2 * c, c), 0) & (c - 1)
        ti = lax.broadcasted_iota(jnp.int32, (2 * c, c), 1)
        if reverse:
            row_t, ti, si = c - 1 - row_t, c - 1 - ti, c - 1 - si
        for hp in range(GLA_HEADS // 2):
            cs = slice(hp * pw, (hp + 1) * pw)
            qp, kp = q[:, cs], k[:, cs]
            att = jnp.where(ti == si, _dot_nt(_pair_block_diag(kp.astype(BF16)), qp.astype(BF16)), 0.0)
            for l, b in enumerate(GLA_LEVELS):
                x = (jnp.exp2(args[(2 + l) * c:(3 + l) * c, cs])
                     * jnp.where((row_t & b) != 0, qp, kp)).astype(BF16)
                pair = (((ti ^ si) >> (b.bit_length() - 1)) == 1) & ((ti & b) != 0)
                att = jnp.where(pair, _dot_nt(_pair_block_diag(x), x), att)
                yield
            atts.append(att.astype(BF16))

    outs = []
    for h in range(GLA_HEADS):
        sl = slice(h * GLA_DK, (h + 1) * GLA_DK)
        kh = k[:, sl]
        vh = v[:, h * GLA_DV:(h + 1) * GLA_DV]
        state = st_ref[h]
        kd = (kh * jnp.exp2(rem[:, sl])).astype(BF16)
        decay = jnp.exp2(jnp.broadcast_to(last_row[:, sl], (GLA_DK, GLA_DK))).T
        decay = jnp.concatenate([decay] * (GLA_DV // GLA_DK), axis=1)
        if emit:
            qd = (q[:, sl] * jnp.exp2(cum[:, sl])).astype(BF16)
            att_t = atts[h // 2][(h % 2) * c:(h % 2 + 1) * c]
            both = _dot_tn(jnp.concatenate([att_t, kd], axis=1), vh)
            outs.append(_dot(qd, state.astype(BF16)) + both[:c])
            st_ref[h] = decay * state + both[c:]
        else:
            st_ref[h] = decay * state + _dot_tn(kd, vh)
        yield

    if emit:
        o_ref[0, rows] = jnp.concatenate(outs, axis=-1).astype(o_ref.dtype)


def _gla_scan(main, gate, gate_ws, s0s, emit):
    b, l, _ = main.shape
    step_rows = GLA_C * min(GLA_STEP_CHUNKS, l // GLA_C)
    n = l // step_rows
    const = lambda arr: pl.BlockSpec(arr.shape, lambda bi, s: (0,) * arr.ndim)
    state_spec = pl.BlockSpec((1, GLA_HEADS, GLA_DK, GLA_DV), lambda bi, s: (bi, 0, 0, 0))
    state_shape = jax.ShapeDtypeStruct((b, GLA_HEADS, GLA_DK, GLA_DV), F32)
    in_specs, args, out_specs, out_shape = [], [], [], []
    for reverse in (False, True):
        amat = _gla_matrices(reverse)
        w2a, w2b, b2 = gate_ws[reverse]

        def col(block, reverse=reverse):
            return lambda bi, s: (bi, n - 1 - s if reverse else s, block)

        if emit:
            in_specs.append(pl.BlockSpec((1, step_rows, GLA_KEY_WIDTH), col(OFF_Q_GLA // GLA_KEY_WIDTH)))
            args.append(main)
            out_specs.append(pl.BlockSpec((1, step_rows, GLA_VAL_WIDTH), col(0)))
            out_shape.append(jax.ShapeDtypeStruct((b, l, GLA_VAL_WIDTH), BF16))
        in_specs += [pl.BlockSpec((1, step_rows, GLA_KEY_WIDTH), col(OFF_K_GLA // GLA_KEY_WIDTH)),
                     pl.BlockSpec((1, step_rows, GLA_VAL_WIDTH), col(OFF_V_GLA // GLA_VAL_WIDTH)),
                     pl.BlockSpec((1, step_rows, LANE), col(0)),
                     const(w2a), const(w2b), const(b2), const(amat), state_spec]
        args += [main, main, gate, w2a, w2b, b2, amat, s0s[reverse]]
        out_specs.append(state_spec)
        out_shape.append(state_shape)
    res = pl.pallas_call(
        functools.partial(_gla_kernel, emit=emit),
        grid=(b, n),
        in_specs=in_specs,
        out_specs=out_specs,
        out_shape=out_shape,
        scratch_shapes=[pltpu.VMEM((GLA_HEADS, GLA_DK, GLA_DV), F32)] * 2,
        compiler_params=_cparams(("parallel", "arbitrary")),
        name="gla_scan",
    )(*args)
    return (res[0], res[2], res[1], res[3]) if emit else (None, None, res[0], res[1])


def _gla_gate_weights(gate_w, gate_b):
    out = []
    for dr in range(2):
        w = jnp.zeros((LANE, GLA_KEY_WIDTH), F32)
        w = w.at[dr * GLA_GATE_RANK:(dr + 1) * GLA_GATE_RANK].set(gate_w[dr])
        w_hi = w.astype(BF16)
        w_lo = (w - w_hi.astype(F32)).astype(BF16)
        out.append((jnp.concatenate([w_hi, w_hi], axis=0), w_lo, gate_b[dr][None]))
    return out


def _conv_tile(b_ref, c_ref, x_ref, cp_ref, xp_ref, cn_ref, xn_ref, w_ref):
    i, n = pl.program_id(1), pl.num_programs(1)
    u = c_ref[0].astype(F32) * x_ref[0].astype(F32)
    tm = u.shape[0]
    before = jnp.where(i > 0, 1.0, 0.0) * (cp_ref[0, HALO - 1:HALO].astype(F32) * xp_ref[0, HALO - 1:HALO].astype(F32))
    after = jnp.where(i < n - 1, 1.0, 0.0) * (cn_ref[0, 0:1].astype(F32) * xn_ref[0, 0:1].astype(F32))
    t = lax.broadcasted_iota(jnp.int32, u.shape, 0)
    prev = jnp.where(t == 0, before, pltpu.roll(u, 1, axis=0))
    nxt = jnp.where(t == tm - 1, after, pltpu.roll(u, tm - 1, axis=0))
    w = w_ref[...]
    return b_ref[0].astype(F32) * (prev * w[0:1] + u * w[1:2] + nxt * w[2:3])


def _merge_kernel(ona_ref, bsc_ref, csc_ref, xsc_ref, cp_ref, xp_ref, cn_ref, xn_ref, cw_ref, of_ref, ob_ref,
                  r_ref, gna_ref, gsc_ref, ggl_ref, x_ref, gt_ref,
                  gn_ref, wna_ref, wsc_ref, wgl_ref, wo_ref, g2_ref, sh2_ref, sc2_ref, wr_ref, *rest):
    xo_ref, hp_ref, lg_ref = rest[-3:]
    o_sc = _conv_tile(bsc_ref, csc_ref, xsc_ref, cp_ref, xp_ref, cn_ref, xn_ref, cw_ref).astype(BF16)
    o = of_ref[0].astype(F32) + ob_ref[0].astype(F32)
    normed = []
    for h in range(GLA_HEADS):
        oh = o[:, h * GLA_DV:(h + 1) * GLA_DV]
        ms = jnp.mean(oh * oh, axis=-1, keepdims=True)
        normed.append(oh * lax.rsqrt(ms + RMS_EPS))
    r = r_ref[0].astype(F32)
    y_gla = jnp.concatenate(normed, axis=-1) * gn_ref[...] * (r * _sigmoid(r))
    y = (_sigmoid(gna_ref[0].astype(F32)) * _dot(ona_ref[0], wna_ref[...])
         + _sigmoid(gsc_ref[0].astype(F32)) * _dot(o_sc, wsc_ref[...])
         + _sigmoid(ggl_ref[0].astype(F32)) * _dot(y_gla.astype(BF16), wgl_ref[...]))
    xn = x_ref[0] + gt_ref[0] * _dot(y.astype(BF16), wo_ref[...])
    xo_ref[0] = xn
    ms = jnp.mean(xn * xn, axis=-1, keepdims=True)
    h2 = xn * lax.rsqrt(ms + RMS_EPS) * g2_ref[...] * (1.0 + sc2_ref[0]) + sh2_ref[0]
    _store_parts(hp_ref, _pack_rows(h2))
    lg_ref[...] = _dot(h2.astype(BF16), wr_ref[...])


def _merge(o_na, conv_w, o_f, o_b, main, x, gt1, gn, w_na, w_sc, w_gla, w_out, g2, sh2, sc2, w_router, tm,
           n_routed, tok_off, routed=None):
    b, s, d = x.shape
    per_batch = gt1.shape[0] == b
    mod_map = (lambda bi, i: (bi, 0, 0)) if per_batch else (lambda bi, i: (0, 0, 0))
    tok = lambda width, blk: pl.BlockSpec((1, tm, width), lambda bi, i: (bi, i, blk))
    full = lambda arr: pl.BlockSpec(arr.shape, lambda bi, i: (0,) * arr.ndim)
    mod = pl.BlockSpec((1, 1, d), mod_map)
    gn_t = jnp.tile(gn, GLA_HEADS)[None]
    g2_t = g2[None]
    per_tile = tm // HALO
    last_halo = s // HALO - 1
    halo_prev = lambda blk: pl.BlockSpec(
        (1, HALO, SC_WIDTH), lambda bi, i: (bi, jnp.maximum(i * per_tile - 1, 0), blk))
    halo_next = lambda blk: pl.BlockSpec(
        (1, HALO, SC_WIDTH), lambda bi, i: (bi, jnp.minimum((i + 1) * per_tile, last_halo), blk))
    col_b, col_c, col_x = OFF_B_SC // SC_WIDTH, OFF_C_SC // SC_WIDTH, OFF_X_SC // SC_WIDTH
    extend = () if routed is None else tuple(routed)
    operands = (o_na, main, main, main, main, main, main, main, conv_w, o_f, o_b, main, main, main, main, x, gt1,
                gn_t, w_na, w_sc, w_gla, w_out, g2_t, sh2, sc2, w_router)
    n_in = len(operands)
    nblk = s // tm
    blk0 = tok_off // tm
    return pl.pallas_call(
        _merge_kernel,
        grid=(b, s // tm),
        in_specs=[tok(NA_WIDTH, 0), tok(SC_WIDTH, col_b), tok(SC_WIDTH, col_c), tok(SC_WIDTH, col_x),
                  halo_prev(col_c), halo_prev(col_x), halo_next(col_c), halo_next(col_x), full(conv_w),
                  tok(GLA_VAL_WIDTH, 0), tok(GLA_VAL_WIDTH, 0),
                  tok(d, OFF_R_GLA // d), tok(d, OFF_MERGE // d), tok(d, OFF_MERGE // d + 1),
                  tok(d, OFF_MERGE // d + 2), tok(d, 0), mod,
                  full(gn_t), full(w_na), full(w_sc), full(w_gla), full(w_out), full(g2_t), mod, mod,
                  full(w_router)] + [pl.BlockSpec(memory_space=pl.ANY)] * len(extend),
        out_specs=[tok(d, 0),
                   pl.BlockSpec((SC_PARTS, tm, SC_ROW), lambda bi, i: (0, blk0 + bi * nblk + i, 0)),
                   pl.BlockSpec((tm, LANE), lambda bi, i: (blk0 + bi * nblk + i, 0))],
        out_shape=[jax.ShapeDtypeStruct((b, s, d), F32),
                   jax.ShapeDtypeStruct((SC_PARTS, n_routed, SC_ROW), U32),
                   jax.ShapeDtypeStruct((n_routed, LANE), F32)],
        input_output_aliases={n_in: 1, n_in + 1: 2} if extend else {},
        compiler_params=_cparams(("parallel", "parallel")),
        name="merge",
    )(*operands, *extend)


def _router_kernel(lg_ref, br_ref, tri_ref, eidx_ref, gw_ref, rank_ref, cnt_ref, carry_ref):
    tm = lg_ref.shape[0]

    @pl.when(pl.program_id(0) == 0)
    def _():
        carry_ref[...] = jnp.zeros_like(carry_ref)

    scores = _sigmoid(lg_ref[...].T[:N_EXPERTS])
    sel = scores + br_ref[...]
    neg = -jnp.inf

    sel3 = sel.reshape(N_EXPERT_GROUPS, GROUP_SIZE, tm)
    i3 = lax.broadcasted_iota(jnp.int32, sel3.shape, 1)
    m1 = sel3.max(axis=1, keepdims=True)
    first = jnp.where(sel3 == m1, i3, GROUP_SIZE).min(axis=1, keepdims=True)
    m2 = jnp.where(i3 == first, neg, sel3).max(axis=1, keepdims=True)
    gscore = (m1 + m2)[:, 0, :]

    gi = lax.broadcasted_iota(jnp.int32, gscore.shape, 0)
    gmask = jnp.zeros(gscore.shape, jnp.bool_)
    for _ in range(TOPK_GROUPS):
        m = gscore.max(axis=0, keepdims=True)
        pick = gi == jnp.where(gscore == m, gi, N_EXPERT_GROUPS).min(axis=0, keepdims=True)
        gmask = gmask | pick
        gscore = jnp.where(pick, neg, gscore)
    emask = jnp.broadcast_to(gmask[:, None, :], sel3.shape).reshape(N_EXPERTS, tm)
    sel = jnp.where(emask, sel, neg)

    ei = lax.broadcasted_iota(jnp.int32, sel.shape, 0)
    picks, idxs, ws = [], [], []
    for _ in range(TOP_K):
        m = sel.max(axis=0, keepdims=True)
        idx = jnp.where(sel == m, ei, N_EXPERTS).min(axis=0, keepdims=True)
        pick = ei == idx
        picks.append(pick)
        idxs.append(idx)
        ws.append(jnp.where(pick, scores, 0.0).sum(axis=0, keepdims=True))
        sel = jnp.where(pick, neg, sel)
    w = jnp.concatenate(ws, axis=0)
    gw_ref[...] = (w / w.sum(axis=0, keepdims=True) * ROUTED_SCALE).T
    eidx_ref[...] = jnp.concatenate(idxs, axis=0)

    onehot = picks[0]
    for p in picks[1:]:
        onehot = onehot | p
    onehot = jnp.where(onehot, 1.0, 0.0).astype(BF16)
    before = _dot(onehot, tri_ref[...]) + jnp.tile(carry_ref[...], (1, tm // LANE))
    rank_ref[...] = jnp.concatenate(
        [jnp.where(p, before, 0.0).sum(axis=0, keepdims=True) for p in picks], axis=0).astype(jnp.int32)
    carry_ref[...] += _dot(onehot, jnp.ones((tm, LANE), BF16))
    cnt_ref[...] = carry_ref[...]


def _route(logits, b_router):
    t = logits.shape[0]
    tm = next(n for n in ROUTER_TMS if t % n == 0)
    br = jnp.broadcast_to(b_router.astype(F32)[:, None], (N_EXPERTS, tm))
    tri = jnp.asarray(np.triu(np.ones((tm, tm), np.float32), 1), dtype=BF16)
    kt = lambda dt: jax.ShapeDtypeStruct((TOP_K, t), dt)
    eidx, gw, rank, cnt = pl.pallas_call(
        _router_kernel,
        grid=(t // tm,),
        in_specs=[pl.BlockSpec((tm, LANE), lambda i: (i, 0)),
                  pl.BlockSpec((N_EXPERTS, tm), lambda i: (0, 0)),
                  pl.BlockSpec((tm, tm), lambda i: (0, 0))],
        out_specs=[pl.BlockSpec((TOP_K, tm), lambda i: (0, i)),
                   pl.BlockSpec((tm, TOP_K), lambda i: (i, 0)),
                   pl.BlockSpec((TOP_K, tm), lambda i: (0, i)),
                   pl.BlockSpec((N_EXPERTS, LANE), lambda i: (0, 0))],
        out_shape=[kt(jnp.int32), jax.ShapeDtypeStruct((t, TOP_K), F32), kt(jnp.int32),
                   jax.ShapeDtypeStruct((N_EXPERTS, LANE), F32)],
        scratch_shapes=[pltpu.VMEM((N_EXPERTS, LANE), F32)],
        compiler_params=_cparams(("arbitrary",)),
        name="router",
    )(logits, br, tri)
    return eidx, gw, rank, cnt[:, 0].astype(jnp.int32)


def _sc_mesh():
    return plsc.VectorSubcoreMesh(core_axis_name="core", subcore_axis_name="subcore")


def _dispatch_rows(xp, dest, slots):
    parts, t, _ = xp.shape

    @pl.kernel(out_type=jax.ShapeDtypeStruct((parts, slots, SC_ROW), xp.dtype), mesh=_sc_mesh(),
               scratch_types=[], name="moe_dispatch")
    def run(x_hbm, d_hbm, o_hbm):
        for part in range(parts):
            out_part = o_hbm.at[part]

            def body(x_vmem, d_vmem, out_part=out_part):
                for k in range(TOP_K):
                    pltpu.sync_copy(x_vmem, out_part.at[d_vmem.at[k]])

            pltpu.emit_pipeline(
                body,
                grid=(t // SC_WINDOW,),
                in_specs=[pl.BlockSpec((SC_WINDOW, SC_ROW), lambda i: (i, 0)),
                          pl.BlockSpec((TOP_K, SC_WINDOW), lambda i: (0, i))],
                out_specs=[],
                core_axis_name=("core", "subcore"),
                dimension_semantics=(pltpu.PARALLEL,),
            )(x_hbm.at[part], d_hbm)

    return run(xp, dest)


def _gather_rows(yp, dest, t0, n):
    parts = yp.shape[0]
    nwin = n // SC_WINDOW
    win0 = t0 // SC_WINDOW

    @pl.kernel(out_type=jax.ShapeDtypeStruct((parts, TOP_K * n, SC_ROW), yp.dtype), mesh=_sc_mesh(),
               scratch_types=[], name="moe_gather")
    def run(y_hbm, d_hbm, o_hbm):
        for part in range(parts):
            table = y_hbm.at[part]

            def body(d_vmem, o_vmem, table=table):
                pltpu.sync_copy(table.at[d_vmem.at[0]], o_vmem)

            pltpu.emit_pipeline(
                body,
                grid=(TOP_K, nwin),
                in_specs=[pl.BlockSpec((1, SC_WINDOW), lambda k, j: (k, win0 + j))],
                out_specs=[pl.BlockSpec((SC_WINDOW, SC_ROW), lambda k, j: (k * nwin + j, 0))],
                core_axis_name=("core", "subcore"),
                dimension_semantics=(pltpu.PARALLEL, pltpu.PARALLEL),
            )(d_hbm, o_hbm.at[part])

    return run(yp, dest).reshape(parts, TOP_K, n, SC_ROW)


def _expert_kernel(be_ref, bv_ref, bs_ref, nx_ref, sl_ref, x_ref, wg_hbm, wu_hbm, wd_hbm, o_ref,
                   wg_f, wu_f, wd_f, wg_s, wu_s, wd_s, sems, *, layer):
    i = pl.program_id(0)
    valid = bv_ref[i]
    expert = be_ref[i]
    new_expert = (i == 0) | (expert != be_ref[jnp.maximum(i - 1, 0)])
    slot = sl_ref[i]

    def fetch(which, into):
        return [pltpu.make_async_copy(src.at[layer, which], dst.at[into], sems.at[into, j])
                for j, (src, dst) in enumerate(((wg_hbm, wg_f), (wu_hbm, wu_f), (wd_hbm, wd_f)))]

    @pl.when(i == 0)
    def _():
        for cp in fetch(expert, slot):
            cp.start()

    @pl.when(new_expert)
    def _():
        for cp in fetch(expert, slot):
            cp.wait()
        upcoming = nx_ref[i]

        @pl.when(upcoming >= 0)
        def _():
            for cp in fetch(upcoming, 1 - slot):
                cp.start()

        wg_s[...] = wg_f[slot].astype(BF16)
        wu_s[...] = wu_f[slot].astype(BF16)
        wd_s[...] = wd_f[slot].astype(BF16)

    @pl.when(valid > 0)
    def _():
        w = _load_parts(x_ref)
        row = lax.broadcasted_iota(jnp.int32, w.shape, 0)
        w = jnp.where(row < valid, w, jnp.uint32(0))
        lo, hi = _unpack_rows(w)
        x = jnp.concatenate([lo, hi], axis=1).astype(BF16)
        a = _dot(x, wg_s[...])
        hid = a * _sigmoid(a) * _dot(x, wu_s[...])
        _store_parts(o_ref, _pack_rows(_dot(hid.astype(BF16), wd_s[...])))


def _experts(xs, blk_e, blk_valid, blk_src, blk_next, blk_slot, layer, w_gate, w_up, w_down):
    parts, slots, _ = xs.shape
    d = D_MODEL
    nb = slots // MOE_BLOCK
    data = pl.BlockSpec((parts, MOE_BLOCK, SC_ROW), lambda i, be, bv, bs, nx, sl: (0, bs[i], 0))
    stage = lambda shape: pltpu.VMEM((2,) + shape, F32)
    return pl.pallas_call(
        functools.partial(_expert_kernel, layer=layer),
        grid_spec=pltpu.PrefetchScalarGridSpec(
            num_scalar_prefetch=5,
            grid=(nb,),
            in_specs=[data] + [pl.BlockSpec(memory_space=pl.ANY)] * 3,
            out_specs=data,
            scratch_shapes=[stage((d, EXPERT_FF)), stage((d, EXPERT_FF)), stage((EXPERT_FF, d)),
                            pltpu.VMEM((d, EXPERT_FF), BF16), pltpu.VMEM((d, EXPERT_FF), BF16),
                            pltpu.VMEM((EXPERT_FF, d), BF16), pltpu.SemaphoreType.DMA((2, 3))]),
        out_shape=jax.ShapeDtypeStruct((parts, slots, SC_ROW), U32),
        compiler_params=_cparams(("arbitrary",)),
        name="experts",
    )(blk_e, blk_valid, blk_src, blk_next, blk_slot, xs, w_gate, w_up, w_down)


def _combine_kernel(yg_ref, gw_ref, hp_ref, x_ref, gt_ref, wsg_ref, wsu_ref, wsd_ref, gf_ref, o_ref, *, final):
    h = jnp.concatenate(_unpack_rows(_load_parts(hp_ref)), axis=1).astype(BF16)
    a = _dot(h, wsg_ref[...])
    hid = a * _sigmoid(a) * _dot(h, wsu_ref[...])
    y = _dot(hid.astype(BF16), wsd_ref[...])
    gw = gw_ref[...]
    y_lo = y[:, :D_MODEL // 2]
    y_hi = y[:, D_MODEL // 2:]
    for k in range(TOP_K):
        lo, hi = _unpack_rows(_load_parts(yg_ref, k))
        y_lo = y_lo + gw[:, k:k + 1] * lo
        y_hi = y_hi + gw[:, k:k + 1] * hi
    y = jnp.concatenate([y_lo, y_hi], axis=1)
    xn = x_ref[0] + gt_ref[0] * y
    if final:
        ms = jnp.mean(xn * xn, axis=-1, keepdims=True)
        xn = xn * lax.rsqrt(ms + RMS_EPS) * gf_ref[...]
    o_ref[0] = xn


def _combine(yg, gw, tok_off, hp, x, gt2, b0, nb, ws_gate, ws_up, ws_down, g_final, final, tm):
    b, s, d = x.shape
    per_batch = gt2.shape[0] == b
    mod_map = (lambda bi, i: (b0 + bi, 0, 0)) if per_batch else (lambda bi, i: (0, 0, 0))
    full = lambda arr: pl.BlockSpec(arr.shape, lambda bi, i: (0,) * arr.ndim)
    tok = lambda width: pl.BlockSpec((1, tm, width), lambda bi, i: (b0 + bi, i, 0))
    gf = g_final[None]
    nblk = s // tm
    blk0 = (tok_off + b0 * s) // tm
    return pl.pallas_call(
        functools.partial(_combine_kernel, final=final),
        grid=(nb, nblk),
        in_specs=[pl.BlockSpec((SC_PARTS, TOP_K, tm, SC_ROW), lambda bi, i: (0, 0, bi * nblk + i, 0)),
                  pl.BlockSpec((tm, TOP_K), lambda bi, i: (blk0 + bi * nblk + i, 0)),
                  pl.BlockSpec((SC_PARTS, tm, SC_ROW), lambda bi, i: (0, blk0 + bi * nblk + i, 0)),
                  tok(d),
                  pl.BlockSpec((1, 1, d), mod_map),
                  full(ws_gate), full(ws_up), full(ws_down), full(gf)],
        out_specs=tok(d),
        out_shape=jax.ShapeDtypeStruct((b, s, d), F32),
        input_output_aliases={3: 0},
        compiler_params=_cparams(("parallel", "parallel")),
        name="combine",
    )(yg, gw, hp, x, gt2, ws_gate, ws_up, ws_down, gf)


def _project_latent(x, p, b0=0, nb=None, prev=None):
    w_main, w_gate = p['prep']['w_in']
    mods = p['prep']['mods']
    return _proj_in(x, p['g_norm1'], mods[0], mods[1], w_main, w_gate, tm=min(PROJ_TM, x.shape[1]), tn=PROJ_TN,
                    b0=b0, nb=nb, prev=prev)


def _layer(x, ctx_s, p, ctx_out, final, g_final, projected=None, after_piece=None, p_next=None):
    b, s, d = x.shape
    sc = ctx_s.shape[1]
    prep = p['prep']
    sh1, sc1, gt1, sh2, sc2, gt2 = prep['mods']
    csh1, csc1, cgt1, csh2, csc2, cgt2 = prep['mods_ctx']

    w_main, w_gate = prep['w_in']
    main, gate = _project_latent(x, p) if projected is None else projected
    ctx_flat = ctx_s.reshape(1, b * sc, d)
    n_ctx, tn_ctx = (N_MAIN, PROJ_TN) if ctx_out else (N_KV_MAIN, N_KV_MAIN // 2)
    main_c, gate_c = _proj_in(ctx_flat, p['g_norm1'], csh1, csc1, w_main, w_gate, tm=min(PROJ_TM_CTX, b * sc),
                              tn=tn_ctx, n=n_ctx)
    main_c = main_c.reshape(b, sc, n_ctx)
    gate_c = gate_c.reshape(b, sc, LANE)

    o_na = _na_latent(main, main_c, prep['na_bias'])

    gate_ws = prep['gate_ws']
    s0 = jnp.zeros((b, GLA_HEADS, GLA_DK, GLA_DV), F32)
    o_cf, o_cb, st_f, st_b = _gla_scan(main_c, gate_c, gate_ws, (s0, s0), ctx_out)
    o_f, o_b, _, _ = _gla_scan(main, gate, gate_ws, (st_f, st_b), True)

    w_na, w_sc, w_gla, w_out, w_router = (prep[name] for name in ('w_na', 'w_sc', 'w_gla', 'w_out', 'w_router'))
    n_lat = b * s
    t = n_lat + (b * sc if ctx_out else 0)
    x, hp_all, lg_all = _merge(o_na, p['conv_w'], o_f, o_b, main, x, gt1, p['gla_norm_g'], w_na, w_sc, w_gla,
                               w_out, p['g_norm2'], sh2, sc2, w_router, tm=min(MERGE_TM, s), n_routed=t, tok_off=0)
    if ctx_out:
        o_na_c = _dense_attn(main_c)
        ctx_s, hp_all, lg_all = _merge(o_na_c, p['conv_w'], o_cf, o_cb, main_c, ctx_s, cgt1, p['gla_norm_g'],
                                       w_na, w_sc, w_gla, w_out, p['g_norm2'], csh2, csc2, w_router,
                                       tm=min(MERGE_TM_CTX, sc), n_routed=t, tok_off=n_lat, routed=(hp_all, lg_all))

    eidx, gw, rank, counts = _route(lg_all, p['b_router'])
    padded = (counts + MOE_BLOCK - 1) // MOE_BLOCK * MOE_BLOCK
    pad_end = jnp.cumsum(padded)
    pad_start = pad_end - padded
    onehot = eidx[:, :, None] == jnp.arange(N_EXPERTS, dtype=jnp.int32)
    dest = jnp.sum(jnp.where(onehot, pad_start, 0), axis=-1) + rank
    n_blocks = -(-(t * TOP_K + N_EXPERTS * (MOE_BLOCK - 1)) // MOE_BLOCK)
    slots = n_blocks * MOE_BLOCK
    blk_start = jnp.arange(n_blocks, dtype=jnp.int32) * MOE_BLOCK
    blk_e = jnp.minimum(jnp.sum(pad_end[None, :] <= blk_start[:, None], axis=1), N_EXPERTS - 1).astype(jnp.int32)
    used_end = (pad_start + counts)[blk_e]
    blk_valid = jnp.clip(used_end - blk_start, 0, MOE_BLOCK).astype(jnp.int32)
    n_used = pad_end[-1] // MOE_BLOCK
    blk_src = jnp.minimum(jnp.arange(n_blocks, dtype=jnp.int32), n_used - 1)
    blk_e = blk_e[blk_src]
    ids = jnp.arange(N_EXPERTS, dtype=jnp.int32)
    used = counts > 0
    later_used = jnp.where(used[None, :] & (ids[None, :] > ids[:, None]), ids[None, :], N_EXPERTS).min(axis=1)
    next_used = jnp.where(later_used == N_EXPERTS, -1, later_used).astype(jnp.int32)
    blk_next = next_used[blk_e]
    blk_slot = ((jnp.cumsum(used) - 1) % 2).astype(jnp.int32)[blk_e]

    xs = _dispatch_rows(hp_all, dest, slots)
    if p_next is not None:
        xs, p_next['prep'] = lax.optimization_barrier((xs, p_next['prep']))
    ys = _experts(xs, blk_e, blk_valid, blk_src, blk_next, blk_slot, p['layer'], p['w_exp_gate'], p['w_exp_up'],
                  p['w_exp_down'])
    gw_t = gw
    ws_gate, ws_up, ws_down = prep['ws_gate'], prep['ws_up'], prep['ws_down']

    def gathered(t0, n):
        return _gather_rows(ys, dest, t0, n)

    pieces = next(n for n in (4, 2, 1) if b % n == 0)
    nb = b // pieces
    for q in range(pieces):
        x = _combine(gathered(q * nb * s, nb * s), gw_t, 0, hp_all, x, gt2, q * nb, nb, ws_gate, ws_up, ws_down,
                     g_final, final, tm=min(COMBINE_TM, s))
        if after_piece is not None:
            after_piece(x, q * nb, nb)
    if ctx_out:
        ctx_s = _combine(gathered(n_lat, b * sc), gw_t, n_lat, hp_all, ctx_s, cgt2, 0, b, ws_gate, ws_up, ws_down,
                         g_final, False, tm=min(COMBINE_TM_CTX, sc))
    return x, ctx_s


def kernel(x, c, ctx, c_ctx, w_mod, b_mod, g_norm1, g_norm2, w_in, na_rpb, w_branch_na, conv_w, w_branch_sc,
           gla_gate_w, gla_gate_b, gla_norm_g, w_branch_gla, w_out, w_router, b_router, w_exp_gate, w_exp_up,
           w_exp_down, w_sh_gate, w_sh_up, w_sh_down, g_final):
    stacked = dict(g_norm1=g_norm1, g_norm2=g_norm2, na_rpb=na_rpb, w_branch_na=w_branch_na,
                   conv_w=conv_w, w_branch_sc=w_branch_sc, gla_gate_w=gla_gate_w, gla_gate_b=gla_gate_b,
                   gla_norm_g=gla_norm_g, w_branch_gla=w_branch_gla, w_out=w_out, w_router=w_router,
                   b_router=b_router,
                   w_sh_gate=w_sh_gate, w_sh_up=w_sh_up, w_sh_down=w_sh_down)
    depth = w_in.shape[0]
    rows = x.shape[1] // GRID_W
    layers = []
    for i in range(depth):
        p = {name: arr[i] for name, arr in stacked.items()}
        p.update(layer=i, w_exp_gate=w_exp_gate, w_exp_up=w_exp_up, w_exp_down=w_exp_down)
        mods, mods_ctx = _mod_vectors(c, c_ctx, w_mod, b_mod, i)
        p['prep'] = dict(
            w_in=_prep_w_in(w_in, i), mods=mods, mods_ctx=mods_ctx,
            na_bias=_na_bias_table(p['na_rpb'], rows, min(NA_WIN_R, rows)),
            gate_ws=_gla_gate_weights(p['gla_gate_w'], p['gla_gate_b']),
            w_na=p['w_branch_na'].astype(BF16), w_sc=p['w_branch_sc'].astype(BF16),
            w_gla=p['w_branch_gla'].astype(BF16), w_out=p['w_out'].astype(BF16),
            w_router=jnp.pad(p['w_router'], ((0, 0), (0, LANE - N_EXPERTS))).astype(BF16),
            ws_gate=p['w_sh_gate'].astype(BF16), ws_up=p['w_sh_up'].astype(BF16),
            ws_down=p['w_sh_down'].astype(BF16))
        layers.append(p)

    ctx_s = ctx
    projected = None
    for i, p in enumerate(layers):
        last = i == depth - 1
        p_next = None if last else layers[i + 1]
        after_piece = None
        next_projected = []
        if not last:
            def after_piece(xq, b0, nb, p_next=p_next, acc=next_projected):
                acc.append(_project_latent(xq, p_next, b0, nb, acc[-1] if acc else None))

        x, ctx_s = _layer(x, ctx_s, p, not last, last, g_final, projected, after_piece, p_next)
        projected = next_projected[-1] if next_projected else None
    return x
```

```python
import functools

import numpy as np
import jax
import jax.numpy as jnp
from jax import lax
from jax.experimental import pallas as pl
from jax.experimental.pallas import tpu as pltpu
from jax.experimental.pallas import tpu_sc as plsc

F32 = jnp.float32
BF16 = jnp.bfloat16
U32 = jnp.uint32

D_MODEL = 1024
N_MOD = 6
RMS_EPS = 1e-6
NEG_INF = -1e30
GRID_W = 64
NA_HEADS = 8
NA_HEAD_DIM = 64
NA_WIDTH = NA_HEADS * NA_HEAD_DIM
NA_WIN_R = 8
NA_WIN_C = 16
NA_GROUP = 4
SC_WIDTH = 512
GLA_HEADS = 4
GLA_KEY_WIDTH = 512
GLA_VAL_WIDTH = 1024
GLA_DK = GLA_KEY_WIDTH // GLA_HEADS
GLA_DV = GLA_VAL_WIDTH // GLA_HEADS
GLA_GATE_RANK = 16
GLA_GATE_TAU = 16.0
LOG2_E = 1.4426950408889634
N_EXPERTS = 64
N_EXPERT_GROUPS = 8
GROUP_SIZE = N_EXPERTS // N_EXPERT_GROUPS
TOPK_GROUPS = 4
TOP_K = 8
EXPERT_FF = 256
ROUTED_SCALE = 2.5
MOE_BLOCK = 1024

LANE = 128
GLA_C = 128
GLA_STEP_CHUNKS = 4
GLA_LEVELS = tuple(GLA_C >> (i + 1) for i in range(GLA_C.bit_length() - 1))
VMEM_LIMIT = 48 * 1024 * 1024
PROJ_TM, PROJ_TM_CTX, PROJ_TN = 2048, 1024, 1024
MERGE_TM, MERGE_TM_CTX = 512, 256
COMBINE_TM, COMBINE_TM_CTX = 512, 256
ROUTER_TMS = (1024, 512)
MOD_TN = 1536
SC_WINDOW = 128
SC_ROW = 256
SC_PARTS = D_MODEL // 2 // SC_ROW

OFF_V_GLA = 0
OFF_K_NA = 1024
OFF_V_NA = 1536
OFF_K_GLA = 2048
N_KV_MAIN = 2560
OFF_Q_NA = 2560
OFF_B_SC = 3072
OFF_C_SC = 3584
OFF_X_SC = 4096
OFF_Q_GLA = 4608
OFF_R_GLA = 5120
OFF_MERGE = 6144
N_MAIN = 9216


def _cparams(sem, vmem=VMEM_LIMIT):
    return pltpu.CompilerParams(dimension_semantics=sem, vmem_limit_bytes=vmem)


def _dot(a, b):
    return jnp.dot(a, b, preferred_element_type=F32)


def _dot_nt(a, b):
    return lax.dot_general(a, b, (((1,), (1,)), ((), ())), preferred_element_type=F32)


def _dot_tn(a, b):
    return lax.dot_general(a, b, (((0,), (0,)), ((), ())), preferred_element_type=F32)


def _sigmoid(x):
    return 0.5 * jnp.tanh(0.5 * x) + 0.5


def _pack_rows(x):
    n = x.shape[1] // 2
    r = x.astype(BF16).astype(F32)
    lo = pltpu.bitcast(r[:, :n], U32) >> 16
    hi = pltpu.bitcast(r[:, n:], U32)
    return hi | lo


def _store_parts(ref, words):
    for part in range(SC_PARTS):
        ref[part] = words[:, part * SC_ROW:(part + 1) * SC_ROW]


def _load_parts(ref, *lead):
    return jnp.concatenate([ref[(part,) + lead] for part in range(SC_PARTS)], axis=-1)


def _unpack_rows(w):
    lo = pltpu.bitcast(w << 16, F32)
    hi = pltpu.bitcast(w & jnp.uint32(0xFFFF0000), F32)
    return lo, hi


def _mod_kernel(a_ref, w_ref, b_ref, o_ref):
    a = a_ref[...]
    a = a * _sigmoid(a)
    o_ref[...] = _dot(a.astype(BF16), w_ref[0].astype(BF16)) + b_ref[0]


def _mod_vectors(c, c_ctx, w_mod, b_mod, layer):
    b = c.shape[0]
    rows = -(-(b + 1) // 8) * 8
    a = jnp.concatenate([c, c_ctx[None], jnp.zeros((rows - b - 1, D_MODEL), F32)], axis=0)
    n = N_MOD * D_MODEL
    tn = MOD_TN
    out = pl.pallas_call(
        _mod_kernel,
        grid=(n // tn,),
        in_specs=[pl.BlockSpec((rows, D_MODEL), lambda j: (0, 0)),
                  pl.BlockSpec((1, D_MODEL, tn), lambda j: (layer, 0, j)),
                  pl.BlockSpec((1, 1, tn), lambda j: (layer, 0, j))],
        out_specs=pl.BlockSpec((rows, tn), lambda j: (0, j)),
        out_shape=jax.ShapeDtypeStruct((rows, n), F32),
        compiler_params=_cparams(("parallel",)),
        name="mod_vectors",
    )(a, w_mod, b_mod[:, None])
    lat = out[:b].reshape(b, N_MOD, 1, D_MODEL)
    ctx = out[b].reshape(N_MOD, 1, 1, D_MODEL)
    return [lat[:, i] for i in range(N_MOD)], [ctx[i] for i in range(N_MOD)]


def _proj_kernel(x_ref, g_ref, sh_ref, sc_ref, w_ref, wg_ref, *rest):
    o_ref, og_ref, h_ref = rest[-3:]

    @pl.when(pl.program_id(2) == 0)
    def _():
        x = x_ref[0]
        ms = jnp.mean(x * x, axis=-1, keepdims=True)
        h = x * lax.rsqrt(ms + RMS_EPS) * g_ref[...] * (1.0 + sc_ref[0]) + sh_ref[0]
        hb = h.astype(BF16)
        h_ref[...] = hb
        og_ref[0] = _dot(hb, wg_ref[...])

    o_ref[0] = _dot(h_ref[...], w_ref[...]).astype(o_ref.dtype)


W_IN_TILE = 512
W_IN_GATE_SHIFT = 2 * GLA_GATE_RANK


def _prep_w_in_kernel(a_ref, b_ref, o_ref, g_ref):
    t = pl.program_id(0)
    first_lat = N_KV_MAIN // W_IN_TILE
    a = a_ref[0]

    @pl.when(t < first_lat)
    def _():
        o_ref[...] = a.T.astype(BF16)

    @pl.when(t >= first_lat)
    def _():
        moved = jnp.concatenate([a[W_IN_GATE_SHIFT:], b_ref[0]], axis=0)
        scale = jnp.where(t == first_lat, NA_HEAD_DIM ** -0.5, 1.0)
        o_ref[...] = (moved * scale).T.astype(BF16)

    @pl.when(t == first_lat)
    def _():
        head = a[:LANE]
        row = lax.broadcasted_iota(jnp.int32, head.shape, 0)
        g_ref[...] = jnp.where(row < W_IN_GATE_SHIFT, head, 0.0).T.astype(BF16)


def _prep_w_in(w_in, layer):
    d = w_in.shape[1]
    w_t = jnp.swapaxes(w_in, 1, 2)
    first_lat = N_KV_MAIN // W_IN_TILE
    kv_perm = OFF_K_NA // W_IN_TILE

    def a_map(t):
        return (layer, jnp.where(t < first_lat, (t + first_lat - kv_perm) % first_lat, t), 0)

    def b_map(t):
        return (layer, jnp.where(t < first_lat, 0, (t + 1) * (W_IN_TILE // W_IN_GATE_SHIFT)), 0)

    return pl.pallas_call(
        _prep_w_in_kernel,
        grid=(N_MAIN // W_IN_TILE,),
        in_specs=[pl.BlockSpec((1, W_IN_TILE, d), a_map),
                  pl.BlockSpec((1, W_IN_GATE_SHIFT, d), b_map)],
        out_specs=[pl.BlockSpec((d, W_IN_TILE), lambda t: (0, t)),
                   pl.BlockSpec((d, LANE), lambda t: (0, 0))],
        out_shape=[jax.ShapeDtypeStruct((d, N_MAIN), BF16), jax.ShapeDtypeStruct((d, LANE), BF16)],
        compiler_params=_cparams(("arbitrary",)),
        name="prep_w_in",
    )(w_t, w_t)


def _proj_in(x, g, shift, scale, w_main, w_gate, tm, tn, n=None, b0=0, nb=None, prev=None):
    b, s, d = x.shape
    n = w_main.shape[1] if n is None else n
    nb = b if nb is None else nb
    per_batch = shift.shape[0] == b
    mod_map = (lambda bi, i, j: (b0 + bi, 0, 0)) if per_batch else (lambda bi, i, j: (0, 0, 0))
    operands = (x, g[None], shift, scale, w_main, w_gate)
    extend = () if prev is None else tuple(prev)
    return pl.pallas_call(
        _proj_kernel,
        grid=(nb, s // tm, n // tn),
        in_specs=[pl.BlockSpec((1, tm, d), lambda bi, i, j: (b0 + bi, i, 0)),
                  pl.BlockSpec((1, d), lambda bi, i, j: (0, 0)),
                  pl.BlockSpec((1, 1, d), mod_map),
                  pl.BlockSpec((1, 1, d), mod_map),
                  pl.BlockSpec((d, tn), lambda bi, i, j: (0, j)),
                  pl.BlockSpec((d, LANE), lambda bi, i, j: (0, 0))]
                 + [pl.BlockSpec(memory_space=pl.ANY)] * len(extend),
        out_specs=[pl.BlockSpec((1, tm, tn), lambda bi, i, j: (b0 + bi, i, j)),
                   pl.BlockSpec((1, tm, LANE), lambda bi, i, j: (b0 + bi, i, 0))],
        out_shape=[jax.ShapeDtypeStruct((b, s, n), BF16),
                   jax.ShapeDtypeStruct((b, s, LANE), F32)],
        scratch_shapes=[pltpu.VMEM((tm, d), BF16)],
        input_output_aliases={len(operands): 0, len(operands) + 1: 1} if extend else {},
        compiler_params=_cparams(("parallel", "parallel", "arbitrary")),
        name="proj_in",
    )(*operands, *extend)


def _softmax_av(q, keys, vals, biases):
    scores = []
    for kk, bb in zip(keys, biases):
        s = _dot_nt(q, kk)
        scores.append(s if bb is None else s + bb)
    m = scores[0].max(axis=-1, keepdims=True)
    for s in scores[1:]:
        m = jnp.maximum(m, s.max(axis=-1, keepdims=True))
    num = None
    den = None
    for s, vv in zip(scores, vals):
        e = jnp.exp(s - m)
        dsum = e.sum(axis=-1, keepdims=True)
        o = _dot(e.astype(BF16), vv)
        num = o if num is None else num + o
        den = dsum if den is None else den + dsum
    return num / den


def _na_kernel(q_ref, k_ref, v_ref, kc_ref, vc_ref, *rest, rows, kr):
    *bias_refs, o_ref = rest
    kc = kc_ref[0]
    vc = vc_ref[0]
    for j, bias_ref in enumerate(bias_refs):
        r = pl.program_id(1) * len(bias_refs) + j
        row_start = jnp.clip(r - kr // 2, 0, rows - kr)
        start = pl.multiple_of(row_start * GRID_W, GRID_W)
        n_win = kr * GRID_W
        q = q_ref[0, j * GRID_W:(j + 1) * GRID_W, :]
        kw = k_ref[0, pl.ds(start, n_win), :]
        vw = v_ref[0, pl.ds(start, n_win), :]
        o_ref[0, j * GRID_W:(j + 1) * GRID_W, :] = _na_row(q, kw, vw, kc, vc, bias_ref).astype(o_ref.dtype)


def _na_row(q, kw, vw, kc, vc, bias_ref):
    gw = NA_GROUP * NA_HEAD_DIM
    stacked = (NA_GROUP * GRID_W, gw)
    on_head = (lax.broadcasted_iota(jnp.int32, stacked, 0) // GRID_W
               == lax.broadcasted_iota(jnp.int32, stacked, 1) // NA_HEAD_DIM)
    outs = []
    for g in range(NA_HEADS // NA_GROUP):
        sl = slice(g * gw, (g + 1) * gw)
        q_all = jnp.where(on_head, jnp.concatenate([q[:, sl]] * NA_GROUP, axis=0), jnp.zeros((), q.dtype))
        bias = bias_ref[0, g * NA_GROUP * GRID_W:(g + 1) * NA_GROUP * GRID_W, :]
        o_all = _softmax_av(q_all, [kw[:, sl], kc[:, sl]], [vw[:, sl], vc[:, sl]], [bias, None])
        o_all = jnp.where(on_head, o_all, 0.0).reshape(NA_GROUP, GRID_W, gw)
        outs.append(o_all.sum(axis=0))
    return jnp.concatenate(outs, axis=-1)


def _na_bias_table(rpb, rows, kr):
    col = np.arange(GRID_W)
    col_start = np.clip(col - NA_WIN_C // 2, 0, GRID_W - NA_WIN_C)
    col_ok = (col[None, :] >= col_start[:, None]) & (col[None, :] < col_start[:, None] + NA_WIN_C)
    d_col = np.clip(col[None, :] - col[:, None], -(NA_WIN_C - 1), NA_WIN_C - 1) + NA_WIN_C - 1
    n_dr, n_dc = rpb.shape[1], rpb.shape[2]
    onehot = jnp.asarray((d_col.reshape(-1)[None, :] == np.arange(n_dc)[:, None]).astype(np.float32))
    by_col = jnp.dot(rpb.astype(F32).reshape(NA_HEADS * n_dr, n_dc), onehot, precision=lax.Precision.HIGHEST)
    by_col = by_col.reshape(NA_HEADS, n_dr, GRID_W, GRID_W)
    by_col = jnp.where(col_ok[None, None], by_col, NEG_INF)
    tables = []
    for o in range(kr):
        lo = NA_WIN_R - 1 - o
        tables.append(by_col[:, lo:lo + kr].transpose(0, 2, 1, 3).reshape(NA_HEADS, GRID_W, kr * GRID_W))
    return jnp.stack(tables).reshape(kr, NA_HEADS * GRID_W, kr * GRID_W)


def _na_latent(main, main_ctx, bias):
    b, s, _ = main.shape
    sc = main_ctx.shape[1]
    rows = s // GRID_W
    kr = min(NA_WIN_R, rows)
    w = NA_WIDTH

    per_step = next(n for n in (8, 4, 2, 1) if rows % n == 0)

    def bias_spec(j):
        def bias_map(bi, i):
            r = i * per_step + j
            return (r - jnp.clip(r - kr // 2, 0, rows - kr), 0, 0)
        return pl.BlockSpec((1, NA_HEADS * GRID_W, kr * GRID_W), bias_map)

    return pl.pallas_call(
        functools.partial(_na_kernel, rows=rows, kr=kr),
        grid=(b, rows // per_step),
        in_specs=[pl.BlockSpec((1, per_step * GRID_W, w), lambda bi, i: (bi, i, OFF_Q_NA // w)),
                  pl.BlockSpec((1, s, w), lambda bi, i: (bi, 0, OFF_K_NA // w)),
                  pl.BlockSpec((1, s, w), lambda bi, i: (bi, 0, OFF_V_NA // w)),
                  pl.BlockSpec((1, sc, w), lambda bi, i: (bi, 0, OFF_K_NA // w)),
                  pl.BlockSpec((1, sc, w), lambda bi, i: (bi, 0, OFF_V_NA // w))]
                 + [bias_spec(j) for j in range(per_step)],
        out_specs=pl.BlockSpec((1, per_step * GRID_W, w), lambda bi, i: (bi, i, 0)),
        out_shape=jax.ShapeDtypeStruct((b, s, w), BF16),
        compiler_params=_cparams(("parallel", "arbitrary")),
        name="na_latent",
    )(main, main, main, main_ctx, main_ctx, *([bias] * per_step))


def _dense_attn_kernel(q_ref, k_ref, v_ref, o_ref):
    q = q_ref[0]
    k = k_ref[0]
    v = v_ref[0]
    outs = []
    for h in range(NA_HEADS):
        sl = slice(h * NA_HEAD_DIM, (h + 1) * NA_HEAD_DIM)
        outs.append(_softmax_av(q[:, sl], [k[:, sl]], [v[:, sl]], [None]))
    o_ref[0] = jnp.concatenate(outs, axis=-1).astype(o_ref.dtype)


def _dense_attn(main_ctx):
    b, sc, _ = main_ctx.shape
    w = NA_WIDTH
    return pl.pallas_call(
        _dense_attn_kernel,
        grid=(b,),
        in_specs=[pl.BlockSpec((1, sc, w), lambda bi: (bi, 0, OFF_Q_NA // w)),
                  pl.BlockSpec((1, sc, w), lambda bi: (bi, 0, OFF_K_NA // w)),
                  pl.BlockSpec((1, sc, w), lambda bi: (bi, 0, OFF_V_NA // w))],
        out_specs=pl.BlockSpec((1, sc, w), lambda bi: (bi, 0, 0)),
        out_shape=jax.ShapeDtypeStruct((b, sc, w), BF16),
        compiler_params=_cparams(("parallel",)),
        name="ctx_attn",
    )(main_ctx, main_ctx, main_ctx)


HALO = 16


def _gla_matrices(reverse):
    c = GLA_C
    t = np.arange(c)[:, None]
    m = np.arange(c)[None, :]
    blocks = [m <= t, m > t]
    for b in GLA_LEVELS:
        first = (t // (2 * b)) * (2 * b) + b
        is_q = (t & b) != 0
        blocks.append(np.where(is_q, (m > first) & (m <= t), (m > t) & (m <= first)))
    mats = np.stack(blocks).astype(np.float32)
    if reverse:
        mats = mats[:, ::-1, ::-1]
    mats = mats.reshape(-1, c)
    return jnp.asarray(np.concatenate([mats, mats], axis=1), dtype=BF16)


def _split_bf16(x):
    hi = x.astype(BF16)
    return hi, (x - hi.astype(F32)).astype(BF16)


def _pair_block_diag(x):
    lane = lax.broadcasted_iota(jnp.int32, x.shape, 1)
    zero = jnp.zeros((), x.dtype)
    return jnp.concatenate([jnp.where(lane < GLA_DK, x, zero), jnp.where(lane >= GLA_DK, x, zero)], axis=0)


def _gla_kernel(*refs, emit):
    n_in = 9 if emit else 8
    n_out = 2 if emit else 1
    ins, outs, scratch = refs[:2 * n_in], refs[2 * n_in:2 * (n_in + n_out)], refs[2 * (n_in + n_out):]
    ins = [ins[d * n_in:(d + 1) * n_in] for d in range(2)]
    outs = [outs[d * n_out:(d + 1) * n_out] for d in range(2)]
    step = pl.program_id(1)

    @pl.when(step == 0)
    def _():
        for d in range(2):
            scratch[d][...] = ins[d][-1][0]

    pending = [_gla_direction(ins[d][:-1], outs[d][:-1], scratch[d], reverse, emit)
               for d, reverse in enumerate((False, True))]
    while pending:
        pending = [stages for stages in pending if next(stages, "done") != "done"]

    @pl.when(step == pl.num_programs(1) - 1)
    def _():
        for d in range(2):
            outs[d][-1][0] = scratch[d][...]


def _gla_direction(ins, outs, st_ref, reverse, emit):
    gate_ref = ins[-5]
    order = range(gate_ref.shape[1] // GLA_C)
    for sub in (reversed(order) if reverse else order):
        yield from _gla_chunk(ins, outs, st_ref, reverse, emit, slice(sub * GLA_C, (sub + 1) * GLA_C))


def _gla_chunk(ins, outs, st_ref, reverse, emit, rows):
    if emit:
        q_ref, k_ref, v_ref, gt_ref, w2a_ref, w2b_ref, b2_ref, a_ref = ins
        (o_ref,) = outs
    else:
        k_ref, v_ref, gt_ref, w2a_ref, w2b_ref, b2_ref, a_ref = ins
    c = GLA_C
    pw = 2 * GLA_DK

    lr_hi, lr_lo = _split_bf16(gt_ref[0, rows])
    logit = (_dot(jnp.concatenate([lr_hi, lr_lo], axis=1), w2a_ref[...]) + _dot(lr_hi, w2b_ref[...])
             + b2_ref[...])
    g = (jnp.minimum(logit, 0.0) - jnp.log1p(jnp.exp(-jnp.abs(logit)))) * (LOG2_E / GLA_GATE_TAU)
    g_hi, g_lo = _split_bf16(g)
    args = _dot(a_ref[...], jnp.concatenate([g_hi, g_lo], axis=0))
    cum = args[0:c]
    rem = args[c:2 * c]
    last_row = cum[0:1] if reverse else cum[c - 1:c]
    yield

    k = k_ref[0, rows].astype(F32)
    v = v_ref[0, rows]
    atts = []
    if emit:
        q = q_ref[0, rows].astype(F32) * (GLA_DK ** -0.5)
        row_t = lax.broadcasted_iota(jnp.int32, (c, pw), 0)
        si = lax.broadcasted_iota(jnp.int32, (2 * c, c), 0) & (c - 1)
        ti = lax.broadcasted_iota(jnp.int32, (2 * c, c), 1)
        if reverse:
            row_t, ti, si = c - 1 - row_t, c - 1 - ti, c - 1 - si
        for hp in range(GLA_HEADS // 2):
            cs = slice(hp * pw, (hp + 1) * pw)
            qp, kp = q[:, cs], k[:, cs]
            att = jnp.where(ti == si, _dot_nt(_pair_block_diag(kp.astype(BF16)), qp.astype(BF16)), 0.0)
            for l, b in enumerate(GLA_LEVELS):
                x = (jnp.exp2(args[(2 + l) * c:(3 + l) * c, cs])
                     * jnp.where((row_t & b) != 0, qp, kp)).astype(BF16)
                pair = (((ti ^ si) >> (b.bit_length() - 1)) == 1) & ((ti & b) != 0)
                att = jnp.where(pair, _dot_nt(_pair_block_diag(x), x), att)
                yield
            atts.append(att.astype(BF16))

    outs = []
    for h in range(GLA_HEADS):
        sl = slice(h * GLA_DK, (h + 1) * GLA_DK)
        kh = k[:, sl]
        vh = v[:, h * GLA_DV:(h + 1) * GLA_DV]
        state = st_ref[h]
        kd = (kh * jnp.exp2(rem[:, sl])).astype(BF16)
        decay = jnp.exp2(jnp.broadcast_to(last_row[:, sl], (GLA_DK, GLA_DK))).T
        decay = jnp.concatenate([decay] * (GLA_DV // GLA_DK), axis=1)
        if emit:
            qd = (q[:, sl] * jnp.exp2(cum[:, sl])).astype(BF16)
            att_t = atts[h // 2][(h % 2) * c:(h % 2 + 1) * c]
            both = _dot_tn(jnp.concatenate([att_t, kd], axis=1), vh)
            outs.append(_dot(qd, state.astype(BF16)) + both[:c])
            st_ref[h] = decay * state + both[c:]
        else:
            st_ref[h] = decay * state + _dot_tn(kd, vh)
        yield

    if emit:
        o_ref[0, rows] = jnp.concatenate(outs, axis=-1).astype(o_ref.dtype)


def _gla_scan(main, gate, gate_ws, s0s, emit):
    b, l, _ = main.shape
    step_rows = GLA_C * min(GLA_STEP_CHUNKS, l // GLA_C)
    n = l // step_rows
    const = lambda arr: pl.BlockSpec(arr.shape, lambda bi, s: (0,) * arr.ndim)
    state_spec = pl.BlockSpec((1, GLA_HEADS, GLA_DK, GLA_DV), lambda bi, s: (bi, 0, 0, 0))
    state_shape = jax.ShapeDtypeStruct((b, GLA_HEADS, GLA_DK, GLA_DV), F32)
    in_specs, args, out_specs, out_shape = [], [], [], []
    for reverse in (False, True):
        amat = _gla_matrices(reverse)
        w2a, w2b, b2 = gate_ws[reverse]

        def col(block, reverse=reverse):
            return lambda bi, s: (bi, n - 1 - s if reverse else s, block)

        if emit:
            in_specs.append(pl.BlockSpec((1, step_rows, GLA_KEY_WIDTH), col(OFF_Q_GLA // GLA_KEY_WIDTH)))
            args.append(main)
            out_specs.append(pl.BlockSpec((1, step_rows, GLA_VAL_WIDTH), col(0)))
            out_shape.append(jax.ShapeDtypeStruct((b, l, GLA_VAL_WIDTH), BF16))
        in_specs += [pl.BlockSpec((1, step_rows, GLA_KEY_WIDTH), col(OFF_K_GLA // GLA_KEY_WIDTH)),
                     pl.BlockSpec((1, step_rows, GLA_VAL_WIDTH), col(OFF_V_GLA // GLA_VAL_WIDTH)),
                     pl.BlockSpec((1, step_rows, LANE), col(0)),
                     const(w2a), const(w2b), const(b2), const(amat), state_spec]
        args += [main, main, gate, w2a, w2b, b2, amat, s0s[reverse]]
        out_specs.append(state_spec)
        out_shape.append(state_shape)
    res = pl.pallas_call(
        functools.partial(_gla_kernel, emit=emit),
        grid=(b, n),
        in_specs=in_specs,
        out_specs=out_specs,
        out_shape=out_shape,
        scratch_shapes=[pltpu.VMEM((GLA_HEADS, GLA_DK, GLA_DV), F32)] * 2,
        compiler_params=_cparams(("parallel", "arbitrary")),
        name="gla_scan",
    )(*args)
    return (res[0], res[2], res[1], res[3]) if emit else (None, None, res[0], res[1])


def _gla_gate_weights(gate_w, gate_b):
    out = []
    for dr in range(2):
        w = jnp.zeros((LANE, GLA_KEY_WIDTH), F32)
        w = w.at[dr * GLA_GATE_RANK:(dr + 1) * GLA_GATE_RANK].set(gate_w[dr])
        w_hi = w.astype(BF16)
        w_lo = (w - w_hi.astype(F32)).astype(BF16)
        out.append((jnp.concatenate([w_hi, w_hi], axis=0), w_lo, gate_b[dr][None]))
    return out


def _conv_tile(b_ref, c_ref, x_ref, cp_ref, xp_ref, cn_ref, xn_ref, w_ref):
    i, n = pl.program_id(1), pl.num_programs(1)
    u = c_ref[0].astype(F32) * x_ref[0].astype(F32)
    tm = u.shape[0]
    before = jnp.where(i > 0, 1.0, 0.0) * (cp_ref[0, HALO - 1:HALO].astype(F32) * xp_ref[0, HALO - 1:HALO].astype(F32))
    after = jnp.where(i < n - 1, 1.0, 0.0) * (cn_ref[0, 0:1].astype(F32) * xn_ref[0, 0:1].astype(F32))
    t = lax.broadcasted_iota(jnp.int32, u.shape, 0)
    prev = jnp.where(t == 0, before, pltpu.roll(u, 1, axis=0))
    nxt = jnp.where(t == tm - 1, after, pltpu.roll(u, tm - 1, axis=0))
    w = w_ref[...]
    return b_ref[0].astype(F32) * (prev * w[0:1] + u * w[1:2] + nxt * w[2:3])


def _merge_kernel(ona_ref, bsc_ref, csc_ref, xsc_ref, cp_ref, xp_ref, cn_ref, xn_ref, cw_ref, of_ref, ob_ref,
                  r_ref, gna_ref, gsc_ref, ggl_ref, x_ref, gt_ref,
                  gn_ref, wna_ref, wsc_ref, wgl_ref, wo_ref, g2_ref, sh2_ref, sc2_ref, wr_ref, *rest):
    xo_ref, h2_ref, hp_ref, lg_ref = rest[-4:]
    o_sc = _conv_tile(bsc_ref, csc_ref, xsc_ref, cp_ref, xp_ref, cn_ref, xn_ref, cw_ref).astype(BF16)
    o = of_ref[0].astype(F32) + ob_ref[0].astype(F32)
    normed = []
    for h in range(GLA_HEADS):
        oh = o[:, h * GLA_DV:(h + 1) * GLA_DV]
        ms = jnp.mean(oh * oh, axis=-1, keepdims=True)
        normed.append(oh * lax.rsqrt(ms + RMS_EPS))
    r = r_ref[0].astype(F32)
    y_gla = jnp.concatenate(normed, axis=-1) * gn_ref[...] * (r * _sigmoid(r))
    y = (_sigmoid(gna_ref[0].astype(F32)) * _dot(ona_ref[0], wna_ref[...])
         + _sigmoid(gsc_ref[0].astype(F32)) * _dot(o_sc, wsc_ref[...])
         + _sigmoid(ggl_ref[0].astype(F32)) * _dot(y_gla.astype(BF16), wgl_ref[...]))
    xn = x_ref[0] + gt_ref[0] * _dot(y.astype(BF16), wo_ref[...])
    xo_ref[0] = xn
    ms = jnp.mean(xn * xn, axis=-1, keepdims=True)
    h2 = xn * lax.rsqrt(ms + RMS_EPS) * g2_ref[...] * (1.0 + sc2_ref[0]) + sh2_ref[0]
    h2b = h2.astype(BF16)
    h2_ref[0] = h2b
    _store_parts(hp_ref, _pack_rows(h2))
    lg_ref[...] = _dot(h2b, wr_ref[...])


def _merge(o_na, conv_w, o_f, o_b, main, x, gt1, gn, w_na, w_sc, w_gla, w_out, g2, sh2, sc2, w_router, tm,
           n_routed, tok_off, routed=None):
    b, s, d = x.shape
    per_batch = gt1.shape[0] == b
    mod_map = (lambda bi, i: (bi, 0, 0)) if per_batch else (lambda bi, i: (0, 0, 0))
    tok = lambda width, blk: pl.BlockSpec((1, tm, width), lambda bi, i: (bi, i, blk))
    full = lambda arr: pl.BlockSpec(arr.shape, lambda bi, i: (0,) * arr.ndim)
    mod = pl.BlockSpec((1, 1, d), mod_map)
    gn_t = jnp.tile(gn, GLA_HEADS)[None]
    g2_t = g2[None]
    per_tile = tm // HALO
    last_halo = s // HALO - 1
    halo_prev = lambda blk: pl.BlockSpec(
        (1, HALO, SC_WIDTH), lambda bi, i: (bi, jnp.maximum(i * per_tile - 1, 0), blk))
    halo_next = lambda blk: pl.BlockSpec(
        (1, HALO, SC_WIDTH), lambda bi, i: (bi, jnp.minimum((i + 1) * per_tile, last_halo), blk))
    col_b, col_c, col_x = OFF_B_SC // SC_WIDTH, OFF_C_SC // SC_WIDTH, OFF_X_SC // SC_WIDTH
    extend = () if routed is None else tuple(routed)
    operands = (o_na, main, main, main, main, main, main, main, conv_w, o_f, o_b, main, main, main, main, x, gt1,
                gn_t, w_na, w_sc, w_gla, w_out, g2_t, sh2, sc2, w_router)
    n_in = len(operands)
    nblk = s // tm
    blk0 = tok_off // tm
    return pl.pallas_call(
        _merge_kernel,
        grid=(b, s // tm),
        in_specs=[tok(NA_WIDTH, 0), tok(SC_WIDTH, col_b), tok(SC_WIDTH, col_c), tok(SC_WIDTH, col_x),
                  halo_prev(col_c), halo_prev(col_x), halo_next(col_c), halo_next(col_x), full(conv_w),
                  tok(GLA_VAL_WIDTH, 0), tok(GLA_VAL_WIDTH, 0),
                  tok(d, OFF_R_GLA // d), tok(d, OFF_MERGE // d), tok(d, OFF_MERGE // d + 1),
                  tok(d, OFF_MERGE // d + 2), tok(d, 0), mod,
                  full(gn_t), full(w_na), full(w_sc), full(w_gla), full(w_out), full(g2_t), mod, mod,
                  full(w_router)] + [pl.BlockSpec(memory_space=pl.ANY)] * len(extend),
        out_specs=[tok(d, 0), tok(d, 0),
                   pl.BlockSpec((SC_PARTS, tm, SC_ROW), lambda bi, i: (0, blk0 + bi * nblk + i, 0)),
                   pl.BlockSpec((tm, LANE), lambda bi, i: (blk0 + bi * nblk + i, 0))],
        out_shape=[jax.ShapeDtypeStruct((b, s, d), F32),
                   jax.ShapeDtypeStruct((b, s, d), BF16),
                   jax.ShapeDtypeStruct((SC_PARTS, n_routed, SC_ROW), U32),
                   jax.ShapeDtypeStruct((n_routed, LANE), F32)],
        input_output_aliases={n_in: 2, n_in + 1: 3} if extend else {},
        compiler_params=_cparams(("parallel", "parallel")),
        name="merge",
    )(*operands, *extend)


def _router_kernel(lg_ref, br_ref, tri_ref, eidx_ref, gw_ref, rank_ref, cnt_ref, carry_ref):
    tm = lg_ref.shape[0]

    @pl.when(pl.program_id(0) == 0)
    def _():
        carry_ref[...] = jnp.zeros_like(carry_ref)

    scores = _sigmoid(lg_ref[...].T[:N_EXPERTS])
    sel = scores + br_ref[...]
    neg = -jnp.inf

    sel3 = sel.reshape(N_EXPERT_GROUPS, GROUP_SIZE, tm)
    i3 = lax.broadcasted_iota(jnp.int32, sel3.shape, 1)
    m1 = sel3.max(axis=1, keepdims=True)
    first = jnp.where(sel3 == m1, i3, GROUP_SIZE).min(axis=1, keepdims=True)
    m2 = jnp.where(i3 == first, neg, sel3).max(axis=1, keepdims=True)
    gscore = (m1 + m2)[:, 0, :]

    gi = lax.broadcasted_iota(jnp.int32, gscore.shape, 0)
    gmask = jnp.zeros(gscore.shape, jnp.bool_)
    for _ in range(TOPK_GROUPS):
        m = gscore.max(axis=0, keepdims=True)
        pick = gi == jnp.where(gscore == m, gi, N_EXPERT_GROUPS).min(axis=0, keepdims=True)
        gmask = gmask | pick
        gscore = jnp.where(pick, neg, gscore)
    emask = jnp.broadcast_to(gmask[:, None, :], sel3.shape).reshape(N_EXPERTS, tm)
    sel = jnp.where(emask, sel, neg)

    ei = lax.broadcasted_iota(jnp.int32, sel.shape, 0)
    picks, idxs, ws = [], [], []
    for _ in range(TOP_K):
        m = sel.max(axis=0, keepdims=True)
        idx = jnp.where(sel == m, ei, N_EXPERTS).min(axis=0, keepdims=True)
        pick = ei == idx
        picks.append(pick)
        idxs.append(idx)
        ws.append(jnp.where(pick, scores, 0.0).sum(axis=0, keepdims=True))
        sel = jnp.where(pick, neg, sel)
    w = jnp.concatenate(ws, axis=0)
    gw_ref[...] = (w / w.sum(axis=0, keepdims=True) * ROUTED_SCALE).T
    eidx_ref[...] = jnp.concatenate(idxs, axis=0)

    onehot = picks[0]
    for p in picks[1:]:
        onehot = onehot | p
    onehot = jnp.where(onehot, 1.0, 0.0).astype(BF16)
    before = _dot(onehot, tri_ref[...]) + jnp.tile(carry_ref[...], (1, tm // LANE))
    rank_ref[...] = jnp.concatenate(
        [jnp.where(p, before, 0.0).sum(axis=0, keepdims=True) for p in picks], axis=0).astype(jnp.int32)
    carry_ref[...] += _dot(onehot, jnp.ones((tm, LANE), BF16))
    cnt_ref[...] = carry_ref[...]


def _route(logits, b_router):
    t = logits.shape[0]
    tm = next(n for n in ROUTER_TMS if t % n == 0)
    br = jnp.broadcast_to(b_router.astype(F32)[:, None], (N_EXPERTS, tm))
    tri = jnp.asarray(np.triu(np.ones((tm, tm), np.float32), 1), dtype=BF16)
    kt = lambda dt: jax.ShapeDtypeStruct((TOP_K, t), dt)
    eidx, gw, rank, cnt = pl.pallas_call(
        _router_kernel,
        grid=(t // tm,),
        in_specs=[pl.BlockSpec((tm, LANE), lambda i: (i, 0)),
                  pl.BlockSpec((N_EXPERTS, tm), lambda i: (0, 0)),
                  pl.BlockSpec((tm, tm), lambda i: (0, 0))],
        out_specs=[pl.BlockSpec((TOP_K, tm), lambda i: (0, i)),
                   pl.BlockSpec((tm, TOP_K), lambda i: (i, 0)),
                   pl.BlockSpec((TOP_K, tm), lambda i: (0, i)),
                   pl.BlockSpec((N_EXPERTS, LANE), lambda i: (0, 0))],
        out_shape=[kt(jnp.int32), jax.ShapeDtypeStruct((t, TOP_K), F32), kt(jnp.int32),
                   jax.ShapeDtypeStruct((N_EXPERTS, LANE), F32)],
        scratch_shapes=[pltpu.VMEM((N_EXPERTS, LANE), F32)],
        compiler_params=_cparams(("arbitrary",)),
        name="router",
    )(logits, br, tri)
    return eidx, gw, rank, cnt[:, 0].astype(jnp.int32)


def _sc_mesh():
    return plsc.VectorSubcoreMesh(core_axis_name="core", subcore_axis_name="subcore")


def _dispatch_rows(xp, dest, slots):
    parts, t, _ = xp.shape

    @pl.kernel(out_type=jax.ShapeDtypeStruct((parts, slots, SC_ROW), xp.dtype), mesh=_sc_mesh(),
               scratch_types=[], name="moe_dispatch")
    def run(x_hbm, d_hbm, o_hbm):
        for part in range(parts):
            out_part = o_hbm.at[part]

            def body(x_vmem, d_vmem, out_part=out_part):
                for k in range(TOP_K):
                    pltpu.sync_copy(x_vmem, out_part.at[d_vmem.at[k]])

            pltpu.emit_pipeline(
                body,
                grid=(t // SC_WINDOW,),
                in_specs=[pl.BlockSpec((SC_WINDOW, SC_ROW), lambda i: (i, 0)),
                          pl.BlockSpec((TOP_K, SC_WINDOW), lambda i: (0, i))],
                out_specs=[],
                core_axis_name=("core", "subcore"),
                dimension_semantics=(pltpu.PARALLEL,),
            )(x_hbm.at[part], d_hbm)

    return run(xp, dest)


def _gather_rows(yp, dest, t0, n):
    parts = yp.shape[0]
    nwin = n // SC_WINDOW
    win0 = t0 // SC_WINDOW

    @pl.kernel(out_type=jax.ShapeDtypeStruct((parts, TOP_K * n, SC_ROW), yp.dtype), mesh=_sc_mesh(),
               scratch_types=[], name="moe_gather")
    def run(y_hbm, d_hbm, o_hbm):
        for part in range(parts):
            table = y_hbm.at[part]

            def body(d_vmem, o_vmem, table=table):
                pltpu.sync_copy(table.at[d_vmem.at[0]], o_vmem)

            pltpu.emit_pipeline(
                body,
                grid=(TOP_K, nwin),
                in_specs=[pl.BlockSpec((1, SC_WINDOW), lambda k, j: (k, win0 + j))],
                out_specs=[pl.BlockSpec((SC_WINDOW, SC_ROW), lambda k, j: (k * nwin + j, 0))],
                core_axis_name=("core", "subcore"),
                dimension_semantics=(pltpu.PARALLEL, pltpu.PARALLEL),
            )(d_hbm, o_hbm.at[part])

    return run(yp, dest).reshape(parts, TOP_K, n, SC_ROW)


def _expert_kernel(be_ref, bv_ref, bs_ref, nx_ref, sl_ref, x_ref, wg_hbm, wu_hbm, wd_hbm, o_ref,
                   wg_f, wu_f, wd_f, wg_s, wu_s, wd_s, sems, *, layer):
    i = pl.program_id(0)
    valid = bv_ref[i]
    expert = be_ref[i]
    new_expert = (i == 0) | (expert != be_ref[jnp.maximum(i - 1, 0)])
    slot = sl_ref[i]

    def fetch(which, into):
        return [pltpu.make_async_copy(src.at[layer, which], dst.at[into], sems.at[into, j])
                for j, (src, dst) in enumerate(((wg_hbm, wg_f), (wu_hbm, wu_f), (wd_hbm, wd_f)))]

    @pl.when(i == 0)
    def _():
        for cp in fetch(expert, slot):
            cp.start()

    @pl.when(new_expert)
    def _():
        for cp in fetch(expert, slot):
            cp.wait()
        upcoming = nx_ref[i]

        @pl.when(upcoming >= 0)
        def _():
            for cp in fetch(upcoming, 1 - slot):
                cp.start()

        wg_s[...] = wg_f[slot].astype(BF16)
        wu_s[...] = wu_f[slot].astype(BF16)
        wd_s[...] = wd_f[slot].astype(BF16)

    @pl.when(valid > 0)
    def _():
        w = _load_parts(x_ref)
        row = lax.broadcasted_iota(jnp.int32, w.shape, 0)
        w = jnp.where(row < valid, w, jnp.uint32(0))
        lo, hi = _unpack_rows(w)
        x = jnp.concatenate([lo, hi], axis=1).astype(BF16)
        a = _dot(x, wg_s[...])
        hid = a * _sigmoid(a) * _dot(x, wu_s[...])
        _store_parts(o_ref, _pack_rows(_dot(hid.astype(BF16), wd_s[...])))


def _experts(xs, blk_e, blk_valid, blk_src, blk_next, blk_slot, layer, w_gate, w_up, w_down):
    parts, slots, _ = xs.shape
    d = D_MODEL
    nb = slots // MOE_BLOCK
    data = pl.BlockSpec((parts, MOE_BLOCK, SC_ROW), lambda i, be, bv, bs, nx, sl: (0, bs[i], 0))
    stage = lambda shape: pltpu.VMEM((2,) + shape, F32)
    return pl.pallas_call(
        functools.partial(_expert_kernel, layer=layer),
        grid_spec=pltpu.PrefetchScalarGridSpec(
            num_scalar_prefetch=5,
            grid=(nb,),
            in_specs=[data] + [pl.BlockSpec(memory_space=pl.ANY)] * 3,
            out_specs=data,
            scratch_shapes=[stage((d, EXPERT_FF)), stage((d, EXPERT_FF)), stage((EXPERT_FF, d)),
                            pltpu.VMEM((d, EXPERT_FF), BF16), pltpu.VMEM((d, EXPERT_FF), BF16),
                            pltpu.VMEM((EXPERT_FF, d), BF16), pltpu.SemaphoreType.DMA((2, 3))]),
        out_shape=jax.ShapeDtypeStruct((parts, slots, SC_ROW), U32),
        compiler_params=_cparams(("arbitrary",)),
        name="experts",
    )(blk_e, blk_valid, blk_src, blk_next, blk_slot, xs, w_gate, w_up, w_down)


def _combine_kernel(yg_ref, gw_ref, h_ref, x_ref, gt_ref, wsg_ref, wsu_ref, wsd_ref, gf_ref, o_ref, *, final):
    h = h_ref[0]
    a = _dot(h, wsg_ref[...])
    hid = a * _sigmoid(a) * _dot(h, wsu_ref[...])
    y = _dot(hid.astype(BF16), wsd_ref[...])
    gw = gw_ref[...]
    y_lo = y[:, :D_MODEL // 2]
    y_hi = y[:, D_MODEL // 2:]
    for k in range(TOP_K):
        lo, hi = _unpack_rows(_load_parts(yg_ref, k))
        y_lo = y_lo + gw[:, k:k + 1] * lo
        y_hi = y_hi + gw[:, k:k + 1] * hi
    y = jnp.concatenate([y_lo, y_hi], axis=1)
    xn = x_ref[0] + gt_ref[0] * y
    if final:
        ms = jnp.mean(xn * xn, axis=-1, keepdims=True)
        xn = xn * lax.rsqrt(ms + RMS_EPS) * gf_ref[...]
    o_ref[0] = xn


def _combine(yg, gw, tok_off, h2, x, gt2, b0, nb, ws_gate, ws_up, ws_down, g_final, final, tm):
    b, s, d = x.shape
    per_batch = gt2.shape[0] == b
    mod_map = (lambda bi, i: (b0 + bi, 0, 0)) if per_batch else (lambda bi, i: (0, 0, 0))
    full = lambda arr: pl.BlockSpec(arr.shape, lambda bi, i: (0,) * arr.ndim)
    tok = lambda width: pl.BlockSpec((1, tm, width), lambda bi, i: (b0 + bi, i, 0))
    gf = g_final[None]
    nblk = s // tm
    blk0 = (tok_off + b0 * s) // tm
    return pl.pallas_call(
        functools.partial(_combine_kernel, final=final),
        grid=(nb, nblk),
        in_specs=[pl.BlockSpec((SC_PARTS, TOP_K, tm, SC_ROW), lambda bi, i: (0, 0, bi * nblk + i, 0)),
                  pl.BlockSpec((tm, TOP_K), lambda bi, i: (blk0 + bi * nblk + i, 0)),
                  tok(d), tok(d),
                  pl.BlockSpec((1, 1, d), mod_map),
                  full(ws_gate), full(ws_up), full(ws_down), full(gf)],
        out_specs=tok(d),
        out_shape=jax.ShapeDtypeStruct((b, s, d), F32),
        input_output_aliases={3: 0},
        compiler_params=_cparams(("parallel", "parallel")),
        name="combine",
    )(yg, gw, h2, x, gt2, ws_gate, ws_up, ws_down, gf)


def _project_latent(x, p, b0=0, nb=None, prev=None):
    w_main, w_gate = p['prep']['w_in']
    mods = p['prep']['mods']
    return _proj_in(x, p['g_norm1'], mods[0], mods[1], w_main, w_gate, tm=min(PROJ_TM, x.shape[1]), tn=PROJ_TN,
                    b0=b0, nb=nb, prev=prev)


def _layer(x, ctx_s, p, ctx_out, final, g_final, projected=None, after_piece=None, p_next=None):
    b, s, d = x.shape
    sc = ctx_s.shape[1]
    prep = p['prep']
    sh1, sc1, gt1, sh2, sc2, gt2 = prep['mods']
    csh1, csc1, cgt1, csh2, csc2, cgt2 = prep['mods_ctx']

    w_main, w_gate = prep['w_in']
    main, gate = _project_latent(x, p) if projected is None else projected
    ctx_flat = ctx_s.reshape(1, b * sc, d)
    n_ctx, tn_ctx = (N_MAIN, PROJ_TN) if ctx_out else (N_KV_MAIN, N_KV_MAIN // 2)
    main_c, gate_c = _proj_in(ctx_flat, p['g_norm1'], csh1, csc1, w_main, w_gate, tm=min(PROJ_TM_CTX, b * sc),
                              tn=tn_ctx, n=n_ctx)
    main_c = main_c.reshape(b, sc, n_ctx)
    gate_c = gate_c.reshape(b, sc, LANE)

    o_na = _na_latent(main, main_c, prep['na_bias'])

    gate_ws = prep['gate_ws']
    s0 = jnp.zeros((b, GLA_HEADS, GLA_DK, GLA_DV), F32)
    o_cf, o_cb, st_f, st_b = _gla_scan(main_c, gate_c, gate_ws, (s0, s0), ctx_out)
    o_f, o_b, _, _ = _gla_scan(main, gate, gate_ws, (st_f, st_b), True)

    w_na, w_sc, w_gla, w_out, w_router = (prep[name] for name in ('w_na', 'w_sc', 'w_gla', 'w_out', 'w_router'))
    n_lat = b * s
    t = n_lat + (b * sc if ctx_out else 0)
    x, h2, hp_all, lg_all = _merge(o_na, p['conv_w'], o_f, o_b, main, x, gt1, p['gla_norm_g'], w_na, w_sc, w_gla,
                               w_out, p['g_norm2'], sh2, sc2, w_router, tm=min(MERGE_TM, s), n_routed=t, tok_off=0)
    if ctx_out:
        o_na_c = _dense_attn(main_c)
        ctx_s, h2_c, hp_all, lg_all = _merge(o_na_c, p['conv_w'], o_cf, o_cb, main_c, ctx_s, cgt1, p['gla_norm_g'],
                                       w_na, w_sc, w_gla, w_out, p['g_norm2'], csh2, csc2, w_router,
                                       tm=min(MERGE_TM_CTX, sc), n_routed=t, tok_off=n_lat, routed=(hp_all, lg_all))

    eidx, gw, rank, counts = _route(lg_all, p['b_router'])
    padded = (counts + MOE_BLOCK - 1) // MOE_BLOCK * MOE_BLOCK
    pad_end = jnp.cumsum(padded)
    pad_start = pad_end - padded
    onehot = eidx[:, :, None] == jnp.arange(N_EXPERTS, dtype=jnp.int32)
    dest = jnp.sum(jnp.where(onehot, pad_start, 0), axis=-1) + rank
    n_blocks = -(-(t * TOP_K + N_EXPERTS * (MOE_BLOCK - 1)) // MOE_BLOCK)
    slots = n_blocks * MOE_BLOCK
    blk_start = jnp.arange(n_blocks, dtype=jnp.int32) * MOE_BLOCK
    blk_e = jnp.minimum(jnp.sum(pad_end[None, :] <= blk_start[:, None], axis=1), N_EXPERTS - 1).astype(jnp.int32)
    used_end = (pad_start + counts)[blk_e]
    blk_valid = jnp.clip(used_end - blk_start, 0, MOE_BLOCK).astype(jnp.int32)
    n_used = pad_end[-1] // MOE_BLOCK
    blk_src = jnp.minimum(jnp.arange(n_blocks, dtype=jnp.int32), n_used - 1)
    blk_e = blk_e[blk_src]
    ids = jnp.arange(N_EXPERTS, dtype=jnp.int32)
    used = counts > 0
    later_used = jnp.where(used[None, :] & (ids[None, :] > ids[:, None]), ids[None, :], N_EXPERTS).min(axis=1)
    next_used = jnp.where(later_used == N_EXPERTS, -1, later_used).astype(jnp.int32)
    blk_next = next_used[blk_e]
    blk_slot = ((jnp.cumsum(used) - 1) % 2).astype(jnp.int32)[blk_e]

    xs = _dispatch_rows(hp_all, dest, slots)
    if p_next is not None:
        xs, p_next['prep'] = lax.optimization_barrier((xs, p_next['prep']))
    ys = _experts(xs, blk_e, blk_valid, blk_src, blk_next, blk_slot, p['layer'], p['w_exp_gate'], p['w_exp_up'],
                  p['w_exp_down'])
    gw_t = gw
    ws_gate, ws_up, ws_down = prep['ws_gate'], prep['ws_up'], prep['ws_down']

    def gathered(t0, n):
        return _gather_rows(ys, dest, t0, n)

    pieces = next(n for n in (4, 2, 1) if b % n == 0)
    nb = b // pieces
    for q in range(pieces):
        x = _combine(gathered(q * nb * s, nb * s), gw_t, 0, h2, x, gt2, q * nb, nb, ws_gate, ws_up, ws_down,
                     g_final, final, tm=min(COMBINE_TM, s))
        if after_piece is not None:
            after_piece(x, q * nb, nb)
    if ctx_out:
        ctx_s = _combine(gathered(n_lat, b * sc), gw_t, n_lat, h2_c, ctx_s, cgt2, 0, b, ws_gate, ws_up, ws_down,
                         g_final, False, tm=min(COMBINE_TM_CTX, sc))
    return x, ctx_s


def kernel(x, c, ctx, c_ctx, w_mod, b_mod, g_norm1, g_norm2, w_in, na_rpb, w_branch_na, conv_w, w_branch_sc,
           gla_gate_w, gla_gate_b, gla_norm_g, w_branch_gla, w_out, w_router, b_router, w_exp_gate, w_exp_up,
           w_exp_down, w_sh_gate, w_sh_up, w_sh_down, g_final):
    stacked = dict(g_norm1=g_norm1, g_norm2=g_norm2, na_rpb=na_rpb, w_branch_na=w_branch_na,
                   conv_w=conv_w, w_branch_sc=w_branch_sc, gla_gate_w=gla_gate_w, gla_gate_b=gla_gate_b,
                   gla_norm_g=gla_norm_g, w_branch_gla=w_branch_gla, w_out=w_out, w_router=w_router,
                   b_router=b_router,
                   w_sh_gate=w_sh_gate, w_sh_up=w_sh_up, w_sh_down=w_sh_down)
    depth = w_in.shape[0]
    rows = x.shape[1] // GRID_W
    layers = []
    for i in range(depth):
        p = {name: arr[i] for name, arr in stacked.items()}
        p.update(layer=i, w_exp_gate=w_exp_gate, w_exp_up=w_exp_up, w_exp_down=w_exp_down)
        mods, mods_ctx = _mod_vectors(c, c_ctx, w_mod, b_mod, i)
        p['prep'] = dict(
            w_in=_prep_w_in(w_in, i), mods=mods, mods_ctx=mods_ctx,
            na_bias=_na_bias_table(p['na_rpb'], rows, min(NA_WIN_R, rows)),
            gate_ws=_gla_gate_weights(p['gla_gate_w'], p['gla_gate_b']),
            w_na=p['w_branch_na'].astype(BF16), w_sc=p['w_branch_sc'].astype(BF16),
            w_gla=p['w_branch_gla'].astype(BF16), w_out=p['w_out'].astype(BF16),
            w_router=jnp.pad(p['w_router'], ((0, 0), (0, LANE - N_EXPERTS))).astype(BF16),
            ws_gate=p['w_sh_gate'].astype(BF16), ws_up=p['w_sh_up'].astype(BF16),
            ws_down=p['w_sh_down'].astype(BF16))
        layers.append(p)

    ctx_s = ctx
    projected = None
    for i, p in enumerate(layers):
        last = i == depth - 1
        p_next = None if last else layers[i + 1]
        after_piece = None
        next_projected = []
        if not last:
            def after_piece(xq, b0, nb, p_next=p_next, acc=next_projected):
                acc.append(_project_latent(xq, p_next, b0, nb, acc[-1] if acc else None))

        x, ctx_s = _layer(x, ctx_s, p, not last, last, g_final, projected, after_piece, p_next)
        projected = next_projected[-1] if next_projected else None
    return x
```

```python
import functools

import numpy as np
import jax
import jax.numpy as jnp
from jax import lax
from jax.experimental import pallas as pl
from jax.experimental.pallas import tpu as pltpu
from jax.experimental.pallas import tpu_sc as plsc

F32 = jnp.float32
BF16 = jnp.bfloat16
U32 = jnp.uint32

D_MODEL = 1024
N_MOD = 6
RMS_EPS = 1e-6
NEG_INF = -1e30
GRID_W = 64
NA_HEADS = 8
NA_HEAD_DIM = 64
NA_WIDTH = NA_HEADS * NA_HEAD_DIM
NA_WIN_R = 8
NA_WIN_C = 16
NA_GROUP = 4
SC_WIDTH = 512
GLA_HEADS = 4
GLA_KEY_WIDTH = 512
GLA_VAL_WIDTH = 1024
GLA_DK = GLA_KEY_WIDTH // GLA_HEADS
GLA_DV = GLA_VAL_WIDTH // GLA_HEADS
GLA_GATE_RANK = 16
GLA_GATE_TAU = 16.0
LOG2_E = 1.4426950408889634
N_EXPERTS = 64
N_EXPERT_GROUPS = 8
GROUP_SIZE = N_EXPERTS // N_EXPERT_GROUPS
TOPK_GROUPS = 4
TOP_K = 8
EXPERT_FF = 256
ROUTED_SCALE = 2.5
MOE_BLOCK = 1024

LANE = 128
GLA_C = 128
GLA_STEP_CHUNKS = 4
GLA_LEVELS = tuple(GLA_C >> (i + 1) for i in range(GLA_C.bit_length() - 1))
VMEM_LIMIT = 48 * 1024 * 1024
PROJ_TM, PROJ_TM_CTX, PROJ_TN = 2048, 1024, 1024
MERGE_TM, MERGE_TM_CTX = 512, 256
COMBINE_TM, COMBINE_TM_CTX = 512, 256
ROUTER_TMS = (1024, 512)
MOD_TN = 1536
SC_WINDOW = 128
SC_ROW = 256
SC_PARTS = D_MODEL // 2 // SC_ROW

OFF_V_GLA = 0
OFF_K_NA = 1024
OFF_V_NA = 1536
OFF_K_GLA = 2048
N_KV_MAIN = 2560
OFF_Q_NA = 2560
OFF_B_SC = 3072
OFF_C_SC = 3584
OFF_X_SC = 4096
OFF_Q_GLA = 4608
OFF_R_GLA = 5120
OFF_MERGE = 6144
N_MAIN = 9216


def _cparams(sem, vmem=VMEM_LIMIT):
    return pltpu.CompilerParams(dimension_semantics=sem, vmem_limit_bytes=vmem)


def _dot(a, b):
    return jnp.dot(a, b, preferred_element_type=F32)


def _dot_nt(a, b):
    return lax.dot_general(a, b, (((1,), (1,)), ((), ())), preferred_element_type=F32)


def _dot_tn(a, b):
    return lax.dot_general(a, b, (((0,), (0,)), ((), ())), preferred_element_type=F32)


def _sigmoid(x):
    return 0.5 * jnp.tanh(0.5 * x) + 0.5


def _pack_rows(x):
    n = x.shape[1] // 2
    r = x.astype(BF16).astype(F32)
    lo = pltpu.bitcast(r[:, :n], U32) >> 16
    hi = pltpu.bitcast(r[:, n:], U32)
    return hi | lo


def _store_parts(ref, words):
    for part in range(SC_PARTS):
        ref[part] = words[:, part * SC_ROW:(part + 1) * SC_ROW]


def _load_parts(ref, *lead):
    return jnp.concatenate([ref[(part,) + lead] for part in range(SC_PARTS)], axis=-1)


def _unpack_rows(w):
    lo = pltpu.bitcast(w << 16, F32)
    hi = pltpu.bitcast(w & jnp.uint32(0xFFFF0000), F32)
    return lo, hi


def _mod_kernel(a_ref, w_ref, b_ref, o_ref):
    a = a_ref[...]
    a = a * _sigmoid(a)
    o_ref[...] = _dot(a.astype(BF16), w_ref[0].astype(BF16)) + b_ref[0]


def _mod_vectors(c, c_ctx, w_mod, b_mod, layer):
    b = c.shape[0]
    rows = -(-(b + 1) // 8) * 8
    a = jnp.concatenate([c, c_ctx[None], jnp.zeros((rows - b - 1, D_MODEL), F32)], axis=0)
    n = N_MOD * D_MODEL
    tn = MOD_TN
    out = pl.pallas_call(
        _mod_kernel,
        grid=(n // tn,),
        in_specs=[pl.BlockSpec((rows, D_MODEL), lambda j: (0, 0)),
                  pl.BlockSpec((1, D_MODEL, tn), lambda j: (layer, 0, j)),
                  pl.BlockSpec((1, 1, tn), lambda j: (layer, 0, j))],
        out_specs=pl.BlockSpec((rows, tn), lambda j: (0, j)),
        out_shape=jax.ShapeDtypeStruct((rows, n), F32),
        compiler_params=_cparams(("parallel",)),
        name="mod_vectors",
    )(a, w_mod, b_mod[:, None])
    lat = out[:b].reshape(b, N_MOD, 1, D_MODEL)
    ctx = out[b].reshape(N_MOD, 1, 1, D_MODEL)
    return [lat[:, i] for i in range(N_MOD)], [ctx[i] for i in range(N_MOD)]


def _proj_kernel(x_ref, g_ref, sh_ref, sc_ref, w_ref, wg_ref, *rest):
    o_ref, og_ref, h_ref = rest[-3:]

    @pl.when(pl.program_id(2) == 0)
    def _():
        x = x_ref[0]
        ms = jnp.mean(x * x, axis=-1, keepdims=True)
        h = x * lax.rsqrt(ms + RMS_EPS) * g_ref[...] * (1.0 + sc_ref[0]) + sh_ref[0]
        hb = h.astype(BF16)
        h_ref[...] = hb
        og_ref[0] = _dot(hb, wg_ref[...])

    o_ref[0] = _dot(h_ref[...], w_ref[...]).astype(o_ref.dtype)


W_IN_TILE = 512
W_IN_GATE_SHIFT = 2 * GLA_GATE_RANK


def _prep_w_in_kernel(a_ref, b_ref, o_ref, g_ref):
    t = pl.program_id(0)
    first_lat = N_KV_MAIN // W_IN_TILE
    a = a_ref[0]

    @pl.when(t < first_lat)
    def _():
        o_ref[...] = a.T.astype(BF16)

    @pl.when(t >= first_lat)
    def _():
        moved = jnp.concatenate([a[W_IN_GATE_SHIFT:], b_ref[0]], axis=0)
        scale = jnp.where(t == first_lat, NA_HEAD_DIM ** -0.5, 1.0)
        o_ref[...] = (moved * scale).T.astype(BF16)

    @pl.when(t == first_lat)
    def _():
        head = a[:LANE]
        row = lax.broadcasted_iota(jnp.int32, head.shape, 0)
        g_ref[...] = jnp.where(row < W_IN_GATE_SHIFT, head, 0.0).T.astype(BF16)


def _prep_w_in(w_in, layer):
    d = w_in.shape[1]
    w_t = jnp.swapaxes(w_in, 1, 2)
    first_lat = N_KV_MAIN // W_IN_TILE
    kv_perm = OFF_K_NA // W_IN_TILE

    def a_map(t):
        return (layer, jnp.where(t < first_lat, (t + first_lat - kv_perm) % first_lat, t), 0)

    def b_map(t):
        return (layer, jnp.where(t < first_lat, 0, (t + 1) * (W_IN_TILE // W_IN_GATE_SHIFT)), 0)

    return pl.pallas_call(
        _prep_w_in_kernel,
        grid=(N_MAIN // W_IN_TILE,),
        in_specs=[pl.BlockSpec((1, W_IN_TILE, d), a_map),
                  pl.BlockSpec((1, W_IN_GATE_SHIFT, d), b_map)],
        out_specs=[pl.BlockSpec((d, W_IN_TILE), lambda t: (0, t)),
                   pl.BlockSpec((d, LANE), lambda t: (0, 0))],
        out_shape=[jax.ShapeDtypeStruct((d, N_MAIN), BF16), jax.ShapeDtypeStruct((d, LANE), BF16)],
        compiler_params=_cparams(("arbitrary",)),
        name="prep_w_in",
    )(w_t, w_t)


def _proj_in(x, g, shift, scale, w_main, w_gate, tm, tn, n=None, b0=0, nb=None, prev=None):
    b, s, d = x.shape
    n = w_main.shape[1] if n is None else n
    nb = b if nb is None else nb
    per_batch = shift.shape[0] == b
    mod_map = (lambda bi, i, j: (b0 + bi, 0, 0)) if per_batch else (lambda bi, i, j: (0, 0, 0))
    operands = (x, g[None], shift, scale, w_main, w_gate)
    extend = () if prev is None else tuple(prev)
    return pl.pallas_call(
        _proj_kernel,
        grid=(nb, s // tm, n // tn),
        in_specs=[pl.BlockSpec((1, tm, d), lambda bi, i, j: (b0 + bi, i, 0)),
                  pl.BlockSpec((1, d), lambda bi, i, j: (0, 0)),
                  pl.BlockSpec((1, 1, d), mod_map),
                  pl.BlockSpec((1, 1, d), mod_map),
                  pl.BlockSpec((d, tn), lambda bi, i, j: (0, j)),
                  pl.BlockSpec((d, LANE), lambda bi, i, j: (0, 0))]
                 + [pl.BlockSpec(memory_space=pl.ANY)] * len(extend),
        out_specs=[pl.BlockSpec((1, tm, tn), lambda bi, i, j: (b0 + bi, i, j)),
                   pl.BlockSpec((1, tm, LANE), lambda bi, i, j: (b0 + bi, i, 0))],
        out_shape=[jax.ShapeDtypeStruct((b, s, n), BF16),
                   jax.ShapeDtypeStruct((b, s, LANE), F32)],
        scratch_shapes=[pltpu.VMEM((tm, d), BF16)],
        input_output_aliases={len(operands): 0, len(operands) + 1: 1} if extend else {},
        compiler_params=_cparams(("parallel", "parallel", "arbitrary")),
        name="proj_in",
    )(*operands, *extend)


def _softmax_av(q, keys, vals, biases):
    scores = []
    for kk, bb in zip(keys, biases):
        s = _dot_nt(q, kk)
        scores.append(s if bb is None else s + bb)
    m = scores[0].max(axis=-1, keepdims=True)
    for s in scores[1:]:
        m = jnp.maximum(m, s.max(axis=-1, keepdims=True))
    num = None
    den = None
    for s, vv in zip(scores, vals):
        e = jnp.exp(s - m)
        dsum = e.sum(axis=-1, keepdims=True)
        o = _dot(e.astype(BF16), vv)
        num = o if num is None else num + o
        den = dsum if den is None else den + dsum
    return num / den


def _na_kernel(q_ref, k_ref, v_ref, kc_ref, vc_ref, *rest, rows, kr):
    *bias_refs, o_ref = rest
    kc = kc_ref[0]
    vc = vc_ref[0]
    for j, bias_ref in enumerate(bias_refs):
        r = pl.program_id(1) * len(bias_refs) + j
        row_start = jnp.clip(r - kr // 2, 0, rows - kr)
        start = pl.multiple_of(row_start * GRID_W, GRID_W)
        n_win = kr * GRID_W
        q = q_ref[0, j * GRID_W:(j + 1) * GRID_W, :]
        kw = k_ref[0, pl.ds(start, n_win), :]
        vw = v_ref[0, pl.ds(start, n_win), :]
        o_ref[0, j * GRID_W:(j + 1) * GRID_W, :] = _na_row(q, kw, vw, kc, vc, bias_ref).astype(o_ref.dtype)


def _na_row(q, kw, vw, kc, vc, bias_ref):
    gw = NA_GROUP * NA_HEAD_DIM
    stacked = (NA_GROUP * GRID_W, gw)
    on_head = (lax.broadcasted_iota(jnp.int32, stacked, 0) // GRID_W
               == lax.broadcasted_iota(jnp.int32, stacked, 1) // NA_HEAD_DIM)
    outs = []
    for g in range(NA_HEADS // NA_GROUP):
        sl = slice(g * gw, (g + 1) * gw)
        q_all = jnp.where(on_head, jnp.concatenate([q[:, sl]] * NA_GROUP, axis=0), jnp.zeros((), q.dtype))
        bias = bias_ref[0, g * NA_GROUP * GRID_W:(g + 1) * NA_GROUP * GRID_W, :]
        o_all = _softmax_av(q_all, [kw[:, sl], kc[:, sl]], [vw[:, sl], vc[:, sl]], [bias, None])
        o_all = jnp.where(on_head, o_all, 0.0).reshape(NA_GROUP, GRID_W, gw)
        outs.append(o_all.sum(axis=0))
    return jnp.concatenate(outs, axis=-1)


def _na_bias_table(rpb, rows, kr):
    col = np.arange(GRID_W)
    col_start = np.clip(col - NA_WIN_C // 2, 0, GRID_W - NA_WIN_C)
    col_ok = (col[None, :] >= col_start[:, None]) & (col[None, :] < col_start[:, None] + NA_WIN_C)
    d_col = np.clip(col[None, :] - col[:, None], -(NA_WIN_C - 1), NA_WIN_C - 1) + NA_WIN_C - 1
    n_dr, n_dc = rpb.shape[1], rpb.shape[2]
    onehot = jnp.asarray((d_col.reshape(-1)[None, :] == np.arange(n_dc)[:, None]).astype(np.float32))
    by_col = jnp.dot(rpb.astype(F32).reshape(NA_HEADS * n_dr, n_dc), onehot, precision=lax.Precision.HIGHEST)
    by_col = by_col.reshape(NA_HEADS, n_dr, GRID_W, GRID_W)
    by_col = jnp.where(col_ok[None, None], by_col, NEG_INF)
    tables = []
    for o in range(kr):
        lo = NA_WIN_R - 1 - o
        tables.append(by_col[:, lo:lo + kr].transpose(0, 2, 1, 3).reshape(NA_HEADS, GRID_W, kr * GRID_W))
    return jnp.stack(tables).reshape(kr, NA_HEADS * GRID_W, kr * GRID_W)


def _na_latent(main, main_ctx, bias):
    b, s, _ = main.shape
    sc = main_ctx.shape[1]
    rows = s // GRID_W
    kr = min(NA_WIN_R, rows)
    w = NA_WIDTH

    per_step = next(n for n in (8, 4, 2, 1) if rows % n == 0)

    def bias_spec(j):
        def bias_map(bi, i):
            r = i * per_step + j
            return (r - jnp.clip(r - kr // 2, 0, rows - kr), 0, 0)
        return pl.BlockSpec((1, NA_HEADS * GRID_W, kr * GRID_W), bias_map)

    return pl.pallas_call(
        functools.partial(_na_kernel, rows=rows, kr=kr),
        grid=(b, rows // per_step),
        in_specs=[pl.BlockSpec((1, per_step * GRID_W, w), lambda bi, i: (bi, i, OFF_Q_NA // w)),
                  pl.BlockSpec((1, s, w), lambda bi, i: (bi, 0, OFF_K_NA // w)),
                  pl.BlockSpec((1, s, w), lambda bi, i: (bi, 0, OFF_V_NA // w)),
                  pl.BlockSpec((1, sc, w), lambda bi, i: (bi, 0, OFF_K_NA // w)),
                  pl.BlockSpec((1, sc, w), lambda bi, i: (bi, 0, OFF_V_NA // w))]
                 + [bias_spec(j) for j in range(per_step)],
        out_specs=pl.BlockSpec((1, per_step * GRID_W, w), lambda bi, i: (bi, i, 0)),
        out_shape=jax.ShapeDtypeStruct((b, s, w), BF16),
        compiler_params=_cparams(("parallel", "arbitrary")),
        name="na_latent",
    )(main, main, main, main_ctx, main_ctx, *([bias] * per_step))


def _dense_attn_kernel(q_ref, k_ref, v_ref, o_ref):
    q = q_ref[0]
    k = k_ref[0]
    v = v_ref[0]
    outs = []
    for h in range(NA_HEADS):
        sl = slice(h * NA_HEAD_DIM, (h + 1) * NA_HEAD_DIM)
        outs.append(_softmax_av(q[:, sl], [k[:, sl]], [v[:, sl]], [None]))
    o_ref[0] = jnp.concatenate(outs, axis=-1).astype(o_ref.dtype)


def _dense_attn(main_ctx):
    b, sc, _ = main_ctx.shape
    w = NA_WIDTH
    return pl.pallas_call(
        _dense_attn_kernel,
        grid=(b,),
        in_specs=[pl.BlockSpec((1, sc, w), lambda bi: (bi, 0, OFF_Q_NA // w)),
                  pl.BlockSpec((1, sc, w), lambda bi: (bi, 0, OFF_K_NA // w)),
                  pl.BlockSpec((1, sc, w), lambda bi: (bi, 0, OFF_V_NA // w))],
        out_specs=pl.BlockSpec((1, sc, w), lambda bi: (bi, 0, 0)),
        out_shape=jax.ShapeDtypeStruct((b, sc, w), BF16),
        compiler_params=_cparams(("parallel",)),
        name="ctx_attn",
    )(main_ctx, main_ctx, main_ctx)


HALO = 16


def _gla_matrices(reverse):
    c = GLA_C
    t = np.arange(c)[:, None]
    m = np.arange(c)[None, :]
    blocks = [m <= t, m > t]
    for b in GLA_LEVELS:
        first = (t // (2 * b)) * (2 * b) + b
        is_q = (t & b) != 0
        blocks.append(np.where(is_q, (m > first) & (m <= t), (m > t) & (m <= first)))
    mats = np.stack(blocks).astype(np.float32)
    if reverse:
        mats = mats[:, ::-1, ::-1]
    mats = mats.reshape(-1, c)
    return jnp.asarray(np.concatenate([mats, mats], axis=1), dtype=BF16)


def _split_bf16(x):
    hi = x.astype(BF16)
    return hi, (x - hi.astype(F32)).astype(BF16)


def _pair_block_diag(x):
    lane = lax.broadcasted_iota(jnp.int32, x.shape, 1)
    zero = jnp.zeros((), x.dtype)
    return jnp.concatenate([jnp.where(lane < GLA_DK, x, zero), jnp.where(lane >= GLA_DK, x, zero)], axis=0)


def _gla_kernel(*refs, emit):
    n_in = 9 if emit else 8
    n_out = 2 if emit else 1
    ins, outs, scratch = refs[:2 * n_in], refs[2 * n_in:2 * (n_in + n_out)], refs[2 * (n_in + n_out):]
    ins = [ins[d * n_in:(d + 1) * n_in] for d in range(2)]
    outs = [outs[d * n_out:(d + 1) * n_out] for d in range(2)]
    step = pl.program_id(1)

    @pl.when(step == 0)
    def _():
        for d in range(2):
            scratch[d][...] = ins[d][-1][0]

    pending = [_gla_direction(ins[d][:-1], outs[d][:-1], scratch[d], reverse, emit)
               for d, reverse in enumerate((False, True))]
    while pending:
        pending = [stages for stages in pending if next(stages, "done") != "done"]

    @pl.when(step == pl.num_programs(1) - 1)
    def _():
        for d in range(2):
            outs[d][-1][0] = scratch[d][...]


def _gla_direction(ins, outs, st_ref, reverse, emit):
    gate_ref = ins[-5]
    order = range(gate_ref.shape[1] // GLA_C)
    for sub in (reversed(order) if reverse else order):
        yield from _gla_chunk(ins, outs, st_ref, reverse, emit, slice(sub * GLA_C, (sub + 1) * GLA_C))


def _gla_chunk(ins, outs, st_ref, reverse, emit, rows):
    if emit:
        q_ref, k_ref, v_ref, gt_ref, w2a_ref, w2b_ref, b2_ref, a_ref = ins
        (o_ref,) = outs
    else:
        k_ref, v_ref, gt_ref, w2a_ref, w2b_ref, b2_ref, a_ref = ins
    c = GLA_C
    pw = 2 * GLA_DK

    lr_hi, lr_lo = _split_bf16(gt_ref[0, rows])
    logit = (_dot(jnp.concatenate([lr_hi, lr_lo], axis=1), w2a_ref[...]) + _dot(lr_hi, w2b_ref[...])
             + b2_ref[...])
    g = (jnp.minimum(logit, 0.0) - jnp.log1p(jnp.exp(-jnp.abs(logit)))) * (LOG2_E / GLA_GATE_TAU)
    g_hi, g_lo = _split_bf16(g)
    args = _dot(a_ref[...], jnp.concatenate([g_hi, g_lo], axis=0))
    cum = args[0:c]
    rem = args[c:2 * c]
    last_row = cum[0:1] if reverse else cum[c - 1:c]
    yield

    k = k_ref[0, rows].astype(F32)
    v = v_ref[0, rows]
    atts = []
    if emit:
        q = q_ref[0, rows].astype(F32) * (GLA_DK ** -0.5)
        row_t = lax.broadcasted_iota(jnp.int32, (c, pw), 0)
        si = lax.broadcasted_iota(jnp.int32, (2 * c, c), 0) & (c - 1)
        ti = lax.broadcasted_iota(jnp.int32, (2 * c, c), 1)
        if reverse:
            row_t, ti, si = c - 1 - row_t, c - 1 - ti, c - 1 - si
        for hp in range(GLA_HEADS // 2):
            cs = slice(hp * pw, (hp + 1) * pw)
            qp, kp = q[:, cs], k[:, cs]
            att = jnp.where(ti == si, _dot_nt(_pair_block_diag(kp.astype(BF16)), qp.astype(BF16)), 0.0)
            for l, b in enumerate(GLA_LEVELS):
                x = (jnp.exp2(args[(2 + l) * c:(3 + l) * c, cs])
                     * jnp.where((row_t & b) != 0, qp, kp)).astype(BF16)
                pair = (((ti ^ si) >> (b.bit_length() - 1)) == 1) & ((ti & b) != 0)
                att = jnp.where(pair, _dot_nt(_pair_block_diag(x), x), att)
                yield
            atts.append(att.astype(BF16))

    outs = []
    for h in range(GLA_HEADS):
        sl = slice(h * GLA_DK, (h + 1) * GLA_DK)
        kh = k[:, sl]
        vh = v[:, h * GLA_DV:(h + 1) * GLA_DV]
        state = st_ref[h]
        kd = (kh * jnp.exp2(rem[:, sl])).astype(BF16)
        decay = jnp.exp2(jnp.broadcast_to(last_row[:, sl], (GLA_DK, GLA_DK))).T
        decay = jnp.concatenate([decay] * (GLA_DV // GLA_DK), axis=1)
        if emit:
            qd = (q[:, sl] * jnp.exp2(cum[:, sl])).astype(BF16)
            att_t = atts[h // 2][(h % 2) * c:(h % 2 + 1) * c]
            both = _dot_tn(jnp.concatenate([att_t, kd], axis=1), vh)
            outs.append(_dot(qd, state.astype(BF16)) + both[:c])
            st_ref[h] = decay * state + both[c:]
        else:
            st_ref[h] = decay * state + _dot_tn(kd, vh)
        yield

    if emit:
        o_ref[0, rows] = jnp.concatenate(outs, axis=-1).astype(o_ref.dtype)


def _gla_scan(main, gate, gate_ws, s0s, emit):
    b, l, _ = main.shape
    step_rows = GLA_C * min(GLA_STEP_CHUNKS, l // GLA_C)
    n = l // step_rows
    const = lambda arr: pl.BlockSpec(arr.shape, lambda bi, s: (0,) * arr.ndim)
    state_spec = pl.BlockSpec((1, GLA_HEADS, GLA_DK, GLA_DV), lambda bi, s: (bi, 0, 0, 0))
    state_shape = jax.ShapeDtypeStruct((b, GLA_HEADS, GLA_DK, GLA_DV), F32)
    in_specs, args, out_specs, out_shape = [], [], [], []
    for reverse in (False, True):
        amat = _gla_matrices(reverse)
        w2a, w2b, b2 = gate_ws[reverse]

        def col(block, reverse=reverse):
            return lambda bi, s: (bi, n - 1 - s if reverse else s, block)

        if emit:
            in_specs.append(pl.BlockSpec((1, step_rows, GLA_KEY_WIDTH), col(OFF_Q_GLA // GLA_KEY_WIDTH)))
            args.append(main)
            out_specs.append(pl.BlockSpec((1, step_rows, GLA_VAL_WIDTH), col(0)))
            out_shape.append(jax.ShapeDtypeStruct((b, l, GLA_VAL_WIDTH), BF16))
        in_specs += [pl.BlockSpec((1, step_rows, GLA_KEY_WIDTH), col(OFF_K_GLA // GLA_KEY_WIDTH)),
                     pl.BlockSpec((1, step_rows, GLA_VAL_WIDTH), col(OFF_V_GLA // GLA_VAL_WIDTH)),
                     pl.BlockSpec((1, step_rows, LANE), col(0)),
                     const(w2a), const(w2b), const(b2), const(amat), state_spec]
        args += [main, main, gate, w2a, w2b, b2, amat, s0s[reverse]]
        out_specs.append(state_spec)
        out_shape.append(state_shape)
    res = pl.pallas_call(
        functools.partial(_gla_kernel, emit=emit),
        grid=(b, n),
        in_specs=in_specs,
        out_specs=out_specs,
        out_shape=out_shape,
        scratch_shapes=[pltpu.VMEM((GLA_HEADS, GLA_DK, GLA_DV), F32)] * 2,
        compiler_params=_cparams(("parallel", "arbitrary")),
        name="gla_scan",
    )(*args)
    return (res[0], res[2], res[1], res[3]) if emit else (None, None, res[0], res[1])


def _gla_gate_weights(gate_w, gate_b):
    out = []
    for dr in range(2):
        w = jnp.zeros((LANE, GLA_KEY_WIDTH), F32)
        w = w.at[dr * GLA_GATE_RANK:(dr + 1) * GLA_GATE_RANK].set(gate_w[dr])
        w_hi = w.astype(BF16)
        w_lo = (w - w_hi.astype(F32)).astype(BF16)
        out.append((jnp.concatenate([w_hi, w_hi], axis=0), w_lo, gate_b[dr][None]))
    return out


def _conv_tile(b_ref, c_ref, x_ref, cp_ref, xp_ref, cn_ref, xn_ref, w_ref):
    i, n = pl.program_id(1), pl.num_programs(1)
    u = c_ref[0].astype(F32) * x_ref[0].astype(F32)
    tm = u.shape[0]
    before = jnp.where(i > 0, 1.0, 0.0) * (cp_ref[0, HALO - 1:HALO].astype(F32) * xp_ref[0, HALO - 1:HALO].astype(F32))
    after = jnp.where(i < n - 1, 1.0, 0.0) * (cn_ref[0, 0:1].astype(F32) * xn_ref[0, 0:1].astype(F32))
    t = lax.broadcasted_iota(jnp.int32, u.shape, 0)
    prev = jnp.where(t == 0, before, pltpu.roll(u, 1, axis=0))
    nxt = jnp.where(t == tm - 1, after, pltpu.roll(u, tm - 1, axis=0))
    w = w_ref[...]
    return b_ref[0].astype(F32) * (prev * w[0:1] + u * w[1:2] + nxt * w[2:3])


def _merge_kernel(ona_ref, bsc_ref, csc_ref, xsc_ref, cp_ref, xp_ref, cn_ref, xn_ref, cw_ref, of_ref, ob_ref,
                  r_ref, gna_ref, gsc_ref, ggl_ref, x_ref, gt_ref,
                  gn_ref, wna_ref, wsc_ref, wgl_ref, wo_ref, g2_ref, sh2_ref, sc2_ref, wr_ref, *rest):
    xo_ref, h2_ref, hp_ref, lg_ref = rest[-4:]
    o_sc = _conv_tile(bsc_ref, csc_ref, xsc_ref, cp_ref, xp_ref, cn_ref, xn_ref, cw_ref).astype(BF16)
    o = of_ref[0].astype(F32) + ob_ref[0].astype(F32)
    normed = []
    for h in range(GLA_HEADS):
        oh = o[:, h * GLA_DV:(h + 1) * GLA_DV]
        ms = jnp.mean(oh * oh, axis=-1, keepdims=True)
        normed.append(oh * lax.rsqrt(ms + RMS_EPS))
    r = r_ref[0].astype(F32)
    y_gla = jnp.concatenate(normed, axis=-1) * gn_ref[...] * (r * _sigmoid(r))
    y = (_sigmoid(gna_ref[0].astype(F32)) * _dot(ona_ref[0], wna_ref[...])
         + _sigmoid(gsc_ref[0].astype(F32)) * _dot(o_sc, wsc_ref[...])
         + _sigmoid(ggl_ref[0].astype(F32)) * _dot(y_gla.astype(BF16), wgl_ref[...]))
    xn = x_ref[0] + gt_ref[0] * _dot(y.astype(BF16), wo_ref[...])
    xo_ref[0] = xn
    ms = jnp.mean(xn * xn, axis=-1, keepdims=True)
    h2 = xn * lax.rsqrt(ms + RMS_EPS) * g2_ref[...] * (1.0 + sc2_ref[0]) + sh2_ref[0]
    h2b = h2.astype(BF16)
    h2_ref[0] = h2b
    _store_parts(hp_ref, _pack_rows(h2))
    lg_ref[...] = _dot(h2b, wr_ref[...])


def _merge(o_na, conv_w, o_f, o_b, main, x, gt1, gn, w_na, w_sc, w_gla, w_out, g2, sh2, sc2, w_router, tm,
           n_routed, tok_off, routed=None):
    b, s, d = x.shape
    per_batch = gt1.shape[0] == b
    mod_map = (lambda bi, i: (bi, 0, 0)) if per_batch else (lambda bi, i: (0, 0, 0))
    tok = lambda width, blk: pl.BlockSpec((1, tm, width), lambda bi, i: (bi, i, blk))
    full = lambda arr: pl.BlockSpec(arr.shape, lambda bi, i: (0,) * arr.ndim)
    mod = pl.BlockSpec((1, 1, d), mod_map)
    gn_t = jnp.tile(gn, GLA_HEADS)[None]
    g2_t = g2[None]
    per_tile = tm // HALO
    last_halo = s // HALO - 1
    halo_prev = lambda blk: pl.BlockSpec(
        (1, HALO, SC_WIDTH), lambda bi, i: (bi, jnp.maximum(i * per_tile - 1, 0), blk))
    halo_next = lambda blk: pl.BlockSpec(
        (1, HALO, SC_WIDTH), lambda bi, i: (bi, jnp.minimum((i + 1) * per_tile, last_halo), blk))
    col_b, col_c, col_x = OFF_B_SC // SC_WIDTH, OFF_C_SC // SC_WIDTH, OFF_X_SC // SC_WIDTH
    extend = () if routed is None else tuple(routed)
    operands = (o_na, main, main, main, main, main, main, main, conv_w, o_f, o_b, main, main, main, main, x, gt1,
                gn_t, w_na, w_sc, w_gla, w_out, g2_t, sh2, sc2, w_router)
    n_in = len(operands)
    nblk = s // tm
    blk0 = tok_off // tm
    return pl.pallas_call(
        _merge_kernel,
        grid=(b, s // tm),
        in_specs=[tok(NA_WIDTH, 0), tok(SC_WIDTH, col_b), tok(SC_WIDTH, col_c), tok(SC_WIDTH, col_x),
                  halo_prev(col_c), halo_prev(col_x), halo_next(col_c), halo_next(col_x), full(conv_w),
                  tok(GLA_VAL_WIDTH, 0), tok(GLA_VAL_WIDTH, 0),
                  tok(d, OFF_R_GLA // d), tok(d, OFF_MERGE // d), tok(d, OFF_MERGE // d + 1),
                  tok(d, OFF_MERGE // d + 2), tok(d, 0), mod,
                  full(gn_t), full(w_na), full(w_sc), full(w_gla), full(w_out), full(g2_t), mod, mod,
                  full(w_router)] + [pl.BlockSpec(memory_space=pl.ANY)] * len(extend),
        out_specs=[tok(d, 0), tok(d, 0),
                   pl.BlockSpec((SC_PARTS, tm, SC_ROW), lambda bi, i: (0, blk0 + bi * nblk + i, 0)),
                   pl.BlockSpec((tm, LANE), lambda bi, i: (blk0 + bi * nblk + i, 0))],
        out_shape=[jax.ShapeDtypeStruct((b, s, d), F32),
                   jax.ShapeDtypeStruct((b, s, d), BF16),
                   jax.ShapeDtypeStruct((SC_PARTS, n_routed, SC_ROW), U32),
                   jax.ShapeDtypeStruct((n_routed, LANE), F32)],
        input_output_aliases={n_in: 2, n_in + 1: 3} if extend else {},
        compiler_params=_cparams(("parallel", "parallel")),
        name="merge",
    )(*operands, *extend)


def _router_kernel(lg_ref, br_ref, tri_ref, eidx_ref, gw_ref, rank_ref, cnt_ref, carry_ref):
    tm = lg_ref.shape[0]

    @pl.when(pl.program_id(0) == 0)
    def _():
        carry_ref[...] = jnp.zeros_like(carry_ref)

    scores = _sigmoid(lg_ref[...].T[:N_EXPERTS])
    sel = scores + br_ref[...]
    neg = -jnp.inf

    sel3 = sel.reshape(N_EXPERT_GROUPS, GROUP_SIZE, tm)
    i3 = lax.broadcasted_iota(jnp.int32, sel3.shape, 1)
    m1 = sel3.max(axis=1, keepdims=True)
    first = jnp.where(sel3 == m1, i3, GROUP_SIZE).min(axis=1, keepdims=True)
    m2 = jnp.where(i3 == first, neg, sel3).max(axis=1, keepdims=True)
    gscore = (m1 + m2)[:, 0, :]

    gi = lax.broadcasted_iota(jnp.int32, gscore.shape, 0)
    gmask = jnp.zeros(gscore.shape, jnp.bool_)
    for _ in range(TOPK_GROUPS):
        m = gscore.max(axis=0, keepdims=True)
        pick = gi == jnp.where(gscore == m, gi, N_EXPERT_GROUPS).min(axis=0, keepdims=True)
        gmask = gmask | pick
        gscore = jnp.where(pick, neg, gscore)
    emask = jnp.broadcast_to(gmask[:, None, :], sel3.shape).reshape(N_EXPERTS, tm)
    sel = jnp.where(emask, sel, neg)

    ei = lax.broadcasted_iota(jnp.int32, sel.shape, 0)
    picks, idxs, ws = [], [], []
    for _ in range(TOP_K):
        m = sel.max(axis=0, keepdims=True)
        idx = jnp.where(sel == m, ei, N_EXPERTS).min(axis=0, keepdims=True)
        pick = ei == idx
        picks.append(pick)
        idxs.append(idx)
        ws.append(jnp.where(pick, scores, 0.0).sum(axis=0, keepdims=True))
        sel = jnp.where(pick, neg, sel)
    w = jnp.concatenate(ws, axis=0)
    gw_ref[...] = w / w.sum(axis=0, keepdims=True) * ROUTED_SCALE
    eidx_ref[...] = jnp.concatenate(idxs, axis=0)

    onehot = picks[0]
    for p in picks[1:]:
        onehot = onehot | p
    onehot = jnp.where(onehot, 1.0, 0.0).astype(BF16)
    before = _dot(onehot, tri_ref[...]) + jnp.tile(carry_ref[...], (1, tm // LANE))
    rank_ref[...] = jnp.concatenate(
        [jnp.where(p, before, 0.0).sum(axis=0, keepdims=True) for p in picks], axis=0).astype(jnp.int32)
    carry_ref[...] += _dot(onehot, jnp.ones((tm, LANE), BF16))
    cnt_ref[...] = carry_ref[...]


def _route(logits, b_router):
    t = logits.shape[0]
    tm = next(n for n in ROUTER_TMS if t % n == 0)
    br = jnp.broadcast_to(b_router.astype(F32)[:, None], (N_EXPERTS, tm))
    tri = jnp.asarray(np.triu(np.ones((tm, tm), np.float32), 1), dtype=BF16)
    kt = lambda dt: jax.ShapeDtypeStruct((TOP_K, t), dt)
    eidx, gw, rank, cnt = pl.pallas_call(
        _router_kernel,
        grid=(t // tm,),
        in_specs=[pl.BlockSpec((tm, LANE), lambda i: (i, 0)),
                  pl.BlockSpec((N_EXPERTS, tm), lambda i: (0, 0)),
                  pl.BlockSpec((tm, tm), lambda i: (0, 0))],
        out_specs=[pl.BlockSpec((TOP_K, tm), lambda i: (0, i)),
                   pl.BlockSpec((TOP_K, tm), lambda i: (0, i)),
                   pl.BlockSpec((TOP_K, tm), lambda i: (0, i)),
                   pl.BlockSpec((N_EXPERTS, LANE), lambda i: (0, 0))],
        out_shape=[kt(jnp.int32), kt(F32), kt(jnp.int32),
                   jax.ShapeDtypeStruct((N_EXPERTS, LANE), F32)],
        scratch_shapes=[pltpu.VMEM((N_EXPERTS, LANE), F32)],
        compiler_params=_cparams(("arbitrary",)),
        name="router",
    )(logits, br, tri)
    return eidx, gw, rank, cnt[:, 0].astype(jnp.int32)


def _sc_mesh():
    return plsc.VectorSubcoreMesh(core_axis_name="core", subcore_axis_name="subcore")


def _dispatch_rows(xp, dest, slots):
    parts, t, _ = xp.shape

    @pl.kernel(out_type=jax.ShapeDtypeStruct((parts, slots, SC_ROW), xp.dtype), mesh=_sc_mesh(),
               scratch_types=[], name="moe_dispatch")
    def run(x_hbm, d_hbm, o_hbm):
        for part in range(parts):
            out_part = o_hbm.at[part]

            def body(x_vmem, d_vmem, out_part=out_part):
                for k in range(TOP_K):
                    pltpu.sync_copy(x_vmem, out_part.at[d_vmem.at[k]])

            pltpu.emit_pipeline(
                body,
                grid=(t // SC_WINDOW,),
                in_specs=[pl.BlockSpec((SC_WINDOW, SC_ROW), lambda i: (i, 0)),
                          pl.BlockSpec((TOP_K, SC_WINDOW), lambda i: (0, i))],
                out_specs=[],
                core_axis_name=("core", "subcore"),
                dimension_semantics=(pltpu.PARALLEL,),
            )(x_hbm.at[part], d_hbm)

    return run(xp, dest)


def _gather_rows(yp, dest, t0, n):
    parts = yp.shape[0]
    nwin = n // SC_WINDOW
    win0 = t0 // SC_WINDOW

    @pl.kernel(out_type=jax.ShapeDtypeStruct((parts, TOP_K * n, SC_ROW), yp.dtype), mesh=_sc_mesh(),
               scratch_types=[], name="moe_gather")
    def run(y_hbm, d_hbm, o_hbm):
        for part in range(parts):
            table = y_hbm.at[part]

            def body(d_vmem, o_vmem, table=table):
                pltpu.sync_copy(table.at[d_vmem.at[0]], o_vmem)

            pltpu.emit_pipeline(
                body,
                grid=(TOP_K, nwin),
                in_specs=[pl.BlockSpec((1, SC_WINDOW), lambda k, j: (k, win0 + j))],
                out_specs=[pl.BlockSpec((SC_WINDOW, SC_ROW), lambda k, j: (k * nwin + j, 0))],
                core_axis_name=("core", "subcore"),
                dimension_semantics=(pltpu.PARALLEL, pltpu.PARALLEL),
            )(d_hbm, o_hbm.at[part])

    return run(yp, dest).reshape(parts, TOP_K, n, SC_ROW)


def _expert_kernel(be_ref, bv_ref, bs_ref, nx_ref, sl_ref, x_ref, wg_hbm, wu_hbm, wd_hbm, o_ref,
                   wg_f, wu_f, wd_f, wg_s, wu_s, wd_s, sems, *, layer):
    i = pl.program_id(0)
    valid = bv_ref[i]
    expert = be_ref[i]
    new_expert = (i == 0) | (expert != be_ref[jnp.maximum(i - 1, 0)])
    slot = sl_ref[i]

    def fetch(which, into):
        return [pltpu.make_async_copy(src.at[layer, which], dst.at[into], sems.at[into, j])
                for j, (src, dst) in enumerate(((wg_hbm, wg_f), (wu_hbm, wu_f), (wd_hbm, wd_f)))]

    @pl.when(i == 0)
    def _():
        for cp in fetch(expert, slot):
            cp.start()

    @pl.when(new_expert)
    def _():
        for cp in fetch(expert, slot):
            cp.wait()
        upcoming = nx_ref[i]

        @pl.when(upcoming >= 0)
        def _():
            for cp in fetch(upcoming, 1 - slot):
                cp.start()

        wg_s[...] = wg_f[slot].astype(BF16)
        wu_s[...] = wu_f[slot].astype(BF16)
        wd_s[...] = wd_f[slot].astype(BF16)

    @pl.when(valid > 0)
    def _():
        w = _load_parts(x_ref)
        row = lax.broadcasted_iota(jnp.int32, w.shape, 0)
        w = jnp.where(row < valid, w, jnp.uint32(0))
        lo, hi = _unpack_rows(w)
        x = jnp.concatenate([lo, hi], axis=1).astype(BF16)
        a = _dot(x, wg_s[...])
        hid = a * _sigmoid(a) * _dot(x, wu_s[...])
        _store_parts(o_ref, _pack_rows(_dot(hid.astype(BF16), wd_s[...])))


def _experts(xs, blk_e, blk_valid, blk_src, blk_next, blk_slot, layer, w_gate, w_up, w_down):
    parts, slots, _ = xs.shape
    d = D_MODEL
    nb = slots // MOE_BLOCK
    data = pl.BlockSpec((parts, MOE_BLOCK, SC_ROW), lambda i, be, bv, bs, nx, sl: (0, bs[i], 0))
    stage = lambda shape: pltpu.VMEM((2,) + shape, F32)
    return pl.pallas_call(
        functools.partial(_expert_kernel, layer=layer),
        grid_spec=pltpu.PrefetchScalarGridSpec(
            num_scalar_prefetch=5,
            grid=(nb,),
            in_specs=[data] + [pl.BlockSpec(memory_space=pl.ANY)] * 3,
            out_specs=data,
            scratch_shapes=[stage((d, EXPERT_FF)), stage((d, EXPERT_FF)), stage((EXPERT_FF, d)),
                            pltpu.VMEM((d, EXPERT_FF), BF16), pltpu.VMEM((d, EXPERT_FF), BF16),
                            pltpu.VMEM((EXPERT_FF, d), BF16), pltpu.SemaphoreType.DMA((2, 3))]),
        out_shape=jax.ShapeDtypeStruct((parts, slots, SC_ROW), U32),
        compiler_params=_cparams(("arbitrary",)),
        name="experts",
    )(blk_e, blk_valid, blk_src, blk_next, blk_slot, xs, w_gate, w_up, w_down)


def _combine_kernel(yg_ref, gw_ref, h_ref, x_ref, gt_ref, wsg_ref, wsu_ref, wsd_ref, gf_ref, o_ref, *, final):
    h = h_ref[0]
    a = _dot(h, wsg_ref[...])
    hid = a * _sigmoid(a) * _dot(h, wsu_ref[...])
    y = _dot(hid.astype(BF16), wsd_ref[...])
    gw = gw_ref[...]
    y_lo = y[:, :D_MODEL // 2]
    y_hi = y[:, D_MODEL // 2:]
    for k in range(TOP_K):
        lo, hi = _unpack_rows(_load_parts(yg_ref, k))
        y_lo = y_lo + gw[:, k:k + 1] * lo
        y_hi = y_hi + gw[:, k:k + 1] * hi
    y = jnp.concatenate([y_lo, y_hi], axis=1)
    xn = x_ref[0] + gt_ref[0] * y
    if final:
        ms = jnp.mean(xn * xn, axis=-1, keepdims=True)
        xn = xn * lax.rsqrt(ms + RMS_EPS) * gf_ref[...]
    o_ref[0] = xn


def _combine(yg, gw, tok_off, h2, x, gt2, b0, nb, ws_gate, ws_up, ws_down, g_final, final, tm):
    b, s, d = x.shape
    per_batch = gt2.shape[0] == b
    mod_map = (lambda bi, i: (b0 + bi, 0, 0)) if per_batch else (lambda bi, i: (0, 0, 0))
    full = lambda arr: pl.BlockSpec(arr.shape, lambda bi, i: (0,) * arr.ndim)
    tok = lambda width: pl.BlockSpec((1, tm, width), lambda bi, i: (b0 + bi, i, 0))
    gf = g_final[None]
    nblk = s // tm
    blk0 = (tok_off + b0 * s) // tm
    return pl.pallas_call(
        functools.partial(_combine_kernel, final=final),
        grid=(nb, nblk),
        in_specs=[pl.BlockSpec((SC_PARTS, TOP_K, tm, SC_ROW), lambda bi, i: (0, 0, bi * nblk + i, 0)),
                  pl.BlockSpec((tm, TOP_K), lambda bi, i: (blk0 + bi * nblk + i, 0)),
                  tok(d), tok(d),
                  pl.BlockSpec((1, 1, d), mod_map),
                  full(ws_gate), full(ws_up), full(ws_down), full(gf)],
        out_specs=tok(d),
        out_shape=jax.ShapeDtypeStruct((b, s, d), F32),
        input_output_aliases={3: 0},
        compiler_params=_cparams(("parallel", "parallel")),
        name="combine",
    )(yg, gw, h2, x, gt2, ws_gate, ws_up, ws_down, gf)


def _project_latent(x, p, b0=0, nb=None, prev=None):
    w_main, w_gate = p['prep']['w_in']
    mods = p['prep']['mods']
    return _proj_in(x, p['g_norm1'], mods[0], mods[1], w_main, w_gate, tm=min(PROJ_TM, x.shape[1]), tn=PROJ_TN,
                    b0=b0, nb=nb, prev=prev)


def _layer(x, ctx_s, p, ctx_out, final, g_final, projected=None, after_piece=None, p_next=None):
    b, s, d = x.shape
    sc = ctx_s.shape[1]
    prep = p['prep']
    sh1, sc1, gt1, sh2, sc2, gt2 = prep['mods']
    csh1, csc1, cgt1, csh2, csc2, cgt2 = prep['mods_ctx']

    w_main, w_gate = prep['w_in']
    main, gate = _project_latent(x, p) if projected is None else projected
    ctx_flat = ctx_s.reshape(1, b * sc, d)
    n_ctx, tn_ctx = (N_MAIN, PROJ_TN) if ctx_out else (N_KV_MAIN, N_KV_MAIN // 2)
    main_c, gate_c = _proj_in(ctx_flat, p['g_norm1'], csh1, csc1, w_main, w_gate, tm=min(PROJ_TM_CTX, b * sc),
                              tn=tn_ctx, n=n_ctx)
    main_c = main_c.reshape(b, sc, n_ctx)
    gate_c = gate_c.reshape(b, sc, LANE)

    o_na = _na_latent(main, main_c, prep['na_bias'])

    gate_ws = prep['gate_ws']
    s0 = jnp.zeros((b, GLA_HEADS, GLA_DK, GLA_DV), F32)
    o_cf, o_cb, st_f, st_b = _gla_scan(main_c, gate_c, gate_ws, (s0, s0), ctx_out)
    o_f, o_b, _, _ = _gla_scan(main, gate, gate_ws, (st_f, st_b), True)

    w_na, w_sc, w_gla, w_out, w_router = (prep[name] for name in ('w_na', 'w_sc', 'w_gla', 'w_out', 'w_router'))
    n_lat = b * s
    t = n_lat + (b * sc if ctx_out else 0)
    x, h2, hp_all, lg_all = _merge(o_na, p['conv_w'], o_f, o_b, main, x, gt1, p['gla_norm_g'], w_na, w_sc, w_gla,
                               w_out, p['g_norm2'], sh2, sc2, w_router, tm=min(MERGE_TM, s), n_routed=t, tok_off=0)
    if ctx_out:
        o_na_c = _dense_attn(main_c)
        ctx_s, h2_c, hp_all, lg_all = _merge(o_na_c, p['conv_w'], o_cf, o_cb, main_c, ctx_s, cgt1, p['gla_norm_g'],
                                       w_na, w_sc, w_gla, w_out, p['g_norm2'], csh2, csc2, w_router,
                                       tm=min(MERGE_TM_CTX, sc), n_routed=t, tok_off=n_lat, routed=(hp_all, lg_all))

    eidx, gw, rank, counts = _route(lg_all, p['b_router'])
    padded = (counts + MOE_BLOCK - 1) // MOE_BLOCK * MOE_BLOCK
    pad_end = jnp.cumsum(padded)
    pad_start = pad_end - padded
    onehot = eidx[:, :, None] == jnp.arange(N_EXPERTS, dtype=jnp.int32)
    dest = jnp.sum(jnp.where(onehot, pad_start, 0), axis=-1) + rank
    n_blocks = -(-(t * TOP_K + N_EXPERTS * (MOE_BLOCK - 1)) // MOE_BLOCK)
    slots = n_blocks * MOE_BLOCK
    blk_start = jnp.arange(n_blocks, dtype=jnp.int32) * MOE_BLOCK
    blk_e = jnp.minimum(jnp.sum(pad_end[None, :] <= blk_start[:, None], axis=1), N_EXPERTS - 1).astype(jnp.int32)
    used_end = (pad_start + counts)[blk_e]
    blk_valid = jnp.clip(used_end - blk_start, 0, MOE_BLOCK).astype(jnp.int32)
    n_used = pad_end[-1] // MOE_BLOCK
    blk_src = jnp.minimum(jnp.arange(n_blocks, dtype=jnp.int32), n_used - 1)
    blk_e = blk_e[blk_src]
    ids = jnp.arange(N_EXPERTS, dtype=jnp.int32)
    used = counts > 0
    later_used = jnp.where(used[None, :] & (ids[None, :] > ids[:, None]), ids[None, :], N_EXPERTS).min(axis=1)
    next_used = jnp.where(later_used == N_EXPERTS, -1, later_used).astype(jnp.int32)
    blk_next = next_used[blk_e]
    blk_slot = ((jnp.cumsum(used) - 1) % 2).astype(jnp.int32)[blk_e]

    xs = _dispatch_rows(hp_all, dest, slots)
    if p_next is not None:
        xs, p_next['prep'] = lax.optimization_barrier((xs, p_next['prep']))
    ys = _experts(xs, blk_e, blk_valid, blk_src, blk_next, blk_slot, p['layer'], p['w_exp_gate'], p['w_exp_up'],
                  p['w_exp_down'])
    gw_t = gw.T
    ws_gate, ws_up, ws_down = prep['ws_gate'], prep['ws_up'], prep['ws_down']

    def gathered(t0, n):
        return _gather_rows(ys, dest, t0, n)

    pieces = next(n for n in (4, 2, 1) if b % n == 0)
    nb = b // pieces
    for q in range(pieces):
        x = _combine(gathered(q * nb * s, nb * s), gw_t, 0, h2, x, gt2, q * nb, nb, ws_gate, ws_up, ws_down,
                     g_final, final, tm=min(COMBINE_TM, s))
        if after_piece is not None:
            after_piece(x, q * nb, nb)
    if ctx_out:
        ctx_s = _combine(gathered(n_lat, b * sc), gw_t, n_lat, h2_c, ctx_s, cgt2, 0, b, ws_gate, ws_up, ws_down,
                         g_final, False, tm=min(COMBINE_TM_CTX, sc))
    return x, ctx_s


def kernel(x, c, ctx, c_ctx, w_mod, b_mod, g_norm1, g_norm2, w_in, na_rpb, w_branch_na, conv_w, w_branch_sc,
           gla_gate_w, gla_gate_b, gla_norm_g, w_branch_gla, w_out, w_router, b_router, w_exp_gate, w_exp_up,
           w_exp_down, w_sh_gate, w_sh_up, w_sh_down, g_final):
    stacked = dict(g_norm1=g_norm1, g_norm2=g_norm2, na_rpb=na_rpb, w_branch_na=w_branch_na,
                   conv_w=conv_w, w_branch_sc=w_branch_sc, gla_gate_w=gla_gate_w, gla_gate_b=gla_gate_b,
                   gla_norm_g=gla_norm_g, w_branch_gla=w_branch_gla, w_out=w_out, w_router=w_router,
                   b_router=b_router,
                   w_sh_gate=w_sh_gate, w_sh_up=w_sh_up, w_sh_down=w_sh_down)
    depth = w_in.shape[0]
    rows = x.shape[1] // GRID_W
    layers = []
    for i in range(depth):
        p = {name: arr[i] for name, arr in stacked.items()}
        p.update(layer=i, w_exp_gate=w_exp_gate, w_exp_up=w_exp_up, w_exp_down=w_exp_down)
        mods, mods_ctx = _mod_vectors(c, c_ctx, w_mod, b_mod, i)
        p['prep'] = dict(
            w_in=_prep_w_in(w_in, i), mods=mods, mods_ctx=mods_ctx,
            na_bias=_na_bias_table(p['na_rpb'], rows, min(NA_WIN_R, rows)),
            gate_ws=_gla_gate_weights(p['gla_gate_w'], p['gla_gate_b']),
            w_na=p['w_branch_na'].astype(BF16), w_sc=p['w_branch_sc'].astype(BF16),
            w_gla=p['w_branch_gla'].astype(BF16), w_out=p['w_out'].astype(BF16),
            w_router=jnp.pad(p['w_router'], ((0, 0), (0, LANE - N_EXPERTS))).astype(BF16),
            ws_gate=p['w_sh_gate'].astype(BF16), ws_up=p['w_sh_up'].astype(BF16),
            ws_down=p['w_sh_down'].astype(BF16))
        layers.append(p)

    ctx_s = ctx
    projected = None
    for i, p in enumerate(layers):
        last = i == depth - 1
        p_next = None if last else layers[i + 1]
        after_piece = None
        next_projected = []
        if not last:
            def after_piece(xq, b0, nb, p_next=p_next, acc=next_projected):
                acc.append(_project_latent(xq, p_next, b0, nb, acc[-1] if acc else None))

        x, ctx_s = _layer(x, ctx_s, p, not last, last, g_final, projected, after_piece, p_next)
        projected = next_projected[-1] if next_projected else None
    return x
```

```python
import functools

import numpy as np
import jax
import jax.numpy as jnp
from jax import lax
from jax.experimental import pallas as pl
from jax.experimental.pallas import tpu as pltpu
from jax.experimental.pallas import tpu_sc as plsc

F32 = jnp.float32
BF16 = jnp.bfloat16
U32 = jnp.uint32

D_MODEL = 1024
N_MOD = 6
RMS_EPS = 1e-6
NEG_INF = -1e30
GRID_W = 64
NA_HEADS = 8
NA_HEAD_DIM = 64
NA_WIDTH = NA_HEADS * NA_HEAD_DIM
NA_WIN_R = 8
NA_WIN_C = 16
NA_GROUP = 4
SC_WIDTH = 512
GLA_HEADS = 4
GLA_KEY_WIDTH = 512
GLA_VAL_WIDTH = 1024
GLA_DK = GLA_KEY_WIDTH // GLA_HEADS
GLA_DV = GLA_VAL_WIDTH // GLA_HEADS
GLA_GATE_RANK = 16
GLA_GATE_TAU = 16.0
LOG2_E = 1.4426950408889634
N_EXPERTS = 64
N_EXPERT_GROUPS = 8
GROUP_SIZE = N_EXPERTS // N_EXPERT_GROUPS
TOPK_GROUPS = 4
TOP_K = 8
EXPERT_FF = 256
ROUTED_SCALE = 2.5
X_RING = 3
MOE_BLOCK = 1024

LANE = 128
GLA_C = 128
GLA_STEP_CHUNKS = 4
GLA_LEVELS = tuple(GLA_C >> (i + 1) for i in range(GLA_C.bit_length() - 1))
VMEM_LIMIT = 48 * 1024 * 1024
SC_WINDOW = 128
SC_ROW = 256
SC_PARTS = D_MODEL // 2 // SC_ROW

OFF_V_GLA = 0
OFF_K_NA = 1024
OFF_V_NA = 1536
OFF_K_GLA = 2048
N_KV_MAIN = 2560
OFF_Q_NA = 2560
OFF_B_SC = 3072
OFF_C_SC = 3584
OFF_X_SC = 4096
OFF_Q_GLA = 4608
OFF_R_GLA = 5120
OFF_MERGE = 6144
N_MAIN = 9216


def _cparams(sem, vmem=VMEM_LIMIT):
    return pltpu.CompilerParams(dimension_semantics=sem, vmem_limit_bytes=vmem)


def _dot(a, b):
    return jnp.dot(a, b, preferred_element_type=F32)


def _dot_nt(a, b):
    return lax.dot_general(a, b, (((1,), (1,)), ((), ())), preferred_element_type=F32)


def _dot_tn(a, b):
    return lax.dot_general(a, b, (((0,), (0,)), ((), ())), preferred_element_type=F32)


def _sigmoid(x):
    return 0.5 * jnp.tanh(0.5 * x) + 0.5


def _pack_rows(x):
    n = x.shape[1] // 2
    r = x.astype(BF16).astype(F32)
    lo = pltpu.bitcast(r[:, :n], U32) >> 16
    hi = pltpu.bitcast(r[:, n:], U32)
    return hi | lo


def _store_parts(ref, words):
    for part in range(SC_PARTS):
        dst = ref.at[part, 0] if len(ref.shape) == 4 else ref.at[part]
        dst[...] = words[:, part * SC_ROW:(part + 1) * SC_ROW]


def _load_parts(ref, *lead):
    return jnp.concatenate([ref[(part,) + lead] for part in range(SC_PARTS)], axis=-1)


def _unpack_rows(w):
    lo = pltpu.bitcast(w << 16, F32)
    hi = pltpu.bitcast(w & jnp.uint32(0xFFFF0000), F32)
    return lo, hi


def _mod_kernel(a_ref, w_ref, b_ref, o_ref):
    a = a_ref[...]
    a = a * _sigmoid(a)
    o_ref[...] = _dot(a.astype(BF16), w_ref[0].astype(BF16)) + b_ref[0]


def _mod_vectors(c, c_ctx, w_mod, b_mod, layer):
    b = c.shape[0]
    rows = -(-(b + 1) // 8) * 8
    a = jnp.concatenate([c, c_ctx[None], jnp.zeros((rows - b - 1, D_MODEL), F32)], axis=0)
    n = N_MOD * D_MODEL
    tn = 1536
    out = pl.pallas_call(
        _mod_kernel,
        grid=(n // tn,),
        in_specs=[pl.BlockSpec((rows, D_MODEL), lambda j: (0, 0)),
                  pl.BlockSpec((1, D_MODEL, tn), lambda j: (layer, 0, j)),
                  pl.BlockSpec((1, 1, tn), lambda j: (layer, 0, j))],
        out_specs=pl.BlockSpec((rows, tn), lambda j: (0, j)),
        out_shape=jax.ShapeDtypeStruct((rows, n), F32),
        compiler_params=_cparams(("parallel",)),
        name="mod_vectors",
    )(a, w_mod, b_mod[:, None])
    lat = out[:b].reshape(b, N_MOD, 1, D_MODEL)
    ctx = out[b].reshape(N_MOD, 1, 1, D_MODEL)
    return [lat[:, i] for i in range(N_MOD)], [ctx[i] for i in range(N_MOD)]


def _proj_kernel(x_ref, g_ref, sh_ref, sc_ref, w_ref, wg_ref, *rest):
    o_ref, og_ref, h_ref = rest[-3:]

    @pl.when(pl.program_id(2) == 0)
    def _():
        x = x_ref[0]
        ms = jnp.mean(x * x, axis=-1, keepdims=True)
        h = x * lax.rsqrt(ms + RMS_EPS) * g_ref[...] * (1.0 + sc_ref[0]) + sh_ref[0]
        hb = h.astype(BF16)
        h_ref[...] = hb
        og_ref[0] = _dot(hb, wg_ref[...])

    o_ref[0] = _dot(h_ref[...], w_ref[...]).astype(o_ref.dtype)


W_IN_TILE = 512
W_IN_GATE_SHIFT = 2 * GLA_GATE_RANK


def _prep_w_in_kernel(a_ref, b_ref, o_ref, g_ref):
    t = pl.program_id(0)
    first_lat = N_KV_MAIN // W_IN_TILE
    a = a_ref[0]

    @pl.when(t < first_lat)
    def _():
        o_ref[...] = a.T.astype(BF16)

    @pl.when(t >= first_lat)
    def _():
        moved = jnp.concatenate([a[W_IN_GATE_SHIFT:], b_ref[0]], axis=0)
        scale = jnp.where(t == first_lat, NA_HEAD_DIM ** -0.5, 1.0)
        o_ref[...] = (moved * scale).T.astype(BF16)

    @pl.when(t == first_lat)
    def _():
        head = a[:LANE]
        row = lax.broadcasted_iota(jnp.int32, head.shape, 0)
        g_ref[...] = jnp.where(row < W_IN_GATE_SHIFT, head, 0.0).T.astype(BF16)


def _prep_w_in(w_in, layer):
    d = w_in.shape[1]
    w_t = jnp.swapaxes(w_in, 1, 2)
    first_lat = N_KV_MAIN // W_IN_TILE
    kv_perm = OFF_K_NA // W_IN_TILE

    def a_map(t):
        return (layer, jnp.where(t < first_lat, (t + first_lat - kv_perm) % first_lat, t), 0)

    def b_map(t):
        return (layer, jnp.where(t < first_lat, 0, (t + 1) * (W_IN_TILE // W_IN_GATE_SHIFT)), 0)

    return pl.pallas_call(
        _prep_w_in_kernel,
        grid=(N_MAIN // W_IN_TILE,),
        in_specs=[pl.BlockSpec((1, W_IN_TILE, d), a_map),
                  pl.BlockSpec((1, W_IN_GATE_SHIFT, d), b_map)],
        out_specs=[pl.BlockSpec((d, W_IN_TILE), lambda t: (0, t)),
                   pl.BlockSpec((d, LANE), lambda t: (0, 0))],
        out_shape=[jax.ShapeDtypeStruct((d, N_MAIN), BF16), jax.ShapeDtypeStruct((d, LANE), BF16)],
        compiler_params=_cparams(("arbitrary",)),
        name="prep_w_in",
    )(w_t, w_t)


def _proj_in(x, g, shift, scale, w_main, w_gate, tm, tn, n=None, b0=0, nb=None, prev=None):
    b, s, d = x.shape
    n = w_main.shape[1] if n is None else n
    nb = b if nb is None else nb
    per_batch = shift.shape[0] == b
    mod_map = (lambda bi, i, j: (b0 + bi, 0, 0)) if per_batch else (lambda bi, i, j: (0, 0, 0))
    operands = (x, g[None], shift, scale, w_main, w_gate)
    extend = () if prev is None else tuple(prev)
    return pl.pallas_call(
        _proj_kernel,
        grid=(nb, s // tm, n // tn),
        in_specs=[pl.BlockSpec((1, tm, d), lambda bi, i, j: (b0 + bi, i, 0)),
                  pl.BlockSpec((1, d), lambda bi, i, j: (0, 0)),
                  pl.BlockSpec((1, 1, d), mod_map),
                  pl.BlockSpec((1, 1, d), mod_map),
                  pl.BlockSpec((d, tn), lambda bi, i, j: (0, j)),
                  pl.BlockSpec((d, LANE), lambda bi, i, j: (0, 0))]
                 + [pl.BlockSpec(memory_space=pl.ANY)] * len(extend),
        out_specs=[pl.BlockSpec((1, tm, tn), lambda bi, i, j: (b0 + bi, i, j)),
                   pl.BlockSpec((1, tm, LANE), lambda bi, i, j: (b0 + bi, i, 0))],
        out_shape=[jax.ShapeDtypeStruct((b, s, n), BF16),
                   jax.ShapeDtypeStruct((b, s, LANE), F32)],
        scratch_shapes=[pltpu.VMEM((tm, d), BF16)],
        input_output_aliases={len(operands): 0, len(operands) + 1: 1} if extend else {},
        compiler_params=_cparams(("parallel", "parallel", "arbitrary")),
        name="proj_in",
    )(*operands, *extend)


def _softmax_av(q, keys, vals, biases):
    scores = []
    for kk, bb in zip(keys, biases):
        s = _dot_nt(q, kk)
        scores.append(s if bb is None else s + bb)
    m = scores[0].max(axis=-1, keepdims=True)
    for s in scores[1:]:
        m = jnp.maximum(m, s.max(axis=-1, keepdims=True))
    num = None
    den = None
    for s, vv in zip(scores, vals):
        e = jnp.exp(s - m)
        dsum = e.sum(axis=-1, keepdims=True)
        o = _dot(e.astype(BF16), vv)
        num = o if num is None else num + o
        den = dsum if den is None else den + dsum
    return num / den


def _na_kernel(q_ref, k_ref, v_ref, kc_ref, vc_ref, *rest, rows, kr):
    *bias_refs, o_ref = rest
    kc = kc_ref[0]
    vc = vc_ref[0]
    for j, bias_ref in enumerate(bias_refs):
        r = pl.program_id(1) * len(bias_refs) + j
        row_start = jnp.clip(r - kr // 2, 0, rows - kr)
        start = pl.multiple_of(row_start * GRID_W, GRID_W)
        n_win = kr * GRID_W
        q = q_ref[0, j * GRID_W:(j + 1) * GRID_W, :]
        kw = k_ref[0, pl.ds(start, n_win), :]
        vw = v_ref[0, pl.ds(start, n_win), :]
        o_ref[0, j * GRID_W:(j + 1) * GRID_W, :] = _na_row(q, kw, vw, kc, vc, bias_ref).astype(o_ref.dtype)


def _na_row(q, kw, vw, kc, vc, bias_ref):
    gw = NA_GROUP * NA_HEAD_DIM
    stacked = (NA_GROUP * GRID_W, gw)
    on_head = (lax.broadcasted_iota(jnp.int32, stacked, 0) // GRID_W
               == lax.broadcasted_iota(jnp.int32, stacked, 1) // NA_HEAD_DIM)
    outs = []
    for g in range(NA_HEADS // NA_GROUP):
        sl = slice(g * gw, (g + 1) * gw)
        q_all = jnp.where(on_head, jnp.concatenate([q[:, sl]] * NA_GROUP, axis=0), jnp.zeros((), q.dtype))
        bias = bias_ref[0, g * NA_GROUP * GRID_W:(g + 1) * NA_GROUP * GRID_W, :]
        o_all = _softmax_av(q_all, [kw[:, sl], kc[:, sl]], [vw[:, sl], vc[:, sl]], [bias, None])
        o_all = jnp.where(on_head, o_all, 0.0).reshape(NA_GROUP, GRID_W, gw)
        outs.append(o_all.sum(axis=0))
    return jnp.concatenate(outs, axis=-1)


def _na_bias_table(rpb, rows, kr):
    col = np.arange(GRID_W)
    col_start = np.clip(col - NA_WIN_C // 2, 0, GRID_W - NA_WIN_C)
    col_ok = (col[None, :] >= col_start[:, None]) & (col[None, :] < col_start[:, None] + NA_WIN_C)
    d_col = np.clip(col[None, :] - col[:, None], -(NA_WIN_C - 1), NA_WIN_C - 1) + NA_WIN_C - 1
    n_dr, n_dc = rpb.shape[1], rpb.shape[2]
    onehot = jnp.asarray((d_col.reshape(-1)[None, :] == np.arange(n_dc)[:, None]).astype(np.float32))
    by_col = jnp.dot(rpb.astype(F32).reshape(NA_HEADS * n_dr, n_dc), onehot, precision=lax.Precision.HIGHEST)
    by_col = by_col.reshape(NA_HEADS, n_dr, GRID_W, GRID_W)
    by_col = jnp.where(col_ok[None, None], by_col, NEG_INF)
    tables = []
    for o in range(kr):
        lo = NA_WIN_R - 1 - o
        tables.append(by_col[:, lo:lo + kr].transpose(0, 2, 1, 3).reshape(NA_HEADS, GRID_W, kr * GRID_W))
    return jnp.stack(tables).reshape(kr, NA_HEADS * GRID_W, kr * GRID_W)


def _na_latent(main, main_ctx, bias):
    b, s, _ = main.shape
    sc = main_ctx.shape[1]
    rows = s // GRID_W
    kr = min(NA_WIN_R, rows)
    w = NA_WIDTH

    per_step = next(n for n in (8, 4, 2, 1) if rows % n == 0)

    def bias_spec(j):
        def bias_map(bi, i):
            r = i * per_step + j
            return (r - jnp.clip(r - kr // 2, 0, rows - kr), 0, 0)
        return pl.BlockSpec((1, NA_HEADS * GRID_W, kr * GRID_W), bias_map)

    return pl.pallas_call(
        functools.partial(_na_kernel, rows=rows, kr=kr),
        grid=(b, rows // per_step),
        in_specs=[pl.BlockSpec((1, per_step * GRID_W, w), lambda bi, i: (bi, i, OFF_Q_NA // w)),
                  pl.BlockSpec((1, s, w), lambda bi, i: (bi, 0, OFF_K_NA // w)),
                  pl.BlockSpec((1, s, w), lambda bi, i: (bi, 0, OFF_V_NA // w)),
                  pl.BlockSpec((1, sc, w), lambda bi, i: (bi, 0, OFF_K_NA // w)),
                  pl.BlockSpec((1, sc, w), lambda bi, i: (bi, 0, OFF_V_NA // w))]
                 + [bias_spec(j) for j in range(per_step)],
        out_specs=pl.BlockSpec((1, per_step * GRID_W, w), lambda bi, i: (bi, i, 0)),
        out_shape=jax.ShapeDtypeStruct((b, s, w), BF16),
        compiler_params=_cparams(("parallel", "arbitrary")),
        name="na_latent",
    )(main, main, main, main_ctx, main_ctx, *([bias] * per_step))


def _dense_attn_kernel(q_ref, k_ref, v_ref, o_ref):
    q = q_ref[0]
    k = k_ref[0]
    v = v_ref[0]
    outs = []
    for h in range(NA_HEADS):
        sl = slice(h * NA_HEAD_DIM, (h + 1) * NA_HEAD_DIM)
        outs.append(_softmax_av(q[:, sl], [k[:, sl]], [v[:, sl]], [None]))
    o_ref[0] = jnp.concatenate(outs, axis=-1).astype(o_ref.dtype)


def _dense_attn(main_ctx):
    b, sc, _ = main_ctx.shape
    w = NA_WIDTH
    return pl.pallas_call(
        _dense_attn_kernel,
        grid=(b,),
        in_specs=[pl.BlockSpec((1, sc, w), lambda bi: (bi, 0, OFF_Q_NA // w)),
                  pl.BlockSpec((1, sc, w), lambda bi: (bi, 0, OFF_K_NA // w)),
                  pl.BlockSpec((1, sc, w), lambda bi: (bi, 0, OFF_V_NA // w))],
        out_specs=pl.BlockSpec((1, sc, w), lambda bi: (bi, 0, 0)),
        out_shape=jax.ShapeDtypeStruct((b, sc, w), BF16),
        compiler_params=_cparams(("parallel",)),
        name="ctx_attn",
    )(main_ctx, main_ctx, main_ctx)


HALO = 16


def _gla_matrices(reverse):
    c = GLA_C
    t = np.arange(c)[:, None]
    m = np.arange(c)[None, :]
    blocks = [m <= t, m > t]
    for b in GLA_LEVELS:
        first = (t // (2 * b)) * (2 * b) + b
        is_q = (t & b) != 0
        blocks.append(np.where(is_q, (m > first) & (m <= t), (m > t) & (m <= first)))
    mats = np.stack(blocks).astype(np.float32)
    if reverse:
        mats = mats[:, ::-1, ::-1]
    mats = mats.reshape(-1, c)
    return jnp.asarray(np.concatenate([mats, mats], axis=1), dtype=BF16)


def _split_bf16(x):
    hi = x.astype(BF16)
    return hi, (x - hi.astype(F32)).astype(BF16)


def _pair_block_diag(x):
    lane = lax.broadcasted_iota(jnp.int32, x.shape, 1)
    zero = jnp.zeros((), x.dtype)
    return jnp.concatenate([jnp.where(lane < GLA_DK, x, zero), jnp.where(lane >= GLA_DK, x, zero)], axis=0)


def _gla_kernel(*refs, emit):
    n_in = 9 if emit else 8
    n_out = 2 if emit else 1
    ins, outs, scratch = refs[:2 * n_in], refs[2 * n_in:2 * (n_in + n_out)], refs[2 * (n_in + n_out):]
    ins = [ins[d * n_in:(d + 1) * n_in] for d in range(2)]
    outs = [outs[d * n_out:(d + 1) * n_out] for d in range(2)]
    step = pl.program_id(1)

    @pl.when(step == 0)
    def _():
        for d in range(2):
            scratch[d][...] = ins[d][-1][0]

    pending = [_gla_direction(ins[d][:-1], outs[d][:-1], scratch[d], reverse, emit)
               for d, reverse in enumerate((False, True))]
    while pending:
        pending = [stages for stages in pending if next(stages, "done") != "done"]

    @pl.when(step == pl.num_programs(1) - 1)
    def _():
        for d in range(2):
            outs[d][-1][0] = scratch[d][...]


def _gla_direction(ins, outs, st_ref, reverse, emit):
    gate_ref = ins[-5]
    order = range(gate_ref.shape[1] // GLA_C)
    for sub in (reversed(order) if reverse else order):
        yield from _gla_chunk(ins, outs, st_ref, reverse, emit, slice(sub * GLA_C, (sub + 1) * GLA_C))


def _gla_chunk(ins, outs, st_ref, reverse, emit, rows):
    if emit:
        q_ref, k_ref, v_ref, gt_ref, w2a_ref, w2b_ref, b2_ref, a_ref = ins
        (o_ref,) = outs
    else:
        k_ref, v_ref, gt_ref, w2a_ref, w2b_ref, b2_ref, a_ref = ins
    c = GLA_C
    pw = 2 * GLA_DK

    lr_hi, lr_lo = _split_bf16(gt_ref[0, rows])
    logit = (_dot(jnp.concatenate([lr_hi, lr_lo], axis=1), w2a_ref[...]) + _dot(lr_hi, w2b_ref[...])
             + b2_ref[...])
    g = (jnp.minimum(logit, 0.0) - jnp.log1p(jnp.exp(-jnp.abs(logit)))) * (LOG2_E / GLA_GATE_TAU)
    g_hi, g_lo = _split_bf16(g)
    args = _dot(a_ref[...], jnp.concatenate([g_hi, g_lo], axis=0))
    cum = args[0:c]
    rem = args[c:2 * c]
    last_row = cum[0:1] if reverse else cum[c - 1:c]
    yield

    k = k_ref[0, rows].astype(F32)
    v = v_ref[0, rows]
    atts = []
    if emit:
        q = q_ref[0, rows].astype(F32) * (GLA_DK ** -0.5)
        row_t = lax.broadcasted_iota(jnp.int32, (c, pw), 0)
        si = lax.broadcasted_iota(jnp.int32, (2 * c, c), 0) & (c - 1)
        ti = lax.broadcasted_iota(jnp.int32, (2 * c, c), 1)
        if reverse:
            row_t, ti, si = c - 1 - row_t, c - 1 - ti, c - 1 - si
        for hp in range(GLA_HEADS // 2):
            cs = slice(hp * pw, (hp + 1) * pw)
            qp, kp = q[:, cs], k[:, cs]
            att = jnp.where(ti == si, _dot_nt(_pair_block_diag(kp.astype(BF16)), qp.astype(BF16)), 0.0)
            for l, b in enumerate(GLA_LEVELS):
                x = (jnp.exp2(args[(2 + l) * c:(3 + l) * c, cs])
                     * jnp.where((row_t & b) != 0, qp, kp)).astype(BF16)
                pair = (((ti ^ si) >> (b.bit_length() - 1)) == 1) & ((ti & b) != 0)
                att = jnp.where(pair, _dot_nt(_pair_block_diag(x), x), att)
                yield
            atts.append(att.astype(BF16))

    outs = []
    for h in range(GLA_HEADS):
        sl = slice(h * GLA_DK, (h + 1) * GLA_DK)
        kh = k[:, sl]
        vh = v[:, h * GLA_DV:(h + 1) * GLA_DV]
        state = st_ref[h]
        kd = (kh * jnp.exp2(rem[:, sl])).astype(BF16)
        decay = jnp.exp2(jnp.broadcast_to(last_row[:, sl], (GLA_DK, GLA_DK))).T
        decay = jnp.concatenate([decay] * (GLA_DV // GLA_DK), axis=1)
        if emit:
            qd = (q[:, sl] * jnp.exp2(cum[:, sl])).astype(BF16)
            att_t = atts[h // 2][(h % 2) * c:(h % 2 + 1) * c]
            both = _dot_tn(jnp.concatenate([att_t, kd], axis=1), vh)
            outs.append(_dot(qd, state.astype(BF16)) + both[:c])
            st_ref[h] = decay * state + both[c:]
        else:
            st_ref[h] = decay * state + _dot_tn(kd, vh)
        yield

    if emit:
        o_ref[0, rows] = jnp.concatenate(outs, axis=-1).astype(o_ref.dtype)


def _gla_scan(main, gate, gate_ws, s0s, emit):
    b, l, _ = main.shape
    step_rows = GLA_C * min(GLA_STEP_CHUNKS, l // GLA_C)
    n = l // step_rows
    const = lambda arr: pl.BlockSpec(arr.shape, lambda bi, s: (0,) * arr.ndim)
    state_spec = pl.BlockSpec((1, GLA_HEADS, GLA_DK, GLA_DV), lambda bi, s: (bi, 0, 0, 0))
    state_shape = jax.ShapeDtypeStruct((b, GLA_HEADS, GLA_DK, GLA_DV), F32)
    in_specs, args, out_specs, out_shape = [], [], [], []
    for reverse in (False, True):
        amat = _gla_matrices(reverse)
        w2a, w2b, b2 = gate_ws[reverse]

        def col(block, reverse=reverse):
            return lambda bi, s: (bi, n - 1 - s if reverse else s, block)

        if emit:
            in_specs.append(pl.BlockSpec((1, step_rows, GLA_KEY_WIDTH), col(OFF_Q_GLA // GLA_KEY_WIDTH)))
            args.append(main)
            out_specs.append(pl.BlockSpec((1, step_rows, GLA_VAL_WIDTH), col(0)))
            out_shape.append(jax.ShapeDtypeStruct((b, l, GLA_VAL_WIDTH), BF16))
        in_specs += [pl.BlockSpec((1, step_rows, GLA_KEY_WIDTH), col(OFF_K_GLA // GLA_KEY_WIDTH)),
                     pl.BlockSpec((1, step_rows, GLA_VAL_WIDTH), col(OFF_V_GLA // GLA_VAL_WIDTH)),
                     pl.BlockSpec((1, step_rows, LANE), col(0)),
                     const(w2a), const(w2b), const(b2), const(amat), state_spec]
        args += [main, main, gate, w2a, w2b, b2, amat, s0s[reverse]]
        out_specs.append(state_spec)
        out_shape.append(state_shape)
    res = pl.pallas_call(
        functools.partial(_gla_kernel, emit=emit),
        grid=(b, n),
        in_specs=in_specs,
        out_specs=out_specs,
        out_shape=out_shape,
        scratch_shapes=[pltpu.VMEM((GLA_HEADS, GLA_DK, GLA_DV), F32)] * 2,
        compiler_params=_cparams(("parallel", "arbitrary")),
        name="gla_scan",
    )(*args)
    return (res[0], res[2], res[1], res[3]) if emit else (None, None, res[0], res[1])


def _gla_gate_weights(gate_w, gate_b):
    out = []
    for dr in range(2):
        w = jnp.zeros((LANE, GLA_KEY_WIDTH), F32)
        w = w.at[dr * GLA_GATE_RANK:(dr + 1) * GLA_GATE_RANK].set(gate_w[dr])
        w_hi = w.astype(BF16)
        w_lo = (w - w_hi.astype(F32)).astype(BF16)
        out.append((jnp.concatenate([w_hi, w_hi], axis=0), w_lo, gate_b[dr][None]))
    return out


def _conv_tile(b_ref, c_ref, x_ref, cp_ref, xp_ref, cn_ref, xn_ref, w_ref):
    i, n = pl.program_id(1), pl.num_programs(1)
    u = c_ref[0].astype(F32) * x_ref[0].astype(F32)
    tm = u.shape[0]
    before = jnp.where(i > 0, 1.0, 0.0) * (cp_ref[0, HALO - 1:HALO].astype(F32) * xp_ref[0, HALO - 1:HALO].astype(F32))
    after = jnp.where(i < n - 1, 1.0, 0.0) * (cn_ref[0, 0:1].astype(F32) * xn_ref[0, 0:1].astype(F32))
    t = lax.broadcasted_iota(jnp.int32, u.shape, 0)
    prev = jnp.where(t == 0, before, pltpu.roll(u, 1, axis=0))
    nxt = jnp.where(t == tm - 1, after, pltpu.roll(u, tm - 1, axis=0))
    w = w_ref[...]
    return b_ref[0].astype(F32) * (prev * w[0:1] + u * w[1:2] + nxt * w[2:3])


def _merge_kernel(ona_ref, bsc_ref, csc_ref, xsc_ref, cp_ref, xp_ref, cn_ref, xn_ref, cw_ref, of_ref, ob_ref,
                  r_ref, gna_ref, gsc_ref, ggl_ref, x_ref, gt_ref,
                  gn_ref, wna_ref, wsc_ref, wgl_ref, wo_ref, g2_ref, sh2_ref, sc2_ref, wr_ref, *rest):
    xo_ref, h2_ref, hp_ref, lg_ref = rest[-4:]
    o_sc = _conv_tile(bsc_ref, csc_ref, xsc_ref, cp_ref, xp_ref, cn_ref, xn_ref, cw_ref).astype(BF16)
    o = of_ref[0].astype(F32) + ob_ref[0].astype(F32)
    normed = []
    for h in range(GLA_HEADS):
        oh = o[:, h * GLA_DV:(h + 1) * GLA_DV]
        ms = jnp.mean(oh * oh, axis=-1, keepdims=True)
        normed.append(oh * lax.rsqrt(ms + RMS_EPS))
    r = r_ref[0].astype(F32)
    y_gla = jnp.concatenate(normed, axis=-1) * gn_ref[...] * (r * _sigmoid(r))
    y = (_sigmoid(gna_ref[0].astype(F32)) * _dot(ona_ref[0], wna_ref[...])
         + _sigmoid(gsc_ref[0].astype(F32)) * _dot(o_sc, wsc_ref[...])
         + _sigmoid(ggl_ref[0].astype(F32)) * _dot(y_gla.astype(BF16), wgl_ref[...]))
    xn = x_ref[0] + gt_ref[0] * _dot(y.astype(BF16), wo_ref[...])
    xo_ref[0] = xn
    ms = jnp.mean(xn * xn, axis=-1, keepdims=True)
    h2 = xn * lax.rsqrt(ms + RMS_EPS) * g2_ref[...] * (1.0 + sc2_ref[0]) + sh2_ref[0]
    h2b = h2.astype(BF16)
    h2_ref[0] = h2b
    _store_parts(hp_ref, _pack_rows(h2))
    lg_ref[...] = _dot(h2b, wr_ref[...])


def _merge(o_na, conv_w, o_f, o_b, main, x, gt1, gn, w_na, w_sc, w_gla, w_out, g2, sh2, sc2, w_router, tm,
           n_routed, tok_off, routed=None):
    b, s, d = x.shape
    per_batch = gt1.shape[0] == b
    mod_map = (lambda bi, i: (bi, 0, 0)) if per_batch else (lambda bi, i: (0, 0, 0))
    tok = lambda width, blk: pl.BlockSpec((1, tm, width), lambda bi, i: (bi, i, blk))
    full = lambda arr: pl.BlockSpec(arr.shape, lambda bi, i: (0,) * arr.ndim)
    mod = pl.BlockSpec((1, 1, d), mod_map)
    gn_t = jnp.tile(gn, GLA_HEADS)[None]
    g2_t = g2[None]
    per_tile = tm // HALO
    last_halo = s // HALO - 1
    halo_prev = lambda blk: pl.BlockSpec(
        (1, HALO, SC_WIDTH), lambda bi, i: (bi, jnp.maximum(i * per_tile - 1, 0), blk))
    halo_next = lambda blk: pl.BlockSpec(
        (1, HALO, SC_WIDTH), lambda bi, i: (bi, jnp.minimum((i + 1) * per_tile, last_halo), blk))
    col_b, col_c, col_x = OFF_B_SC // SC_WIDTH, OFF_C_SC // SC_WIDTH, OFF_X_SC // SC_WIDTH
    extend = () if routed is None else tuple(routed)
    operands = (o_na, main, main, main, main, main, main, main, conv_w, o_f, o_b, main, main, main, main, x, gt1,
                gn_t, w_na, w_sc, w_gla, w_out, g2_t, sh2, sc2, w_router)
    n_in = len(operands)
    nblk = s // tm
    blk0 = tok_off // tm
    return pl.pallas_call(
        _merge_kernel,
        grid=(b, s // tm),
        in_specs=[tok(NA_WIDTH, 0), tok(SC_WIDTH, col_b), tok(SC_WIDTH, col_c), tok(SC_WIDTH, col_x),
                  halo_prev(col_c), halo_prev(col_x), halo_next(col_c), halo_next(col_x), full(conv_w),
                  tok(GLA_VAL_WIDTH, 0), tok(GLA_VAL_WIDTH, 0),
                  tok(d, OFF_R_GLA // d), tok(d, OFF_MERGE // d), tok(d, OFF_MERGE // d + 1),
                  tok(d, OFF_MERGE // d + 2), tok(d, 0), mod,
                  full(gn_t), full(w_na), full(w_sc), full(w_gla), full(w_out), full(g2_t), mod, mod,
                  full(w_router)] + [pl.BlockSpec(memory_space=pl.ANY)] * len(extend),
        out_specs=[tok(d, 0), tok(d, 0),
                   pl.BlockSpec((SC_PARTS, tm, SC_ROW), lambda bi, i: (0, blk0 + bi * nblk + i, 0)),
                   pl.BlockSpec((tm, LANE), lambda bi, i: (blk0 + bi * nblk + i, 0))],
        out_shape=[jax.ShapeDtypeStruct((b, s, d), F32),
                   jax.ShapeDtypeStruct((b, s, d), BF16),
                   jax.ShapeDtypeStruct((SC_PARTS, n_routed, SC_ROW), U32),
                   jax.ShapeDtypeStruct((n_routed, LANE), F32)],
        input_output_aliases={n_in: 2, n_in + 1: 3} if extend else {},
        compiler_params=_cparams(("parallel", "parallel")),
        name="merge",
    )(*operands, *extend)


def _router_kernel(lg_ref, br_ref, tri_ref, eidx_ref, gw_ref, rank_ref, cnt_ref, carry_ref):
    tm = lg_ref.shape[0]

    @pl.when(pl.program_id(0) == 0)
    def _():
        carry_ref[...] = jnp.zeros_like(carry_ref)

    scores = _sigmoid(lg_ref[...].T[:N_EXPERTS])
    sel = scores + br_ref[...]
    neg = -jnp.inf

    sel3 = sel.reshape(N_EXPERT_GROUPS, GROUP_SIZE, tm)
    i3 = lax.broadcasted_iota(jnp.int32, sel3.shape, 1)
    m1 = sel3.max(axis=1, keepdims=True)
    first = jnp.where(sel3 == m1, i3, GROUP_SIZE).min(axis=1, keepdims=True)
    m2 = jnp.where(i3 == first, neg, sel3).max(axis=1, keepdims=True)
    gscore = (m1 + m2)[:, 0, :]

    gi = lax.broadcasted_iota(jnp.int32, gscore.shape, 0)
    gmask = jnp.zeros(gscore.shape, jnp.bool_)
    for _ in range(TOPK_GROUPS):
        m = gscore.max(axis=0, keepdims=True)
        pick = gi == jnp.where(gscore == m, gi, N_EXPERT_GROUPS).min(axis=0, keepdims=True)
        gmask = gmask | pick
        gscore = jnp.where(pick, neg, gscore)
    emask = jnp.broadcast_to(gmask[:, None, :], sel3.shape).reshape(N_EXPERTS, tm)
    sel = jnp.where(emask, sel, neg)

    ei = lax.broadcasted_iota(jnp.int32, sel.shape, 0)
    picks, idxs, ws = [], [], []
    for _ in range(TOP_K):
        m = sel.max(axis=0, keepdims=True)
        idx = jnp.where(sel == m, ei, N_EXPERTS).min(axis=0, keepdims=True)
        pick = ei == idx
        picks.append(pick)
        idxs.append(idx)
        ws.append(jnp.where(pick, scores, 0.0).sum(axis=0, keepdims=True))
        sel = jnp.where(pick, neg, sel)
    w = jnp.concatenate(ws, axis=0)
    gw_ref[...] = w / w.sum(axis=0, keepdims=True) * ROUTED_SCALE
    eidx_ref[...] = jnp.concatenate(idxs, axis=0)

    onehot = picks[0]
    for p in picks[1:]:
        onehot = onehot | p
    onehot = jnp.where(onehot, 1.0, 0.0).astype(BF16)
    before = _dot(onehot, tri_ref[...]) + jnp.tile(carry_ref[...], (1, tm // LANE))
    rank_ref[...] = jnp.concatenate(
        [jnp.where(p, before, 0.0).sum(axis=0, keepdims=True) for p in picks], axis=0).astype(jnp.int32)
    carry_ref[...] += _dot(onehot, jnp.ones((tm, LANE), BF16))
    cnt_ref[...] = carry_ref[...]


def _route(logits, b_router):
    t = logits.shape[0]
    tm = next(n for n in (1024, 512) if t % n == 0)
    br = jnp.broadcast_to(b_router.astype(F32)[:, None], (N_EXPERTS, tm))
    tri = jnp.asarray(np.triu(np.ones((tm, tm), np.float32), 1), dtype=BF16)
    kt = lambda dt: jax.ShapeDtypeStruct((TOP_K, t), dt)
    eidx, gw, rank, cnt = pl.pallas_call(
        _router_kernel,
        grid=(t // tm,),
        in_specs=[pl.BlockSpec((tm, LANE), lambda i: (i, 0)),
                  pl.BlockSpec((N_EXPERTS, tm), lambda i: (0, 0)),
                  pl.BlockSpec((tm, tm), lambda i: (0, 0))],
        out_specs=[pl.BlockSpec((TOP_K, tm), lambda i: (0, i)),
                   pl.BlockSpec((TOP_K, tm), lambda i: (0, i)),
                   pl.BlockSpec((TOP_K, tm), lambda i: (0, i)),
                   pl.BlockSpec((N_EXPERTS, LANE), lambda i: (0, 0))],
        out_shape=[kt(jnp.int32), kt(F32), kt(jnp.int32),
                   jax.ShapeDtypeStruct((N_EXPERTS, LANE), F32)],
        scratch_shapes=[pltpu.VMEM((N_EXPERTS, LANE), F32)],
        compiler_params=_cparams(("arbitrary",)),
        name="router",
    )(logits, br, tri)
    return eidx, gw, rank, cnt[:, 0].astype(jnp.int32)


def _sc_mesh():
    return plsc.VectorSubcoreMesh(core_axis_name="core", subcore_axis_name="subcore")


def _dispatch_rows(xp, dest, slots):
    parts, t, _ = xp.shape

    @pl.kernel(out_type=jax.ShapeDtypeStruct((parts, slots, SC_ROW), xp.dtype), mesh=_sc_mesh(),
               scratch_types=[], name="moe_dispatch")
    def run(x_hbm, d_hbm, o_hbm):
        for part in range(parts):
            out_part = o_hbm.at[part]

            def body(x_vmem, d_vmem, out_part=out_part):
                for k in range(TOP_K):
                    pltpu.sync_copy(x_vmem, out_part.at[d_vmem.at[k]])

            pltpu.emit_pipeline(
                body,
                grid=(t // SC_WINDOW,),
                in_specs=[pl.BlockSpec((SC_WINDOW, SC_ROW), lambda i: (i, 0)),
                          pl.BlockSpec((TOP_K, SC_WINDOW), lambda i: (0, i))],
                out_specs=[],
                core_axis_name=("core", "subcore"),
                dimension_semantics=(pltpu.PARALLEL,),
            )(x_hbm.at[part], d_hbm)

    return run(xp, dest)


def _gather_rows(yp, dest, t0, n):
    parts = yp.shape[0]
    nwin = n // SC_WINDOW
    win0 = t0 // SC_WINDOW

    @pl.kernel(out_type=jax.ShapeDtypeStruct((parts, TOP_K * n, SC_ROW), yp.dtype), mesh=_sc_mesh(),
               scratch_types=[], name="moe_gather")
    def run(y_hbm, d_hbm, o_hbm):
        for part in range(parts):
            table = y_hbm.at[part]

            def body(d_vmem, o_vmem, table=table):
                pltpu.sync_copy(table.at[d_vmem.at[0]], o_vmem)

            pltpu.emit_pipeline(
                body,
                grid=(TOP_K, nwin),
                in_specs=[pl.BlockSpec((1, SC_WINDOW), lambda k, j: (k, win0 + j))],
                out_specs=[pl.BlockSpec((SC_WINDOW, SC_ROW), lambda k, j: (k * nwin + j, 0))],
                core_axis_name=("core", "subcore"),
                dimension_semantics=(pltpu.PARALLEL, pltpu.PARALLEL),
            )(d_hbm, o_hbm.at[part])

    return run(yp, dest).reshape(parts, TOP_K, n, SC_ROW)


def _expert_kernel(be_ref, bv_ref, bs_ref, nx_ref, sl_ref, xs_hbm, wg_hbm, wu_hbm, wd_hbm, o_ref,
                   x_buf, x_sems, wg_f, wu_f, wd_f, wg_s, wu_s, wd_s, sems, *, layer):
    i = pl.program_id(0)
    valid = bv_ref[i]
    expert = be_ref[i]
    new_expert = (i == 0) | (expert != be_ref[jnp.maximum(i - 1, 0)])
    slot = sl_ref[i]

    def fetch(which, into):
        return [pltpu.make_async_copy(src.at[layer, which], dst.at[into], sems.at[into, j])
                for j, (src, dst) in enumerate(((wg_hbm, wg_f), (wu_hbm, wu_f), (wd_hbm, wd_f)))]

    @pl.when(i == 0)
    def _():
        for cp in fetch(expert, slot):
            cp.start()

    @pl.when(new_expert)
    def _():
        for cp in fetch(expert, slot):
            cp.wait()
        upcoming = nx_ref[i]

        @pl.when(upcoming >= 0)
        def _():
            for cp in fetch(upcoming, 1 - slot):
                cp.start()

        wg_s[...] = wg_f[slot].astype(BF16)
        wu_s[...] = wu_f[slot].astype(BF16)
        wd_s[...] = wd_f[slot].astype(BF16)

    n_steps = pl.num_programs(0)
    ahead = X_RING - 1

    def x_copy(block, ring_slot):
        rows = pl.ds(pl.multiple_of(block * MOE_BLOCK, MOE_BLOCK), MOE_BLOCK)
        return pltpu.make_async_copy(xs_hbm.at[:, rows, :], x_buf.at[ring_slot], x_sems.at[ring_slot])

    def used(block):
        return (block < n_steps) & (bv_ref[jnp.minimum(block, n_steps - 1)] > 0)

    @pl.when(i == 0)
    def _():
        for j in range(ahead):
            pl.when(used(j))(lambda j=j: x_copy(j, j % X_RING).start())

    @pl.when(valid > 0)
    def _():
        x_copy(i, i % X_RING).wait()
        pl.when(used(i + ahead))(lambda: x_copy(i + ahead, (i + ahead) % X_RING).start())
        w = jnp.concatenate([x_buf[i % X_RING, part] for part in range(SC_PARTS)], axis=-1)
        row = lax.broadcasted_iota(jnp.int32, w.shape, 0)
        w = jnp.where(row < valid, w, jnp.uint32(0))
        lo, hi = _unpack_rows(w)
        x = jnp.concatenate([lo, hi], axis=1).astype(BF16)
        a = _dot(x, wg_s[...])
        hid = a * _sigmoid(a) * _dot(x, wu_s[...])
        _store_parts(o_ref, _pack_rows(_dot(hid.astype(BF16), wd_s[...])))


def _experts(xs, blk_e, blk_valid, blk_src, blk_next, blk_slot, layer, w_gate, w_up, w_down):
    parts, slots, _ = xs.shape
    d = D_MODEL
    nb = slots // MOE_BLOCK
    data = pl.BlockSpec((parts, MOE_BLOCK, SC_ROW), lambda i, be, bv, bs, nx, sl: (0, bs[i], 0))
    stage = lambda shape: pltpu.VMEM((2,) + shape, F32)
    return pl.pallas_call(
        functools.partial(_expert_kernel, layer=layer),
        grid_spec=pltpu.PrefetchScalarGridSpec(
            num_scalar_prefetch=5,
            grid=(nb,),
            in_specs=[pl.BlockSpec(memory_space=pl.ANY)] * 4,
            out_specs=data,
            scratch_shapes=[pltpu.VMEM((X_RING, parts, MOE_BLOCK, SC_ROW), U32), pltpu.SemaphoreType.DMA((X_RING,)),
                            stage((d, EXPERT_FF)), stage((d, EXPERT_FF)), stage((EXPERT_FF, d)),
                            pltpu.VMEM((d, EXPERT_FF), BF16), pltpu.VMEM((d, EXPERT_FF), BF16),
                            pltpu.VMEM((EXPERT_FF, d), BF16), pltpu.SemaphoreType.DMA((2, 3))]),
        out_shape=jax.ShapeDtypeStruct((parts, slots, SC_ROW), U32),
        compiler_params=_cparams(("arbitrary",)),
        name="experts",
    )(blk_e, blk_valid, blk_src, blk_next, blk_slot, xs, w_gate, w_up, w_down)


def _combine_kernel(yg_ref, gw_ref, h_ref, x_ref, gt_ref, wsg_ref, wsu_ref, wsd_ref, gf_ref, o_ref, *, final):
    h = h_ref[0]
    a = _dot(h, wsg_ref[...])
    hid = a * _sigmoid(a) * _dot(h, wsu_ref[...])
    y = _dot(hid.astype(BF16), wsd_ref[...])
    gw = gw_ref[...]
    y_lo = y[:, :D_MODEL // 2]
    y_hi = y[:, D_MODEL // 2:]
    for k in range(TOP_K):
        lo, hi = _unpack_rows(_load_parts(yg_ref, k))
        y_lo = y_lo + gw[:, k:k + 1] * lo
        y_hi = y_hi + gw[:, k:k + 1] * hi
    y = jnp.concatenate([y_lo, y_hi], axis=1)
    xn = x_ref[0] + gt_ref[0] * y
    if final:
        ms = jnp.mean(xn * xn, axis=-1, keepdims=True)
        xn = xn * lax.rsqrt(ms + RMS_EPS) * gf_ref[...]
    o_ref[0] = xn


def _combine(yg, gw, tok_off, h2, x, gt2, b0, nb, ws_gate, ws_up, ws_down, g_final, final, tm):
    b, s, d = x.shape
    per_batch = gt2.shape[0] == b
    mod_map = (lambda bi, i: (b0 + bi, 0, 0)) if per_batch else (lambda bi, i: (0, 0, 0))
    full = lambda arr: pl.BlockSpec(arr.shape, lambda bi, i: (0,) * arr.ndim)
    tok = lambda width: pl.BlockSpec((1, tm, width), lambda bi, i: (b0 + bi, i, 0))
    gf = g_final[None]
    nblk = s // tm
    blk0 = (tok_off + b0 * s) // tm
    return pl.pallas_call(
        functools.partial(_combine_kernel, final=final),
        grid=(nb, nblk),
        in_specs=[pl.BlockSpec((SC_PARTS, TOP_K, tm, SC_ROW), lambda bi, i: (0, 0, bi * nblk + i, 0)),
                  pl.BlockSpec((tm, TOP_K), lambda bi, i: (blk0 + bi * nblk + i, 0)),
                  tok(d), tok(d),
                  pl.BlockSpec((1, 1, d), mod_map),
                  full(ws_gate), full(ws_up), full(ws_down), full(gf)],
        out_specs=tok(d),
        out_shape=jax.ShapeDtypeStruct((b, s, d), F32),
        input_output_aliases={3: 0},
        compiler_params=_cparams(("parallel", "parallel")),
        name="combine",
    )(yg, gw, h2, x, gt2, ws_gate, ws_up, ws_down, gf)


def _project_latent(x, p, b0=0, nb=None, prev=None):
    w_main, w_gate = p['prep']['w_in']
    mods = p['prep']['mods']
    return _proj_in(x, p['g_norm1'], mods[0], mods[1], w_main, w_gate, tm=min(2048, x.shape[1]), tn=1024,
                    b0=b0, nb=nb, prev=prev)


def _layer(x, ctx_s, p, ctx_out, final, g_final, projected=None, after_piece=None, p_next=None):
    b, s, d = x.shape
    sc = ctx_s.shape[1]
    prep = p['prep']
    sh1, sc1, gt1, sh2, sc2, gt2 = prep['mods']
    csh1, csc1, cgt1, csh2, csc2, cgt2 = prep['mods_ctx']

    w_main, w_gate = prep['w_in']
    main, gate = _project_latent(x, p) if projected is None else projected
    ctx_flat = ctx_s.reshape(1, b * sc, d)
    n_ctx, tn_ctx = (N_MAIN, 1024) if ctx_out else (N_KV_MAIN, N_KV_MAIN // 2)
    main_c, gate_c = _proj_in(ctx_flat, p['g_norm1'], csh1, csc1, w_main, w_gate, tm=min(1024, b * sc),
                              tn=tn_ctx, n=n_ctx)
    main_c = main_c.reshape(b, sc, n_ctx)
    gate_c = gate_c.reshape(b, sc, LANE)

    o_na = _na_latent(main, main_c, prep['na_bias'])

    gate_ws = prep['gate_ws']
    s0 = jnp.zeros((b, GLA_HEADS, GLA_DK, GLA_DV), F32)
    o_cf, o_cb, st_f, st_b = _gla_scan(main_c, gate_c, gate_ws, (s0, s0), ctx_out)
    o_f, o_b, _, _ = _gla_scan(main, gate, gate_ws, (st_f, st_b), True)

    w_na, w_sc, w_gla, w_out, w_router = (prep[name] for name in ('w_na', 'w_sc', 'w_gla', 'w_out', 'w_router'))
    n_lat = b * s
    t = n_lat + (b * sc if ctx_out else 0)
    x, h2, hp_all, lg_all = _merge(o_na, p['conv_w'], o_f, o_b, main, x, gt1, p['gla_norm_g'], w_na, w_sc, w_gla,
                                   w_out, p['g_norm2'], sh2, sc2, w_router, tm=min(512, s), n_routed=t, tok_off=0)
    if ctx_out:
        o_na_c = _dense_attn(main_c)
        ctx_s, h2_c, hp_all, lg_all = _merge(o_na_c, p['conv_w'], o_cf, o_cb, main_c, ctx_s, cgt1, p['gla_norm_g'],
                                             w_na, w_sc, w_gla, w_out, p['g_norm2'], csh2, csc2, w_router,
                                             tm=min(256, sc), n_routed=t, tok_off=n_lat, routed=(hp_all, lg_all))

    eidx, gw, rank, counts = _route(lg_all, p['b_router'])
    padded = (counts + MOE_BLOCK - 1) // MOE_BLOCK * MOE_BLOCK
    pad_end = jnp.cumsum(padded)
    pad_start = pad_end - padded
    onehot = eidx[:, :, None] == jnp.arange(N_EXPERTS, dtype=jnp.int32)
    dest = jnp.sum(jnp.where(onehot, pad_start, 0), axis=-1) + rank
    n_blocks = -(-(t * TOP_K + N_EXPERTS * (MOE_BLOCK - 1)) // MOE_BLOCK)
    slots = n_blocks * MOE_BLOCK
    blk_start = jnp.arange(n_blocks, dtype=jnp.int32) * MOE_BLOCK
    blk_e = jnp.minimum(jnp.sum(pad_end[None, :] <= blk_start[:, None], axis=1), N_EXPERTS - 1).astype(jnp.int32)
    used_end = (pad_start + counts)[blk_e]
    blk_valid = jnp.clip(used_end - blk_start, 0, MOE_BLOCK).astype(jnp.int32)
    n_used = pad_end[-1] // MOE_BLOCK
    blk_src = jnp.minimum(jnp.arange(n_blocks, dtype=jnp.int32), n_used - 1)
    blk_e = blk_e[blk_src]
    ids = jnp.arange(N_EXPERTS, dtype=jnp.int32)
    used = counts > 0
    later_used = jnp.where(used[None, :] & (ids[None, :] > ids[:, None]), ids[None, :], N_EXPERTS).min(axis=1)
    next_used = jnp.where(later_used == N_EXPERTS, -1, later_used).astype(jnp.int32)
    blk_next = next_used[blk_e]
    blk_slot = ((jnp.cumsum(used) - 1) % 2).astype(jnp.int32)[blk_e]

    xs = _dispatch_rows(hp_all, dest, slots)
    if p_next is not None:
        xs, p_next['prep'] = lax.optimization_barrier((xs, p_next['prep']))
    ys = _experts(xs, blk_e, blk_valid, blk_src, blk_next, blk_slot, p['layer'], p['w_exp_gate'], p['w_exp_up'], p['w_exp_down'])
    gw_t = gw.T
    ws_gate, ws_up, ws_down = prep['ws_gate'], prep['ws_up'], prep['ws_down']

    def gathered(t0, n):
        return _gather_rows(ys, dest, t0, n)

    pieces = next(n for n in (4, 2, 1) if b % n == 0)
    nb = b // pieces
    for q in range(pieces):
        x = _combine(gathered(q * nb * s, nb * s), gw_t, 0, h2, x, gt2, q * nb, nb, ws_gate, ws_up, ws_down,
                     g_final, final, tm=min(512, s))
        if after_piece is not None:
            after_piece(x, q * nb, nb)
    if ctx_out:
        ctx_s = _combine(gathered(n_lat, b * sc), gw_t, n_lat, h2_c, ctx_s, cgt2, 0, b, ws_gate, ws_up, ws_down,
                         g_final, False, tm=min(256, sc))
    return x, ctx_s


def kernel(x, c, ctx, c_ctx, w_mod, b_mod, g_norm1, g_norm2, w_in, na_rpb, w_branch_na, conv_w, w_branch_sc,
           gla_gate_w, gla_gate_b, gla_norm_g, w_branch_gla, w_out, w_router, b_router, w_exp_gate, w_exp_up,
           w_exp_down, w_sh_gate, w_sh_up, w_sh_down, g_final):
    stacked = dict(g_norm1=g_norm1, g_norm2=g_norm2, na_rpb=na_rpb, w_branch_na=w_branch_na,
                   conv_w=conv_w, w_branch_sc=w_branch_sc, gla_gate_w=gla_gate_w, gla_gate_b=gla_gate_b,
                   gla_norm_g=gla_norm_g, w_branch_gla=w_branch_gla, w_out=w_out, w_router=w_router,
                   b_router=b_router,
                   w_sh_gate=w_sh_gate, w_sh_up=w_sh_up, w_sh_down=w_sh_down)
    depth = w_in.shape[0]
    rows = x.shape[1] // GRID_W
    layers = []
    for i in range(depth):
        p = {name: arr[i] for name, arr in stacked.items()}
        p.update(layer=i, w_exp_gate=w_exp_gate, w_exp_up=w_exp_up, w_exp_down=w_exp_down)
        mods, mods_ctx = _mod_vectors(c, c_ctx, w_mod, b_mod, i)
        p['prep'] = dict(
            w_in=_prep_w_in(w_in, i), mods=mods, mods_ctx=mods_ctx,
            na_bias=_na_bias_table(p['na_rpb'], rows, min(NA_WIN_R, rows)),
            gate_ws=_gla_gate_weights(p['gla_gate_w'], p['gla_gate_b']),
            w_na=p['w_branch_na'].astype(BF16), w_sc=p['w_branch_sc'].astype(BF16),
            w_gla=p['w_branch_gla'].astype(BF16), w_out=p['w_out'].astype(BF16),
            w_router=jnp.pad(p['w_router'], ((0, 0), (0, LANE - N_EXPERTS))).astype(BF16),
            ws_gate=p['w_sh_gate'].astype(BF16), ws_up=p['w_sh_up'].astype(BF16),
            ws_down=p['w_sh_down'].astype(BF16))
        layers.append(p)

    ctx_s = ctx
    projected = None
    for i, p in enumerate(layers):
        last = i == depth - 1
        p_next = None if last else layers[i + 1]
        after_piece = None
        next_projected = []
        if not last:
            def after_piece(xq, b0, nb, p_next=p_next, acc=next_projected):
                acc.append(_project_latent(xq, p_next, b0, nb, acc[-1] if acc else None))

        x, ctx_s = _layer(x, ctx_s, p, not last, last, g_final, projected, after_piece, p_next)
        projected = next_projected[-1] if next_projected else None
    return x
```

```python
import functools

import numpy as np
import jax
import jax.numpy as jnp
from jax import lax
from jax.experimental import pallas as pl
from jax.experimental.pallas import tpu as pltpu
from jax.experimental.pallas import tpu_sc as plsc

F32 = jnp.float32
BF16 = jnp.bfloat16
U32 = jnp.uint32

D_MODEL = 1024
N_MOD = 6
RMS_EPS = 1e-6
NEG_INF = -1e30
GRID_W = 64
NA_HEADS = 8
NA_HEAD_DIM = 64
NA_WIDTH = NA_HEADS * NA_HEAD_DIM
NA_WIN_R = 8
NA_WIN_C = 16
NA_GROUP = 4
SC_WIDTH = 512
GLA_HEADS = 4
GLA_KEY_WIDTH = 512
GLA_VAL_WIDTH = 1024
GLA_DK = GLA_KEY_WIDTH // GLA_HEADS
GLA_DV = GLA_VAL_WIDTH // GLA_HEADS
GLA_GATE_RANK = 16
GLA_GATE_TAU = 16.0
LOG2_E = 1.4426950408889634
N_EXPERTS = 64
N_EXPERT_GROUPS = 8
GROUP_SIZE = N_EXPERTS // N_EXPERT_GROUPS
TOPK_GROUPS = 4
TOP_K = 8
EXPERT_FF = 256
ROUTED_SCALE = 2.5
X_RING = 4
MOE_BLOCK = 1024

LANE = 128
GLA_C = 128
GLA_STEP_CHUNKS = 4
GLA_LEVELS = tuple(GLA_C >> (i + 1) for i in range(GLA_C.bit_length() - 1))
VMEM_LIMIT = 48 * 1024 * 1024
SC_WINDOW = 128
SC_ROW = 256
SC_PARTS = D_MODEL // 2 // SC_ROW

OFF_V_GLA = 0
OFF_K_NA = 1024
OFF_V_NA = 1536
OFF_K_GLA = 2048
N_KV_MAIN = 2560
OFF_Q_NA = 2560
OFF_B_SC = 3072
OFF_C_SC = 3584
OFF_X_SC = 4096
OFF_Q_GLA = 4608
OFF_R_GLA = 5120
OFF_MERGE = 6144
N_MAIN = 9216


def _cparams(sem, vmem=VMEM_LIMIT):
    return pltpu.CompilerParams(dimension_semantics=sem, vmem_limit_bytes=vmem)


def _dot(a, b):
    return jnp.dot(a, b, preferred_element_type=F32)


def _dot_nt(a, b):
    return lax.dot_general(a, b, (((1,), (1,)), ((), ())), preferred_element_type=F32)


def _dot_tn(a, b):
    return lax.dot_general(a, b, (((0,), (0,)), ((), ())), preferred_element_type=F32)


def _sigmoid(x):
    return 0.5 * jnp.tanh(0.5 * x) + 0.5


def _pack_rows(x):
    n = x.shape[1] // 2
    r = x.astype(BF16).astype(F32)
    lo = pltpu.bitcast(r[:, :n], U32) >> 16
    hi = pltpu.bitcast(r[:, n:], U32)
    return hi | lo


def _store_parts(ref, words):
    for part in range(SC_PARTS):
        dst = ref.at[part, 0] if len(ref.shape) == 4 else ref.at[part]
        dst[...] = words[:, part * SC_ROW:(part + 1) * SC_ROW]


def _load_parts(ref, *lead):
    return jnp.concatenate([ref[(part,) + lead] for part in range(SC_PARTS)], axis=-1)


def _unpack_rows(w):
    lo = pltpu.bitcast(w << 16, F32)
    hi = pltpu.bitcast(w & jnp.uint32(0xFFFF0000), F32)
    return lo, hi


def _mod_kernel(a_ref, w_ref, b_ref, o_ref):
    a = a_ref[...]
    a = a * _sigmoid(a)
    o_ref[...] = _dot(a.astype(BF16), w_ref[0].astype(BF16)) + b_ref[0]


def _mod_vectors(c, c_ctx, w_mod, b_mod, layer):
    b = c.shape[0]
    rows = -(-(b + 1) // 8) * 8
    a = jnp.concatenate([c, c_ctx[None], jnp.zeros((rows - b - 1, D_MODEL), F32)], axis=0)
    n = N_MOD * D_MODEL
    tn = 1536
    out = pl.pallas_call(
        _mod_kernel,
        grid=(n // tn,),
        in_specs=[pl.BlockSpec((rows, D_MODEL), lambda j: (0, 0)),
                  pl.BlockSpec((1, D_MODEL, tn), lambda j: (layer, 0, j)),
                  pl.BlockSpec((1, 1, tn), lambda j: (layer, 0, j))],
        out_specs=pl.BlockSpec((rows, tn), lambda j: (0, j)),
        out_shape=jax.ShapeDtypeStruct((rows, n), F32),
        compiler_params=_cparams(("parallel",)),
        name="mod_vectors",
    )(a, w_mod, b_mod[:, None])
    lat = out[:b].reshape(b, N_MOD, 1, D_MODEL)
    ctx = out[b].reshape(N_MOD, 1, 1, D_MODEL)
    return [lat[:, i] for i in range(N_MOD)], [ctx[i] for i in range(N_MOD)]


def _proj_kernel(x_ref, g_ref, sh_ref, sc_ref, w_ref, wg_ref, *rest):
    o_ref, og_ref, h_ref = rest[-3:]

    @pl.when(pl.program_id(2) == 0)
    def _():
        x = x_ref[0]
        ms = jnp.mean(x * x, axis=-1, keepdims=True)
        h = x * lax.rsqrt(ms + RMS_EPS) * g_ref[...] * (1.0 + sc_ref[0]) + sh_ref[0]
        hb = h.astype(BF16)
        h_ref[...] = hb
        og_ref[0] = _dot(hb, wg_ref[...])

    o_ref[0] = _dot(h_ref[...], w_ref[...]).astype(o_ref.dtype)


W_IN_TILE = 512
W_IN_GATE_SHIFT = 2 * GLA_GATE_RANK


def _prep_w_in_kernel(a_ref, b_ref, o_ref, g_ref):
    t = pl.program_id(0)
    first_lat = N_KV_MAIN // W_IN_TILE
    a = a_ref[0]

    @pl.when(t < first_lat)
    def _():
        o_ref[...] = a.T.astype(BF16)

    @pl.when(t >= first_lat)
    def _():
        moved = jnp.concatenate([a[W_IN_GATE_SHIFT:], b_ref[0]], axis=0)
        scale = jnp.where(t == first_lat, NA_HEAD_DIM ** -0.5, 1.0)
        o_ref[...] = (moved * scale).T.astype(BF16)

    @pl.when(t == first_lat)
    def _():
        head = a[:LANE]
        row = lax.broadcasted_iota(jnp.int32, head.shape, 0)
        g_ref[...] = jnp.where(row < W_IN_GATE_SHIFT, head, 0.0).T.astype(BF16)


def _prep_w_in(w_in, layer):
    d = w_in.shape[1]
    w_t = jnp.swapaxes(w_in, 1, 2)
    first_lat = N_KV_MAIN // W_IN_TILE
    kv_perm = OFF_K_NA // W_IN_TILE

    def a_map(t):
        return (layer, jnp.where(t < first_lat, (t + first_lat - kv_perm) % first_lat, t), 0)

    def b_map(t):
        return (layer, jnp.where(t < first_lat, 0, (t + 1) * (W_IN_TILE // W_IN_GATE_SHIFT)), 0)

    return pl.pallas_call(
        _prep_w_in_kernel,
        grid=(N_MAIN // W_IN_TILE,),
        in_specs=[pl.BlockSpec((1, W_IN_TILE, d), a_map),
                  pl.BlockSpec((1, W_IN_GATE_SHIFT, d), b_map)],
        out_specs=[pl.BlockSpec((d, W_IN_TILE), lambda t: (0, t)),
                   pl.BlockSpec((d, LANE), lambda t: (0, 0))],
        out_shape=[jax.ShapeDtypeStruct((d, N_MAIN), BF16), jax.ShapeDtypeStruct((d, LANE), BF16)],
        compiler_params=_cparams(("arbitrary",)),
        name="prep_w_in",
    )(w_t, w_t)


def _proj_in(x, g, shift, scale, w_main, w_gate, tm, tn, n=None, b0=0, nb=None, prev=None):
    b, s, d = x.shape
    n = w_main.shape[1] if n is None else n
    nb = b if nb is None else nb
    per_batch = shift.shape[0] == b
    mod_map = (lambda bi, i, j: (b0 + bi, 0, 0)) if per_batch else (lambda bi, i, j: (0, 0, 0))
    operands = (x, g[None], shift, scale, w_main, w_gate)
    extend = () if prev is None else tuple(prev)
    return pl.pallas_call(
        _proj_kernel,
        grid=(nb, s // tm, n // tn),
        in_specs=[pl.BlockSpec((1, tm, d), lambda bi, i, j: (b0 + bi, i, 0)),
                  pl.BlockSpec((1, d), lambda bi, i, j: (0, 0)),
                  pl.BlockSpec((1, 1, d), mod_map),
                  pl.BlockSpec((1, 1, d), mod_map),
                  pl.BlockSpec((d, tn), lambda bi, i, j: (0, j)),
                  pl.BlockSpec((d, LANE), lambda bi, i, j: (0, 0))]
                 + [pl.BlockSpec(memory_space=pl.ANY)] * len(extend),
        out_specs=[pl.BlockSpec((1, tm, tn), lambda bi, i, j: (b0 + bi, i, j)),
                   pl.BlockSpec((1, tm, LANE), lambda bi, i, j: (b0 + bi, i, 0))],
        out_shape=[jax.ShapeDtypeStruct((b, s, n), BF16),
                   jax.ShapeDtypeStruct((b, s, LANE), F32)],
        scratch_shapes=[pltpu.VMEM((tm, d), BF16)],
        input_output_aliases={len(operands): 0, len(operands) + 1: 1} if extend else {},
        compiler_params=_cparams(("parallel", "parallel", "arbitrary")),
        name="proj_in",
    )(*operands, *extend)


def _softmax_av(q, keys, vals, biases):
    scores = []
    for kk, bb in zip(keys, biases):
        s = _dot_nt(q, kk)
        scores.append(s if bb is None else s + bb)
    m = scores[0].max(axis=-1, keepdims=True)
    for s in scores[1:]:
        m = jnp.maximum(m, s.max(axis=-1, keepdims=True))
    num = None
    den = None
    for s, vv in zip(scores, vals):
        e = jnp.exp(s - m)
        dsum = e.sum(axis=-1, keepdims=True)
        o = _dot(e.astype(BF16), vv)
        num = o if num is None else num + o
        den = dsum if den is None else den + dsum
    return num / den


def _na_kernel(q_ref, k_ref, v_ref, kc_ref, vc_ref, *rest, rows, kr):
    *bias_refs, o_ref = rest
    kc = kc_ref[0]
    vc = vc_ref[0]
    for j, bias_ref in enumerate(bias_refs):
        r = pl.program_id(1) * len(bias_refs) + j
        row_start = jnp.clip(r - kr // 2, 0, rows - kr)
        start = pl.multiple_of(row_start * GRID_W, GRID_W)
        n_win = kr * GRID_W
        q = q_ref[0, j * GRID_W:(j + 1) * GRID_W, :]
        kw = k_ref[0, pl.ds(start, n_win), :]
        vw = v_ref[0, pl.ds(start, n_win), :]
        o_ref[0, j * GRID_W:(j + 1) * GRID_W, :] = _na_row(q, kw, vw, kc, vc, bias_ref).astype(o_ref.dtype)


def _na_row(q, kw, vw, kc, vc, bias_ref):
    gw = NA_GROUP * NA_HEAD_DIM
    stacked = (NA_GROUP * GRID_W, gw)
    on_head = (lax.broadcasted_iota(jnp.int32, stacked, 0) // GRID_W
               == lax.broadcasted_iota(jnp.int32, stacked, 1) // NA_HEAD_DIM)
    outs = []
    for g in range(NA_HEADS // NA_GROUP):
        sl = slice(g * gw, (g + 1) * gw)
        q_all = jnp.where(on_head, jnp.concatenate([q[:, sl]] * NA_GROUP, axis=0), jnp.zeros((), q.dtype))
        bias = bias_ref[0, g * NA_GROUP * GRID_W:(g + 1) * NA_GROUP * GRID_W, :]
        o_all = _softmax_av(q_all, [kw[:, sl], kc[:, sl]], [vw[:, sl], vc[:, sl]], [bias, None])
        o_all = jnp.where(on_head, o_all, 0.0).reshape(NA_GROUP, GRID_W, gw)
        outs.append(o_all.sum(axis=0))
    return jnp.concatenate(outs, axis=-1)


def _na_bias_table(rpb, rows, kr):
    col = np.arange(GRID_W)
    col_start = np.clip(col - NA_WIN_C // 2, 0, GRID_W - NA_WIN_C)
    col_ok = (col[None, :] >= col_start[:, None]) & (col[None, :] < col_start[:, None] + NA_WIN_C)
    d_col = np.clip(col[None, :] - col[:, None], -(NA_WIN_C - 1), NA_WIN_C - 1) + NA_WIN_C - 1
    n_dr, n_dc = rpb.shape[1], rpb.shape[2]
    onehot = jnp.asarray((d_col.reshape(-1)[None, :] == np.arange(n_dc)[:, None]).astype(np.float32))
    by_col = jnp.dot(rpb.astype(F32).reshape(NA_HEADS * n_dr, n_dc), onehot, precision=lax.Precision.HIGHEST)
    by_col = by_col.reshape(NA_HEADS, n_dr, GRID_W, GRID_W)
    by_col = jnp.where(col_ok[None, None], by_col, NEG_INF)
    tables = []
    for o in range(kr):
        lo = NA_WIN_R - 1 - o
        tables.append(by_col[:, lo:lo + kr].transpose(0, 2, 1, 3).reshape(NA_HEADS, GRID_W, kr * GRID_W))
    return jnp.stack(tables).reshape(kr, NA_HEADS * GRID_W, kr * GRID_W)


def _na_latent(main, main_ctx, bias):
    b, s, _ = main.shape
    sc = main_ctx.shape[1]
    rows = s // GRID_W
    kr = min(NA_WIN_R, rows)
    w = NA_WIDTH

    per_step = next(n for n in (8, 4, 2, 1) if rows % n == 0)

    def bias_spec(j):
        def bias_map(bi, i):
            r = i * per_step + j
            return (r - jnp.clip(r - kr // 2, 0, rows - kr), 0, 0)
        return pl.BlockSpec((1, NA_HEADS * GRID_W, kr * GRID_W), bias_map)

    return pl.pallas_call(
        functools.partial(_na_kernel, rows=rows, kr=kr),
        grid=(b, rows // per_step),
        in_specs=[pl.BlockSpec((1, per_step * GRID_W, w), lambda bi, i: (bi, i, OFF_Q_NA // w)),
                  pl.BlockSpec((1, s, w), lambda bi, i: (bi, 0, OFF_K_NA // w)),
                  pl.BlockSpec((1, s, w), lambda bi, i: (bi, 0, OFF_V_NA // w)),
                  pl.BlockSpec((1, sc, w), lambda bi, i: (bi, 0, OFF_K_NA // w)),
                  pl.BlockSpec((1, sc, w), lambda bi, i: (bi, 0, OFF_V_NA // w))]
                 + [bias_spec(j) for j in range(per_step)],
        out_specs=pl.BlockSpec((1, per_step * GRID_W, w), lambda bi, i: (bi, i, 0)),
        out_shape=jax.ShapeDtypeStruct((b, s, w), BF16),
        compiler_params=_cparams(("parallel", "arbitrary")),
        name="na_latent",
    )(main, main, main, main_ctx, main_ctx, *([bias] * per_step))


def _dense_attn_kernel(q_ref, k_ref, v_ref, o_ref):
    q = q_ref[0]
    k = k_ref[0]
    v = v_ref[0]
    outs = []
    for h in range(NA_HEADS):
        sl = slice(h * NA_HEAD_DIM, (h + 1) * NA_HEAD_DIM)
        outs.append(_softmax_av(q[:, sl], [k[:, sl]], [v[:, sl]], [None]))
    o_ref[0] = jnp.concatenate(outs, axis=-1).astype(o_ref.dtype)


def _dense_attn(main_ctx):
    b, sc, _ = main_ctx.shape
    w = NA_WIDTH
    return pl.pallas_call(
        _dense_attn_kernel,
        grid=(b,),
        in_specs=[pl.BlockSpec((1, sc, w), lambda bi: (bi, 0, OFF_Q_NA // w)),
                  pl.BlockSpec((1, sc, w), lambda bi: (bi, 0, OFF_K_NA // w)),
                  pl.BlockSpec((1, sc, w), lambda bi: (bi, 0, OFF_V_NA // w))],
        out_specs=pl.BlockSpec((1, sc, w), lambda bi: (bi, 0, 0)),
        out_shape=jax.ShapeDtypeStruct((b, sc, w), BF16),
        compiler_params=_cparams(("parallel",)),
        name="ctx_attn",
    )(main_ctx, main_ctx, main_ctx)


HALO = 16


def _gla_matrices(reverse):
    c = GLA_C
    t = np.arange(c)[:, None]
    m = np.arange(c)[None, :]
    blocks = [m <= t, m > t]
    for b in GLA_LEVELS:
        first = (t // (2 * b)) * (2 * b) + b
        is_q = (t & b) != 0
        blocks.append(np.where(is_q, (m > first) & (m <= t), (m > t) & (m <= first)))
    mats = np.stack(blocks).astype(np.float32)
    if reverse:
        mats = mats[:, ::-1, ::-1]
    mats = mats.reshape(-1, c)
    return jnp.asarray(np.concatenate([mats, mats], axis=1), dtype=BF16)


def _split_bf16(x):
    hi = x.astype(BF16)
    return hi, (x - hi.astype(F32)).astype(BF16)


def _pair_block_diag(x):
    lane = lax.broadcasted_iota(jnp.int32, x.shape, 1)
    zero = jnp.zeros((), x.dtype)
    return jnp.concatenate([jnp.where(lane < GLA_DK, x, zero), jnp.where(lane >= GLA_DK, x, zero)], axis=0)


def _gla_kernel(*refs, emit):
    n_in = 9 if emit else 8
    n_out = 2 if emit else 1
    ins, outs, scratch = refs[:2 * n_in], refs[2 * n_in:2 * (n_in + n_out)], refs[2 * (n_in + n_out):]
    ins = [ins[d * n_in:(d + 1) * n_in] for d in range(2)]
    outs = [outs[d * n_out:(d + 1) * n_out] for d in range(2)]
    step = pl.program_id(1)

    @pl.when(step == 0)
    def _():
        for d in range(2):
            scratch[d][...] = ins[d][-1][0]

    pending = [_gla_direction(ins[d][:-1], outs[d][:-1], scratch[d], reverse, emit)
               for d, reverse in enumerate((False, True))]
    while pending:
        pending = [stages for stages in pending if next(stages, "done") != "done"]

    @pl.when(step == pl.num_programs(1) - 1)
    def _():
        for d in range(2):
            outs[d][-1][0] = scratch[d][...]


def _gla_direction(ins, outs, st_ref, reverse, emit):
    gate_ref = ins[-5]
    order = range(gate_ref.shape[1] // GLA_C)
    for sub in (reversed(order) if reverse else order):
        yield from _gla_chunk(ins, outs, st_ref, reverse, emit, slice(sub * GLA_C, (sub + 1) * GLA_C))


def _gla_chunk(ins, outs, st_ref, reverse, emit, rows):
    if emit:
        q_ref, k_ref, v_ref, gt_ref, w2a_ref, w2b_ref, b2_ref, a_ref = ins
        (o_ref,) = outs
    else:
        k_ref, v_ref, gt_ref, w2a_ref, w2b_ref, b2_ref, a_ref = ins
    c = GLA_C
    pw = 2 * GLA_DK

    lr_hi, lr_lo = _split_bf16(gt_ref[0, rows])
    logit = (_dot(jnp.concatenate([lr_hi, lr_lo], axis=1), w2a_ref[...]) + _dot(lr_hi, w2b_ref[...])
             + b2_ref[...])
    g = (jnp.minimum(logit, 0.0) - jnp.log1p(jnp.exp(-jnp.abs(logit)))) * (LOG2_E / GLA_GATE_TAU)
    g_hi, g_lo = _split_bf16(g)
    args = _dot(a_ref[...], jnp.concatenate([g_hi, g_lo], axis=0))
    cum = args[0:c]
    rem = args[c:2 * c]
    last_row = cum[0:1] if reverse else cum[c - 1:c]
    yield

    k = k_ref[0, rows].astype(F32)
    v = v_ref[0, rows]
    atts = []
    if emit:
        q = q_ref[0, rows].astype(F32) * (GLA_DK ** -0.5)
        row_t = lax.broadcasted_iota(jnp.int32, (c, pw), 0)
        si = lax.broadcasted_iota(jnp.int32, (2 * c, c), 0) & (c - 1)
        ti = lax.broadcasted_iota(jnp.int32, (2 * c, c), 1)
        if reverse:
            row_t, ti, si = c - 1 - row_t, c - 1 - ti, c - 1 - si
        for hp in range(GLA_HEADS // 2):
            cs = slice(hp * pw, (hp + 1) * pw)
            qp, kp = q[:, cs], k[:, cs]
            att = jnp.where(ti == si, _dot_nt(_pair_block_diag(kp.astype(BF16)), qp.astype(BF16)), 0.0)
            for l, b in enumerate(GLA_LEVELS):
                x = (jnp.exp2(args[(2 + l) * c:(3 + l) * c, cs])
                     * jnp.where((row_t & b) != 0, qp, kp)).astype(BF16)
                pair = (((ti ^ si) >> (b.bit_length() - 1)) == 1) & ((ti & b) != 0)
                att = jnp.where(pair, _dot_nt(_pair_block_diag(x), x), att)
                yield
            atts.append(att.astype(BF16))

    outs = []
    for h in range(GLA_HEADS):
        sl = slice(h * GLA_DK, (h + 1) * GLA_DK)
        kh = k[:, sl]
        vh = v[:, h * GLA_DV:(h + 1) * GLA_DV]
        state = st_ref[h]
        kd = (kh * jnp.exp2(rem[:, sl])).astype(BF16)
        decay = jnp.exp2(jnp.broadcast_to(last_row[:, sl], (GLA_DK, GLA_DK))).T
        decay = jnp.concatenate([decay] * (GLA_DV // GLA_DK), axis=1)
        if emit:
            qd = (q[:, sl] * jnp.exp2(cum[:, sl])).astype(BF16)
            att_t = atts[h // 2][(h % 2) * c:(h % 2 + 1) * c]
            both = _dot_tn(jnp.concatenate([att_t, kd], axis=1), vh)
            outs.append(_dot(qd, state.astype(BF16)) + both[:c])
            st_ref[h] = decay * state + both[c:]
        else:
            st_ref[h] = decay * state + _dot_tn(kd, vh)
        yield

    if emit:
        o_ref[0, rows] = jnp.concatenate(outs, axis=-1).astype(o_ref.dtype)


def _gla_scan(main, gate, gate_ws, s0s, emit):
    b, l, _ = main.shape
    step_rows = GLA_C * min(GLA_STEP_CHUNKS, l // GLA_C)
    n = l // step_rows
    const = lambda arr: pl.BlockSpec(arr.shape, lambda bi, s: (0,) * arr.ndim)
    state_spec = pl.BlockSpec((1, GLA_HEADS, GLA_DK, GLA_DV), lambda bi, s: (bi, 0, 0, 0))
    state_shape = jax.ShapeDtypeStruct((b, GLA_HEADS, GLA_DK, GLA_DV), F32)
    in_specs, args, out_specs, out_shape = [], [], [], []
    for reverse in (False, True):
        amat = _gla_matrices(reverse)
        w2a, w2b, b2 = gate_ws[reverse]

        def col(block, reverse=reverse):
            return lambda bi, s: (bi, n - 1 - s if reverse else s, block)

        if emit:
            in_specs.append(pl.BlockSpec((1, step_rows, GLA_KEY_WIDTH), col(OFF_Q_GLA // GLA_KEY_WIDTH)))
            args.append(main)
            out_specs.append(pl.BlockSpec((1, step_rows, GLA_VAL_WIDTH), col(0)))
            out_shape.append(jax.ShapeDtypeStruct((b, l, GLA_VAL_WIDTH), BF16))
        in_specs += [pl.BlockSpec((1, step_rows, GLA_KEY_WIDTH), col(OFF_K_GLA // GLA_KEY_WIDTH)),
                     pl.BlockSpec((1, step_rows, GLA_VAL_WIDTH), col(OFF_V_GLA // GLA_VAL_WIDTH)),
                     pl.BlockSpec((1, step_rows, LANE), col(0)),
                     const(w2a), const(w2b), const(b2), const(amat), state_spec]
        args += [main, main, gate, w2a, w2b, b2, amat, s0s[reverse]]
        out_specs.append(state_spec)
        out_shape.append(state_shape)
    res = pl.pallas_call(
        functools.partial(_gla_kernel, emit=emit),
        grid=(b, n),
        in_specs=in_specs,
        out_specs=out_specs,
        out_shape=out_shape,
        scratch_shapes=[pltpu.VMEM((GLA_HEADS, GLA_DK, GLA_DV), F32)] * 2,
        compiler_params=_cparams(("parallel", "arbitrary")),
        name="gla_scan",
    )(*args)
    return (res[0], res[2], res[1], res[3]) if emit else (None, None, res[0], res[1])


def _gla_gate_weights(gate_w, gate_b):
    out = []
    for dr in range(2):
        w = jnp.zeros((LANE, GLA_KEY_WIDTH), F32)
        w = w.at[dr * GLA_GATE_RANK:(dr + 1) * GLA_GATE_RANK].set(gate_w[dr])
        w_hi = w.astype(BF16)
        w_lo = (w - w_hi.astype(F32)).astype(BF16)
        out.append((jnp.concatenate([w_hi, w_hi], axis=0), w_lo, gate_b[dr][None]))
    return out


def _conv_tile(b_ref, c_ref, x_ref, cp_ref, xp_ref, cn_ref, xn_ref, w_ref):
    i, n = pl.program_id(1), pl.num_programs(1)
    u = c_ref[0].astype(F32) * x_ref[0].astype(F32)
    tm = u.shape[0]
    before = jnp.where(i > 0, 1.0, 0.0) * (cp_ref[0, HALO - 1:HALO].astype(F32) * xp_ref[0, HALO - 1:HALO].astype(F32))
    after = jnp.where(i < n - 1, 1.0, 0.0) * (cn_ref[0, 0:1].astype(F32) * xn_ref[0, 0:1].astype(F32))
    t = lax.broadcasted_iota(jnp.int32, u.shape, 0)
    prev = jnp.where(t == 0, before, pltpu.roll(u, 1, axis=0))
    nxt = jnp.where(t == tm - 1, after, pltpu.roll(u, tm - 1, axis=0))
    w = w_ref[...]
    return b_ref[0].astype(F32) * (prev * w[0:1] + u * w[1:2] + nxt * w[2:3])


def _merge_kernel(ona_ref, bsc_ref, csc_ref, xsc_ref, cp_ref, xp_ref, cn_ref, xn_ref, cw_ref, of_ref, ob_ref,
                  r_ref, gna_ref, gsc_ref, ggl_ref, x_ref, gt_ref,
                  gn_ref, wna_ref, wsc_ref, wgl_ref, wo_ref, g2_ref, sh2_ref, sc2_ref, wr_ref, *rest):
    xo_ref, h2_ref, hp_ref, lg_ref = rest[-4:]
    o_sc = _conv_tile(bsc_ref, csc_ref, xsc_ref, cp_ref, xp_ref, cn_ref, xn_ref, cw_ref).astype(BF16)
    o = of_ref[0].astype(F32) + ob_ref[0].astype(F32)
    normed = []
    for h in range(GLA_HEADS):
        oh = o[:, h * GLA_DV:(h + 1) * GLA_DV]
        ms = jnp.mean(oh * oh, axis=-1, keepdims=True)
        normed.append(oh * lax.rsqrt(ms + RMS_EPS))
    r = r_ref[0].astype(F32)
    y_gla = jnp.concatenate(normed, axis=-1) * gn_ref[...] * (r * _sigmoid(r))
    y = (_sigmoid(gna_ref[0].astype(F32)) * _dot(ona_ref[0], wna_ref[...])
         + _sigmoid(gsc_ref[0].astype(F32)) * _dot(o_sc, wsc_ref[...])
         + _sigmoid(ggl_ref[0].astype(F32)) * _dot(y_gla.astype(BF16), wgl_ref[...]))
    xn = x_ref[0] + gt_ref[0] * _dot(y.astype(BF16), wo_ref[...])
    xo_ref[0] = xn
    ms = jnp.mean(xn * xn, axis=-1, keepdims=True)
    h2 = xn * lax.rsqrt(ms + RMS_EPS) * g2_ref[...] * (1.0 + sc2_ref[0]) + sh2_ref[0]
    h2b = h2.astype(BF16)
    h2_ref[0] = h2b
    _store_parts(hp_ref, _pack_rows(h2))
    lg_ref[...] = _dot(h2b, wr_ref[...])


def _merge(o_na, conv_w, o_f, o_b, main, x, gt1, gn, w_na, w_sc, w_gla, w_out, g2, sh2, sc2, w_router, tm,
           n_routed, tok_off, routed=None):
    b, s, d = x.shape
    per_batch = gt1.shape[0] == b
    mod_map = (lambda bi, i: (bi, 0, 0)) if per_batch else (lambda bi, i: (0, 0, 0))
    tok = lambda width, blk: pl.BlockSpec((1, tm, width), lambda bi, i: (bi, i, blk))
    full = lambda arr: pl.BlockSpec(arr.shape, lambda bi, i: (0,) * arr.ndim)
    mod = pl.BlockSpec((1, 1, d), mod_map)
    gn_t = jnp.tile(gn, GLA_HEADS)[None]
    g2_t = g2[None]
    per_tile = tm // HALO
    last_halo = s // HALO - 1
    halo_prev = lambda blk: pl.BlockSpec(
        (1, HALO, SC_WIDTH), lambda bi, i: (bi, jnp.maximum(i * per_tile - 1, 0), blk))
    halo_next = lambda blk: pl.BlockSpec(
        (1, HALO, SC_WIDTH), lambda bi, i: (bi, jnp.minimum((i + 1) * per_tile, last_halo), blk))
    col_b, col_c, col_x = OFF_B_SC // SC_WIDTH, OFF_C_SC // SC_WIDTH, OFF_X_SC // SC_WIDTH
    extend = () if routed is None else tuple(routed)
    operands = (o_na, main, main, main, main, main, main, main, conv_w, o_f, o_b, main, main, main, main, x, gt1,
                gn_t, w_na, w_sc, w_gla, w_out, g2_t, sh2, sc2, w_router)
    n_in = len(operands)
    nblk = s // tm
    blk0 = tok_off // tm
    return pl.pallas_call(
        _merge_kernel,
        grid=(b, s // tm),
        in_specs=[tok(NA_WIDTH, 0), tok(SC_WIDTH, col_b), tok(SC_WIDTH, col_c), tok(SC_WIDTH, col_x),
                  halo_prev(col_c), halo_prev(col_x), halo_next(col_c), halo_next(col_x), full(conv_w),
                  tok(GLA_VAL_WIDTH, 0), tok(GLA_VAL_WIDTH, 0),
                  tok(d, OFF_R_GLA // d), tok(d, OFF_MERGE // d), tok(d, OFF_MERGE // d + 1),
                  tok(d, OFF_MERGE // d + 2), tok(d, 0), mod,
                  full(gn_t), full(w_na), full(w_sc), full(w_gla), full(w_out), full(g2_t), mod, mod,
                  full(w_router)] + [pl.BlockSpec(memory_space=pl.ANY)] * len(extend),
        out_specs=[tok(d, 0), tok(d, 0),
                   pl.BlockSpec((SC_PARTS, tm, SC_ROW), lambda bi, i: (0, blk0 + bi * nblk + i, 0)),
                   pl.BlockSpec((tm, LANE), lambda bi, i: (blk0 + bi * nblk + i, 0))],
        out_shape=[jax.ShapeDtypeStruct((b, s, d), F32),
                   jax.ShapeDtypeStruct((b, s, d), BF16),
                   jax.ShapeDtypeStruct((SC_PARTS, n_routed, SC_ROW), U32),
                   jax.ShapeDtypeStruct((n_routed, LANE), F32)],
        input_output_aliases={n_in: 2, n_in + 1: 3} if extend else {},
        compiler_params=_cparams(("parallel", "parallel")),
        name="merge",
    )(*operands, *extend)


def _router_kernel(lg_ref, br_ref, tri_ref, eidx_ref, gw_ref, rank_ref, cnt_ref, carry_ref):
    tm = lg_ref.shape[0]

    @pl.when(pl.program_id(0) == 0)
    def _():
        carry_ref[...] = jnp.zeros_like(carry_ref)

    scores = _sigmoid(lg_ref[...].T[:N_EXPERTS])
    sel = scores + br_ref[...]
    neg = -jnp.inf

    sel3 = sel.reshape(N_EXPERT_GROUPS, GROUP_SIZE, tm)
    i3 = lax.broadcasted_iota(jnp.int32, sel3.shape, 1)
    m1 = sel3.max(axis=1, keepdims=True)
    first = jnp.where(sel3 == m1, i3, GROUP_SIZE).min(axis=1, keepdims=True)
    m2 = jnp.where(i3 == first, neg, sel3).max(axis=1, keepdims=True)
    gscore = (m1 + m2)[:, 0, :]

    gi = lax.broadcasted_iota(jnp.int32, gscore.shape, 0)
    gmask = jnp.zeros(gscore.shape, jnp.bool_)
    for _ in range(TOPK_GROUPS):
        m = gscore.max(axis=0, keepdims=True)
        pick = gi == jnp.where(gscore == m, gi, N_EXPERT_GROUPS).min(axis=0, keepdims=True)
        gmask = gmask | pick
        gscore = jnp.where(pick, neg, gscore)
    emask = jnp.broadcast_to(gmask[:, None, :], sel3.shape).reshape(N_EXPERTS, tm)
    sel = jnp.where(emask, sel, neg)

    ei = lax.broadcasted_iota(jnp.int32, sel.shape, 0)
    picks, idxs, ws = [], [], []
    for _ in range(TOP_K):
        m = sel.max(axis=0, keepdims=True)
        idx = jnp.where(sel == m, ei, N_EXPERTS).min(axis=0, keepdims=True)
        pick = ei == idx
        picks.append(pick)
        idxs.append(idx)
        ws.append(jnp.where(pick, scores, 0.0).sum(axis=0, keepdims=True))
        sel = jnp.where(pick, neg, sel)
    w = jnp.concatenate(ws, axis=0)
    gw_ref[...] = w / w.sum(axis=0, keepdims=True) * ROUTED_SCALE
    eidx_ref[...] = jnp.concatenate(idxs, axis=0)

    onehot = picks[0]
    for p in picks[1:]:
        onehot = onehot | p
    onehot = jnp.where(onehot, 1.0, 0.0).astype(BF16)
    before = _dot(onehot, tri_ref[...]) + jnp.tile(carry_ref[...], (1, tm // LANE))
    rank_ref[...] = jnp.concatenate(
        [jnp.where(p, before, 0.0).sum(axis=0, keepdims=True) for p in picks], axis=0).astype(jnp.int32)
    carry_ref[...] += _dot(onehot, jnp.ones((tm, LANE), BF16))
    cnt_ref[...] = carry_ref[...]


def _route(logits, b_router):
    t = logits.shape[0]
    tm = next(n for n in (1024, 512) if t % n == 0)
    br = jnp.broadcast_to(b_router.astype(F32)[:, None], (N_EXPERTS, tm))
    tri = jnp.asarray(np.triu(np.ones((tm, tm), np.float32), 1), dtype=BF16)
    kt = lambda dt: jax.ShapeDtypeStruct((TOP_K, t), dt)
    eidx, gw, rank, cnt = pl.pallas_call(
        _router_kernel,
        grid=(t // tm,),
        in_specs=[pl.BlockSpec((tm, LANE), lambda i: (i, 0)),
                  pl.BlockSpec((N_EXPERTS, tm), lambda i: (0, 0)),
                  pl.BlockSpec((tm, tm), lambda i: (0, 0))],
        out_specs=[pl.BlockSpec((TOP_K, tm), lambda i: (0, i)),
                   pl.BlockSpec((TOP_K, tm), lambda i: (0, i)),
                   pl.BlockSpec((TOP_K, tm), lambda i: (0, i)),
                   pl.BlockSpec((N_EXPERTS, LANE), lambda i: (0, 0))],
        out_shape=[kt(jnp.int32), kt(F32), kt(jnp.int32),
                   jax.ShapeDtypeStruct((N_EXPERTS, LANE), F32)],
        scratch_shapes=[pltpu.VMEM((N_EXPERTS, LANE), F32)],
        compiler_params=_cparams(("arbitrary",)),
        name="router",
    )(logits, br, tri)
    return eidx, gw, rank, cnt[:, 0].astype(jnp.int32)


def _sc_mesh():
    return plsc.VectorSubcoreMesh(core_axis_name="core", subcore_axis_name="subcore")


def _dispatch_rows(xp, dest, slots):
    parts, t, _ = xp.shape

    @pl.kernel(out_type=jax.ShapeDtypeStruct((parts, slots, SC_ROW), xp.dtype), mesh=_sc_mesh(),
               scratch_types=[], name="moe_dispatch")
    def run(x_hbm, d_hbm, o_hbm):
        for part in range(parts):
            out_part = o_hbm.at[part]

            def body(x_vmem, d_vmem, out_part=out_part):
                for k in range(TOP_K):
                    pltpu.sync_copy(x_vmem, out_part.at[d_vmem.at[k]])

            pltpu.emit_pipeline(
                body,
                grid=(t // SC_WINDOW,),
                in_specs=[pl.BlockSpec((SC_WINDOW, SC_ROW), lambda i: (i, 0)),
                          pl.BlockSpec((TOP_K, SC_WINDOW), lambda i: (0, i))],
                out_specs=[],
                core_axis_name=("core", "subcore"),
                dimension_semantics=(pltpu.PARALLEL,),
            )(x_hbm.at[part], d_hbm)

    return run(xp, dest)


def _gather_rows(yp, dest, t0, n):
    parts = yp.shape[0]
    nwin = n // SC_WINDOW
    win0 = t0 // SC_WINDOW

    @pl.kernel(out_type=jax.ShapeDtypeStruct((parts, TOP_K * n, SC_ROW), yp.dtype), mesh=_sc_mesh(),
               scratch_types=[], name="moe_gather")
    def run(y_hbm, d_hbm, o_hbm):
        for part in range(parts):
            table = y_hbm.at[part]

            def body(d_vmem, o_vmem, table=table):
                pltpu.sync_copy(table.at[d_vmem.at[0]], o_vmem)

            pltpu.emit_pipeline(
                body,
                grid=(TOP_K, nwin),
                in_specs=[pl.BlockSpec((1, SC_WINDOW), lambda k, j: (k, win0 + j))],
                out_specs=[pl.BlockSpec((SC_WINDOW, SC_ROW), lambda k, j: (k * nwin + j, 0))],
                core_axis_name=("core", "subcore"),
                dimension_semantics=(pltpu.PARALLEL, pltpu.PARALLEL),
            )(d_hbm, o_hbm.at[part])

    return run(yp, dest).reshape(parts, TOP_K, n, SC_ROW)


def _expert_kernel(be_ref, bv_ref, bs_ref, nx_ref, sl_ref, xs_hbm, wg_hbm, wu_hbm, wd_hbm, o_ref,
                   x_buf, x_sems, wg_f, wu_f, wd_f, wg_s, wu_s, wd_s, sems, *, layer):
    i = pl.program_id(0)
    valid = bv_ref[i]
    expert = be_ref[i]
    new_expert = (i == 0) | (expert != be_ref[jnp.maximum(i - 1, 0)])
    slot = sl_ref[i]

    def fetch(which, into):
        return [pltpu.make_async_copy(src.at[layer, which], dst.at[into], sems.at[into, j])
                for j, (src, dst) in enumerate(((wg_hbm, wg_f), (wu_hbm, wu_f), (wd_hbm, wd_f)))]

    @pl.when(i == 0)
    def _():
        for cp in fetch(expert, slot):
            cp.start()

    @pl.when(new_expert)
    def _():
        for cp in fetch(expert, slot):
            cp.wait()
        upcoming = nx_ref[i]

        @pl.when(upcoming >= 0)
        def _():
            for cp in fetch(upcoming, 1 - slot):
                cp.start()

        wg_s[...] = wg_f[slot].astype(BF16)
        wu_s[...] = wu_f[slot].astype(BF16)
        wd_s[...] = wd_f[slot].astype(BF16)

    n_steps = pl.num_programs(0)
    ahead = X_RING - 1

    def x_copy(block, ring_slot):
        rows = pl.ds(pl.multiple_of(block * MOE_BLOCK, MOE_BLOCK), MOE_BLOCK)
        return pltpu.make_async_copy(xs_hbm.at[:, rows, :], x_buf.at[ring_slot], x_sems.at[ring_slot])

    def used(block):
        return (block < n_steps) & (bv_ref[jnp.minimum(block, n_steps - 1)] > 0)

    @pl.when(i == 0)
    def _():
        for j in range(ahead):
            pl.when(used(j))(lambda j=j: x_copy(j, j % X_RING).start())

    @pl.when(valid > 0)
    def _():
        x_copy(i, i % X_RING).wait()
        pl.when(used(i + ahead))(lambda: x_copy(i + ahead, (i + ahead) % X_RING).start())
        w = jnp.concatenate([x_buf[i % X_RING, part] for part in range(SC_PARTS)], axis=-1)
        row = lax.broadcasted_iota(jnp.int32, w.shape, 0)
        w = jnp.where(row < valid, w, jnp.uint32(0))
        lo, hi = _unpack_rows(w)
        x = jnp.concatenate([lo, hi], axis=1).astype(BF16)
        a = _dot(x, wg_s[...])
        hid = a * _sigmoid(a) * _dot(x, wu_s[...])
        _store_parts(o_ref, _pack_rows(_dot(hid.astype(BF16), wd_s[...])))


def _experts(xs, blk_e, blk_valid, blk_src, blk_next, blk_slot, layer, w_gate, w_up, w_down):
    parts, slots, _ = xs.shape
    d = D_MODEL
    nb = slots // MOE_BLOCK
    data = pl.BlockSpec((parts, MOE_BLOCK, SC_ROW), lambda i, be, bv, bs, nx, sl: (0, bs[i], 0))
    stage = lambda shape: pltpu.VMEM((2,) + shape, F32)
    return pl.pallas_call(
        functools.partial(_expert_kernel, layer=layer),
        grid_spec=pltpu.PrefetchScalarGridSpec(
            num_scalar_prefetch=5,
            grid=(nb,),
            in_specs=[pl.BlockSpec(memory_space=pl.ANY)] * 4,
            out_specs=data,
            scratch_shapes=[pltpu.VMEM((X_RING, parts, MOE_BLOCK, SC_ROW), U32), pltpu.SemaphoreType.DMA((X_RING,)),
                            stage((d, EXPERT_FF)), stage((d, EXPERT_FF)), stage((EXPERT_FF, d)),
                            pltpu.VMEM((d, EXPERT_FF), BF16), pltpu.VMEM((d, EXPERT_FF), BF16),
                            pltpu.VMEM((EXPERT_FF, d), BF16), pltpu.SemaphoreType.DMA((2, 3))]),
        out_shape=jax.ShapeDtypeStruct((parts, slots, SC_ROW), U32),
        compiler_params=_cparams(("arbitrary",)),
        name="experts",
    )(blk_e, blk_valid, blk_src, blk_next, blk_slot, xs, w_gate, w_up, w_down)


def _combine_kernel(yg_ref, gw_ref, h_ref, x_ref, gt_ref, wsg_ref, wsu_ref, wsd_ref, gf_ref, o_ref, *, final):
    h = h_ref[0]
    a = _dot(h, wsg_ref[...])
    hid = a * _sigmoid(a) * _dot(h, wsu_ref[...])
    y = _dot(hid.astype(BF16), wsd_ref[...])
    gw = gw_ref[...]
    y_lo = y[:, :D_MODEL // 2]
    y_hi = y[:, D_MODEL // 2:]
    for k in range(TOP_K):
        lo, hi = _unpack_rows(_load_parts(yg_ref, k))
        y_lo = y_lo + gw[:, k:k + 1] * lo
        y_hi = y_hi + gw[:, k:k + 1] * hi
    y = jnp.concatenate([y_lo, y_hi], axis=1)
    xn = x_ref[0] + gt_ref[0] * y
    if final:
        ms = jnp.mean(xn * xn, axis=-1, keepdims=True)
        xn = xn * lax.rsqrt(ms + RMS_EPS) * gf_ref[...]
    o_ref[0] = xn


def _combine(yg, gw, tok_off, h2, x, gt2, b0, nb, ws_gate, ws_up, ws_down, g_final, final, tm):
    b, s, d = x.shape
    per_batch = gt2.shape[0] == b
    mod_map = (lambda bi, i: (b0 + bi, 0, 0)) if per_batch else (lambda bi, i: (0, 0, 0))
    full = lambda arr: pl.BlockSpec(arr.shape, lambda bi, i: (0,) * arr.ndim)
    tok = lambda width: pl.BlockSpec((1, tm, width), lambda bi, i: (b0 + bi, i, 0))
    gf = g_final[None]
    nblk = s // tm
    blk0 = (tok_off + b0 * s) // tm
    return pl.pallas_call(
        functools.partial(_combine_kernel, final=final),
        grid=(nb, nblk),
        in_specs=[pl.BlockSpec((SC_PARTS, TOP_K, tm, SC_ROW), lambda bi, i: (0, 0, bi * nblk + i, 0)),
                  pl.BlockSpec((tm, TOP_K), lambda bi, i: (blk0 + bi * nblk + i, 0)),
                  tok(d), tok(d),
                  pl.BlockSpec((1, 1, d), mod_map),
                  full(ws_gate), full(ws_up), full(ws_down), full(gf)],
        out_specs=tok(d),
        out_shape=jax.ShapeDtypeStruct((b, s, d), F32),
        input_output_aliases={3: 0},
        compiler_params=_cparams(("parallel", "parallel")),
        name="combine",
    )(yg, gw, h2, x, gt2, ws_gate, ws_up, ws_down, gf)


def _project_latent(x, p, b0=0, nb=None, prev=None):
    w_main, w_gate = p['prep']['w_in']
    mods = p['prep']['mods']
    return _proj_in(x, p['g_norm1'], mods[0], mods[1], w_main, w_gate, tm=min(2048, x.shape[1]), tn=1024,
                    b0=b0, nb=nb, prev=prev)


def _layer(x, ctx_s, p, ctx_out, final, g_final, projected=None, after_piece=None, p_next=None):
    b, s, d = x.shape
    sc = ctx_s.shape[1]
    prep = p['prep']
    sh1, sc1, gt1, sh2, sc2, gt2 = prep['mods']
    csh1, csc1, cgt1, csh2, csc2, cgt2 = prep['mods_ctx']

    w_main, w_gate = prep['w_in']
    main, gate = _project_latent(x, p) if projected is None else projected
    ctx_flat = ctx_s.reshape(1, b * sc, d)
    n_ctx, tn_ctx = (N_MAIN, 1024) if ctx_out else (N_KV_MAIN, N_KV_MAIN // 2)
    main_c, gate_c = _proj_in(ctx_flat, p['g_norm1'], csh1, csc1, w_main, w_gate, tm=min(1024, b * sc),
                              tn=tn_ctx, n=n_ctx)
    main_c = main_c.reshape(b, sc, n_ctx)
    gate_c = gate_c.reshape(b, sc, LANE)

    o_na = _na_latent(main, main_c, prep['na_bias'])

    gate_ws = prep['gate_ws']
    s0 = jnp.zeros((b, GLA_HEADS, GLA_DK, GLA_DV), F32)
    o_cf, o_cb, st_f, st_b = _gla_scan(main_c, gate_c, gate_ws, (s0, s0), ctx_out)
    o_f, o_b, _, _ = _gla_scan(main, gate, gate_ws, (st_f, st_b), True)

    w_na, w_sc, w_gla, w_out, w_router = (prep[name] for name in ('w_na', 'w_sc', 'w_gla', 'w_out', 'w_router'))
    n_lat = b * s
    t = n_lat + (b * sc if ctx_out else 0)
    x, h2, hp_all, lg_all = _merge(o_na, p['conv_w'], o_f, o_b, main, x, gt1, p['gla_norm_g'], w_na, w_sc, w_gla,
                                   w_out, p['g_norm2'], sh2, sc2, w_router, tm=min(512, s), n_routed=t, tok_off=0)
    if ctx_out:
        o_na_c = _dense_attn(main_c)
        ctx_s, h2_c, hp_all, lg_all = _merge(o_na_c, p['conv_w'], o_cf, o_cb, main_c, ctx_s, cgt1, p['gla_norm_g'],
                                             w_na, w_sc, w_gla, w_out, p['g_norm2'], csh2, csc2, w_router,
                                             tm=min(256, sc), n_routed=t, tok_off=n_lat, routed=(hp_all, lg_all))

    eidx, gw, rank, counts = _route(lg_all, p['b_router'])
    padded = (counts + MOE_BLOCK - 1) // MOE_BLOCK * MOE_BLOCK
    pad_end = jnp.cumsum(padded)
    pad_start = pad_end - padded
    onehot = eidx[:, :, None] == jnp.arange(N_EXPERTS, dtype=jnp.int32)
    dest = jnp.sum(jnp.where(onehot, pad_start, 0), axis=-1) + rank
    n_blocks = -(-(t * TOP_K + N_EXPERTS * (MOE_BLOCK - 1)) // MOE_BLOCK)
    slots = n_blocks * MOE_BLOCK
    blk_start = jnp.arange(n_blocks, dtype=jnp.int32) * MOE_BLOCK
    blk_e = jnp.minimum(jnp.sum(pad_end[None, :] <= blk_start[:, None], axis=1), N_EXPERTS - 1).astype(jnp.int32)
    used_end = (pad_start + counts)[blk_e]
    blk_valid = jnp.clip(used_end - blk_start, 0, MOE_BLOCK).astype(jnp.int32)
    n_used = pad_end[-1] // MOE_BLOCK
    blk_src = jnp.minimum(jnp.arange(n_blocks, dtype=jnp.int32), n_used - 1)
    blk_e = blk_e[blk_src]
    ids = jnp.arange(N_EXPERTS, dtype=jnp.int32)
    used = counts > 0
    later_used = jnp.where(used[None, :] & (ids[None, :] > ids[:, None]), ids[None, :], N_EXPERTS).min(axis=1)
    next_used = jnp.where(later_used == N_EXPERTS, -1, later_used).astype(jnp.int32)
    blk_next = next_used[blk_e]
    blk_slot = ((jnp.cumsum(used) - 1) % 2).astype(jnp.int32)[blk_e]

    xs = _dispatch_rows(hp_all, dest, slots)
    if p_next is not None:
        xs, p_next['prep'] = lax.optimization_barrier((xs, p_next['prep']))
    ys = _experts(xs, blk_e, blk_valid, blk_src, blk_next, blk_slot, p['layer'], p['w_exp_gate'], p['w_exp_up'], p['w_exp_down'])
    gw_t = gw.T
    ws_gate, ws_up, ws_down = prep['ws_gate'], prep['ws_up'], prep['ws_down']

    def gathered(t0, n):
        return _gather_rows(ys, dest, t0, n)

    pieces = next(n for n in (4, 2, 1) if b % n == 0)
    nb = b // pieces
    for q in range(pieces):
        x = _combine(gathered(q * nb * s, nb * s), gw_t, 0, h2, x, gt2, q * nb, nb, ws_gate, ws_up, ws_down,
                     g_final, final, tm=min(512, s))
        if after_piece is not None:
            after_piece(x, q * nb, nb)
    if ctx_out:
        ctx_s = _combine(gathered(n_lat, b * sc), gw_t, n_lat, h2_c, ctx_s, cgt2, 0, b, ws_gate, ws_up, ws_down,
                         g_final, False, tm=min(256, sc))
    return x, ctx_s


def kernel(x, c, ctx, c_ctx, w_mod, b_mod, g_norm1, g_norm2, w_in, na_rpb, w_branch_na, conv_w, w_branch_sc,
           gla_gate_w, gla_gate_b, gla_norm_g, w_branch_gla, w_out, w_router, b_router, w_exp_gate, w_exp_up,
           w_exp_down, w_sh_gate, w_sh_up, w_sh_down, g_final):
    stacked = dict(g_norm1=g_norm1, g_norm2=g_norm2, na_rpb=na_rpb, w_branch_na=w_branch_na,
                   conv_w=conv_w, w_branch_sc=w_branch_sc, gla_gate_w=gla_gate_w, gla_gate_b=gla_gate_b,
                   gla_norm_g=gla_norm_g, w_branch_gla=w_branch_gla, w_out=w_out, w_router=w_router,
                   b_router=b_router,
                   w_sh_gate=w_sh_gate, w_sh_up=w_sh_up, w_sh_down=w_sh_down)
    depth = w_in.shape[0]
    rows = x.shape[1] // GRID_W
    layers = []
    for i in range(depth):
        p = {name: arr[i] for name, arr in stacked.items()}
        p.update(layer=i, w_exp_gate=w_exp_gate, w_exp_up=w_exp_up, w_exp_down=w_exp_down)
        mods, mods_ctx = _mod_vectors(c, c_ctx, w_mod, b_mod, i)
        p['prep'] = dict(
            w_in=_prep_w_in(w_in, i), mods=mods, mods_ctx=mods_ctx,
            na_bias=_na_bias_table(p['na_rpb'], rows, min(NA_WIN_R, rows)),
            gate_ws=_gla_gate_weights(p['gla_gate_w'], p['gla_gate_b']),
            w_na=p['w_branch_na'].astype(BF16), w_sc=p['w_branch_sc'].astype(BF16),
            w_gla=p['w_branch_gla'].astype(BF16), w_out=p['w_out'].astype(BF16),
            w_router=jnp.pad(p['w_router'], ((0, 0), (0, LANE - N_EXPERTS))).astype(BF16),
            ws_gate=p['w_sh_gate'].astype(BF16), ws_up=p['w_sh_up'].astype(BF16),
            ws_down=p['w_sh_down'].astype(BF16))
        layers.append(p)

    ctx_s = ctx
    projected = None
    for i, p in enumerate(layers):
        last = i == depth - 1
        p_next = None if last else layers[i + 1]
        after_piece = None
        next_projected = []
        if not last:
            def after_piece(xq, b0, nb, p_next=p_next, acc=next_projected):
                acc.append(_project_latent(xq, p_next, b0, nb, acc[-1] if acc else None))

        x, ctx_s = _layer(x, ctx_s, p, not last, last, g_final, projected, after_piece, p_next)
        projected = next_projected[-1] if next_projected else None
    return x
```

```python
import functools

import numpy as np
import jax
import jax.numpy as jnp
from jax import lax
from jax.experimental import pallas as pl
from jax.experimental.pallas import tpu as pltpu
from jax.experimental.pallas import tpu_sc as plsc

F32 = jnp.float32
BF16 = jnp.bfloat16
U32 = jnp.uint32

D_MODEL = 1024
N_MOD = 6
RMS_EPS = 1e-6
NEG_INF = -1e30
GRID_W = 64
NA_HEADS = 8
NA_HEAD_DIM = 64
NA_WIDTH = NA_HEADS * NA_HEAD_DIM
NA_WIN_R = 8
NA_WIN_C = 16
NA_GROUP = 4
SC_WIDTH = 512
GLA_HEADS = 4
GLA_KEY_WIDTH = 512
GLA_VAL_WIDTH = 1024
GLA_DK = GLA_KEY_WIDTH // GLA_HEADS
GLA_DV = GLA_VAL_WIDTH // GLA_HEADS
GLA_GATE_RANK = 16
GLA_GATE_TAU = 16.0
LOG2_E = 1.4426950408889634
N_EXPERTS = 64
N_EXPERT_GROUPS = 8
GROUP_SIZE = N_EXPERTS // N_EXPERT_GROUPS
TOPK_GROUPS = 4
TOP_K = 8
EXPERT_FF = 256
ROUTED_SCALE = 2.5
WEIGHT_DMA_PRIORITY = 1
X_RING = 4
MOE_BLOCK = 1024

LANE = 128
GLA_C = 128
GLA_STEP_CHUNKS = 4
GLA_LEVELS = tuple(GLA_C >> (i + 1) for i in range(GLA_C.bit_length() - 1))
VMEM_LIMIT = 48 * 1024 * 1024
SC_WINDOW = 128
SC_ROW = 256
SC_PARTS = D_MODEL // 2 // SC_ROW

OFF_V_GLA = 0
OFF_K_NA = 1024
OFF_V_NA = 1536
OFF_K_GLA = 2048
N_KV_MAIN = 2560
OFF_Q_NA = 2560
OFF_B_SC = 3072
OFF_C_SC = 3584
OFF_X_SC = 4096
OFF_Q_GLA = 4608
OFF_R_GLA = 5120
OFF_MERGE = 6144
N_MAIN = 9216


def _cparams(sem, vmem=VMEM_LIMIT):
    return pltpu.CompilerParams(dimension_semantics=sem, vmem_limit_bytes=vmem)


def _dot(a, b):
    return jnp.dot(a, b, preferred_element_type=F32)


def _dot_nt(a, b):
    return lax.dot_general(a, b, (((1,), (1,)), ((), ())), preferred_element_type=F32)


def _dot_tn(a, b):
    return lax.dot_general(a, b, (((0,), (0,)), ((), ())), preferred_element_type=F32)


def _sigmoid(x):
    return 0.5 * jnp.tanh(0.5 * x) + 0.5


def _pack_rows(x):
    n = x.shape[1] // 2
    r = x.astype(BF16).astype(F32)
    lo = pltpu.bitcast(r[:, :n], U32) >> 16
    hi = pltpu.bitcast(r[:, n:], U32)
    return hi | lo


def _store_parts(ref, words):
    for part in range(SC_PARTS):
        dst = ref.at[part, 0] if len(ref.shape) == 4 else ref.at[part]
        dst[...] = words[:, part * SC_ROW:(part + 1) * SC_ROW]


def _load_parts(ref, *lead):
    return jnp.concatenate([ref[(part,) + lead] for part in range(SC_PARTS)], axis=-1)


def _unpack_rows(w):
    lo = pltpu.bitcast(w << 16, F32)
    hi = pltpu.bitcast(w & jnp.uint32(0xFFFF0000), F32)
    return lo, hi


def _mod_kernel(a_ref, w_ref, b_ref, o_ref):
    a = a_ref[...]
    a = a * _sigmoid(a)
    o_ref[...] = _dot(a.astype(BF16), w_ref[0].astype(BF16)) + b_ref[0]


def _mod_vectors(c, c_ctx, w_mod, b_mod, layer):
    b = c.shape[0]
    rows = -(-(b + 1) // 8) * 8
    a = jnp.concatenate([c, c_ctx[None], jnp.zeros((rows - b - 1, D_MODEL), F32)], axis=0)
    n = N_MOD * D_MODEL
    tn = 1536
    out = pl.pallas_call(
        _mod_kernel,
        grid=(n // tn,),
        in_specs=[pl.BlockSpec((rows, D_MODEL), lambda j: (0, 0)),
                  pl.BlockSpec((1, D_MODEL, tn), lambda j: (layer, 0, j)),
                  pl.BlockSpec((1, 1, tn), lambda j: (layer, 0, j))],
        out_specs=pl.BlockSpec((rows, tn), lambda j: (0, j)),
        out_shape=jax.ShapeDtypeStruct((rows, n), F32),
        compiler_params=_cparams(("parallel",)),
        name="mod_vectors",
    )(a, w_mod, b_mod[:, None])
    lat = out[:b].reshape(b, N_MOD, 1, D_MODEL)
    ctx = out[b].reshape(N_MOD, 1, 1, D_MODEL)
    return [lat[:, i] for i in range(N_MOD)], [ctx[i] for i in range(N_MOD)]


def _proj_kernel(x_ref, g_ref, sh_ref, sc_ref, w_ref, wg_ref, *rest):
    o_ref, og_ref, h_ref = rest[-3:]

    @pl.when(pl.program_id(2) == 0)
    def _():
        x = x_ref[0]
        ms = jnp.mean(x * x, axis=-1, keepdims=True)
        h = x * lax.rsqrt(ms + RMS_EPS) * g_ref[...] * (1.0 + sc_ref[0]) + sh_ref[0]
        hb = h.astype(BF16)
        h_ref[...] = hb
        og_ref[0] = _dot(hb, wg_ref[...])

    o_ref[0] = _dot(h_ref[...], w_ref[...]).astype(o_ref.dtype)


W_IN_TILE = 512
W_IN_GATE_SHIFT = 2 * GLA_GATE_RANK


def _prep_w_in_kernel(a_ref, b_ref, o_ref, g_ref):
    t = pl.program_id(0)
    first_lat = N_KV_MAIN // W_IN_TILE
    a = a_ref[0]

    @pl.when(t < first_lat)
    def _():
        o_ref[...] = a.T.astype(BF16)

    @pl.when(t >= first_lat)
    def _():
        moved = jnp.concatenate([a[W_IN_GATE_SHIFT:], b_ref[0]], axis=0)
        scale = jnp.where(t == first_lat, NA_HEAD_DIM ** -0.5, 1.0)
        o_ref[...] = (moved * scale).T.astype(BF16)

    @pl.when(t == first_lat)
    def _():
        head = a[:LANE]
        row = lax.broadcasted_iota(jnp.int32, head.shape, 0)
        g_ref[...] = jnp.where(row < W_IN_GATE_SHIFT, head, 0.0).T.astype(BF16)


def _prep_w_in(w_in, layer):
    d = w_in.shape[1]
    w_t = jnp.swapaxes(w_in, 1, 2)
    first_lat = N_KV_MAIN // W_IN_TILE
    kv_perm = OFF_K_NA // W_IN_TILE

    def a_map(t):
        return (layer, jnp.where(t < first_lat, (t + first_lat - kv_perm) % first_lat, t), 0)

    def b_map(t):
        return (layer, jnp.where(t < first_lat, 0, (t + 1) * (W_IN_TILE // W_IN_GATE_SHIFT)), 0)

    return pl.pallas_call(
        _prep_w_in_kernel,
        grid=(N_MAIN // W_IN_TILE,),
        in_specs=[pl.BlockSpec((1, W_IN_TILE, d), a_map),
                  pl.BlockSpec((1, W_IN_GATE_SHIFT, d), b_map)],
        out_specs=[pl.BlockSpec((d, W_IN_TILE), lambda t: (0, t)),
                   pl.BlockSpec((d, LANE), lambda t: (0, 0))],
        out_shape=[jax.ShapeDtypeStruct((d, N_MAIN), BF16), jax.ShapeDtypeStruct((d, LANE), BF16)],
        compiler_params=_cparams(("arbitrary",)),
        name="prep_w_in",
    )(w_t, w_t)


def _proj_in(x, g, shift, scale, w_main, w_gate, tm, tn, n=None, b0=0, nb=None, prev=None):
    b, s, d = x.shape
    n = w_main.shape[1] if n is None else n
    nb = b if nb is None else nb
    per_batch = shift.shape[0] == b
    mod_map = (lambda bi, i, j: (b0 + bi, 0, 0)) if per_batch else (lambda bi, i, j: (0, 0, 0))
    operands = (x, g[None], shift, scale, w_main, w_gate)
    extend = () if prev is None else tuple(prev)
    return pl.pallas_call(
        _proj_kernel,
        grid=(nb, s // tm, n // tn),
        in_specs=[pl.BlockSpec((1, tm, d), lambda bi, i, j: (b0 + bi, i, 0)),
                  pl.BlockSpec((1, d), lambda bi, i, j: (0, 0)),
                  pl.BlockSpec((1, 1, d), mod_map),
                  pl.BlockSpec((1, 1, d), mod_map),
                  pl.BlockSpec((d, tn), lambda bi, i, j: (0, j)),
                  pl.BlockSpec((d, LANE), lambda bi, i, j: (0, 0))]
                 + [pl.BlockSpec(memory_space=pl.ANY)] * len(extend),
        out_specs=[pl.BlockSpec((1, tm, tn), lambda bi, i, j: (b0 + bi, i, j)),
                   pl.BlockSpec((1, tm, LANE), lambda bi, i, j: (b0 + bi, i, 0))],
        out_shape=[jax.ShapeDtypeStruct((b, s, n), BF16),
                   jax.ShapeDtypeStruct((b, s, LANE), F32)],
        scratch_shapes=[pltpu.VMEM((tm, d), BF16)],
        input_output_aliases={len(operands): 0, len(operands) + 1: 1} if extend else {},
        compiler_params=_cparams(("parallel", "parallel", "arbitrary")),
        name="proj_in",
    )(*operands, *extend)


def _softmax_av(q, keys, vals, biases):
    scores = []
    for kk, bb in zip(keys, biases):
        s = _dot_nt(q, kk)
        scores.append(s if bb is None else s + bb)
    m = scores[0].max(axis=-1, keepdims=True)
    for s in scores[1:]:
        m = jnp.maximum(m, s.max(axis=-1, keepdims=True))
    num = None
    den = None
    for s, vv in zip(scores, vals):
        e = jnp.exp(s - m)
        dsum = e.sum(axis=-1, keepdims=True)
        o = _dot(e.astype(BF16), vv)
        num = o if num is None else num + o
        den = dsum if den is None else den + dsum
    return num / den


def _na_kernel(q_ref, k_ref, v_ref, kc_ref, vc_ref, *rest, rows, kr):
    *bias_refs, o_ref = rest
    kc = kc_ref[0]
    vc = vc_ref[0]
    for j, bias_ref in enumerate(bias_refs):
        r = pl.program_id(1) * len(bias_refs) + j
        row_start = jnp.clip(r - kr // 2, 0, rows - kr)
        start = pl.multiple_of(row_start * GRID_W, GRID_W)
        n_win = kr * GRID_W
        q = q_ref[0, j * GRID_W:(j + 1) * GRID_W, :]
        kw = k_ref[0, pl.ds(start, n_win), :]
        vw = v_ref[0, pl.ds(start, n_win), :]
        o_ref[0, j * GRID_W:(j + 1) * GRID_W, :] = _na_row(q, kw, vw, kc, vc, bias_ref).astype(o_ref.dtype)


def _na_row(q, kw, vw, kc, vc, bias_ref):
    gw = NA_GROUP * NA_HEAD_DIM
    stacked = (NA_GROUP * GRID_W, gw)
    on_head = (lax.broadcasted_iota(jnp.int32, stacked, 0) // GRID_W
               == lax.broadcasted_iota(jnp.int32, stacked, 1) // NA_HEAD_DIM)
    outs = []
    for g in range(NA_HEADS // NA_GROUP):
        sl = slice(g * gw, (g + 1) * gw)
        q_all = jnp.where(on_head, jnp.concatenate([q[:, sl]] * NA_GROUP, axis=0), jnp.zeros((), q.dtype))
        bias = bias_ref[0, g * NA_GROUP * GRID_W:(g + 1) * NA_GROUP * GRID_W, :]
        o_all = _softmax_av(q_all, [kw[:, sl], kc[:, sl]], [vw[:, sl], vc[:, sl]], [bias, None])
        o_all = jnp.where(on_head, o_all, 0.0).reshape(NA_GROUP, GRID_W, gw)
        outs.append(o_all.sum(axis=0))
    return jnp.concatenate(outs, axis=-1)


def _na_bias_table(rpb, rows, kr):
    col = np.arange(GRID_W)
    col_start = np.clip(col - NA_WIN_C // 2, 0, GRID_W - NA_WIN_C)
    col_ok = (col[None, :] >= col_start[:, None]) & (col[None, :] < col_start[:, None] + NA_WIN_C)
    d_col = np.clip(col[None, :] - col[:, None], -(NA_WIN_C - 1), NA_WIN_C - 1) + NA_WIN_C - 1
    n_dr, n_dc = rpb.shape[1], rpb.shape[2]
    onehot = jnp.asarray((d_col.reshape(-1)[None, :] == np.arange(n_dc)[:, None]).astype(np.float32))
    by_col = jnp.dot(rpb.astype(F32).reshape(NA_HEADS * n_dr, n_dc), onehot, precision=lax.Precision.HIGHEST)
    by_col = by_col.reshape(NA_HEADS, n_dr, GRID_W, GRID_W)
    by_col = jnp.where(col_ok[None, None], by_col, NEG_INF)
    tables = []
    for o in range(kr):
        lo = NA_WIN_R - 1 - o
        tables.append(by_col[:, lo:lo + kr].transpose(0, 2, 1, 3).reshape(NA_HEADS, GRID_W, kr * GRID_W))
    return jnp.stack(tables).reshape(kr, NA_HEADS * GRID_W, kr * GRID_W)


def _na_latent(main, main_ctx, bias):
    b, s, _ = main.shape
    sc = main_ctx.shape[1]
    rows = s // GRID_W
    kr = min(NA_WIN_R, rows)
    w = NA_WIDTH

    per_step = next(n for n in (8, 4, 2, 1) if rows % n == 0)

    def bias_spec(j):
        def bias_map(bi, i):
            r = i * per_step + j
            return (r - jnp.clip(r - kr // 2, 0, rows - kr), 0, 0)
        return pl.BlockSpec((1, NA_HEADS * GRID_W, kr * GRID_W), bias_map)

    return pl.pallas_call(
        functools.partial(_na_kernel, rows=rows, kr=kr),
        grid=(b, rows // per_step),
        in_specs=[pl.BlockSpec((1, per_step * GRID_W, w), lambda bi, i: (bi, i, OFF_Q_NA // w)),
                  pl.BlockSpec((1, s, w), lambda bi, i: (bi, 0, OFF_K_NA // w)),
                  pl.BlockSpec((1, s, w), lambda bi, i: (bi, 0, OFF_V_NA // w)),
                  pl.BlockSpec((1, sc, w), lambda bi, i: (bi, 0, OFF_K_NA // w)),
                  pl.BlockSpec((1, sc, w), lambda bi, i: (bi, 0, OFF_V_NA // w))]
                 + [bias_spec(j) for j in range(per_step)],
        out_specs=pl.BlockSpec((1, per_step * GRID_W, w), lambda bi, i: (bi, i, 0)),
        out_shape=jax.ShapeDtypeStruct((b, s, w), BF16),
        compiler_params=_cparams(("parallel", "arbitrary")),
        name="na_latent",
    )(main, main, main, main_ctx, main_ctx, *([bias] * per_step))


def _dense_attn_kernel(q_ref, k_ref, v_ref, o_ref):
    q = q_ref[0]
    k = k_ref[0]
    v = v_ref[0]
    outs = []
    for h in range(NA_HEADS):
        sl = slice(h * NA_HEAD_DIM, (h + 1) * NA_HEAD_DIM)
        outs.append(_softmax_av(q[:, sl], [k[:, sl]], [v[:, sl]], [None]))
    o_ref[0] = jnp.concatenate(outs, axis=-1).astype(o_ref.dtype)


def _dense_attn(main_ctx):
    b, sc, _ = main_ctx.shape
    w = NA_WIDTH
    return pl.pallas_call(
        _dense_attn_kernel,
        grid=(b,),
        in_specs=[pl.BlockSpec((1, sc, w), lambda bi: (bi, 0, OFF_Q_NA // w)),
                  pl.BlockSpec((1, sc, w), lambda bi: (bi, 0, OFF_K_NA // w)),
                  pl.BlockSpec((1, sc, w), lambda bi: (bi, 0, OFF_V_NA // w))],
        out_specs=pl.BlockSpec((1, sc, w), lambda bi: (bi, 0, 0)),
        out_shape=jax.ShapeDtypeStruct((b, sc, w), BF16),
        compiler_params=_cparams(("parallel",)),
        name="ctx_attn",
    )(main_ctx, main_ctx, main_ctx)


HALO = 16


def _gla_matrices(reverse):
    c = GLA_C
    t = np.arange(c)[:, None]
    m = np.arange(c)[None, :]
    blocks = [m <= t, m > t]
    for b in GLA_LEVELS:
        first = (t // (2 * b)) * (2 * b) + b
        is_q = (t & b) != 0
        blocks.append(np.where(is_q, (m > first) & (m <= t), (m > t) & (m <= first)))
    mats = np.stack(blocks).astype(np.float32)
    if reverse:
        mats = mats[:, ::-1, ::-1]
    mats = mats.reshape(-1, c)
    return jnp.asarray(np.concatenate([mats, mats], axis=1), dtype=BF16)


def _split_bf16(x):
    hi = x.astype(BF16)
    return hi, (x - hi.astype(F32)).astype(BF16)


def _pair_block_diag(x):
    lane = lax.broadcasted_iota(jnp.int32, x.shape, 1)
    zero = jnp.zeros((), x.dtype)
    return jnp.concatenate([jnp.where(lane < GLA_DK, x, zero), jnp.where(lane >= GLA_DK, x, zero)], axis=0)


def _gla_kernel(*refs, emit):
    n_in = 9 if emit else 8
    n_out = 2 if emit else 1
    ins, outs, scratch = refs[:2 * n_in], refs[2 * n_in:2 * (n_in + n_out)], refs[2 * (n_in + n_out):]
    ins = [ins[d * n_in:(d + 1) * n_in] for d in range(2)]
    outs = [outs[d * n_out:(d + 1) * n_out] for d in range(2)]
    step = pl.program_id(1)

    @pl.when(step == 0)
    def _():
        for d in range(2):
            scratch[d][...] = ins[d][-1][0]

    pending = [_gla_direction(ins[d][:-1], outs[d][:-1], scratch[d], reverse, emit)
               for d, reverse in enumerate((False, True))]
    while pending:
        pending = [stages for stages in pending if next(stages, "done") != "done"]

    @pl.when(step == pl.num_programs(1) - 1)
    def _():
        for d in range(2):
            outs[d][-1][0] = scratch[d][...]


def _gla_direction(ins, outs, st_ref, reverse, emit):
    gate_ref = ins[-5]
    order = range(gate_ref.shape[1] // GLA_C)
    for sub in (reversed(order) if reverse else order):
        yield from _gla_chunk(ins, outs, st_ref, reverse, emit, slice(sub * GLA_C, (sub + 1) * GLA_C))


def _gla_chunk(ins, outs, st_ref, reverse, emit, rows):
    if emit:
        q_ref, k_ref, v_ref, gt_ref, w2a_ref, w2b_ref, b2_ref, a_ref = ins
        (o_ref,) = outs
    else:
        k_ref, v_ref, gt_ref, w2a_ref, w2b_ref, b2_ref, a_ref = ins
    c = GLA_C
    pw = 2 * GLA_DK

    lr_hi, lr_lo = _split_bf16(gt_ref[0, rows])
    logit = (_dot(jnp.concatenate([lr_hi, lr_lo], axis=1), w2a_ref[...]) + _dot(lr_hi, w2b_ref[...])
             + b2_ref[...])
    g = (jnp.minimum(logit, 0.0) - jnp.log1p(jnp.exp(-jnp.abs(logit)))) * (LOG2_E / GLA_GATE_TAU)
    g_hi, g_lo = _split_bf16(g)
    args = _dot(a_ref[...], jnp.concatenate([g_hi, g_lo], axis=0))
    cum = args[0:c]
    rem = args[c:2 * c]
    last_row = cum[0:1] if reverse else cum[c - 1:c]
    yield

    k = k_ref[0, rows].astype(F32)
    v = v_ref[0, rows]
    atts = []
    if emit:
        q = q_ref[0, rows].astype(F32) * (GLA_DK ** -0.5)
        row_t = lax.broadcasted_iota(jnp.int32, (c, pw), 0)
        si = lax.broadcasted_iota(jnp.int32, (2 * c, c), 0) & (c - 1)
        ti = lax.broadcasted_iota(jnp.int32, (2 * c, c), 1)
        if reverse:
            row_t, ti, si = c - 1 - row_t, c - 1 - ti, c - 1 - si
        for hp in range(GLA_HEADS // 2):
            cs = slice(hp * pw, (hp + 1) * pw)
            qp, kp = q[:, cs], k[:, cs]
            att = jnp.where(ti == si, _dot_nt(_pair_block_diag(kp.astype(BF16)), qp.astype(BF16)), 0.0)
            for l, b in enumerate(GLA_LEVELS):
                x = (jnp.exp2(args[(2 + l) * c:(3 + l) * c, cs])
                     * jnp.where((row_t & b) != 0, qp, kp)).astype(BF16)
                pair = (((ti ^ si) >> (b.bit_length() - 1)) == 1) & ((ti & b) != 0)
                att = jnp.where(pair, _dot_nt(_pair_block_diag(x), x), att)
                yield
            atts.append(att.astype(BF16))

    outs = []
    for h in range(GLA_HEADS):
        sl = slice(h * GLA_DK, (h + 1) * GLA_DK)
        kh = k[:, sl]
        vh = v[:, h * GLA_DV:(h + 1) * GLA_DV]
        state = st_ref[h]
        kd = (kh * jnp.exp2(rem[:, sl])).astype(BF16)
        decay = jnp.exp2(jnp.broadcast_to(last_row[:, sl], (GLA_DK, GLA_DK))).T
        decay = jnp.concatenate([decay] * (GLA_DV // GLA_DK), axis=1)
        if emit:
            qd = (q[:, sl] * jnp.exp2(cum[:, sl])).astype(BF16)
            att_t = atts[h // 2][(h % 2) * c:(h % 2 + 1) * c]
            both = _dot_tn(jnp.concatenate([att_t, kd], axis=1), vh)
            outs.append(_dot(qd, state.astype(BF16)) + both[:c])
            st_ref[h] = decay * state + both[c:]
        else:
            st_ref[h] = decay * state + _dot_tn(kd, vh)
        yield

    if emit:
        o_ref[0, rows] = jnp.concatenate(outs, axis=-1).astype(o_ref.dtype)


def _gla_scan(main, gate, gate_ws, s0s, emit):
    b, l, _ = main.shape
    step_rows = GLA_C * min(GLA_STEP_CHUNKS, l // GLA_C)
    n = l // step_rows
    const = lambda arr: pl.BlockSpec(arr.shape, lambda bi, s: (0,) * arr.ndim)
    state_spec = pl.BlockSpec((1, GLA_HEADS, GLA_DK, GLA_DV), lambda bi, s: (bi, 0, 0, 0))
    state_shape = jax.ShapeDtypeStruct((b, GLA_HEADS, GLA_DK, GLA_DV), F32)
    in_specs, args, out_specs, out_shape = [], [], [], []
    for reverse in (False, True):
        amat = _gla_matrices(reverse)
        w2a, w2b, b2 = gate_ws[reverse]

        def col(block, reverse=reverse):
            return lambda bi, s: (bi, n - 1 - s if reverse else s, block)

        if emit:
            in_specs.append(pl.BlockSpec((1, step_rows, GLA_KEY_WIDTH), col(OFF_Q_GLA // GLA_KEY_WIDTH)))
            args.append(main)
            out_specs.append(pl.BlockSpec((1, step_rows, GLA_VAL_WIDTH), col(0)))
            out_shape.append(jax.ShapeDtypeStruct((b, l, GLA_VAL_WIDTH), BF16))
        in_specs += [pl.BlockSpec((1, step_rows, GLA_KEY_WIDTH), col(OFF_K_GLA // GLA_KEY_WIDTH)),
                     pl.BlockSpec((1, step_rows, GLA_VAL_WIDTH), col(OFF_V_GLA // GLA_VAL_WIDTH)),
                     pl.BlockSpec((1, step_rows, LANE), col(0)),
                     const(w2a), const(w2b), const(b2), const(amat), state_spec]
        args += [main, main, gate, w2a, w2b, b2, amat, s0s[reverse]]
        out_specs.append(state_spec)
        out_shape.append(state_shape)
    res = pl.pallas_call(
        functools.partial(_gla_kernel, emit=emit),
        grid=(b, n),
        in_specs=in_specs,
        out_specs=out_specs,
        out_shape=out_shape,
        scratch_shapes=[pltpu.VMEM((GLA_HEADS, GLA_DK, GLA_DV), F32)] * 2,
        compiler_params=_cparams(("parallel", "arbitrary")),
        name="gla_scan",
    )(*args)
    return (res[0], res[2], res[1], res[3]) if emit else (None, None, res[0], res[1])


def _gla_gate_weights(gate_w, gate_b):
    out = []
    for dr in range(2):
        w = jnp.zeros((LANE, GLA_KEY_WIDTH), F32)
        w = w.at[dr * GLA_GATE_RANK:(dr + 1) * GLA_GATE_RANK].set(gate_w[dr])
        w_hi = w.astype(BF16)
        w_lo = (w - w_hi.astype(F32)).astype(BF16)
        out.append((jnp.concatenate([w_hi, w_hi], axis=0), w_lo, gate_b[dr][None]))
    return out


def _conv_tile(b_ref, c_ref, x_ref, cp_ref, xp_ref, cn_ref, xn_ref, w_ref):
    i, n = pl.program_id(1), pl.num_programs(1)
    u = c_ref[0].astype(F32) * x_ref[0].astype(F32)
    tm = u.shape[0]
    before = jnp.where(i > 0, 1.0, 0.0) * (cp_ref[0, HALO - 1:HALO].astype(F32) * xp_ref[0, HALO - 1:HALO].astype(F32))
    after = jnp.where(i < n - 1, 1.0, 0.0) * (cn_ref[0, 0:1].astype(F32) * xn_ref[0, 0:1].astype(F32))
    t = lax.broadcasted_iota(jnp.int32, u.shape, 0)
    prev = jnp.where(t == 0, before, pltpu.roll(u, 1, axis=0))
    nxt = jnp.where(t == tm - 1, after, pltpu.roll(u, tm - 1, axis=0))
    w = w_ref[...]
    return b_ref[0].astype(F32) * (prev * w[0:1] + u * w[1:2] + nxt * w[2:3])


def _merge_kernel(ona_ref, bsc_ref, csc_ref, xsc_ref, cp_ref, xp_ref, cn_ref, xn_ref, cw_ref, of_ref, ob_ref,
                  r_ref, gna_ref, gsc_ref, ggl_ref, x_ref, gt_ref,
                  gn_ref, wna_ref, wsc_ref, wgl_ref, wo_ref, g2_ref, sh2_ref, sc2_ref, wr_ref, *rest):
    xo_ref, h2_ref, hp_ref, lg_ref = rest[-4:]
    o_sc = _conv_tile(bsc_ref, csc_ref, xsc_ref, cp_ref, xp_ref, cn_ref, xn_ref, cw_ref).astype(BF16)
    o = of_ref[0].astype(F32) + ob_ref[0].astype(F32)
    normed = []
    for h in range(GLA_HEADS):
        oh = o[:, h * GLA_DV:(h + 1) * GLA_DV]
        ms = jnp.mean(oh * oh, axis=-1, keepdims=True)
        normed.append(oh * lax.rsqrt(ms + RMS_EPS))
    r = r_ref[0].astype(F32)
    y_gla = jnp.concatenate(normed, axis=-1) * gn_ref[...] * (r * _sigmoid(r))
    y = (_sigmoid(gna_ref[0].astype(F32)) * _dot(ona_ref[0], wna_ref[...])
         + _sigmoid(gsc_ref[0].astype(F32)) * _dot(o_sc, wsc_ref[...])
         + _sigmoid(ggl_ref[0].astype(F32)) * _dot(y_gla.astype(BF16), wgl_ref[...]))
    xn = x_ref[0] + gt_ref[0] * _dot(y.astype(BF16), wo_ref[...])
    xo_ref[0] = xn
    ms = jnp.mean(xn * xn, axis=-1, keepdims=True)
    h2 = xn * lax.rsqrt(ms + RMS_EPS) * g2_ref[...] * (1.0 + sc2_ref[0]) + sh2_ref[0]
    h2b = h2.astype(BF16)
    h2_ref[0] = h2b
    _store_parts(hp_ref, _pack_rows(h2))
    lg_ref[...] = _dot(h2b, wr_ref[...])


def _merge(o_na, conv_w, o_f, o_b, main, x, gt1, gn, w_na, w_sc, w_gla, w_out, g2, sh2, sc2, w_router, tm,
           n_routed, tok_off, routed=None):
    b, s, d = x.shape
    per_batch = gt1.shape[0] == b
    mod_map = (lambda bi, i: (bi, 0, 0)) if per_batch else (lambda bi, i: (0, 0, 0))
    tok = lambda width, blk: pl.BlockSpec((1, tm, width), lambda bi, i: (bi, i, blk))
    full = lambda arr: pl.BlockSpec(arr.shape, lambda bi, i: (0,) * arr.ndim)
    mod = pl.BlockSpec((1, 1, d), mod_map)
    gn_t = jnp.tile(gn, GLA_HEADS)[None]
    g2_t = g2[None]
    per_tile = tm // HALO
    last_halo = s // HALO - 1
    halo_prev = lambda blk: pl.BlockSpec(
        (1, HALO, SC_WIDTH), lambda bi, i: (bi, jnp.maximum(i * per_tile - 1, 0), blk))
    halo_next = lambda blk: pl.BlockSpec(
        (1, HALO, SC_WIDTH), lambda bi, i: (bi, jnp.minimum((i + 1) * per_tile, last_halo), blk))
    col_b, col_c, col_x = OFF_B_SC // SC_WIDTH, OFF_C_SC // SC_WIDTH, OFF_X_SC // SC_WIDTH
    extend = () if routed is None else tuple(routed)
    operands = (o_na, main, main, main, main, main, main, main, conv_w, o_f, o_b, main, main, main, main, x, gt1,
                gn_t, w_na, w_sc, w_gla, w_out, g2_t, sh2, sc2, w_router)
    n_in = len(operands)
    nblk = s // tm
    blk0 = tok_off // tm
    return pl.pallas_call(
        _merge_kernel,
        grid=(b, s // tm),
        in_specs=[tok(NA_WIDTH, 0), tok(SC_WIDTH, col_b), tok(SC_WIDTH, col_c), tok(SC_WIDTH, col_x),
                  halo_prev(col_c), halo_prev(col_x), halo_next(col_c), halo_next(col_x), full(conv_w),
                  tok(GLA_VAL_WIDTH, 0), tok(GLA_VAL_WIDTH, 0),
                  tok(d, OFF_R_GLA // d), tok(d, OFF_MERGE // d), tok(d, OFF_MERGE // d + 1),
                  tok(d, OFF_MERGE // d + 2), tok(d, 0), mod,
                  full(gn_t), full(w_na), full(w_sc), full(w_gla), full(w_out), full(g2_t), mod, mod,
                  full(w_router)] + [pl.BlockSpec(memory_space=pl.ANY)] * len(extend),
        out_specs=[tok(d, 0), tok(d, 0),
                   pl.BlockSpec((SC_PARTS, tm, SC_ROW), lambda bi, i: (0, blk0 + bi * nblk + i, 0)),
                   pl.BlockSpec((tm, LANE), lambda bi, i: (blk0 + bi * nblk + i, 0))],
        out_shape=[jax.ShapeDtypeStruct((b, s, d), F32),
                   jax.ShapeDtypeStruct((b, s, d), BF16),
                   jax.ShapeDtypeStruct((SC_PARTS, n_routed, SC_ROW), U32),
                   jax.ShapeDtypeStruct((n_routed, LANE), F32)],
        input_output_aliases={n_in: 2, n_in + 1: 3} if extend else {},
        compiler_params=_cparams(("parallel", "parallel")),
        name="merge",
    )(*operands, *extend)


def _router_kernel(lg_ref, br_ref, tri_ref, eidx_ref, gw_ref, rank_ref, cnt_ref, carry_ref):
    tm = lg_ref.shape[0]

    @pl.when(pl.program_id(0) == 0)
    def _():
        carry_ref[...] = jnp.zeros_like(carry_ref)

    scores = _sigmoid(lg_ref[...].T[:N_EXPERTS])
    sel = scores + br_ref[...]
    neg = -jnp.inf

    sel3 = sel.reshape(N_EXPERT_GROUPS, GROUP_SIZE, tm)
    i3 = lax.broadcasted_iota(jnp.int32, sel3.shape, 1)
    m1 = sel3.max(axis=1, keepdims=True)
    first = jnp.where(sel3 == m1, i3, GROUP_SIZE).min(axis=1, keepdims=True)
    m2 = jnp.where(i3 == first, neg, sel3).max(axis=1, keepdims=True)
    gscore = (m1 + m2)[:, 0, :]

    gi = lax.broadcasted_iota(jnp.int32, gscore.shape, 0)
    gmask = jnp.zeros(gscore.shape, jnp.bool_)
    for _ in range(TOPK_GROUPS):
        m = gscore.max(axis=0, keepdims=True)
        pick = gi == jnp.where(gscore == m, gi, N_EXPERT_GROUPS).min(axis=0, keepdims=True)
        gmask = gmask | pick
        gscore = jnp.where(pick, neg, gscore)
    emask = jnp.broadcast_to(gmask[:, None, :], sel3.shape).reshape(N_EXPERTS, tm)
    sel = jnp.where(emask, sel, neg)

    ei = lax.broadcasted_iota(jnp.int32, sel.shape, 0)
    picks, idxs, ws = [], [], []
    for _ in range(TOP_K):
        m = sel.max(axis=0, keepdims=True)
        idx = jnp.where(sel == m, ei, N_EXPERTS).min(axis=0, keepdims=True)
        pick = ei == idx
        picks.append(pick)
        idxs.append(idx)
        ws.append(jnp.where(pick, scores, 0.0).sum(axis=0, keepdims=True))
        sel = jnp.where(pick, neg, sel)
    w = jnp.concatenate(ws, axis=0)
    gw_ref[...] = w / w.sum(axis=0, keepdims=True) * ROUTED_SCALE
    eidx_ref[...] = jnp.concatenate(idxs, axis=0)

    onehot = picks[0]
    for p in picks[1:]:
        onehot = onehot | p
    onehot = jnp.where(onehot, 1.0, 0.0).astype(BF16)
    before = _dot(onehot, tri_ref[...]) + jnp.tile(carry_ref[...], (1, tm // LANE))
    rank_ref[...] = jnp.concatenate(
        [jnp.where(p, before, 0.0).sum(axis=0, keepdims=True) for p in picks], axis=0).astype(jnp.int32)
    carry_ref[...] += _dot(onehot, jnp.ones((tm, LANE), BF16))
    cnt_ref[...] = carry_ref[...]


def _route(logits, b_router):
    t = logits.shape[0]
    tm = next(n for n in (1024, 512) if t % n == 0)
    br = jnp.broadcast_to(b_router.astype(F32)[:, None], (N_EXPERTS, tm))
    tri = jnp.asarray(np.triu(np.ones((tm, tm), np.float32), 1), dtype=BF16)
    kt = lambda dt: jax.ShapeDtypeStruct((TOP_K, t), dt)
    eidx, gw, rank, cnt = pl.pallas_call(
        _router_kernel,
        grid=(t // tm,),
        in_specs=[pl.BlockSpec((tm, LANE), lambda i: (i, 0)),
                  pl.BlockSpec((N_EXPERTS, tm), lambda i: (0, 0)),
                  pl.BlockSpec((tm, tm), lambda i: (0, 0))],
        out_specs=[pl.BlockSpec((TOP_K, tm), lambda i: (0, i)),
                   pl.BlockSpec((TOP_K, tm), lambda i: (0, i)),
                   pl.BlockSpec((TOP_K, tm), lambda i: (0, i)),
                   pl.BlockSpec((N_EXPERTS, LANE), lambda i: (0, 0))],
        out_shape=[kt(jnp.int32), kt(F32), kt(jnp.int32),
                   jax.ShapeDtypeStruct((N_EXPERTS, LANE), F32)],
        scratch_shapes=[pltpu.VMEM((N_EXPERTS, LANE), F32)],
        compiler_params=_cparams(("arbitrary",)),
        name="router",
    )(logits, br, tri)
    return eidx, gw, rank, cnt[:, 0].astype(jnp.int32)


def _sc_mesh():
    return plsc.VectorSubcoreMesh(core_axis_name="core", subcore_axis_name="subcore")


def _dispatch_rows(xp, dest, slots):
    parts, t, _ = xp.shape

    @pl.kernel(out_type=jax.ShapeDtypeStruct((parts, slots, SC_ROW), xp.dtype), mesh=_sc_mesh(),
               scratch_types=[], name="moe_dispatch")
    def run(x_hbm, d_hbm, o_hbm):
        for part in range(parts):
            out_part = o_hbm.at[part]

            def body(x_vmem, d_vmem, out_part=out_part):
                for k in range(TOP_K):
                    pltpu.sync_copy(x_vmem, out_part.at[d_vmem.at[k]])

            pltpu.emit_pipeline(
                body,
                grid=(t // SC_WINDOW,),
                in_specs=[pl.BlockSpec((SC_WINDOW, SC_ROW), lambda i: (i, 0)),
                          pl.BlockSpec((TOP_K, SC_WINDOW), lambda i: (0, i))],
                out_specs=[],
                core_axis_name=("core", "subcore"),
                dimension_semantics=(pltpu.PARALLEL,),
            )(x_hbm.at[part], d_hbm)

    return run(xp, dest)


def _gather_rows(yp, dest, t0, n):
    parts = yp.shape[0]
    nwin = n // SC_WINDOW
    win0 = t0 // SC_WINDOW

    @pl.kernel(out_type=jax.ShapeDtypeStruct((parts, TOP_K * n, SC_ROW), yp.dtype), mesh=_sc_mesh(),
               scratch_types=[], name="moe_gather")
    def run(y_hbm, d_hbm, o_hbm):
        for part in range(parts):
            table = y_hbm.at[part]

            def body(d_vmem, o_vmem, table=table):
                pltpu.sync_copy(table.at[d_vmem.at[0]], o_vmem)

            pltpu.emit_pipeline(
                body,
                grid=(TOP_K, nwin),
                in_specs=[pl.BlockSpec((1, SC_WINDOW), lambda k, j: (k, win0 + j))],
                out_specs=[pl.BlockSpec((SC_WINDOW, SC_ROW), lambda k, j: (k * nwin + j, 0))],
                core_axis_name=("core", "subcore"),
                dimension_semantics=(pltpu.PARALLEL, pltpu.PARALLEL),
            )(d_hbm, o_hbm.at[part])

    return run(yp, dest).reshape(parts, TOP_K, n, SC_ROW)


def _expert_kernel(be_ref, bv_ref, bs_ref, nx_ref, sl_ref, xs_hbm, wg_hbm, wu_hbm, wd_hbm, o_ref,
                   x_buf, x_sems, wg_f, wu_f, wd_f, wg_s, wu_s, wd_s, sems, *, layer):
    i = pl.program_id(0)
    valid = bv_ref[i]
    expert = be_ref[i]
    new_expert = (i == 0) | (expert != be_ref[jnp.maximum(i - 1, 0)])
    slot = sl_ref[i]

    def fetch(which, into):
        return [pltpu.make_async_copy(src.at[layer, which], dst.at[into], sems.at[into, j])
                for j, (src, dst) in enumerate(((wg_hbm, wg_f), (wu_hbm, wu_f), (wd_hbm, wd_f)))]

    @pl.when(i == 0)
    def _():
        for cp in fetch(expert, slot):
            cp.start(priority=WEIGHT_DMA_PRIORITY)

    @pl.when(new_expert)
    def _():
        for cp in fetch(expert, slot):
            cp.wait()
        upcoming = nx_ref[i]

        @pl.when(upcoming >= 0)
        def _():
            for cp in fetch(upcoming, 1 - slot):
                cp.start(priority=WEIGHT_DMA_PRIORITY)

        wg_s[...] = wg_f[slot].astype(BF16)
        wu_s[...] = wu_f[slot].astype(BF16)
        wd_s[...] = wd_f[slot].astype(BF16)

    n_steps = pl.num_programs(0)
    ahead = X_RING - 1

    def x_copy(block, ring_slot):
        rows = pl.ds(pl.multiple_of(block * MOE_BLOCK, MOE_BLOCK), MOE_BLOCK)
        return pltpu.make_async_copy(xs_hbm.at[:, rows, :], x_buf.at[ring_slot], x_sems.at[ring_slot])

    def used(block):
        return (block < n_steps) & (bv_ref[jnp.minimum(block, n_steps - 1)] > 0)

    @pl.when(i == 0)
    def _():
        for j in range(ahead):
            pl.when(used(j))(lambda j=j: x_copy(j, j % X_RING).start())

    @pl.when(valid > 0)
    def _():
        x_copy(i, i % X_RING).wait()
        pl.when(used(i + ahead))(lambda: x_copy(i + ahead, (i + ahead) % X_RING).start())
        w = jnp.concatenate([x_buf[i % X_RING, part] for part in range(SC_PARTS)], axis=-1)
        row = lax.broadcasted_iota(jnp.int32, w.shape, 0)
        w = jnp.where(row < valid, w, jnp.uint32(0))
        lo, hi = _unpack_rows(w)
        x = jnp.concatenate([lo, hi], axis=1).astype(BF16)
        a = _dot(x, wg_s[...])
        hid = a * _sigmoid(a) * _dot(x, wu_s[...])
        _store_parts(o_ref, _pack_rows(_dot(hid.astype(BF16), wd_s[...])))


def _experts(xs, blk_e, blk_valid, blk_src, blk_next, blk_slot, layer, w_gate, w_up, w_down):
    parts, slots, _ = xs.shape
    d = D_MODEL
    nb = slots // MOE_BLOCK
    data = pl.BlockSpec((parts, MOE_BLOCK, SC_ROW), lambda i, be, bv, bs, nx, sl: (0, bs[i], 0))
    stage = lambda shape: pltpu.VMEM((2,) + shape, F32)
    return pl.pallas_call(
        functools.partial(_expert_kernel, layer=layer),
        grid_spec=pltpu.PrefetchScalarGridSpec(
            num_scalar_prefetch=5,
            grid=(nb,),
            in_specs=[pl.BlockSpec(memory_space=pl.ANY)] * 4,
            out_specs=data,
            scratch_shapes=[pltpu.VMEM((X_RING, parts, MOE_BLOCK, SC_ROW), U32), pltpu.SemaphoreType.DMA((X_RING,)),
                            stage((d, EXPERT_FF)), stage((d, EXPERT_FF)), stage((EXPERT_FF, d)),
                            pltpu.VMEM((d, EXPERT_FF), BF16), pltpu.VMEM((d, EXPERT_FF), BF16),
                            pltpu.VMEM((EXPERT_FF, d), BF16), pltpu.SemaphoreType.DMA((2, 3))]),
        out_shape=jax.ShapeDtypeStruct((parts, slots, SC_ROW), U32),
        compiler_params=_cparams(("arbitrary",)),
        name="experts",
    )(blk_e, blk_valid, blk_src, blk_next, blk_slot, xs, w_gate, w_up, w_down)


def _combine_kernel(yg_ref, gw_ref, h_ref, x_ref, gt_ref, wsg_ref, wsu_ref, wsd_ref, gf_ref, o_ref, *, final):
    h = h_ref[0]
    a = _dot(h, wsg_ref[...])
    hid = a * _sigmoid(a) * _dot(h, wsu_ref[...])
    y = _dot(hid.astype(BF16), wsd_ref[...])
    gw = gw_ref[...]
    y_lo = y[:, :D_MODEL // 2]
    y_hi = y[:, D_MODEL // 2:]
    for k in range(TOP_K):
        lo, hi = _unpack_rows(_load_parts(yg_ref, k))
        y_lo = y_lo + gw[:, k:k + 1] * lo
        y_hi = y_hi + gw[:, k:k + 1] * hi
    y = jnp.concatenate([y_lo, y_hi], axis=1)
    xn = x_ref[0] + gt_ref[0] * y
    if final:
        ms = jnp.mean(xn * xn, axis=-1, keepdims=True)
        xn = xn * lax.rsqrt(ms + RMS_EPS) * gf_ref[...]
    o_ref[0] = xn


def _combine(yg, gw, tok_off, h2, x, gt2, b0, nb, ws_gate, ws_up, ws_down, g_final, final, tm):
    b, s, d = x.shape
    per_batch = gt2.shape[0] == b
    mod_map = (lambda bi, i: (b0 + bi, 0, 0)) if per_batch else (lambda bi, i: (0, 0, 0))
    full = lambda arr: pl.BlockSpec(arr.shape, lambda bi, i: (0,) * arr.ndim)
    tok = lambda width: pl.BlockSpec((1, tm, width), lambda bi, i: (b0 + bi, i, 0))
    gf = g_final[None]
    nblk = s // tm
    blk0 = (tok_off + b0 * s) // tm
    return pl.pallas_call(
        functools.partial(_combine_kernel, final=final),
        grid=(nb, nblk),
        in_specs=[pl.BlockSpec((SC_PARTS, TOP_K, tm, SC_ROW), lambda bi, i: (0, 0, bi * nblk + i, 0)),
                  pl.BlockSpec((tm, TOP_K), lambda bi, i: (blk0 + bi * nblk + i, 0)),
                  tok(d), tok(d),
                  pl.BlockSpec((1, 1, d), mod_map),
                  full(ws_gate), full(ws_up), full(ws_down), full(gf)],
        out_specs=tok(d),
        out_shape=jax.ShapeDtypeStruct((b, s, d), F32),
        input_output_aliases={3: 0},
        compiler_params=_cparams(("parallel", "parallel")),
        name="combine",
    )(yg, gw, h2, x, gt2, ws_gate, ws_up, ws_down, gf)


def _project_latent(x, p, b0=0, nb=None, prev=None):
    w_main, w_gate = p['prep']['w_in']
    mods = p['prep']['mods']
    return _proj_in(x, p['g_norm1'], mods[0], mods[1], w_main, w_gate, tm=min(2048, x.shape[1]), tn=1024,
                    b0=b0, nb=nb, prev=prev)


def _layer(x, ctx_s, p, ctx_out, final, g_final, projected=None, after_piece=None, p_next=None):
    b, s, d = x.shape
    sc = ctx_s.shape[1]
    prep = p['prep']
    sh1, sc1, gt1, sh2, sc2, gt2 = prep['mods']
    csh1, csc1, cgt1, csh2, csc2, cgt2 = prep['mods_ctx']

    w_main, w_gate = prep['w_in']
    main, gate = _project_latent(x, p) if projected is None else projected
    ctx_flat = ctx_s.reshape(1, b * sc, d)
    n_ctx, tn_ctx = (N_MAIN, 1024) if ctx_out else (N_KV_MAIN, N_KV_MAIN // 2)
    main_c, gate_c = _proj_in(ctx_flat, p['g_norm1'], csh1, csc1, w_main, w_gate, tm=min(1024, b * sc),
                              tn=tn_ctx, n=n_ctx)
    main_c = main_c.reshape(b, sc, n_ctx)
    gate_c = gate_c.reshape(b, sc, LANE)

    o_na = _na_latent(main, main_c, prep['na_bias'])

    gate_ws = prep['gate_ws']
    s0 = jnp.zeros((b, GLA_HEADS, GLA_DK, GLA_DV), F32)
    o_cf, o_cb, st_f, st_b = _gla_scan(main_c, gate_c, gate_ws, (s0, s0), ctx_out)
    o_f, o_b, _, _ = _gla_scan(main, gate, gate_ws, (st_f, st_b), True)

    w_na, w_sc, w_gla, w_out, w_router = (prep[name] for name in ('w_na', 'w_sc', 'w_gla', 'w_out', 'w_router'))
    n_lat = b * s
    t = n_lat + (b * sc if ctx_out else 0)
    x, h2, hp_all, lg_all = _merge(o_na, p['conv_w'], o_f, o_b, main, x, gt1, p['gla_norm_g'], w_na, w_sc, w_gla,
                                   w_out, p['g_norm2'], sh2, sc2, w_router, tm=min(512, s), n_routed=t, tok_off=0)
    if ctx_out:
        o_na_c = _dense_attn(main_c)
        ctx_s, h2_c, hp_all, lg_all = _merge(o_na_c, p['conv_w'], o_cf, o_cb, main_c, ctx_s, cgt1, p['gla_norm_g'],
                                             w_na, w_sc, w_gla, w_out, p['g_norm2'], csh2, csc2, w_router,
                                             tm=min(256, sc), n_routed=t, tok_off=n_lat, routed=(hp_all, lg_all))

    eidx, gw, rank, counts = _route(lg_all, p['b_router'])
    padded = (counts + MOE_BLOCK - 1) // MOE_BLOCK * MOE_BLOCK
    pad_end = jnp.cumsum(padded)
    pad_start = pad_end - padded
    onehot = eidx[:, :, None] == jnp.arange(N_EXPERTS, dtype=jnp.int32)
    dest = jnp.sum(jnp.where(onehot, pad_start, 0), axis=-1) + rank
    n_blocks = -(-(t * TOP_K + N_EXPERTS * (MOE_BLOCK - 1)) // MOE_BLOCK)
    slots = n_blocks * MOE_BLOCK
    blk_start = jnp.arange(n_blocks, dtype=jnp.int32) * MOE_BLOCK
    blk_e = jnp.minimum(jnp.sum(pad_end[None, :] <= blk_start[:, None], axis=1), N_EXPERTS - 1).astype(jnp.int32)
    used_end = (pad_start + counts)[blk_e]
    blk_valid = jnp.clip(used_end - blk_start, 0, MOE_BLOCK).astype(jnp.int32)
    n_used = pad_end[-1] // MOE_BLOCK
    blk_src = jnp.minimum(jnp.arange(n_blocks, dtype=jnp.int32), n_used - 1)
    blk_e = blk_e[blk_src]
    ids = jnp.arange(N_EXPERTS, dtype=jnp.int32)
    used = counts > 0
    later_used = jnp.where(used[None, :] & (ids[None, :] > ids[:, None]), ids[None, :], N_EXPERTS).min(axis=1)
    next_used = jnp.where(later_used == N_EXPERTS, -1, later_used).astype(jnp.int32)
    blk_next = next_used[blk_e]
    blk_slot = ((jnp.cumsum(used) - 1) % 2).astype(jnp.int32)[blk_e]

    xs = _dispatch_rows(hp_all, dest, slots)
    if p_next is not None:
        xs, p_next['prep'] = lax.optimization_barrier((xs, p_next['prep']))
    ys = _experts(xs, blk_e, blk_valid, blk_src, blk_next, blk_slot, p['layer'], p['w_exp_gate'], p['w_exp_up'], p['w_exp_down'])
    gw_t = gw.T
    ws_gate, ws_up, ws_down = prep['ws_gate'], prep['ws_up'], prep['ws_down']

    def gathered(t0, n):
        return _gather_rows(ys, dest, t0, n)

    pieces = next(n for n in (4, 2, 1) if b % n == 0)
    nb = b // pieces
    for q in range(pieces):
        x = _combine(gathered(q * nb * s, nb * s), gw_t, 0, h2, x, gt2, q * nb, nb, ws_gate, ws_up, ws_down,
                     g_final, final, tm=min(512, s))
        if after_piece is not None:
            after_piece(x, q * nb, nb)
    if ctx_out:
        ctx_s = _combine(gathered(n_lat, b * sc), gw_t, n_lat, h2_c, ctx_s, cgt2, 0, b, ws_gate, ws_up, ws_down,
                         g_final, False, tm=min(256, sc))
    return x, ctx_s


def kernel(x, c, ctx, c_ctx, w_mod, b_mod, g_norm1, g_norm2, w_in, na_rpb, w_branch_na, conv_w, w_branch_sc,
           gla_gate_w, gla_gate_b, gla_norm_g, w_branch_gla, w_out, w_router, b_router, w_exp_gate, w_exp_up,
           w_exp_down, w_sh_gate, w_sh_up, w_sh_down, g_final):
    stacked = dict(g_norm1=g_norm1, g_norm2=g_norm2, na_rpb=na_rpb, w_branch_na=w_branch_na,
                   conv_w=conv_w, w_branch_sc=w_branch_sc, gla_gate_w=gla_gate_w, gla_gate_b=gla_gate_b,
                   gla_norm_g=gla_norm_g, w_branch_gla=w_branch_gla, w_out=w_out, w_router=w_router,
                   b_router=b_router,
                   w_sh_gate=w_sh_gate, w_sh_up=w_sh_up, w_sh_down=w_sh_down)
    depth = w_in.shape[0]
    rows = x.shape[1] // GRID_W
    layers = []
    for i in range(depth):
        p = {name: arr[i] for name, arr in stacked.items()}
        p.update(layer=i, w_exp_gate=w_exp_gate, w_exp_up=w_exp_up, w_exp_down=w_exp_down)
        mods, mods_ctx = _mod_vectors(c, c_ctx, w_mod, b_mod, i)
        p['prep'] = dict(
            w_in=_prep_w_in(w_in, i), mods=mods, mods_ctx=mods_ctx,
            na_bias=_na_bias_table(p['na_rpb'], rows, min(NA_WIN_R, rows)),
            gate_ws=_gla_gate_weights(p['gla_gate_w'], p['gla_gate_b']),
            w_na=p['w_branch_na'].astype(BF16), w_sc=p['w_branch_sc'].astype(BF16),
            w_gla=p['w_branch_gla'].astype(BF16), w_out=p['w_out'].astype(BF16),
            w_router=jnp.pad(p['w_router'], ((0, 0), (0, LANE - N_EXPERTS))).astype(BF16),
            ws_gate=p['w_sh_gate'].astype(BF16), ws_up=p['w_sh_up'].astype(BF16),
            ws_down=p['w_sh_down'].astype(BF16))
        layers.append(p)

    ctx_s = ctx
    projected = None
    for i, p in enumerate(layers):
        last = i == depth - 1
        p_next = None if last else layers[i + 1]
        after_piece = None
        next_projected = []
        if not last:
            def after_piece(xq, b0, nb, p_next=p_next, acc=next_projected):
                acc.append(_project_latent(xq, p_next, b0, nb, acc[-1] if acc else None))

        x, ctx_s = _layer(x, ctx_s, p, not last, last, g_final, projected, after_piece, p_next)
        projected = next_projected[-1] if next_projected else None
    return x
```
